```python
import jax, jax.numpy as jnp
from jax import lax
import numpy as np

D_MODEL = 2048
BATCH = 8
SEQ = 4096
DEPTH = 2

N_A = DEPTH // 2
N_B = DEPTH - N_A
CHUNK = 128
D_A = D_MODEL
G_A = 16
GD_A = D_A // G_A
N_HEADS = 16
HEAD_DIM = 128
D_B = N_HEADS * HEAD_DIM
BLOCK_Q = 128
D_FF = 5632
CONV_W = 3
EPS = 1e-6

kernel_name = "yoco_gmlp_stickbreaking_hybrid"


def rms_norm(x, g):
    xf = x.astype(jnp.float32)
    ms = jnp.mean(xf * xf, axis=-1, keepdims=True)
    return (xf * lax.rsqrt(ms + EPS) * g.astype(jnp.float32)).astype(x.dtype)


def chunked_gmlp(h, w_in, v_g, w_s, b_s, w_out):
    B, S, _ = h.shape
    uv = jax.nn.gelu(h @ w_in)
    u, v = jnp.split(uv, 2, axis=-1)
    v = rms_norm(v.reshape(B, S, G_A, GD_A), v_g.reshape(G_A, GD_A))
    v = v.reshape(B, S // CHUNK, CHUNK, G_A, GD_A)
    causal = jnp.tril(jnp.ones((CHUNK, CHUNK), dtype=w_s.dtype))
    w = w_s * causal[None]
    mixed = jnp.einsum('gts,bnsgd->bntgd', w, v) + b_s.T[None, None, :, :, None]
    out = u * mixed.reshape(B, S, D_A)
    return out @ w_out


def stick_breaking_attention(q, k, v):
    B, S, H, dh = q.shape
    nb = S // BLOCK_Q
    scale = 1.0 / np.sqrt(dh).astype(np.float32)
    qb = q.reshape(B, nb, BLOCK_Q, H, dh).transpose(1, 0, 2, 3, 4)
    kf = k.astype(jnp.float32)
    vf = v.astype(jnp.float32)
    kpos = jnp.arange(S)

    def one_block(args):
        qi, i = args
        z = jnp.einsum('bqhd,bkhd->bhqk', qi.astype(jnp.float32), kf) * scale
        qpos = i * BLOCK_Q + jnp.arange(BLOCK_Q)
        mask = kpos[None, :] < qpos[:, None]
        log_beta = jax.nn.log_sigmoid(z)
        log_1m = jnp.where(mask, jax.nn.log_sigmoid(-z), 0.0)
        suffix = jnp.flip(jnp.cumsum(jnp.flip(log_1m, -1), axis=-1), -1) - log_1m
        a = jnp.where(mask, jnp.exp(log_beta + suffix), 0.0)
        return jnp.einsum('bhqk,bkhd->bqhd', a, vf)

    out = lax.map(one_block, (qb, jnp.arange(nb)))
    return out.transpose(1, 0, 2, 3, 4).reshape(B, S, H * dh).astype(q.dtype)


def conv_ffn(h, w_up, conv_w, conv_b, w_down):
    S = h.shape[1]
    a = h @ w_up
    ap = jnp.pad(a, ((0, 0), (CONV_W - 1, 0), (0, 0)))
    c = conv_b + sum(ap[:, tap:tap + S] * conv_w[tap] for tap in range(CONV_W))
    gate, val = jnp.split(c, 2, axis=-1)
    return (jax.nn.silu(gate) * val) @ w_down


def _fwd_setup_inputs(seed: int = 0) -> dict:
    key = jax.random.key(seed)
    ks = jax.random.split(key, 24)
    f32 = jnp.float32

    def nrm(k, shape, scale):
        return jax.random.normal(k, shape, f32) * scale

    def gain(k, shape):
        return 1.0 + 0.02 * jax.random.normal(k, shape, f32)

    return {
        "x": jax.random.normal(ks[0], (BATCH, SEQ, D_MODEL), f32),
        "pre_mix_g": gain(ks[1], (DEPTH, D_MODEL)),
        "post_mix_g": gain(ks[2], (DEPTH, D_MODEL)),
        "pre_ffn_g": gain(ks[3], (DEPTH, D_MODEL)),
        "post_ffn_g": gain(ks[4], (DEPTH, D_MODEL)),
        "a_w_in": nrm(ks[5], (N_A, D_MODEL, 2 * D_A), D_MODEL ** -0.5),
        "a_v_norm_g": gain(ks[6], (N_A, D_A)),
        "a_w_spatial": nrm(ks[7], (N_A, G_A, CHUNK, CHUNK), CHUNK ** -0.5),
        "a_b_spatial": gain(ks[8], (N_A, G_A, CHUNK)),
        "a_w_out": nrm(ks[9], (N_A, D_A, D_MODEL), D_A ** -0.5),
        "kv_norm_g": gain(ks[10], (D_MODEL,)),
        "w_k": nrm(ks[11], (D_MODEL, D_B), D_MODEL ** -0.5),
        "w_v": nrm(ks[12], (D_MODEL, D_B), D_MODEL ** -0.5),
        "b_w_q": nrm(ks[13], (N_B, D_MODEL, D_B), D_MODEL ** -0.5),
        "b_w_o": nrm(ks[14], (N_B, D_B, D_MODEL), D_B ** -0.5),
        "ffn_w_up": nrm(ks[15], (DEPTH, D_MODEL, 2 * D_FF), D_MODEL ** -0.5),
        "ffn_conv_w": nrm(ks[16], (DEPTH, CONV_W, 2 * D_FF), CONV_W ** -0.5),
        "ffn_conv_b": nrm(ks[17], (DEPTH, 2 * D_FF), 0.02),
        "ffn_w_down": nrm(ks[18], (DEPTH, D_FF, D_MODEL), D_FF ** -0.5),
    }


def _fwd_reference(x, pre_mix_g, post_mix_g, pre_ffn_g, post_ffn_g,
              a_w_in, a_v_norm_g, a_w_spatial, a_b_spatial, a_w_out,
              kv_norm_g, w_k, w_v, b_w_q, b_w_o,
              ffn_w_up, ffn_conv_w, ffn_conv_b, ffn_w_down):
    B, S, _ = x.shape
    h = x
    k_shared = None
    v_shared = None
    for layer in range(DEPTH):
        hn = rms_norm(h, pre_mix_g[layer])
        if layer < N_A:
            mix = chunked_gmlp(hn, a_w_in[layer], a_v_norm_g[layer],
                               a_w_spatial[layer], a_b_spatial[layer], a_w_out[layer])
        else:
            j = layer - N_A
            if j == 0:
                kvn = rms_norm(h, kv_norm_g)
                k_shared = (kvn @ w_k).reshape(B, S, N_HEADS, HEAD_DIM)
                v_shared = (kvn @ w_v).reshape(B, S, N_HEADS, HEAD_DIM)
            q = (hn @ b_w_q[j]).reshape(B, S, N_HEADS, HEAD_DIM)
            mix = stick_breaking_attention(q, k_shared, v_shared) @ b_w_o[j]
        h = h + rms_norm(mix, post_mix_g[layer])
        f = conv_ffn(rms_norm(h, pre_ffn_g[layer]), ffn_w_up[layer],
                     ffn_conv_w[layer], ffn_conv_b[layer], ffn_w_down[layer])
        h = h + rms_norm(f, post_ffn_g[layer])
    return h


import jax as _jax
import jax.numpy as _jnp

TWIN_FORMAT = 'train_step'
FWD_PARAMS = ['x', 'pre_mix_g', 'post_mix_g', 'pre_ffn_g', 'post_ffn_g', 'a_w_in', 'a_v_norm_g', 'a_w_spatial', 'a_b_spatial', 'a_w_out', 'kv_norm_g', 'w_k', 'w_v', 'b_w_q', 'b_w_o', 'ffn_w_up', 'ffn_conv_w', 'ffn_conv_b', 'ffn_w_down']
TWIN_WEIGHTS = ['pre_mix_g', 'post_mix_g', 'pre_ffn_g', 'post_ffn_g', 'a_w_in', 'a_v_norm_g', 'a_w_spatial', 'a_b_spatial', 'a_w_out', 'kv_norm_g', 'w_k', 'w_v', 'b_w_q', 'b_w_o', 'ffn_w_up', 'ffn_conv_w', 'ffn_conv_b', 'ffn_w_down']
TWIN_DIFF_INPUT = 'x'
TWIN_INPUTS = ['x', 'pre_mix_g', 'post_mix_g', 'pre_ffn_g', 'post_ffn_g', 'a_w_in', 'a_v_norm_g', 'a_w_spatial', 'a_b_spatial', 'a_w_out', 'kv_norm_g', 'w_k', 'w_v', 'b_w_q', 'b_w_o', 'ffn_w_up', 'ffn_conv_w', 'ffn_conv_b', 'ffn_w_down', 'loss_target', 'm_pre_mix_g', 'm_post_mix_g', 'm_pre_ffn_g', 'm_post_ffn_g', 'm_a_w_in', 'm_a_v_norm_g', 'm_a_w_spatial', 'm_a_b_spatial', 'm_a_w_out', 'm_kv_norm_g', 'm_w_k', 'm_w_v', 'm_b_w_q', 'm_b_w_o', 'm_ffn_w_up', 'm_ffn_conv_w', 'm_ffn_conv_b', 'm_ffn_w_down', 'v_pre_mix_g', 'v_post_mix_g', 'v_pre_ffn_g', 'v_post_ffn_g', 'v_a_w_in', 'v_a_v_norm_g', 'v_a_w_spatial', 'v_a_b_spatial', 'v_a_w_out', 'v_kv_norm_g', 'v_w_k', 'v_w_v', 'v_b_w_q', 'v_b_w_o', 'v_ffn_w_up', 'v_ffn_conv_w', 'v_ffn_conv_b', 'v_ffn_w_down']
TWIN_OUTPUTS = ['loss', 'grad_x', 'grad_pre_mix_g', 'grad_post_mix_g', 'grad_pre_ffn_g', 'grad_post_ffn_g', 'grad_a_w_in', 'grad_a_v_norm_g', 'grad_a_w_spatial', 'grad_a_b_spatial', 'grad_a_w_out', 'grad_kv_norm_g', 'grad_w_k', 'grad_w_v', 'grad_b_w_q', 'grad_b_w_o', 'grad_ffn_w_up', 'grad_ffn_conv_w', 'grad_ffn_conv_b', 'grad_ffn_w_down', 'delta_pre_mix_g', 'delta_post_mix_g', 'delta_pre_ffn_g', 'delta_post_ffn_g', 'delta_a_w_in', 'delta_a_v_norm_g', 'delta_a_w_spatial', 'delta_a_b_spatial', 'delta_a_w_out', 'delta_kv_norm_g', 'delta_w_k', 'delta_w_v', 'delta_b_w_q', 'delta_b_w_o', 'delta_ffn_w_up', 'delta_ffn_conv_w', 'delta_ffn_conv_b', 'delta_ffn_w_down', 'new_m_pre_mix_g', 'new_m_post_mix_g', 'new_m_pre_ffn_g', 'new_m_post_ffn_g', 'new_m_a_w_in', 'new_m_a_v_norm_g', 'new_m_a_w_spatial', 'new_m_a_b_spatial', 'new_m_a_w_out', 'new_m_kv_norm_g', 'new_m_w_k', 'new_m_w_v', 'new_m_b_w_q', 'new_m_b_w_o', 'new_m_ffn_w_up', 'new_m_ffn_conv_w', 'new_m_ffn_conv_b', 'new_m_ffn_w_down', 'new_v_pre_mix_g', 'new_v_post_mix_g', 'new_v_pre_ffn_g', 'new_v_post_ffn_g', 'new_v_a_w_in', 'new_v_a_v_norm_g', 'new_v_a_w_spatial', 'new_v_a_b_spatial', 'new_v_a_w_out', 'new_v_kv_norm_g', 'new_v_w_k', 'new_v_w_v', 'new_v_b_w_q', 'new_v_b_w_o', 'new_v_ffn_w_up', 'new_v_ffn_conv_w', 'new_v_ffn_conv_b', 'new_v_ffn_w_down']
TWIN_LEAF_KINDS = {'loss': 'loss', 'grad_x': 'grad_x', 'grad_pre_mix_g': 'grad_w', 'grad_post_mix_g': 'grad_w', 'grad_pre_ffn_g': 'grad_w', 'grad_post_ffn_g': 'grad_w', 'grad_a_w_in': 'grad_w', 'grad_a_v_norm_g': 'grad_w', 'grad_a_w_spatial': 'grad_w', 'grad_a_b_spatial': 'grad_w', 'grad_a_w_out': 'grad_w', 'grad_kv_norm_g': 'grad_w', 'grad_w_k': 'grad_w', 'grad_w_v': 'grad_w', 'grad_b_w_q': 'grad_w', 'grad_b_w_o': 'grad_w', 'grad_ffn_w_up': 'grad_w', 'grad_ffn_conv_w': 'grad_w', 'grad_ffn_conv_b': 'grad_w', 'grad_ffn_w_down': 'grad_w', 'delta_pre_mix_g': 'delta_w', 'delta_post_mix_g': 'delta_w', 'delta_pre_ffn_g': 'delta_w', 'delta_post_ffn_g': 'delta_w', 'delta_a_w_in': 'delta_w', 'delta_a_v_norm_g': 'delta_w', 'delta_a_w_spatial': 'delta_w', 'delta_a_b_spatial': 'delta_w', 'delta_a_w_out': 'delta_w', 'delta_kv_norm_g': 'delta_w', 'delta_w_k': 'delta_w', 'delta_w_v': 'delta_w', 'delta_b_w_q': 'delta_w', 'delta_b_w_o': 'delta_w', 'delta_ffn_w_up': 'delta_w', 'delta_ffn_conv_w': 'delta_w', 'delta_ffn_conv_b': 'delta_w', 'delta_ffn_w_down': 'delta_w', 'new_m_pre_mix_g': 'new_m', 'new_m_post_mix_g': 'new_m', 'new_m_pre_ffn_g': 'new_m', 'new_m_post_ffn_g': 'new_m', 'new_m_a_w_in': 'new_m', 'new_m_a_v_norm_g': 'new_m', 'new_m_a_w_spatial': 'new_m', 'new_m_a_b_spatial': 'new_m', 'new_m_a_w_out': 'new_m', 'new_m_kv_norm_g': 'new_m', 'new_m_w_k': 'new_m', 'new_m_w_v': 'new_m', 'new_m_b_w_q': 'new_m', 'new_m_b_w_o': 'new_m', 'new_m_ffn_w_up': 'new_m', 'new_m_ffn_conv_w': 'new_m', 'new_m_ffn_conv_b': 'new_m', 'new_m_ffn_w_down': 'new_m', 'new_v_pre_mix_g': 'new_v', 'new_v_post_mix_g': 'new_v', 'new_v_pre_ffn_g': 'new_v', 'new_v_post_ffn_g': 'new_v', 'new_v_a_w_in': 'new_v', 'new_v_a_v_norm_g': 'new_v', 'new_v_a_w_spatial': 'new_v', 'new_v_a_b_spatial': 'new_v', 'new_v_a_w_out': 'new_v', 'new_v_kv_norm_g': 'new_v', 'new_v_w_k': 'new_v', 'new_v_w_v': 'new_v', 'new_v_b_w_q': 'new_v', 'new_v_b_w_o': 'new_v', 'new_v_ffn_w_up': 'new_v', 'new_v_ffn_conv_w': 'new_v', 'new_v_ffn_conv_b': 'new_v', 'new_v_ffn_w_down': 'new_v'}


def _forward(args):
    return _fwd_reference(*[args[k] for k in FWD_PARAMS])


def _output_shape():
    def fwd():
        inp = _fwd_setup_inputs(0)
        return _fwd_reference(*[inp[k] for k in FWD_PARAMS])
    out = _jax.eval_shape(fwd)
    return out.shape, out.dtype

N_MICROBATCH = 1
ADAM_LR = 0.001
ADAM_B1 = 0.9
ADAM_B2 = 0.999
ADAM_EPS = 1e-08
ADAM_WD = 0.01
ADAM_STEP = 10
PER_EXAMPLE_BATCH_AXIS = {'x': 0, 'loss_target': 0}
SHARED_INPUTS = []
_WEIGHT_DTYPES = {'pre_mix_g': _jnp.float32, 'post_mix_g': _jnp.float32, 'pre_ffn_g': _jnp.float32, 'post_ffn_g': _jnp.float32, 'a_w_in': _jnp.float32, 'a_v_norm_g': _jnp.float32, 'a_w_spatial': _jnp.float32, 'a_b_spatial': _jnp.float32, 'a_w_out': _jnp.float32, 'kv_norm_g': _jnp.float32, 'w_k': _jnp.float32, 'w_v': _jnp.float32, 'b_w_q': _jnp.float32, 'b_w_o': _jnp.float32, 'ffn_w_up': _jnp.float32, 'ffn_conv_w': _jnp.float32, 'ffn_conv_b': _jnp.float32, 'ffn_w_down': _jnp.float32}
MOMENT_SCALE = {'pre_mix_g': 3.849094e-01, 'post_mix_g': 1.655153e+01, 'pre_ffn_g': 1.045882e+00, 'post_ffn_g': 1.603761e+01, 'a_w_in': 3.791638e-01, 'a_v_norm_g': 2.457648e-01, 'a_w_spatial': 2.398877e-01, 'a_b_spatial': 3.865832e-01, 'a_w_out': 3.146457e+00, 'kv_norm_g': 2.039469e+00, 'w_k': 1.083299e-01, 'w_v': 2.029028e+00, 'b_w_q': 1.080447e-01, 'b_w_o': 2.044484e+00, 'ffn_w_up': 4.378972e-01, 'ffn_conv_w': 5.383217e-01, 'ffn_conv_b': 1.869437e+00, 'ffn_w_down': 9.268776e-01}


def _to_microbatches(a, axis):
    t = _jnp.moveaxis(a, axis, 0)
    t = t.reshape((N_MICROBATCH, t.shape[0] // N_MICROBATCH) + t.shape[1:])
    return _jnp.moveaxis(t, 1, axis + 1)


def setup_inputs(seed: int = 0) -> dict:
    inp = _fwd_setup_inputs(seed)
    key = _jax.random.fold_in(_jax.random.key(seed), 7919)
    shape, _ = _output_shape()
    out = dict(inp)
    out["loss_target"] = _jax.random.normal(_jax.random.fold_in(key, 0), shape, _jnp.float32)
    for i, name in enumerate(TWIN_WEIGHTS):
        w = inp[name].astype(_jnp.float32)
        if MOMENT_SCALE is None:
            s = _jnp.sqrt(_jnp.mean(_jnp.square(w)) + 1e-30)
        else:
            s = MOMENT_SCALE[name]
        km, kv = _jax.random.split(_jax.random.fold_in(key, i + 1))
        out[name] = w
        out["m_" + name] = s * _jax.random.normal(km, w.shape, _jnp.float32)
        out["v_" + name] = (s * s) * _jax.random.uniform(kv, w.shape, _jnp.float32, 0.5, 1.5)
    if N_MICROBATCH > 1:
        for name, axis in PER_EXAMPLE_BATCH_AXIS.items():
            out[name] = _to_microbatches(out[name], axis)
    return {'x': out['x'], 'pre_mix_g': out['pre_mix_g'], 'post_mix_g': out['post_mix_g'], 'pre_ffn_g': out['pre_ffn_g'], 'post_ffn_g': out['post_ffn_g'], 'a_w_in': out['a_w_in'], 'a_v_norm_g': out['a_v_norm_g'], 'a_w_spatial': out['a_w_spatial'], 'a_b_spatial': out['a_b_spatial'], 'a_w_out': out['a_w_out'], 'kv_norm_g': out['kv_norm_g'], 'w_k': out['w_k'], 'w_v': out['w_v'], 'b_w_q': out['b_w_q'], 'b_w_o': out['b_w_o'], 'ffn_w_up': out['ffn_w_up'], 'ffn_conv_w': out['ffn_conv_w'], 'ffn_conv_b': out['ffn_conv_b'], 'ffn_w_down': out['ffn_w_down'], 'loss_target': out['loss_target'], 'm_pre_mix_g': out['m_pre_mix_g'], 'm_post_mix_g': out['m_post_mix_g'], 'm_pre_ffn_g': out['m_pre_ffn_g'], 'm_post_ffn_g': out['m_post_ffn_g'], 'm_a_w_in': out['m_a_w_in'], 'm_a_v_norm_g': out['m_a_v_norm_g'], 'm_a_w_spatial': out['m_a_w_spatial'], 'm_a_b_spatial': out['m_a_b_spatial'], 'm_a_w_out': out['m_a_w_out'], 'm_kv_norm_g': out['m_kv_norm_g'], 'm_w_k': out['m_w_k'], 'm_w_v': out['m_w_v'], 'm_b_w_q': out['m_b_w_q'], 'm_b_w_o': out['m_b_w_o'], 'm_ffn_w_up': out['m_ffn_w_up'], 'm_ffn_conv_w': out['m_ffn_conv_w'], 'm_ffn_conv_b': out['m_ffn_conv_b'], 'm_ffn_w_down': out['m_ffn_w_down'], 'v_pre_mix_g': out['v_pre_mix_g'], 'v_post_mix_g': out['v_post_mix_g'], 'v_pre_ffn_g': out['v_pre_ffn_g'], 'v_post_ffn_g': out['v_post_ffn_g'], 'v_a_w_in': out['v_a_w_in'], 'v_a_v_norm_g': out['v_a_v_norm_g'], 'v_a_w_spatial': out['v_a_w_spatial'], 'v_a_b_spatial': out['v_a_b_spatial'], 'v_a_w_out': out['v_a_w_out'], 'v_kv_norm_g': out['v_kv_norm_g'], 'v_w_k': out['v_w_k'], 'v_w_v': out['v_w_v'], 'v_b_w_q': out['v_b_w_q'], 'v_b_w_o': out['v_b_w_o'], 'v_ffn_w_up': out['v_ffn_w_up'], 'v_ffn_conv_w': out['v_ffn_conv_w'], 'v_ffn_conv_b': out['v_ffn_conv_b'], 'v_ffn_w_down': out['v_ffn_w_down']}


def _loss(weights, diff, rest, loss_target):
    with _jax.named_scope("forward"):
        args = {**rest, TWIN_DIFF_INPUT: diff, **{k: w.astype(_WEIGHT_DTYPES[k]) for k, w in weights.items()}}
        y = _forward(args)
    with _jax.named_scope("loss_head"):
        err = _jnp.square(y.astype(_jnp.float32) - loss_target)
        return 0.5 * _jnp.sum(_jnp.mean(err, axis=-1)) if err.ndim else 0.5 * err


def _adamw(w, g, m, v):
    m = ADAM_B1 * m + (1.0 - ADAM_B1) * g
    v = ADAM_B2 * v + (1.0 - ADAM_B2) * _jnp.square(g)
    m_hat = m / (1.0 - ADAM_B1 ** ADAM_STEP)
    v_hat = v / (1.0 - ADAM_B2 ** ADAM_STEP)
    delta = -ADAM_LR * (m_hat / (_jnp.sqrt(v_hat) + ADAM_EPS) + ADAM_WD * w)
    return delta, m, v


def reference(x, pre_mix_g, post_mix_g, pre_ffn_g, post_ffn_g, a_w_in, a_v_norm_g, a_w_spatial, a_b_spatial, a_w_out, kv_norm_g, w_k, w_v, b_w_q, b_w_o, ffn_w_up, ffn_conv_w, ffn_conv_b, ffn_w_down, loss_target, m_pre_mix_g, m_post_mix_g, m_pre_ffn_g, m_post_ffn_g, m_a_w_in, m_a_v_norm_g, m_a_w_spatial, m_a_b_spatial, m_a_w_out, m_kv_norm_g, m_w_k, m_w_v, m_b_w_q, m_b_w_o, m_ffn_w_up, m_ffn_conv_w, m_ffn_conv_b, m_ffn_w_down, v_pre_mix_g, v_post_mix_g, v_pre_ffn_g, v_post_ffn_g, v_a_w_in, v_a_v_norm_g, v_a_w_spatial, v_a_b_spatial, v_a_w_out, v_kv_norm_g, v_w_k, v_w_v, v_b_w_q, v_b_w_o, v_ffn_w_up, v_ffn_conv_w, v_ffn_conv_b, v_ffn_w_down):
    given = dict(x=x, pre_mix_g=pre_mix_g, post_mix_g=post_mix_g, pre_ffn_g=pre_ffn_g, post_ffn_g=post_ffn_g, a_w_in=a_w_in, a_v_norm_g=a_v_norm_g, a_w_spatial=a_w_spatial, a_b_spatial=a_b_spatial, a_w_out=a_w_out, kv_norm_g=kv_norm_g, w_k=w_k, w_v=w_v, b_w_q=b_w_q, b_w_o=b_w_o, ffn_w_up=ffn_w_up, ffn_conv_w=ffn_conv_w, ffn_conv_b=ffn_conv_b, ffn_w_down=ffn_w_down, loss_target=loss_target, m_pre_mix_g=m_pre_mix_g, m_post_mix_g=m_post_mix_g, m_pre_ffn_g=m_pre_ffn_g, m_post_ffn_g=m_post_ffn_g, m_a_w_in=m_a_w_in, m_a_v_norm_g=m_a_v_norm_g, m_a_w_spatial=m_a_w_spatial, m_a_b_spatial=m_a_b_spatial, m_a_w_out=m_a_w_out, m_kv_norm_g=m_kv_norm_g, m_w_k=m_w_k, m_w_v=m_w_v, m_b_w_q=m_b_w_q, m_b_w_o=m_b_w_o, m_ffn_w_up=m_ffn_w_up, m_ffn_conv_w=m_ffn_conv_w, m_ffn_conv_b=m_ffn_conv_b, m_ffn_w_down=m_ffn_w_down, v_pre_mix_g=v_pre_mix_g, v_post_mix_g=v_post_mix_g, v_pre_ffn_g=v_pre_ffn_g, v_post_ffn_g=v_post_ffn_g, v_a_w_in=v_a_w_in, v_a_v_norm_g=v_a_v_norm_g, v_a_w_spatial=v_a_w_spatial, v_a_b_spatial=v_a_b_spatial, v_a_w_out=v_a_w_out, v_kv_norm_g=v_kv_norm_g, v_w_k=v_w_k, v_w_v=v_w_v, v_b_w_q=v_b_w_q, v_b_w_o=v_b_w_o, v_ffn_w_up=v_ffn_w_up, v_ffn_conv_w=v_ffn_conv_w, v_ffn_conv_b=v_ffn_conv_b, v_ffn_w_down=v_ffn_w_down)
    weights = {n: given[n] for n in TWIN_WEIGHTS}
    shared = {n: given[n] for n in SHARED_INPUTS}
    per_example = {n: given[n] for n in ['x']}
    grad_fn = _jax.value_and_grad(_loss, argnums=(0, 1))

    def one_microbatch(ex, loss_target):
        ex = dict(ex)
        diff = ex.pop(TWIN_DIFF_INPUT)
        return grad_fn(weights, diff, {**shared, **ex}, loss_target)

    if N_MICROBATCH == 1:
        loss, (grad_w, grad_x) = one_microbatch(per_example, given["loss_target"])
    else:
        def body(carry, xs):
            loss_sum, grad_sum = carry
            l_k, (gw_k, gx_k) = one_microbatch(xs[0], xs[1])
            with _jax.named_scope("update"):
                return (loss_sum + l_k, _jax.tree.map(_jnp.add, grad_sum, gw_k)), gx_k

        init = (_jnp.zeros((), _jnp.float32), _jax.tree.map(_jnp.zeros_like, weights))
        (loss, grad_w), grad_x = _jax.lax.scan(body, init, (per_example, given["loss_target"]))
    with _jax.named_scope("update"):
        delta_w, new_m, new_v = {}, {}, {}
        for n in TWIN_WEIGHTS:
            delta_w[n], new_m[n], new_v[n] = _adamw(weights[n], grad_w[n], given["m_" + n], given["v_" + n])
    return (loss, grad_x, *[grad_w[n] for n in TWIN_WEIGHTS], *[delta_w[n] for n in TWIN_WEIGHTS],
            *[new_m[n] for n in TWIN_WEIGHTS], *[new_v[n] for n in TWIN_WEIGHTS])
```

```python
import functools
import math

import jax
import jax.numpy as jnp
from jax import lax
from jax.experimental import pallas as pl
from jax.experimental.pallas import tpu as pltpu

F32 = jnp.float32
BF = jnp.bfloat16
MESH = pl.DeviceIdType.MESH

N_DEV = 8
NORM_EPS = 1e-6
GROUP = 128
CONV_TAPS = 3
ADAM_LR, ADAM_B1, ADAM_B2, ADAM_EPS, ADAM_WD, ADAM_STEP = 0.001, 0.9, 0.999, 1e-08, 0.01, 10
EXP_FLOOR = -104.0

V7X_LANES = 128
V7X_VMEM_BYTES = 64 * 1024 * 1024
_MIB = 1024 * 1024

_NN = (((1,), (0,)), ((), ()))
_NT = (((1,), (1,)), ((), ()))
_TN = (((0,), (0,)), ((), ()))


def _tile(n, pref):
    if n <= pref:
        return n
    t = (pref // V7X_LANES) * V7X_LANES
    while t > V7X_LANES and n % t:
        t -= V7X_LANES
    assert n % t == 0, (n, pref)
    return t


def _nbytes(shape, dtype):
    return math.prod(shape) * jnp.dtype(dtype).itemsize


def _params(sem=None, vmem=None):
    kw = {}
    if sem is not None:
        kw["dimension_semantics"] = sem
    if vmem is not None:
        kw["vmem_limit_bytes"] = int(min(max(vmem, 16 * _MIB), V7X_VMEM_BYTES - 8 * _MIB))
    return pltpu.CompilerParams(**kw)


def _mm(name, a, b, *, dims, grid, a_blk, a_map, b_blk, b_map, o_blk, o_map, out_shape, out_dtype,
        add=None, add_blk=None, add_map=None):
    nk = grid[2]
    acc_shape = tuple(d for d in o_blk if d is not None)

    def body(*refs):
        if add is None:
            a_ref, b_ref, o_ref = refs[:3]
            c_ref, scr = None, refs[3:]
        else:
            a_ref, b_ref, c_ref, o_ref = refs[:4]
            scr = refs[4:]
        part = lax.dot_general(a_ref[...], b_ref[...], dims, preferred_element_type=F32)

        def finish(x):
            if c_ref is not None:
                x = x + c_ref[...].astype(F32)
            o_ref[...] = x.astype(o_ref.dtype)

        if nk == 1:
            finish(part)
        else:
            acc = scr[0]
            k = pl.program_id(2)

            @pl.when(k == 0)
            def _():
                acc[...] = part

            @pl.when(k > 0)
            def _():
                acc[...] += part

            @pl.when(k == nk - 1)
            def _():
                finish(acc[...])

    in_specs = [pl.BlockSpec(a_blk, a_map), pl.BlockSpec(b_blk, b_map)]
    operands = [a, b]
    vmem = 2 * (_nbytes(acc_shape, out_dtype) + _nbytes([d for d in a_blk if d], a.dtype)
                + _nbytes([d for d in b_blk if d], b.dtype)) + 2 * _nbytes(acc_shape, F32)
    if add is not None:
        in_specs.append(pl.BlockSpec(add_blk, add_map))
        operands.append(add)
        vmem += 2 * _nbytes(acc_shape, add.dtype)
    scratch = [pltpu.VMEM(acc_shape, F32)] if nk > 1 else []
    return pl.pallas_call(
        body, name=name, grid=grid, in_specs=in_specs, out_specs=pl.BlockSpec(o_blk, o_map),
        out_shape=jax.ShapeDtypeStruct(out_shape, out_dtype), scratch_shapes=scratch,
        compiler_params=_params(("parallel", "parallel", "arbitrary"), vmem + 8 * _MIB),
    )(*operands)


def _mm_nn(name, x, w, out_dtype):
    t, kd = x.shape
    n = w.shape[1]
    tm, tn, tk = _tile(t, 1024), _tile(n, 1024), _tile(kd, 1024 if kd > 2048 else 2048)
    return _mm(name, x, w, dims=_NN, grid=(t // tm, n // tn, kd // tk),
               a_blk=(tm, tk), a_map=lambda i, j, k: (i, k), b_blk=(tk, tn), b_map=lambda i, j, k: (k, j),
               o_blk=(tm, tn), o_map=lambda i, j, k: (i, j), out_shape=(t, n), out_dtype=out_dtype)


def _mm_nn_blk(name, x, g, out_dtype, halves=False):
    t, kd = x.shape
    cw = g.shape[2]
    tm = _tile(t, 1024)
    if halves:
        o_blk, o_map, out_shape = (None, tm, cw), (lambda i, j, k: (j // 4, i, j % 4)), (2, t, 4 * cw)
    else:
        o_blk, o_map, out_shape = (tm, cw), (lambda i, j, k: (i, j)), (t, N_DEV * cw)
    return _mm(name, x, g, dims=_NN, grid=(t // tm, N_DEV, 1),
               a_blk=(tm, kd), a_map=lambda i, j, k: (i, 0), b_blk=(None, kd, cw), b_map=lambda i, j, k: (j, 0, 0),
               o_blk=o_blk, o_map=o_map, out_shape=out_shape, out_dtype=out_dtype)


def _mm_nt(name, dy, w, out_dtype, add=None):
    t, n = dy.shape
    kd = w.shape[0]
    tm, tn = _tile(t, 1024), _tile(kd, 512)
    kw = {}
    if add is not None:
        kw = dict(add=add, add_blk=(tm, tn), add_map=lambda i, j, k: (i, j))
    return _mm(name, dy, w, dims=_NT, grid=(t // tm, kd // tn, 1),
               a_blk=(tm, n), a_map=lambda i, j, k: (i, 0), b_blk=(tn, n), b_map=lambda i, j, k: (j, 0),
               o_blk=(tm, tn), o_map=lambda i, j, k: (i, j), out_shape=(t, kd), out_dtype=out_dtype, **kw)


def _mm_nt_blk(name, dy, g, out_dtype, halves=False):
    kd, cw = g.shape[1], g.shape[2]
    t = dy.shape[1] if halves else dy.shape[0]
    tm = _tile(t, 512)
    if halves:
        a_blk, a_map = (None, tm, cw), (lambda i, j, k: (k // 4, i, k % 4))
    else:
        a_blk, a_map = (tm, cw), (lambda i, j, k: (i, k))
    return _mm(name, dy, g, dims=_NT, grid=(t // tm, 1, N_DEV),
               a_blk=a_blk, a_map=a_map, b_blk=(None, kd, cw), b_map=lambda i, j, k: (k, 0, 0),
               o_blk=(tm, kd), o_map=lambda i, j, k: (i, 0), out_shape=(t, kd), out_dtype=out_dtype)


def _mm_tn(name, x, dy):
    t, kd = x.shape
    n = dy.shape[1]
    tmx, tk = _tile(kd, 1024), _tile(t, 512)
    return _mm(name, x, dy, dims=_TN, grid=(kd // tmx, 1, t // tk),
               a_blk=(tk, tmx), a_map=lambda i, j, k: (k, i), b_blk=(tk, n), b_map=lambda i, j, k: (k, 0),
               o_blk=(tmx, n), o_map=lambda i, j, k: (i, 0), out_shape=(kd, n), out_dtype=BF)


def _mm_tn_blk(name, x, dy, halves=False):
    t, kd = x.shape
    cw = dy.shape[2] // 4 if halves else dy.shape[1] // N_DEV
    tmx, tk = _tile(kd, 2048 if cw <= 512 else 1024), _tile(t, 1024 if cw <= 512 else 512)
    if halves:
        b_blk, b_map = (None, tk, cw), (lambda i, j, k: (j // 4, k, j % 4))
    else:
        b_blk, b_map = (tk, cw), (lambda i, j, k: (k, j))
    return _mm(name, x, dy, dims=_TN, grid=(kd // tmx, N_DEV, t // tk),
               a_blk=(tk, tmx), a_map=lambda i, j, k: (k, i), b_blk=b_blk, b_map=b_map,
               o_blk=(None, tmx, cw), o_map=lambda i, j, k: (j, i, 0), out_shape=(N_DEV, kd, cw), out_dtype=BF)


def _rms(x, g):
    r = lax.rsqrt(jnp.mean(x * x, axis=-1, keepdims=True) + NORM_EPS)
    return x * r * g


def _rms_bwd_math(x, g, dy):
    d = x.shape[-1]
    r = lax.rsqrt(jnp.mean(x * x, axis=-1, keepdims=True) + NORM_EPS)
    xh = x * r
    u = dy * g
    dx = r * u - xh * (jnp.sum(xh * u, axis=-1, keepdims=True) * (r / d))
    return dx, jnp.sum(dy * xh, axis=0, keepdims=True)


def _row_specs(tr, d, n):
    return [pl.BlockSpec((tr, d), lambda i: (i, 0)) for _ in range(n)]


def _vec_specs(d, n):
    return [pl.BlockSpec((1, d), lambda i: (0, 0)) for _ in range(n)]


def _norms_fwd(name, h, gains):
    t, d = h.shape
    tr, ng = _tile(t, 256), len(gains)

    def body(h_ref, *refs):
        x = h_ref[...]
        for g_ref, o_ref in zip(refs[:ng], refs[ng:]):
            o_ref[...] = _rms(x, g_ref[...]).astype(BF)

    return pl.pallas_call(
        body, name=name, grid=(t // tr,), in_specs=_row_specs(tr, d, 1) + _vec_specs(d, ng),
        out_specs=_row_specs(tr, d, ng), out_shape=[jax.ShapeDtypeStruct((t, d), BF)] * ng,
        compiler_params=_params(("parallel",)),
    )(h, *[g.reshape(1, d) for g in gains])


def _resid_norms(name, h, m, g_post, gains):
    t, d = h.shape
    tr, ng = _tile(t, 256), len(gains)

    def body(h_ref, m_ref, gp_ref, *refs):
        hn = h_ref[...] + _rms(m_ref[...], gp_ref[...])
        refs[ng][...] = hn
        for g_ref, o_ref in zip(refs[:ng], refs[ng + 1:]):
            o_ref[...] = _rms(hn, g_ref[...]).astype(BF)

    return pl.pallas_call(
        body, name=name, grid=(t // tr,), in_specs=_row_specs(tr, d, 2) + _vec_specs(d, 1 + ng),
        out_specs=_row_specs(tr, d, 1 + ng),
        out_shape=[jax.ShapeDtypeStruct((t, d), F32)] + [jax.ShapeDtypeStruct((t, d), BF)] * ng,
        compiler_params=_params(("parallel",)),
    )(h, m, g_post.reshape(1, d), *[g.reshape(1, d) for g in gains])


def _resid_loss(name, h, m, g_post, target):
    t, d = h.shape
    tr = _tile(t, 256)

    def body(h_ref, m_ref, gp_ref, t_ref, dy_ref, loss_ref):
        diff = h_ref[...] + _rms(m_ref[...], gp_ref[...]) - t_ref[...]
        dy_ref[...] = diff * (1.0 / d)

        @pl.when(pl.program_id(0) == 0)
        def _():
            loss_ref[...] = jnp.zeros_like(loss_ref)

        per_row = jnp.sum(diff * diff, axis=-1, keepdims=True) * (1.0 / d)
        loss_ref[...] += 0.5 * jnp.sum(per_row, axis=0, keepdims=True)

    return pl.pallas_call(
        body, name=name, grid=(t // tr,),
        in_specs=_row_specs(tr, d, 2) + _vec_specs(d, 1) + _row_specs(tr, d, 1),
        out_specs=[pl.BlockSpec((tr, d), lambda i: (i, 0)), pl.BlockSpec((1, 1), lambda i: (0, 0))],
        out_shape=[jax.ShapeDtypeStruct((t, d), F32), jax.ShapeDtypeStruct((1, 1), F32)],
        compiler_params=_params(("arbitrary",)),
    )(h, m, g_post.reshape(1, d), target)


def _norm_bwd(name, x, g, dy, res=None, out_dtype=F32):
    t, d = x.shape
    tr = _tile(t, 256)
    has_res = res is not None

    def body(x_ref, dy_ref, g_ref, *refs):
        dx_ref, dg_ref = refs[-2:]
        dx, dg = _rms_bwd_math(x_ref[...].astype(F32), g_ref[...], dy_ref[...].astype(F32))
        if has_res:
            dx = dx + refs[0][...]
        dx_ref[...] = dx.astype(dx_ref.dtype)

        @pl.when(pl.program_id(0) == 0)
        def _():
            dg_ref[...] = jnp.zeros_like(dg_ref)

        dg_ref[...] += dg

    ops = [x, dy, g.reshape(1, d)] + ([res] if has_res else [])
    return pl.pallas_call(
        body, name=name, grid=(t // tr,),
        in_specs=_row_specs(tr, d, 2) + _vec_specs(d, 1) + _row_specs(tr, d, int(has_res)),
        out_specs=[pl.BlockSpec((tr, d), lambda i: (i, 0)), pl.BlockSpec((1, d), lambda i: (0, 0))],
        out_shape=[jax.ShapeDtypeStruct((t, d), out_dtype), jax.ShapeDtypeStruct((1, d), F32)],
        compiler_params=_params(("arbitrary",)),
    )(*ops)


_GELU_C = math.sqrt(2.0 / math.pi)
_GELU_A = 0.044715


def _gelu(x):
    return 0.5 * x * (1.0 + jnp.tanh(_GELU_C * (x + _GELU_A * x * x * x)))


def _gelu_and_grad(x):
    th = jnp.tanh(_GELU_C * (x + _GELU_A * x * x * x))
    grad = 0.5 * (1.0 + th) + 0.5 * x * (1.0 - th * th) * (_GELU_C * (1.0 + 3.0 * _GELU_A * x * x))
    return 0.5 * x * (1.0 + th), grad


def _causal(n):
    return lax.broadcasted_iota(jnp.int32, (n, n), 1) <= lax.broadcasted_iota(jnp.int32, (n, n), 0)


def _sgu_fwd(name, p, v_gain, w_s, b_st):
    t, da2 = p.shape
    da = da2 // 2
    ng = da // GROUP

    def body(p_ref, vg_ref, ws_ref, bst_ref, o_ref):
        keep = _causal(GROUP)
        for g in range(ng):
            lo = g * GROUP
            u = _gelu(p_ref[:, lo:lo + GROUP])
            vn = _rms(_gelu(p_ref[:, da + lo:da + lo + GROUP]), vg_ref[:, lo:lo + GROUP])
            w = jnp.where(keep, ws_ref[g], 0.0).astype(BF)
            mixed = jnp.dot(w, vn.astype(BF), preferred_element_type=F32) + bst_ref[:, g:g + 1]
            o_ref[:, lo:lo + GROUP] = (u * mixed).astype(BF)

    return pl.pallas_call(
        body, name=name, grid=(t // GROUP,),
        in_specs=[pl.BlockSpec((GROUP, da2), lambda i: (i, 0)), pl.BlockSpec((1, da), lambda i: (0, 0)),
                  pl.BlockSpec((ng, GROUP, GROUP), lambda i: (0, 0, 0)), pl.BlockSpec((GROUP, ng), lambda i: (0, 0))],
        out_specs=pl.BlockSpec((GROUP, da), lambda i: (i, 0)),
        out_shape=jax.ShapeDtypeStruct((t, da), BF),
        compiler_params=_params(("parallel",)),
    )(p, v_gain, w_s, b_st)


def _sgu_bwd(name, p, v_gain, w_s, w_st, b_st, dout):
    t, da2 = p.shape
    da = da2 // 2
    ng = da // GROUP

    def body(p_ref, vg_ref, ws_ref, wst_ref, bst_ref, do_ref, dp_ref, dvg_ref, dws_ref, dbst_ref):
        @pl.when(pl.program_id(0) == 0)
        def _():
            dvg_ref[...] = jnp.zeros_like(dvg_ref)
            dws_ref[...] = jnp.zeros_like(dws_ref)
            dbst_ref[...] = jnp.zeros_like(dbst_ref)

        keep = _causal(GROUP)
        keep_t = lax.broadcasted_iota(jnp.int32, (GROUP, GROUP), 0) <= lax.broadcasted_iota(jnp.int32, (GROUP, GROUP), 1)
        for g in range(ng):
            lo = g * GROUP
            u, du = _gelu_and_grad(p_ref[:, lo:lo + GROUP])
            v, dv_act = _gelu_and_grad(p_ref[:, da + lo:da + lo + GROUP])
            gain = vg_ref[:, lo:lo + GROUP]
            r = lax.rsqrt(jnp.mean(v * v, axis=-1, keepdims=True) + NORM_EPS)
            vh = v * r
            vnb = (vh * gain).astype(BF)
            w = jnp.where(keep, ws_ref[g], 0.0).astype(BF)
            wt = jnp.where(keep_t, wst_ref[g], 0.0).astype(BF)
            mixed = jnp.dot(w, vnb, preferred_element_type=F32) + bst_ref[:, g:g + 1]
            dout_g = do_ref[:, lo:lo + GROUP].astype(F32)
            dmixed = dout_g * u
            dmb = dmixed.astype(BF)
            dbst_ref[:, g:g + 1] += jnp.sum(dmixed, axis=1, keepdims=True)
            dws_ref[g] += jnp.where(keep, lax.dot_general(dmb, vnb, _NT, preferred_element_type=F32), 0.0)
            dvn = jnp.dot(wt, dmb, preferred_element_type=F32)
            dvg_ref[:, lo:lo + GROUP] += jnp.sum(dvn * vh, axis=0, keepdims=True)
            dvh = dvn * gain
            dv = r * dvh - vh * (jnp.sum(vh * dvh, axis=-1, keepdims=True) * (r / GROUP))
            dp_ref[:, lo:lo + GROUP] = (dout_g * mixed * du).astype(BF)
            dp_ref[:, da + lo:da + lo + GROUP] = (dv * dv_act).astype(BF)

    full = lambda *shape: pl.BlockSpec(shape, lambda i: (0,) * len(shape))
    return pl.pallas_call(
        body, name=name, grid=(t // GROUP,),
        in_specs=[pl.BlockSpec((GROUP, da2), lambda i: (i, 0)), full(1, da), full(ng, GROUP, GROUP), full(ng, GROUP, GROUP),
                  full(GROUP, ng), pl.BlockSpec((GROUP, da), lambda i: (i, 0))],
        out_specs=[pl.BlockSpec((GROUP, da2), lambda i: (i, 0)), full(1, da), full(ng, GROUP, GROUP), full(GROUP, ng)],
        out_shape=[jax.ShapeDtypeStruct((t, da2), BF), jax.ShapeDtypeStruct((1, da), F32),
                   jax.ShapeDtypeStruct((ng, GROUP, GROUP), F32), jax.ShapeDtypeStruct((GROUP, ng), F32)],
        compiler_params=_params(("arbitrary",)),
    )(p, v_gain, w_s, w_st, b_st, dout)


_CONV_ROWS = 256
_CONV_COLS = 128


def _shift_down(x, prev, k):
    top = pltpu.roll(jnp.concatenate([prev, x[:8]], axis=0), k, 0)[8:16]
    return jnp.concatenate([top, pltpu.roll(x, k, 0)[8:]], axis=0)


def _shift_up(x, nxt, k):
    n = x.shape[0]
    bottom = pltpu.roll(jnp.concatenate([x[n - 8:], nxt], axis=0), 16 - k, 0)[0:8]
    return jnp.concatenate([pltpu.roll(x, n - k, 0)[:n - 8], bottom], axis=0)


def _conv_taps(a_ref, half, r0, first):
    x = a_ref[half, pl.ds(r0, _rows(a_ref)), :]
    prev = a_ref[half, pl.ds(jnp.maximum(r0 - 8, 0), 8), :]
    prev = jnp.where(first, 0.0, prev)
    return x, _shift_down(x, prev, 1), _shift_down(x, prev, 2)


def _rows(a_ref):
    return min(_CONV_ROWS, a_ref.shape[1])


def _conv_fwd(name, a3, cw, cb):
    _, t, f = a3.shape
    tc, rows = _CONV_COLS, min(_CONV_ROWS, t)

    def body(a_ref, cw_ref, cb_ref, y_ref):
        def step(r, carry):
            r0 = pl.multiple_of(r * rows, rows)
            c = []
            for half in range(2):
                x, x1, x2 = _conv_taps(a_ref, half, r0, r == 0)
                w = cw_ref[half]
                c.append(cb_ref[half] + (w[0:1] * x2 + w[1:2] * x1 + w[2:3] * x))
            y_ref[pl.ds(r0, rows), :] = (c[0] * jax.nn.sigmoid(c[0]) * c[1]).astype(BF)
            return carry

        lax.fori_loop(0, t // rows, step, 0)

    col = lambda *lead: pl.BlockSpec((*lead, tc), lambda j: (0,) * len(lead) + (j,))
    return pl.pallas_call(
        body, name=name, grid=(f // tc,), in_specs=[col(2, t), col(2, CONV_TAPS), col(2, 1)],
        out_specs=col(t), out_shape=jax.ShapeDtypeStruct((t, f), BF),
        compiler_params=_params(("parallel",), 40 * _MIB),
    )(a3, cw, cb)


def _conv_bwd(name, a3, cw, cb, dy):
    _, t, f = a3.shape
    tc, rows = _CONV_COLS, min(_CONV_ROWS, t)
    n_steps = t // rows

    def body(a_ref, cw_ref, cb_ref, dy_ref, da_ref, dcw_ref, dcb_ref):
        dcw_ref[...] = jnp.zeros_like(dcw_ref)
        dcb_ref[...] = jnp.zeros_like(dcb_ref)

        def step(s, nxt):
            r = n_steps - 1 - s
            r0 = pl.multiple_of(r * rows, rows)
            taps, c = [], []
            for half in range(2):
                x, x1, x2 = _conv_taps(a_ref, half, r0, r == 0)
                w = cw_ref[half]
                taps.append((x2, x1, x))
                c.append(cb_ref[half] + (w[0:1] * x2 + w[1:2] * x1 + w[2:3] * x))
            gate, val = c
            sg = jax.nn.sigmoid(gate)
            dyv = dy_ref[pl.ds(r0, rows), :].astype(F32)
            dcs = (dyv * val * (sg * (1.0 + gate * (1.0 - sg))), dyv * (gate * sg))
            new_nxt = []
            for half in range(2):
                dc, w = dcs[half], cw_ref[half]
                dcb_ref[half] += jnp.sum(dc, axis=0, keepdims=True)
                for tap in range(CONV_TAPS):
                    dcw_ref[half, tap:tap + 1, :] += jnp.sum(dc * taps[half][tap], axis=0, keepdims=True)
                da = w[2:3] * dc + w[1:2] * _shift_up(dc, nxt[half], 1) + w[0:1] * _shift_up(dc, nxt[half], 2)
                da_ref[half, pl.ds(r0, rows), :] = da.astype(BF)
                new_nxt.append(dc[:8])
            return tuple(new_nxt)

        zeros = jnp.zeros((8, tc), F32)
        lax.fori_loop(0, n_steps, step, (zeros, zeros))

    col = lambda *lead: pl.BlockSpec((*lead, tc), lambda j: (0,) * len(lead) + (j,))
    return pl.pallas_call(
        body, name=name, grid=(f // tc,), in_specs=[col(2, t), col(2, CONV_TAPS), col(2, 1), col(t)],
        out_specs=[col(2, t), col(2, CONV_TAPS), col(2, 1)],
        out_shape=[jax.ShapeDtypeStruct((2, t, f), BF), jax.ShapeDtypeStruct((2, CONV_TAPS, f), F32),
                   jax.ShapeDtypeStruct((2, 1, f), F32)],
        compiler_params=_params(("parallel",), 40 * _MIB),
    )(a3, cw, cb, dy)


_ATT_BLOCK = 256


def _split_dot(x, tri):
    hi = x.astype(BF)
    lo = (x - hi.astype(F32)).astype(BF)
    return jnp.dot(hi, tri, preferred_element_type=F32) + jnp.dot(lo, tri, preferred_element_type=F32)


def _logits(qb, kb, i, j, blk):
    z = lax.dot_general(qb, kb, _NT, preferred_element_type=F32) * (1.0 / math.sqrt(GROUP))
    qpos = i * blk + lax.broadcasted_iota(jnp.int32, (blk, blk), 0)
    kpos = j * blk + lax.broadcasted_iota(jnp.int32, (blk, blk), 1)
    mask = kpos < qpos
    lb = jnp.minimum(z, 0.0) - jnp.log1p(jnp.exp(-jnp.abs(z)))
    return mask, lb, jnp.where(mask, lb - z, 0.0)


def _attn_fwd(name, q, k, v):
    t, hd = q.shape
    blk = min(_ATT_BLOCK, t)

    def body(q_ref, k_ref, v_ref, o_ref, rest_ref):
        i = pl.program_id(1)
        qb = q_ref[...]
        ri = lax.broadcasted_iota(jnp.int32, (blk, blk), 0)
        ci = lax.broadcasted_iota(jnp.int32, (blk, blk), 1)
        tri = (ri >= ci).astype(BF)

        def step(s, carry):
            right, acc = carry
            j = i - s
            k0 = pl.multiple_of(j * blk, blk)
            mask, lb, l1m = _logits(qb, k_ref[pl.ds(k0, blk), :], i, j, blk)
            incl = _split_dot(l1m, tri)
            a = jnp.where(mask, jnp.exp(lb + (incl - l1m + right)), 0.0)
            acc = acc + jnp.dot(a.astype(BF), v_ref[pl.ds(k0, blk), :], preferred_element_type=F32)
            return right + incl[:, 0:1], acc

        right, acc = lax.fori_loop(0, i + 1, step, (jnp.zeros((blk, 1), F32), jnp.zeros((blk, GROUP), F32)))
        o_ref[...] = acc.astype(BF)
        rest_ref[...] = jnp.broadcast_to(right, (blk, GROUP))

    qspec = pl.BlockSpec((blk, GROUP), lambda h, i: (i, h))
    kvspec = pl.BlockSpec((t, GROUP), lambda h, i: (0, h))
    return pl.pallas_call(
        body, name=name, grid=(hd // GROUP, t // blk), in_specs=[qspec, kvspec, kvspec], out_specs=[qspec, qspec],
        out_shape=[jax.ShapeDtypeStruct((t, hd), BF), jax.ShapeDtypeStruct((t, hd), F32)],
        compiler_params=_params(("parallel", "arbitrary"), 32 * _MIB),
    )(q, k, v)


def _attn_bwd(name, q, k, v, rest, do):
    t, hd = q.shape
    blk = min(_ATT_BLOCK, t)
    nq = t // blk
    scale = 1.0 / math.sqrt(GROUP)

    def body(q_ref, k_ref, v_ref, rest_ref, do_ref, dq_ref, dk_ref, dv_ref, dk_acc, dv_acc):
        i = pl.program_id(1)

        @pl.when(i == 0)
        def _():
            dk_acc[...] = jnp.zeros_like(dk_acc)
            dv_acc[...] = jnp.zeros_like(dv_acc)

        qb, dob = q_ref[...], do_ref[...]
        total = rest_ref[:, 0:1]
        ri = lax.broadcasted_iota(jnp.int32, (blk, blk), 0)
        ci = lax.broadcasted_iota(jnp.int32, (blk, blk), 1)
        tri = (ri <= ci).astype(BF)

        def step(j, carry):
            left, gleft, dq = carry
            k0 = pl.multiple_of(j * blk, blk)
            kb, vb = k_ref[pl.ds(k0, blk), :], v_ref[pl.ds(k0, blk), :]
            mask, lb, l1m = _logits(qb, kb, i, j, blk)
            pre = _split_dot(l1m, tri)
            a = jnp.where(mask, jnp.exp(lb + (total - left - pre)), 0.0)
            g = a * lax.dot_general(dob, vb, _NT, preferred_element_type=F32)
            gpre = _split_dot(g, tri)
            beta = jnp.exp(lb)
            dz = jnp.where(mask, g * (1.0 - beta) - (gleft + gpre - g) * beta, 0.0) * scale
            dzb, ab = dz.astype(BF), a.astype(BF)
            dk_acc[pl.ds(k0, blk), :] += lax.dot_general(dzb, qb, _TN, preferred_element_type=F32)
            dv_acc[pl.ds(k0, blk), :] += lax.dot_general(ab, dob, _TN, preferred_element_type=F32)
            dq = dq + jnp.dot(dzb, kb, preferred_element_type=F32)
            return left + pre[:, blk - 1:blk], gleft + gpre[:, blk - 1:blk], dq

        zero = jnp.zeros((blk, 1), F32)
        _, _, dq = lax.fori_loop(0, i + 1, step, (zero, zero, jnp.zeros((blk, GROUP), F32)))
        dq_ref[...] = dq.astype(BF)

        @pl.when(i == nq - 1)
        def _():
            dk_ref[...] = dk_acc[...].astype(BF)
            dv_ref[...] = dv_acc[...].astype(BF)

    qspec = pl.BlockSpec((blk, GROUP), lambda h, i: (i, h))
    kvspec = pl.BlockSpec((t, GROUP), lambda h, i: (0, h))
    return pl.pallas_call(
        body, name=name, grid=(hd // GROUP, nq), in_specs=[qspec, kvspec, kvspec, qspec, qspec],
        out_specs=[qspec, kvspec, kvspec], out_shape=[jax.ShapeDtypeStruct((t, hd), BF)] * 3,
        scratch_shapes=[pltpu.VMEM((t, GROUP), F32), pltpu.VMEM((t, GROUP), F32)],
        compiler_params=_params(("parallel", "arbitrary"), 40 * _MIB),
    )(q, k, v, rest, do)


def _adamw_math(w, g, m, v):
    m = ADAM_B1 * m + (1.0 - ADAM_B1) * g
    v = ADAM_B2 * v + (1.0 - ADAM_B2) * (g * g)
    m_hat = m / (1.0 - ADAM_B1 ** ADAM_STEP)
    v_hat = v / (1.0 - ADAM_B2 ** ADAM_STEP)
    return -ADAM_LR * (m_hat / (jnp.sqrt(v_hat) + ADAM_EPS) + ADAM_WD * w), m, v


def _sum_adamw(name, parts, w, m, v):
    _, r, c = parts.shape
    tr = r if r * c <= 512 * 1024 else _tile_rows(r, max(8, (512 * 1024 // c) // 8 * 8))

    def body(p_ref, w_ref, m_ref, v_ref, g_ref, d_ref, nm_ref, nv_ref):
        g = p_ref[0].astype(F32)
        for dev in range(1, N_DEV):
            g = g + p_ref[dev].astype(F32)
        g_ref[...] = g
        d_ref[...], nm_ref[...], nv_ref[...] = _adamw_math(w_ref[...], g, m_ref[...], v_ref[...])

    row = pl.BlockSpec((tr, c), lambda i: (i, 0))
    return pl.pallas_call(
        body, name=name, grid=(r // tr,),
        in_specs=[pl.BlockSpec((N_DEV, tr, c), lambda i: (0, i, 0)), row, row, row], out_specs=[row] * 4,
        out_shape=[jax.ShapeDtypeStruct((r, c), F32)] * 4, compiler_params=_params(("parallel",), 40 * _MIB),
    )(parts, w, m, v)


def _tile_rows(r, pref):
    t = min(r, pref)
    while r % t or t % 8:
        t -= 1
    return t


def _place():
    x, y, c = lax.axis_index("x"), lax.axis_index("y"), lax.axis_index("c")
    return x, y, c, 4 * x + 2 * y + c


def _flip(x, y, c, k):
    return (1 - x if k & 4 else x, 1 - y if k & 2 else y, 1 - c if k & 1 else c)


def _all_gather(name, shards):
    n = len(shards)

    def body(*refs):
        ins, outs = refs[:n], refs[n:2 * n]
        send_sems, recv_sems, local_sems = refs[2 * n:]
        x, y, c, _ = _place()
        me, sibling = (x, y, c), (x, y, 1 - c)
        chips = [(1 - x, y), (x, 1 - y), (1 - x, 1 - y)]

        def copy(t, sem, block, to, src=None):
            slot = outs[t].at[4 * block[0] + 2 * block[1] + block[2]]
            return pltpu.make_async_remote_copy(
                src_ref=slot if src is None else src, dst_ref=slot, send_sem=send_sems.at[t, sem],
                recv_sem=recv_sems.at[t, sem], device_id=to, device_id_type=MESH)

        local, sent = [], []
        for t in range(n):
            local.append(pltpu.make_async_copy(ins[t], outs[t].at[4 * x + 2 * y + c], local_sems.at[t]))
            local[-1].start()
            first = [copy(t, 0, me, sibling, src=ins[t])]
            first += [copy(t, 1 + j, me, (*chip, c), src=ins[t]) for j, chip in enumerate(chips)]
            for cp in first:
                cp.start()
            sent += first
        for t in range(n):
            for j, chip in enumerate(chips):
                copy(t, 1 + j, (*chip, c), me).wait_recv()
                sent.append(copy(t, 4 + j, (*chip, c), sibling))
                sent[-1].start()
        for t in range(n):
            copy(t, 0, sibling, me).wait_recv()
            for j, chip in enumerate(chips):
                copy(t, 4 + j, (*chip, 1 - c), me).wait_recv()
        for cp in sent:
            cp.wait_send()
        for cp in local:
            cp.wait()

    any_spec = pl.BlockSpec(memory_space=pl.ANY)
    return pl.pallas_call(
        body, name=name, in_specs=[any_spec] * n, out_specs=[any_spec] * n,
        out_shape=[jax.ShapeDtypeStruct((N_DEV, *s.shape), s.dtype) for s in shards],
        scratch_shapes=[pltpu.SemaphoreType.DMA((n, 7)), pltpu.SemaphoreType.DMA((n, 7)), pltpu.SemaphoreType.DMA((n,))],
        compiler_params=pltpu.CompilerParams(has_side_effects=True),
    )(*shards)


def _scatter_partials(name, partials):
    n = len(partials)

    def body(*refs):
        ins, outs = refs[:n], refs[n:2 * n]
        send_sems, recv_sems, local_sems = refs[2 * n:]
        x, y, c, me = _place()

        def copy(t, k):
            peer = _flip(x, y, c, k)
            return pltpu.make_async_remote_copy(
                src_ref=ins[t].at[4 * peer[0] + 2 * peer[1] + peer[2]], dst_ref=outs[t].at[me],
                send_sem=send_sems.at[t, k - 1], recv_sem=recv_sems.at[t, k - 1], device_id=peer, device_id_type=MESH)

        for t in range(n):
            mine = pltpu.make_async_copy(ins[t].at[me], outs[t].at[me], local_sems.at[t])
            mine.start()
            for k in range(1, N_DEV):
                copy(t, k).start()
            if t > 0:
                _drain(t - 1, copy, ins, outs, local_sems, me)
        _drain(n - 1, copy, ins, outs, local_sems, me)

    any_spec = pl.BlockSpec(memory_space=pl.ANY)
    return pl.pallas_call(
        body, name=name, in_specs=[any_spec] * n, out_specs=[any_spec] * n,
        out_shape=[jax.ShapeDtypeStruct(p.shape, p.dtype) for p in partials],
        scratch_shapes=[pltpu.SemaphoreType.DMA((n, 7)), pltpu.SemaphoreType.DMA((n, 7)), pltpu.SemaphoreType.DMA((n,))],
        compiler_params=pltpu.CompilerParams(has_side_effects=True),
    )(*partials)


def _drain(t, copy, ins, outs, local_sems, me):
    for k in range(1, N_DEV):
        copy(t, k).wait()
    pltpu.make_async_copy(ins[t].at[me], outs[t].at[me], local_sems.at[t]).wait()


def _all_reduce_small(name, groups):
    c = groups[0][0].shape[1]
    parts, offsets, starts, r = [], [], [], 0
    for group in groups:
        starts.append(r)
        for p in group:
            parts.append(p)
            offsets.append(r)
            r += p.shape[0]
        r = -(-r // 8) * 8
    n = len(parts)

    def body(*refs):
        out_ref, slots, send_sems, recv_sems = refs[n:]
        x, y, c_, me = _place()
        slots[me] = jnp.zeros((r, c), F32)
        for p_ref, off in zip(refs[:n], offsets):
            slots[me, off:off + p_ref.shape[0], :] = p_ref[...]

        def copy(k):
            return pltpu.make_async_remote_copy(
                src_ref=slots.at[me], dst_ref=slots.at[me], send_sem=send_sems.at[k - 1], recv_sem=recv_sems.at[k - 1],
                device_id=_flip(x, y, c_, k), device_id_type=MESH)

        for k in range(1, N_DEV):
            copy(k).start()
        for k in range(1, N_DEV):
            copy(k).wait()
        total = slots[0]
        for dev in range(1, N_DEV):
            total = total + slots[dev]
        out_ref[...] = total

    vm = pl.BlockSpec(memory_space=pltpu.VMEM)
    summed = pl.pallas_call(
        body, name=name, in_specs=[vm] * n, out_specs=vm, out_shape=jax.ShapeDtypeStruct((r, c), F32),
        scratch_shapes=[pltpu.VMEM((N_DEV, r, c), F32), pltpu.SemaphoreType.DMA((7,)), pltpu.SemaphoreType.DMA((7,))],
        compiler_params=_params(None, (N_DEV + 6) * r * c * 4 + 8 * _MIB),
    )(*parts)
    return summed, starts


def _adamw(name, g, w, m, v):
    cols = w.shape[-1]
    flat = lambda a: a.reshape(-1, cols)

    def body(g_ref, w_ref, m_ref, v_ref, d_ref, nm_ref, nv_ref):
        d_ref[...], nm_ref[...], nv_ref[...] = _adamw_math(w_ref[...], g_ref[...], m_ref[...], v_ref[...])

    outs = pl.pallas_call(body, name=name, out_shape=[jax.ShapeDtypeStruct(flat(w).shape, F32)] * 3)(
        flat(g), flat(w), flat(m), flat(v))
    return [o.reshape(w.shape) for o in outs]


def kernel(x, pre_mix_g, post_mix_g, pre_ffn_g, post_ffn_g, a_w_in, a_v_norm_g, a_w_spatial, a_b_spatial, a_w_out, kv_norm_g, w_k, w_v, b_w_q, b_w_o, ffn_w_up, ffn_conv_w, ffn_conv_b, ffn_w_down, loss_target, m_pre_mix_g, m_post_mix_g, m_pre_ffn_g, m_post_ffn_g, m_a_w_in, m_a_v_norm_g, m_a_w_spatial, m_a_b_spatial, m_a_w_out, m_kv_norm_g, m_w_k, m_w_v, m_b_w_q, m_b_w_o, m_ffn_w_up, m_ffn_conv_w, m_ffn_conv_b, m_ffn_w_down, v_pre_mix_g, v_post_mix_g, v_pre_ffn_g, v_post_ffn_g, v_a_w_in, v_a_v_norm_g, v_a_w_spatial, v_a_b_spatial, v_a_w_out, v_kv_norm_g, v_w_k, v_w_v, v_b_w_q, v_b_w_o, v_ffn_w_up, v_ffn_conv_w, v_ffn_conv_b, v_ffn_w_down):
    t, d = x.shape[1], x.shape[2]
    f = ffn_w_down.shape[1] * N_DEV
    ng = d // GROUP
    me = 4 * lax.axis_index("x") + 2 * lax.axis_index("y") + lax.axis_index("c")
    x2, target = x.reshape(t, d), loss_target.reshape(t, d)

    gathered = _all_gather("gather_weights", [
        a_w_in[0].astype(BF), a_w_out[0].astype(BF), w_k.astype(BF), w_v.astype(BF), b_w_q[0].astype(BF),
        b_w_o[0].astype(BF), ffn_w_up[0].astype(BF), ffn_w_up[1].astype(BF), ffn_w_down[0].astype(BF),
        ffn_w_down[1].astype(BF), ffn_conv_w.reshape(2 * CONV_TAPS, -1), a_v_norm_g])
    g_in, g_out, g_k, g_v, g_q, g_o, g_up0, g_up1, g_dn0, g_dn1, g_cw, g_vg = gathered
    w_out_f, w_k_f, w_v_f, w_q_f, w_o_f = (g.reshape(d, d) for g in (g_out, g_k, g_v, g_q, g_o))
    g_up = (g_up0, g_up1)
    w_dn_f = (g_dn0.reshape(f, d), g_dn1.reshape(f, d))
    cw_full = jnp.transpose(g_cw.reshape(N_DEV, 2, CONV_TAPS, -1), (1, 2, 0, 3)).reshape(2, CONV_TAPS, 2, f)
    cw_l = [jnp.transpose(cw_full[l], (1, 0, 2)) for l in range(2)]
    cb_l = [ffn_conv_b[l].reshape(2, 1, f) for l in range(2)]
    vg_full = g_vg.reshape(1, d)
    w_s = a_w_spatial[0]
    w_st = jnp.swapaxes(w_s, 1, 2)
    b_st = a_b_spatial[0].T

    def ffn_fwd(l, h_in, fn):
        a3 = _mm_nn_blk(f"ffn{l}_up", fn, g_up[l], F32, halves=True)
        yv = _conv_fwd(f"ffn{l}_conv", a3, cw_l[l], cb_l[l])
        fo = _mm_nn(f"ffn{l}_down", yv, w_dn_f[l], F32)
        return a3, yv, fo

    (hn0,) = _norms_fwd("pre_mix0", x2, [pre_mix_g[0]])
    p0 = _mm_nn_blk("sgu_in", hn0, g_in, F32)
    sg = _sgu_fwd("sgu", p0, vg_full, w_s, b_st)
    mix0 = _mm_nn("sgu_out", sg, w_out_f, F32)
    h1, fn0 = _resid_norms("post_mix0", x2, mix0, post_mix_g[0], [pre_ffn_g[0]])
    a3_0, y0, f0 = ffn_fwd(0, h1, fn0)
    h2, hn1, kvn = _resid_norms("post_ffn0", h1, f0, post_ffn_g[0], [pre_mix_g[1], kv_norm_g])
    q = _mm_nn("attn_q", hn1, w_q_f, BF)
    kk = _mm_nn("attn_k", kvn, w_k_f, BF)
    vv = _mm_nn("attn_v", kvn, w_v_f, BF)
    att, rest = _attn_fwd("attn", q, kk, vv)
    mix1 = _mm_nn("attn_o", att, w_o_f, F32)
    h3, fn1 = _resid_norms("post_mix1", h2, mix1, post_mix_g[1], [pre_ffn_g[1]])
    a3_1, y1, f1 = ffn_fwd(1, h3, fn1)
    dh, loss_part = _resid_loss("loss", h3, f1, post_ffn_g[1], target)
    loss = lax.psum(loss_part[0, 0], ("x", "y", "c"))

    def ffn_bwd(l, dh_out, h_in, fn, a3, yv, fo):
        dfo, d_post = _norm_bwd(f"post_ffn{l}_bwd", fo, post_ffn_g[l], dh_out, out_dtype=BF)
        dy = _mm_nt(f"ffn{l}_down_dx", dfo, w_dn_f[l], BF)
        dw_dn = _mm_tn(f"ffn{l}_down_dw", yv, dfo)
        da3, dcw, dcb = _conv_bwd(f"ffn{l}_conv_bwd", a3, cw_l[l], cb_l[l], dy)
        dfn = _mm_nt_blk(f"ffn{l}_up_dx", da3, g_up[l], F32, halves=True)
        dw_up = _mm_tn_blk(f"ffn{l}_up_dw", fn, da3, halves=True)
        dh_in, d_pre = _norm_bwd(f"pre_ffn{l}_bwd", h_in, pre_ffn_g[l], dfn, res=dh_out)
        return dh_in, d_post, d_pre, dw_dn, dw_up, dcw, dcb

    dh3, d_post_ffn1, d_pre_ffn1, dw_dn1, dw_up1, dcw1, dcb1 = ffn_bwd(1, dh, h3, fn1, a3_1, y1, f1)
    dmix1, d_post_mix1 = _norm_bwd("post_mix1_bwd", mix1, post_mix_g[1], dh3, out_dtype=BF)
    datt = _mm_nt("attn_o_dx", dmix1, w_o_f, BF)
    dw_o = _mm_tn("attn_o_dw", att, dmix1)
    dq, dk, dv = _attn_bwd("attn_bwd", q, kk, vv, rest, datt)
    dhn1 = _mm_nt("attn_q_dx", dq, w_q_f, F32)
    dw_q = _mm_tn("attn_q_dw", hn1, dq)
    dkvn = _mm_nt("attn_v_dx", dv, w_v_f, F32, add=_mm_nt("attn_k_dx", dk, w_k_f, F32))
    dw_k = _mm_tn("attn_k_dw", kvn, dk)
    dw_v = _mm_tn("attn_v_dw", kvn, dv)
    dh2a, d_pre_mix1 = _norm_bwd("pre_mix1_bwd", h2, pre_mix_g[1], dhn1, res=dh3)
    dh2, d_kv = _norm_bwd("kv_norm_bwd", h2, kv_norm_g, dkvn, res=dh2a)
    dh1, d_post_ffn0, d_pre_ffn0, dw_dn0, dw_up0, dcw0, dcb0 = ffn_bwd(0, dh2, h1, fn0, a3_0, y0, f0)
    dmix0, d_post_mix0 = _norm_bwd("post_mix0_bwd", mix0, post_mix_g[0], dh1, out_dtype=BF)
    dsg = _mm_nt("sgu_out_dx", dmix0, w_out_f, BF)
    dw_out = _mm_tn("sgu_out_dw", sg, dmix0)
    dp0, d_vg, d_ws, d_bst = _sgu_bwd("sgu_bwd", p0, vg_full, w_s, w_st, b_st, dsg)
    dhn0 = _mm_nt_blk("sgu_in_dx", dp0, g_in, F32)
    dw_in = _mm_tn_blk("sgu_in_dw", hn0, dp0)
    grad_x, d_pre_mix0 = _norm_bwd("pre_mix0_bwd", x2, pre_mix_g[0], dhn0, res=dh1)

    def conv_w_grad(dcw):
        return jnp.transpose(dcw, (1, 0, 2)).reshape(CONV_TAPS, 2 * f)

    small = [
        ([d_pre_mix0, d_pre_mix1], (2, d)), ([d_post_mix0, d_post_mix1], (2, d)),
        ([d_pre_ffn0, d_pre_ffn1], (2, d)), ([d_post_ffn0, d_post_ffn1], (2, d)),
        ([d_kv], (d,)), ([d_vg], (1, d)), ([d_bst.T], (1, ng, GROUP)), ([d_ws], (1, ng, GROUP, GROUP)),
        ([dcb0, dcb1], (2, 2 * f)), ([conv_w_grad(dcw0), conv_w_grad(dcw1)], (2, CONV_TAPS, 2 * f)),
    ]
    width = V7X_LANES * math.gcd(d // V7X_LANES, 2 * f // V7X_LANES)
    summed, offsets = _all_reduce_small("reduce_small", [[a.reshape(-1, width) for a in group] for group, _ in small])
    full = [summed[off:off + math.prod(shape) // width].reshape(shape) for off, (_, shape) in zip(offsets, small)]
    g_pre_mix, g_post_mix, g_pre_ffn, g_post_ffn, g_kv, g_vgain, g_bs, g_ws, g_cb, g_cwf = full
    cw_w = 2 * f // N_DEV
    g_vgain = lax.dynamic_slice_in_dim(g_vgain, me * (d // N_DEV), d // N_DEV, axis=1)
    g_cwf = lax.dynamic_slice_in_dim(g_cwf, me * cw_w, cw_w, axis=2)

    big = [dw_in, dw_out.reshape(N_DEV, -1, d), dw_k.reshape(N_DEV, -1, d), dw_v.reshape(N_DEV, -1, d),
           dw_q.reshape(N_DEV, -1, d), dw_o.reshape(N_DEV, -1, d), dw_up0, dw_up1,
           dw_dn0.reshape(N_DEV, -1, d), dw_dn1.reshape(N_DEV, -1, d)]
    parts = _scatter_partials("scatter_grads", big)

    def big_update(name, part, w, m, v):
        return _sum_adamw(name, part, w.reshape(part.shape[1:]), m.reshape(part.shape[1:]), v.reshape(part.shape[1:]))

    def small_update(name, g, w, m, v):
        return [g] + _adamw(name, g, w, m, v)

    def stacked(name, part0, part1, w, m, v):
        outs = [big_update(f"{name}{l}", p, w[l], m[l], v[l]) for l, p in enumerate((part0, part1))]
        return [jnp.stack([a, b]).reshape(w.shape) for a, b in zip(*outs)]

    def single(name, part, w, m, v):
        return [o.reshape(w.shape) for o in big_update(name, part, w, m, v)]

    results = {
        "pre_mix_g": small_update("adam_pre_mix", g_pre_mix, pre_mix_g, m_pre_mix_g, v_pre_mix_g),
        "post_mix_g": small_update("adam_post_mix", g_post_mix, post_mix_g, m_post_mix_g, v_post_mix_g),
        "pre_ffn_g": small_update("adam_pre_ffn", g_pre_ffn, pre_ffn_g, m_pre_ffn_g, v_pre_ffn_g),
        "post_ffn_g": small_update("adam_post_ffn", g_post_ffn, post_ffn_g, m_post_ffn_g, v_post_ffn_g),
        "a_w_in": single("adam_a_w_in", parts[0], a_w_in, m_a_w_in, v_a_w_in),
        "a_v_norm_g": small_update("adam_a_v_norm", g_vgain, a_v_norm_g, m_a_v_norm_g, v_a_v_norm_g),
        "a_w_spatial": small_update("adam_a_w_spatial", g_ws, a_w_spatial, m_a_w_spatial, v_a_w_spatial),
        "a_b_spatial": small_update("adam_a_b_spatial", g_bs, a_b_spatial, m_a_b_spatial, v_a_b_spatial),
        "a_w_out": single("adam_a_w_out", parts[1], a_w_out, m_a_w_out, v_a_w_out),
        "kv_norm_g": small_update("adam_kv_norm", g_kv, kv_norm_g, m_kv_norm_g, v_kv_norm_g),
        "w_k": single("adam_w_k", parts[2], w_k, m_w_k, v_w_k),
        "w_v": single("adam_w_v", parts[3], w_v, m_w_v, v_w_v),
        "b_w_q": single("adam_b_w_q", parts[4], b_w_q, m_b_w_q, v_b_w_q),
        "b_w_o": single("adam_b_w_o", parts[5], b_w_o, m_b_w_o, v_b_w_o),
        "ffn_w_up": stacked("adam_ffn_w_up", parts[6], parts[7], ffn_w_up, m_ffn_w_up, v_ffn_w_up),
        "ffn_conv_w": small_update("adam_ffn_conv_w", g_cwf, ffn_conv_w, m_ffn_conv_w, v_ffn_conv_w),
        "ffn_conv_b": small_update("adam_ffn_conv_b", g_cb, ffn_conv_b, m_ffn_conv_b, v_ffn_conv_b),
        "ffn_w_down": stacked("adam_ffn_w_down", parts[8], parts[9], ffn_w_down, m_ffn_w_down, v_ffn_w_down),
    }
    order = ["pre_mix_g", "post_mix_g", "pre_ffn_g", "post_ffn_g", "a_w_in", "a_v_norm_g", "a_w_spatial", "a_b_spatial",
             "a_w_out", "kv_norm_g", "w_k", "w_v", "b_w_q", "b_w_o", "ffn_w_up", "ffn_conv_w", "ffn_conv_b", "ffn_w_down"]
    outs = [loss, grad_x.reshape(x.shape)]
    for idx in range(4):
        outs += [results[n][idx] for n in order]
    return tuple(outs)
```

```python
import math

import jax
import jax.numpy as jnp
from jax import lax
from jax.experimental import pallas as pl
from jax.experimental.pallas import tpu as pltpu

F32 = jnp.float32
BF = jnp.bfloat16
MESH = pl.DeviceIdType.MESH

N_DEV = 8
NORM_EPS = 1e-6
GROUP = 128
CONV_TAPS = 3
ADAM_LR, ADAM_B1, ADAM_B2, ADAM_EPS, ADAM_WD, ADAM_STEP = 0.001, 0.9, 0.999, 1e-08, 0.01, 10
EXP_FLOOR = -104.0

V7X_LANES = 128
V7X_VMEM_BYTES = 64 * 1024 * 1024
_MIB = 1024 * 1024

_NN = (((1,), (0,)), ((), ()))
_NT = (((1,), (1,)), ((), ()))
_TN = (((0,), (0,)), ((), ()))


def _tile(n, pref):
    if n <= pref:
        return n
    t = (pref // V7X_LANES) * V7X_LANES
    while t > V7X_LANES and n % t:
        t -= V7X_LANES
    assert n % t == 0, (n, pref)
    return t


def _nbytes(shape, dtype):
    return math.prod(shape) * jnp.dtype(dtype).itemsize


def _params(sem=None, vmem=None):
    kw = {}
    if sem is not None:
        kw["dimension_semantics"] = sem
    if vmem is not None:
        kw["vmem_limit_bytes"] = int(min(max(vmem, 16 * _MIB), V7X_VMEM_BYTES - 8 * _MIB))
    return pltpu.CompilerParams(**kw)


def _mm(name, a, b, *, dims, grid, a_blk, a_map, b_blk, b_map, o_blk, o_map, out_shape, out_dtype,
        add=None, add_blk=None, add_map=None):
    nk = grid[2]
    acc_shape = tuple(d for d in o_blk if d is not None)

    def body(*refs):
        if add is None:
            a_ref, b_ref, o_ref = refs[:3]
            c_ref, scr = None, refs[3:]
        else:
            a_ref, b_ref, c_ref, o_ref = refs[:4]
            scr = refs[4:]
        part = lax.dot_general(a_ref[...], b_ref[...], dims, preferred_element_type=F32)

        def finish(x):
            if c_ref is not None:
                x = x + c_ref[...].astype(F32)
            o_ref[...] = x.astype(o_ref.dtype)

        if nk == 1:
            finish(part)
        else:
            acc = scr[0]
            k = pl.program_id(2)

            @pl.when(k == 0)
            def _():
                acc[...] = part

            @pl.when(k > 0)
            def _():
                acc[...] += part

            @pl.when(k == nk - 1)
            def _():
                finish(acc[...])

    in_specs = [pl.BlockSpec(a_blk, a_map), pl.BlockSpec(b_blk, b_map)]
    operands = [a, b]
    vmem = 2 * (_nbytes(acc_shape, out_dtype) + _nbytes([d for d in a_blk if d], a.dtype)
                + _nbytes([d for d in b_blk if d], b.dtype)) + 2 * _nbytes(acc_shape, F32)
    if add is not None:
        in_specs.append(pl.BlockSpec(add_blk, add_map))
        operands.append(add)
        vmem += 2 * _nbytes(acc_shape, add.dtype)
    scratch = [pltpu.VMEM(acc_shape, F32)] if nk > 1 else []
    return pl.pallas_call(
        body, name=name, grid=grid, in_specs=in_specs, out_specs=pl.BlockSpec(o_blk, o_map),
        out_shape=jax.ShapeDtypeStruct(out_shape, out_dtype), scratch_shapes=scratch,
        compiler_params=_params(("parallel", "parallel", "arbitrary"), vmem + 8 * _MIB),
    )(*operands)


def _mm_nn(name, x, w, out_dtype):
    t, kd = x.shape
    n = w.shape[1]
    tm, tn, tk = _tile(t, 1024), _tile(n, 1024), _tile(kd, 1024 if kd > 2048 else 2048)
    return _mm(name, x, w, dims=_NN, grid=(t // tm, n // tn, kd // tk),
               a_blk=(tm, tk), a_map=lambda i, j, k: (i, k), b_blk=(tk, tn), b_map=lambda i, j, k: (k, j),
               o_blk=(tm, tn), o_map=lambda i, j, k: (i, j), out_shape=(t, n), out_dtype=out_dtype)


def _mm_nn_blk(name, x, g, out_dtype, halves=False):
    t, kd = x.shape
    cw = g.shape[2]
    tm = _tile(t, 1024)
    if halves:
        o_blk, o_map, out_shape = (None, tm, cw), (lambda i, j, k: (j // 4, i, j % 4)), (2, t, 4 * cw)
    else:
        o_blk, o_map, out_shape = (tm, cw), (lambda i, j, k: (i, j)), (t, N_DEV * cw)
    return _mm(name, x, g, dims=_NN, grid=(t // tm, N_DEV, 1),
               a_blk=(tm, kd), a_map=lambda i, j, k: (i, 0), b_blk=(None, kd, cw), b_map=lambda i, j, k: (j, 0, 0),
               o_blk=o_blk, o_map=o_map, out_shape=out_shape, out_dtype=out_dtype)


def _mm_nt(name, dy, w, out_dtype, add=None):
    t, n = dy.shape
    kd = w.shape[0]
    tm, tn = _tile(t, 1024), _tile(kd, 512)
    kw = {}
    if add is not None:
        kw = dict(add=add, add_blk=(tm, tn), add_map=lambda i, j, k: (i, j))
    return _mm(name, dy, w, dims=_NT, grid=(t // tm, kd // tn, 1),
               a_blk=(tm, n), a_map=lambda i, j, k: (i, 0), b_blk=(tn, n), b_map=lambda i, j, k: (j, 0),
               o_blk=(tm, tn), o_map=lambda i, j, k: (i, j), out_shape=(t, kd), out_dtype=out_dtype, **kw)


def _mm_nt_blk(name, dy, g, out_dtype, halves=False):
    kd, cw = g.shape[1], g.shape[2]
    t = dy.shape[1] if halves else dy.shape[0]
    tm = _tile(t, 512)
    if halves:
        a_blk, a_map = (None, tm, cw), (lambda i, j, k: (k // 4, i, k % 4))
    else:
        a_blk, a_map = (tm, cw), (lambda i, j, k: (i, k))
    return _mm(name, dy, g, dims=_NT, grid=(t // tm, 1, N_DEV),
               a_blk=a_blk, a_map=a_map, b_blk=(None, kd, cw), b_map=lambda i, j, k: (k, 0, 0),
               o_blk=(tm, kd), o_map=lambda i, j, k: (i, 0), out_shape=(t, kd), out_dtype=out_dtype)


def _mm_tn(name, x, dy):
    t, kd = x.shape
    n = dy.shape[1]
    tmx, tk = _tile(kd, 1024), _tile(t, 512)
    return _mm(name, x, dy, dims=_TN, grid=(kd // tmx, 1, t // tk),
               a_blk=(tk, tmx), a_map=lambda i, j, k: (k, i), b_blk=(tk, n), b_map=lambda i, j, k: (k, 0),
               o_blk=(tmx, n), o_map=lambda i, j, k: (i, 0), out_shape=(kd, n), out_dtype=BF)


def _mm_tn_blk(name, x, dy, halves=False):
    t, kd = x.shape
    cw = dy.shape[2] // 4 if halves else dy.shape[1] // N_DEV
    tmx, tk = _tile(kd, 2048 if cw <= 512 else 1024), _tile(t, 1024 if cw <= 512 else 512)
    if halves:
        b_blk, b_map = (None, tk, cw), (lambda i, j, k: (j // 4, k, j % 4))
    else:
        b_blk, b_map = (tk, cw), (lambda i, j, k: (k, j))
    return _mm(name, x, dy, dims=_TN, grid=(kd // tmx, N_DEV, t // tk),
               a_blk=(tk, tmx), a_map=lambda i, j, k: (k, i), b_blk=b_blk, b_map=b_map,
               o_blk=(None, tmx, cw), o_map=lambda i, j, k: (j, i, 0), out_shape=(N_DEV, kd, cw), out_dtype=BF)


def _rms(x, g):
    r = lax.rsqrt(jnp.mean(x * x, axis=-1, keepdims=True) + NORM_EPS)
    return x * r * g


def _rms_bwd_math(x, g, dy):
    d = x.shape[-1]
    r = lax.rsqrt(jnp.mean(x * x, axis=-1, keepdims=True) + NORM_EPS)
    xh = x * r
    u = dy * g
    dx = r * u - xh * (jnp.sum(xh * u, axis=-1, keepdims=True) * (r / d))
    return dx, jnp.sum(dy * xh, axis=0, keepdims=True)


def _row_specs(tr, d, n):
    return [pl.BlockSpec((tr, d), lambda i: (i, 0)) for _ in range(n)]


def _vec_specs(d, n):
    return [pl.BlockSpec((1, d), lambda i: (0, 0)) for _ in range(n)]


def _norms_fwd(name, h, gains):
    t, d = h.shape
    tr, ng = _tile(t, 256), len(gains)

    def body(h_ref, *refs):
        x = h_ref[...]
        for g_ref, o_ref in zip(refs[:ng], refs[ng:]):
            o_ref[...] = _rms(x, g_ref[...]).astype(BF)

    return pl.pallas_call(
        body, name=name, grid=(t // tr,), in_specs=_row_specs(tr, d, 1) + _vec_specs(d, ng),
        out_specs=_row_specs(tr, d, ng), out_shape=[jax.ShapeDtypeStruct((t, d), BF)] * ng,
        compiler_params=_params(("parallel",)),
    )(h, *[g.reshape(1, d) for g in gains])


def _resid_norms(name, h, m, g_post, gains):
    t, d = h.shape
    tr, ng = _tile(t, 256), len(gains)

    def body(h_ref, m_ref, gp_ref, *refs):
        hn = h_ref[...] + _rms(m_ref[...], gp_ref[...])
        refs[ng][...] = hn
        for g_ref, o_ref in zip(refs[:ng], refs[ng + 1:]):
            o_ref[...] = _rms(hn, g_ref[...]).astype(BF)

    return pl.pallas_call(
        body, name=name, grid=(t // tr,), in_specs=_row_specs(tr, d, 2) + _vec_specs(d, 1 + ng),
        out_specs=_row_specs(tr, d, 1 + ng),
        out_shape=[jax.ShapeDtypeStruct((t, d), F32)] + [jax.ShapeDtypeStruct((t, d), BF)] * ng,
        compiler_params=_params(("parallel",)),
    )(h, m, g_post.reshape(1, d), *[g.reshape(1, d) for g in gains])


def _resid_loss(name, h, m, g_post, target):
    t, d = h.shape
    tr = _tile(t, 256)

    def body(h_ref, m_ref, gp_ref, t_ref, dy_ref, loss_ref):
        diff = h_ref[...] + _rms(m_ref[...], gp_ref[...]) - t_ref[...]
        dy_ref[...] = diff * (1.0 / d)

        @pl.when(pl.program_id(0) == 0)
        def _():
            loss_ref[...] = jnp.zeros_like(loss_ref)

        per_row = jnp.sum(diff * diff, axis=-1, keepdims=True) * (1.0 / d)
        loss_ref[...] += 0.5 * jnp.sum(per_row, axis=0, keepdims=True)

    return pl.pallas_call(
        body, name=name, grid=(t // tr,),
        in_specs=_row_specs(tr, d, 2) + _vec_specs(d, 1) + _row_specs(tr, d, 1),
        out_specs=[pl.BlockSpec((tr, d), lambda i: (i, 0)), pl.BlockSpec((1, 1), lambda i: (0, 0))],
        out_shape=[jax.ShapeDtypeStruct((t, d), F32), jax.ShapeDtypeStruct((1, 1), F32)],
        compiler_params=_params(("arbitrary",)),
    )(h, m, g_post.reshape(1, d), target)


def _norm_bwd(name, x, g, dy, res=None, out_dtype=F32):
    t, d = x.shape
    tr = _tile(t, 256)
    has_res = res is not None

    def body(x_ref, dy_ref, g_ref, *refs):
        dx_ref, dg_ref = refs[-2:]
        dx, dg = _rms_bwd_math(x_ref[...].astype(F32), g_ref[...], dy_ref[...].astype(F32))
        if has_res:
            dx = dx + refs[0][...]
        dx_ref[...] = dx.astype(dx_ref.dtype)

        @pl.when(pl.program_id(0) == 0)
        def _():
            dg_ref[...] = jnp.zeros_like(dg_ref)

        dg_ref[...] += dg

    ops = [x, dy, g.reshape(1, d)] + ([res] if has_res else [])
    return pl.pallas_call(
        body, name=name, grid=(t // tr,),
        in_specs=_row_specs(tr, d, 2) + _vec_specs(d, 1) + _row_specs(tr, d, int(has_res)),
        out_specs=[pl.BlockSpec((tr, d), lambda i: (i, 0)), pl.BlockSpec((1, d), lambda i: (0, 0))],
        out_shape=[jax.ShapeDtypeStruct((t, d), out_dtype), jax.ShapeDtypeStruct((1, d), F32)],
        compiler_params=_params(("arbitrary",)),
    )(*ops)


_GELU_C = math.sqrt(2.0 / math.pi)
_GELU_A = 0.044715


def _gelu(x):
    return 0.5 * x * (1.0 + jnp.tanh(_GELU_C * (x + _GELU_A * x * x * x)))


def _gelu_and_grad(x):
    th = jnp.tanh(_GELU_C * (x + _GELU_A * x * x * x))
    grad = 0.5 * (1.0 + th) + 0.5 * x * (1.0 - th * th) * (_GELU_C * (1.0 + 3.0 * _GELU_A * x * x))
    return 0.5 * x * (1.0 + th), grad


def _causal(n):
    return lax.broadcasted_iota(jnp.int32, (n, n), 1) <= lax.broadcasted_iota(jnp.int32, (n, n), 0)


def _sgu_fwd(name, p, v_gain, w_s, b_st):
    t, da2 = p.shape
    da = da2 // 2
    ng = da // GROUP

    def body(p_ref, vg_ref, ws_ref, bst_ref, o_ref):
        keep = _causal(GROUP)
        for g in range(ng):
            lo = g * GROUP
            u = _gelu(p_ref[:, lo:lo + GROUP])
            vn = _rms(_gelu(p_ref[:, da + lo:da + lo + GROUP]), vg_ref[:, lo:lo + GROUP])
            w = jnp.where(keep, ws_ref[g], 0.0).astype(BF)
            mixed = jnp.dot(w, vn.astype(BF), preferred_element_type=F32) + bst_ref[:, g:g + 1]
            o_ref[:, lo:lo + GROUP] = (u * mixed).astype(BF)

    return pl.pallas_call(
        body, name=name, grid=(t // GROUP,),
        in_specs=[pl.BlockSpec((GROUP, da2), lambda i: (i, 0)), pl.BlockSpec((1, da), lambda i: (0, 0)),
                  pl.BlockSpec((ng, GROUP, GROUP), lambda i: (0, 0, 0)), pl.BlockSpec((GROUP, ng), lambda i: (0, 0))],
        out_specs=pl.BlockSpec((GROUP, da), lambda i: (i, 0)),
        out_shape=jax.ShapeDtypeStruct((t, da), BF),
        compiler_params=_params(("parallel",)),
    )(p, v_gain, w_s, b_st)


def _sgu_bwd(name, p, v_gain, w_s, w_st, b_st, dout):
    t, da2 = p.shape
    da = da2 // 2
    ng = da // GROUP

    def body(p_ref, vg_ref, ws_ref, wst_ref, bst_ref, do_ref, dp_ref, dvg_ref, dws_ref, dbst_ref):
        @pl.when(pl.program_id(0) == 0)
        def _():
            dvg_ref[...] = jnp.zeros_like(dvg_ref)
            dws_ref[...] = jnp.zeros_like(dws_ref)
            dbst_ref[...] = jnp.zeros_like(dbst_ref)

        keep = _causal(GROUP)
        keep_t = lax.broadcasted_iota(jnp.int32, (GROUP, GROUP), 0) <= lax.broadcasted_iota(jnp.int32, (GROUP, GROUP), 1)
        for g in range(ng):
            lo = g * GROUP
            u, du = _gelu_and_grad(p_ref[:, lo:lo + GROUP])
            v, dv_act = _gelu_and_grad(p_ref[:, da + lo:da + lo + GROUP])
            gain = vg_ref[:, lo:lo + GROUP]
            r = lax.rsqrt(jnp.mean(v * v, axis=-1, keepdims=True) + NORM_EPS)
            vh = v * r
            vnb = (vh * gain).astype(BF)
            w = jnp.where(keep, ws_ref[g], 0.0).astype(BF)
            wt = jnp.where(keep_t, wst_ref[g], 0.0).astype(BF)
            mixed = jnp.dot(w, vnb, preferred_element_type=F32) + bst_ref[:, g:g + 1]
            dout_g = do_ref[:, lo:lo + GROUP].astype(F32)
            dmixed = dout_g * u
            dmb = dmixed.astype(BF)
            dbst_ref[:, g:g + 1] += jnp.sum(dmixed, axis=1, keepdims=True)
            dws_ref[g] += jnp.where(keep, lax.dot_general(dmb, vnb, _NT, preferred_element_type=F32), 0.0)
            dvn = jnp.dot(wt, dmb, preferred_element_type=F32)
            dvg_ref[:, lo:lo + GROUP] += jnp.sum(dvn * vh, axis=0, keepdims=True)
            dvh = dvn * gain
            dv = r * dvh - vh * (jnp.sum(vh * dvh, axis=-1, keepdims=True) * (r / GROUP))
            dp_ref[:, lo:lo + GROUP] = (dout_g * mixed * du).astype(BF)
            dp_ref[:, da + lo:da + lo + GROUP] = (dv * dv_act).astype(BF)

    full = lambda *shape: pl.BlockSpec(shape, lambda i: (0,) * len(shape))
    return pl.pallas_call(
        body, name=name, grid=(t // GROUP,),
        in_specs=[pl.BlockSpec((GROUP, da2), lambda i: (i, 0)), full(1, da), full(ng, GROUP, GROUP), full(ng, GROUP, GROUP),
                  full(GROUP, ng), pl.BlockSpec((GROUP, da), lambda i: (i, 0))],
        out_specs=[pl.BlockSpec((GROUP, da2), lambda i: (i, 0)), full(1, da), full(ng, GROUP, GROUP), full(GROUP, ng)],
        out_shape=[jax.ShapeDtypeStruct((t, da2), BF), jax.ShapeDtypeStruct((1, da), F32),
                   jax.ShapeDtypeStruct((ng, GROUP, GROUP), F32), jax.ShapeDtypeStruct((GROUP, ng), F32)],
        compiler_params=_params(("arbitrary",)),
    )(p, v_gain, w_s, w_st, b_st, dout)


_CONV_ROWS = 256
_CONV_COLS = 128


def _shift_down(x, prev, k):
    top = pltpu.roll(jnp.concatenate([prev, x[:8]], axis=0), k, 0)[8:16]
    return jnp.concatenate([top, pltpu.roll(x, k, 0)[8:]], axis=0)


def _shift_up(x, nxt, k):
    n = x.shape[0]
    bottom = pltpu.roll(jnp.concatenate([x[n - 8:], nxt], axis=0), 16 - k, 0)[0:8]
    return jnp.concatenate([pltpu.roll(x, n - k, 0)[:n - 8], bottom], axis=0)


def _conv_taps(a_ref, half, r0, first):
    x = a_ref[half, pl.ds(r0, _rows(a_ref)), :]
    prev = a_ref[half, pl.ds(jnp.maximum(r0 - 8, 0), 8), :]
    prev = jnp.where(first, 0.0, prev)
    return x, _shift_down(x, prev, 1), _shift_down(x, prev, 2)


def _rows(a_ref):
    return min(_CONV_ROWS, a_ref.shape[1])


def _conv_fwd(name, a3, cw, cb):
    _, t, f = a3.shape
    tc, rows = _CONV_COLS, min(_CONV_ROWS, t)

    def body(a_ref, cw_ref, cb_ref, y_ref):
        def step(r, carry):
            r0 = pl.multiple_of(r * rows, rows)
            c = []
            for half in range(2):
                x, x1, x2 = _conv_taps(a_ref, half, r0, r == 0)
                w = cw_ref[half]
                c.append(cb_ref[half] + (w[0:1] * x2 + w[1:2] * x1 + w[2:3] * x))
            y_ref[pl.ds(r0, rows), :] = (c[0] * jax.nn.sigmoid(c[0]) * c[1]).astype(BF)
            return carry

        lax.fori_loop(0, t // rows, step, 0)

    col = lambda *lead: pl.BlockSpec((*lead, tc), lambda j: (0,) * len(lead) + (j,))
    return pl.pallas_call(
        body, name=name, grid=(f // tc,), in_specs=[col(2, t), col(2, CONV_TAPS), col(2, 1)],
        out_specs=col(t), out_shape=jax.ShapeDtypeStruct((t, f), BF),
        compiler_params=_params(("parallel",), 40 * _MIB),
    )(a3, cw, cb)


def _conv_bwd(name, a3, cw, cb, dy):
    _, t, f = a3.shape
    tc, rows = _CONV_COLS, min(_CONV_ROWS, t)
    n_steps = t // rows

    def body(a_ref, cw_ref, cb_ref, dy_ref, da_ref, dcw_ref, dcb_ref):
        dcw_ref[...] = jnp.zeros_like(dcw_ref)
        dcb_ref[...] = jnp.zeros_like(dcb_ref)

        def step(s, nxt):
            r = n_steps - 1 - s
            r0 = pl.multiple_of(r * rows, rows)
            taps, c = [], []
            for half in range(2):
                x, x1, x2 = _conv_taps(a_ref, half, r0, r == 0)
                w = cw_ref[half]
                taps.append((x2, x1, x))
                c.append(cb_ref[half] + (w[0:1] * x2 + w[1:2] * x1 + w[2:3] * x))
            gate, val = c
            sg = jax.nn.sigmoid(gate)
            dyv = dy_ref[pl.ds(r0, rows), :].astype(F32)
            dcs = (dyv * val * (sg * (1.0 + gate * (1.0 - sg))), dyv * (gate * sg))
            new_nxt = []
            for half in range(2):
                dc, w = dcs[half], cw_ref[half]
                dcb_ref[half] += jnp.sum(dc, axis=0, keepdims=True)
                for tap in range(CONV_TAPS):
                    dcw_ref[half, tap:tap + 1, :] += jnp.sum(dc * taps[half][tap], axis=0, keepdims=True)
                da = w[2:3] * dc + w[1:2] * _shift_up(dc, nxt[half], 1) + w[0:1] * _shift_up(dc, nxt[half], 2)
                da_ref[half, pl.ds(r0, rows), :] = da.astype(BF)
                new_nxt.append(dc[:8])
            return tuple(new_nxt)

        zeros = jnp.zeros((8, tc), F32)
        lax.fori_loop(0, n_steps, step, (zeros, zeros))

    col = lambda *lead: pl.BlockSpec((*lead, tc), lambda j: (0,) * len(lead) + (j,))
    return pl.pallas_call(
        body, name=name, grid=(f // tc,), in_specs=[col(2, t), col(2, CONV_TAPS), col(2, 1), col(t)],
        out_specs=[col(2, t), col(2, CONV_TAPS), col(2, 1)],
        out_shape=[jax.ShapeDtypeStruct((2, t, f), BF), jax.ShapeDtypeStruct((2, CONV_TAPS, f), F32),
                   jax.ShapeDtypeStruct((2, 1, f), F32)],
        compiler_params=_params(("parallel",), 40 * _MIB),
    )(a3, cw, cb, dy)


_ATT_BLOCK = 256


def _split_dot(x, tri):
    hi = x.astype(BF)
    lo = (x - hi.astype(F32)).astype(BF)
    return jnp.dot(hi, tri, preferred_element_type=F32) + jnp.dot(lo, tri, preferred_element_type=F32)


def _logits(qb, kb, diagonal):
    z = lax.dot_general(qb, kb, _NT, preferred_element_type=F32) * (1.0 / math.sqrt(GROUP))
    lb = jnp.minimum(z, 0.0) - jnp.log(1.0 + jnp.exp(-jnp.abs(z)))
    if not diagonal:
        return lb, lb - z, None
    n = z.shape[0]
    mask = lax.broadcasted_iota(jnp.int32, (n, n), 1) < lax.broadcasted_iota(jnp.int32, (n, n), 0)
    return lb, jnp.where(mask, lb - z, 0.0), mask


def _attn_fwd(name, q, k, v):
    t, hd = q.shape
    blk = min(_ATT_BLOCK, t)

    def body(q_ref, k_ref, v_ref, o_ref, rest_ref, first_ref):
        h, i = pl.program_id(0), pl.program_id(1)
        qb = q_ref[...]
        ri = lax.broadcasted_iota(jnp.int32, (blk, blk), 0)
        ci = lax.broadcasted_iota(jnp.int32, (blk, blk), 1)
        tri = (ri >= ci).astype(BF)

        def tile(j, right, acc, diagonal):
            k0 = pl.multiple_of(j * blk, blk)
            lb, l1m, mask = _logits(qb, k_ref[pl.ds(k0, blk), :], diagonal)
            incl = _split_dot(l1m, tri)
            a = jnp.exp(lb + (incl - l1m + right))
            if diagonal:
                a = jnp.where(mask, a, 0.0)
            acc = acc + jnp.dot(a.astype(BF), v_ref[pl.ds(k0, blk), :], preferred_element_type=F32)
            return right + incl[:, 0:1], acc

        right, acc = tile(i, jnp.zeros((blk, 1), F32), jnp.zeros((blk, GROUP), F32), True)

        def more(carry):
            j, right, _ = carry
            return jnp.logical_and(j >= 0, jnp.max(right) > EXP_FLOOR)

        def step(carry):
            j, right, acc = carry
            right, acc = tile(j, right, acc, False)
            return j - 1, right, acc

        j, right, acc = lax.while_loop(more, step, (i - 1, right, acc))
        o_ref[...] = acc.astype(BF)
        rest_ref[...] = jnp.broadcast_to(right, (blk, GROUP))
        first_ref[h, i] = (j + 1).astype(F32)

    qspec = pl.BlockSpec((blk, GROUP), lambda h, i: (i, h))
    kvspec = pl.BlockSpec((t, GROUP), lambda h, i: (0, h))
    return pl.pallas_call(
        body, name=name, grid=(hd // GROUP, t // blk), in_specs=[qspec, kvspec, kvspec],
        out_specs=[qspec, qspec, pl.BlockSpec(memory_space=pltpu.SMEM)],
        out_shape=[jax.ShapeDtypeStruct((t, hd), BF), jax.ShapeDtypeStruct((t, hd), F32),
                   jax.ShapeDtypeStruct((hd // GROUP, t // blk), F32)],
        compiler_params=_params(("arbitrary", "arbitrary"), 32 * _MIB),
    )(q, k, v)


def _attn_bwd(name, q, k, v, rest, first, do):
    t, hd = q.shape
    blk = min(_ATT_BLOCK, t)
    nq = t // blk
    scale = 1.0 / math.sqrt(GROUP)

    def body(first_ref, q_ref, k_ref, v_ref, rest_ref, do_ref, dq_ref, dk_ref, dv_ref, dk_acc, dv_acc):
        h, i = pl.program_id(0), pl.program_id(1)

        @pl.when(i == 0)
        def _():
            dk_acc[...] = jnp.zeros_like(dk_acc)
            dv_acc[...] = jnp.zeros_like(dv_acc)

        qb, dob = q_ref[...], do_ref[...]
        total = rest_ref[:, 0:1]
        ri = lax.broadcasted_iota(jnp.int32, (blk, blk), 0)
        ci = lax.broadcasted_iota(jnp.int32, (blk, blk), 1)
        tri = (ri <= ci).astype(BF)

        def tile(j, left, gleft, dq, diagonal):
            k0 = pl.multiple_of(j * blk, blk)
            kb, vb = k_ref[pl.ds(k0, blk), :], v_ref[pl.ds(k0, blk), :]
            lb, l1m, mask = _logits(qb, kb, diagonal)
            pre = _split_dot(l1m, tri)
            a = jnp.exp(lb + (total - left - pre))
            if diagonal:
                a = jnp.where(mask, a, 0.0)
            g = a * lax.dot_general(dob, vb, _NT, preferred_element_type=F32)
            gpre = _split_dot(g, tri)
            beta = jnp.exp(lb)
            dz = (g * (1.0 - beta) - (gleft + gpre - g) * beta) * scale
            if diagonal:
                dz = jnp.where(mask, dz, 0.0)
            dzb, ab = dz.astype(BF), a.astype(BF)
            dk_acc[pl.ds(k0, blk), :] += lax.dot_general(dzb, qb, _TN, preferred_element_type=F32)
            dv_acc[pl.ds(k0, blk), :] += lax.dot_general(ab, dob, _TN, preferred_element_type=F32)
            dq = dq + jnp.dot(dzb, kb, preferred_element_type=F32)
            return left + pre[:, blk - 1:blk], gleft + gpre[:, blk - 1:blk], dq

        zero = jnp.zeros((blk, 1), F32)
        first_block = jnp.clip(first_ref[h, i].astype(jnp.int32), 0, i)
        carry = lax.fori_loop(first_block, i, lambda j, c: tile(j, *c, False), (zero, zero, jnp.zeros((blk, GROUP), F32)))
        _, _, dq = tile(i, *carry, True)
        dq_ref[...] = dq.astype(BF)

        @pl.when(i == nq - 1)
        def _():
            dk_ref[...] = dk_acc[...].astype(BF)
            dv_ref[...] = dv_acc[...].astype(BF)

    qspec = pl.BlockSpec((blk, GROUP), lambda h, i: (i, h))
    kvspec = pl.BlockSpec((t, GROUP), lambda h, i: (0, h))
    return pl.pallas_call(
        body, name=name, grid=(hd // GROUP, nq),
        in_specs=[pl.BlockSpec(memory_space=pltpu.SMEM), qspec, kvspec, kvspec, qspec, qspec],
        out_specs=[qspec, kvspec, kvspec], out_shape=[jax.ShapeDtypeStruct((t, hd), BF)] * 3,
        scratch_shapes=[pltpu.VMEM((t, GROUP), F32), pltpu.VMEM((t, GROUP), F32)],
        compiler_params=_params(("arbitrary", "arbitrary"), 40 * _MIB),
    )(first, q, k, v, rest, do)


def _adamw_math(w, g, m, v):
    m = ADAM_B1 * m + (1.0 - ADAM_B1) * g
    v = ADAM_B2 * v + (1.0 - ADAM_B2) * (g * g)
    m_hat = m / (1.0 - ADAM_B1 ** ADAM_STEP)
    v_hat = v / (1.0 - ADAM_B2 ** ADAM_STEP)
    return -ADAM_LR * (m_hat / (jnp.sqrt(v_hat) + ADAM_EPS) + ADAM_WD * w), m, v


def _sum_adamw(name, parts, w, m, v):
    _, r, c = parts.shape
    tr = r if r * c <= 512 * 1024 else _tile_rows(r, max(8, (512 * 1024 // c) // 8 * 8))

    def body(p_ref, w_ref, m_ref, v_ref, g_ref, d_ref, nm_ref, nv_ref):
        g = p_ref[0].astype(F32)
        for dev in range(1, N_DEV):
            g = g + p_ref[dev].astype(F32)
        g_ref[...] = g
        d_ref[...], nm_ref[...], nv_ref[...] = _adamw_math(w_ref[...], g, m_ref[...], v_ref[...])

    row = pl.BlockSpec((tr, c), lambda i: (i, 0))
    return pl.pallas_call(
        body, name=name, grid=(r // tr,),
        in_specs=[pl.BlockSpec((N_DEV, tr, c), lambda i: (0, i, 0)), row, row, row], out_specs=[row] * 4,
        out_shape=[jax.ShapeDtypeStruct((r, c), F32)] * 4, compiler_params=_params(("parallel",), 40 * _MIB),
    )(parts, w, m, v)


def _tile_rows(r, pref):
    t = min(r, pref)
    while r % t or t % 8:
        t -= 1
    return t


def _place():
    x, y, c = lax.axis_index("x"), lax.axis_index("y"), lax.axis_index("c")
    return x, y, c, 4 * x + 2 * y + c


def _flip(x, y, c, k):
    return (1 - x if k & 4 else x, 1 - y if k & 2 else y, 1 - c if k & 1 else c)


def _all_gather(name, shards):
    n = len(shards)

    def body(*refs):
        ins, outs = refs[:n], refs[n:2 * n]
        send_sems, recv_sems, local_sems = refs[2 * n:]
        x, y, c, _ = _place()
        me, sibling = (x, y, c), (x, y, 1 - c)
        chips = [(1 - x, y), (x, 1 - y), (1 - x, 1 - y)]

        def copy(t, sem, block, to, src=None):
            slot = outs[t].at[4 * block[0] + 2 * block[1] + block[2]]
            return pltpu.make_async_remote_copy(
                src_ref=slot if src is None else src, dst_ref=slot, send_sem=send_sems.at[t, sem],
                recv_sem=recv_sems.at[t, sem], device_id=to, device_id_type=MESH)

        local, sent = [], []
        for t in range(n):
            local.append(pltpu.make_async_copy(ins[t], outs[t].at[4 * x + 2 * y + c], local_sems.at[t]))
            local[-1].start()
            first = [copy(t, 0, me, sibling, src=ins[t])]
            first += [copy(t, 1 + j, me, (*chip, c), src=ins[t]) for j, chip in enumerate(chips)]
            for cp in first:
                cp.start()
            sent += first
        for t in range(n):
            for j, chip in enumerate(chips):
                copy(t, 1 + j, (*chip, c), me).wait_recv()
                sent.append(copy(t, 4 + j, (*chip, c), sibling))
                sent[-1].start()
        for t in range(n):
            copy(t, 0, sibling, me).wait_recv()
            for j, chip in enumerate(chips):
                copy(t, 4 + j, (*chip, 1 - c), me).wait_recv()
        for cp in sent:
            cp.wait_send()
        for cp in local:
            cp.wait()

    any_spec = pl.BlockSpec(memory_space=pl.ANY)
    return pl.pallas_call(
        body, name=name, in_specs=[any_spec] * n, out_specs=[any_spec] * n,
        out_shape=[jax.ShapeDtypeStruct((N_DEV, *s.shape), s.dtype) for s in shards],
        scratch_shapes=[pltpu.SemaphoreType.DMA((n, 7)), pltpu.SemaphoreType.DMA((n, 7)), pltpu.SemaphoreType.DMA((n,))],
        compiler_params=pltpu.CompilerParams(has_side_effects=True),
    )(*shards)


def _scatter_partials(name, partials):
    n = len(partials)

    def body(*refs):
        ins, outs = refs[:n], refs[n:2 * n]
        send_sems, recv_sems, local_sems = refs[2 * n:]
        x, y, c, me = _place()

        def copy(t, k):
            peer = _flip(x, y, c, k)
            return pltpu.make_async_remote_copy(
                src_ref=ins[t].at[4 * peer[0] + 2 * peer[1] + peer[2]], dst_ref=outs[t].at[me],
                send_sem=send_sems.at[t, k - 1], recv_sem=recv_sems.at[t, k - 1], device_id=peer, device_id_type=MESH)

        for t in range(n):
            mine = pltpu.make_async_copy(ins[t].at[me], outs[t].at[me], local_sems.at[t])
            mine.start()
            for k in range(1, N_DEV):
                copy(t, k).start()
            if t > 0:
                _drain(t - 1, copy, ins, outs, local_sems, me)
        _drain(n - 1, copy, ins, outs, local_sems, me)

    any_spec = pl.BlockSpec(memory_space=pl.ANY)
    return pl.pallas_call(
        body, name=name, in_specs=[any_spec] * n, out_specs=[any_spec] * n,
        out_shape=[jax.ShapeDtypeStruct(p.shape, p.dtype) for p in partials],
        scratch_shapes=[pltpu.SemaphoreType.DMA((n, 7)), pltpu.SemaphoreType.DMA((n, 7)), pltpu.SemaphoreType.DMA((n,))],
        compiler_params=pltpu.CompilerParams(has_side_effects=True),
    )(*partials)


def _drain(t, copy, ins, outs, local_sems, me):
    for k in range(1, N_DEV):
        copy(t, k).wait()
    pltpu.make_async_copy(ins[t].at[me], outs[t].at[me], local_sems.at[t]).wait()


def _all_reduce_small(name, groups):
    c = groups[0][0].shape[1]
    parts, offsets, starts, r = [], [], [], 0
    for group in groups:
        starts.append(r)
        for p in group:
            parts.append(p)
            offsets.append(r)
            r += p.shape[0]
        r = -(-r // 8) * 8
    n = len(parts)

    def body(*refs):
        out_ref, slots, send_sems, recv_sems = refs[n:]
        x, y, c_, me = _place()
        slots[me] = jnp.zeros((r, c), F32)
        for p_ref, off in zip(refs[:n], offsets):
            slots[me, off:off + p_ref.shape[0], :] = p_ref[...]

        def copy(k):
            return pltpu.make_async_remote_copy(
                src_ref=slots.at[me], dst_ref=slots.at[me], send_sem=send_sems.at[k - 1], recv_sem=recv_sems.at[k - 1],
                device_id=_flip(x, y, c_, k), device_id_type=MESH)

        for k in range(1, N_DEV):
            copy(k).start()
        for k in range(1, N_DEV):
            copy(k).wait()
        total = slots[0]
        for dev in range(1, N_DEV):
            total = total + slots[dev]
        out_ref[...] = total

    vm = pl.BlockSpec(memory_space=pltpu.VMEM)
    summed = pl.pallas_call(
        body, name=name, in_specs=[vm] * n, out_specs=vm, out_shape=jax.ShapeDtypeStruct((r, c), F32),
        scratch_shapes=[pltpu.VMEM((N_DEV, r, c), F32), pltpu.SemaphoreType.DMA((7,)), pltpu.SemaphoreType.DMA((7,))],
        compiler_params=_params(None, (N_DEV + 6) * r * c * 4 + 8 * _MIB),
    )(*parts)
    return summed, starts


def _adamw(name, g, w, m, v):
    cols = w.shape[-1]
    flat = lambda a: a.reshape(-1, cols)

    def body(g_ref, w_ref, m_ref, v_ref, d_ref, nm_ref, nv_ref):
        d_ref[...], nm_ref[...], nv_ref[...] = _adamw_math(w_ref[...], g_ref[...], m_ref[...], v_ref[...])

    outs = pl.pallas_call(body, name=name, out_shape=[jax.ShapeDtypeStruct(flat(w).shape, F32)] * 3)(
        flat(g), flat(w), flat(m), flat(v))
    return [o.reshape(w.shape) for o in outs]


def kernel(x, pre_mix_g, post_mix_g, pre_ffn_g, post_ffn_g, a_w_in, a_v_norm_g, a_w_spatial, a_b_spatial, a_w_out, kv_norm_g, w_k, w_v, b_w_q, b_w_o, ffn_w_up, ffn_conv_w, ffn_conv_b, ffn_w_down, loss_target, m_pre_mix_g, m_post_mix_g, m_pre_ffn_g, m_post_ffn_g, m_a_w_in, m_a_v_norm_g, m_a_w_spatial, m_a_b_spatial, m_a_w_out, m_kv_norm_g, m_w_k, m_w_v, m_b_w_q, m_b_w_o, m_ffn_w_up, m_ffn_conv_w, m_ffn_conv_b, m_ffn_w_down, v_pre_mix_g, v_post_mix_g, v_pre_ffn_g, v_post_ffn_g, v_a_w_in, v_a_v_norm_g, v_a_w_spatial, v_a_b_spatial, v_a_w_out, v_kv_norm_g, v_w_k, v_w_v, v_b_w_q, v_b_w_o, v_ffn_w_up, v_ffn_conv_w, v_ffn_conv_b, v_ffn_w_down):
    t, d = x.shape[1], x.shape[2]
    f = ffn_w_down.shape[1] * N_DEV
    ng = d // GROUP
    me = 4 * lax.axis_index("x") + 2 * lax.axis_index("y") + lax.axis_index("c")
    x2, target = x.reshape(t, d), loss_target.reshape(t, d)

    gathered = _all_gather("gather_weights", [
        a_w_in[0].astype(BF), a_w_out[0].astype(BF), w_k.astype(BF), w_v.astype(BF), b_w_q[0].astype(BF),
        b_w_o[0].astype(BF), ffn_w_up[0].astype(BF), ffn_w_up[1].astype(BF), ffn_w_down[0].astype(BF),
        ffn_w_down[1].astype(BF), ffn_conv_w.reshape(2 * CONV_TAPS, -1), a_v_norm_g])
    g_in, g_out, g_k, g_v, g_q, g_o, g_up0, g_up1, g_dn0, g_dn1, g_cw, g_vg = gathered
    w_out_f, w_k_f, w_v_f, w_q_f, w_o_f = (g.reshape(d, d) for g in (g_out, g_k, g_v, g_q, g_o))
    g_up = (g_up0, g_up1)
    w_dn_f = (g_dn0.reshape(f, d), g_dn1.reshape(f, d))
    cw_full = jnp.transpose(g_cw.reshape(N_DEV, 2, CONV_TAPS, -1), (1, 2, 0, 3)).reshape(2, CONV_TAPS, 2, f)
    cw_l = [jnp.transpose(cw_full[l], (1, 0, 2)) for l in range(2)]
    cb_l = [ffn_conv_b[l].reshape(2, 1, f) for l in range(2)]
    vg_full = g_vg.reshape(1, d)
    w_s = a_w_spatial[0]
    w_st = jnp.swapaxes(w_s, 1, 2)
    b_st = a_b_spatial[0].T

    def ffn_fwd(l, h_in, fn):
        a3 = _mm_nn_blk(f"ffn{l}_up", fn, g_up[l], F32, halves=True)
        yv = _conv_fwd(f"ffn{l}_conv", a3, cw_l[l], cb_l[l])
        fo = _mm_nn(f"ffn{l}_down", yv, w_dn_f[l], F32)
        return a3, yv, fo

    (hn0,) = _norms_fwd("pre_mix0", x2, [pre_mix_g[0]])
    p0 = _mm_nn_blk("sgu_in", hn0, g_in, F32)
    sg = _sgu_fwd("sgu", p0, vg_full, w_s, b_st)
    mix0 = _mm_nn("sgu_out", sg, w_out_f, F32)
    h1, fn0 = _resid_norms("post_mix0", x2, mix0, post_mix_g[0], [pre_ffn_g[0]])
    a3_0, y0, f0 = ffn_fwd(0, h1, fn0)
    h2, hn1, kvn = _resid_norms("post_ffn0", h1, f0, post_ffn_g[0], [pre_mix_g[1], kv_norm_g])
    q = _mm_nn("attn_q", hn1, w_q_f, BF)
    kk = _mm_nn("attn_k", kvn, w_k_f, BF)
    vv = _mm_nn("attn_v", kvn, w_v_f, BF)
    att, rest, first = _attn_fwd("attn", q, kk, vv)
    mix1 = _mm_nn("attn_o", att, w_o_f, F32)
    h3, fn1 = _resid_norms("post_mix1", h2, mix1, post_mix_g[1], [pre_ffn_g[1]])
    a3_1, y1, f1 = ffn_fwd(1, h3, fn1)
    dh, loss_part = _resid_loss("loss", h3, f1, post_ffn_g[1], target)
    loss = lax.psum(loss_part[0, 0], ("x", "y", "c"))

    def ffn_bwd(l, dh_out, h_in, fn, a3, yv, fo):
        dfo, d_post = _norm_bwd(f"post_ffn{l}_bwd", fo, post_ffn_g[l], dh_out, out_dtype=BF)
        dy = _mm_nt(f"ffn{l}_down_dx", dfo, w_dn_f[l], BF)
        dw_dn = _mm_tn(f"ffn{l}_down_dw", yv, dfo)
        da3, dcw, dcb = _conv_bwd(f"ffn{l}_conv_bwd", a3, cw_l[l], cb_l[l], dy)
        dfn = _mm_nt_blk(f"ffn{l}_up_dx", da3, g_up[l], F32, halves=True)
        dw_up = _mm_tn_blk(f"ffn{l}_up_dw", fn, da3, halves=True)
        dh_in, d_pre = _norm_bwd(f"pre_ffn{l}_bwd", h_in, pre_ffn_g[l], dfn, res=dh_out)
        return dh_in, d_post, d_pre, dw_dn, dw_up, dcw, dcb

    dh3, d_post_ffn1, d_pre_ffn1, dw_dn1, dw_up1, dcw1, dcb1 = ffn_bwd(1, dh, h3, fn1, a3_1, y1, f1)
    dmix1, d_post_mix1 = _norm_bwd("post_mix1_bwd", mix1, post_mix_g[1], dh3, out_dtype=BF)
    datt = _mm_nt("attn_o_dx", dmix1, w_o_f, BF)
    dw_o = _mm_tn("attn_o_dw", att, dmix1)
    dq, dk, dv = _attn_bwd("attn_bwd", q, kk, vv, rest, first, datt)
    dhn1 = _mm_nt("attn_q_dx", dq, w_q_f, F32)
    dw_q = _mm_tn("attn_q_dw", hn1, dq)
    dkvn = _mm_nt("attn_v_dx", dv, w_v_f, F32, add=_mm_nt("attn_k_dx", dk, w_k_f, F32))
    dw_k = _mm_tn("attn_k_dw", kvn, dk)
    dw_v = _mm_tn("attn_v_dw", kvn, dv)
    dh2a, d_pre_mix1 = _norm_bwd("pre_mix1_bwd", h2, pre_mix_g[1], dhn1, res=dh3)
    dh2, d_kv = _norm_bwd("kv_norm_bwd", h2, kv_norm_g, dkvn, res=dh2a)
    dh1, d_post_ffn0, d_pre_ffn0, dw_dn0, dw_up0, dcw0, dcb0 = ffn_bwd(0, dh2, h1, fn0, a3_0, y0, f0)
    dmix0, d_post_mix0 = _norm_bwd("post_mix0_bwd", mix0, post_mix_g[0], dh1, out_dtype=BF)
    dsg = _mm_nt("sgu_out_dx", dmix0, w_out_f, BF)
    dw_out = _mm_tn("sgu_out_dw", sg, dmix0)
    dp0, d_vg, d_ws, d_bst = _sgu_bwd("sgu_bwd", p0, vg_full, w_s, w_st, b_st, dsg)
    dhn0 = _mm_nt_blk("sgu_in_dx", dp0, g_in, F32)
    dw_in = _mm_tn_blk("sgu_in_dw", hn0, dp0)
    grad_x, d_pre_mix0 = _norm_bwd("pre_mix0_bwd", x2, pre_mix_g[0], dhn0, res=dh1)

    def conv_w_grad(dcw):
        return jnp.transpose(dcw, (1, 0, 2)).reshape(CONV_TAPS, 2 * f)

    small = [
        ([d_pre_mix0, d_pre_mix1], (2, d)), ([d_post_mix0, d_post_mix1], (2, d)),
        ([d_pre_ffn0, d_pre_ffn1], (2, d)), ([d_post_ffn0, d_post_ffn1], (2, d)),
        ([d_kv], (d,)), ([d_vg], (1, d)), ([d_bst.T], (1, ng, GROUP)), ([d_ws], (1, ng, GROUP, GROUP)),
        ([dcb0, dcb1], (2, 2 * f)), ([conv_w_grad(dcw0), conv_w_grad(dcw1)], (2, CONV_TAPS, 2 * f)),
    ]
    width = V7X_LANES * math.gcd(d // V7X_LANES, 2 * f // V7X_LANES)
    summed, offsets = _all_reduce_small("reduce_small", [[a.reshape(-1, width) for a in group] for group, _ in small])
    full = [summed[off:off + math.prod(shape) // width].reshape(shape) for off, (_, shape) in zip(offsets, small)]
    g_pre_mix, g_post_mix, g_pre_ffn, g_post_ffn, g_kv, g_vgain, g_bs, g_ws, g_cb, g_cwf = full
    cw_w = 2 * f // N_DEV
    g_vgain = lax.dynamic_slice_in_dim(g_vgain, me * (d // N_DEV), d // N_DEV, axis=1)
    g_cwf = lax.dynamic_slice_in_dim(g_cwf, me * cw_w, cw_w, axis=2)

    big = [dw_in, dw_out.reshape(N_DEV, -1, d), dw_k.reshape(N_DEV, -1, d), dw_v.reshape(N_DEV, -1, d),
           dw_q.reshape(N_DEV, -1, d), dw_o.reshape(N_DEV, -1, d), dw_up0, dw_up1,
           dw_dn0.reshape(N_DEV, -1, d), dw_dn1.reshape(N_DEV, -1, d)]
    parts = _scatter_partials("scatter_grads", big)

    def big_update(name, part, w, m, v):
        return _sum_adamw(name, part, w.reshape(part.shape[1:]), m.reshape(part.shape[1:]), v.reshape(part.shape[1:]))

    def small_update(name, g, w, m, v):
        return [g] + _adamw(name, g, w, m, v)

    def stacked(name, part0, part1, w, m, v):
        outs = [big_update(f"{name}{l}", p, w[l], m[l], v[l]) for l, p in enumerate((part0, part1))]
        return [jnp.stack([a, b]).reshape(w.shape) for a, b in zip(*outs)]

    def single(name, part, w, m, v):
        return [o.reshape(w.shape) for o in big_update(name, part, w, m, v)]

    results = {
        "pre_mix_g": small_update("adam_pre_mix", g_pre_mix, pre_mix_g, m_pre_mix_g, v_pre_mix_g),
        "post_mix_g": small_update("adam_post_mix", g_post_mix, post_mix_g, m_post_mix_g, v_post_mix_g),
        "pre_ffn_g": small_update("adam_pre_ffn", g_pre_ffn, pre_ffn_g, m_pre_ffn_g, v_pre_ffn_g),
        "post_ffn_g": small_update("adam_post_ffn", g_post_ffn, post_ffn_g, m_post_ffn_g, v_post_ffn_g),
        "a_w_in": single("adam_a_w_in", parts[0], a_w_in, m_a_w_in, v_a_w_in),
        "a_v_norm_g": small_update("adam_a_v_norm", g_vgain, a_v_norm_g, m_a_v_norm_g, v_a_v_norm_g),
        "a_w_spatial": small_update("adam_a_w_spatial", g_ws, a_w_spatial, m_a_w_spatial, v_a_w_spatial),
        "a_b_spatial": small_update("adam_a_b_spatial", g_bs, a_b_spatial, m_a_b_spatial, v_a_b_spatial),
        "a_w_out": single("adam_a_w_out", parts[1], a_w_out, m_a_w_out, v_a_w_out),
        "kv_norm_g": small_update("adam_kv_norm", g_kv, kv_norm_g, m_kv_norm_g, v_kv_norm_g),
        "w_k": single("adam_w_k", parts[2], w_k, m_w_k, v_w_k),
        "w_v": single("adam_w_v", parts[3], w_v, m_w_v, v_w_v),
        "b_w_q": single("adam_b_w_q", parts[4], b_w_q, m_b_w_q, v_b_w_q),
        "b_w_o": single("adam_b_w_o", parts[5], b_w_o, m_b_w_o, v_b_w_o),
        "ffn_w_up": stacked("adam_ffn_w_up", parts[6], parts[7], ffn_w_up, m_ffn_w_up, v_ffn_w_up),
        "ffn_conv_w": small_update("adam_ffn_conv_w", g_cwf, ffn_conv_w, m_ffn_conv_w, v_ffn_conv_w),
        "ffn_conv_b": small_update("adam_ffn_conv_b", g_cb, ffn_conv_b, m_ffn_conv_b, v_ffn_conv_b),
        "ffn_w_down": stacked("adam_ffn_w_down", parts[8], parts[9], ffn_w_down, m_ffn_w_down, v_ffn_w_down),
    }
    order = ["pre_mix_g", "post_mix_g", "pre_ffn_g", "post_ffn_g", "a_w_in", "a_v_norm_g", "a_w_spatial", "a_b_spatial",
             "a_w_out", "kv_norm_g", "w_k", "w_v", "b_w_q", "b_w_o", "ffn_w_up", "ffn_conv_w", "ffn_conv_b", "ffn_w_down"]
    outs = [loss, grad_x.reshape(x.shape)]
    for idx in range(4):
        outs += [results[n][idx] for n in order]
    return tuple(outs)
```

```python
import math
from typing import NamedTuple, Optional

import jax
import jax.numpy as jnp
from jax import lax
from jax.experimental import pallas as pl
from jax.experimental.pallas import tpu as pltpu

F32 = jnp.float32
BF = jnp.bfloat16
MESH = pl.DeviceIdType.MESH

N_DEV = 8
NORM_EPS = 1e-6
GROUP = 128
CONV_TAPS = 3
ADAM_LR, ADAM_B1, ADAM_B2, ADAM_EPS, ADAM_WD, ADAM_STEP = 0.001, 0.9, 0.999, 1e-08, 0.01, 10
EXP_FLOOR = -104.0

V7X_LANES = 128
V7X_VMEM_BYTES = 64 * 1024 * 1024
_MIB = 1024 * 1024

_NN = (((1,), (0,)), ((), ()))
_NT = (((1,), (1,)), ((), ()))
_TN = (((0,), (0,)), ((), ()))


def _tile(n, pref):
    if n <= pref:
        return n
    t = (pref // V7X_LANES) * V7X_LANES
    while t > V7X_LANES and n % t:
        t -= V7X_LANES
    assert n % t == 0, (n, pref)
    return t


def _nbytes(shape, dtype):
    return math.prod(shape) * jnp.dtype(dtype).itemsize


def _params(sem=None, vmem=None):
    kw = {}
    if sem is not None:
        kw["dimension_semantics"] = sem
    if vmem is not None:
        kw["vmem_limit_bytes"] = int(min(max(vmem, 16 * _MIB), V7X_VMEM_BYTES - 8 * _MIB))
    return pltpu.CompilerParams(**kw)


def _place():
    x, y, c = lax.axis_index("x"), lax.axis_index("y"), lax.axis_index("c")
    return x, y, c, 4 * x + 2 * y + c


def _flip(x, y, c, k):
    return (1 - x if k & 4 else x, 1 - y if k & 2 else y, 1 - c if k & 1 else c)


class _Carry(NamedTuple):
    gather: bool
    src: jax.Array
    dst: Optional[jax.Array] = None
    lo: int = 0
    hi: Optional[int] = None


def _gather(src):
    return _Carry(True, src)


def _scatter(src, lo=0, hi=None, dst=None):
    return _Carry(False, src, dst, lo, src.shape[1] if hi is None else hi)


def _carry_phases(carries, srcs, dsts, send_sems, recv_sems, local_sems):
    x, y, c, me = _place()
    here, sibling = (x, y, c), (x, y, 1 - c)
    chips = [(1 - x, y), (x, 1 - y), (1 - x, 1 - y)]

    def block_copy(u, sem, block, to, from_src=False):
        slot = dsts[u].at[4 * block[0] + 2 * block[1] + block[2]]
        return pltpu.make_async_remote_copy(
            src_ref=srcs[u] if from_src else slot, dst_ref=slot, send_sem=send_sems.at[u, sem],
            recv_sem=recv_sems.at[u, sem], device_id=to, device_id_type=MESH)

    def partial_copy(u, k):
        rows = pl.ds(carries[u].lo, carries[u].hi - carries[u].lo)
        peer = _flip(x, y, c, k)
        return pltpu.make_async_remote_copy(
            src_ref=srcs[u].at[4 * peer[0] + 2 * peer[1] + peer[2], rows], dst_ref=dsts[u].at[me, rows],
            send_sem=send_sems.at[u, k - 1], recv_sem=recv_sems.at[u, k - 1], device_id=peer, device_id_type=MESH)

    def local_copy(u):
        if carries[u].gather:
            return pltpu.make_async_copy(srcs[u], dsts[u].at[me], local_sems.at[u])
        rows = pl.ds(carries[u].lo, carries[u].hi - carries[u].lo)
        return pltpu.make_async_copy(srcs[u].at[me, rows], dsts[u].at[me, rows], local_sems.at[u])

    def first():
        for u, cr in enumerate(carries):
            local_copy(u).start()
            if cr.gather:
                block_copy(u, 0, here, sibling, from_src=True).start()
                for j, chip in enumerate(chips):
                    block_copy(u, 1 + j, here, (*chip, c), from_src=True).start()
            else:
                for k in range(1, N_DEV):
                    partial_copy(u, k).start()

    def middle():
        for u, cr in enumerate(carries):
            if cr.gather:
                for j, chip in enumerate(chips):
                    block_copy(u, 1 + j, (*chip, c), here).wait_recv()
                    block_copy(u, 4 + j, (*chip, c), sibling).start()

    def last():
        for u, cr in enumerate(carries):
            if cr.gather:
                block_copy(u, 0, sibling, here).wait_recv()
                for j, chip in enumerate(chips):
                    block_copy(u, 4 + j, (*chip, 1 - c), here).wait_recv()
                block_copy(u, 0, here, sibling, from_src=True).wait_send()
                for j, chip in enumerate(chips):
                    block_copy(u, 1 + j, here, (*chip, c), from_src=True).wait_send()
                    block_copy(u, 4 + j, (*chip, c), sibling).wait_send()
            else:
                for k in range(1, N_DEV):
                    partial_copy(u, k).wait()
            local_copy(u).wait()

    return first, middle, last


def _call(body, *, name, grid, in_specs, out_specs, out_shape, operands, scratch=(), sem=None, vmem=None,
          carries=(), middle_at=0.6):
    if not carries:
        return pl.pallas_call(
            body, name=name, grid=grid, in_specs=in_specs, out_specs=out_specs, out_shape=out_shape,
            scratch_shapes=list(scratch), compiler_params=_params(sem, vmem))(*operands)
    n_in, n_out, n_scr, nc = len(in_specs), len(out_specs), len(scratch), len(carries)
    given = [u for u, cr in enumerate(carries) if cr.dst is not None]
    steps = math.prod(grid)
    middle_step = min(steps - 1, int(steps * middle_at))

    def wrapped(*refs):
        ins, srcs = refs[:n_in], refs[n_in:n_in + nc]
        at = n_in + nc + len(given)
        outs, dsts = refs[at:at + n_out], refs[at + n_out:at + n_out + nc]
        at += n_out + nc
        scr, (send_sems, recv_sems, local_sems) = refs[at:at + n_scr], refs[at + n_scr:]
        first, middle, last = _carry_phases(carries, srcs, dsts, send_sems, recv_sems, local_sems)
        step = 0
        for axis, size in enumerate(grid):
            step = step * size + pl.program_id(axis)
        pl.when(step == 0)(first)
        body(*ins, *outs, *scr)
        pl.when(step == middle_step)(middle)
        pl.when(step == steps - 1)(last)

    any_spec = pl.BlockSpec(memory_space=pl.ANY)
    dst_shapes = [jax.ShapeDtypeStruct((N_DEV, *cr.src.shape) if cr.gather else cr.src.shape, cr.src.dtype) for cr in carries]
    return pl.pallas_call(
        wrapped, name=name, grid=grid, in_specs=list(in_specs) + [any_spec] * (nc + len(given)),
        out_specs=list(out_specs) + [any_spec] * nc, out_shape=list(out_shape) + dst_shapes,
        input_output_aliases={n_in + nc + g: n_out + u for g, u in enumerate(given)},
        scratch_shapes=list(scratch) + [pltpu.SemaphoreType.DMA((nc, 7)), pltpu.SemaphoreType.DMA((nc, 7)),
                                        pltpu.SemaphoreType.DMA((nc,))],
        compiler_params=_params(("arbitrary",) * len(grid), vmem),
    )(*operands, *[cr.src for cr in carries], *[carries[u].dst for u in given])


def _mm(name, a, b, *, dims, grid, a_blk, a_map, b_blk, b_map, o_blk, o_map, out_shape, out_dtype,
        add=None, add_blk=None, add_map=None, carries=()):
    nk = grid[2]
    acc_shape = tuple(d for d in o_blk if d is not None)

    def body(*refs):
        if add is None:
            a_ref, b_ref, o_ref = refs[:3]
            c_ref, scr = None, refs[3:]
        else:
            a_ref, b_ref, c_ref, o_ref = refs[:4]
            scr = refs[4:]
        part = lax.dot_general(a_ref[...], b_ref[...], dims, preferred_element_type=F32)

        def finish(x):
            if c_ref is not None:
                x = x + c_ref[...].astype(F32)
            o_ref[...] = x.astype(o_ref.dtype)

        if nk == 1:
            finish(part)
        else:
            acc = scr[0]
            k = pl.program_id(2)

            @pl.when(k == 0)
            def _():
                acc[...] = part

            @pl.when(k > 0)
            def _():
                acc[...] += part

            @pl.when(k == nk - 1)
            def _():
                finish(acc[...])

    in_specs = [pl.BlockSpec(a_blk, a_map), pl.BlockSpec(b_blk, b_map)]
    operands = [a, b]
    vmem = 2 * (_nbytes(acc_shape, out_dtype) + _nbytes([d for d in a_blk if d], a.dtype)
                + _nbytes([d for d in b_blk if d], b.dtype)) + 2 * _nbytes(acc_shape, F32)
    if add is not None:
        in_specs.append(pl.BlockSpec(add_blk, add_map))
        operands.append(add)
        vmem += 2 * _nbytes(acc_shape, add.dtype)
    scratch = [pltpu.VMEM(acc_shape, F32)] if nk > 1 else []
    out, *dsts = _call(
        body, name=name, grid=grid, in_specs=in_specs, out_specs=[pl.BlockSpec(o_blk, o_map)],
        out_shape=[jax.ShapeDtypeStruct(out_shape, out_dtype)], operands=operands, scratch=scratch,
        sem=("parallel", "parallel", "arbitrary"), vmem=vmem + 8 * _MIB, carries=carries)
    return (out, dsts) if carries else out


def _mm_nn(name, x, w, out_dtype, carries=()):
    t, kd = x.shape
    n = w.shape[1]
    tm, tn, tk = _tile(t, 1024), _tile(n, 1024), _tile(kd, 1024 if kd > 2048 else 2048)
    return _mm(name, x, w, dims=_NN, grid=(t // tm, n // tn, kd // tk),
               a_blk=(tm, tk), a_map=lambda i, j, k: (i, k), b_blk=(tk, tn), b_map=lambda i, j, k: (k, j),
               o_blk=(tm, tn), o_map=lambda i, j, k: (i, j), out_shape=(t, n), out_dtype=out_dtype, carries=carries)


def _mm_nn_blk(name, x, g, out_dtype, halves=False, carries=()):
    t, kd = x.shape
    cw = g.shape[2]
    tm = _tile(t, 1024)
    if halves:
        o_blk, o_map, out_shape = (None, tm, cw), (lambda i, j, k: (j // 4, i, j % 4)), (2, t, 4 * cw)
    else:
        o_blk, o_map, out_shape = (tm, cw), (lambda i, j, k: (i, j)), (t, N_DEV * cw)
    return _mm(name, x, g, dims=_NN, grid=(t // tm, N_DEV, 1),
               a_blk=(tm, kd), a_map=lambda i, j, k: (i, 0), b_blk=(None, kd, cw), b_map=lambda i, j, k: (j, 0, 0),
               o_blk=o_blk, o_map=o_map, out_shape=out_shape, out_dtype=out_dtype, carries=carries)


def _mm_nt(name, dy, w, out_dtype, add=None, carries=()):
    t, n = dy.shape
    kd = w.shape[0]
    tm, tn = _tile(t, 1024), _tile(kd, 512)
    kw = {}
    if add is not None:
        kw = dict(add=add, add_blk=(tm, tn), add_map=lambda i, j, k: (i, j))
    return _mm(name, dy, w, dims=_NT, grid=(t // tm, kd // tn, 1),
               a_blk=(tm, n), a_map=lambda i, j, k: (i, 0), b_blk=(tn, n), b_map=lambda i, j, k: (j, 0),
               o_blk=(tm, tn), o_map=lambda i, j, k: (i, j), out_shape=(t, kd), out_dtype=out_dtype, carries=carries, **kw)


def _mm_nt_blk(name, dy, g, out_dtype, halves=False, carries=()):
    kd, cw = g.shape[1], g.shape[2]
    t = dy.shape[1] if halves else dy.shape[0]
    tm = _tile(t, 512)
    if halves:
        a_blk, a_map = (None, tm, cw), (lambda i, j, k: (k // 4, i, k % 4))
    else:
        a_blk, a_map = (tm, cw), (lambda i, j, k: (i, k))
    return _mm(name, dy, g, dims=_NT, grid=(t // tm, 1, N_DEV),
               a_blk=a_blk, a_map=a_map, b_blk=(None, kd, cw), b_map=lambda i, j, k: (k, 0, 0),
               o_blk=(tm, kd), o_map=lambda i, j, k: (i, 0), out_shape=(t, kd), out_dtype=out_dtype, carries=carries)


def _mm_tn(name, x, dy, carries=()):
    t, kd = x.shape
    n = dy.shape[1]
    tmx, tk = _tile(kd, 1024), _tile(t, 512)
    return _mm(name, x, dy, dims=_TN, grid=(kd // tmx, 1, t // tk),
               a_blk=(tk, tmx), a_map=lambda i, j, k: (k, i), b_blk=(tk, n), b_map=lambda i, j, k: (k, 0),
               o_blk=(tmx, n), o_map=lambda i, j, k: (i, 0), out_shape=(kd, n), out_dtype=BF, carries=carries)


def _mm_tn_blk(name, x, dy, halves=False, carries=()):
    t, kd = x.shape
    cw = dy.shape[2] // 4 if halves else dy.shape[1] // N_DEV
    tmx, tk = _tile(kd, 2048 if cw <= 512 else 1024), _tile(t, 1024 if cw <= 512 else 512)
    if halves:
        b_blk, b_map = (None, tk, cw), (lambda i, j, k: (j // 4, k, j % 4))
    else:
        b_blk, b_map = (tk, cw), (lambda i, j, k: (k, j))
    return _mm(name, x, dy, dims=_TN, grid=(kd // tmx, N_DEV, t // tk),
               a_blk=(tk, tmx), a_map=lambda i, j, k: (k, i), b_blk=b_blk, b_map=b_map,
               o_blk=(None, tmx, cw), o_map=lambda i, j, k: (j, i, 0), out_shape=(N_DEV, kd, cw), out_dtype=BF,
               carries=carries)


def _rms(x, g):
    r = lax.rsqrt(jnp.mean(x * x, axis=-1, keepdims=True) + NORM_EPS)
    return x * r * g


def _rms_bwd_math(x, g, dy):
    d = x.shape[-1]
    r = lax.rsqrt(jnp.mean(x * x, axis=-1, keepdims=True) + NORM_EPS)
    xh = x * r
    u = dy * g
    dx = r * u - xh * (jnp.sum(xh * u, axis=-1, keepdims=True) * (r / d))
    return dx, jnp.sum(dy * xh, axis=0, keepdims=True)


def _row_specs(tr, d, n):
    return [pl.BlockSpec((tr, d), lambda i: (i, 0)) for _ in range(n)]


def _vec_specs(d, n):
    return [pl.BlockSpec((1, d), lambda i: (0, 0)) for _ in range(n)]


def _norms_fwd(name, h, gains):
    t, d = h.shape
    tr, ng = _tile(t, 256), len(gains)

    def body(h_ref, *refs):
        x = h_ref[...]
        for g_ref, o_ref in zip(refs[:ng], refs[ng:]):
            o_ref[...] = _rms(x, g_ref[...]).astype(BF)

    return pl.pallas_call(
        body, name=name, grid=(t // tr,), in_specs=_row_specs(tr, d, 1) + _vec_specs(d, ng),
        out_specs=_row_specs(tr, d, ng), out_shape=[jax.ShapeDtypeStruct((t, d), BF)] * ng,
        compiler_params=_params(("parallel",)),
    )(h, *[g.reshape(1, d) for g in gains])


def _resid_norms(name, h, m, g_post, gains):
    t, d = h.shape
    tr, ng = _tile(t, 256), len(gains)

    def body(h_ref, m_ref, gp_ref, *refs):
        hn = h_ref[...] + _rms(m_ref[...], gp_ref[...])
        refs[ng][...] = hn
        for g_ref, o_ref in zip(refs[:ng], refs[ng + 1:]):
            o_ref[...] = _rms(hn, g_ref[...]).astype(BF)

    return pl.pallas_call(
        body, name=name, grid=(t // tr,), in_specs=_row_specs(tr, d, 2) + _vec_specs(d, 1 + ng),
        out_specs=_row_specs(tr, d, 1 + ng),
        out_shape=[jax.ShapeDtypeStruct((t, d), F32)] + [jax.ShapeDtypeStruct((t, d), BF)] * ng,
        compiler_params=_params(("parallel",)),
    )(h, m, g_post.reshape(1, d), *[g.reshape(1, d) for g in gains])


def _resid_loss(name, h, m, g_post, target):
    t, d = h.shape
    tr = _tile(t, 256)

    def body(h_ref, m_ref, gp_ref, t_ref, dy_ref, loss_ref):
        diff = h_ref[...] + _rms(m_ref[...], gp_ref[...]) - t_ref[...]
        dy_ref[...] = diff * (1.0 / d)

        @pl.when(pl.program_id(0) == 0)
        def _():
            loss_ref[...] = jnp.zeros_like(loss_ref)

        per_row = jnp.sum(diff * diff, axis=-1, keepdims=True) * (1.0 / d)
        loss_ref[...] += 0.5 * jnp.sum(per_row, axis=0, keepdims=True)

    return pl.pallas_call(
        body, name=name, grid=(t // tr,),
        in_specs=_row_specs(tr, d, 2) + _vec_specs(d, 1) + _row_specs(tr, d, 1),
        out_specs=[pl.BlockSpec((tr, d), lambda i: (i, 0)), pl.BlockSpec((1, 1), lambda i: (0, 0))],
        out_shape=[jax.ShapeDtypeStruct((t, d), F32), jax.ShapeDtypeStruct((1, 1), F32)],
        compiler_params=_params(("arbitrary",)),
    )(h, m, g_post.reshape(1, d), target)


def _norm_bwd(name, x, g, dy, res=None, out_dtype=F32):
    t, d = x.shape
    tr = _tile(t, 256)
    has_res = res is not None

    def body(x_ref, dy_ref, g_ref, *refs):
        dx_ref, dg_ref = refs[-2:]
        dx, dg = _rms_bwd_math(x_ref[...].astype(F32), g_ref[...], dy_ref[...].astype(F32))
        if has_res:
            dx = dx + refs[0][...]
        dx_ref[...] = dx.astype(dx_ref.dtype)

        @pl.when(pl.program_id(0) == 0)
        def _():
            dg_ref[...] = jnp.zeros_like(dg_ref)

        dg_ref[...] += dg

    ops = [x, dy, g.reshape(1, d)] + ([res] if has_res else [])
    return pl.pallas_call(
        body, name=name, grid=(t // tr,),
        in_specs=_row_specs(tr, d, 2) + _vec_specs(d, 1) + _row_specs(tr, d, int(has_res)),
        out_specs=[pl.BlockSpec((tr, d), lambda i: (i, 0)), pl.BlockSpec((1, d), lambda i: (0, 0))],
        out_shape=[jax.ShapeDtypeStruct((t, d), out_dtype), jax.ShapeDtypeStruct((1, d), F32)],
        compiler_params=_params(("arbitrary",)),
    )(*ops)


_GELU_C = math.sqrt(2.0 / math.pi)
_GELU_A = 0.044715


def _gelu(x):
    return 0.5 * x * (1.0 + jnp.tanh(_GELU_C * (x + _GELU_A * x * x * x)))


def _gelu_and_grad(x):
    th = jnp.tanh(_GELU_C * (x + _GELU_A * x * x * x))
    grad = 0.5 * (1.0 + th) + 0.5 * x * (1.0 - th * th) * (_GELU_C * (1.0 + 3.0 * _GELU_A * x * x))
    return 0.5 * x * (1.0 + th), grad


def _causal(n):
    return lax.broadcasted_iota(jnp.int32, (n, n), 1) <= lax.broadcasted_iota(jnp.int32, (n, n), 0)


def _sgu_fwd(name, p, v_gain, w_s, b_st):
    t, da2 = p.shape
    da = da2 // 2
    ng = da // GROUP

    def body(p_ref, vg_ref, ws_ref, bst_ref, o_ref):
        keep = _causal(GROUP)
        for g in range(ng):
            lo = g * GROUP
            u = _gelu(p_ref[:, lo:lo + GROUP])
            vn = _rms(_gelu(p_ref[:, da + lo:da + lo + GROUP]), vg_ref[:, lo:lo + GROUP])
            w = jnp.where(keep, ws_ref[g], 0.0).astype(BF)
            mixed = jnp.dot(w, vn.astype(BF), preferred_element_type=F32) + bst_ref[:, g:g + 1]
            o_ref[:, lo:lo + GROUP] = (u * mixed).astype(BF)

    return pl.pallas_call(
        body, name=name, grid=(t // GROUP,),
        in_specs=[pl.BlockSpec((GROUP, da2), lambda i: (i, 0)), pl.BlockSpec((1, da), lambda i: (0, 0)),
                  pl.BlockSpec((ng, GROUP, GROUP), lambda i: (0, 0, 0)), pl.BlockSpec((GROUP, ng), lambda i: (0, 0))],
        out_specs=pl.BlockSpec((GROUP, da), lambda i: (i, 0)),
        out_shape=jax.ShapeDtypeStruct((t, da), BF),
        compiler_params=_params(("parallel",)),
    )(p, v_gain, w_s, b_st)


def _sgu_bwd(name, p, v_gain, w_s, w_st, b_st, dout, carries=()):
    t, da2 = p.shape
    da = da2 // 2
    ng = da // GROUP

    def body(p_ref, vg_ref, ws_ref, wst_ref, bst_ref, do_ref, dp_ref, dvg_ref, dws_ref, dbst_ref):
        @pl.when(pl.program_id(0) == 0)
        def _():
            dvg_ref[...] = jnp.zeros_like(dvg_ref)
            dws_ref[...] = jnp.zeros_like(dws_ref)
            dbst_ref[...] = jnp.zeros_like(dbst_ref)

        keep = _causal(GROUP)
        keep_t = lax.broadcasted_iota(jnp.int32, (GROUP, GROUP), 0) <= lax.broadcasted_iota(jnp.int32, (GROUP, GROUP), 1)
        for g in range(ng):
            lo = g * GROUP
            u, du = _gelu_and_grad(p_ref[:, lo:lo + GROUP])
            v, dv_act = _gelu_and_grad(p_ref[:, da + lo:da + lo + GROUP])
            gain = vg_ref[:, lo:lo + GROUP]
            r = lax.rsqrt(jnp.mean(v * v, axis=-1, keepdims=True) + NORM_EPS)
            vh = v * r
            vnb = (vh * gain).astype(BF)
            w = jnp.where(keep, ws_ref[g], 0.0).astype(BF)
            wt = jnp.where(keep_t, wst_ref[g], 0.0).astype(BF)
            mixed = jnp.dot(w, vnb, preferred_element_type=F32) + bst_ref[:, g:g + 1]
            dout_g = do_ref[:, lo:lo + GROUP].astype(F32)
            dmixed = dout_g * u
            dmb = dmixed.astype(BF)
            dbst_ref[:, g:g + 1] += jnp.sum(dmixed, axis=1, keepdims=True)
            dws_ref[g] += jnp.where(keep, lax.dot_general(dmb, vnb, _NT, preferred_element_type=F32), 0.0)
            dvn = jnp.dot(wt, dmb, preferred_element_type=F32)
            dvg_ref[:, lo:lo + GROUP] += jnp.sum(dvn * vh, axis=0, keepdims=True)
            dvh = dvn * gain
            dv = r * dvh - vh * (jnp.sum(vh * dvh, axis=-1, keepdims=True) * (r / GROUP))
            dp_ref[:, lo:lo + GROUP] = (dout_g * mixed * du).astype(BF)
            dp_ref[:, da + lo:da + lo + GROUP] = (dv * dv_act).astype(BF)

    full = lambda *shape: pl.BlockSpec(shape, lambda i: (0,) * len(shape))
    outs = _call(
        body, name=name, grid=(t // GROUP,),
        in_specs=[pl.BlockSpec((GROUP, da2), lambda i: (i, 0)), full(1, da), full(ng, GROUP, GROUP), full(ng, GROUP, GROUP),
                  full(GROUP, ng), pl.BlockSpec((GROUP, da), lambda i: (i, 0))],
        out_specs=[pl.BlockSpec((GROUP, da2), lambda i: (i, 0)), full(1, da), full(ng, GROUP, GROUP), full(GROUP, ng)],
        out_shape=[jax.ShapeDtypeStruct((t, da2), BF), jax.ShapeDtypeStruct((1, da), F32),
                   jax.ShapeDtypeStruct((ng, GROUP, GROUP), F32), jax.ShapeDtypeStruct((GROUP, ng), F32)],
        operands=(p, v_gain, w_s, w_st, b_st, dout), sem=("arbitrary",), carries=carries)
    return outs[:4], outs[4:]


_CONV_ROWS = 256
_CONV_COLS = 128


def _shift_down(x, prev, k):
    top = pltpu.roll(jnp.concatenate([prev, x[:8]], axis=0), k, 0)[8:16]
    return jnp.concatenate([top, pltpu.roll(x, k, 0)[8:]], axis=0)


def _shift_up(x, nxt, k):
    n = x.shape[0]
    bottom = pltpu.roll(jnp.concatenate([x[n - 8:], nxt], axis=0), 16 - k, 0)[0:8]
    return jnp.concatenate([pltpu.roll(x, n - k, 0)[:n - 8], bottom], axis=0)


def _conv_taps(a_ref, half, r0, first):
    x = a_ref[half, pl.ds(r0, _rows(a_ref)), :]
    prev = a_ref[half, pl.ds(jnp.maximum(r0 - 8, 0), 8), :]
    prev = jnp.where(first, 0.0, prev)
    return x, _shift_down(x, prev, 1), _shift_down(x, prev, 2)


def _rows(a_ref):
    return min(_CONV_ROWS, a_ref.shape[1])


def _conv_fwd(name, a3, cw, cb, carries=()):
    _, t, f = a3.shape
    tc, rows = _CONV_COLS, min(_CONV_ROWS, t)

    def body(a_ref, cw_ref, cb_ref, y_ref):
        def step(r, carry):
            r0 = pl.multiple_of(r * rows, rows)
            c = []
            for half in range(2):
                x, x1, x2 = _conv_taps(a_ref, half, r0, r == 0)
                w = cw_ref[half]
                c.append(cb_ref[half] + (w[0:1] * x2 + w[1:2] * x1 + w[2:3] * x))
            y_ref[pl.ds(r0, rows), :] = (c[0] * jax.nn.sigmoid(c[0]) * c[1]).astype(BF)
            return carry

        lax.fori_loop(0, t // rows, step, 0)

    col = lambda *lead: pl.BlockSpec((*lead, tc), lambda j: (0,) * len(lead) + (j,))
    y, *dsts = _call(
        body, name=name, grid=(f // tc,), in_specs=[col(2, t), col(2, CONV_TAPS), col(2, 1)],
        out_specs=[col(t)], out_shape=[jax.ShapeDtypeStruct((t, f), BF)], operands=(a3, cw, cb),
        sem=("parallel",), vmem=40 * _MIB, carries=carries)
    return y, dsts


def _conv_bwd(name, a3, cw, cb, dy, carries=()):
    _, t, f = a3.shape
    tc, rows = _CONV_COLS, min(_CONV_ROWS, t)
    n_steps = t // rows

    def body(a_ref, cw_ref, cb_ref, dy_ref, da_ref, dcw_ref, dcb_ref):
        dcw_ref[...] = jnp.zeros_like(dcw_ref)
        dcb_ref[...] = jnp.zeros_like(dcb_ref)

        def step(s, nxt):
            r = n_steps - 1 - s
            r0 = pl.multiple_of(r * rows, rows)
            taps, c = [], []
            for half in range(2):
                x, x1, x2 = _conv_taps(a_ref, half, r0, r == 0)
                w = cw_ref[half]
                taps.append((x2, x1, x))
                c.append(cb_ref[half] + (w[0:1] * x2 + w[1:2] * x1 + w[2:3] * x))
            gate, val = c
            sg = jax.nn.sigmoid(gate)
            dyv = dy_ref[pl.ds(r0, rows), :].astype(F32)
            dcs = (dyv * val * (sg * (1.0 + gate * (1.0 - sg))), dyv * (gate * sg))
            new_nxt = []
            for half in range(2):
                dc, w = dcs[half], cw_ref[half]
                dcb_ref[half] += jnp.sum(dc, axis=0, keepdims=True)
                for tap in range(CONV_TAPS):
                    dcw_ref[half, tap:tap + 1, :] += jnp.sum(dc * taps[half][tap], axis=0, keepdims=True)
                da = w[2:3] * dc + w[1:2] * _shift_up(dc, nxt[half], 1) + w[0:1] * _shift_up(dc, nxt[half], 2)
                da_ref[half, pl.ds(r0, rows), :] = da.astype(BF)
                new_nxt.append(dc[:8])
            return tuple(new_nxt)

        zeros = jnp.zeros((8, tc), F32)
        lax.fori_loop(0, n_steps, step, (zeros, zeros))

    col = lambda *lead: pl.BlockSpec((*lead, tc), lambda j: (0,) * len(lead) + (j,))
    outs = _call(
        body, name=name, grid=(f // tc,), in_specs=[col(2, t), col(2, CONV_TAPS), col(2, 1), col(t)],
        out_specs=[col(2, t), col(2, CONV_TAPS), col(2, 1)],
        out_shape=[jax.ShapeDtypeStruct((2, t, f), BF), jax.ShapeDtypeStruct((2, CONV_TAPS, f), F32),
                   jax.ShapeDtypeStruct((2, 1, f), F32)],
        operands=(a3, cw, cb, dy), sem=("parallel",), vmem=40 * _MIB, carries=carries)
    return outs[:3], outs[3:]


_ATT_BLOCK = 256


def _split_dot(x, tri):
    hi = x.astype(BF)
    lo = (x - hi.astype(F32)).astype(BF)
    return jnp.dot(hi, tri, preferred_element_type=F32) + jnp.dot(lo, tri, preferred_element_type=F32)


def _logits(qb, kb, diagonal):
    z = lax.dot_general(qb, kb, _NT, preferred_element_type=F32) * (1.0 / math.sqrt(GROUP))
    lb = jnp.minimum(z, 0.0) - jnp.log(1.0 + jnp.exp(-jnp.abs(z)))
    if not diagonal:
        return lb, lb - z, None
    n = z.shape[0]
    mask = lax.broadcasted_iota(jnp.int32, (n, n), 1) < lax.broadcasted_iota(jnp.int32, (n, n), 0)
    return lb, jnp.where(mask, lb - z, 0.0), mask


def _attn_fwd(name, q, k, v, carries=()):
    t, hd = q.shape
    blk = min(_ATT_BLOCK, t)

    def body(q_ref, k_ref, v_ref, o_ref, rest_ref, first_ref):
        h, i = pl.program_id(0), pl.program_id(1)
        qb = q_ref[...]
        ri = lax.broadcasted_iota(jnp.int32, (blk, blk), 0)
        ci = lax.broadcasted_iota(jnp.int32, (blk, blk), 1)
        tri = (ri >= ci).astype(BF)

        def tile(j, right, acc, diagonal):
            k0 = pl.multiple_of(j * blk, blk)
            lb, l1m, mask = _logits(qb, k_ref[pl.ds(k0, blk), :], diagonal)
            incl = _split_dot(l1m, tri)
            a = jnp.exp(lb + (incl - l1m + right))
            if diagonal:
                a = jnp.where(mask, a, 0.0)
            acc = acc + jnp.dot(a.astype(BF), v_ref[pl.ds(k0, blk), :], preferred_element_type=F32)
            return right + incl[:, 0:1], acc

        right, acc = tile(i, jnp.zeros((blk, 1), F32), jnp.zeros((blk, GROUP), F32), True)

        def more(carry):
            j, right, _ = carry
            return jnp.logical_and(j >= 0, jnp.max(right) > EXP_FLOOR)

        def step(carry):
            j, right, acc = carry
            right, acc = tile(j, right, acc, False)
            return j - 1, right, acc

        j, right, acc = lax.while_loop(more, step, (i - 1, right, acc))
        o_ref[...] = acc.astype(BF)
        rest_ref[...] = jnp.broadcast_to(right, (blk, GROUP))
        first_ref[h, i] = (j + 1).astype(F32)

    qspec = pl.BlockSpec((blk, GROUP), lambda h, i: (i, h))
    kvspec = pl.BlockSpec((t, GROUP), lambda h, i: (0, h))
    outs = _call(
        body, name=name, grid=(hd // GROUP, t // blk), in_specs=[qspec, kvspec, kvspec],
        out_specs=[qspec, qspec, pl.BlockSpec(memory_space=pltpu.SMEM)],
        out_shape=[jax.ShapeDtypeStruct((t, hd), BF), jax.ShapeDtypeStruct((t, hd), F32),
                   jax.ShapeDtypeStruct((hd // GROUP, t // blk), F32)],
        operands=(q, k, v), sem=("arbitrary", "arbitrary"), vmem=32 * _MIB, carries=carries, middle_at=0.75)
    return outs[:3], outs[3:]


def _attn_bwd(name, q, k, v, rest, first, do, carries=()):
    t, hd = q.shape
    blk = min(_ATT_BLOCK, t)
    nq = t // blk
    scale = 1.0 / math.sqrt(GROUP)

    def body(first_ref, q_ref, k_ref, v_ref, rest_ref, do_ref, dq_ref, dk_ref, dv_ref, dk_acc, dv_acc):
        h, i = pl.program_id(0), pl.program_id(1)

        @pl.when(i == 0)
        def _():
            dk_acc[...] = jnp.zeros_like(dk_acc)
            dv_acc[...] = jnp.zeros_like(dv_acc)

        qb, dob = q_ref[...], do_ref[...]
        total = rest_ref[:, 0:1]
        ri = lax.broadcasted_iota(jnp.int32, (blk, blk), 0)
        ci = lax.broadcasted_iota(jnp.int32, (blk, blk), 1)
        tri = (ri <= ci).astype(BF)

        def tile(j, left, gleft, dq, diagonal):
            k0 = pl.multiple_of(j * blk, blk)
            kb, vb = k_ref[pl.ds(k0, blk), :], v_ref[pl.ds(k0, blk), :]
            lb, l1m, mask = _logits(qb, kb, diagonal)
            pre = _split_dot(l1m, tri)
            a = jnp.exp(lb + (total - left - pre))
            if diagonal:
                a = jnp.where(mask, a, 0.0)
            g = a * lax.dot_general(dob, vb, _NT, preferred_element_type=F32)
            gpre = _split_dot(g, tri)
            beta = jnp.exp(lb)
            dz = (g * (1.0 - beta) - (gleft + gpre - g) * beta) * scale
            if diagonal:
                dz = jnp.where(mask, dz, 0.0)
            dzb, ab = dz.astype(BF), a.astype(BF)
            dk_acc[pl.ds(k0, blk), :] += lax.dot_general(dzb, qb, _TN, preferred_element_type=F32)
            dv_acc[pl.ds(k0, blk), :] += lax.dot_general(ab, dob, _TN, preferred_element_type=F32)
            dq = dq + jnp.dot(dzb, kb, preferred_element_type=F32)
            return left + pre[:, blk - 1:blk], gleft + gpre[:, blk - 1:blk], dq

        zero = jnp.zeros((blk, 1), F32)
        first_block = jnp.clip(first_ref[h, i].astype(jnp.int32), 0, i)
        carry = lax.fori_loop(first_block, i, lambda j, c: tile(j, *c, False), (zero, zero, jnp.zeros((blk, GROUP), F32)))
        _, _, dq = tile(i, *carry, True)
        dq_ref[...] = dq.astype(BF)

        @pl.when(i == nq - 1)
        def _():
            dk_ref[...] = dk_acc[...].astype(BF)
            dv_ref[...] = dv_acc[...].astype(BF)

    qspec = pl.BlockSpec((blk, GROUP), lambda h, i: (i, h))
    kvspec = pl.BlockSpec((t, GROUP), lambda h, i: (0, h))
    outs = _call(
        body, name=name, grid=(hd // GROUP, nq),
        in_specs=[pl.BlockSpec(memory_space=pltpu.SMEM), qspec, kvspec, kvspec, qspec, qspec],
        out_specs=[qspec, kvspec, kvspec], out_shape=[jax.ShapeDtypeStruct((t, hd), BF)] * 3,
        operands=(first, q, k, v, rest, do), scratch=[pltpu.VMEM((t, GROUP), F32), pltpu.VMEM((t, GROUP), F32)],
        sem=("arbitrary", "arbitrary"), vmem=40 * _MIB, carries=carries)
    return outs[:3], outs[3:]


def _adamw_math(w, g, m, v):
    m = ADAM_B1 * m + (1.0 - ADAM_B1) * g
    v = ADAM_B2 * v + (1.0 - ADAM_B2) * (g * g)
    m_hat = m / (1.0 - ADAM_B1 ** ADAM_STEP)
    v_hat = v / (1.0 - ADAM_B2 ** ADAM_STEP)
    return -ADAM_LR * (m_hat / (jnp.sqrt(v_hat) + ADAM_EPS) + ADAM_WD * w), m, v


def _sum_adamw(name, parts, w, m, v):
    _, r, c = parts.shape
    tr = r if r * c <= 512 * 1024 else _tile_rows(r, max(8, (512 * 1024 // c) // 8 * 8))

    def body(p_ref, w_ref, m_ref, v_ref, g_ref, d_ref, nm_ref, nv_ref):
        g = p_ref[0].astype(F32)
        for dev in range(1, N_DEV):
            g = g + p_ref[dev].astype(F32)
        g_ref[...] = g
        d_ref[...], nm_ref[...], nv_ref[...] = _adamw_math(w_ref[...], g, m_ref[...], v_ref[...])

    row = pl.BlockSpec((tr, c), lambda i: (i, 0))
    return pl.pallas_call(
        body, name=name, grid=(r // tr,),
        in_specs=[pl.BlockSpec((N_DEV, tr, c), lambda i: (0, i, 0)), row, row, row], out_specs=[row] * 4,
        out_shape=[jax.ShapeDtypeStruct((r, c), F32)] * 4, compiler_params=_params(("parallel",), 40 * _MIB),
    )(parts, w, m, v)


def _tile_rows(r, pref):
    t = min(r, pref)
    while r % t or t % 8:
        t -= 1
    return t


def _all_gather(name, shards):
    n = len(shards)

    def body(*refs):
        ins, outs = refs[:n], refs[n:2 * n]
        send_sems, recv_sems, local_sems = refs[2 * n:]
        x, y, c, _ = _place()
        me, sibling = (x, y, c), (x, y, 1 - c)
        chips = [(1 - x, y), (x, 1 - y), (1 - x, 1 - y)]

        def copy(t, sem, block, to, src=None):
            slot = outs[t].at[4 * block[0] + 2 * block[1] + block[2]]
            return pltpu.make_async_remote_copy(
                src_ref=slot if src is None else src, dst_ref=slot, send_sem=send_sems.at[t, sem],
                recv_sem=recv_sems.at[t, sem], device_id=to, device_id_type=MESH)

        local, sent = [], []
        for t in range(n):
            local.append(pltpu.make_async_copy(ins[t], outs[t].at[4 * x + 2 * y + c], local_sems.at[t]))
            local[-1].start()
            first = [copy(t, 0, me, sibling, src=ins[t])]
            first += [copy(t, 1 + j, me, (*chip, c), src=ins[t]) for j, chip in enumerate(chips)]
            for cp in first:
                cp.start()
            sent += first
        for t in range(n):
            for j, chip in enumerate(chips):
                copy(t, 1 + j, (*chip, c), me).wait_recv()
                sent.append(copy(t, 4 + j, (*chip, c), sibling))
                sent[-1].start()
        for t in range(n):
            copy(t, 0, sibling, me).wait_recv()
            for j, chip in enumerate(chips):
                copy(t, 4 + j, (*chip, 1 - c), me).wait_recv()
        for cp in sent:
            cp.wait_send()
        for cp in local:
            cp.wait()

    any_spec = pl.BlockSpec(memory_space=pl.ANY)
    return pl.pallas_call(
        body, name=name, in_specs=[any_spec] * n, out_specs=[any_spec] * n,
        out_shape=[jax.ShapeDtypeStruct((N_DEV, *s.shape), s.dtype) for s in shards],
        scratch_shapes=[pltpu.SemaphoreType.DMA((n, 7)), pltpu.SemaphoreType.DMA((n, 7)), pltpu.SemaphoreType.DMA((n,))],
        compiler_params=pltpu.CompilerParams(has_side_effects=True),
    )(*shards)


def _scatter_partials(name, partials):
    n = len(partials)

    def body(*refs):
        ins, outs = refs[:n], refs[n:2 * n]
        send_sems, recv_sems, local_sems = refs[2 * n:]
        x, y, c, me = _place()

        def copy(t, k):
            peer = _flip(x, y, c, k)
            return pltpu.make_async_remote_copy(
                src_ref=ins[t].at[4 * peer[0] + 2 * peer[1] + peer[2]], dst_ref=outs[t].at[me],
                send_sem=send_sems.at[t, k - 1], recv_sem=recv_sems.at[t, k - 1], device_id=peer, device_id_type=MESH)

        for t in range(n):
            mine = pltpu.make_async_copy(ins[t].at[me], outs[t].at[me], local_sems.at[t])
            mine.start()
            for k in range(1, N_DEV):
                copy(t, k).start()
            if t > 0:
                _drain(t - 1, copy, ins, outs, local_sems, me)
        _drain(n - 1, copy, ins, outs, local_sems, me)

    any_spec = pl.BlockSpec(memory_space=pl.ANY)
    return pl.pallas_call(
        body, name=name, in_specs=[any_spec] * n, out_specs=[any_spec] * n,
        out_shape=[jax.ShapeDtypeStruct(p.shape, p.dtype) for p in partials],
        scratch_shapes=[pltpu.SemaphoreType.DMA((n, 7)), pltpu.SemaphoreType.DMA((n, 7)), pltpu.SemaphoreType.DMA((n,))],
        compiler_params=pltpu.CompilerParams(has_side_effects=True),
    )(*partials)


def _drain(t, copy, ins, outs, local_sems, me):
    for k in range(1, N_DEV):
        copy(t, k).wait()
    pltpu.make_async_copy(ins[t].at[me], outs[t].at[me], local_sems.at[t]).wait()


def _all_reduce_small(name, groups):
    c = groups[0][0].shape[1]
    parts, offsets, starts, r = [], [], [], 0
    for group in groups:
        starts.append(r)
        for p in group:
            parts.append(p)
            offsets.append(r)
            r += p.shape[0]
        r = -(-r // 8) * 8
    n = len(parts)

    def body(*refs):
        out_ref, slots, send_sems, recv_sems = refs[n:]
        x, y, c_, me = _place()
        slots[me] = jnp.zeros((r, c), F32)
        for p_ref, off in zip(refs[:n], offsets):
            slots[me, off:off + p_ref.shape[0], :] = p_ref[...]

        def copy(k):
            return pltpu.make_async_remote_copy(
                src_ref=slots.at[me], dst_ref=slots.at[me], send_sem=send_sems.at[k - 1], recv_sem=recv_sems.at[k - 1],
                device_id=_flip(x, y, c_, k), device_id_type=MESH)

        for k in range(1, N_DEV):
            copy(k).start()
        for k in range(1, N_DEV):
            copy(k).wait()
        total = slots[0]
        for dev in range(1, N_DEV):
            total = total + slots[dev]
        out_ref[...] = total

    vm = pl.BlockSpec(memory_space=pltpu.VMEM)
    summed = pl.pallas_call(
        body, name=name, in_specs=[vm] * n, out_specs=vm, out_shape=jax.ShapeDtypeStruct((r, c), F32),
        scratch_shapes=[pltpu.VMEM((N_DEV, r, c), F32), pltpu.SemaphoreType.DMA((7,)), pltpu.SemaphoreType.DMA((7,))],
        compiler_params=_params(None, (N_DEV + 6) * r * c * 4 + 8 * _MIB),
    )(*parts)
    return summed, starts


def _adamw(name, g, w, m, v):
    cols = w.shape[-1]
    flat = lambda a: a.reshape(-1, cols)

    def body(g_ref, w_ref, m_ref, v_ref, d_ref, nm_ref, nv_ref):
        d_ref[...], nm_ref[...], nv_ref[...] = _adamw_math(w_ref[...], g_ref[...], m_ref[...], v_ref[...])

    outs = pl.pallas_call(body, name=name, out_shape=[jax.ShapeDtypeStruct(flat(w).shape, F32)] * 3)(
        flat(g), flat(w), flat(m), flat(v))
    return [o.reshape(w.shape) for o in outs]


def kernel(x, pre_mix_g, post_mix_g, pre_ffn_g, post_ffn_g, a_w_in, a_v_norm_g, a_w_spatial, a_b_spatial, a_w_out, kv_norm_g, w_k, w_v, b_w_q, b_w_o, ffn_w_up, ffn_conv_w, ffn_conv_b, ffn_w_down, loss_target, m_pre_mix_g, m_post_mix_g, m_pre_ffn_g, m_post_ffn_g, m_a_w_in, m_a_v_norm_g, m_a_w_spatial, m_a_b_spatial, m_a_w_out, m_kv_norm_g, m_w_k, m_w_v, m_b_w_q, m_b_w_o, m_ffn_w_up, m_ffn_conv_w, m_ffn_conv_b, m_ffn_w_down, v_pre_mix_g, v_post_mix_g, v_pre_ffn_g, v_post_ffn_g, v_a_w_in, v_a_v_norm_g, v_a_w_spatial, v_a_b_spatial, v_a_w_out, v_kv_norm_g, v_w_k, v_w_v, v_b_w_q, v_b_w_o, v_ffn_w_up, v_ffn_conv_w, v_ffn_conv_b, v_ffn_w_down):
    t, d = x.shape[1], x.shape[2]
    f = ffn_w_down.shape[1] * N_DEV
    ng = d // GROUP
    me = 4 * lax.axis_index("x") + 2 * lax.axis_index("y") + lax.axis_index("c")
    x2, target = x.reshape(t, d), loss_target.reshape(t, d)

    g_in, g_out, g_up0, g_cw, g_vg = _all_gather("gather_first", [
        a_w_in[0].astype(BF), a_w_out[0].astype(BF), ffn_w_up[0].astype(BF), ffn_conv_w.reshape(2 * CONV_TAPS, -1),
        a_v_norm_g])
    w_out_f = g_out.reshape(d, d)
    cw_full = jnp.transpose(g_cw.reshape(N_DEV, 2, CONV_TAPS, -1), (1, 2, 0, 3)).reshape(2, CONV_TAPS, 2, f)
    cw_l = [jnp.transpose(cw_full[l], (1, 0, 2)) for l in range(2)]
    cb_l = [ffn_conv_b[l].reshape(2, 1, f) for l in range(2)]
    vg_full = g_vg.reshape(1, d)
    w_s = a_w_spatial[0]
    w_st = jnp.swapaxes(w_s, 1, 2)
    b_st = a_b_spatial[0].T

    (hn0,) = _norms_fwd("pre_mix0", x2, [pre_mix_g[0]])
    p0 = _mm_nn_blk("sgu_in", hn0, g_in, F32)
    sg = _sgu_fwd("sgu", p0, vg_full, w_s, b_st)
    mix0 = _mm_nn("sgu_out", sg, w_out_f, F32)
    h1, fn0 = _resid_norms("post_mix0", x2, mix0, post_mix_g[0], [pre_ffn_g[0]])
    a3_0, (g_dn0,) = _mm_nn_blk("ffn0_up", fn0, g_up0, F32, halves=True, carries=[_gather(ffn_w_down[0].astype(BF))])
    y0, (g_q,) = _conv_fwd("ffn0_conv", a3_0, cw_l[0], cb_l[0], carries=[_gather(b_w_q[0].astype(BF))])
    f0, (g_k, g_v) = _mm_nn("ffn0_down", y0, g_dn0.reshape(f, d), F32,
                            carries=[_gather(w_k.astype(BF)), _gather(w_v.astype(BF))])
    w_q_f, w_k_f, w_v_f = g_q.reshape(d, d), g_k.reshape(d, d), g_v.reshape(d, d)
    h2, hn1, kvn = _resid_norms("post_ffn0", h1, f0, post_ffn_g[0], [pre_mix_g[1], kv_norm_g])
    q = _mm_nn("attn_q", hn1, w_q_f, BF)
    kk = _mm_nn("attn_k", kvn, w_k_f, BF)
    vv = _mm_nn("attn_v", kvn, w_v_f, BF)
    (att, rest, first), (g_o, g_up1) = _attn_fwd(
        "attn", q, kk, vv, carries=[_gather(b_w_o[0].astype(BF)), _gather(ffn_w_up[1].astype(BF))])
    w_o_f = g_o.reshape(d, d)
    mix1 = _mm_nn("attn_o", att, w_o_f, F32)
    h3, fn1 = _resid_norms("post_mix1", h2, mix1, post_mix_g[1], [pre_ffn_g[1]])
    a3_1, (g_dn1,) = _mm_nn_blk("ffn1_up", fn1, g_up1, F32, halves=True, carries=[_gather(ffn_w_down[1].astype(BF))])
    y1, _ = _conv_fwd("ffn1_conv", a3_1, cw_l[1], cb_l[1])
    f1 = _mm_nn("ffn1_down", y1, g_dn1.reshape(f, d), F32)
    g_up = (g_up0, g_up1)
    w_dn_f = (g_dn0.reshape(f, d), g_dn1.reshape(f, d))
    dh, loss_part = _resid_loss("loss", h3, f1, post_ffn_g[1], target)
    loss = lax.psum(loss_part[0, 0], ("x", "y", "c"))

    def blocks(dw):
        return dw.reshape(N_DEV, -1, d)

    def split(result, carries):
        return result if carries else (result, [])

    def ffn_bwd(l, dh_out, h_in, fn, a3, yv, fo, with_dw=(), with_dx=()):
        dfo, d_post = _norm_bwd(f"post_ffn{l}_bwd", fo, post_ffn_g[l], dh_out, out_dtype=BF)
        dw_dn, sent_dw = split(_mm_tn(f"ffn{l}_down_dw", yv, dfo, carries=with_dw), with_dw)
        dy, sent_dx = split(_mm_nt(f"ffn{l}_down_dx", dfo, w_dn_f[l], BF, carries=with_dx), with_dx)
        (da3, dcw, dcb), (p_dn,) = _conv_bwd(f"ffn{l}_conv_bwd", a3, cw_l[l], cb_l[l], dy, carries=[_scatter(blocks(dw_dn))])
        dw_up = _mm_tn_blk(f"ffn{l}_up_dw", fn, da3, halves=True)
        dfn, (p_up,) = _mm_nt_blk(f"ffn{l}_up_dx", da3, g_up[l], F32, halves=True, carries=[_scatter(dw_up, 0, d // 2)])
        dh_in, d_pre = _norm_bwd(f"pre_ffn{l}_bwd", h_in, pre_ffn_g[l], dfn, res=dh_out)
        return dh_in, d_post, d_pre, dcw, dcb, p_dn, dw_up, p_up, list(sent_dw) + list(sent_dx)

    dh3, d_post_ffn1, d_pre_ffn1, dcw1, dcb1, p_dn1, dw_up1, p_up1, _ = ffn_bwd(1, dh, h3, fn1, a3_1, y1, f1)
    dmix1, d_post_mix1 = _norm_bwd("post_mix1_bwd", mix1, post_mix_g[1], dh3, out_dtype=BF)
    dw_o = _mm_tn("attn_o_dw", att, dmix1)
    datt = _mm_nt("attn_o_dx", dmix1, w_o_f, BF)
    (dq, dk, dv), (p_up1, p_o) = _attn_bwd(
        "attn_bwd", q, kk, vv, rest, first, datt, carries=[_scatter(dw_up1, d // 2, d, dst=p_up1), _scatter(blocks(dw_o))])
    dw_q = _mm_tn("attn_q_dw", hn1, dq)
    dw_k = _mm_tn("attn_k_dw", kvn, dk)
    dw_v = _mm_tn("attn_v_dw", kvn, dv)
    dhn1 = _mm_nt("attn_q_dx", dq, w_q_f, F32)
    dkvn = _mm_nt("attn_v_dx", dv, w_v_f, F32, add=_mm_nt("attn_k_dx", dk, w_k_f, F32))
    dh2a, d_pre_mix1 = _norm_bwd("pre_mix1_bwd", h2, pre_mix_g[1], dhn1, res=dh3)
    dh2, d_kv = _norm_bwd("kv_norm_bwd", h2, kv_norm_g, dkvn, res=dh2a)
    dh1, d_post_ffn0, d_pre_ffn0, dcw0, dcb0, p_dn0, dw_up0, p_up0, (p_q, p_k, p_v) = ffn_bwd(
        0, dh2, h1, fn0, a3_0, y0, f0, with_dw=[_scatter(blocks(dw_q)), _scatter(blocks(dw_k))],
        with_dx=[_scatter(blocks(dw_v))])
    dmix0, d_post_mix0 = _norm_bwd("post_mix0_bwd", mix0, post_mix_g[0], dh1, out_dtype=BF)
    dw_out = _mm_tn("sgu_out_dw", sg, dmix0)
    dsg = _mm_nt("sgu_out_dx", dmix0, w_out_f, BF)
    (dp0, d_vg, d_ws, d_bst), (p_up0,) = _sgu_bwd(
        "sgu_bwd", p0, vg_full, w_s, w_st, b_st, dsg, carries=[_scatter(dw_up0, d // 2, 3 * d // 4, dst=p_up0)])
    dw_in, (p_out,) = _mm_tn_blk("sgu_in_dw", hn0, dp0, carries=[_scatter(blocks(dw_out))])
    dhn0, (p_up0,) = _mm_nt_blk("sgu_in_dx", dp0, g_in, F32, carries=[_scatter(dw_up0, 3 * d // 4, d, dst=p_up0)])
    grad_x, d_pre_mix0 = _norm_bwd("pre_mix0_bwd", x2, pre_mix_g[0], dhn0, res=dh1)
    (p_in,) = _scatter_partials("scatter_last", [dw_in])

    def conv_w_grad(dcw):
        return jnp.transpose(dcw, (1, 0, 2)).reshape(CONV_TAPS, 2 * f)

    small = [
        ([d_pre_mix0, d_pre_mix1], (2, d)), ([d_post_mix0, d_post_mix1], (2, d)),
        ([d_pre_ffn0, d_pre_ffn1], (2, d)), ([d_post_ffn0, d_post_ffn1], (2, d)),
        ([d_kv], (d,)), ([d_vg], (1, d)), ([d_bst.T], (1, ng, GROUP)), ([d_ws], (1, ng, GROUP, GROUP)),
        ([dcb0, dcb1], (2, 2 * f)), ([conv_w_grad(dcw0), conv_w_grad(dcw1)], (2, CONV_TAPS, 2 * f)),
    ]
    width = V7X_LANES * math.gcd(d // V7X_LANES, 2 * f // V7X_LANES)
    summed, offsets = _all_reduce_small("reduce_small", [[a.reshape(-1, width) for a in group] for group, _ in small])
    full = [summed[off:off + math.prod(shape) // width].reshape(shape) for off, (_, shape) in zip(offsets, small)]
    g_pre_mix, g_post_mix, g_pre_ffn, g_post_ffn, g_kv, g_vgain, g_bs, g_ws, g_cb, g_cwf = full
    cw_w = 2 * f // N_DEV
    g_vgain = lax.dynamic_slice_in_dim(g_vgain, me * (d // N_DEV), d // N_DEV, axis=1)
    g_cwf = lax.dynamic_slice_in_dim(g_cwf, me * cw_w, cw_w, axis=2)

    parts = [p_in, p_out, p_k, p_v, p_q, p_o, p_up0, p_up1, p_dn0, p_dn1]

    def big_update(name, part, w, m, v):
        return _sum_adamw(name, part, w.reshape(part.shape[1:]), m.reshape(part.shape[1:]), v.reshape(part.shape[1:]))

    def small_update(name, g, w, m, v):
        return [g] + _adamw(name, g, w, m, v)

    def stacked(name, part0, part1, w, m, v):
        outs = [big_update(f"{name}{l}", p, w[l], m[l], v[l]) for l, p in enumerate((part0, part1))]
        return [jnp.stack([a, b]).reshape(w.shape) for a, b in zip(*outs)]

    def single(name, part, w, m, v):
        return [o.reshape(w.shape) for o in big_update(name, part, w, m, v)]

    results = {
        "pre_mix_g": small_update("adam_pre_mix", g_pre_mix, pre_mix_g, m_pre_mix_g, v_pre_mix_g),
        "post_mix_g": small_update("adam_post_mix", g_post_mix, post_mix_g, m_post_mix_g, v_post_mix_g),
        "pre_ffn_g": small_update("adam_pre_ffn", g_pre_ffn, pre_ffn_g, m_pre_ffn_g, v_pre_ffn_g),
        "post_ffn_g": small_update("adam_post_ffn", g_post_ffn, post_ffn_g, m_post_ffn_g, v_post_ffn_g),
        "a_w_in": single("adam_a_w_in", parts[0], a_w_in, m_a_w_in, v_a_w_in),
        "a_v_norm_g": small_update("adam_a_v_norm", g_vgain, a_v_norm_g, m_a_v_norm_g, v_a_v_norm_g),
        "a_w_spatial": small_update("adam_a_w_spatial", g_ws, a_w_spatial, m_a_w_spatial, v_a_w_spatial),
        "a_b_spatial": small_update("adam_a_b_spatial", g_bs, a_b_spatial, m_a_b_spatial, v_a_b_spatial),
        "a_w_out": single("adam_a_w_out", parts[1], a_w_out, m_a_w_out, v_a_w_out),
        "kv_norm_g": small_update("adam_kv_norm", g_kv, kv_norm_g, m_kv_norm_g, v_kv_norm_g),
        "w_k": single("adam_w_k", parts[2], w_k, m_w_k, v_w_k),
        "w_v": single("adam_w_v", parts[3], w_v, m_w_v, v_w_v),
        "b_w_q": single("adam_b_w_q", parts[4], b_w_q, m_b_w_q, v_b_w_q),
        "b_w_o": single("adam_b_w_o", parts[5], b_w_o, m_b_w_o, v_b_w_o),
        "ffn_w_up": stacked("adam_ffn_w_up", parts[6], parts[7], ffn_w_up, m_ffn_w_up, v_ffn_w_up),
        "ffn_conv_w": small_update("adam_ffn_conv_w", g_cwf, ffn_conv_w, m_ffn_conv_w, v_ffn_conv_w),
        "ffn_conv_b": small_update("adam_ffn_conv_b", g_cb, ffn_conv_b, m_ffn_conv_b, v_ffn_conv_b),
        "ffn_w_down": stacked("adam_ffn_w_down", parts[8], parts[9], ffn_w_down, m_ffn_w_down, v_ffn_w_down),
    }
    order = ["pre_mix_g", "post_mix_g", "pre_ffn_g", "post_ffn_g", "a_w_in", "a_v_norm_g", "a_w_spatial", "a_b_spatial",
             "a_w_out", "kv_norm_g", "w_k", "w_v", "b_w_q", "b_w_o", "ffn_w_up", "ffn_conv_w", "ffn_conv_b", "ffn_w_down"]
    outs = [loss, grad_x.reshape(x.shape)]
    for idx in range(4):
        outs += [results[n][idx] for n in order]
    return tuple(outs)
```

```python
import math
from typing import NamedTuple, Optional

import jax
import jax.numpy as jnp
from jax import lax
from jax.experimental import pallas as pl
from jax.experimental.pallas import tpu as pltpu

F32 = jnp.float32
BF = jnp.bfloat16
MESH = pl.DeviceIdType.MESH

N_DEV = 8
NORM_EPS = 1e-6
GROUP = 128
CONV_TAPS = 3
ADAM_LR, ADAM_B1, ADAM_B2, ADAM_EPS, ADAM_WD, ADAM_STEP = 0.001, 0.9, 0.999, 1e-08, 0.01, 10
EXP_FLOOR = -104.0

V7X_LANES = 128
V7X_VMEM_BYTES = 64 * 1024 * 1024
_MIB = 1024 * 1024

_NN = (((1,), (0,)), ((), ()))
_NT = (((1,), (1,)), ((), ()))
_TN = (((0,), (0,)), ((), ()))


def _tile(n, pref):
    if n <= pref:
        return n
    t = (pref // V7X_LANES) * V7X_LANES
    while t > V7X_LANES and n % t:
        t -= V7X_LANES
    assert n % t == 0, (n, pref)
    return t


def _nbytes(shape, dtype):
    return math.prod(shape) * jnp.dtype(dtype).itemsize


def _params(sem=None, vmem=None):
    kw = {}
    if sem is not None:
        kw["dimension_semantics"] = sem
    if vmem is not None:
        kw["vmem_limit_bytes"] = int(min(max(vmem, 16 * _MIB), V7X_VMEM_BYTES - 8 * _MIB))
    return pltpu.CompilerParams(**kw)


def _place():
    x, y, c = lax.axis_index("x"), lax.axis_index("y"), lax.axis_index("c")
    return x, y, c, 4 * x + 2 * y + c


def _flip(x, y, c, k):
    return (1 - x if k & 4 else x, 1 - y if k & 2 else y, 1 - c if k & 1 else c)


class _Carry(NamedTuple):
    gather: bool
    src: jax.Array
    dst: Optional[jax.Array]
    lo: int
    hi: int


def _gather(src, lo=0, hi=None, dst=None):
    return _Carry(True, src, dst, lo, src.shape[0] if hi is None else hi)


def _scatter(src, lo=0, hi=None, dst=None):
    return _Carry(False, src, dst, lo, src.shape[1] if hi is None else hi)


def _carry_phases(carries, srcs, dsts, send_sems, recv_sems, local_sems):
    x, y, c, me = _place()
    here, sibling = (x, y, c), (x, y, 1 - c)
    chips = [(1 - x, y), (x, 1 - y), (1 - x, 1 - y)]

    def rows(u):
        return pl.ds(carries[u].lo, carries[u].hi - carries[u].lo)

    def block_copy(u, sem, block, to, from_src=False):
        slot = dsts[u].at[4 * block[0] + 2 * block[1] + block[2], rows(u)]
        return pltpu.make_async_remote_copy(
            src_ref=srcs[u].at[rows(u)] if from_src else slot, dst_ref=slot, send_sem=send_sems.at[u, sem],
            recv_sem=recv_sems.at[u, sem], device_id=to, device_id_type=MESH)

    def partial_copy(u, k):
        peer = _flip(x, y, c, k)
        return pltpu.make_async_remote_copy(
            src_ref=srcs[u].at[4 * peer[0] + 2 * peer[1] + peer[2], rows(u)], dst_ref=dsts[u].at[me, rows(u)],
            send_sem=send_sems.at[u, k - 1], recv_sem=recv_sems.at[u, k - 1], device_id=peer, device_id_type=MESH)

    def local_copy(u):
        src = srcs[u].at[rows(u)] if carries[u].gather else srcs[u].at[me, rows(u)]
        return pltpu.make_async_copy(src, dsts[u].at[me, rows(u)], local_sems.at[u])

    def first():
        for u, cr in enumerate(carries):
            local_copy(u).start()
            if cr.gather:
                block_copy(u, 0, here, sibling, from_src=True).start()
                for j, chip in enumerate(chips):
                    block_copy(u, 1 + j, here, (*chip, c), from_src=True).start()
            else:
                for k in range(1, N_DEV):
                    partial_copy(u, k).start()

    def middle():
        for u, cr in enumerate(carries):
            if cr.gather:
                for j, chip in enumerate(chips):
                    block_copy(u, 1 + j, (*chip, c), here).wait_recv()
                    block_copy(u, 4 + j, (*chip, c), sibling).start()

    def last():
        for u, cr in enumerate(carries):
            if cr.gather:
                block_copy(u, 0, sibling, here).wait_recv()
                for j, chip in enumerate(chips):
                    block_copy(u, 4 + j, (*chip, 1 - c), here).wait_recv()
                block_copy(u, 0, here, sibling, from_src=True).wait_send()
                for j, chip in enumerate(chips):
                    block_copy(u, 1 + j, here, (*chip, c), from_src=True).wait_send()
                    block_copy(u, 4 + j, (*chip, c), sibling).wait_send()
            else:
                for k in range(1, N_DEV):
                    partial_copy(u, k).wait()
            local_copy(u).wait()

    return first, middle, last


def _call(body, *, name, grid, in_specs, out_specs, out_shape, operands, scratch=(), sem=None, vmem=None,
          carries=(), middle_at=0.6):
    if not carries:
        return pl.pallas_call(
            body, name=name, grid=grid, in_specs=in_specs, out_specs=out_specs, out_shape=out_shape,
            scratch_shapes=list(scratch), compiler_params=_params(sem, vmem))(*operands)
    n_in, n_out, n_scr, nc = len(in_specs), len(out_specs), len(scratch), len(carries)
    given = [u for u, cr in enumerate(carries) if cr.dst is not None]
    steps = math.prod(grid)
    middle_step = min(steps - 1, int(steps * middle_at))

    def wrapped(*refs):
        ins, srcs = refs[:n_in], refs[n_in:n_in + nc]
        at = n_in + nc + len(given)
        outs, dsts = refs[at:at + n_out], refs[at + n_out:at + n_out + nc]
        at += n_out + nc
        scr, (send_sems, recv_sems, local_sems) = refs[at:at + n_scr], refs[at + n_scr:]
        first, middle, last = _carry_phases(carries, srcs, dsts, send_sems, recv_sems, local_sems)
        step = 0
        for axis, size in enumerate(grid):
            step = step * size + pl.program_id(axis)
        pl.when(step == 0)(first)
        body(*ins, *outs, *scr)
        pl.when(step == middle_step)(middle)
        pl.when(step == steps - 1)(last)

    any_spec = pl.BlockSpec(memory_space=pl.ANY)
    dst_shapes = [jax.ShapeDtypeStruct((N_DEV, *cr.src.shape) if cr.gather else cr.src.shape, cr.src.dtype) for cr in carries]
    return pl.pallas_call(
        wrapped, name=name, grid=grid, in_specs=list(in_specs) + [any_spec] * (nc + len(given)),
        out_specs=list(out_specs) + [any_spec] * nc, out_shape=list(out_shape) + dst_shapes,
        input_output_aliases={n_in + nc + g: n_out + u for g, u in enumerate(given)},
        scratch_shapes=list(scratch) + [pltpu.SemaphoreType.DMA((nc, 7)), pltpu.SemaphoreType.DMA((nc, 7)),
                                        pltpu.SemaphoreType.DMA((nc,))],
        compiler_params=_params(("arbitrary",) * len(grid), vmem),
    )(*operands, *[cr.src for cr in carries], *[carries[u].dst for u in given])


def _mm(name, a, b, *, dims, grid, a_blk, a_map, b_blk, b_map, o_blk, o_map, out_shape, out_dtype,
        add=None, add_blk=None, add_map=None, carries=()):
    nk = grid[2]
    assert add is None or nk == 1
    acc_shape = tuple(d for d in o_blk if d is not None)
    in_place = out_dtype == F32

    def body(*refs):
        if add is None:
            a_ref, b_ref, o_ref = refs[:3]
            c_ref, scr = None, refs[3:]
        else:
            a_ref, b_ref, c_ref, o_ref = refs[:4]
            scr = refs[4:]
        part = lax.dot_general(a_ref[...], b_ref[...], dims, preferred_element_type=F32)
        if c_ref is not None:
            part = part + c_ref[...].astype(F32)
        if nk == 1:
            o_ref[...] = part.astype(o_ref.dtype)
            return
        acc = o_ref if in_place else scr[0]
        k = pl.program_id(2)

        @pl.when(k == 0)
        def _():
            acc[...] = part

        @pl.when(k > 0 if in_place else jnp.logical_and(k > 0, k < nk - 1))
        def _():
            acc[...] += part

        if not in_place:
            @pl.when(k == nk - 1)
            def _():
                o_ref[...] = (acc[...] + part).astype(o_ref.dtype)

    in_specs = [pl.BlockSpec(a_blk, a_map), pl.BlockSpec(b_blk, b_map)]
    operands = [a, b]
    scratch = [pltpu.VMEM(acc_shape, F32)] if nk > 1 and not in_place else []
    vmem = 2 * (_nbytes(acc_shape, out_dtype) + _nbytes([d for d in a_blk if d], a.dtype)
                + _nbytes([d for d in b_blk if d], b.dtype)) + (2 + len(scratch)) * _nbytes(acc_shape, F32)
    if add is not None:
        in_specs.append(pl.BlockSpec(add_blk, add_map))
        operands.append(add)
        vmem += 2 * _nbytes(acc_shape, add.dtype)
    out, *dsts = _call(
        body, name=name, grid=grid, in_specs=in_specs, out_specs=[pl.BlockSpec(o_blk, o_map)],
        out_shape=[jax.ShapeDtypeStruct(out_shape, out_dtype)], operands=operands, scratch=scratch,
        sem=("parallel", "parallel", "arbitrary"), vmem=vmem + 8 * _MIB, carries=carries)
    return (out, dsts) if carries else out


def _mm_nn(name, x, w, out_dtype, carries=()):
    t, kd = x.shape
    n = w.shape[1]
    tm, tn, tk = _tile(t, 1024), _tile(n, 1024), _tile(kd, 1536 if kd > 2048 else 2048)
    return _mm(name, x, w, dims=_NN, grid=(t // tm, n // tn, kd // tk),
               a_blk=(tm, tk), a_map=lambda i, j, k: (i, k), b_blk=(tk, tn), b_map=lambda i, j, k: (k, j),
               o_blk=(tm, tn), o_map=lambda i, j, k: (i, j), out_shape=(t, n), out_dtype=out_dtype, carries=carries)


def _mm_nn_blk(name, x, g, out_dtype, halves=False, carries=()):
    t, kd = x.shape
    cw = g.shape[2]
    tm = _tile(t, 1024)
    if halves:
        o_blk, o_map, out_shape = (None, tm, cw), (lambda i, j, k: (j // 4, i, j % 4)), (2, t, 4 * cw)
    else:
        o_blk, o_map, out_shape = (tm, cw), (lambda i, j, k: (i, j)), (t, N_DEV * cw)
    return _mm(name, x, g, dims=_NN, grid=(t // tm, N_DEV, 1),
               a_blk=(tm, kd), a_map=lambda i, j, k: (i, 0), b_blk=(None, kd, cw), b_map=lambda i, j, k: (j, 0, 0),
               o_blk=o_blk, o_map=o_map, out_shape=out_shape, out_dtype=out_dtype, carries=carries)


def _mm_nt(name, dy, w, out_dtype, add=None, carries=()):
    t, n = dy.shape
    kd = w.shape[0]
    tm, tn = _tile(t, 1024), _tile(kd, 512)
    kw = {}
    if add is not None:
        kw = dict(add=add, add_blk=(tm, tn), add_map=lambda i, j, k: (i, j))
    return _mm(name, dy, w, dims=_NT, grid=(t // tm, kd // tn, 1),
               a_blk=(tm, n), a_map=lambda i, j, k: (i, 0), b_blk=(tn, n), b_map=lambda i, j, k: (j, 0),
               o_blk=(tm, tn), o_map=lambda i, j, k: (i, j), out_shape=(t, kd), out_dtype=out_dtype, carries=carries, **kw)


def _mm_nt_blk(name, dy, g, out_dtype, halves=False, carries=()):
    kd, cw = g.shape[1], g.shape[2]
    t = dy.shape[1] if halves else dy.shape[0]
    tm = _tile(t, 512)
    if halves:
        a_blk, a_map = (None, tm, cw), (lambda i, j, k: (k // 4, i, k % 4))
    else:
        a_blk, a_map = (tm, cw), (lambda i, j, k: (i, k))
    return _mm(name, dy, g, dims=_NT, grid=(t // tm, 1, N_DEV),
               a_blk=a_blk, a_map=a_map, b_blk=(None, kd, cw), b_map=lambda i, j, k: (k, 0, 0),
               o_blk=(tm, kd), o_map=lambda i, j, k: (i, 0), out_shape=(t, kd), out_dtype=out_dtype, carries=carries)


def _mm_tn(name, x, dy, carries=()):
    t, kd = x.shape
    n = dy.shape[1]
    tmx, tk = _tile(kd, 1024), _tile(t, 2048)
    return _mm(name, x, dy, dims=_TN, grid=(kd // tmx, 1, t // tk),
               a_blk=(tk, tmx), a_map=lambda i, j, k: (k, i), b_blk=(tk, n), b_map=lambda i, j, k: (k, 0),
               o_blk=(tmx, n), o_map=lambda i, j, k: (i, 0), out_shape=(kd, n), out_dtype=BF, carries=carries)


def _mm_tn_blk(name, x, dy, halves=False, carries=()):
    t, kd = x.shape
    cw = dy.shape[2] // 4 if halves else dy.shape[1] // N_DEV
    tmx, tk = _tile(kd, 2048 if cw <= 512 else 1024), _tile(t, 2048)
    if halves:
        b_blk, b_map = (None, tk, cw), (lambda i, j, k: (j // 4, k, j % 4))
    else:
        b_blk, b_map = (tk, cw), (lambda i, j, k: (k, j))
    return _mm(name, x, dy, dims=_TN, grid=(kd // tmx, N_DEV, t // tk),
               a_blk=(tk, tmx), a_map=lambda i, j, k: (k, i), b_blk=b_blk, b_map=b_map,
               o_blk=(None, tmx, cw), o_map=lambda i, j, k: (j, i, 0), out_shape=(N_DEV, kd, cw), out_dtype=BF,
               carries=carries)


def _rms(x, g):
    r = lax.rsqrt(jnp.mean(x * x, axis=-1, keepdims=True) + NORM_EPS)
    return x * r * g


def _rms_bwd_math(x, g, dy):
    d = x.shape[-1]
    r = lax.rsqrt(jnp.mean(x * x, axis=-1, keepdims=True) + NORM_EPS)
    xh = x * r
    u = dy * g
    dx = r * u - xh * (jnp.sum(xh * u, axis=-1, keepdims=True) * (r / d))
    return dx, jnp.sum(dy * xh, axis=0, keepdims=True)


def _row_specs(tr, d, n):
    return [pl.BlockSpec((tr, d), lambda i: (i, 0)) for _ in range(n)]


def _vec_specs(d, n):
    return [pl.BlockSpec((1, d), lambda i: (0, 0)) for _ in range(n)]


def _norms_fwd(name, h, gains):
    t, d = h.shape
    tr, ng = _tile(t, 256), len(gains)

    def body(h_ref, *refs):
        x = h_ref[...]
        for g_ref, o_ref in zip(refs[:ng], refs[ng:]):
            o_ref[...] = _rms(x, g_ref[...]).astype(BF)

    return pl.pallas_call(
        body, name=name, grid=(t // tr,), in_specs=_row_specs(tr, d, 1) + _vec_specs(d, ng),
        out_specs=_row_specs(tr, d, ng), out_shape=[jax.ShapeDtypeStruct((t, d), BF)] * ng,
        compiler_params=_params(("parallel",)),
    )(h, *[g.reshape(1, d) for g in gains])


def _resid_norms(name, h, m, g_post, gains, carries=()):
    t, d = h.shape
    tr, ng = _tile(t, 256), len(gains)

    def body(h_ref, m_ref, gp_ref, *refs):
        hn = h_ref[...] + _rms(m_ref[...], gp_ref[...])
        refs[ng][...] = hn
        for g_ref, o_ref in zip(refs[:ng], refs[ng + 1:]):
            o_ref[...] = _rms(hn, g_ref[...]).astype(BF)

    return _call(
        body, name=name, grid=(t // tr,), in_specs=_row_specs(tr, d, 2) + _vec_specs(d, 1 + ng),
        out_specs=_row_specs(tr, d, 1 + ng),
        out_shape=[jax.ShapeDtypeStruct((t, d), F32)] + [jax.ShapeDtypeStruct((t, d), BF)] * ng,
        operands=(h, m, g_post.reshape(1, d), *[g.reshape(1, d) for g in gains]), sem=("parallel",), carries=carries)


def _resid_loss(name, h, m, g_post, target):
    t, d = h.shape
    tr = _tile(t, 256)

    def body(h_ref, m_ref, gp_ref, t_ref, dy_ref, loss_ref):
        diff = h_ref[...] + _rms(m_ref[...], gp_ref[...]) - t_ref[...]
        dy_ref[...] = diff * (1.0 / d)

        @pl.when(pl.program_id(0) == 0)
        def _():
            loss_ref[...] = jnp.zeros_like(loss_ref)

        per_row = jnp.sum(diff * diff, axis=-1, keepdims=True) * (1.0 / d)
        loss_ref[...] += 0.5 * jnp.sum(per_row, axis=0, keepdims=True)

    return pl.pallas_call(
        body, name=name, grid=(t // tr,),
        in_specs=_row_specs(tr, d, 2) + _vec_specs(d, 1) + _row_specs(tr, d, 1),
        out_specs=[pl.BlockSpec((tr, d), lambda i: (i, 0)), pl.BlockSpec((1, 1), lambda i: (0, 0))],
        out_shape=[jax.ShapeDtypeStruct((t, d), F32), jax.ShapeDtypeStruct((1, 1), F32)],
        compiler_params=_params(("arbitrary",)),
    )(h, m, g_post.reshape(1, d), target)


def _norm_bwd(name, x, g, dy, res=None, out_dtype=F32, carries=()):
    t, d = x.shape
    tr = _tile(t, 256)
    has_res = res is not None

    def body(x_ref, dy_ref, g_ref, *refs):
        dx_ref, dg_ref = refs[-2:]
        dx, dg = _rms_bwd_math(x_ref[...].astype(F32), g_ref[...], dy_ref[...].astype(F32))
        if has_res:
            dx = dx + refs[0][...]
        dx_ref[...] = dx.astype(dx_ref.dtype)

        @pl.when(pl.program_id(0) == 0)
        def _():
            dg_ref[...] = jnp.zeros_like(dg_ref)

        dg_ref[...] += dg

    ops = [x, dy, g.reshape(1, d)] + ([res] if has_res else [])
    return _call(
        body, name=name, grid=(t // tr,),
        in_specs=_row_specs(tr, d, 2) + _vec_specs(d, 1) + _row_specs(tr, d, int(has_res)),
        out_specs=[pl.BlockSpec((tr, d), lambda i: (i, 0)), pl.BlockSpec((1, d), lambda i: (0, 0))],
        out_shape=[jax.ShapeDtypeStruct((t, d), out_dtype), jax.ShapeDtypeStruct((1, d), F32)],
        operands=ops, sem=("arbitrary",), carries=carries)


_GELU_C = math.sqrt(2.0 / math.pi)
_GELU_A = 0.044715


def _gelu(x):
    return 0.5 * x * (1.0 + jnp.tanh(_GELU_C * (x + _GELU_A * x * x * x)))


def _gelu_and_grad(x):
    th = jnp.tanh(_GELU_C * (x + _GELU_A * x * x * x))
    grad = 0.5 * (1.0 + th) + 0.5 * x * (1.0 - th * th) * (_GELU_C * (1.0 + 3.0 * _GELU_A * x * x))
    return 0.5 * x * (1.0 + th), grad


def _causal(n):
    return lax.broadcasted_iota(jnp.int32, (n, n), 1) <= lax.broadcasted_iota(jnp.int32, (n, n), 0)


def _sgu_fwd(name, p, v_gain, w_s, b_st, carries=()):
    t, da2 = p.shape
    da = da2 // 2
    ng = da // GROUP

    def body(p_ref, vg_ref, ws_ref, bst_ref, o_ref):
        keep = _causal(GROUP)
        for g in range(ng):
            lo = g * GROUP
            u = _gelu(p_ref[:, lo:lo + GROUP])
            vn = _rms(_gelu(p_ref[:, da + lo:da + lo + GROUP]), vg_ref[:, lo:lo + GROUP])
            w = jnp.where(keep, ws_ref[g], 0.0).astype(BF)
            mixed = jnp.dot(w, vn.astype(BF), preferred_element_type=F32) + bst_ref[:, g:g + 1]
            o_ref[:, lo:lo + GROUP] = (u * mixed).astype(BF)

    return _call(
        body, name=name, grid=(t // GROUP,),
        in_specs=[pl.BlockSpec((GROUP, da2), lambda i: (i, 0)), pl.BlockSpec((1, da), lambda i: (0, 0)),
                  pl.BlockSpec((ng, GROUP, GROUP), lambda i: (0, 0, 0)), pl.BlockSpec((GROUP, ng), lambda i: (0, 0))],
        out_specs=[pl.BlockSpec((GROUP, da), lambda i: (i, 0))], out_shape=[jax.ShapeDtypeStruct((t, da), BF)],
        operands=(p, v_gain, w_s, b_st), sem=("parallel",), carries=carries)


def _sgu_bwd(name, p, v_gain, w_s, w_st, b_st, dout, carries=()):
    t, da2 = p.shape
    da = da2 // 2
    ng = da // GROUP

    def body(p_ref, vg_ref, ws_ref, wst_ref, bst_ref, do_ref, dp_ref, dvg_ref, dws_ref, dbst_ref):
        @pl.when(pl.program_id(0) == 0)
        def _():
            dvg_ref[...] = jnp.zeros_like(dvg_ref)
            dws_ref[...] = jnp.zeros_like(dws_ref)
            dbst_ref[...] = jnp.zeros_like(dbst_ref)

        keep = _causal(GROUP)
        keep_t = lax.broadcasted_iota(jnp.int32, (GROUP, GROUP), 0) <= lax.broadcasted_iota(jnp.int32, (GROUP, GROUP), 1)
        for g in range(ng):
            lo = g * GROUP
            u, du = _gelu_and_grad(p_ref[:, lo:lo + GROUP])
            v, dv_act = _gelu_and_grad(p_ref[:, da + lo:da + lo + GROUP])
            gain = vg_ref[:, lo:lo + GROUP]
            r = lax.rsqrt(jnp.mean(v * v, axis=-1, keepdims=True) + NORM_EPS)
            vh = v * r
            vnb = (vh * gain).astype(BF)
            w = jnp.where(keep, ws_ref[g], 0.0).astype(BF)
            wt = jnp.where(keep_t, wst_ref[g], 0.0).astype(BF)
            mixed = jnp.dot(w, vnb, preferred_element_type=F32) + bst_ref[:, g:g + 1]
            dout_g = do_ref[:, lo:lo + GROUP].astype(F32)
            dmixed = dout_g * u
            dmb = dmixed.astype(BF)
            dbst_ref[:, g:g + 1] += jnp.sum(dmixed, axis=1, keepdims=True)
            dws_ref[g] += jnp.where(keep, lax.dot_general(dmb, vnb, _NT, preferred_element_type=F32), 0.0)
            dvn = jnp.dot(wt, dmb, preferred_element_type=F32)
            dvg_ref[:, lo:lo + GROUP] += jnp.sum(dvn * vh, axis=0, keepdims=True)
            dvh = dvn * gain
            dv = r * dvh - vh * (jnp.sum(vh * dvh, axis=-1, keepdims=True) * (r / GROUP))
            dp_ref[:, lo:lo + GROUP] = (dout_g * mixed * du).astype(BF)
            dp_ref[:, da + lo:da + lo + GROUP] = (dv * dv_act).astype(BF)

    full = lambda *shape: pl.BlockSpec(shape, lambda i: (0,) * len(shape))
    outs = _call(
        body, name=name, grid=(t // GROUP,),
        in_specs=[pl.BlockSpec((GROUP, da2), lambda i: (i, 0)), full(1, da), full(ng, GROUP, GROUP), full(ng, GROUP, GROUP),
                  full(GROUP, ng), pl.BlockSpec((GROUP, da), lambda i: (i, 0))],
        out_specs=[pl.BlockSpec((GROUP, da2), lambda i: (i, 0)), full(1, da), full(ng, GROUP, GROUP), full(GROUP, ng)],
        out_shape=[jax.ShapeDtypeStruct((t, da2), BF), jax.ShapeDtypeStruct((1, da), F32),
                   jax.ShapeDtypeStruct((ng, GROUP, GROUP), F32), jax.ShapeDtypeStruct((GROUP, ng), F32)],
        operands=(p, v_gain, w_s, w_st, b_st, dout), sem=("arbitrary",), carries=carries)
    return outs[:4], outs[4:]


_CONV_ROWS = 256
_CONV_COLS = 128


def _shift_down(x, prev, k):
    top = pltpu.roll(jnp.concatenate([prev, x[:8]], axis=0), k, 0)[8:16]
    return jnp.concatenate([top, pltpu.roll(x, k, 0)[8:]], axis=0)


def _shift_up(x, nxt, k):
    n = x.shape[0]
    bottom = pltpu.roll(jnp.concatenate([x[n - 8:], nxt], axis=0), 16 - k, 0)[0:8]
    return jnp.concatenate([pltpu.roll(x, n - k, 0)[:n - 8], bottom], axis=0)


def _conv_taps(a_ref, half, r0, first):
    x = a_ref[half, pl.ds(r0, _rows(a_ref)), :]
    prev = a_ref[half, pl.ds(jnp.maximum(r0 - 8, 0), 8), :]
    prev = jnp.where(first, 0.0, prev)
    return x, _shift_down(x, prev, 1), _shift_down(x, prev, 2)


def _rows(a_ref):
    return min(_CONV_ROWS, a_ref.shape[1])


def _conv_fwd(name, a3, cw, cb, carries=()):
    _, t, f = a3.shape
    tc, rows = _CONV_COLS, min(_CONV_ROWS, t)

    def body(a_ref, cw_ref, cb_ref, y_ref):
        def step(r, carry):
            r0 = pl.multiple_of(r * rows, rows)
            c = []
            for half in range(2):
                x, x1, x2 = _conv_taps(a_ref, half, r0, r == 0)
                w = cw_ref[half]
                c.append(cb_ref[half] + (w[0:1] * x2 + w[1:2] * x1 + w[2:3] * x))
            y_ref[pl.ds(r0, rows), :] = (c[0] * jax.nn.sigmoid(c[0]) * c[1]).astype(BF)
            return carry

        lax.fori_loop(0, t // rows, step, 0)

    col = lambda *lead: pl.BlockSpec((*lead, tc), lambda j: (0,) * len(lead) + (j,))
    y, *dsts = _call(
        body, name=name, grid=(f // tc,), in_specs=[col(2, t), col(2, CONV_TAPS), col(2, 1)],
        out_specs=[col(t)], out_shape=[jax.ShapeDtypeStruct((t, f), BF)], operands=(a3, cw, cb),
        sem=("parallel",), vmem=40 * _MIB, carries=carries)
    return y, dsts


def _conv_bwd(name, a3, cw, cb, dy, carries=()):
    _, t, f = a3.shape
    tc, rows = _CONV_COLS, min(_CONV_ROWS, t)
    n_steps = t // rows

    def body(a_ref, cw_ref, cb_ref, dy_ref, da_ref, dcw_ref, dcb_ref):
        dcw_ref[...] = jnp.zeros_like(dcw_ref)
        dcb_ref[...] = jnp.zeros_like(dcb_ref)

        def step(s, nxt):
            r = n_steps - 1 - s
            r0 = pl.multiple_of(r * rows, rows)
            taps, c = [], []
            for half in range(2):
                x, x1, x2 = _conv_taps(a_ref, half, r0, r == 0)
                w = cw_ref[half]
                taps.append((x2, x1, x))
                c.append(cb_ref[half] + (w[0:1] * x2 + w[1:2] * x1 + w[2:3] * x))
            gate, val = c
            sg = jax.nn.sigmoid(gate)
            dyv = dy_ref[pl.ds(r0, rows), :].astype(F32)
            dcs = (dyv * val * (sg * (1.0 + gate * (1.0 - sg))), dyv * (gate * sg))
            new_nxt = []
            for half in range(2):
                dc, w = dcs[half], cw_ref[half]
                dcb_ref[half] += jnp.sum(dc, axis=0, keepdims=True)
                for tap in range(CONV_TAPS):
                    dcw_ref[half, tap:tap + 1, :] += jnp.sum(dc * taps[half][tap], axis=0, keepdims=True)
                da = w[2:3] * dc + w[1:2] * _shift_up(dc, nxt[half], 1) + w[0:1] * _shift_up(dc, nxt[half], 2)
                da_ref[half, pl.ds(r0, rows), :] = da.astype(BF)
                new_nxt.append(dc[:8])
            return tuple(new_nxt)

        zeros = jnp.zeros((8, tc), F32)
        lax.fori_loop(0, n_steps, step, (zeros, zeros))

    col = lambda *lead: pl.BlockSpec((*lead, tc), lambda j: (0,) * len(lead) + (j,))
    outs = _call(
        body, name=name, grid=(f // tc,), in_specs=[col(2, t), col(2, CONV_TAPS), col(2, 1), col(t)],
        out_specs=[col(2, t), col(2, CONV_TAPS), col(2, 1)],
        out_shape=[jax.ShapeDtypeStruct((2, t, f), BF), jax.ShapeDtypeStruct((2, CONV_TAPS, f), F32),
                   jax.ShapeDtypeStruct((2, 1, f), F32)],
        operands=(a3, cw, cb, dy), sem=("parallel",), vmem=40 * _MIB, carries=carries)
    return outs[:3], outs[3:]


_ATT_BLOCK = 256


def _split_dot(x, tri):
    hi = x.astype(BF)
    lo = (x - hi.astype(F32)).astype(BF)
    return jnp.dot(hi, tri, preferred_element_type=F32) + jnp.dot(lo, tri, preferred_element_type=F32)


def _logits(qb, kb, diagonal):
    z = lax.dot_general(qb, kb, _NT, preferred_element_type=F32) * (1.0 / math.sqrt(GROUP))
    lb = jnp.minimum(z, 0.0) - jnp.log(1.0 + jnp.exp(-jnp.abs(z)))
    if not diagonal:
        return lb, lb - z, None
    n = z.shape[0]
    mask = lax.broadcasted_iota(jnp.int32, (n, n), 1) < lax.broadcasted_iota(jnp.int32, (n, n), 0)
    return lb, jnp.where(mask, lb - z, 0.0), mask


def _attn_fwd(name, q, k, v, carries=()):
    t, hd = q.shape
    blk = min(_ATT_BLOCK, t)

    def body(q_ref, k_ref, v_ref, o_ref, rest_ref, first_ref):
        h, i = pl.program_id(0), pl.program_id(1)
        qb = q_ref[...]
        ri = lax.broadcasted_iota(jnp.int32, (blk, blk), 0)
        ci = lax.broadcasted_iota(jnp.int32, (blk, blk), 1)
        tri = (ri >= ci).astype(BF)

        def tile(j, right, acc, diagonal):
            k0 = pl.multiple_of(j * blk, blk)
            lb, l1m, mask = _logits(qb, k_ref[pl.ds(k0, blk), :], diagonal)
            incl = _split_dot(l1m, tri)
            a = jnp.exp(lb + (incl - l1m + right))
            if diagonal:
                a = jnp.where(mask, a, 0.0)
            acc = acc + jnp.dot(a.astype(BF), v_ref[pl.ds(k0, blk), :], preferred_element_type=F32)
            return right + incl[:, 0:1], acc

        right, acc = tile(i, jnp.zeros((blk, 1), F32), jnp.zeros((blk, GROUP), F32), True)

        def more(carry):
            j, right, _ = carry
            return jnp.logical_and(j >= 0, jnp.max(right) > EXP_FLOOR)

        def step(carry):
            j, right, acc = carry
            right, acc = tile(j, right, acc, False)
            return j - 1, right, acc

        j, right, acc = lax.while_loop(more, step, (i - 1, right, acc))
        o_ref[...] = acc.astype(BF)
        rest_ref[...] = jnp.broadcast_to(right, (blk, GROUP))
        first_ref[h, i] = (j + 1).astype(F32)

    qspec = pl.BlockSpec((blk, GROUP), lambda h, i: (i, h))
    kvspec = pl.BlockSpec((t, GROUP), lambda h, i: (0, h))
    outs = _call(
        body, name=name, grid=(hd // GROUP, t // blk), in_specs=[qspec, kvspec, kvspec],
        out_specs=[qspec, qspec, pl.BlockSpec(memory_space=pltpu.SMEM)],
        out_shape=[jax.ShapeDtypeStruct((t, hd), BF), jax.ShapeDtypeStruct((t, hd), F32),
                   jax.ShapeDtypeStruct((hd // GROUP, t // blk), F32)],
        operands=(q, k, v), sem=("arbitrary", "arbitrary"), vmem=32 * _MIB, carries=carries, middle_at=0.75)
    return outs[:3], outs[3:]


def _attn_bwd(name, q, k, v, rest, first, do, carries=()):
    t, hd = q.shape
    blk = min(_ATT_BLOCK, t)
    nq = t // blk
    scale = 1.0 / math.sqrt(GROUP)

    def body(first_ref, q_ref, k_ref, v_ref, rest_ref, do_ref, dq_ref, dk_ref, dv_ref, dk_acc, dv_acc):
        h, i = pl.program_id(0), pl.program_id(1)

        @pl.when(i == 0)
        def _():
            dk_acc[...] = jnp.zeros_like(dk_acc)
            dv_acc[...] = jnp.zeros_like(dv_acc)

        qb, dob = q_ref[...], do_ref[...]
        total = rest_ref[:, 0:1]
        ri = lax.broadcasted_iota(jnp.int32, (blk, blk), 0)
        ci = lax.broadcasted_iota(jnp.int32, (blk, blk), 1)
        tri = (ri <= ci).astype(BF)

        def tile(j, left, gleft, dq, diagonal):
            k0 = pl.multiple_of(j * blk, blk)
            kb, vb = k_ref[pl.ds(k0, blk), :], v_ref[pl.ds(k0, blk), :]
            lb, l1m, mask = _logits(qb, kb, diagonal)
            pre = _split_dot(l1m, tri)
            a = jnp.exp(lb + (total - left - pre))
            if diagonal:
                a = jnp.where(mask, a, 0.0)
            g = a * lax.dot_general(dob, vb, _NT, preferred_element_type=F32)
            gpre = _split_dot(g, tri)
            beta = jnp.exp(lb)
            dz = (g * (1.0 - beta) - (gleft + gpre - g) * beta) * scale
            if diagonal:
                dz = jnp.where(mask, dz, 0.0)
            dzb, ab = dz.astype(BF), a.astype(BF)
            dk_acc[pl.ds(k0, blk), :] += lax.dot_general(dzb, qb, _TN, preferred_element_type=F32)
            dv_acc[pl.ds(k0, blk), :] += lax.dot_general(ab, dob, _TN, preferred_element_type=F32)
            dq = dq + jnp.dot(dzb, kb, preferred_element_type=F32)
            return left + pre[:, blk - 1:blk], gleft + gpre[:, blk - 1:blk], dq

        zero = jnp.zeros((blk, 1), F32)
        first_block = jnp.clip(first_ref[h, i].astype(jnp.int32), 0, i)
        carry = lax.fori_loop(first_block, i, lambda j, c: tile(j, *c, False), (zero, zero, jnp.zeros((blk, GROUP), F32)))
        _, _, dq = tile(i, *carry, True)
        dq_ref[...] = dq.astype(BF)

        @pl.when(i == nq - 1)
        def _():
            dk_ref[...] = dk_acc[...].astype(BF)
            dv_ref[...] = dv_acc[...].astype(BF)

    qspec = pl.BlockSpec((blk, GROUP), lambda h, i: (i, h))
    kvspec = pl.BlockSpec((t, GROUP), lambda h, i: (0, h))
    outs = _call(
        body, name=name, grid=(hd // GROUP, nq),
        in_specs=[pl.BlockSpec(memory_space=pltpu.SMEM), qspec, kvspec, kvspec, qspec, qspec],
        out_specs=[qspec, kvspec, kvspec], out_shape=[jax.ShapeDtypeStruct((t, hd), BF)] * 3,
        operands=(first, q, k, v, rest, do), scratch=[pltpu.VMEM((t, GROUP), F32), pltpu.VMEM((t, GROUP), F32)],
        sem=("arbitrary", "arbitrary"), vmem=40 * _MIB, carries=carries)
    return outs[:3], outs[3:]


def _adamw_math(w, g, m, v):
    m = ADAM_B1 * m + (1.0 - ADAM_B1) * g
    v = ADAM_B2 * v + (1.0 - ADAM_B2) * (g * g)
    m_hat = m / (1.0 - ADAM_B1 ** ADAM_STEP)
    v_hat = v / (1.0 - ADAM_B2 ** ADAM_STEP)
    return -ADAM_LR * (m_hat / (jnp.sqrt(v_hat) + ADAM_EPS) + ADAM_WD * w), m, v


def _sum_adamw(name, parts, w, m, v):
    _, r, c = parts.shape
    tr = r if r * c <= 512 * 1024 else _tile_rows(r, max(8, (512 * 1024 // c) // 8 * 8))

    def body(p_ref, w_ref, m_ref, v_ref, g_ref, d_ref, nm_ref, nv_ref):
        g = p_ref[0].astype(F32)
        for dev in range(1, N_DEV):
            g = g + p_ref[dev].astype(F32)
        g_ref[...] = g
        d_ref[...], nm_ref[...], nv_ref[...] = _adamw_math(w_ref[...], g, m_ref[...], v_ref[...])

    row = pl.BlockSpec((tr, c), lambda i: (i, 0))
    return pl.pallas_call(
        body, name=name, grid=(r // tr,),
        in_specs=[pl.BlockSpec((N_DEV, tr, c), lambda i: (0, i, 0)), row, row, row], out_specs=[row] * 4,
        out_shape=[jax.ShapeDtypeStruct((r, c), F32)] * 4, compiler_params=_params(("parallel",), 40 * _MIB),
    )(parts, w, m, v)


def _tile_rows(r, pref):
    t = min(r, pref)
    while r % t or t % 8:
        t -= 1
    return t


def _exchange(name, carries):
    return _call(lambda: None, name=name, grid=(1,), in_specs=[], out_specs=[], out_shape=[], operands=(), carries=carries)


def _all_reduce_small(name, groups):
    c = groups[0][0].shape[1]
    parts, offsets, starts, r = [], [], [], 0
    for group in groups:
        starts.append(r)
        for p in group:
            parts.append(p)
            offsets.append(r)
            r += p.shape[0]
        r = -(-r // 8) * 8
    n = len(parts)

    def body(*refs):
        out_ref, slots, send_sems, recv_sems = refs[n:]
        x, y, c_, me = _place()
        slots[me] = jnp.zeros((r, c), F32)
        for p_ref, off in zip(refs[:n], offsets):
            slots[me, off:off + p_ref.shape[0], :] = p_ref[...]

        def copy(k):
            return pltpu.make_async_remote_copy(
                src_ref=slots.at[me], dst_ref=slots.at[me], send_sem=send_sems.at[k - 1], recv_sem=recv_sems.at[k - 1],
                device_id=_flip(x, y, c_, k), device_id_type=MESH)

        for k in range(1, N_DEV):
            copy(k).start()
        for k in range(1, N_DEV):
            copy(k).wait()
        total = slots[0]
        for dev in range(1, N_DEV):
            total = total + slots[dev]
        out_ref[...] = total

    vm = pl.BlockSpec(memory_space=pltpu.VMEM)
    summed = pl.pallas_call(
        body, name=name, in_specs=[vm] * n, out_specs=vm, out_shape=jax.ShapeDtypeStruct((r, c), F32),
        scratch_shapes=[pltpu.VMEM((N_DEV, r, c), F32), pltpu.SemaphoreType.DMA((7,)), pltpu.SemaphoreType.DMA((7,))],
        compiler_params=_params(None, (N_DEV + 6) * r * c * 4 + 8 * _MIB),
    )(*parts)
    return summed, starts


def _adamw(name, g, w, m, v):
    cols = w.shape[-1]
    flat = lambda a: a.reshape(-1, cols)

    def body(g_ref, w_ref, m_ref, v_ref, d_ref, nm_ref, nv_ref):
        d_ref[...], nm_ref[...], nv_ref[...] = _adamw_math(w_ref[...], g_ref[...], m_ref[...], v_ref[...])

    outs = pl.pallas_call(body, name=name, out_shape=[jax.ShapeDtypeStruct(flat(w).shape, F32)] * 3)(
        flat(g), flat(w), flat(m), flat(v))
    return [o.reshape(w.shape) for o in outs]


def kernel(x, pre_mix_g, post_mix_g, pre_ffn_g, post_ffn_g, a_w_in, a_v_norm_g, a_w_spatial, a_b_spatial, a_w_out, kv_norm_g, w_k, w_v, b_w_q, b_w_o, ffn_w_up, ffn_conv_w, ffn_conv_b, ffn_w_down, loss_target, m_pre_mix_g, m_post_mix_g, m_pre_ffn_g, m_post_ffn_g, m_a_w_in, m_a_v_norm_g, m_a_w_spatial, m_a_b_spatial, m_a_w_out, m_kv_norm_g, m_w_k, m_w_v, m_b_w_q, m_b_w_o, m_ffn_w_up, m_ffn_conv_w, m_ffn_conv_b, m_ffn_w_down, v_pre_mix_g, v_post_mix_g, v_pre_ffn_g, v_post_ffn_g, v_a_w_in, v_a_v_norm_g, v_a_w_spatial, v_a_b_spatial, v_a_w_out, v_kv_norm_g, v_w_k, v_w_v, v_b_w_q, v_b_w_o, v_ffn_w_up, v_ffn_conv_w, v_ffn_conv_b, v_ffn_w_down):
    t, d = x.shape[1], x.shape[2]
    f = ffn_w_down.shape[1] * N_DEV
    ng = d // GROUP
    me = 4 * lax.axis_index("x") + 2 * lax.axis_index("y") + lax.axis_index("c")
    x2, target = x.reshape(t, d), loss_target.reshape(t, d)

    g_in, g_cw, g_vg = _exchange("gather_first", [
        _gather(a_w_in[0].astype(BF)), _gather(ffn_conv_w.reshape(2 * CONV_TAPS, -1)), _gather(a_v_norm_g)])
    up0 = ffn_w_up[0].astype(BF)
    cw_full = jnp.transpose(g_cw.reshape(N_DEV, 2, CONV_TAPS, -1), (1, 2, 0, 3)).reshape(2, CONV_TAPS, 2, f)
    cw_l = [jnp.transpose(cw_full[l], (1, 0, 2)) for l in range(2)]
    cb_l = [ffn_conv_b[l].reshape(2, 1, f) for l in range(2)]
    vg_full = g_vg.reshape(1, d)
    w_s = a_w_spatial[0]
    w_st = jnp.swapaxes(w_s, 1, 2)
    b_st = a_b_spatial[0].T

    (hn0,) = _norms_fwd("pre_mix0", x2, [pre_mix_g[0]])
    p0, (g_out, g_up0) = _mm_nn_blk("sgu_in", hn0, g_in, F32, carries=[
        _gather(a_w_out[0].astype(BF)), _gather(up0, 0, d // 4)])
    w_out_f = g_out.reshape(d, d)
    sg, g_up0 = _sgu_fwd("sgu", p0, vg_full, w_s, b_st, carries=[_gather(up0, d // 4, d // 2, dst=g_up0)])
    mix0, (g_up0,) = _mm_nn("sgu_out", sg, w_out_f, F32, carries=[_gather(up0, d // 2, 3 * d // 4, dst=g_up0)])
    h1, fn0, g_up0 = _resid_norms("post_mix0", x2, mix0, post_mix_g[0], [pre_ffn_g[0]],
                                  carries=[_gather(up0, 3 * d // 4, d, dst=g_up0)])
    a3_0, (g_dn0,) = _mm_nn_blk("ffn0_up", fn0, g_up0, F32, halves=True, carries=[_gather(ffn_w_down[0].astype(BF))])
    y0, (g_q,) = _conv_fwd("ffn0_conv", a3_0, cw_l[0], cb_l[0], carries=[_gather(b_w_q[0].astype(BF))])
    f0, (g_k, g_v) = _mm_nn("ffn0_down", y0, g_dn0.reshape(f, d), F32,
                            carries=[_gather(w_k.astype(BF)), _gather(w_v.astype(BF))])
    w_q_f, w_k_f, w_v_f = g_q.reshape(d, d), g_k.reshape(d, d), g_v.reshape(d, d)
    h2, hn1, kvn = _resid_norms("post_ffn0", h1, f0, post_ffn_g[0], [pre_mix_g[1], kv_norm_g])
    q = _mm_nn("attn_q", hn1, w_q_f, BF)
    kk = _mm_nn("attn_k", kvn, w_k_f, BF)
    vv = _mm_nn("attn_v", kvn, w_v_f, BF)
    (att, rest, first), (g_o, g_up1) = _attn_fwd(
        "attn", q, kk, vv, carries=[_gather(b_w_o[0].astype(BF)), _gather(ffn_w_up[1].astype(BF))])
    w_o_f = g_o.reshape(d, d)
    mix1 = _mm_nn("attn_o", att, w_o_f, F32)
    h3, fn1 = _resid_norms("post_mix1", h2, mix1, post_mix_g[1], [pre_ffn_g[1]])
    a3_1, (g_dn1,) = _mm_nn_blk("ffn1_up", fn1, g_up1, F32, halves=True, carries=[_gather(ffn_w_down[1].astype(BF))])
    y1, _ = _conv_fwd("ffn1_conv", a3_1, cw_l[1], cb_l[1])
    f1 = _mm_nn("ffn1_down", y1, g_dn1.reshape(f, d), F32)
    g_up = (g_up0, g_up1)
    w_dn_f = (g_dn0.reshape(f, d), g_dn1.reshape(f, d))
    dh, loss_part = _resid_loss("loss", h3, f1, post_ffn_g[1], target)
    loss = lax.psum(loss_part[0, 0], ("x", "y", "c"))

    def blocks(dw):
        return dw.reshape(N_DEV, -1, d)

    def split(result, carries):
        return result if carries else (result, [])

    def ffn_bwd(l, dh_out, h_in, fn, a3, yv, fo, with_dw=(), with_dx=()):
        dfo, d_post = _norm_bwd(f"post_ffn{l}_bwd", fo, post_ffn_g[l], dh_out, out_dtype=BF)
        dw_dn, sent_dw = split(_mm_tn(f"ffn{l}_down_dw", yv, dfo, carries=with_dw), with_dw)
        dy, sent_dx = split(_mm_nt(f"ffn{l}_down_dx", dfo, w_dn_f[l], BF, carries=with_dx), with_dx)
        (da3, dcw, dcb), (p_dn,) = _conv_bwd(f"ffn{l}_conv_bwd", a3, cw_l[l], cb_l[l], dy, carries=[_scatter(blocks(dw_dn))])
        dw_up = _mm_tn_blk(f"ffn{l}_up_dw", fn, da3, halves=True)
        dfn, (p_up,) = _mm_nt_blk(f"ffn{l}_up_dx", da3, g_up[l], F32, halves=True, carries=[_scatter(dw_up, 0, d // 2)])
        dh_in, d_pre, p_up = _norm_bwd(f"pre_ffn{l}_bwd", h_in, pre_ffn_g[l], dfn, res=dh_out,
                                       carries=[_scatter(dw_up, d // 2, 9 * d // 16, dst=p_up)])
        return dh_in, d_post, d_pre, dcw, dcb, p_dn, dw_up, p_up, list(sent_dw) + list(sent_dx)

    dh3, d_post_ffn1, d_pre_ffn1, dcw1, dcb1, p_dn1, dw_up1, p_up1, _ = ffn_bwd(1, dh, h3, fn1, a3_1, y1, f1)
    dmix1, d_post_mix1 = _norm_bwd("post_mix1_bwd", mix1, post_mix_g[1], dh3, out_dtype=BF)
    dw_o = _mm_tn("attn_o_dw", att, dmix1)
    datt = _mm_nt("attn_o_dx", dmix1, w_o_f, BF)
    (dq, dk, dv), (p_up1, p_o) = _attn_bwd(
        "attn_bwd", q, kk, vv, rest, first, datt, carries=[_scatter(dw_up1, 9 * d // 16, d, dst=p_up1), _scatter(blocks(dw_o))])
    dw_q = _mm_tn("attn_q_dw", hn1, dq)
    dw_k = _mm_tn("attn_k_dw", kvn, dk)
    dw_v = _mm_tn("attn_v_dw", kvn, dv)
    dhn1 = _mm_nt("attn_q_dx", dq, w_q_f, F32)
    dkvn = _mm_nt("attn_v_dx", dv, w_v_f, F32, add=_mm_nt("attn_k_dx", dk, w_k_f, F32))
    dh2a, d_pre_mix1 = _norm_bwd("pre_mix1_bwd", h2, pre_mix_g[1], dhn1, res=dh3)
    dh2, d_kv = _norm_bwd("kv_norm_bwd", h2, kv_norm_g, dkvn, res=dh2a)
    dh1, d_post_ffn0, d_pre_ffn0, dcw0, dcb0, p_dn0, dw_up0, p_up0, (p_q, p_k, p_v) = ffn_bwd(
        0, dh2, h1, fn0, a3_0, y0, f0, with_dw=[_scatter(blocks(dw_q)), _scatter(blocks(dw_k))],
        with_dx=[_scatter(blocks(dw_v))])
    dmix0, d_post_mix0 = _norm_bwd("post_mix0_bwd", mix0, post_mix_g[0], dh1, out_dtype=BF)
    dw_out, (p_up0,) = _mm_tn("sgu_out_dw", sg, dmix0, carries=[_scatter(dw_up0, 9 * d // 16, 11 * d // 16, dst=p_up0)])
    dsg = _mm_nt("sgu_out_dx", dmix0, w_out_f, BF)
    (dp0, d_vg, d_ws, d_bst), (p_up0,) = _sgu_bwd(
        "sgu_bwd", p0, vg_full, w_s, w_st, b_st, dsg, carries=[_scatter(dw_up0, 11 * d // 16, 15 * d // 16, dst=p_up0)])
    dw_in, (p_out,) = _mm_tn_blk("sgu_in_dw", hn0, dp0, carries=[_scatter(blocks(dw_out))])
    dhn0, (p_in,) = _mm_nt_blk("sgu_in_dx", dp0, g_in, F32, carries=[_scatter(dw_in, 0, 5 * d // 8)])
    grad_x, d_pre_mix0, p_in = _norm_bwd("pre_mix0_bwd", x2, pre_mix_g[0], dhn0, res=dh1,
                                         carries=[_scatter(dw_in, 5 * d // 8, 7 * d // 8, dst=p_in)])
    p_up0, p_in = _exchange("scatter_last", [_scatter(dw_up0, 15 * d // 16, d, dst=p_up0),
                                             _scatter(dw_in, 7 * d // 8, d, dst=p_in)])

    def conv_w_grad(dcw):
        return jnp.transpose(dcw, (1, 0, 2)).reshape(CONV_TAPS, 2 * f)

    small = [
        ([d_pre_mix0, d_pre_mix1], (2, d)), ([d_post_mix0, d_post_mix1], (2, d)),
        ([d_pre_ffn0, d_pre_ffn1], (2, d)), ([d_post_ffn0, d_post_ffn1], (2, d)),
        ([d_kv], (d,)), ([d_vg], (1, d)), ([d_bst.T], (1, ng, GROUP)), ([d_ws], (1, ng, GROUP, GROUP)),
        ([dcb0, dcb1], (2, 2 * f)), ([conv_w_grad(dcw0), conv_w_grad(dcw1)], (2, CONV_TAPS, 2 * f)),
    ]
    width = V7X_LANES * math.gcd(d // V7X_LANES, 2 * f // V7X_LANES)
    summed, offsets = _all_reduce_small("reduce_small", [[a.reshape(-1, width) for a in group] for group, _ in small])
    full = [summed[off:off + math.prod(shape) // width].reshape(shape) for off, (_, shape) in zip(offsets, small)]
    g_pre_mix, g_post_mix, g_pre_ffn, g_post_ffn, g_kv, g_vgain, g_bs, g_ws, g_cb, g_cwf = full
    cw_w = 2 * f // N_DEV
    g_vgain = lax.dynamic_slice_in_dim(g_vgain, me * (d // N_DEV), d // N_DEV, axis=1)
    g_cwf = lax.dynamic_slice_in_dim(g_cwf, me * cw_w, cw_w, axis=2)

    parts = [p_in, p_out, p_k, p_v, p_q, p_o, p_up0, p_up1, p_dn0, p_dn1]

    def big_update(name, part, w, m, v):
        return _sum_adamw(name, part, w.reshape(part.shape[1:]), m.reshape(part.shape[1:]), v.reshape(part.shape[1:]))

    def small_update(name, g, w, m, v):
        return [g] + _adamw(name, g, w, m, v)

    def stacked(name, part0, part1, w, m, v):
        outs = [big_update(f"{name}{l}", p, w[l], m[l], v[l]) for l, p in enumerate((part0, part1))]
        return [jnp.stack([a, b]).reshape(w.shape) for a, b in zip(*outs)]

    def single(name, part, w, m, v):
        return [o.reshape(w.shape) for o in big_update(name, part, w, m, v)]

    results = {
        "pre_mix_g": small_update("adam_pre_mix", g_pre_mix, pre_mix_g, m_pre_mix_g, v_pre_mix_g),
        "post_mix_g": small_update("adam_post_mix", g_post_mix, post_mix_g, m_post_mix_g, v_post_mix_g),
        "pre_ffn_g": small_update("adam_pre_ffn", g_pre_ffn, pre_ffn_g, m_pre_ffn_g, v_pre_ffn_g),
        "post_ffn_g": small_update("adam_post_ffn", g_post_ffn, post_ffn_g, m_post_ffn_g, v_post_ffn_g),
        "a_w_in": single("adam_a_w_in", parts[0], a_w_in, m_a_w_in, v_a_w_in),
        "a_v_norm_g": small_update("adam_a_v_norm", g_vgain, a_v_norm_g, m_a_v_norm_g, v_a_v_norm_g),
        "a_w_spatial": small_update("adam_a_w_spatial", g_ws, a_w_spatial, m_a_w_spatial, v_a_w_spatial),
        "a_b_spatial": small_update("adam_a_b_spatial", g_bs, a_b_spatial, m_a_b_spatial, v_a_b_spatial),
        "a_w_out": single("adam_a_w_out", parts[1], a_w_out, m_a_w_out, v_a_w_out),
        "kv_norm_g": small_update("adam_kv_norm", g_kv, kv_norm_g, m_kv_norm_g, v_kv_norm_g),
        "w_k": single("adam_w_k", parts[2], w_k, m_w_k, v_w_k),
        "w_v": single("adam_w_v", parts[3], w_v, m_w_v, v_w_v),
        "b_w_q": single("adam_b_w_q", parts[4], b_w_q, m_b_w_q, v_b_w_q),
        "b_w_o": single("adam_b_w_o", parts[5], b_w_o, m_b_w_o, v_b_w_o),
        "ffn_w_up": stacked("adam_ffn_w_up", parts[6], parts[7], ffn_w_up, m_ffn_w_up, v_ffn_w_up),
        "ffn_conv_w": small_update("adam_ffn_conv_w", g_cwf, ffn_conv_w, m_ffn_conv_w, v_ffn_conv_w),
        "ffn_conv_b": small_update("adam_ffn_conv_b", g_cb, ffn_conv_b, m_ffn_conv_b, v_ffn_conv_b),
        "ffn_w_down": stacked("adam_ffn_w_down", parts[8], parts[9], ffn_w_down, m_ffn_w_down, v_ffn_w_down),
    }
    order = ["pre_mix_g", "post_mix_g", "pre_ffn_g", "post_ffn_g", "a_w_in", "a_v_norm_g", "a_w_spatial", "a_b_spatial",
             "a_w_out", "kv_norm_g", "w_k", "w_v", "b_w_q", "b_w_o", "ffn_w_up", "ffn_conv_w", "ffn_conv_b", "ffn_w_down"]
    outs = [loss, grad_x.reshape(x.shape)]
    for idx in range(4):
        outs += [results[n][idx] for n in order]
    return tuple(outs)
```

```python
import math
from typing import NamedTuple, Optional

import jax
import jax.numpy as jnp
from jax import lax
from jax.experimental import pallas as pl
from jax.experimental.pallas import tpu as pltpu

F32 = jnp.float32
BF = jnp.bfloat16
MESH = pl.DeviceIdType.MESH

N_DEV = 8
NORM_EPS = 1e-6
GROUP = 128
CONV_TAPS = 3
ADAM_LR, ADAM_B1, ADAM_B2, ADAM_EPS, ADAM_WD, ADAM_STEP = 0.001, 0.9, 0.999, 1e-08, 0.01, 10
EXP_FLOOR = -104.0

V7X_LANES = 128
V7X_VMEM_BYTES = 64 * 1024 * 1024
_MIB = 1024 * 1024

_NN = (((1,), (0,)), ((), ()))
_NT = (((1,), (1,)), ((), ()))
_TN = (((0,), (0,)), ((), ()))


def _tile(n, pref):
    if n <= pref:
        return n
    t = (pref // V7X_LANES) * V7X_LANES
    while t > V7X_LANES and n % t:
        t -= V7X_LANES
    assert n % t == 0, (n, pref)
    return t


def _nbytes(shape, dtype):
    return math.prod(shape) * jnp.dtype(dtype).itemsize


def _params(sem=None, vmem=None):
    kw = {}
    if sem is not None:
        kw["dimension_semantics"] = sem
    if vmem is not None:
        kw["vmem_limit_bytes"] = int(min(max(vmem, 16 * _MIB), V7X_VMEM_BYTES - 8 * _MIB))
    return pltpu.CompilerParams(**kw)


def _place():
    x, y, c = lax.axis_index("x"), lax.axis_index("y"), lax.axis_index("c")
    return x, y, c, 4 * x + 2 * y + c


def _flip(x, y, c, k):
    return (1 - x if k & 4 else x, 1 - y if k & 2 else y, 1 - c if k & 1 else c)


class _Carry(NamedTuple):
    gather: bool
    src: jax.Array
    dst: Optional[jax.Array]
    lo: int
    hi: int


def _gather(src, lo=0, hi=None, dst=None):
    return _Carry(True, src, dst, lo, src.shape[0] if hi is None else hi)


def _scatter(src, lo=0, hi=None, dst=None):
    return _Carry(False, src, dst, lo, src.shape[1] if hi is None else hi)


def _carry_phases(carries, srcs, dsts, send_sems, recv_sems, local_sems):
    x, y, c, me = _place()
    here, sibling = (x, y, c), (x, y, 1 - c)
    chips = [(1 - x, y), (x, 1 - y), (1 - x, 1 - y)]

    def rows(u):
        return pl.ds(carries[u].lo, carries[u].hi - carries[u].lo)

    def block_copy(u, sem, block, to, from_src=False):
        slot = dsts[u].at[4 * block[0] + 2 * block[1] + block[2], rows(u)]
        return pltpu.make_async_remote_copy(
            src_ref=srcs[u].at[rows(u)] if from_src else slot, dst_ref=slot, send_sem=send_sems.at[u, sem],
            recv_sem=recv_sems.at[u, sem], device_id=to, device_id_type=MESH)

    def partial_copy(u, k):
        peer = _flip(x, y, c, k)
        return pltpu.make_async_remote_copy(
            src_ref=srcs[u].at[4 * peer[0] + 2 * peer[1] + peer[2], rows(u)], dst_ref=dsts[u].at[me, rows(u)],
            send_sem=send_sems.at[u, k - 1], recv_sem=recv_sems.at[u, k - 1], device_id=peer, device_id_type=MESH)

    def local_copy(u):
        src = srcs[u].at[rows(u)] if carries[u].gather else srcs[u].at[me, rows(u)]
        return pltpu.make_async_copy(src, dsts[u].at[me, rows(u)], local_sems.at[u])

    def first():
        for u, cr in enumerate(carries):
            local_copy(u).start()
            if cr.gather:
                block_copy(u, 0, here, sibling, from_src=True).start()
                for j, chip in enumerate(chips):
                    block_copy(u, 1 + j, here, (*chip, c), from_src=True).start()
            else:
                for k in range(1, N_DEV):
                    partial_copy(u, k).start()

    def middle():
        for u, cr in enumerate(carries):
            if cr.gather:
                for j, chip in enumerate(chips):
                    block_copy(u, 1 + j, (*chip, c), here).wait_recv()
                    block_copy(u, 4 + j, (*chip, c), sibling).start()

    def last():
        for u, cr in enumerate(carries):
            if cr.gather:
                block_copy(u, 0, sibling, here).wait_recv()
                for j, chip in enumerate(chips):
                    block_copy(u, 4 + j, (*chip, 1 - c), here).wait_recv()
                block_copy(u, 0, here, sibling, from_src=True).wait_send()
                for j, chip in enumerate(chips):
                    block_copy(u, 1 + j, here, (*chip, c), from_src=True).wait_send()
                    block_copy(u, 4 + j, (*chip, c), sibling).wait_send()
            else:
                for k in range(1, N_DEV):
                    partial_copy(u, k).wait()
            local_copy(u).wait()

    return first, middle, last


def _call(body, *, name, grid, in_specs, out_specs, out_shape, operands, scratch=(), sem=None, vmem=None,
          carries=(), middle_at=0.6):
    if not carries:
        return pl.pallas_call(
            body, name=name, grid=grid, in_specs=in_specs, out_specs=out_specs, out_shape=out_shape,
            scratch_shapes=list(scratch), compiler_params=_params(sem, vmem))(*operands)
    n_in, n_out, n_scr, nc = len(in_specs), len(out_specs), len(scratch), len(carries)
    given = [u for u, cr in enumerate(carries) if cr.dst is not None]
    steps = math.prod(grid)
    middle_step = min(steps - 1, int(steps * middle_at))

    def wrapped(*refs):
        ins, srcs = refs[:n_in], refs[n_in:n_in + nc]
        at = n_in + nc + len(given)
        outs, dsts = refs[at:at + n_out], refs[at + n_out:at + n_out + nc]
        at += n_out + nc
        scr, (send_sems, recv_sems, local_sems) = refs[at:at + n_scr], refs[at + n_scr:]
        first, middle, last = _carry_phases(carries, srcs, dsts, send_sems, recv_sems, local_sems)
        step = 0
        for axis, size in enumerate(grid):
            step = step * size + pl.program_id(axis)
        pl.when(step == 0)(first)
        body(*ins, *outs, *scr)
        pl.when(step == middle_step)(middle)
        pl.when(step == steps - 1)(last)

    any_spec = pl.BlockSpec(memory_space=pl.ANY)
    dst_shapes = [jax.ShapeDtypeStruct((N_DEV, *cr.src.shape) if cr.gather else cr.src.shape, cr.src.dtype) for cr in carries]
    return pl.pallas_call(
        wrapped, name=name, grid=grid, in_specs=list(in_specs) + [any_spec] * (nc + len(given)),
        out_specs=list(out_specs) + [any_spec] * nc, out_shape=list(out_shape) + dst_shapes,
        input_output_aliases={n_in + nc + g: n_out + u for g, u in enumerate(given)},
        scratch_shapes=list(scratch) + [pltpu.SemaphoreType.DMA((nc, 7)), pltpu.SemaphoreType.DMA((nc, 7)),
                                        pltpu.SemaphoreType.DMA((nc,))],
        compiler_params=_params(("arbitrary",) * len(grid), vmem),
    )(*operands, *[cr.src for cr in carries], *[carries[u].dst for u in given])


def _mm(name, a, b, *, dims, grid, a_blk, a_map, b_blk, b_map, o_blk, o_map, out_shape, out_dtype,
        add=None, add_blk=None, add_map=None, carries=(), b_slabs=1):
    nk = grid[2]
    assert add is None or nk == 1
    acc_shape = tuple(d for d in o_blk if d is not None)
    in_place = out_dtype == F32

    def body(*refs):
        if add is None:
            a_ref, b_ref, o_ref = refs[:3]
            c_ref, scr = None, refs[3:]
        else:
            a_ref, b_ref, c_ref, o_ref = refs[:4]
            scr = refs[4:]
        if b_slabs == 1:
            part = lax.dot_general(a_ref[...], b_ref[...], dims, preferred_element_type=F32)
        else:
            cw = b_ref.shape[2]
            part = sum(lax.dot_general(a_ref[:, s * cw:(s + 1) * cw], b_ref[s], dims, preferred_element_type=F32)
                       for s in range(b_slabs))
        if c_ref is not None:
            part = part + c_ref[...].astype(F32)
        if nk == 1:
            o_ref[...] = part.astype(o_ref.dtype)
            return
        acc = o_ref if in_place else scr[0]
        k = pl.program_id(2)

        @pl.when(k == 0)
        def _():
            acc[...] = part

        @pl.when(k > 0 if in_place else jnp.logical_and(k > 0, k < nk - 1))
        def _():
            acc[...] += part

        if not in_place:
            @pl.when(k == nk - 1)
            def _():
                o_ref[...] = (acc[...] + part).astype(o_ref.dtype)

    in_specs = [pl.BlockSpec(a_blk, a_map), pl.BlockSpec(b_blk, b_map)]
    operands = [a, b]
    scratch = [pltpu.VMEM(acc_shape, F32)] if nk > 1 and not in_place else []
    vmem = 2 * (_nbytes(acc_shape, out_dtype) + _nbytes([d for d in a_blk if d], a.dtype)
                + _nbytes([d for d in b_blk if d], b.dtype)) + (2 + len(scratch)) * _nbytes(acc_shape, F32)
    if add is not None:
        in_specs.append(pl.BlockSpec(add_blk, add_map))
        operands.append(add)
        vmem += 2 * _nbytes(acc_shape, add.dtype)
    out, *dsts = _call(
        body, name=name, grid=grid, in_specs=in_specs, out_specs=[pl.BlockSpec(o_blk, o_map)],
        out_shape=[jax.ShapeDtypeStruct(out_shape, out_dtype)], operands=operands, scratch=scratch,
        sem=("parallel", "parallel", "arbitrary"), vmem=vmem + 8 * _MIB, carries=carries)
    return (out, dsts) if carries else out


def _mm_nn(name, x, w, out_dtype, carries=()):
    t, kd = x.shape
    n = w.shape[1]
    tm, tn, tk = _tile(t, 1024), _tile(n, 1024), _tile(kd, 1536 if kd > 2048 else 2048)
    return _mm(name, x, w, dims=_NN, grid=(t // tm, n // tn, kd // tk),
               a_blk=(tm, tk), a_map=lambda i, j, k: (i, k), b_blk=(tk, tn), b_map=lambda i, j, k: (k, j),
               o_blk=(tm, tn), o_map=lambda i, j, k: (i, j), out_shape=(t, n), out_dtype=out_dtype, carries=carries)


def _mm_nn_blk(name, x, g, out_dtype, halves=False, carries=()):
    t, kd = x.shape
    cw = g.shape[2]
    tm = _tile(t, 1024)
    if halves:
        o_blk, o_map, out_shape = (None, tm, cw), (lambda i, j, k: (j // 4, i, j % 4)), (2, t, 4 * cw)
    else:
        o_blk, o_map, out_shape = (tm, cw), (lambda i, j, k: (i, j)), (t, N_DEV * cw)
    return _mm(name, x, g, dims=_NN, grid=(t // tm, N_DEV, 1),
               a_blk=(tm, kd), a_map=lambda i, j, k: (i, 0), b_blk=(None, kd, cw), b_map=lambda i, j, k: (j, 0, 0),
               o_blk=o_blk, o_map=o_map, out_shape=out_shape, out_dtype=out_dtype, carries=carries)


def _mm_nt(name, dy, w, out_dtype, add=None, carries=()):
    t, n = dy.shape
    kd = w.shape[0]
    tm, tn = _tile(t, 1024), _tile(kd, 512)
    kw = {}
    if add is not None:
        kw = dict(add=add, add_blk=(tm, tn), add_map=lambda i, j, k: (i, j))
    return _mm(name, dy, w, dims=_NT, grid=(t // tm, kd // tn, 1),
               a_blk=(tm, n), a_map=lambda i, j, k: (i, 0), b_blk=(tn, n), b_map=lambda i, j, k: (j, 0),
               o_blk=(tm, tn), o_map=lambda i, j, k: (i, j), out_shape=(t, kd), out_dtype=out_dtype, carries=carries, **kw)


def _mm_nt_blk(name, dy, g, out_dtype, halves=False, carries=()):
    kd, cw = g.shape[1], g.shape[2]
    t = dy.shape[1] if halves else dy.shape[0]
    tm = _tile(t, 512)
    slabs = 2 if cw > 512 else 4
    nk = N_DEV // slabs
    if halves:
        a_blk, a_map = (None, tm, slabs * cw), (lambda i, j, k: (k // (nk // 2), i, k % (nk // 2)))
    else:
        a_blk, a_map = (tm, slabs * cw), (lambda i, j, k: (i, k))
    return _mm(name, dy, g, dims=_NT, grid=(t // tm, 1, nk), b_slabs=slabs,
               a_blk=a_blk, a_map=a_map, b_blk=(slabs, kd, cw), b_map=lambda i, j, k: (k, 0, 0),
               o_blk=(tm, kd), o_map=lambda i, j, k: (i, 0), out_shape=(t, kd), out_dtype=out_dtype, carries=carries)


def _mm_tn(name, x, dy, carries=()):
    t, kd = x.shape
    n = dy.shape[1]
    tmx, tk = _tile(kd, 1024), _tile(t, 2048)
    return _mm(name, x, dy, dims=_TN, grid=(kd // tmx, 1, t // tk),
               a_blk=(tk, tmx), a_map=lambda i, j, k: (k, i), b_blk=(tk, n), b_map=lambda i, j, k: (k, 0),
               o_blk=(tmx, n), o_map=lambda i, j, k: (i, 0), out_shape=(kd, n), out_dtype=BF, carries=carries)


def _mm_tn_blk(name, x, dy, halves=False, carries=()):
    t, kd = x.shape
    cw = dy.shape[2] // 4 if halves else dy.shape[1] // N_DEV
    tmx, tk = _tile(kd, 2048 if cw <= 512 else 1024), _tile(t, 2048)
    if halves:
        b_blk, b_map = (None, tk, cw), (lambda i, j, k: (j // 4, k, j % 4))
    else:
        b_blk, b_map = (tk, cw), (lambda i, j, k: (k, j))
    return _mm(name, x, dy, dims=_TN, grid=(kd // tmx, N_DEV, t // tk),
               a_blk=(tk, tmx), a_map=lambda i, j, k: (k, i), b_blk=b_blk, b_map=b_map,
               o_blk=(None, tmx, cw), o_map=lambda i, j, k: (j, i, 0), out_shape=(N_DEV, kd, cw), out_dtype=BF,
               carries=carries)


def _rms(x, g):
    r = lax.rsqrt(jnp.mean(x * x, axis=-1, keepdims=True) + NORM_EPS)
    return x * r * g


def _rms_bwd_math(x, g, dy):
    d = x.shape[-1]
    r = lax.rsqrt(jnp.mean(x * x, axis=-1, keepdims=True) + NORM_EPS)
    xh = x * r
    u = dy * g
    dx = r * u - xh * (jnp.sum(xh * u, axis=-1, keepdims=True) * (r / d))
    return dx, jnp.sum(dy * xh, axis=0, keepdims=True)


def _row_specs(tr, d, n):
    return [pl.BlockSpec((tr, d), lambda i: (i, 0)) for _ in range(n)]


def _vec_specs(d, n):
    return [pl.BlockSpec((1, d), lambda i: (0, 0)) for _ in range(n)]


def _norms_fwd(name, h, gains):
    t, d = h.shape
    tr, ng = _tile(t, 256), len(gains)

    def body(h_ref, *refs):
        x = h_ref[...]
        for g_ref, o_ref in zip(refs[:ng], refs[ng:]):
            o_ref[...] = _rms(x, g_ref[...]).astype(BF)

    return pl.pallas_call(
        body, name=name, grid=(t // tr,), in_specs=_row_specs(tr, d, 1) + _vec_specs(d, ng),
        out_specs=_row_specs(tr, d, ng), out_shape=[jax.ShapeDtypeStruct((t, d), BF)] * ng,
        compiler_params=_params(("parallel",)),
    )(h, *[g.reshape(1, d) for g in gains])


def _resid_norms(name, h, m, g_post, gains, carries=()):
    t, d = h.shape
    tr, ng = _tile(t, 256), len(gains)

    def body(h_ref, m_ref, gp_ref, *refs):
        hn = h_ref[...] + _rms(m_ref[...], gp_ref[...])
        refs[ng][...] = hn
        for g_ref, o_ref in zip(refs[:ng], refs[ng + 1:]):
            o_ref[...] = _rms(hn, g_ref[...]).astype(BF)

    return _call(
        body, name=name, grid=(t // tr,), in_specs=_row_specs(tr, d, 2) + _vec_specs(d, 1 + ng),
        out_specs=_row_specs(tr, d, 1 + ng),
        out_shape=[jax.ShapeDtypeStruct((t, d), F32)] + [jax.ShapeDtypeStruct((t, d), BF)] * ng,
        operands=(h, m, g_post.reshape(1, d), *[g.reshape(1, d) for g in gains]), sem=("parallel",), carries=carries)


def _resid_loss(name, h, m, g_post, target):
    t, d = h.shape
    tr = _tile(t, 256)

    def body(h_ref, m_ref, gp_ref, t_ref, dy_ref, loss_ref):
        diff = h_ref[...] + _rms(m_ref[...], gp_ref[...]) - t_ref[...]
        dy_ref[...] = diff * (1.0 / d)

        @pl.when(pl.program_id(0) == 0)
        def _():
            loss_ref[...] = jnp.zeros_like(loss_ref)

        per_row = jnp.sum(diff * diff, axis=-1, keepdims=True) * (1.0 / d)
        loss_ref[...] += 0.5 * jnp.sum(per_row, axis=0, keepdims=True)

    return pl.pallas_call(
        body, name=name, grid=(t // tr,),
        in_specs=_row_specs(tr, d, 2) + _vec_specs(d, 1) + _row_specs(tr, d, 1),
        out_specs=[pl.BlockSpec((tr, d), lambda i: (i, 0)), pl.BlockSpec((1, 1), lambda i: (0, 0))],
        out_shape=[jax.ShapeDtypeStruct((t, d), F32), jax.ShapeDtypeStruct((1, 1), F32)],
        compiler_params=_params(("arbitrary",)),
    )(h, m, g_post.reshape(1, d), target)


def _norm_bwd(name, x, g, dy, res=None, out_dtype=F32, carries=()):
    t, d = x.shape
    tr = _tile(t, 256)
    has_res = res is not None

    def body(x_ref, dy_ref, g_ref, *refs):
        dx_ref, dg_ref = refs[-2:]
        dx, dg = _rms_bwd_math(x_ref[...].astype(F32), g_ref[...], dy_ref[...].astype(F32))
        if has_res:
            dx = dx + refs[0][...]
        dx_ref[...] = dx.astype(dx_ref.dtype)

        @pl.when(pl.program_id(0) == 0)
        def _():
            dg_ref[...] = jnp.zeros_like(dg_ref)

        dg_ref[...] += dg

    ops = [x, dy, g.reshape(1, d)] + ([res] if has_res else [])
    return _call(
        body, name=name, grid=(t // tr,),
        in_specs=_row_specs(tr, d, 2) + _vec_specs(d, 1) + _row_specs(tr, d, int(has_res)),
        out_specs=[pl.BlockSpec((tr, d), lambda i: (i, 0)), pl.BlockSpec((1, d), lambda i: (0, 0))],
        out_shape=[jax.ShapeDtypeStruct((t, d), out_dtype), jax.ShapeDtypeStruct((1, d), F32)],
        operands=ops, sem=("arbitrary",), carries=carries)


_GELU_C = math.sqrt(2.0 / math.pi)
_GELU_A = 0.044715


def _gelu(x):
    return 0.5 * x * (1.0 + jnp.tanh(_GELU_C * (x + _GELU_A * x * x * x)))


def _gelu_and_grad(x):
    th = jnp.tanh(_GELU_C * (x + _GELU_A * x * x * x))
    grad = 0.5 * (1.0 + th) + 0.5 * x * (1.0 - th * th) * (_GELU_C * (1.0 + 3.0 * _GELU_A * x * x))
    return 0.5 * x * (1.0 + th), grad


def _causal(n):
    return lax.broadcasted_iota(jnp.int32, (n, n), 1) <= lax.broadcasted_iota(jnp.int32, (n, n), 0)


def _sgu_fwd(name, p, v_gain, w_s, b_st, carries=()):
    t, da2 = p.shape
    da = da2 // 2
    ng = da // GROUP

    def body(p_ref, vg_ref, ws_ref, bst_ref, o_ref):
        keep = _causal(GROUP)
        for g in range(ng):
            lo = g * GROUP
            u = _gelu(p_ref[:, lo:lo + GROUP])
            vn = _rms(_gelu(p_ref[:, da + lo:da + lo + GROUP]), vg_ref[:, lo:lo + GROUP])
            w = jnp.where(keep, ws_ref[g], 0.0).astype(BF)
            mixed = jnp.dot(w, vn.astype(BF), preferred_element_type=F32) + bst_ref[:, g:g + 1]
            o_ref[:, lo:lo + GROUP] = (u * mixed).astype(BF)

    return _call(
        body, name=name, grid=(t // GROUP,),
        in_specs=[pl.BlockSpec((GROUP, da2), lambda i: (i, 0)), pl.BlockSpec((1, da), lambda i: (0, 0)),
                  pl.BlockSpec((ng, GROUP, GROUP), lambda i: (0, 0, 0)), pl.BlockSpec((GROUP, ng), lambda i: (0, 0))],
        out_specs=[pl.BlockSpec((GROUP, da), lambda i: (i, 0))], out_shape=[jax.ShapeDtypeStruct((t, da), BF)],
        operands=(p, v_gain, w_s, b_st), sem=("parallel",), carries=carries)


def _sgu_bwd(name, p, v_gain, w_s, w_st, b_st, dout, carries=()):
    t, da2 = p.shape
    da = da2 // 2
    ng = da // GROUP

    def body(p_ref, vg_ref, ws_ref, wst_ref, bst_ref, do_ref, dp_ref, dvg_ref, dws_ref, dbst_ref):
        @pl.when(pl.program_id(0) == 0)
        def _():
            dvg_ref[...] = jnp.zeros_like(dvg_ref)
            dws_ref[...] = jnp.zeros_like(dws_ref)
            dbst_ref[...] = jnp.zeros_like(dbst_ref)

        keep = _causal(GROUP)
        keep_t = lax.broadcasted_iota(jnp.int32, (GROUP, GROUP), 0) <= lax.broadcasted_iota(jnp.int32, (GROUP, GROUP), 1)
        for g in range(ng):
            lo = g * GROUP
            u, du = _gelu_and_grad(p_ref[:, lo:lo + GROUP])
            v, dv_act = _gelu_and_grad(p_ref[:, da + lo:da + lo + GROUP])
            gain = vg_ref[:, lo:lo + GROUP]
            r = lax.rsqrt(jnp.mean(v * v, axis=-1, keepdims=True) + NORM_EPS)
            vh = v * r
            vnb = (vh * gain).astype(BF)
            w = jnp.where(keep, ws_ref[g], 0.0).astype(BF)
            wt = jnp.where(keep_t, wst_ref[g], 0.0).astype(BF)
            mixed = jnp.dot(w, vnb, preferred_element_type=F32) + bst_ref[:, g:g + 1]
            dout_g = do_ref[:, lo:lo + GROUP].astype(F32)
            dmixed = dout_g * u
            dmb = dmixed.astype(BF)
            dbst_ref[:, g:g + 1] += jnp.sum(dmixed, axis=1, keepdims=True)
            dws_ref[g] += jnp.where(keep, lax.dot_general(dmb, vnb, _NT, preferred_element_type=F32), 0.0)
            dvn = jnp.dot(wt, dmb, preferred_element_type=F32)
            dvg_ref[:, lo:lo + GROUP] += jnp.sum(dvn * vh, axis=0, keepdims=True)
            dvh = dvn * gain
            dv = r * dvh - vh * (jnp.sum(vh * dvh, axis=-1, keepdims=True) * (r / GROUP))
            dp_ref[:, lo:lo + GROUP] = (dout_g * mixed * du).astype(BF)
            dp_ref[:, da + lo:da + lo + GROUP] = (dv * dv_act).astype(BF)

    full = lambda *shape: pl.BlockSpec(shape, lambda i: (0,) * len(shape))
    outs = _call(
        body, name=name, grid=(t // GROUP,),
        in_specs=[pl.BlockSpec((GROUP, da2), lambda i: (i, 0)), full(1, da), full(ng, GROUP, GROUP), full(ng, GROUP, GROUP),
                  full(GROUP, ng), pl.BlockSpec((GROUP, da), lambda i: (i, 0))],
        out_specs=[pl.BlockSpec((GROUP, da2), lambda i: (i, 0)), full(1, da), full(ng, GROUP, GROUP), full(GROUP, ng)],
        out_shape=[jax.ShapeDtypeStruct((t, da2), BF), jax.ShapeDtypeStruct((1, da), F32),
                   jax.ShapeDtypeStruct((ng, GROUP, GROUP), F32), jax.ShapeDtypeStruct((GROUP, ng), F32)],
        operands=(p, v_gain, w_s, w_st, b_st, dout), sem=("arbitrary",), carries=carries)
    return outs[:4], outs[4:]


_CONV_ROWS = 256
_CONV_COLS = 128


def _shift_down(x, prev, k):
    top = pltpu.roll(jnp.concatenate([prev, x[:8]], axis=0), k, 0)[8:16]
    return jnp.concatenate([top, pltpu.roll(x, k, 0)[8:]], axis=0)


def _shift_up(x, nxt, k):
    n = x.shape[0]
    bottom = pltpu.roll(jnp.concatenate([x[n - 8:], nxt], axis=0), 16 - k, 0)[0:8]
    return jnp.concatenate([pltpu.roll(x, n - k, 0)[:n - 8], bottom], axis=0)


def _conv_taps(a_ref, half, r0, first):
    x = a_ref[half, pl.ds(r0, _rows(a_ref)), :]
    prev = a_ref[half, pl.ds(jnp.maximum(r0 - 8, 0), 8), :]
    prev = jnp.where(first, 0.0, prev)
    return x, _shift_down(x, prev, 1), _shift_down(x, prev, 2)


def _rows(a_ref):
    return min(_CONV_ROWS, a_ref.shape[1])


def _conv_fwd(name, a3, cw, cb, carries=()):
    _, t, f = a3.shape
    tc, rows = _CONV_COLS, min(_CONV_ROWS, t)

    def body(a_ref, cw_ref, cb_ref, y_ref):
        def step(r, carry):
            r0 = pl.multiple_of(r * rows, rows)
            c = []
            for half in range(2):
                x, x1, x2 = _conv_taps(a_ref, half, r0, r == 0)
                w = cw_ref[half]
                c.append(cb_ref[half] + (w[0:1] * x2 + w[1:2] * x1 + w[2:3] * x))
            y_ref[pl.ds(r0, rows), :] = (c[0] * jax.nn.sigmoid(c[0]) * c[1]).astype(BF)
            return carry

        lax.fori_loop(0, t // rows, step, 0)

    col = lambda *lead: pl.BlockSpec((*lead, tc), lambda j: (0,) * len(lead) + (j,))
    y, *dsts = _call(
        body, name=name, grid=(f // tc,), in_specs=[col(2, t), col(2, CONV_TAPS), col(2, 1)],
        out_specs=[col(t)], out_shape=[jax.ShapeDtypeStruct((t, f), BF)], operands=(a3, cw, cb),
        sem=("parallel",), vmem=40 * _MIB, carries=carries)
    return y, dsts


def _conv_bwd(name, a3, cw, cb, dy, carries=()):
    _, t, f = a3.shape
    tc, rows = _CONV_COLS, min(_CONV_ROWS, t)
    n_steps = t // rows

    def body(a_ref, cw_ref, cb_ref, dy_ref, da_ref, dcw_ref, dcb_ref):
        dcw_ref[...] = jnp.zeros_like(dcw_ref)
        dcb_ref[...] = jnp.zeros_like(dcb_ref)

        def step(s, nxt):
            r = n_steps - 1 - s
            r0 = pl.multiple_of(r * rows, rows)
            taps, c = [], []
            for half in range(2):
                x, x1, x2 = _conv_taps(a_ref, half, r0, r == 0)
                w = cw_ref[half]
                taps.append((x2, x1, x))
                c.append(cb_ref[half] + (w[0:1] * x2 + w[1:2] * x1 + w[2:3] * x))
            gate, val = c
            sg = jax.nn.sigmoid(gate)
            dyv = dy_ref[pl.ds(r0, rows), :].astype(F32)
            dcs = (dyv * val * (sg * (1.0 + gate * (1.0 - sg))), dyv * (gate * sg))
            new_nxt = []
            for half in range(2):
                dc, w = dcs[half], cw_ref[half]
                dcb_ref[half] += jnp.sum(dc, axis=0, keepdims=True)
                for tap in range(CONV_TAPS):
                    dcw_ref[half, tap:tap + 1, :] += jnp.sum(dc * taps[half][tap], axis=0, keepdims=True)
                da = w[2:3] * dc + w[1:2] * _shift_up(dc, nxt[half], 1) + w[0:1] * _shift_up(dc, nxt[half], 2)
                da_ref[half, pl.ds(r0, rows), :] = da.astype(BF)
                new_nxt.append(dc[:8])
            return tuple(new_nxt)

        zeros = jnp.zeros((8, tc), F32)
        lax.fori_loop(0, n_steps, step, (zeros, zeros))

    col = lambda *lead: pl.BlockSpec((*lead, tc), lambda j: (0,) * len(lead) + (j,))
    outs = _call(
        body, name=name, grid=(f // tc,), in_specs=[col(2, t), col(2, CONV_TAPS), col(2, 1), col(t)],
        out_specs=[col(2, t), col(2, CONV_TAPS), col(2, 1)],
        out_shape=[jax.ShapeDtypeStruct((2, t, f), BF), jax.ShapeDtypeStruct((2, CONV_TAPS, f), F32),
                   jax.ShapeDtypeStruct((2, 1, f), F32)],
        operands=(a3, cw, cb, dy), sem=("parallel",), vmem=40 * _MIB, carries=carries)
    return outs[:3], outs[3:]


_ATT_BLOCK = 256


def _split_dot(x, tri):
    hi = x.astype(BF)
    lo = (x - hi.astype(F32)).astype(BF)
    return jnp.dot(hi, tri, preferred_element_type=F32) + jnp.dot(lo, tri, preferred_element_type=F32)


def _logits(qb, kb, diagonal):
    z = lax.dot_general(qb, kb, _NT, preferred_element_type=F32) * (1.0 / math.sqrt(GROUP))
    lb = jnp.minimum(z, 0.0) - jnp.log(1.0 + jnp.exp(-jnp.abs(z)))
    if not diagonal:
        return lb, lb - z, None
    n = z.shape[0]
    mask = lax.broadcasted_iota(jnp.int32, (n, n), 1) < lax.broadcasted_iota(jnp.int32, (n, n), 0)
    return lb, jnp.where(mask, lb - z, 0.0), mask


def _attn_fwd(name, q, k, v, carries=()):
    t, hd = q.shape
    blk = min(_ATT_BLOCK, t)

    def body(q_ref, k_ref, v_ref, o_ref, rest_ref, first_ref):
        h, i = pl.program_id(0), pl.program_id(1)
        qb = q_ref[...]
        ri = lax.broadcasted_iota(jnp.int32, (blk, blk), 0)
        ci = lax.broadcasted_iota(jnp.int32, (blk, blk), 1)
        tri = (ri >= ci).astype(BF)

        def tile(j, right, acc, diagonal):
            k0 = pl.multiple_of(j * blk, blk)
            lb, l1m, mask = _logits(qb, k_ref[pl.ds(k0, blk), :], diagonal)
            incl = _split_dot(l1m, tri)
            a = jnp.exp(lb + (incl - l1m + right))
            if diagonal:
                a = jnp.where(mask, a, 0.0)
            acc = acc + jnp.dot(a.astype(BF), v_ref[pl.ds(k0, blk), :], preferred_element_type=F32)
            return right + incl[:, 0:1], acc

        right, acc = tile(i, jnp.zeros((blk, 1), F32), jnp.zeros((blk, GROUP), F32), True)

        def more(carry):
            j, right, _ = carry
            return jnp.logical_and(j >= 0, jnp.max(right) > EXP_FLOOR)

        def step(carry):
            j, right, acc = carry
            right, acc = tile(j, right, acc, False)
            return j - 1, right, acc

        j, right, acc = lax.while_loop(more, step, (i - 1, right, acc))
        o_ref[...] = acc.astype(BF)
        rest_ref[...] = jnp.broadcast_to(right, (blk, GROUP))
        first_ref[h, i] = (j + 1).astype(F32)

    qspec = pl.BlockSpec((blk, GROUP), lambda h, i: (i, h))
    kvspec = pl.BlockSpec((t, GROUP), lambda h, i: (0, h))
    outs = _call(
        body, name=name, grid=(hd // GROUP, t // blk), in_specs=[qspec, kvspec, kvspec],
        out_specs=[qspec, qspec, pl.BlockSpec(memory_space=pltpu.SMEM)],
        out_shape=[jax.ShapeDtypeStruct((t, hd), BF), jax.ShapeDtypeStruct((t, hd), F32),
                   jax.ShapeDtypeStruct((hd // GROUP, t // blk), F32)],
        operands=(q, k, v), sem=("arbitrary", "arbitrary"), vmem=32 * _MIB, carries=carries, middle_at=0.75)
    return outs[:3], outs[3:]


def _attn_bwd(name, q, k, v, rest, first, do, carries=()):
    t, hd = q.shape
    blk = min(_ATT_BLOCK, t)
    nq = t // blk
    scale = 1.0 / math.sqrt(GROUP)

    def body(first_ref, q_ref, k_ref, v_ref, rest_ref, do_ref, dq_ref, dk_ref, dv_ref, dk_acc, dv_acc):
        h, i = pl.program_id(0), pl.program_id(1)

        @pl.when(i == 0)
        def _():
            dk_acc[...] = jnp.zeros_like(dk_acc)
            dv_acc[...] = jnp.zeros_like(dv_acc)

        qb, dob = q_ref[...], do_ref[...]
        total = rest_ref[:, 0:1]
        ri = lax.broadcasted_iota(jnp.int32, (blk, blk), 0)
        ci = lax.broadcasted_iota(jnp.int32, (blk, blk), 1)
        tri = (ri <= ci).astype(BF)

        def tile(j, left, gleft, dq, diagonal):
            k0 = pl.multiple_of(j * blk, blk)
            kb, vb = k_ref[pl.ds(k0, blk), :], v_ref[pl.ds(k0, blk), :]
            lb, l1m, mask = _logits(qb, kb, diagonal)
            pre = _split_dot(l1m, tri)
            a = jnp.exp(lb + (total - left - pre))
            if diagonal:
                a = jnp.where(mask, a, 0.0)
            g = a * lax.dot_general(dob, vb, _NT, preferred_element_type=F32)
            gpre = _split_dot(g, tri)
            beta = jnp.exp(lb)
            dz = (g * (1.0 - beta) - (gleft + gpre - g) * beta) * scale
            if diagonal:
                dz = jnp.where(mask, dz, 0.0)
            dzb, ab = dz.astype(BF), a.astype(BF)
            dk_acc[pl.ds(k0, blk), :] += lax.dot_general(dzb, qb, _TN, preferred_element_type=F32)
            dv_acc[pl.ds(k0, blk), :] += lax.dot_general(ab, dob, _TN, preferred_element_type=F32)
            dq = dq + jnp.dot(dzb, kb, preferred_element_type=F32)
            return left + pre[:, blk - 1:blk], gleft + gpre[:, blk - 1:blk], dq

        zero = jnp.zeros((blk, 1), F32)
        first_block = jnp.clip(first_ref[h, i].astype(jnp.int32), 0, i)
        carry = lax.fori_loop(first_block, i, lambda j, c: tile(j, *c, False), (zero, zero, jnp.zeros((blk, GROUP), F32)))
        _, _, dq = tile(i, *carry, True)
        dq_ref[...] = dq.astype(BF)

        @pl.when(i == nq - 1)
        def _():
            dk_ref[...] = dk_acc[...].astype(BF)
            dv_ref[...] = dv_acc[...].astype(BF)

    qspec = pl.BlockSpec((blk, GROUP), lambda h, i: (i, h))
    kvspec = pl.BlockSpec((t, GROUP), lambda h, i: (0, h))
    outs = _call(
        body, name=name, grid=(hd // GROUP, nq),
        in_specs=[pl.BlockSpec(memory_space=pltpu.SMEM), qspec, kvspec, kvspec, qspec, qspec],
        out_specs=[qspec, kvspec, kvspec], out_shape=[jax.ShapeDtypeStruct((t, hd), BF)] * 3,
        operands=(first, q, k, v, rest, do), scratch=[pltpu.VMEM((t, GROUP), F32), pltpu.VMEM((t, GROUP), F32)],
        sem=("arbitrary", "arbitrary"), vmem=40 * _MIB, carries=carries)
    return outs[:3], outs[3:]


def _adamw_math(w, g, m, v):
    m = ADAM_B1 * m + (1.0 - ADAM_B1) * g
    v = ADAM_B2 * v + (1.0 - ADAM_B2) * (g * g)
    m_hat = m / (1.0 - ADAM_B1 ** ADAM_STEP)
    v_hat = v / (1.0 - ADAM_B2 ** ADAM_STEP)
    return -ADAM_LR * (m_hat / (jnp.sqrt(v_hat) + ADAM_EPS) + ADAM_WD * w), m, v


def _sum_adamw(name, parts, w, m, v):
    layers, r, c = w.shape
    budget = 512 * 1024 // layers
    tr = r if r * c <= budget else _tile_rows(r, max(8, (budget // c) // 8 * 8))
    n = r // tr

    def body(*refs):
        p_refs, (w_ref, m_ref, v_ref, g_ref, d_ref, nm_ref, nv_ref) = refs[:layers], refs[layers:]
        for layer, p_ref in enumerate(p_refs):
            @pl.when(pl.program_id(0) == layer)
            def _(p_ref=p_ref):
                g = p_ref[0].astype(F32)
                for dev in range(1, N_DEV):
                    g = g + p_ref[dev].astype(F32)
                g_ref[...] = g
                d_ref[...], nm_ref[...], nv_ref[...] = _adamw_math(w_ref[...], g, m_ref[...], v_ref[...])

    def part_spec(layer):
        return pl.BlockSpec((N_DEV, tr, c), lambda l, i: (0, jnp.where(l < layer, 0, jnp.where(l == layer, i, n - 1)), 0))

    row = pl.BlockSpec((None, tr, c), lambda l, i: (l, i, 0))
    return pl.pallas_call(
        body, name=name, grid=(layers, n), in_specs=[part_spec(layer) for layer in range(layers)] + [row] * 3,
        out_specs=[row] * 4, out_shape=[jax.ShapeDtypeStruct((layers, r, c), F32)] * 4,
        compiler_params=_params(("arbitrary", "arbitrary"), 40 * _MIB),
    )(*parts, w, m, v)


def _tile_rows(r, pref):
    t = min(r, pref)
    while r % t or t % 8:
        t -= 1
    return t


def _exchange(name, carries):
    return _call(lambda: None, name=name, grid=(1,), in_specs=[], out_specs=[], out_shape=[], operands=(), carries=carries)


def _all_reduce_small(name, groups):
    c = groups[0][0].shape[1]
    parts, offsets, starts, r = [], [], [], 0
    for group in groups:
        starts.append(r)
        for p in group:
            parts.append(p)
            offsets.append(r)
            r += p.shape[0]
        r = -(-r // 8) * 8
    n = len(parts)

    def body(*refs):
        out_ref, slots, send_sems, recv_sems = refs[n:]
        x, y, c_, me = _place()
        slots[me] = jnp.zeros((r, c), F32)
        for p_ref, off in zip(refs[:n], offsets):
            slots[me, off:off + p_ref.shape[0], :] = p_ref[...]

        def copy(k):
            return pltpu.make_async_remote_copy(
                src_ref=slots.at[me], dst_ref=slots.at[me], send_sem=send_sems.at[k - 1], recv_sem=recv_sems.at[k - 1],
                device_id=_flip(x, y, c_, k), device_id_type=MESH)

        for k in range(1, N_DEV):
            copy(k).start()
        for k in range(1, N_DEV):
            copy(k).wait()
        total = slots[0]
        for dev in range(1, N_DEV):
            total = total + slots[dev]
        out_ref[...] = total

    vm = pl.BlockSpec(memory_space=pltpu.VMEM)
    summed = pl.pallas_call(
        body, name=name, in_specs=[vm] * n, out_specs=vm, out_shape=jax.ShapeDtypeStruct((r, c), F32),
        scratch_shapes=[pltpu.VMEM((N_DEV, r, c), F32), pltpu.SemaphoreType.DMA((7,)), pltpu.SemaphoreType.DMA((7,))],
        compiler_params=_params(None, (N_DEV + 6) * r * c * 4 + 8 * _MIB),
    )(*parts)
    return summed, starts


def _adamw(name, g, w, m, v):
    cols = w.shape[-1]
    flat = lambda a: a.reshape(-1, cols)

    def body(g_ref, w_ref, m_ref, v_ref, d_ref, nm_ref, nv_ref):
        d_ref[...], nm_ref[...], nv_ref[...] = _adamw_math(w_ref[...], g_ref[...], m_ref[...], v_ref[...])

    outs = pl.pallas_call(body, name=name, out_shape=[jax.ShapeDtypeStruct(flat(w).shape, F32)] * 3)(
        flat(g), flat(w), flat(m), flat(v))
    return [o.reshape(w.shape) for o in outs]


def kernel(x, pre_mix_g, post_mix_g, pre_ffn_g, post_ffn_g, a_w_in, a_v_norm_g, a_w_spatial, a_b_spatial, a_w_out, kv_norm_g, w_k, w_v, b_w_q, b_w_o, ffn_w_up, ffn_conv_w, ffn_conv_b, ffn_w_down, loss_target, m_pre_mix_g, m_post_mix_g, m_pre_ffn_g, m_post_ffn_g, m_a_w_in, m_a_v_norm_g, m_a_w_spatial, m_a_b_spatial, m_a_w_out, m_kv_norm_g, m_w_k, m_w_v, m_b_w_q, m_b_w_o, m_ffn_w_up, m_ffn_conv_w, m_ffn_conv_b, m_ffn_w_down, v_pre_mix_g, v_post_mix_g, v_pre_ffn_g, v_post_ffn_g, v_a_w_in, v_a_v_norm_g, v_a_w_spatial, v_a_b_spatial, v_a_w_out, v_kv_norm_g, v_w_k, v_w_v, v_b_w_q, v_b_w_o, v_ffn_w_up, v_ffn_conv_w, v_ffn_conv_b, v_ffn_w_down):
    t, d = x.shape[1], x.shape[2]
    f = ffn_w_down.shape[1] * N_DEV
    ng = d // GROUP
    me = 4 * lax.axis_index("x") + 2 * lax.axis_index("y") + lax.axis_index("c")
    x2, target = x.reshape(t, d), loss_target.reshape(t, d)

    g_in, g_cw, g_vg = _exchange("gather_first", [
        _gather(a_w_in[0].astype(BF)), _gather(ffn_conv_w.reshape(2 * CONV_TAPS, -1)), _gather(a_v_norm_g)])
    up0 = ffn_w_up[0].astype(BF)
    cw_full = jnp.transpose(g_cw.reshape(N_DEV, 2, CONV_TAPS, -1), (1, 2, 0, 3)).reshape(2, CONV_TAPS, 2, f)
    cw_l = [jnp.transpose(cw_full[l], (1, 0, 2)) for l in range(2)]
    cb_l = [ffn_conv_b[l].reshape(2, 1, f) for l in range(2)]
    vg_full = g_vg.reshape(1, d)
    w_s = a_w_spatial[0]
    w_st = jnp.swapaxes(w_s, 1, 2)
    b_st = a_b_spatial[0].T

    (hn0,) = _norms_fwd("pre_mix0", x2, [pre_mix_g[0]])
    p0, (g_out, g_up0) = _mm_nn_blk("sgu_in", hn0, g_in, F32, carries=[
        _gather(a_w_out[0].astype(BF)), _gather(up0, 0, d // 4)])
    w_out_f = g_out.reshape(d, d)
    sg, g_up0 = _sgu_fwd("sgu", p0, vg_full, w_s, b_st, carries=[_gather(up0, d // 4, d // 2, dst=g_up0)])
    mix0, (g_up0,) = _mm_nn("sgu_out", sg, w_out_f, F32, carries=[_gather(up0, d // 2, 3 * d // 4, dst=g_up0)])
    h1, fn0, g_up0 = _resid_norms("post_mix0", x2, mix0, post_mix_g[0], [pre_ffn_g[0]],
                                  carries=[_gather(up0, 3 * d // 4, d, dst=g_up0)])
    a3_0, (g_dn0,) = _mm_nn_blk("ffn0_up", fn0, g_up0, F32, halves=True, carries=[_gather(ffn_w_down[0].astype(BF))])
    y0, (g_q,) = _conv_fwd("ffn0_conv", a3_0, cw_l[0], cb_l[0], carries=[_gather(b_w_q[0].astype(BF))])
    f0, (g_k, g_v) = _mm_nn("ffn0_down", y0, g_dn0.reshape(f, d), F32,
                            carries=[_gather(w_k.astype(BF)), _gather(w_v.astype(BF))])
    w_q_f, w_k_f, w_v_f = g_q.reshape(d, d), g_k.reshape(d, d), g_v.reshape(d, d)
    h2, hn1, kvn = _resid_norms("post_ffn0", h1, f0, post_ffn_g[0], [pre_mix_g[1], kv_norm_g])
    q = _mm_nn("attn_q", hn1, w_q_f, BF)
    kk = _mm_nn("attn_k", kvn, w_k_f, BF)
    vv = _mm_nn("attn_v", kvn, w_v_f, BF)
    (att, rest, first), (g_o, g_up1) = _attn_fwd(
        "attn", q, kk, vv, carries=[_gather(b_w_o[0].astype(BF)), _gather(ffn_w_up[1].astype(BF))])
    w_o_f = g_o.reshape(d, d)
    mix1 = _mm_nn("attn_o", att, w_o_f, F32)
    h3, fn1 = _resid_norms("post_mix1", h2, mix1, post_mix_g[1], [pre_ffn_g[1]])
    a3_1, (g_dn1,) = _mm_nn_blk("ffn1_up", fn1, g_up1, F32, halves=True, carries=[_gather(ffn_w_down[1].astype(BF))])
    y1, _ = _conv_fwd("ffn1_conv", a3_1, cw_l[1], cb_l[1])
    f1 = _mm_nn("ffn1_down", y1, g_dn1.reshape(f, d), F32)
    g_up = (g_up0, g_up1)
    w_dn_f = (g_dn0.reshape(f, d), g_dn1.reshape(f, d))
    dh, loss_part = _resid_loss("loss", h3, f1, post_ffn_g[1], target)
    loss = lax.psum(loss_part[0, 0], ("x", "y", "c"))

    def blocks(dw):
        return dw.reshape(N_DEV, -1, d)

    def split(result, carries):
        return result if carries else (result, [])

    def ffn_bwd(l, dh_out, h_in, fn, a3, yv, fo, with_dw=(), with_dx=(), with_up=()):
        dfo, d_post = _norm_bwd(f"post_ffn{l}_bwd", fo, post_ffn_g[l], dh_out, out_dtype=BF)
        dw_dn, sent_dw = split(_mm_tn(f"ffn{l}_down_dw", yv, dfo, carries=with_dw), with_dw)
        dy, sent_dx = split(_mm_nt(f"ffn{l}_down_dx", dfo, w_dn_f[l], BF, carries=with_dx), with_dx)
        (da3, dcw, dcb), (p_dn,) = _conv_bwd(f"ffn{l}_conv_bwd", a3, cw_l[l], cb_l[l], dy, carries=[_scatter(blocks(dw_dn))])
        dw_up, sent_up = split(_mm_tn_blk(f"ffn{l}_up_dw", fn, da3, halves=True, carries=with_up), with_up)
        dfn, (p_up,) = _mm_nt_blk(f"ffn{l}_up_dx", da3, g_up[l], F32, halves=True, carries=[_scatter(dw_up, 0, d // 2)])
        dh_in, d_pre, p_up = _norm_bwd(f"pre_ffn{l}_bwd", h_in, pre_ffn_g[l], dfn, res=dh_out,
                                       carries=[_scatter(dw_up, d // 2, 9 * d // 16, dst=p_up)])
        return dh_in, d_post, d_pre, dcw, dcb, p_dn, dw_up, p_up, list(sent_dw) + list(sent_dx) + list(sent_up)

    dh3, d_post_ffn1, d_pre_ffn1, dcw1, dcb1, p_dn1, dw_up1, p_up1, _ = ffn_bwd(1, dh, h3, fn1, a3_1, y1, f1)
    dmix1, d_post_mix1 = _norm_bwd("post_mix1_bwd", mix1, post_mix_g[1], dh3, out_dtype=BF)
    dw_o = _mm_tn("attn_o_dw", att, dmix1)
    datt = _mm_nt("attn_o_dx", dmix1, w_o_f, BF)
    (dq, dk, dv), (p_up1, p_o) = _attn_bwd(
        "attn_bwd", q, kk, vv, rest, first, datt, carries=[_scatter(dw_up1, 9 * d // 16, d, dst=p_up1), _scatter(blocks(dw_o))])
    dw_q = _mm_tn("attn_q_dw", hn1, dq)
    dw_k = _mm_tn("attn_k_dw", kvn, dk)
    dw_v = _mm_tn("attn_v_dw", kvn, dv)
    dhn1 = _mm_nt("attn_q_dx", dq, w_q_f, F32)
    dkvn = _mm_nt("attn_v_dx", dv, w_v_f, F32, add=_mm_nt("attn_k_dx", dk, w_k_f, F32))
    dh2a, d_pre_mix1 = _norm_bwd("pre_mix1_bwd", h2, pre_mix_g[1], dhn1, res=dh3)
    dh2, d_kv = _norm_bwd("kv_norm_bwd", h2, kv_norm_g, dkvn, res=dh2a)
    dh1, d_post_ffn0, d_pre_ffn0, dcw0, dcb0, p_dn0, dw_up0, p_up0, (p_q, p_k, p_v) = ffn_bwd(
        0, dh2, h1, fn0, a3_0, y0, f0, with_dw=[_scatter(blocks(dw_q))], with_dx=[_scatter(blocks(dw_k))],
        with_up=[_scatter(blocks(dw_v))])
    dmix0, d_post_mix0 = _norm_bwd("post_mix0_bwd", mix0, post_mix_g[0], dh1, out_dtype=BF)
    dw_out, (p_up0,) = _mm_tn("sgu_out_dw", sg, dmix0, carries=[_scatter(dw_up0, 9 * d // 16, 11 * d // 16, dst=p_up0)])
    dsg = _mm_nt("sgu_out_dx", dmix0, w_out_f, BF)
    (dp0, d_vg, d_ws, d_bst), (p_up0,) = _sgu_bwd(
        "sgu_bwd", p0, vg_full, w_s, w_st, b_st, dsg, carries=[_scatter(dw_up0, 11 * d // 16, 15 * d // 16, dst=p_up0)])
    dw_in, (p_out,) = _mm_tn_blk("sgu_in_dw", hn0, dp0, carries=[_scatter(blocks(dw_out))])
    dhn0, (p_in,) = _mm_nt_blk("sgu_in_dx", dp0, g_in, F32, carries=[_scatter(dw_in, 0, 5 * d // 8)])
    grad_x, d_pre_mix0, p_in = _norm_bwd("pre_mix0_bwd", x2, pre_mix_g[0], dhn0, res=dh1,
                                         carries=[_scatter(dw_in, 5 * d // 8, 7 * d // 8, dst=p_in)])
    p_up0, p_in = _exchange("scatter_last", [_scatter(dw_up0, 15 * d // 16, d, dst=p_up0),
                                             _scatter(dw_in, 7 * d // 8, d, dst=p_in)])

    def conv_w_grad(dcw):
        return jnp.transpose(dcw, (1, 0, 2)).reshape(CONV_TAPS, 2 * f)

    small = [
        ([d_pre_mix0, d_pre_mix1], (2, d)), ([d_post_mix0, d_post_mix1], (2, d)),
        ([d_pre_ffn0, d_pre_ffn1], (2, d)), ([d_post_ffn0, d_post_ffn1], (2, d)),
        ([d_kv], (d,)), ([d_vg], (1, d)), ([d_bst.T], (1, ng, GROUP)), ([d_ws], (1, ng, GROUP, GROUP)),
        ([dcb0, dcb1], (2, 2 * f)), ([conv_w_grad(dcw0), conv_w_grad(dcw1)], (2, CONV_TAPS, 2 * f)),
    ]
    width = V7X_LANES * math.gcd(d // V7X_LANES, 2 * f // V7X_LANES)
    summed, offsets = _all_reduce_small("reduce_small", [[a.reshape(-1, width) for a in group] for group, _ in small])
    full = [summed[off:off + math.prod(shape) // width].reshape(shape) for off, (_, shape) in zip(offsets, small)]
    g_pre_mix, g_post_mix, g_pre_ffn, g_post_ffn, g_kv, g_vgain, g_bs, g_ws, g_cb, g_cwf = full
    cw_w = 2 * f // N_DEV
    g_vgain = lax.dynamic_slice_in_dim(g_vgain, me * (d // N_DEV), d // N_DEV, axis=1)
    g_cwf = lax.dynamic_slice_in_dim(g_cwf, me * cw_w, cw_w, axis=2)

    parts = [p_in, p_out, p_k, p_v, p_q, p_o, p_up0, p_up1, p_dn0, p_dn1]

    def small_update(name, g, w, m, v):
        return [g] + _adamw(name, g, w, m, v)

    def stacked(name, part0, part1, w, m, v):
        shape = (2, *part0.shape[1:])
        return [o.reshape(w.shape) for o in _sum_adamw(name, [part0, part1], w.reshape(shape), m.reshape(shape), v.reshape(shape))]

    def single(name, part, w, m, v):
        shape = (1, *part.shape[1:])
        return [o.reshape(w.shape) for o in _sum_adamw(name, [part], w.reshape(shape), m.reshape(shape), v.reshape(shape))]

    results = {
        "pre_mix_g": small_update("adam_pre_mix", g_pre_mix, pre_mix_g, m_pre_mix_g, v_pre_mix_g),
        "post_mix_g": small_update("adam_post_mix", g_post_mix, post_mix_g, m_post_mix_g, v_post_mix_g),
        "pre_ffn_g": small_update("adam_pre_ffn", g_pre_ffn, pre_ffn_g, m_pre_ffn_g, v_pre_ffn_g),
        "post_ffn_g": small_update("adam_post_ffn", g_post_ffn, post_ffn_g, m_post_ffn_g, v_post_ffn_g),
        "a_w_in": single("adam_a_w_in", parts[0], a_w_in, m_a_w_in, v_a_w_in),
        "a_v_norm_g": small_update("adam_a_v_norm", g_vgain, a_v_norm_g, m_a_v_norm_g, v_a_v_norm_g),
        "a_w_spatial": small_update("adam_a_w_spatial", g_ws, a_w_spatial, m_a_w_spatial, v_a_w_spatial),
        "a_b_spatial": small_update("adam_a_b_spatial", g_bs, a_b_spatial, m_a_b_spatial, v_a_b_spatial),
        "a_w_out": single("adam_a_w_out", parts[1], a_w_out, m_a_w_out, v_a_w_out),
        "kv_norm_g": small_update("adam_kv_norm", g_kv, kv_norm_g, m_kv_norm_g, v_kv_norm_g),
        "w_k": single("adam_w_k", parts[2], w_k, m_w_k, v_w_k),
        "w_v": single("adam_w_v", parts[3], w_v, m_w_v, v_w_v),
        "b_w_q": single("adam_b_w_q", parts[4], b_w_q, m_b_w_q, v_b_w_q),
        "b_w_o": single("adam_b_w_o", parts[5], b_w_o, m_b_w_o, v_b_w_o),
        "ffn_w_up": stacked("adam_ffn_w_up", parts[6], parts[7], ffn_w_up, m_ffn_w_up, v_ffn_w_up),
        "ffn_conv_w": small_update("adam_ffn_conv_w", g_cwf, ffn_conv_w, m_ffn_conv_w, v_ffn_conv_w),
        "ffn_conv_b": small_update("adam_ffn_conv_b", g_cb, ffn_conv_b, m_ffn_conv_b, v_ffn_conv_b),
        "ffn_w_down": stacked("adam_ffn_w_down", parts[8], parts[9], ffn_w_down, m_ffn_w_down, v_ffn_w_down),
    }
    order = ["pre_mix_g", "post_mix_g", "pre_ffn_g", "post_ffn_g", "a_w_in", "a_v_norm_g", "a_w_spatial", "a_b_spatial",
             "a_w_out", "kv_norm_g", "w_k", "w_v", "b_w_q", "b_w_o", "ffn_w_up", "ffn_conv_w", "ffn_conv_b", "ffn_w_down"]
    outs = [loss, grad_x.reshape(x.shape)]
    for idx in range(4):
        outs += [results[n][idx] for n in order]
    return tuple(outs)
```

```python
import functools
import math
from typing import NamedTuple, Optional

import jax
import jax.numpy as jnp
from jax import lax
from jax.experimental import pallas as pl
from jax.experimental.pallas import tpu as pltpu

F32 = jnp.float32
BF = jnp.bfloat16
MESH = pl.DeviceIdType.MESH

N_DEV = 8
NORM_EPS = 1e-6
GROUP = 128
CONV_TAPS = 3
ADAM_LR, ADAM_B1, ADAM_B2, ADAM_EPS, ADAM_WD, ADAM_STEP = 0.001, 0.9, 0.999, 1e-08, 0.01, 10
EXP_FLOOR = -104.0

V7X_LANES = 128
V7X_VMEM_BYTES = 64 * 1024 * 1024
_MIB = 1024 * 1024

_NN = (((1,), (0,)), ((), ()))
_NT = (((1,), (1,)), ((), ()))
_TN = (((0,), (0,)), ((), ()))


def _tile(n, pref):
    if n <= pref:
        return n
    t = (pref // V7X_LANES) * V7X_LANES
    while t > V7X_LANES and n % t:
        t -= V7X_LANES
    assert n % t == 0, (n, pref)
    return t


def _nbytes(shape, dtype):
    return math.prod(shape) * jnp.dtype(dtype).itemsize


def _params(sem=None, vmem=None):
    kw = {}
    if sem is not None:
        kw["dimension_semantics"] = sem
    if vmem is not None:
        kw["vmem_limit_bytes"] = int(min(max(vmem, 16 * _MIB), V7X_VMEM_BYTES - 8 * _MIB))
    return pltpu.CompilerParams(**kw)


def _place():
    x, y, c = lax.axis_index("x"), lax.axis_index("y"), lax.axis_index("c")
    return x, y, c, 4 * x + 2 * y + c


def _flip(x, y, c, k):
    return (1 - x if k & 4 else x, 1 - y if k & 2 else y, 1 - c if k & 1 else c)


class _Carry(NamedTuple):
    gather: bool
    src: jax.Array
    dst: Optional[jax.Array]
    lo: int
    hi: int


def _gather(src, lo=0, hi=None, dst=None):
    return _Carry(True, src, dst, lo, src.shape[0] if hi is None else hi)


def _scatter(src, lo=0, hi=None, dst=None):
    return _Carry(False, src, dst, lo, src.shape[1] if hi is None else hi)


def _carry_phases(carries, srcs, dsts, send_sems, recv_sems, local_sems):
    x, y, c, me = _place()
    here, sibling = (x, y, c), (x, y, 1 - c)
    chips = [(1 - x, y), (x, 1 - y), (1 - x, 1 - y)]

    def rows(u):
        return pl.ds(carries[u].lo, carries[u].hi - carries[u].lo)

    def block_copy(u, sem, block, to, from_src=False):
        slot = dsts[u].at[4 * block[0] + 2 * block[1] + block[2], rows(u)]
        return pltpu.make_async_remote_copy(
            src_ref=srcs[u].at[rows(u)] if from_src else slot, dst_ref=slot, send_sem=send_sems.at[u, sem],
            recv_sem=recv_sems.at[u, sem], device_id=to, device_id_type=MESH)

    def partial_copy(u, k):
        peer = _flip(x, y, c, k)
        return pltpu.make_async_remote_copy(
            src_ref=srcs[u].at[4 * peer[0] + 2 * peer[1] + peer[2], rows(u)], dst_ref=dsts[u].at[me, rows(u)],
            send_sem=send_sems.at[u, k - 1], recv_sem=recv_sems.at[u, k - 1], device_id=peer, device_id_type=MESH)

    def local_copy(u):
        src = srcs[u].at[rows(u)] if carries[u].gather else srcs[u].at[me, rows(u)]
        return pltpu.make_async_copy(src, dsts[u].at[me, rows(u)], local_sems.at[u])

    def first():
        for u, cr in enumerate(carries):
            local_copy(u).start()
            if cr.gather:
                block_copy(u, 0, here, sibling, from_src=True).start()
                for j, chip in enumerate(chips):
                    block_copy(u, 1 + j, here, (*chip, c), from_src=True).start()
            else:
                for k in range(1, N_DEV):
                    partial_copy(u, k).start()

    def middle():
        for u, cr in enumerate(carries):
            if cr.gather:
                for j, chip in enumerate(chips):
                    block_copy(u, 1 + j, (*chip, c), here).wait_recv()
                    block_copy(u, 4 + j, (*chip, c), sibling).start()

    def last():
        for u, cr in enumerate(carries):
            if cr.gather:
                block_copy(u, 0, sibling, here).wait_recv()
                for j, chip in enumerate(chips):
                    block_copy(u, 4 + j, (*chip, 1 - c), here).wait_recv()
                block_copy(u, 0, here, sibling, from_src=True).wait_send()
                for j, chip in enumerate(chips):
                    block_copy(u, 1 + j, here, (*chip, c), from_src=True).wait_send()
                    block_copy(u, 4 + j, (*chip, c), sibling).wait_send()
            else:
                for k in range(1, N_DEV):
                    partial_copy(u, k).wait()
            local_copy(u).wait()

    return first, middle, last


def _call(body, *, name, grid, in_specs, out_specs, out_shape, operands, scratch=(), sem=None, vmem=None,
          carries=(), middle_at=0.6):
    if not carries:
        return pl.pallas_call(
            body, name=name, grid=grid, in_specs=in_specs, out_specs=out_specs, out_shape=out_shape,
            scratch_shapes=list(scratch), compiler_params=_params(sem, vmem))(*operands)
    n_in, n_out, n_scr, nc = len(in_specs), len(out_specs), len(scratch), len(carries)
    given = [u for u, cr in enumerate(carries) if cr.dst is not None]
    steps = math.prod(grid)
    middle_step = min(steps - 1, int(steps * middle_at))

    def wrapped(*refs):
        ins, srcs = refs[:n_in], refs[n_in:n_in + nc]
        at = n_in + nc + len(given)
        outs, dsts = refs[at:at + n_out], refs[at + n_out:at + n_out + nc]
        at += n_out + nc
        scr, (send_sems, recv_sems, local_sems) = refs[at:at + n_scr], refs[at + n_scr:]
        first, middle, last = _carry_phases(carries, srcs, dsts, send_sems, recv_sems, local_sems)
        step = 0
        for axis, size in enumerate(grid):
            step = step * size + pl.program_id(axis)
        pl.when(step == 0)(first)
        body(*ins, *outs, *scr)
        pl.when(step == middle_step)(middle)
        pl.when(step == steps - 1)(last)

    any_spec = pl.BlockSpec(memory_space=pl.ANY)
    dst_shapes = [jax.ShapeDtypeStruct((N_DEV, *cr.src.shape) if cr.gather else cr.src.shape, cr.src.dtype) for cr in carries]
    return pl.pallas_call(
        wrapped, name=name, grid=grid, in_specs=list(in_specs) + [any_spec] * (nc + len(given)),
        out_specs=list(out_specs) + [any_spec] * nc, out_shape=list(out_shape) + dst_shapes,
        input_output_aliases={n_in + nc + g: n_out + u for g, u in enumerate(given)},
        scratch_shapes=list(scratch) + [pltpu.SemaphoreType.DMA((nc, 7)), pltpu.SemaphoreType.DMA((nc, 7)),
                                        pltpu.SemaphoreType.DMA((nc,))],
        compiler_params=_params(("arbitrary",) * len(grid), vmem),
    )(*operands, *[cr.src for cr in carries], *[carries[u].dst for u in given])


def _mm(name, a, b, *, dims, grid, a_blk, a_map, b_blk, b_map, o_blk, o_map, out_shape, out_dtype,
        add=None, add_blk=None, add_map=None, carries=(), b_slabs=1):
    nk = grid[2]
    assert add is None or nk == 1
    acc_shape = tuple(d for d in o_blk if d is not None)
    in_place = out_dtype == F32

    def body(*refs):
        if add is None:
            a_ref, b_ref, o_ref = refs[:3]
            c_ref, scr = None, refs[3:]
        else:
            a_ref, b_ref, c_ref, o_ref = refs[:4]
            scr = refs[4:]
        if b_slabs == 1:
            part = lax.dot_general(a_ref[...], b_ref[...], dims, preferred_element_type=F32)
        else:
            cw = b_ref.shape[2]
            part = sum(lax.dot_general(a_ref[:, s * cw:(s + 1) * cw], b_ref[s], dims, preferred_element_type=F32)
                       for s in range(b_slabs))
        if c_ref is not None:
            part = part + c_ref[...].astype(F32)
        if nk == 1:
            o_ref[...] = part.astype(o_ref.dtype)
            return
        acc = o_ref if in_place else scr[0]
        k = pl.program_id(2)

        @pl.when(k == 0)
        def _():
            acc[...] = part

        @pl.when(k > 0 if in_place else jnp.logical_and(k > 0, k < nk - 1))
        def _():
            acc[...] += part

        if not in_place:
            @pl.when(k == nk - 1)
            def _():
                o_ref[...] = (acc[...] + part).astype(o_ref.dtype)

    in_specs = [pl.BlockSpec(a_blk, a_map), pl.BlockSpec(b_blk, b_map)]
    operands = [a, b]
    scratch = [pltpu.VMEM(acc_shape, F32)] if nk > 1 and not in_place else []
    vmem = 2 * (_nbytes(acc_shape, out_dtype) + _nbytes([d for d in a_blk if d], a.dtype)
                + _nbytes([d for d in b_blk if d], b.dtype)) + (2 + len(scratch)) * _nbytes(acc_shape, F32)
    if add is not None:
        in_specs.append(pl.BlockSpec(add_blk, add_map))
        operands.append(add)
        vmem += 2 * _nbytes(acc_shape, add.dtype)
    out, *dsts = _call(
        body, name=name, grid=grid, in_specs=in_specs, out_specs=[pl.BlockSpec(o_blk, o_map)],
        out_shape=[jax.ShapeDtypeStruct(out_shape, out_dtype)], operands=operands, scratch=scratch,
        sem=("parallel", "parallel", "arbitrary"), vmem=vmem + 8 * _MIB, carries=carries)
    return (out, dsts) if carries else out


def _mm_nn(name, x, w, out_dtype, carries=()):
    t, kd = x.shape
    n = w.shape[1]
    tm, tn, tk = _tile(t, 1024), _tile(n, 1024), _tile(kd, 1536 if kd > 2048 else 2048)
    return _mm(name, x, w, dims=_NN, grid=(t // tm, n // tn, kd // tk),
               a_blk=(tm, tk), a_map=lambda i, j, k: (i, k), b_blk=(tk, tn), b_map=lambda i, j, k: (k, j),
               o_blk=(tm, tn), o_map=lambda i, j, k: (i, j), out_shape=(t, n), out_dtype=out_dtype, carries=carries)


def _mm_nn_blk(name, x, g, out_dtype, halves=False, carries=()):
    t, kd = x.shape
    cw = g.shape[2]
    tm = _tile(t, 1024)
    if halves:
        o_blk, o_map, out_shape = (None, tm, cw), (lambda i, j, k: (j // 4, i, j % 4)), (2, t, 4 * cw)
    else:
        o_blk, o_map, out_shape = (tm, cw), (lambda i, j, k: (i, j)), (t, N_DEV * cw)
    return _mm(name, x, g, dims=_NN, grid=(t // tm, N_DEV, 1),
               a_blk=(tm, kd), a_map=lambda i, j, k: (i, 0), b_blk=(None, kd, cw), b_map=lambda i, j, k: (j, 0, 0),
               o_blk=o_blk, o_map=o_map, out_shape=out_shape, out_dtype=out_dtype, carries=carries)


def _mm_nt(name, dy, w, out_dtype, add=None, carries=()):
    t, n = dy.shape
    kd = w.shape[0]
    tm, tn = _tile(t, 1024), _tile(kd, 512)
    kw = {}
    if add is not None:
        kw = dict(add=add, add_blk=(tm, tn), add_map=lambda i, j, k: (i, j))
    return _mm(name, dy, w, dims=_NT, grid=(t // tm, kd // tn, 1),
               a_blk=(tm, n), a_map=lambda i, j, k: (i, 0), b_blk=(tn, n), b_map=lambda i, j, k: (j, 0),
               o_blk=(tm, tn), o_map=lambda i, j, k: (i, j), out_shape=(t, kd), out_dtype=out_dtype, carries=carries, **kw)


def _mm_nt_blk(name, dy, g, out_dtype, halves=False, carries=()):
    kd, cw = g.shape[1], g.shape[2]
    t = dy.shape[1] if halves else dy.shape[0]
    tm = _tile(t, 512)
    slabs = 2 if cw > 512 else 4
    nk = N_DEV // slabs
    if halves:
        a_blk, a_map = (None, tm, slabs * cw), (lambda i, j, k: (k // (nk // 2), i, k % (nk // 2)))
    else:
        a_blk, a_map = (tm, slabs * cw), (lambda i, j, k: (i, k))
    return _mm(name, dy, g, dims=_NT, grid=(t // tm, 1, nk), b_slabs=slabs,
               a_blk=a_blk, a_map=a_map, b_blk=(slabs, kd, cw), b_map=lambda i, j, k: (k, 0, 0),
               o_blk=(tm, kd), o_map=lambda i, j, k: (i, 0), out_shape=(t, kd), out_dtype=out_dtype, carries=carries)


def _mm_tn(name, x, dy, carries=()):
    t, kd = x.shape
    n = dy.shape[1]
    tmx, tk = _tile(kd, 1024), _tile(t, 2048)
    return _mm(name, x, dy, dims=_TN, grid=(kd // tmx, 1, t // tk),
               a_blk=(tk, tmx), a_map=lambda i, j, k: (k, i), b_blk=(tk, n), b_map=lambda i, j, k: (k, 0),
               o_blk=(tmx, n), o_map=lambda i, j, k: (i, 0), out_shape=(kd, n), out_dtype=BF, carries=carries)


def _mm_tn_blk(name, x, dy, halves=False, carries=()):
    t, kd = x.shape
    cw = dy.shape[2] // 4 if halves else dy.shape[1] // N_DEV
    tmx, tk = _tile(kd, 2048 if cw <= 512 else 1024), _tile(t, 2048)
    if halves:
        b_blk, b_map = (None, tk, cw), (lambda i, j, k: (j // 4, k, j % 4))
    else:
        b_blk, b_map = (tk, cw), (lambda i, j, k: (k, j))
    return _mm(name, x, dy, dims=_TN, grid=(kd // tmx, N_DEV, t // tk),
               a_blk=(tk, tmx), a_map=lambda i, j, k: (k, i), b_blk=b_blk, b_map=b_map,
               o_blk=(None, tmx, cw), o_map=lambda i, j, k: (j, i, 0), out_shape=(N_DEV, kd, cw), out_dtype=BF,
               carries=carries)


def _rms(x, g):
    r = lax.rsqrt(jnp.mean(x * x, axis=-1, keepdims=True) + NORM_EPS)
    return x * r * g


def _rms_bwd_math(x, g, dy):
    d = x.shape[-1]
    r = lax.rsqrt(jnp.mean(x * x, axis=-1, keepdims=True) + NORM_EPS)
    xh = x * r
    u = dy * g
    dx = r * u - xh * (jnp.sum(xh * u, axis=-1, keepdims=True) * (r / d))
    return dx, jnp.sum(dy * xh, axis=0, keepdims=True)


def _row_specs(tr, d, n):
    return [pl.BlockSpec((tr, d), lambda i: (i, 0)) for _ in range(n)]


def _vec_specs(d, n):
    return [pl.BlockSpec((1, d), lambda i: (0, 0)) for _ in range(n)]


def _norms_fwd(name, h, gains):
    t, d = h.shape
    tr, ng = _tile(t, 256), len(gains)

    def body(h_ref, *refs):
        x = h_ref[...]
        for g_ref, o_ref in zip(refs[:ng], refs[ng:]):
            o_ref[...] = _rms(x, g_ref[...]).astype(BF)

    return pl.pallas_call(
        body, name=name, grid=(t // tr,), in_specs=_row_specs(tr, d, 1) + _vec_specs(d, ng),
        out_specs=_row_specs(tr, d, ng), out_shape=[jax.ShapeDtypeStruct((t, d), BF)] * ng,
        compiler_params=_params(("parallel",)),
    )(h, *[g.reshape(1, d) for g in gains])


def _resid_norms(name, h, m, g_post, gains, carries=()):
    t, d = h.shape
    tr, ng = _tile(t, 256), len(gains)

    def body(h_ref, m_ref, gp_ref, *refs):
        hn = h_ref[...] + _rms(m_ref[...], gp_ref[...])
        refs[ng][...] = hn
        for g_ref, o_ref in zip(refs[:ng], refs[ng + 1:]):
            o_ref[...] = _rms(hn, g_ref[...]).astype(BF)

    return _call(
        body, name=name, grid=(t // tr,), in_specs=_row_specs(tr, d, 2) + _vec_specs(d, 1 + ng),
        out_specs=_row_specs(tr, d, 1 + ng),
        out_shape=[jax.ShapeDtypeStruct((t, d), F32)] + [jax.ShapeDtypeStruct((t, d), BF)] * ng,
        operands=(h, m, g_post.reshape(1, d), *[g.reshape(1, d) for g in gains]), sem=("parallel",), carries=carries)


def _resid_loss(name, h, m, g_post, target):
    t, d = h.shape
    tr = _tile(t, 256)

    def body(h_ref, m_ref, gp_ref, t_ref, dy_ref, loss_ref):
        diff = h_ref[...] + _rms(m_ref[...], gp_ref[...]) - t_ref[...]
        dy_ref[...] = diff * (1.0 / d)

        @pl.when(pl.program_id(0) == 0)
        def _():
            loss_ref[...] = jnp.zeros_like(loss_ref)

        per_row = jnp.sum(diff * diff, axis=-1, keepdims=True) * (1.0 / d)
        loss_ref[...] += 0.5 * jnp.sum(per_row, axis=0, keepdims=True)

    return pl.pallas_call(
        body, name=name, grid=(t // tr,),
        in_specs=_row_specs(tr, d, 2) + _vec_specs(d, 1) + _row_specs(tr, d, 1),
        out_specs=[pl.BlockSpec((tr, d), lambda i: (i, 0)), pl.BlockSpec((1, 1), lambda i: (0, 0))],
        out_shape=[jax.ShapeDtypeStruct((t, d), F32), jax.ShapeDtypeStruct((1, 1), F32)],
        compiler_params=_params(("arbitrary",)),
    )(h, m, g_post.reshape(1, d), target)


def _norm_bwd(name, x, g, dy, res=None, out_dtype=F32, carries=()):
    t, d = x.shape
    tr = _tile(t, 256)
    has_res = res is not None

    def body(x_ref, dy_ref, g_ref, *refs):
        dx_ref, dg_ref = refs[-2:]
        dx, dg = _rms_bwd_math(x_ref[...].astype(F32), g_ref[...], dy_ref[...].astype(F32))
        if has_res:
            dx = dx + refs[0][...]
        dx_ref[...] = dx.astype(dx_ref.dtype)

        @pl.when(pl.program_id(0) == 0)
        def _():
            dg_ref[...] = jnp.zeros_like(dg_ref)

        dg_ref[...] += dg

    ops = [x, dy, g.reshape(1, d)] + ([res] if has_res else [])
    return _call(
        body, name=name, grid=(t // tr,),
        in_specs=_row_specs(tr, d, 2) + _vec_specs(d, 1) + _row_specs(tr, d, int(has_res)),
        out_specs=[pl.BlockSpec((tr, d), lambda i: (i, 0)), pl.BlockSpec((1, d), lambda i: (0, 0))],
        out_shape=[jax.ShapeDtypeStruct((t, d), out_dtype), jax.ShapeDtypeStruct((1, d), F32)],
        operands=ops, sem=("arbitrary",), carries=carries)


_GELU_C = math.sqrt(2.0 / math.pi)
_GELU_A = 0.044715


def _gelu(x):
    return 0.5 * x * (1.0 + jnp.tanh(_GELU_C * (x + _GELU_A * x * x * x)))


def _gelu_and_grad(x):
    th = jnp.tanh(_GELU_C * (x + _GELU_A * x * x * x))
    grad = 0.5 * (1.0 + th) + 0.5 * x * (1.0 - th * th) * (_GELU_C * (1.0 + 3.0 * _GELU_A * x * x))
    return 0.5 * x * (1.0 + th), grad


def _causal(n):
    return lax.broadcasted_iota(jnp.int32, (n, n), 1) <= lax.broadcasted_iota(jnp.int32, (n, n), 0)


def _sgu_fwd(name, p, v_gain, w_s, b_st, carries=()):
    t, da2 = p.shape
    da = da2 // 2
    ng = da // GROUP

    def body(p_ref, vg_ref, ws_ref, bst_ref, o_ref):
        keep = _causal(GROUP)
        for g in range(ng):
            lo = g * GROUP
            u = _gelu(p_ref[:, lo:lo + GROUP])
            vn = _rms(_gelu(p_ref[:, da + lo:da + lo + GROUP]), vg_ref[:, lo:lo + GROUP])
            w = jnp.where(keep, ws_ref[g], 0.0).astype(BF)
            mixed = jnp.dot(w, vn.astype(BF), preferred_element_type=F32) + bst_ref[:, g:g + 1]
            o_ref[:, lo:lo + GROUP] = (u * mixed).astype(BF)

    return _call(
        body, name=name, grid=(t // GROUP,),
        in_specs=[pl.BlockSpec((GROUP, da2), lambda i: (i, 0)), pl.BlockSpec((1, da), lambda i: (0, 0)),
                  pl.BlockSpec((ng, GROUP, GROUP), lambda i: (0, 0, 0)), pl.BlockSpec((GROUP, ng), lambda i: (0, 0))],
        out_specs=[pl.BlockSpec((GROUP, da), lambda i: (i, 0))], out_shape=[jax.ShapeDtypeStruct((t, da), BF)],
        operands=(p, v_gain, w_s, b_st), sem=("parallel",), carries=carries)


def _sgu_bwd(name, p, v_gain, w_s, w_st, b_st, dout, carries=()):
    t, da2 = p.shape
    da = da2 // 2
    ng = da // GROUP

    def body(p_ref, vg_ref, ws_ref, wst_ref, bst_ref, do_ref, dp_ref, dvg_ref, dws_ref, dbst_ref):
        @pl.when(pl.program_id(0) == 0)
        def _():
            dvg_ref[...] = jnp.zeros_like(dvg_ref)
            dws_ref[...] = jnp.zeros_like(dws_ref)
            dbst_ref[...] = jnp.zeros_like(dbst_ref)

        keep = _causal(GROUP)
        keep_t = lax.broadcasted_iota(jnp.int32, (GROUP, GROUP), 0) <= lax.broadcasted_iota(jnp.int32, (GROUP, GROUP), 1)
        for g in range(ng):
            lo = g * GROUP
            u, du = _gelu_and_grad(p_ref[:, lo:lo + GROUP])
            v, dv_act = _gelu_and_grad(p_ref[:, da + lo:da + lo + GROUP])
            gain = vg_ref[:, lo:lo + GROUP]
            r = lax.rsqrt(jnp.mean(v * v, axis=-1, keepdims=True) + NORM_EPS)
            vh = v * r
            vnb = (vh * gain).astype(BF)
            w = jnp.where(keep, ws_ref[g], 0.0).astype(BF)
            wt = jnp.where(keep_t, wst_ref[g], 0.0).astype(BF)
            mixed = jnp.dot(w, vnb, preferred_element_type=F32) + bst_ref[:, g:g + 1]
            dout_g = do_ref[:, lo:lo + GROUP].astype(F32)
            dmixed = dout_g * u
            dmb = dmixed.astype(BF)
            dbst_ref[:, g:g + 1] += jnp.sum(dmixed, axis=1, keepdims=True)
            dws_ref[g] += jnp.where(keep, lax.dot_general(dmb, vnb, _NT, preferred_element_type=F32), 0.0)
            dvn = jnp.dot(wt, dmb, preferred_element_type=F32)
            dvg_ref[:, lo:lo + GROUP] += jnp.sum(dvn * vh, axis=0, keepdims=True)
            dvh = dvn * gain
            dv = r * dvh - vh * (jnp.sum(vh * dvh, axis=-1, keepdims=True) * (r / GROUP))
            dp_ref[:, lo:lo + GROUP] = (dout_g * mixed * du).astype(BF)
            dp_ref[:, da + lo:da + lo + GROUP] = (dv * dv_act).astype(BF)

    full = lambda *shape: pl.BlockSpec(shape, lambda i: (0,) * len(shape))
    outs = _call(
        body, name=name, grid=(t // GROUP,),
        in_specs=[pl.BlockSpec((GROUP, da2), lambda i: (i, 0)), full(1, da), full(ng, GROUP, GROUP), full(ng, GROUP, GROUP),
                  full(GROUP, ng), pl.BlockSpec((GROUP, da), lambda i: (i, 0))],
        out_specs=[pl.BlockSpec((GROUP, da2), lambda i: (i, 0)), full(1, da), full(ng, GROUP, GROUP), full(GROUP, ng)],
        out_shape=[jax.ShapeDtypeStruct((t, da2), BF), jax.ShapeDtypeStruct((1, da), F32),
                   jax.ShapeDtypeStruct((ng, GROUP, GROUP), F32), jax.ShapeDtypeStruct((GROUP, ng), F32)],
        operands=(p, v_gain, w_s, w_st, b_st, dout), sem=("arbitrary",), carries=carries)
    return outs[:4], outs[4:]


_CONV_ROWS = 256
_CONV_COLS = 128


def _shift_down(x, prev, k):
    top = pltpu.roll(jnp.concatenate([prev, x[:8]], axis=0), k, 0)[8:16]
    return jnp.concatenate([top, pltpu.roll(x, k, 0)[8:]], axis=0)


def _shift_up(x, nxt, k):
    n = x.shape[0]
    bottom = pltpu.roll(jnp.concatenate([x[n - 8:], nxt], axis=0), 16 - k, 0)[0:8]
    return jnp.concatenate([pltpu.roll(x, n - k, 0)[:n - 8], bottom], axis=0)


def _conv_taps(a_ref, half, r0, first):
    x = a_ref[half, pl.ds(r0, _rows(a_ref)), :]
    prev = a_ref[half, pl.ds(jnp.maximum(r0 - 8, 0), 8), :]
    prev = jnp.where(first, 0.0, prev)
    return x, _shift_down(x, prev, 1), _shift_down(x, prev, 2)


def _rows(a_ref):
    return min(_CONV_ROWS, a_ref.shape[1])


def _conv_fwd(name, a3, cw, cb, carries=()):
    _, t, f = a3.shape
    tc, rows = _CONV_COLS, min(_CONV_ROWS, t)

    def body(a_ref, cw_ref, cb_ref, y_ref):
        def step(r, carry):
            r0 = pl.multiple_of(r * rows, rows)
            c = []
            for half in range(2):
                x, x1, x2 = _conv_taps(a_ref, half, r0, r == 0)
                w = cw_ref[half]
                c.append(cb_ref[half] + (w[0:1] * x2 + w[1:2] * x1 + w[2:3] * x))
            y_ref[pl.ds(r0, rows), :] = (c[0] * jax.nn.sigmoid(c[0]) * c[1]).astype(BF)
            return carry

        lax.fori_loop(0, t // rows, step, 0)

    col = lambda *lead: pl.BlockSpec((*lead, tc), lambda j: (0,) * len(lead) + (j,))
    y, *dsts = _call(
        body, name=name, grid=(f // tc,), in_specs=[col(2, t), col(2, CONV_TAPS), col(2, 1)],
        out_specs=[col(t)], out_shape=[jax.ShapeDtypeStruct((t, f), BF)], operands=(a3, cw, cb),
        sem=("parallel",), vmem=40 * _MIB, carries=carries)
    return y, dsts


def _conv_bwd(name, a3, cw, cb, dy, carries=()):
    _, t, f = a3.shape
    tc, rows = _CONV_COLS, min(_CONV_ROWS, t)
    n_steps = t // rows

    def body(a_ref, cw_ref, cb_ref, dy_ref, da_ref, dcw_ref, dcb_ref):
        dcw_ref[...] = jnp.zeros_like(dcw_ref)
        dcb_ref[...] = jnp.zeros_like(dcb_ref)

        def step(s, nxt):
            r = n_steps - 1 - s
            r0 = pl.multiple_of(r * rows, rows)
            taps, c = [], []
            for half in range(2):
                x, x1, x2 = _conv_taps(a_ref, half, r0, r == 0)
                w = cw_ref[half]
                taps.append((x2, x1, x))
                c.append(cb_ref[half] + (w[0:1] * x2 + w[1:2] * x1 + w[2:3] * x))
            gate, val = c
            sg = jax.nn.sigmoid(gate)
            dyv = dy_ref[pl.ds(r0, rows), :].astype(F32)
            dcs = (dyv * val * (sg * (1.0 + gate * (1.0 - sg))), dyv * (gate * sg))
            new_nxt = []
            for half in range(2):
                dc, w = dcs[half], cw_ref[half]
                dcb_ref[half] += jnp.sum(dc, axis=0, keepdims=True)
                for tap in range(CONV_TAPS):
                    dcw_ref[half, tap:tap + 1, :] += jnp.sum(dc * taps[half][tap], axis=0, keepdims=True)
                da = w[2:3] * dc + w[1:2] * _shift_up(dc, nxt[half], 1) + w[0:1] * _shift_up(dc, nxt[half], 2)
                da_ref[half, pl.ds(r0, rows), :] = da.astype(BF)
                new_nxt.append(dc[:8])
            return tuple(new_nxt)

        zeros = jnp.zeros((8, tc), F32)
        lax.fori_loop(0, n_steps, step, (zeros, zeros))

    col = lambda *lead: pl.BlockSpec((*lead, tc), lambda j: (0,) * len(lead) + (j,))
    outs = _call(
        body, name=name, grid=(f // tc,), in_specs=[col(2, t), col(2, CONV_TAPS), col(2, 1), col(t)],
        out_specs=[col(2, t), col(2, CONV_TAPS), col(2, 1)],
        out_shape=[jax.ShapeDtypeStruct((2, t, f), BF), jax.ShapeDtypeStruct((2, CONV_TAPS, f), F32),
                   jax.ShapeDtypeStruct((2, 1, f), F32)],
        operands=(a3, cw, cb, dy), sem=("parallel",), vmem=40 * _MIB, carries=carries)
    return outs[:3], outs[3:]


_ATT_BLOCK = 256


def _split_dot(x, tri):
    hi = x.astype(BF)
    lo = (x - hi.astype(F32)).astype(BF)
    return jnp.dot(hi, tri, preferred_element_type=F32) + jnp.dot(lo, tri, preferred_element_type=F32)


_ATT_HEADS_FWD = 4
_ATT_HEADS_BWD = 4


def _logits(qb, kb, diagonal):
    z = lax.dot_general(qb, kb, _NT, preferred_element_type=F32) * (1.0 / math.sqrt(GROUP))
    lb = jnp.minimum(z, 0.0) - jnp.log(1.0 + jnp.exp(-jnp.abs(z)))
    if not diagonal:
        return lb, lb - z, None
    mask = lax.broadcasted_iota(jnp.int32, z.shape, 1) < lax.broadcasted_iota(jnp.int32, z.shape, 0)
    return lb, jnp.where(mask, lb - z, 0.0), mask


def _head(ref, g, rows=slice(None)):
    return ref[rows, g * GROUP:(g + 1) * GROUP]


def _attn_fwd(name, q, k, v, carries=()):
    t, hd = q.shape
    blk = min(_ATT_BLOCK, t)
    heads = min(_ATT_HEADS_FWD, hd // GROUP)
    gs = range(heads)

    def body(q_ref, k_ref, v_ref, o_ref, rest_ref, first_ref):
        hg, i = pl.program_id(0), pl.program_id(1)
        ri = lax.broadcasted_iota(jnp.int32, (blk, blk), 0)
        ci = lax.broadcasted_iota(jnp.int32, (blk, blk), 1)
        tri = (ri >= ci).astype(BF)

        def tile(j, state, diagonal):
            keys = pl.ds(pl.multiple_of(j * blk, blk), blk)
            logit = [_logits(_head(q_ref, g), _head(k_ref, g, keys), diagonal) for g in gs]
            incl = [_split_dot(logit[g][1], tri) for g in gs]
            a = [jnp.exp(logit[g][0] + (incl[g] - logit[g][1] + state[g][0])) for g in gs]
            if diagonal:
                a = [jnp.where(logit[g][2], a[g], 0.0) for g in gs]
            out = [state[g][1] + jnp.dot(a[g].astype(BF), _head(v_ref, g, keys), preferred_element_type=F32) for g in gs]
            return tuple((state[g][0] + incl[g][:, 0:1], out[g]) for g in gs)

        state = tile(i, ((jnp.zeros((blk, 1), F32), jnp.zeros((blk, GROUP), F32)),) * heads, True)

        def more(carry):
            j, state = carry
            live = functools.reduce(jnp.maximum, [jnp.max(right) for right, _ in state])
            return jnp.logical_and(j >= 0, live > EXP_FLOOR)

        j, state = lax.while_loop(more, lambda c: (c[0] - 1, tile(c[0], c[1], False)), (i - 1, state))
        for g, (right, acc) in enumerate(state):
            o_ref[:, g * GROUP:(g + 1) * GROUP] = acc.astype(BF)
            rest_ref[:, g * GROUP:(g + 1) * GROUP] = jnp.broadcast_to(right, (blk, GROUP))
        first_ref[hg, i] = (j + 1).astype(F32)

    qspec = pl.BlockSpec((blk, heads * GROUP), lambda h, i: (i, h))
    kvspec = pl.BlockSpec((t, heads * GROUP), lambda h, i: (0, h))
    groups = hd // (heads * GROUP)
    outs = _call(
        body, name=name, grid=(groups, t // blk), in_specs=[qspec, kvspec, kvspec],
        out_specs=[qspec, qspec, pl.BlockSpec(memory_space=pltpu.SMEM)],
        out_shape=[jax.ShapeDtypeStruct((t, hd), BF), jax.ShapeDtypeStruct((t, hd), F32),
                   jax.ShapeDtypeStruct((groups, t // blk), F32)],
        operands=(q, k, v), sem=("arbitrary", "arbitrary"), vmem=40 * _MIB, carries=carries, middle_at=0.75)
    return outs[:3], outs[3:]


def _attn_bwd(name, q, k, v, rest, first, do, carries=()):
    t, hd = q.shape
    blk = min(_ATT_BLOCK, t)
    nq = t // blk
    scale = 1.0 / math.sqrt(GROUP)
    heads = min(_ATT_HEADS_BWD, hd // GROUP)
    per_first = min(_ATT_HEADS_FWD, hd // GROUP) // heads
    gs = range(heads)

    def body(first_ref, q_ref, k_ref, v_ref, rest_ref, do_ref, dq_ref, dk_ref, dv_ref, dk_acc, dv_acc):
        hg, i = pl.program_id(0), pl.program_id(1)

        @pl.when(i == 0)
        def _():
            dk_acc[...] = jnp.zeros_like(dk_acc)
            dv_acc[...] = jnp.zeros_like(dv_acc)

        ri = lax.broadcasted_iota(jnp.int32, (blk, blk), 0)
        ci = lax.broadcasted_iota(jnp.int32, (blk, blk), 1)
        tri = (ri <= ci).astype(BF)

        def tile(j, state, diagonal):
            keys = pl.ds(pl.multiple_of(j * blk, blk), blk)
            qs, dos = [_head(q_ref, g) for g in gs], [_head(do_ref, g) for g in gs]
            kb, vb = [_head(k_ref, g, keys) for g in gs], [_head(v_ref, g, keys) for g in gs]
            logit = [_logits(qs[g], kb[g], diagonal) for g in gs]
            pre = [_split_dot(logit[g][1], tri) for g in gs]
            a = [jnp.exp(logit[g][0] + (rest_ref[:, g * GROUP:g * GROUP + 1] - state[g][0] - pre[g])) for g in gs]
            if diagonal:
                a = [jnp.where(logit[g][2], a[g], 0.0) for g in gs]
            gw = [a[g] * lax.dot_general(dos[g], vb[g], _NT, preferred_element_type=F32) for g in gs]
            gpre = [_split_dot(gw[g], tri) for g in gs]
            dz = []
            for g in gs:
                beta = jnp.exp(logit[g][0])
                d = (gw[g] * (1.0 - beta) - (state[g][1] + gpre[g] - gw[g]) * beta) * scale
                dz.append((jnp.where(logit[g][2], d, 0.0) if diagonal else d).astype(BF))
            for g in gs:
                dk_acc[keys, g * GROUP:(g + 1) * GROUP] += lax.dot_general(dz[g], qs[g], _TN, preferred_element_type=F32)
                dv_acc[keys, g * GROUP:(g + 1) * GROUP] += lax.dot_general(a[g].astype(BF), dos[g], _TN,
                                                                           preferred_element_type=F32)
            return tuple((state[g][0] + pre[g][:, blk - 1:blk], state[g][1] + gpre[g][:, blk - 1:blk],
                          state[g][2] + jnp.dot(dz[g], kb[g], preferred_element_type=F32)) for g in gs)

        zero = jnp.zeros((blk, 1), F32)
        first_block = jnp.clip(first_ref[hg // per_first, i].astype(jnp.int32), 0, i)
        state = lax.fori_loop(first_block, i, lambda j, c: tile(j, c, False),
                              ((zero, zero, jnp.zeros((blk, GROUP), F32)),) * heads)
        for g, (_, _, dq) in enumerate(tile(i, state, True)):
            dq_ref[:, g * GROUP:(g + 1) * GROUP] = dq.astype(BF)

        @pl.when(i == nq - 1)
        def _():
            dk_ref[...] = dk_acc[...].astype(BF)
            dv_ref[...] = dv_acc[...].astype(BF)

    qspec = pl.BlockSpec((blk, heads * GROUP), lambda h, i: (i, h))
    kvspec = pl.BlockSpec((t, heads * GROUP), lambda h, i: (0, h), pipeline_mode=pl.Buffered(1))
    outs = _call(
        body, name=name, grid=(hd // (heads * GROUP), nq),
        in_specs=[pl.BlockSpec(memory_space=pltpu.SMEM), qspec, kvspec, kvspec, qspec, qspec],
        out_specs=[qspec, kvspec, kvspec], out_shape=[jax.ShapeDtypeStruct((t, hd), BF)] * 3,
        operands=(first, q, k, v, rest, do),
        scratch=[pltpu.VMEM((t, heads * GROUP), F32), pltpu.VMEM((t, heads * GROUP), F32)],
        sem=("arbitrary", "arbitrary"), vmem=48 * _MIB, carries=carries)
    return outs[:3], outs[3:]


def _adamw_math(w, g, m, v):
    m = ADAM_B1 * m + (1.0 - ADAM_B1) * g
    v = ADAM_B2 * v + (1.0 - ADAM_B2) * (g * g)
    m_hat = m / (1.0 - ADAM_B1 ** ADAM_STEP)
    v_hat = v / (1.0 - ADAM_B2 ** ADAM_STEP)
    return -ADAM_LR * (m_hat / (jnp.sqrt(v_hat) + ADAM_EPS) + ADAM_WD * w), m, v


def _sum_adamw(name, parts, w, m, v):
    layers, r, c = w.shape
    budget = 512 * 1024 // layers
    tr = r if r * c <= budget else _tile_rows(r, max(8, (budget // c) // 8 * 8))
    n = r // tr

    def body(*refs):
        p_refs, (w_ref, m_ref, v_ref, g_ref, d_ref, nm_ref, nv_ref) = refs[:layers], refs[layers:]
        for layer, p_ref in enumerate(p_refs):
            @pl.when(pl.program_id(0) == layer)
            def _(p_ref=p_ref):
                g = p_ref[0].astype(F32)
                for dev in range(1, N_DEV):
                    g = g + p_ref[dev].astype(F32)
                g_ref[...] = g
                d_ref[...], nm_ref[...], nv_ref[...] = _adamw_math(w_ref[...], g, m_ref[...], v_ref[...])

    def part_spec(layer):
        return pl.BlockSpec((N_DEV, tr, c), lambda l, i: (0, jnp.where(l < layer, 0, jnp.where(l == layer, i, n - 1)), 0))

    row = pl.BlockSpec((None, tr, c), lambda l, i: (l, i, 0))
    return pl.pallas_call(
        body, name=name, grid=(layers, n), in_specs=[part_spec(layer) for layer in range(layers)] + [row] * 3,
        out_specs=[row] * 4, out_shape=[jax.ShapeDtypeStruct((layers, r, c), F32)] * 4,
        compiler_params=_params(("arbitrary", "arbitrary"), 40 * _MIB),
    )(*parts, w, m, v)


def _tile_rows(r, pref):
    t = min(r, pref)
    while r % t or t % 8:
        t -= 1
    return t


def _exchange(name, carries):
    return _call(lambda: None, name=name, grid=(1,), in_specs=[], out_specs=[], out_shape=[], operands=(), carries=carries)


def _all_reduce_small(name, groups):
    c = groups[0][0].shape[1]
    parts, offsets, starts, r = [], [], [], 0
    for group in groups:
        starts.append(r)
        for p in group:
            parts.append(p)
            offsets.append(r)
            r += p.shape[0]
        r = -(-r // 8) * 8
    n = len(parts)

    def body(*refs):
        out_ref, slots, send_sems, recv_sems = refs[n:]
        x, y, c_, me = _place()
        slots[me] = jnp.zeros((r, c), F32)
        for p_ref, off in zip(refs[:n], offsets):
            slots[me, off:off + p_ref.shape[0], :] = p_ref[...]

        def copy(k):
            return pltpu.make_async_remote_copy(
                src_ref=slots.at[me], dst_ref=slots.at[me], send_sem=send_sems.at[k - 1], recv_sem=recv_sems.at[k - 1],
                device_id=_flip(x, y, c_, k), device_id_type=MESH)

        for k in range(1, N_DEV):
            copy(k).start()
        for k in range(1, N_DEV):
            copy(k).wait()
        total = slots[0]
        for dev in range(1, N_DEV):
            total = total + slots[dev]
        out_ref[...] = total

    vm = pl.BlockSpec(memory_space=pltpu.VMEM)
    summed = pl.pallas_call(
        body, name=name, in_specs=[vm] * n, out_specs=vm, out_shape=jax.ShapeDtypeStruct((r, c), F32),
        scratch_shapes=[pltpu.VMEM((N_DEV, r, c), F32), pltpu.SemaphoreType.DMA((7,)), pltpu.SemaphoreType.DMA((7,))],
        compiler_params=_params(None, (N_DEV + 6) * r * c * 4 + 8 * _MIB),
    )(*parts)
    return summed, starts


def _adamw(name, g, w, m, v):
    cols = w.shape[-1]
    flat = lambda a: a.reshape(-1, cols)

    def body(g_ref, w_ref, m_ref, v_ref, d_ref, nm_ref, nv_ref):
        d_ref[...], nm_ref[...], nv_ref[...] = _adamw_math(w_ref[...], g_ref[...], m_ref[...], v_ref[...])

    outs = pl.pallas_call(body, name=name, out_shape=[jax.ShapeDtypeStruct(flat(w).shape, F32)] * 3)(
        flat(g), flat(w), flat(m), flat(v))
    return [o.reshape(w.shape) for o in outs]


def kernel(x, pre_mix_g, post_mix_g, pre_ffn_g, post_ffn_g, a_w_in, a_v_norm_g, a_w_spatial, a_b_spatial, a_w_out, kv_norm_g, w_k, w_v, b_w_q, b_w_o, ffn_w_up, ffn_conv_w, ffn_conv_b, ffn_w_down, loss_target, m_pre_mix_g, m_post_mix_g, m_pre_ffn_g, m_post_ffn_g, m_a_w_in, m_a_v_norm_g, m_a_w_spatial, m_a_b_spatial, m_a_w_out, m_kv_norm_g, m_w_k, m_w_v, m_b_w_q, m_b_w_o, m_ffn_w_up, m_ffn_conv_w, m_ffn_conv_b, m_ffn_w_down, v_pre_mix_g, v_post_mix_g, v_pre_ffn_g, v_post_ffn_g, v_a_w_in, v_a_v_norm_g, v_a_w_spatial, v_a_b_spatial, v_a_w_out, v_kv_norm_g, v_w_k, v_w_v, v_b_w_q, v_b_w_o, v_ffn_w_up, v_ffn_conv_w, v_ffn_conv_b, v_ffn_w_down):
    t, d = x.shape[1], x.shape[2]
    f = ffn_w_down.shape[1] * N_DEV
    ng = d // GROUP
    me = 4 * lax.axis_index("x") + 2 * lax.axis_index("y") + lax.axis_index("c")
    x2, target = x.reshape(t, d), loss_target.reshape(t, d)

    g_in, g_cw, g_vg = _exchange("gather_first", [
        _gather(a_w_in[0].astype(BF)), _gather(ffn_conv_w.reshape(2 * CONV_TAPS, -1)), _gather(a_v_norm_g)])
    up0 = ffn_w_up[0].astype(BF)
    cw_full = jnp.transpose(g_cw.reshape(N_DEV, 2, CONV_TAPS, -1), (1, 2, 0, 3)).reshape(2, CONV_TAPS, 2, f)
    cw_l = [jnp.transpose(cw_full[l], (1, 0, 2)) for l in range(2)]
    cb_l = [ffn_conv_b[l].reshape(2, 1, f) for l in range(2)]
    vg_full = g_vg.reshape(1, d)
    w_s = a_w_spatial[0]
    w_st = jnp.swapaxes(w_s, 1, 2)
    b_st = a_b_spatial[0].T

    (hn0,) = _norms_fwd("pre_mix0", x2, [pre_mix_g[0]])
    p0, (g_out, g_up0) = _mm_nn_blk("sgu_in", hn0, g_in, F32, carries=[
        _gather(a_w_out[0].astype(BF)), _gather(up0, 0, d // 4)])
    w_out_f = g_out.reshape(d, d)
    sg, g_up0 = _sgu_fwd("sgu", p0, vg_full, w_s, b_st, carries=[_gather(up0, d // 4, d // 2, dst=g_up0)])
    mix0, (g_up0,) = _mm_nn("sgu_out", sg, w_out_f, F32, carries=[_gather(up0, d // 2, 3 * d // 4, dst=g_up0)])
    h1, fn0, g_up0 = _resid_norms("post_mix0", x2, mix0, post_mix_g[0], [pre_ffn_g[0]],
                                  carries=[_gather(up0, 3 * d // 4, d, dst=g_up0)])
    a3_0, (g_dn0,) = _mm_nn_blk("ffn0_up", fn0, g_up0, F32, halves=True, carries=[_gather(ffn_w_down[0].astype(BF))])
    y0, (g_q,) = _conv_fwd("ffn0_conv", a3_0, cw_l[0], cb_l[0], carries=[_gather(b_w_q[0].astype(BF))])
    f0, (g_k, g_v) = _mm_nn("ffn0_down", y0, g_dn0.reshape(f, d), F32,
                            carries=[_gather(w_k.astype(BF)), _gather(w_v.astype(BF))])
    w_q_f, w_k_f, w_v_f = g_q.reshape(d, d), g_k.reshape(d, d), g_v.reshape(d, d)
    h2, hn1, kvn = _resid_norms("post_ffn0", h1, f0, post_ffn_g[0], [pre_mix_g[1], kv_norm_g])
    q = _mm_nn("attn_q", hn1, w_q_f, BF)
    kk = _mm_nn("attn_k", kvn, w_k_f, BF)
    vv = _mm_nn("attn_v", kvn, w_v_f, BF)
    (att, rest, first), (g_o, g_up1) = _attn_fwd(
        "attn", q, kk, vv, carries=[_gather(b_w_o[0].astype(BF)), _gather(ffn_w_up[1].astype(BF))])
    w_o_f = g_o.reshape(d, d)
    mix1 = _mm_nn("attn_o", att, w_o_f, F32)
    h3, fn1 = _resid_norms("post_mix1", h2, mix1, post_mix_g[1], [pre_ffn_g[1]])
    a3_1, (g_dn1,) = _mm_nn_blk("ffn1_up", fn1, g_up1, F32, halves=True, carries=[_gather(ffn_w_down[1].astype(BF))])
    y1, _ = _conv_fwd("ffn1_conv", a3_1, cw_l[1], cb_l[1])
    f1 = _mm_nn("ffn1_down", y1, g_dn1.reshape(f, d), F32)
    g_up = (g_up0, g_up1)
    w_dn_f = (g_dn0.reshape(f, d), g_dn1.reshape(f, d))
    dh, loss_part = _resid_loss("loss", h3, f1, post_ffn_g[1], target)
    loss = lax.psum(loss_part[0, 0], ("x", "y", "c"))

    def blocks(dw):
        return dw.reshape(N_DEV, -1, d)

    def split(result, carries):
        return result if carries else (result, [])

    def ffn_bwd(l, dh_out, h_in, fn, a3, yv, fo, with_dw=(), with_dx=(), with_up=()):
        dfo, d_post = _norm_bwd(f"post_ffn{l}_bwd", fo, post_ffn_g[l], dh_out, out_dtype=BF)
        dw_dn, sent_dw = split(_mm_tn(f"ffn{l}_down_dw", yv, dfo, carries=with_dw), with_dw)
        dy, sent_dx = split(_mm_nt(f"ffn{l}_down_dx", dfo, w_dn_f[l], BF, carries=with_dx), with_dx)
        (da3, dcw, dcb), (p_dn,) = _conv_bwd(f"ffn{l}_conv_bwd", a3, cw_l[l], cb_l[l], dy, carries=[_scatter(blocks(dw_dn))])
        dw_up, sent_up = split(_mm_tn_blk(f"ffn{l}_up_dw", fn, da3, halves=True, carries=with_up), with_up)
        dfn, (p_up,) = _mm_nt_blk(f"ffn{l}_up_dx", da3, g_up[l], F32, halves=True, carries=[_scatter(dw_up, 0, d // 2)])
        dh_in, d_pre, p_up = _norm_bwd(f"pre_ffn{l}_bwd", h_in, pre_ffn_g[l], dfn, res=dh_out,
                                       carries=[_scatter(dw_up, d // 2, 9 * d // 16, dst=p_up)])
        return dh_in, d_post, d_pre, dcw, dcb, p_dn, dw_up, p_up, list(sent_dw) + list(sent_dx) + list(sent_up)

    dh3, d_post_ffn1, d_pre_ffn1, dcw1, dcb1, p_dn1, dw_up1, p_up1, _ = ffn_bwd(1, dh, h3, fn1, a3_1, y1, f1)
    dmix1, d_post_mix1 = _norm_bwd("post_mix1_bwd", mix1, post_mix_g[1], dh3, out_dtype=BF)
    dw_o = _mm_tn("attn_o_dw", att, dmix1)
    datt = _mm_nt("attn_o_dx", dmix1, w_o_f, BF)
    (dq, dk, dv), (p_up1, p_o) = _attn_bwd(
        "attn_bwd", q, kk, vv, rest, first, datt, carries=[_scatter(dw_up1, 9 * d // 16, d, dst=p_up1), _scatter(blocks(dw_o))])
    dw_q = _mm_tn("attn_q_dw", hn1, dq)
    dw_k = _mm_tn("attn_k_dw", kvn, dk)
    dw_v = _mm_tn("attn_v_dw", kvn, dv)
    dhn1 = _mm_nt("attn_q_dx", dq, w_q_f, F32)
    dkvn = _mm_nt("attn_v_dx", dv, w_v_f, F32, add=_mm_nt("attn_k_dx", dk, w_k_f, F32))
    dh2a, d_pre_mix1 = _norm_bwd("pre_mix1_bwd", h2, pre_mix_g[1], dhn1, res=dh3)
    dh2, d_kv = _norm_bwd("kv_norm_bwd", h2, kv_norm_g, dkvn, res=dh2a)
    dh1, d_post_ffn0, d_pre_ffn0, dcw0, dcb0, p_dn0, dw_up0, p_up0, (p_q, p_k, p_v) = ffn_bwd(
        0, dh2, h1, fn0, a3_0, y0, f0, with_dw=[_scatter(blocks(dw_q))], with_dx=[_scatter(blocks(dw_k))],
        with_up=[_scatter(blocks(dw_v))])
    dmix0, d_post_mix0 = _norm_bwd("post_mix0_bwd", mix0, post_mix_g[0], dh1, out_dtype=BF)
    dw_out, (p_up0,) = _mm_tn("sgu_out_dw", sg, dmix0, carries=[_scatter(dw_up0, 9 * d // 16, 11 * d // 16, dst=p_up0)])
    dsg = _mm_nt("sgu_out_dx", dmix0, w_out_f, BF)
    (dp0, d_vg, d_ws, d_bst), (p_up0,) = _sgu_bwd(
        "sgu_bwd", p0, vg_full, w_s, w_st, b_st, dsg, carries=[_scatter(dw_up0, 11 * d // 16, 15 * d // 16, dst=p_up0)])
    dw_in, (p_out,) = _mm_tn_blk("sgu_in_dw", hn0, dp0, carries=[_scatter(blocks(dw_out))])
    dhn0, (p_in,) = _mm_nt_blk("sgu_in_dx", dp0, g_in, F32, carries=[_scatter(dw_in, 0, 5 * d // 8)])
    grad_x, d_pre_mix0, p_in = _norm_bwd("pre_mix0_bwd", x2, pre_mix_g[0], dhn0, res=dh1,
                                         carries=[_scatter(dw_in, 5 * d // 8, 7 * d // 8, dst=p_in)])
    p_up0, p_in = _exchange("scatter_last", [_scatter(dw_up0, 15 * d // 16, d, dst=p_up0),
                                             _scatter(dw_in, 7 * d // 8, d, dst=p_in)])

    def conv_w_grad(dcw):
        return jnp.transpose(dcw, (1, 0, 2)).reshape(CONV_TAPS, 2 * f)

    small = [
        ([d_pre_mix0, d_pre_mix1], (2, d)), ([d_post_mix0, d_post_mix1], (2, d)),
        ([d_pre_ffn0, d_pre_ffn1], (2, d)), ([d_post_ffn0, d_post_ffn1], (2, d)),
        ([d_kv], (d,)), ([d_vg], (1, d)), ([d_bst.T], (1, ng, GROUP)), ([d_ws], (1, ng, GROUP, GROUP)),
        ([dcb0, dcb1], (2, 2 * f)), ([conv_w_grad(dcw0), conv_w_grad(dcw1)], (2, CONV_TAPS, 2 * f)),
    ]
    width = V7X_LANES * math.gcd(d // V7X_LANES, 2 * f // V7X_LANES)
    summed, offsets = _all_reduce_small("reduce_small", [[a.reshape(-1, width) for a in group] for group, _ in small])
    full = [summed[off:off + math.prod(shape) // width].reshape(shape) for off, (_, shape) in zip(offsets, small)]
    g_pre_mix, g_post_mix, g_pre_ffn, g_post_ffn, g_kv, g_vgain, g_bs, g_ws, g_cb, g_cwf = full
    cw_w = 2 * f // N_DEV
    g_vgain = lax.dynamic_slice_in_dim(g_vgain, me * (d // N_DEV), d // N_DEV, axis=1)
    g_cwf = lax.dynamic_slice_in_dim(g_cwf, me * cw_w, cw_w, axis=2)

    parts = [p_in, p_out, p_k, p_v, p_q, p_o, p_up0, p_up1, p_dn0, p_dn1]

    def small_update(name, g, w, m, v):
        return [g] + _adamw(name, g, w, m, v)

    def stacked(name, part0, part1, w, m, v):
        shape = (2, *part0.shape[1:])
        return [o.reshape(w.shape) for o in _sum_adamw(name, [part0, part1], w.reshape(shape), m.reshape(shape), v.reshape(shape))]

    def single(name, part, w, m, v):
        shape = (1, *part.shape[1:])
        return [o.reshape(w.shape) for o in _sum_adamw(name, [part], w.reshape(shape), m.reshape(shape), v.reshape(shape))]

    results = {
        "pre_mix_g": small_update("adam_pre_mix", g_pre_mix, pre_mix_g, m_pre_mix_g, v_pre_mix_g),
        "post_mix_g": small_update("adam_post_mix", g_post_mix, post_mix_g, m_post_mix_g, v_post_mix_g),
        "pre_ffn_g": small_update("adam_pre_ffn", g_pre_ffn, pre_ffn_g, m_pre_ffn_g, v_pre_ffn_g),
        "post_ffn_g": small_update("adam_post_ffn", g_post_ffn, post_ffn_g, m_post_ffn_g, v_post_ffn_g),
        "a_w_in": single("adam_a_w_in", parts[0], a_w_in, m_a_w_in, v_a_w_in),
        "a_v_norm_g": small_update("adam_a_v_norm", g_vgain, a_v_norm_g, m_a_v_norm_g, v_a_v_norm_g),
        "a_w_spatial": small_update("adam_a_w_spatial", g_ws, a_w_spatial, m_a_w_spatial, v_a_w_spatial),
        "a_b_spatial": small_update("adam_a_b_spatial", g_bs, a_b_spatial, m_a_b_spatial, v_a_b_spatial),
        "a_w_out": single("adam_a_w_out", parts[1], a_w_out, m_a_w_out, v_a_w_out),
        "kv_norm_g": small_update("adam_kv_norm", g_kv, kv_norm_g, m_kv_norm_g, v_kv_norm_g),
        "w_k": single("adam_w_k", parts[2], w_k, m_w_k, v_w_k),
        "w_v": single("adam_w_v", parts[3], w_v, m_w_v, v_w_v),
        "b_w_q": single("adam_b_w_q", parts[4], b_w_q, m_b_w_q, v_b_w_q),
        "b_w_o": single("adam_b_w_o", parts[5], b_w_o, m_b_w_o, v_b_w_o),
        "ffn_w_up": stacked("adam_ffn_w_up", parts[6], parts[7], ffn_w_up, m_ffn_w_up, v_ffn_w_up),
        "ffn_conv_w": small_update("adam_ffn_conv_w", g_cwf, ffn_conv_w, m_ffn_conv_w, v_ffn_conv_w),
        "ffn_conv_b": small_update("adam_ffn_conv_b", g_cb, ffn_conv_b, m_ffn_conv_b, v_ffn_conv_b),
        "ffn_w_down": stacked("adam_ffn_w_down", parts[8], parts[9], ffn_w_down, m_ffn_w_down, v_ffn_w_down),
    }
    order = ["pre_mix_g", "post_mix_g", "pre_ffn_g", "post_ffn_g", "a_w_in", "a_v_norm_g", "a_w_spatial", "a_b_spatial",
             "a_w_out", "kv_norm_g", "w_k", "w_v", "b_w_q", "b_w_o", "ffn_w_up", "ffn_conv_w", "ffn_conv_b", "ffn_w_down"]
    outs = [loss, grad_x.reshape(x.shape)]
    for idx in range(4):
        outs += [results[n][idx] for n in order]
    return tuple(outs)
```

```python
import functools
import math
from typing import NamedTuple, Optional

import jax
import jax.numpy as jnp
from jax import lax
from jax.experimental import pallas as pl
from jax.experimental.pallas import tpu as pltpu

F32 = jnp.float32
BF = jnp.bfloat16
MESH = pl.DeviceIdType.MESH

N_DEV = 8
NORM_EPS = 1e-6
GROUP = 128
CONV_TAPS = 3
ADAM_LR, ADAM_B1, ADAM_B2, ADAM_EPS, ADAM_WD, ADAM_STEP = 0.001, 0.9, 0.999, 1e-08, 0.01, 10
EXP_FLOOR = -104.0

V7X_LANES = 128
V7X_VMEM_BYTES = 64 * 1024 * 1024
_MIB = 1024 * 1024

_NN = (((1,), (0,)), ((), ()))
_NT = (((1,), (1,)), ((), ()))
_TN = (((0,), (0,)), ((), ()))


def _tile(n, pref):
    if n <= pref:
        return n
    t = (pref // V7X_LANES) * V7X_LANES
    while t > V7X_LANES and n % t:
        t -= V7X_LANES
    assert n % t == 0, (n, pref)
    return t


def _nbytes(shape, dtype):
    return math.prod(shape) * jnp.dtype(dtype).itemsize


def _params(sem=None, vmem=None):
    kw = {}
    if sem is not None:
        kw["dimension_semantics"] = sem
    if vmem is not None:
        kw["vmem_limit_bytes"] = int(min(max(vmem, 16 * _MIB), V7X_VMEM_BYTES - 8 * _MIB))
    return pltpu.CompilerParams(**kw)


def _place():
    x, y, c = lax.axis_index("x"), lax.axis_index("y"), lax.axis_index("c")
    return x, y, c, 4 * x + 2 * y + c


def _flip(x, y, c, k):
    return (1 - x if k & 4 else x, 1 - y if k & 2 else y, 1 - c if k & 1 else c)


class _Carry(NamedTuple):
    gather: bool
    src: jax.Array
    dst: Optional[jax.Array]
    lo: int
    hi: int


def _gather(src, lo=0, hi=None, dst=None):
    return _Carry(True, src, dst, lo, src.shape[0] if hi is None else hi)


def _scatter(src, lo=0, hi=None, dst=None):
    return _Carry(False, src, dst, lo, src.shape[1] if hi is None else hi)


def _carry_phases(carries, srcs, dsts, send_sems, recv_sems, local_sems):
    x, y, c, me = _place()
    here, sibling = (x, y, c), (x, y, 1 - c)
    chips = [(1 - x, y), (x, 1 - y), (1 - x, 1 - y)]

    def rows(u):
        return pl.ds(carries[u].lo, carries[u].hi - carries[u].lo)

    def block_copy(u, sem, block, to, from_src=False):
        slot = dsts[u].at[4 * block[0] + 2 * block[1] + block[2], rows(u)]
        return pltpu.make_async_remote_copy(
            src_ref=srcs[u].at[rows(u)] if from_src else slot, dst_ref=slot, send_sem=send_sems.at[u, sem],
            recv_sem=recv_sems.at[u, sem], device_id=to, device_id_type=MESH)

    def partial_copy(u, k):
        peer = _flip(x, y, c, k)
        return pltpu.make_async_remote_copy(
            src_ref=srcs[u].at[4 * peer[0] + 2 * peer[1] + peer[2], rows(u)], dst_ref=dsts[u].at[me, rows(u)],
            send_sem=send_sems.at[u, k - 1], recv_sem=recv_sems.at[u, k - 1], device_id=peer, device_id_type=MESH)

    def local_copy(u):
        src = srcs[u].at[rows(u)] if carries[u].gather else srcs[u].at[me, rows(u)]
        return pltpu.make_async_copy(src, dsts[u].at[me, rows(u)], local_sems.at[u])

    def first():
        for u, cr in enumerate(carries):
            local_copy(u).start()
            if cr.gather:
                block_copy(u, 0, here, sibling, from_src=True).start()
                for j, chip in enumerate(chips):
                    block_copy(u, 1 + j, here, (*chip, c), from_src=True).start()
            else:
                for k in range(1, N_DEV):
                    partial_copy(u, k).start()

    def middle():
        for u, cr in enumerate(carries):
            if cr.gather:
                for j, chip in enumerate(chips):
                    block_copy(u, 1 + j, (*chip, c), here).wait_recv()
                    block_copy(u, 4 + j, (*chip, c), sibling).start()

    def last():
        for u, cr in enumerate(carries):
            if cr.gather:
                block_copy(u, 0, sibling, here).wait_recv()
                for j, chip in enumerate(chips):
                    block_copy(u, 4 + j, (*chip, 1 - c), here).wait_recv()
                block_copy(u, 0, here, sibling, from_src=True).wait_send()
                for j, chip in enumerate(chips):
                    block_copy(u, 1 + j, here, (*chip, c), from_src=True).wait_send()
                    block_copy(u, 4 + j, (*chip, c), sibling).wait_send()
            else:
                for k in range(1, N_DEV):
                    partial_copy(u, k).wait()
            local_copy(u).wait()

    return first, middle, last


def _call(body, *, name, grid, in_specs, out_specs, out_shape, operands, scratch=(), sem=None, vmem=None,
          carries=(), middle_at=0.6):
    if not carries:
        return pl.pallas_call(
            body, name=name, grid=grid, in_specs=in_specs, out_specs=out_specs, out_shape=out_shape,
            scratch_shapes=list(scratch), compiler_params=_params(sem, vmem))(*operands)
    n_in, n_out, n_scr, nc = len(in_specs), len(out_specs), len(scratch), len(carries)
    given = [u for u, cr in enumerate(carries) if cr.dst is not None]
    steps = math.prod(grid)
    middle_step = min(steps - 1, int(steps * middle_at))

    def wrapped(*refs):
        ins, srcs = refs[:n_in], refs[n_in:n_in + nc]
        at = n_in + nc + len(given)
        outs, dsts = refs[at:at + n_out], refs[at + n_out:at + n_out + nc]
        at += n_out + nc
        scr, (send_sems, recv_sems, local_sems) = refs[at:at + n_scr], refs[at + n_scr:]
        first, middle, last = _carry_phases(carries, srcs, dsts, send_sems, recv_sems, local_sems)
        step = 0
        for axis, size in enumerate(grid):
            step = step * size + pl.program_id(axis)
        pl.when(step == 0)(first)
        body(*ins, *outs, *scr)
        pl.when(step == middle_step)(middle)
        pl.when(step == steps - 1)(last)

    any_spec = pl.BlockSpec(memory_space=pl.ANY)
    dst_shapes = [jax.ShapeDtypeStruct((N_DEV, *cr.src.shape) if cr.gather else cr.src.shape, cr.src.dtype) for cr in carries]
    return pl.pallas_call(
        wrapped, name=name, grid=grid, in_specs=list(in_specs) + [any_spec] * (nc + len(given)),
        out_specs=list(out_specs) + [any_spec] * nc, out_shape=list(out_shape) + dst_shapes,
        input_output_aliases={n_in + nc + g: n_out + u for g, u in enumerate(given)},
        scratch_shapes=list(scratch) + [pltpu.SemaphoreType.DMA((nc, 7)), pltpu.SemaphoreType.DMA((nc, 7)),
                                        pltpu.SemaphoreType.DMA((nc,))],
        compiler_params=_params(("arbitrary",) * len(grid), vmem),
    )(*operands, *[cr.src for cr in carries], *[carries[u].dst for u in given])


def _mm(name, a, b, *, dims, grid, a_blk, a_map, b_blk, b_map, o_blk, o_map, out_shape, out_dtype,
        add=None, add_blk=None, add_map=None, carries=(), b_slabs=1):
    nk = grid[2]
    assert add is None or nk == 1
    acc_shape = tuple(d for d in o_blk if d is not None)
    in_place = out_dtype == F32

    def body(*refs):
        if add is None:
            a_ref, b_ref, o_ref = refs[:3]
            c_ref, scr = None, refs[3:]
        else:
            a_ref, b_ref, c_ref, o_ref = refs[:4]
            scr = refs[4:]
        if b_slabs == 1:
            part = lax.dot_general(a_ref[...], b_ref[...], dims, preferred_element_type=F32)
        else:
            cw = b_ref.shape[2]
            part = sum(lax.dot_general(a_ref[:, s * cw:(s + 1) * cw], b_ref[s], dims, preferred_element_type=F32)
                       for s in range(b_slabs))
        if c_ref is not None:
            part = part + c_ref[...].astype(F32)
        if nk == 1:
            o_ref[...] = part.astype(o_ref.dtype)
            return
        acc = o_ref if in_place else scr[0]
        k = pl.program_id(2)

        @pl.when(k == 0)
        def _():
            acc[...] = part

        @pl.when(k > 0 if in_place else jnp.logical_and(k > 0, k < nk - 1))
        def _():
            acc[...] += part

        if not in_place:
            @pl.when(k == nk - 1)
            def _():
                o_ref[...] = (acc[...] + part).astype(o_ref.dtype)

    in_specs = [pl.BlockSpec(a_blk, a_map), pl.BlockSpec(b_blk, b_map)]
    operands = [a, b]
    scratch = [pltpu.VMEM(acc_shape, F32)] if nk > 1 and not in_place else []
    vmem = 2 * (_nbytes(acc_shape, out_dtype) + _nbytes([d for d in a_blk if d], a.dtype)
                + _nbytes([d for d in b_blk if d], b.dtype)) + (2 + len(scratch)) * _nbytes(acc_shape, F32)
    if add is not None:
        in_specs.append(pl.BlockSpec(add_blk, add_map))
        operands.append(add)
        vmem += 2 * _nbytes(acc_shape, add.dtype)
    out, *dsts = _call(
        body, name=name, grid=grid, in_specs=in_specs, out_specs=[pl.BlockSpec(o_blk, o_map)],
        out_shape=[jax.ShapeDtypeStruct(out_shape, out_dtype)], operands=operands, scratch=scratch,
        sem=("parallel", "parallel", "arbitrary"), vmem=vmem + 8 * _MIB, carries=carries)
    return (out, dsts) if carries else out


def _mm_nn(name, x, w, out_dtype, carries=()):
    t, kd = x.shape
    n = w.shape[1]
    tm, tn, tk = _tile(t, 1024), _tile(n, 1024), _tile(kd, 1536 if kd > 2048 else 2048)
    return _mm(name, x, w, dims=_NN, grid=(t // tm, n // tn, kd // tk),
               a_blk=(tm, tk), a_map=lambda i, j, k: (i, k), b_blk=(tk, tn), b_map=lambda i, j, k: (k, j),
               o_blk=(tm, tn), o_map=lambda i, j, k: (i, j), out_shape=(t, n), out_dtype=out_dtype, carries=carries)


def _mm_nn_blk(name, x, g, out_dtype, halves=False, carries=()):
    t, kd = x.shape
    cw = g.shape[2]
    tm = _tile(t, 1024)
    if halves:
        o_blk, o_map, out_shape = (None, tm, cw), (lambda i, j, k: (j // 4, i, j % 4)), (2, t, 4 * cw)
    else:
        o_blk, o_map, out_shape = (tm, cw), (lambda i, j, k: (i, j)), (t, N_DEV * cw)
    return _mm(name, x, g, dims=_NN, grid=(t // tm, N_DEV, 1),
               a_blk=(tm, kd), a_map=lambda i, j, k: (i, 0), b_blk=(None, kd, cw), b_map=lambda i, j, k: (j, 0, 0),
               o_blk=o_blk, o_map=o_map, out_shape=out_shape, out_dtype=out_dtype, carries=carries)


def _mm_nt(name, dy, w, out_dtype, add=None, carries=()):
    t, n = dy.shape
    kd = w.shape[0]
    tm, tn = _tile(t, 1024), _tile(kd, 512)
    kw = {}
    if add is not None:
        kw = dict(add=add, add_blk=(tm, tn), add_map=lambda i, j, k: (i, j))
    return _mm(name, dy, w, dims=_NT, grid=(t // tm, kd // tn, 1),
               a_blk=(tm, n), a_map=lambda i, j, k: (i, 0), b_blk=(tn, n), b_map=lambda i, j, k: (j, 0),
               o_blk=(tm, tn), o_map=lambda i, j, k: (i, j), out_shape=(t, kd), out_dtype=out_dtype, carries=carries, **kw)


def _mm_nt_blk(name, dy, g, out_dtype, halves=False, carries=()):
    kd, cw = g.shape[1], g.shape[2]
    t = dy.shape[1] if halves else dy.shape[0]
    tm = _tile(t, 512)
    slabs = 2 if cw > 512 else 4
    nk = N_DEV // slabs
    if halves:
        a_blk, a_map = (None, tm, slabs * cw), (lambda i, j, k: (k // (nk // 2), i, k % (nk // 2)))
    else:
        a_blk, a_map = (tm, slabs * cw), (lambda i, j, k: (i, k))
    return _mm(name, dy, g, dims=_NT, grid=(t // tm, 1, nk), b_slabs=slabs,
               a_blk=a_blk, a_map=a_map, b_blk=(slabs, kd, cw), b_map=lambda i, j, k: (k, 0, 0),
               o_blk=(tm, kd), o_map=lambda i, j, k: (i, 0), out_shape=(t, kd), out_dtype=out_dtype, carries=carries)


def _mm_tn(name, x, dy, carries=()):
    t, kd = x.shape
    n = dy.shape[1]
    tmx, tk = _tile(kd, 1024), _tile(t, 2048)
    return _mm(name, x, dy, dims=_TN, grid=(kd // tmx, 1, t // tk),
               a_blk=(tk, tmx), a_map=lambda i, j, k: (k, i), b_blk=(tk, n), b_map=lambda i, j, k: (k, 0),
               o_blk=(tmx, n), o_map=lambda i, j, k: (i, 0), out_shape=(kd, n), out_dtype=BF, carries=carries)


def _mm_tn_blk(name, x, dy, halves=False, carries=()):
    t, kd = x.shape
    cw = dy.shape[2] // 4 if halves else dy.shape[1] // N_DEV
    tmx, tk = _tile(kd, 2048 if cw <= 512 else 1024), _tile(t, 2048)
    if halves:
        b_blk, b_map = (None, tk, cw), (lambda i, j, k: (j // 4, k, j % 4))
    else:
        b_blk, b_map = (tk, cw), (lambda i, j, k: (k, j))
    return _mm(name, x, dy, dims=_TN, grid=(kd // tmx, N_DEV, t // tk),
               a_blk=(tk, tmx), a_map=lambda i, j, k: (k, i), b_blk=b_blk, b_map=b_map,
               o_blk=(None, tmx, cw), o_map=lambda i, j, k: (j, i, 0), out_shape=(N_DEV, kd, cw), out_dtype=BF,
               carries=carries)


def _rms(x, g):
    r = lax.rsqrt(jnp.mean(x * x, axis=-1, keepdims=True) + NORM_EPS)
    return x * r * g


def _rms_bwd_math(x, g, dy):
    d = x.shape[-1]
    r = lax.rsqrt(jnp.mean(x * x, axis=-1, keepdims=True) + NORM_EPS)
    xh = x * r
    u = dy * g
    dx = r * u - xh * (jnp.sum(xh * u, axis=-1, keepdims=True) * (r / d))
    return dx, jnp.sum(dy * xh, axis=0, keepdims=True)


def _row_specs(tr, d, n):
    return [pl.BlockSpec((tr, d), lambda i: (i, 0)) for _ in range(n)]


def _vec_specs(d, n):
    return [pl.BlockSpec((1, d), lambda i: (0, 0)) for _ in range(n)]


def _norms_fwd(name, h, gains):
    t, d = h.shape
    tr, ng = _tile(t, 256), len(gains)

    def body(h_ref, *refs):
        x = h_ref[...]
        for g_ref, o_ref in zip(refs[:ng], refs[ng:]):
            o_ref[...] = _rms(x, g_ref[...]).astype(BF)

    return pl.pallas_call(
        body, name=name, grid=(t // tr,), in_specs=_row_specs(tr, d, 1) + _vec_specs(d, ng),
        out_specs=_row_specs(tr, d, ng), out_shape=[jax.ShapeDtypeStruct((t, d), BF)] * ng,
        compiler_params=_params(("parallel",)),
    )(h, *[g.reshape(1, d) for g in gains])


def _resid_norms(name, h, m, g_post, gains, carries=()):
    t, d = h.shape
    tr, ng = _tile(t, 256), len(gains)

    def body(h_ref, m_ref, gp_ref, *refs):
        hn = h_ref[...] + _rms(m_ref[...], gp_ref[...])
        refs[ng][...] = hn
        for g_ref, o_ref in zip(refs[:ng], refs[ng + 1:]):
            o_ref[...] = _rms(hn, g_ref[...]).astype(BF)

    return _call(
        body, name=name, grid=(t // tr,), in_specs=_row_specs(tr, d, 2) + _vec_specs(d, 1 + ng),
        out_specs=_row_specs(tr, d, 1 + ng),
        out_shape=[jax.ShapeDtypeStruct((t, d), F32)] + [jax.ShapeDtypeStruct((t, d), BF)] * ng,
        operands=(h, m, g_post.reshape(1, d), *[g.reshape(1, d) for g in gains]), sem=("parallel",), carries=carries)


def _resid_loss(name, h, m, g_post, target):
    t, d = h.shape
    tr = _tile(t, 256)

    def body(h_ref, m_ref, gp_ref, t_ref, dy_ref, loss_ref):
        diff = h_ref[...] + _rms(m_ref[...], gp_ref[...]) - t_ref[...]
        dy_ref[...] = diff * (1.0 / d)

        @pl.when(pl.program_id(0) == 0)
        def _():
            loss_ref[...] = jnp.zeros_like(loss_ref)

        per_row = jnp.sum(diff * diff, axis=-1, keepdims=True) * (1.0 / d)
        loss_ref[...] += 0.5 * jnp.sum(per_row, axis=0, keepdims=True)

    return pl.pallas_call(
        body, name=name, grid=(t // tr,),
        in_specs=_row_specs(tr, d, 2) + _vec_specs(d, 1) + _row_specs(tr, d, 1),
        out_specs=[pl.BlockSpec((tr, d), lambda i: (i, 0)), pl.BlockSpec((1, 1), lambda i: (0, 0))],
        out_shape=[jax.ShapeDtypeStruct((t, d), F32), jax.ShapeDtypeStruct((1, 1), F32)],
        compiler_params=_params(("arbitrary",)),
    )(h, m, g_post.reshape(1, d), target)


def _norm_bwd(name, x, g, dy, res=None, out_dtype=F32, carries=()):
    t, d = x.shape
    tr = _tile(t, 256)
    has_res = res is not None

    def body(x_ref, dy_ref, g_ref, *refs):
        dx_ref, dg_ref = refs[-2:]
        dx, dg = _rms_bwd_math(x_ref[...].astype(F32), g_ref[...], dy_ref[...].astype(F32))
        if has_res:
            dx = dx + refs[0][...]
        dx_ref[...] = dx.astype(dx_ref.dtype)

        @pl.when(pl.program_id(0) == 0)
        def _():
            dg_ref[...] = jnp.zeros_like(dg_ref)

        dg_ref[...] += dg

    ops = [x, dy, g.reshape(1, d)] + ([res] if has_res else [])
    return _call(
        body, name=name, grid=(t // tr,),
        in_specs=_row_specs(tr, d, 2) + _vec_specs(d, 1) + _row_specs(tr, d, int(has_res)),
        out_specs=[pl.BlockSpec((tr, d), lambda i: (i, 0)), pl.BlockSpec((1, d), lambda i: (0, 0))],
        out_shape=[jax.ShapeDtypeStruct((t, d), out_dtype), jax.ShapeDtypeStruct((1, d), F32)],
        operands=ops, sem=("arbitrary",), carries=carries)


_GELU_C = math.sqrt(2.0 / math.pi)
_GELU_A = 0.044715


def _gelu(x):
    return 0.5 * x * (1.0 + jnp.tanh(_GELU_C * (x + _GELU_A * x * x * x)))


def _gelu_and_grad(x):
    th = jnp.tanh(_GELU_C * (x + _GELU_A * x * x * x))
    grad = 0.5 * (1.0 + th) + 0.5 * x * (1.0 - th * th) * (_GELU_C * (1.0 + 3.0 * _GELU_A * x * x))
    return 0.5 * x * (1.0 + th), grad


def _causal(n):
    return lax.broadcasted_iota(jnp.int32, (n, n), 1) <= lax.broadcasted_iota(jnp.int32, (n, n), 0)


def _sgu_fwd(name, p, v_gain, w_s, b_st, carries=()):
    t, da2 = p.shape
    da = da2 // 2
    ng = da // GROUP

    def body(p_ref, vg_ref, ws_ref, bst_ref, o_ref):
        keep = _causal(GROUP)
        for g in range(ng):
            lo = g * GROUP
            u = _gelu(p_ref[:, lo:lo + GROUP])
            vn = _rms(_gelu(p_ref[:, da + lo:da + lo + GROUP]), vg_ref[:, lo:lo + GROUP])
            w = jnp.where(keep, ws_ref[g], 0.0).astype(BF)
            mixed = jnp.dot(w, vn.astype(BF), preferred_element_type=F32) + bst_ref[:, g:g + 1]
            o_ref[:, lo:lo + GROUP] = (u * mixed).astype(BF)

    return _call(
        body, name=name, grid=(t // GROUP,),
        in_specs=[pl.BlockSpec((GROUP, da2), lambda i: (i, 0)), pl.BlockSpec((1, da), lambda i: (0, 0)),
                  pl.BlockSpec((ng, GROUP, GROUP), lambda i: (0, 0, 0)), pl.BlockSpec((GROUP, ng), lambda i: (0, 0))],
        out_specs=[pl.BlockSpec((GROUP, da), lambda i: (i, 0))], out_shape=[jax.ShapeDtypeStruct((t, da), BF)],
        operands=(p, v_gain, w_s, b_st), sem=("parallel",), carries=carries)


def _sgu_bwd(name, p, v_gain, w_s, w_st, b_st, dout, carries=()):
    t, da2 = p.shape
    da = da2 // 2
    ng = da // GROUP

    def body(p_ref, vg_ref, ws_ref, wst_ref, bst_ref, do_ref, dp_ref, dvg_ref, dws_ref, dbst_ref):
        @pl.when(pl.program_id(0) == 0)
        def _():
            dvg_ref[...] = jnp.zeros_like(dvg_ref)
            dws_ref[...] = jnp.zeros_like(dws_ref)
            dbst_ref[...] = jnp.zeros_like(dbst_ref)

        keep = _causal(GROUP)
        keep_t = lax.broadcasted_iota(jnp.int32, (GROUP, GROUP), 0) <= lax.broadcasted_iota(jnp.int32, (GROUP, GROUP), 1)
        for g in range(ng):
            lo = g * GROUP
            u, du = _gelu_and_grad(p_ref[:, lo:lo + GROUP])
            v, dv_act = _gelu_and_grad(p_ref[:, da + lo:da + lo + GROUP])
            gain = vg_ref[:, lo:lo + GROUP]
            r = lax.rsqrt(jnp.mean(v * v, axis=-1, keepdims=True) + NORM_EPS)
            vh = v * r
            vnb = (vh * gain).astype(BF)
            w = jnp.where(keep, ws_ref[g], 0.0).astype(BF)
            wt = jnp.where(keep_t, wst_ref[g], 0.0).astype(BF)
            mixed = jnp.dot(w, vnb, preferred_element_type=F32) + bst_ref[:, g:g + 1]
            dout_g = do_ref[:, lo:lo + GROUP].astype(F32)
            dmixed = dout_g * u
            dmb = dmixed.astype(BF)
            dbst_ref[:, g:g + 1] += jnp.sum(dmixed, axis=1, keepdims=True)
            dws_ref[g] += jnp.where(keep, lax.dot_general(dmb, vnb, _NT, preferred_element_type=F32), 0.0)
            dvn = jnp.dot(wt, dmb, preferred_element_type=F32)
            dvg_ref[:, lo:lo + GROUP] += jnp.sum(dvn * vh, axis=0, keepdims=True)
            dvh = dvn * gain
            dv = r * dvh - vh * (jnp.sum(vh * dvh, axis=-1, keepdims=True) * (r / GROUP))
            dp_ref[:, lo:lo + GROUP] = (dout_g * mixed * du).astype(BF)
            dp_ref[:, da + lo:da + lo + GROUP] = (dv * dv_act).astype(BF)

    full = lambda *shape: pl.BlockSpec(shape, lambda i: (0,) * len(shape))
    outs = _call(
        body, name=name, grid=(t // GROUP,),
        in_specs=[pl.BlockSpec((GROUP, da2), lambda i: (i, 0)), full(1, da), full(ng, GROUP, GROUP), full(ng, GROUP, GROUP),
                  full(GROUP, ng), pl.BlockSpec((GROUP, da), lambda i: (i, 0))],
        out_specs=[pl.BlockSpec((GROUP, da2), lambda i: (i, 0)), full(1, da), full(ng, GROUP, GROUP), full(GROUP, ng)],
        out_shape=[jax.ShapeDtypeStruct((t, da2), BF), jax.ShapeDtypeStruct((1, da), F32),
                   jax.ShapeDtypeStruct((ng, GROUP, GROUP), F32), jax.ShapeDtypeStruct((GROUP, ng), F32)],
        operands=(p, v_gain, w_s, w_st, b_st, dout), sem=("arbitrary",), carries=carries)
    return outs[:4], outs[4:]


_CONV_ROWS = 256
_CONV_COLS = 128


def _shift_down(x, prev, k):
    top = pltpu.roll(jnp.concatenate([prev, x[:8]], axis=0), k, 0)[8:16]
    return jnp.concatenate([top, pltpu.roll(x, k, 0)[8:]], axis=0)


def _shift_up(x, nxt, k):
    n = x.shape[0]
    bottom = pltpu.roll(jnp.concatenate([x[n - 8:], nxt], axis=0), 16 - k, 0)[0:8]
    return jnp.concatenate([pltpu.roll(x, n - k, 0)[:n - 8], bottom], axis=0)


def _conv_taps(a_ref, half, r0, first):
    x = a_ref[half, pl.ds(r0, _rows(a_ref)), :]
    prev = a_ref[half, pl.ds(jnp.maximum(r0 - 8, 0), 8), :]
    prev = jnp.where(first, 0.0, prev)
    return x, _shift_down(x, prev, 1), _shift_down(x, prev, 2)


def _rows(a_ref):
    return min(_CONV_ROWS, a_ref.shape[1])


def _conv_fwd(name, a3, cw, cb, carries=()):
    _, t, f = a3.shape
    tc, rows = _CONV_COLS, min(_CONV_ROWS, t)

    def body(a_ref, cw_ref, cb_ref, y_ref):
        def step(r, carry):
            r0 = pl.multiple_of(r * rows, rows)
            c = []
            for half in range(2):
                x, x1, x2 = _conv_taps(a_ref, half, r0, r == 0)
                w = cw_ref[half]
                c.append(cb_ref[half] + (w[0:1] * x2 + w[1:2] * x1 + w[2:3] * x))
            y_ref[pl.ds(r0, rows), :] = (c[0] * jax.nn.sigmoid(c[0]) * c[1]).astype(BF)
            return carry

        lax.fori_loop(0, t // rows, step, 0)

    col = lambda *lead: pl.BlockSpec((*lead, tc), lambda j: (0,) * len(lead) + (j,))
    y, *dsts = _call(
        body, name=name, grid=(f // tc,), in_specs=[col(2, t), col(2, CONV_TAPS), col(2, 1)],
        out_specs=[col(t)], out_shape=[jax.ShapeDtypeStruct((t, f), BF)], operands=(a3, cw, cb),
        sem=("parallel",), vmem=40 * _MIB, carries=carries)
    return y, dsts


def _conv_bwd(name, a3, cw, cb, dy, carries=()):
    _, t, f = a3.shape
    tc, rows = _CONV_COLS, min(_CONV_ROWS, t)
    n_steps = t // rows

    def body(a_ref, cw_ref, cb_ref, dy_ref, da_ref, dcw_ref, dcb_ref):
        dcw_ref[...] = jnp.zeros_like(dcw_ref)
        dcb_ref[...] = jnp.zeros_like(dcb_ref)

        def step(s, nxt):
            r = n_steps - 1 - s
            r0 = pl.multiple_of(r * rows, rows)
            taps, c = [], []
            for half in range(2):
                x, x1, x2 = _conv_taps(a_ref, half, r0, r == 0)
                w = cw_ref[half]
                taps.append((x2, x1, x))
                c.append(cb_ref[half] + (w[0:1] * x2 + w[1:2] * x1 + w[2:3] * x))
            gate, val = c
            sg = jax.nn.sigmoid(gate)
            dyv = dy_ref[pl.ds(r0, rows), :].astype(F32)
            dcs = (dyv * val * (sg * (1.0 + gate * (1.0 - sg))), dyv * (gate * sg))
            new_nxt = []
            for half in range(2):
                dc, w = dcs[half], cw_ref[half]
                dcb_ref[half] += jnp.sum(dc, axis=0, keepdims=True)
                for tap in range(CONV_TAPS):
                    dcw_ref[half, tap:tap + 1, :] += jnp.sum(dc * taps[half][tap], axis=0, keepdims=True)
                da = w[2:3] * dc + w[1:2] * _shift_up(dc, nxt[half], 1) + w[0:1] * _shift_up(dc, nxt[half], 2)
                da_ref[half, pl.ds(r0, rows), :] = da.astype(BF)
                new_nxt.append(dc[:8])
            return tuple(new_nxt)

        zeros = jnp.zeros((8, tc), F32)
        lax.fori_loop(0, n_steps, step, (zeros, zeros))

    col = lambda *lead: pl.BlockSpec((*lead, tc), lambda j: (0,) * len(lead) + (j,))
    outs = _call(
        body, name=name, grid=(f // tc,), in_specs=[col(2, t), col(2, CONV_TAPS), col(2, 1), col(t)],
        out_specs=[col(2, t), col(2, CONV_TAPS), col(2, 1)],
        out_shape=[jax.ShapeDtypeStruct((2, t, f), BF), jax.ShapeDtypeStruct((2, CONV_TAPS, f), F32),
                   jax.ShapeDtypeStruct((2, 1, f), F32)],
        operands=(a3, cw, cb, dy), sem=("parallel",), vmem=40 * _MIB, carries=carries)
    return outs[:3], outs[3:]


_ATT_BLOCK = 256


def _split_dot(x, tri):
    hi = x.astype(BF)
    lo = (x - hi.astype(F32)).astype(BF)
    return jnp.dot(hi, tri, preferred_element_type=F32) + jnp.dot(lo, tri, preferred_element_type=F32)


_ATT_HEADS_FWD = 4
_ATT_HEADS_BWD = 4


def _logits(qb, kb, diagonal):
    z = lax.dot_general(qb, kb, _NT, preferred_element_type=F32) * (1.0 / math.sqrt(GROUP))
    lb = jnp.minimum(z, 0.0) - jnp.log(1.0 + jnp.exp(-jnp.abs(z)))
    if not diagonal:
        return lb, lb - z, None
    mask = lax.broadcasted_iota(jnp.int32, z.shape, 1) < lax.broadcasted_iota(jnp.int32, z.shape, 0)
    return lb, jnp.where(mask, lb - z, 0.0), mask


def _head(ref, g, rows=slice(None)):
    return ref[rows, g * GROUP:(g + 1) * GROUP]


def _attn_fwd(name, q, k, v, carries=()):
    t, hd = q.shape
    blk = min(_ATT_BLOCK, t)
    heads = min(_ATT_HEADS_FWD, hd // GROUP)
    gs = range(heads)

    def body(q_ref, k_ref, v_ref, o_ref, rest_ref, first_ref):
        hg, i = pl.program_id(0), pl.program_id(1)
        ri = lax.broadcasted_iota(jnp.int32, (blk, blk), 0)
        ci = lax.broadcasted_iota(jnp.int32, (blk, blk), 1)
        tri = (ri >= ci).astype(BF)

        def tile(j, state, diagonal):
            keys = pl.ds(pl.multiple_of(j * blk, blk), blk)
            logit = [_logits(_head(q_ref, g), _head(k_ref, g, keys), diagonal) for g in gs]
            incl = [_split_dot(logit[g][1], tri) for g in gs]
            a = [jnp.exp(logit[g][0] + (incl[g] - logit[g][1] + state[g][0])) for g in gs]
            if diagonal:
                a = [jnp.where(logit[g][2], a[g], 0.0) for g in gs]
            out = [state[g][1] + jnp.dot(a[g].astype(BF), _head(v_ref, g, keys), preferred_element_type=F32) for g in gs]
            return tuple((state[g][0] + incl[g][:, 0:1], out[g]) for g in gs)

        state = tile(i, ((jnp.zeros((blk, 1), F32), jnp.zeros((blk, GROUP), F32)),) * heads, True)

        def more(carry):
            j, state = carry
            live = functools.reduce(jnp.maximum, [jnp.max(right) for right, _ in state])
            return jnp.logical_and(j >= 0, live > EXP_FLOOR)

        j, state = lax.while_loop(more, lambda c: (c[0] - 1, tile(c[0], c[1], False)), (i - 1, state))
        for g, (right, acc) in enumerate(state):
            o_ref[:, g * GROUP:(g + 1) * GROUP] = acc.astype(BF)
            rest_ref[:, g * GROUP:(g + 1) * GROUP] = jnp.broadcast_to(right, (blk, GROUP))
        first_ref[hg, i] = (j + 1).astype(F32)

    qspec = pl.BlockSpec((blk, heads * GROUP), lambda h, i: (i, h))
    kvspec = pl.BlockSpec((t, heads * GROUP), lambda h, i: (0, h))
    groups = hd // (heads * GROUP)
    outs = _call(
        body, name=name, grid=(groups, t // blk), in_specs=[qspec, kvspec, kvspec],
        out_specs=[qspec, qspec, pl.BlockSpec(memory_space=pltpu.SMEM)],
        out_shape=[jax.ShapeDtypeStruct((t, hd), BF), jax.ShapeDtypeStruct((t, hd), F32),
                   jax.ShapeDtypeStruct((groups, t // blk), F32)],
        operands=(q, k, v), sem=("arbitrary", "arbitrary"), vmem=40 * _MIB, carries=carries, middle_at=0.75)
    return outs[:3], outs[3:]


def _attn_bwd(name, q, k, v, rest, first, do, carries=()):
    t, hd = q.shape
    blk = min(_ATT_BLOCK, t)
    nq = t // blk
    scale = 1.0 / math.sqrt(GROUP)
    heads = min(_ATT_HEADS_BWD, hd // GROUP)
    per_first = min(_ATT_HEADS_FWD, hd // GROUP) // heads
    gs = range(heads)

    def body(first_ref, q_ref, k_ref, v_ref, rest_ref, do_ref, dq_ref, dk_ref, dv_ref, dk_acc, dv_acc):
        hg, i = pl.program_id(0), pl.program_id(1)

        @pl.when(i == 0)
        def _():
            dk_acc[...] = jnp.zeros_like(dk_acc)
            dv_acc[...] = jnp.zeros_like(dv_acc)

        ri = lax.broadcasted_iota(jnp.int32, (blk, blk), 0)
        ci = lax.broadcasted_iota(jnp.int32, (blk, blk), 1)
        tri = (ri <= ci).astype(BF)

        def tile(j, state, diagonal):
            keys = pl.ds(pl.multiple_of(j * blk, blk), blk)
            qs, dos = [_head(q_ref, g) for g in gs], [_head(do_ref, g) for g in gs]
            kb, vb = [_head(k_ref, g, keys) for g in gs], [_head(v_ref, g, keys) for g in gs]
            logit = [_logits(qs[g], kb[g], diagonal) for g in gs]
            pre = [_split_dot(logit[g][1], tri) for g in gs]
            a = [jnp.exp(logit[g][0] + (rest_ref[:, g * GROUP:g * GROUP + 1] - state[g][0] - pre[g])) for g in gs]
            if diagonal:
                a = [jnp.where(logit[g][2], a[g], 0.0) for g in gs]
            gw = [a[g] * lax.dot_general(dos[g], vb[g], _NT, preferred_element_type=F32) for g in gs]
            gpre = [_split_dot(gw[g], tri) for g in gs]
            dz = []
            for g in gs:
                beta = jnp.exp(logit[g][0])
                d = (gw[g] * (1.0 - beta) - (state[g][1] + gpre[g] - gw[g]) * beta) * scale
                dz.append((jnp.where(logit[g][2], d, 0.0) if diagonal else d).astype(BF))
            for g in gs:
                dk_acc[keys, g * GROUP:(g + 1) * GROUP] += lax.dot_general(dz[g], qs[g], _TN, preferred_element_type=F32)
                dv_acc[keys, g * GROUP:(g + 1) * GROUP] += lax.dot_general(a[g].astype(BF), dos[g], _TN,
                                                                           preferred_element_type=F32)
            return tuple((state[g][0] + pre[g][:, blk - 1:blk], state[g][1] + gpre[g][:, blk - 1:blk],
                          state[g][2] + jnp.dot(dz[g], kb[g], preferred_element_type=F32)) for g in gs)

        zero = jnp.zeros((blk, 1), F32)
        first_block = jnp.clip(first_ref[hg // per_first, i].astype(jnp.int32), 0, i)
        state = lax.fori_loop(first_block, i, lambda j, c: tile(j, c, False),
                              ((zero, zero, jnp.zeros((blk, GROUP), F32)),) * heads)
        for g, (_, _, dq) in enumerate(tile(i, state, True)):
            dq_ref[:, g * GROUP:(g + 1) * GROUP] = dq.astype(BF)

        @pl.when(i == nq - 1)
        def _():
            dk_ref[...] = dk_acc[...].astype(BF)
            dv_ref[...] = dv_acc[...].astype(BF)

    qspec = pl.BlockSpec((blk, heads * GROUP), lambda h, i: (i, h))
    kvspec = pl.BlockSpec((t, heads * GROUP), lambda h, i: (0, h), pipeline_mode=pl.Buffered(1))
    outs = _call(
        body, name=name, grid=(hd // (heads * GROUP), nq),
        in_specs=[pl.BlockSpec(memory_space=pltpu.SMEM), qspec, kvspec, kvspec, qspec, qspec],
        out_specs=[qspec, kvspec, kvspec], out_shape=[jax.ShapeDtypeStruct((t, hd), BF)] * 3,
        operands=(first, q, k, v, rest, do),
        scratch=[pltpu.VMEM((t, heads * GROUP), F32), pltpu.VMEM((t, heads * GROUP), F32)],
        sem=("arbitrary", "arbitrary"), vmem=48 * _MIB, carries=carries)
    return outs[:3], outs[3:]


def _adamw_math(w, g, m, v):
    m = ADAM_B1 * m + (1.0 - ADAM_B1) * g
    v = ADAM_B2 * v + (1.0 - ADAM_B2) * (g * g)
    m_hat = m / (1.0 - ADAM_B1 ** ADAM_STEP)
    v_hat = v / (1.0 - ADAM_B2 ** ADAM_STEP)
    return -ADAM_LR * (m_hat / (jnp.sqrt(v_hat) + ADAM_EPS) + ADAM_WD * w), m, v


def _sum_adamw(name, parts, w, m, v):
    layers, r, c = w.shape
    budget = 512 * 1024 // layers
    tr = r if r * c <= budget else _tile_rows(r, max(8, (budget // c) // 8 * 8))
    n = r // tr

    def body(*refs):
        p_refs, (w_ref, m_ref, v_ref, g_ref, d_ref, nm_ref, nv_ref) = refs[:layers], refs[layers:]
        for layer, p_ref in enumerate(p_refs):
            @pl.when(pl.program_id(0) == layer)
            def _(p_ref=p_ref):
                g = p_ref[0].astype(F32)
                for dev in range(1, N_DEV):
                    g = g + p_ref[dev].astype(F32)
                g_ref[...] = g
                d_ref[...], nm_ref[...], nv_ref[...] = _adamw_math(w_ref[...], g, m_ref[...], v_ref[...])

    def part_spec(layer):
        return pl.BlockSpec((N_DEV, tr, c), lambda l, i: (0, jnp.where(l < layer, 0, jnp.where(l == layer, i, n - 1)), 0))

    row = pl.BlockSpec((None, tr, c), lambda l, i: (l, i, 0))
    return pl.pallas_call(
        body, name=name, grid=(layers, n), in_specs=[part_spec(layer) for layer in range(layers)] + [row] * 3,
        out_specs=[row] * 4, out_shape=[jax.ShapeDtypeStruct((layers, r, c), F32)] * 4,
        compiler_params=_params(("arbitrary", "arbitrary"), 40 * _MIB),
    )(*parts, w, m, v)


def _tile_rows(r, pref):
    t = min(r, pref)
    while r % t or t % 8:
        t -= 1
    return t


def _exchange(name, carries):
    return _call(lambda: None, name=name, grid=(1,), in_specs=[], out_specs=[], out_shape=[], operands=(), carries=carries)


def _all_reduce_small(name, groups):
    c = groups[0][0].shape[1]
    parts, offsets, starts, r = [], [], [], 0
    for group in groups:
        starts.append(r)
        for p in group:
            parts.append(p)
            offsets.append(r)
            r += p.shape[0]
        r = -(-r // 8) * 8
    n = len(parts)

    def body(*refs):
        out_ref, slots, send_sems, recv_sems = refs[n:]
        x, y, c_, me = _place()
        slots[me] = jnp.zeros((r, c), F32)
        for p_ref, off in zip(refs[:n], offsets):
            slots[me, off:off + p_ref.shape[0], :] = p_ref[...]

        here, sibling = (x, y, c_), (x, y, 1 - c_)
        chips = [(1 - x, y), (x, 1 - y), (1 - x, 1 - y)]

        def copy(sem, block, to):
            slot = slots.at[4 * block[0] + 2 * block[1] + block[2]]
            return pltpu.make_async_remote_copy(
                src_ref=slot, dst_ref=slot, send_sem=send_sems.at[sem], recv_sem=recv_sems.at[sem], device_id=to,
                device_id_type=MESH)

        sent = [copy(0, here, sibling)] + [copy(1 + j, here, (*chip, c_)) for j, chip in enumerate(chips)]
        for cp in sent:
            cp.start()
        for j, chip in enumerate(chips):
            copy(1 + j, (*chip, c_), here).wait_recv()
            sent.append(copy(4 + j, (*chip, c_), sibling))
            sent[-1].start()
        copy(0, sibling, here).wait_recv()
        for j, chip in enumerate(chips):
            copy(4 + j, (*chip, 1 - c_), here).wait_recv()
        for cp in sent:
            cp.wait_send()
        total = slots[0]
        for dev in range(1, N_DEV):
            total = total + slots[dev]
        out_ref[...] = total

    vm = pl.BlockSpec(memory_space=pltpu.VMEM)
    summed = pl.pallas_call(
        body, name=name, in_specs=[vm] * n, out_specs=vm, out_shape=jax.ShapeDtypeStruct((r, c), F32),
        scratch_shapes=[pltpu.VMEM((N_DEV, r, c), F32), pltpu.SemaphoreType.DMA((7,)), pltpu.SemaphoreType.DMA((7,))],
        compiler_params=_params(None, (N_DEV + 6) * r * c * 4 + 8 * _MIB),
    )(*parts)
    return summed, starts


def _adamw(name, g, w, m, v):
    cols = w.shape[-1]
    flat = lambda a: a.reshape(-1, cols)

    def body(g_ref, w_ref, m_ref, v_ref, d_ref, nm_ref, nv_ref):
        d_ref[...], nm_ref[...], nv_ref[...] = _adamw_math(w_ref[...], g_ref[...], m_ref[...], v_ref[...])

    outs = pl.pallas_call(body, name=name, out_shape=[jax.ShapeDtypeStruct(flat(w).shape, F32)] * 3)(
        flat(g), flat(w), flat(m), flat(v))
    return [o.reshape(w.shape) for o in outs]


def kernel(x, pre_mix_g, post_mix_g, pre_ffn_g, post_ffn_g, a_w_in, a_v_norm_g, a_w_spatial, a_b_spatial, a_w_out, kv_norm_g, w_k, w_v, b_w_q, b_w_o, ffn_w_up, ffn_conv_w, ffn_conv_b, ffn_w_down, loss_target, m_pre_mix_g, m_post_mix_g, m_pre_ffn_g, m_post_ffn_g, m_a_w_in, m_a_v_norm_g, m_a_w_spatial, m_a_b_spatial, m_a_w_out, m_kv_norm_g, m_w_k, m_w_v, m_b_w_q, m_b_w_o, m_ffn_w_up, m_ffn_conv_w, m_ffn_conv_b, m_ffn_w_down, v_pre_mix_g, v_post_mix_g, v_pre_ffn_g, v_post_ffn_g, v_a_w_in, v_a_v_norm_g, v_a_w_spatial, v_a_b_spatial, v_a_w_out, v_kv_norm_g, v_w_k, v_w_v, v_b_w_q, v_b_w_o, v_ffn_w_up, v_ffn_conv_w, v_ffn_conv_b, v_ffn_w_down):
    t, d = x.shape[1], x.shape[2]
    f = ffn_w_down.shape[1] * N_DEV
    ng = d // GROUP
    me = 4 * lax.axis_index("x") + 2 * lax.axis_index("y") + lax.axis_index("c")
    x2, target = x.reshape(t, d), loss_target.reshape(t, d)

    g_in, g_cw, g_vg = _exchange("gather_first", [
        _gather(a_w_in[0].astype(BF)), _gather(ffn_conv_w.reshape(2 * CONV_TAPS, -1)), _gather(a_v_norm_g)])
    up0 = ffn_w_up[0].astype(BF)
    cw_full = jnp.transpose(g_cw.reshape(N_DEV, 2, CONV_TAPS, -1), (1, 2, 0, 3)).reshape(2, CONV_TAPS, 2, f)
    cw_l = [jnp.transpose(cw_full[l], (1, 0, 2)) for l in range(2)]
    cb_l = [ffn_conv_b[l].reshape(2, 1, f) for l in range(2)]
    vg_full = g_vg.reshape(1, d)
    w_s = a_w_spatial[0]
    w_st = jnp.swapaxes(w_s, 1, 2)
    b_st = a_b_spatial[0].T

    (hn0,) = _norms_fwd("pre_mix0", x2, [pre_mix_g[0]])
    p0, (g_out, g_up0) = _mm_nn_blk("sgu_in", hn0, g_in, F32, carries=[
        _gather(a_w_out[0].astype(BF)), _gather(up0, 0, d // 4)])
    w_out_f = g_out.reshape(d, d)
    sg, g_up0 = _sgu_fwd("sgu", p0, vg_full, w_s, b_st, carries=[_gather(up0, d // 4, d // 2, dst=g_up0)])
    mix0, (g_up0,) = _mm_nn("sgu_out", sg, w_out_f, F32, carries=[_gather(up0, d // 2, 3 * d // 4, dst=g_up0)])
    h1, fn0, g_up0 = _resid_norms("post_mix0", x2, mix0, post_mix_g[0], [pre_ffn_g[0]],
                                  carries=[_gather(up0, 3 * d // 4, d, dst=g_up0)])
    a3_0, (g_dn0,) = _mm_nn_blk("ffn0_up", fn0, g_up0, F32, halves=True, carries=[_gather(ffn_w_down[0].astype(BF))])
    y0, (g_q,) = _conv_fwd("ffn0_conv", a3_0, cw_l[0], cb_l[0], carries=[_gather(b_w_q[0].astype(BF))])
    f0, (g_k, g_v) = _mm_nn("ffn0_down", y0, g_dn0.reshape(f, d), F32,
                            carries=[_gather(w_k.astype(BF)), _gather(w_v.astype(BF))])
    w_q_f, w_k_f, w_v_f = g_q.reshape(d, d), g_k.reshape(d, d), g_v.reshape(d, d)
    h2, hn1, kvn = _resid_norms("post_ffn0", h1, f0, post_ffn_g[0], [pre_mix_g[1], kv_norm_g])
    up1 = ffn_w_up[1].astype(BF)
    e = d // 8
    q, (g_up1,) = _mm_nn("attn_q", hn1, w_q_f, BF, carries=[_gather(up1, 0, e)])
    kk, (g_up1,) = _mm_nn("attn_k", kvn, w_k_f, BF, carries=[_gather(up1, e, 2 * e, dst=g_up1)])
    vv, (g_up1,) = _mm_nn("attn_v", kvn, w_v_f, BF, carries=[_gather(up1, 2 * e, 3 * e, dst=g_up1)])
    (att, rest, first), (g_o, g_up1) = _attn_fwd(
        "attn", q, kk, vv, carries=[_gather(b_w_o[0].astype(BF)), _gather(up1, 3 * e, 6 * e, dst=g_up1)])
    w_o_f = g_o.reshape(d, d)
    mix1, (g_up1,) = _mm_nn("attn_o", att, w_o_f, F32, carries=[_gather(up1, 6 * e, 7 * e, dst=g_up1)])
    h3, fn1, g_up1 = _resid_norms("post_mix1", h2, mix1, post_mix_g[1], [pre_ffn_g[1]],
                                  carries=[_gather(up1, 7 * e, d, dst=g_up1)])
    a3_1, (g_dn1,) = _mm_nn_blk("ffn1_up", fn1, g_up1, F32, halves=True, carries=[_gather(ffn_w_down[1].astype(BF))])
    y1, _ = _conv_fwd("ffn1_conv", a3_1, cw_l[1], cb_l[1])
    f1 = _mm_nn("ffn1_down", y1, g_dn1.reshape(f, d), F32)
    g_up = (g_up0, g_up1)
    w_dn_f = (g_dn0.reshape(f, d), g_dn1.reshape(f, d))
    dh, loss_part = _resid_loss("loss", h3, f1, post_ffn_g[1], target)
    loss = lax.psum(loss_part[0, 0], ("x", "y", "c"))

    def blocks(dw):
        return dw.reshape(N_DEV, -1, d)

    def split(result, carries):
        return result if carries else (result, [])

    def ffn_bwd(l, dh_out, h_in, fn, a3, yv, fo, with_dw=(), with_dx=(), with_up=()):
        dfo, d_post = _norm_bwd(f"post_ffn{l}_bwd", fo, post_ffn_g[l], dh_out, out_dtype=BF)
        dw_dn, sent_dw = split(_mm_tn(f"ffn{l}_down_dw", yv, dfo, carries=with_dw), with_dw)
        dy, sent_dx = split(_mm_nt(f"ffn{l}_down_dx", dfo, w_dn_f[l], BF, carries=with_dx), with_dx)
        (da3, dcw, dcb), (p_dn,) = _conv_bwd(f"ffn{l}_conv_bwd", a3, cw_l[l], cb_l[l], dy, carries=[_scatter(blocks(dw_dn))])
        dw_up, sent_up = split(_mm_tn_blk(f"ffn{l}_up_dw", fn, da3, halves=True, carries=with_up), with_up)
        dfn, (p_up,) = _mm_nt_blk(f"ffn{l}_up_dx", da3, g_up[l], F32, halves=True, carries=[_scatter(dw_up, 0, d // 2)])
        dh_in, d_pre, p_up = _norm_bwd(f"pre_ffn{l}_bwd", h_in, pre_ffn_g[l], dfn, res=dh_out,
                                       carries=[_scatter(dw_up, d // 2, 9 * d // 16, dst=p_up)])
        return dh_in, d_post, d_pre, dcw, dcb, p_dn, dw_up, p_up, list(sent_dw) + list(sent_dx) + list(sent_up)

    dh3, d_post_ffn1, d_pre_ffn1, dcw1, dcb1, p_dn1, dw_up1, p_up1, _ = ffn_bwd(1, dh, h3, fn1, a3_1, y1, f1)
    dmix1, d_post_mix1 = _norm_bwd("post_mix1_bwd", mix1, post_mix_g[1], dh3, out_dtype=BF)
    dw_o = _mm_tn("attn_o_dw", att, dmix1)
    datt = _mm_nt("attn_o_dx", dmix1, w_o_f, BF)
    (dq, dk, dv), (p_up1, p_o) = _attn_bwd(
        "attn_bwd", q, kk, vv, rest, first, datt, carries=[_scatter(dw_up1, 9 * d // 16, d, dst=p_up1), _scatter(blocks(dw_o))])
    dw_q = _mm_tn("attn_q_dw", hn1, dq)
    dw_k = _mm_tn("attn_k_dw", kvn, dk)
    dw_v = _mm_tn("attn_v_dw", kvn, dv)
    dhn1 = _mm_nt("attn_q_dx", dq, w_q_f, F32)
    dkvn = _mm_nt("attn_v_dx", dv, w_v_f, F32, add=_mm_nt("attn_k_dx", dk, w_k_f, F32))
    dh2a, d_pre_mix1 = _norm_bwd("pre_mix1_bwd", h2, pre_mix_g[1], dhn1, res=dh3)
    dh2, d_kv = _norm_bwd("kv_norm_bwd", h2, kv_norm_g, dkvn, res=dh2a)
    dh1, d_post_ffn0, d_pre_ffn0, dcw0, dcb0, p_dn0, dw_up0, p_up0, (p_q, p_k, p_v) = ffn_bwd(
        0, dh2, h1, fn0, a3_0, y0, f0, with_dw=[_scatter(blocks(dw_q))], with_dx=[_scatter(blocks(dw_k))],
        with_up=[_scatter(blocks(dw_v))])
    dmix0, d_post_mix0 = _norm_bwd("post_mix0_bwd", mix0, post_mix_g[0], dh1, out_dtype=BF)
    dw_out, (p_up0,) = _mm_tn("sgu_out_dw", sg, dmix0, carries=[_scatter(dw_up0, 9 * d // 16, 11 * d // 16, dst=p_up0)])
    dsg = _mm_nt("sgu_out_dx", dmix0, w_out_f, BF)
    (dp0, d_vg, d_ws, d_bst), (p_up0,) = _sgu_bwd(
        "sgu_bwd", p0, vg_full, w_s, w_st, b_st, dsg, carries=[_scatter(dw_up0, 11 * d // 16, 15 * d // 16, dst=p_up0)])
    dw_in, (p_out,) = _mm_tn_blk("sgu_in_dw", hn0, dp0, carries=[_scatter(blocks(dw_out))])
    dhn0, (p_in,) = _mm_nt_blk("sgu_in_dx", dp0, g_in, F32, carries=[_scatter(dw_in, 0, 5 * d // 8)])
    grad_x, d_pre_mix0, p_in = _norm_bwd("pre_mix0_bwd", x2, pre_mix_g[0], dhn0, res=dh1,
                                         carries=[_scatter(dw_in, 5 * d // 8, 7 * d // 8, dst=p_in)])
    p_up0, p_in = _exchange("scatter_last", [_scatter(dw_up0, 15 * d // 16, d, dst=p_up0),
                                             _scatter(dw_in, 7 * d // 8, d, dst=p_in)])

    def conv_w_grad(dcw):
        return jnp.transpose(dcw, (1, 0, 2)).reshape(CONV_TAPS, 2 * f)

    small = [
        ([d_pre_mix0, d_pre_mix1], (2, d)), ([d_post_mix0, d_post_mix1], (2, d)),
        ([d_pre_ffn0, d_pre_ffn1], (2, d)), ([d_post_ffn0, d_post_ffn1], (2, d)),
        ([d_kv], (d,)), ([d_vg], (1, d)), ([d_bst.T], (1, ng, GROUP)), ([d_ws], (1, ng, GROUP, GROUP)),
        ([dcb0, dcb1], (2, 2 * f)), ([conv_w_grad(dcw0), conv_w_grad(dcw1)], (2, CONV_TAPS, 2 * f)),
    ]
    width = V7X_LANES * math.gcd(d // V7X_LANES, 2 * f // V7X_LANES)
    summed, offsets = _all_reduce_small("reduce_small", [[a.reshape(-1, width) for a in group] for group, _ in small])
    full = [summed[off:off + math.prod(shape) // width].reshape(shape) for off, (_, shape) in zip(offsets, small)]
    g_pre_mix, g_post_mix, g_pre_ffn, g_post_ffn, g_kv, g_vgain, g_bs, g_ws, g_cb, g_cwf = full
    cw_w = 2 * f // N_DEV
    g_vgain = lax.dynamic_slice_in_dim(g_vgain, me * (d // N_DEV), d // N_DEV, axis=1)
    g_cwf = lax.dynamic_slice_in_dim(g_cwf, me * cw_w, cw_w, axis=2)

    parts = [p_in, p_out, p_k, p_v, p_q, p_o, p_up0, p_up1, p_dn0, p_dn1]

    def small_update(name, g, w, m, v):
        return [g] + _adamw(name, g, w, m, v)

    def stacked(name, part0, part1, w, m, v):
        shape = (2, *part0.shape[1:])
        return [o.reshape(w.shape) for o in _sum_adamw(name, [part0, part1], w.reshape(shape), m.reshape(shape), v.reshape(shape))]

    def single(name, part, w, m, v):
        shape = (1, *part.shape[1:])
        return [o.reshape(w.shape) for o in _sum_adamw(name, [part], w.reshape(shape), m.reshape(shape), v.reshape(shape))]

    results = {
        "pre_mix_g": small_update("adam_pre_mix", g_pre_mix, pre_mix_g, m_pre_mix_g, v_pre_mix_g),
        "post_mix_g": small_update("adam_post_mix", g_post_mix, post_mix_g, m_post_mix_g, v_post_mix_g),
        "pre_ffn_g": small_update("adam_pre_ffn", g_pre_ffn, pre_ffn_g, m_pre_ffn_g, v_pre_ffn_g),
        "post_ffn_g": small_update("adam_post_ffn", g_post_ffn, post_ffn_g, m_post_ffn_g, v_post_ffn_g),
        "a_w_in": single("adam_a_w_in", parts[0], a_w_in, m_a_w_in, v_a_w_in),
        "a_v_norm_g": small_update("adam_a_v_norm", g_vgain, a_v_norm_g, m_a_v_norm_g, v_a_v_norm_g),
        "a_w_spatial": small_update("adam_a_w_spatial", g_ws, a_w_spatial, m_a_w_spatial, v_a_w_spatial),
        "a_b_spatial": small_update("adam_a_b_spatial", g_bs, a_b_spatial, m_a_b_spatial, v_a_b_spatial),
        "a_w_out": single("adam_a_w_out", parts[1], a_w_out, m_a_w_out, v_a_w_out),
        "kv_norm_g": small_update("adam_kv_norm", g_kv, kv_norm_g, m_kv_norm_g, v_kv_norm_g),
        "w_k": single("adam_w_k", parts[2], w_k, m_w_k, v_w_k),
        "w_v": single("adam_w_v", parts[3], w_v, m_w_v, v_w_v),
        "b_w_q": single("adam_b_w_q", parts[4], b_w_q, m_b_w_q, v_b_w_q),
        "b_w_o": single("adam_b_w_o", parts[5], b_w_o, m_b_w_o, v_b_w_o),
        "ffn_w_up": stacked("adam_ffn_w_up", parts[6], parts[7], ffn_w_up, m_ffn_w_up, v_ffn_w_up),
        "ffn_conv_w": small_update("adam_ffn_conv_w", g_cwf, ffn_conv_w, m_ffn_conv_w, v_ffn_conv_w),
        "ffn_conv_b": small_update("adam_ffn_conv_b", g_cb, ffn_conv_b, m_ffn_conv_b, v_ffn_conv_b),
        "ffn_w_down": stacked("adam_ffn_w_down", parts[8], parts[9], ffn_w_down, m_ffn_w_down, v_ffn_w_down),
    }
    order = ["pre_mix_g", "post_mix_g", "pre_ffn_g", "post_ffn_g", "a_w_in", "a_v_norm_g", "a_w_spatial", "a_b_spatial",
             "a_w_out", "kv_norm_g", "w_k", "w_v", "b_w_q", "b_w_o", "ffn_w_up", "ffn_conv_w", "ffn_conv_b", "ffn_w_down"]
    outs = [loss, grad_x.reshape(x.shape)]
    for idx in range(4):
        outs += [results[n][idx] for n in order]
    return tuple(outs)
```

```python
import functools
import math
from typing import NamedTuple, Optional

import jax
import jax.numpy as jnp
from jax import lax
from jax.experimental import pallas as pl
from jax.experimental.pallas import tpu as pltpu

F32 = jnp.float32
BF = jnp.bfloat16
MESH = pl.DeviceIdType.MESH

N_DEV = 8
NORM_EPS = 1e-6
GROUP = 128
CONV_TAPS = 3
ADAM_LR, ADAM_B1, ADAM_B2, ADAM_EPS, ADAM_WD, ADAM_STEP = 0.001, 0.9, 0.999, 1e-08, 0.01, 10
EXP_FLOOR = -104.0

V7X_LANES = 128
V7X_VMEM_BYTES = 64 * 1024 * 1024
_MIB = 1024 * 1024

_NN = (((1,), (0,)), ((), ()))
_NT = (((1,), (1,)), ((), ()))
_TN = (((0,), (0,)), ((), ()))


def _tile(n, pref):
    if n <= pref:
        return n
    t = (pref // V7X_LANES) * V7X_LANES
    while t > V7X_LANES and n % t:
        t -= V7X_LANES
    assert n % t == 0, (n, pref)
    return t


def _nbytes(shape, dtype):
    return math.prod(shape) * jnp.dtype(dtype).itemsize


def _params(sem=None, vmem=None):
    kw = {}
    if sem is not None:
        kw["dimension_semantics"] = sem
    if vmem is not None:
        kw["vmem_limit_bytes"] = int(min(max(vmem, 16 * _MIB), V7X_VMEM_BYTES - 8 * _MIB))
    return pltpu.CompilerParams(**kw)


def _place():
    x, y, c = lax.axis_index("x"), lax.axis_index("y"), lax.axis_index("c")
    return x, y, c, 4 * x + 2 * y + c


def _flip(x, y, c, k):
    return (1 - x if k & 4 else x, 1 - y if k & 2 else y, 1 - c if k & 1 else c)


class _Carry(NamedTuple):
    gather: bool
    src: jax.Array
    dst: Optional[jax.Array]
    lo: int
    hi: int


def _gather(src, lo=0, hi=None, dst=None):
    return _Carry(True, src, dst, lo, src.shape[0] if hi is None else hi)


def _scatter(src, lo=0, hi=None, dst=None):
    return _Carry(False, src, dst, lo, src.shape[1] if hi is None else hi)


def _carry_phases(carries, srcs, dsts, send_sems, recv_sems, local_sems):
    x, y, c, me = _place()
    here, sibling = (x, y, c), (x, y, 1 - c)
    chips = [(1 - x, y), (x, 1 - y), (1 - x, 1 - y)]

    def rows(u):
        return pl.ds(carries[u].lo, carries[u].hi - carries[u].lo)

    def block_copy(u, sem, block, to, from_src=False):
        slot = dsts[u].at[4 * block[0] + 2 * block[1] + block[2], rows(u)]
        return pltpu.make_async_remote_copy(
            src_ref=srcs[u].at[rows(u)] if from_src else slot, dst_ref=slot, send_sem=send_sems.at[u, sem],
            recv_sem=recv_sems.at[u, sem], device_id=to, device_id_type=MESH)

    def partial_copy(u, k):
        peer = _flip(x, y, c, k)
        return pltpu.make_async_remote_copy(
            src_ref=srcs[u].at[4 * peer[0] + 2 * peer[1] + peer[2], rows(u)], dst_ref=dsts[u].at[me, rows(u)],
            send_sem=send_sems.at[u, k - 1], recv_sem=recv_sems.at[u, k - 1], device_id=peer, device_id_type=MESH)

    def local_copy(u):
        src = srcs[u].at[rows(u)] if carries[u].gather else srcs[u].at[me, rows(u)]
        return pltpu.make_async_copy(src, dsts[u].at[me, rows(u)], local_sems.at[u])

    def first():
        for u, cr in enumerate(carries):
            local_copy(u).start()
            if cr.gather:
                block_copy(u, 0, here, sibling, from_src=True).start()
                for j, chip in enumerate(chips):
                    block_copy(u, 1 + j, here, (*chip, c), from_src=True).start()
            else:
                for k in range(1, N_DEV):
                    partial_copy(u, k).start()

    def middle():
        for u, cr in enumerate(carries):
            if cr.gather:
                for j, chip in enumerate(chips):
                    block_copy(u, 1 + j, (*chip, c), here).wait_recv()
                    block_copy(u, 4 + j, (*chip, c), sibling).start()

    def last():
        for u, cr in enumerate(carries):
            if cr.gather:
                block_copy(u, 0, sibling, here).wait_recv()
                for j, chip in enumerate(chips):
                    block_copy(u, 4 + j, (*chip, 1 - c), here).wait_recv()
                block_copy(u, 0, here, sibling, from_src=True).wait_send()
                for j, chip in enumerate(chips):
                    block_copy(u, 1 + j, here, (*chip, c), from_src=True).wait_send()
                    block_copy(u, 4 + j, (*chip, c), sibling).wait_send()
            else:
                for k in range(1, N_DEV):
                    partial_copy(u, k).wait()
            local_copy(u).wait()

    return first, middle, last


def _call(body, *, name, grid, in_specs, out_specs, out_shape, operands, scratch=(), sem=None, vmem=None,
          carries=(), middle_at=0.6):
    if not carries:
        return pl.pallas_call(
            body, name=name, grid=grid, in_specs=in_specs, out_specs=out_specs, out_shape=out_shape,
            scratch_shapes=list(scratch), compiler_params=_params(sem, vmem))(*operands)
    n_in, n_out, n_scr, nc = len(in_specs), len(out_specs), len(scratch), len(carries)
    given = [u for u, cr in enumerate(carries) if cr.dst is not None]
    steps = math.prod(grid)
    middle_step = min(steps - 1, int(steps * middle_at))

    def wrapped(*refs):
        ins, srcs = refs[:n_in], refs[n_in:n_in + nc]
        at = n_in + nc + len(given)
        outs, dsts = refs[at:at + n_out], refs[at + n_out:at + n_out + nc]
        at += n_out + nc
        scr, (send_sems, recv_sems, local_sems) = refs[at:at + n_scr], refs[at + n_scr:]
        first, middle, last = _carry_phases(carries, srcs, dsts, send_sems, recv_sems, local_sems)
        step = 0
        for axis, size in enumerate(grid):
            step = step * size + pl.program_id(axis)
        pl.when(step == 0)(first)
        body(*ins, *outs, *scr)
        pl.when(step == middle_step)(middle)
        pl.when(step == steps - 1)(last)

    any_spec = pl.BlockSpec(memory_space=pl.ANY)
    dst_shapes = [jax.ShapeDtypeStruct((N_DEV, *cr.src.shape) if cr.gather else cr.src.shape, cr.src.dtype) for cr in carries]
    return pl.pallas_call(
        wrapped, name=name, grid=grid, in_specs=list(in_specs) + [any_spec] * (nc + len(given)),
        out_specs=list(out_specs) + [any_spec] * nc, out_shape=list(out_shape) + dst_shapes,
        input_output_aliases={n_in + nc + g: n_out + u for g, u in enumerate(given)},
        scratch_shapes=list(scratch) + [pltpu.SemaphoreType.DMA((nc, 7)), pltpu.SemaphoreType.DMA((nc, 7)),
                                        pltpu.SemaphoreType.DMA((nc,))],
        compiler_params=_params(("arbitrary",) * len(grid), vmem),
    )(*operands, *[cr.src for cr in carries], *[carries[u].dst for u in given])


def _mm(name, a, b, *, dims, grid, a_blk, a_map, b_blk, b_map, o_blk, o_map, out_shape, out_dtype,
        add=None, add_blk=None, add_map=None, carries=(), b_slabs=1):
    nk = grid[2]
    assert add is None or nk == 1
    acc_shape = tuple(d for d in o_blk if d is not None)
    in_place = out_dtype == F32

    def body(*refs):
        if add is None:
            a_ref, b_ref, o_ref = refs[:3]
            c_ref, scr = None, refs[3:]
        else:
            a_ref, b_ref, c_ref, o_ref = refs[:4]
            scr = refs[4:]
        if b_slabs == 1:
            part = lax.dot_general(a_ref[...], b_ref[...], dims, preferred_element_type=F32)
        else:
            cw = b_ref.shape[2]
            part = sum(lax.dot_general(a_ref[:, s * cw:(s + 1) * cw], b_ref[s], dims, preferred_element_type=F32)
                       for s in range(b_slabs))
        if c_ref is not None:
            part = part + c_ref[...].astype(F32)
        if nk == 1:
            o_ref[...] = part.astype(o_ref.dtype)
            return
        acc = o_ref if in_place else scr[0]
        k = pl.program_id(2)

        @pl.when(k == 0)
        def _():
            acc[...] = part

        @pl.when(k > 0 if in_place else jnp.logical_and(k > 0, k < nk - 1))
        def _():
            acc[...] += part

        if not in_place:
            @pl.when(k == nk - 1)
            def _():
                o_ref[...] = (acc[...] + part).astype(o_ref.dtype)

    in_specs = [pl.BlockSpec(a_blk, a_map), pl.BlockSpec(b_blk, b_map)]
    operands = [a, b]
    scratch = [pltpu.VMEM(acc_shape, F32)] if nk > 1 and not in_place else []
    vmem = 2 * (_nbytes(acc_shape, out_dtype) + _nbytes([d for d in a_blk if d], a.dtype)
                + _nbytes([d for d in b_blk if d], b.dtype)) + (2 + len(scratch)) * _nbytes(acc_shape, F32)
    if add is not None:
        in_specs.append(pl.BlockSpec(add_blk, add_map))
        operands.append(add)
        vmem += 2 * _nbytes(acc_shape, add.dtype)
    out, *dsts = _call(
        body, name=name, grid=grid, in_specs=in_specs, out_specs=[pl.BlockSpec(o_blk, o_map)],
        out_shape=[jax.ShapeDtypeStruct(out_shape, out_dtype)], operands=operands, scratch=scratch,
        sem=("parallel", "parallel", "arbitrary"), vmem=vmem + 8 * _MIB, carries=carries)
    return (out, dsts) if carries else out


def _mm_nn(name, x, w, out_dtype, carries=()):
    t, kd = x.shape
    n = w.shape[1]
    tm, tn, tk = _tile(t, 1024), _tile(n, 1024), _tile(kd, 1536 if kd > 2048 else 2048)
    return _mm(name, x, w, dims=_NN, grid=(t // tm, n // tn, kd // tk),
               a_blk=(tm, tk), a_map=lambda i, j, k: (i, k), b_blk=(tk, tn), b_map=lambda i, j, k: (k, j),
               o_blk=(tm, tn), o_map=lambda i, j, k: (i, j), out_shape=(t, n), out_dtype=out_dtype, carries=carries)


def _mm_nn_blk(name, x, g, out_dtype, halves=False, carries=()):
    t, kd = x.shape
    cw = g.shape[2]
    tm = _tile(t, 1024)
    if halves:
        o_blk, o_map, out_shape = (None, tm, cw), (lambda i, j, k: (j // 4, i, j % 4)), (2, t, 4 * cw)
    else:
        o_blk, o_map, out_shape = (tm, cw), (lambda i, j, k: (i, j)), (t, N_DEV * cw)
    return _mm(name, x, g, dims=_NN, grid=(t // tm, N_DEV, 1),
               a_blk=(tm, kd), a_map=lambda i, j, k: (i, 0), b_blk=(None, kd, cw), b_map=lambda i, j, k: (j, 0, 0),
               o_blk=o_blk, o_map=o_map, out_shape=out_shape, out_dtype=out_dtype, carries=carries)


def _mm_nt(name, dy, w, out_dtype, add=None, carries=()):
    t, n = dy.shape
    kd = w.shape[0]
    tm, tn = _tile(t, 1024), _tile(kd, 512)
    kw = {}
    if add is not None:
        kw = dict(add=add, add_blk=(tm, tn), add_map=lambda i, j, k: (i, j))
    return _mm(name, dy, w, dims=_NT, grid=(t // tm, kd // tn, 1),
               a_blk=(tm, n), a_map=lambda i, j, k: (i, 0), b_blk=(tn, n), b_map=lambda i, j, k: (j, 0),
               o_blk=(tm, tn), o_map=lambda i, j, k: (i, j), out_shape=(t, kd), out_dtype=out_dtype, carries=carries, **kw)


def _mm_nt_blk(name, dy, g, out_dtype, halves=False, carries=()):
    kd, cw = g.shape[1], g.shape[2]
    t = dy.shape[1] if halves else dy.shape[0]
    tm = _tile(t, 512)
    slabs = 2 if cw > 512 else 4
    nk = N_DEV // slabs
    if halves:
        a_blk, a_map = (None, tm, slabs * cw), (lambda i, j, k: (k // (nk // 2), i, k % (nk // 2)))
    else:
        a_blk, a_map = (tm, slabs * cw), (lambda i, j, k: (i, k))
    return _mm(name, dy, g, dims=_NT, grid=(t // tm, 1, nk), b_slabs=slabs,
               a_blk=a_blk, a_map=a_map, b_blk=(slabs, kd, cw), b_map=lambda i, j, k: (k, 0, 0),
               o_blk=(tm, kd), o_map=lambda i, j, k: (i, 0), out_shape=(t, kd), out_dtype=out_dtype, carries=carries)


def _mm_tn(name, x, dy, carries=()):
    t, kd = x.shape
    n = dy.shape[1]
    tmx, tk = _tile(kd, 1024), _tile(t, 2048)
    return _mm(name, x, dy, dims=_TN, grid=(kd // tmx, 1, t // tk),
               a_blk=(tk, tmx), a_map=lambda i, j, k: (k, i), b_blk=(tk, n), b_map=lambda i, j, k: (k, 0),
               o_blk=(tmx, n), o_map=lambda i, j, k: (i, 0), out_shape=(kd, n), out_dtype=BF, carries=carries)


def _mm_tn_blk(name, x, dy, halves=False, carries=()):
    t, kd = x.shape
    cw = dy.shape[2] // 4 if halves else dy.shape[1] // N_DEV
    tmx, tk = _tile(kd, 2048 if cw <= 512 else 1024), _tile(t, 2048)
    if halves:
        b_blk, b_map = (None, tk, cw), (lambda i, j, k: (j // 4, k, j % 4))
    else:
        b_blk, b_map = (tk, cw), (lambda i, j, k: (k, j))
    return _mm(name, x, dy, dims=_TN, grid=(kd // tmx, N_DEV, t // tk),
               a_blk=(tk, tmx), a_map=lambda i, j, k: (k, i), b_blk=b_blk, b_map=b_map,
               o_blk=(None, tmx, cw), o_map=lambda i, j, k: (j, i, 0), out_shape=(N_DEV, kd, cw), out_dtype=BF,
               carries=carries)


def _rms(x, g):
    r = lax.rsqrt(jnp.mean(x * x, axis=-1, keepdims=True) + NORM_EPS)
    return x * r * g


def _rms_bwd_math(x, g, dy):
    d = x.shape[-1]
    r = lax.rsqrt(jnp.mean(x * x, axis=-1, keepdims=True) + NORM_EPS)
    xh = x * r
    u = dy * g
    dx = r * u - xh * (jnp.sum(xh * u, axis=-1, keepdims=True) * (r / d))
    return dx, jnp.sum(dy * xh, axis=0, keepdims=True)


def _row_specs(tr, d, n):
    return [pl.BlockSpec((tr, d), lambda i: (i, 0)) for _ in range(n)]


def _vec_specs(d, n):
    return [pl.BlockSpec((1, d), lambda i: (0, 0)) for _ in range(n)]


def _norms_fwd(name, h, gains):
    t, d = h.shape
    tr, ng = _tile(t, 256), len(gains)

    def body(h_ref, *refs):
        x = h_ref[...]
        for g_ref, o_ref in zip(refs[:ng], refs[ng:]):
            o_ref[...] = _rms(x, g_ref[...]).astype(BF)

    return pl.pallas_call(
        body, name=name, grid=(t // tr,), in_specs=_row_specs(tr, d, 1) + _vec_specs(d, ng),
        out_specs=_row_specs(tr, d, ng), out_shape=[jax.ShapeDtypeStruct((t, d), BF)] * ng,
        compiler_params=_params(("parallel",)),
    )(h, *[g.reshape(1, d) for g in gains])


def _resid_norms(name, h, m, g_post, gains, carries=()):
    t, d = h.shape
    tr, ng = _tile(t, 256), len(gains)

    def body(h_ref, m_ref, gp_ref, *refs):
        hn = h_ref[...] + _rms(m_ref[...], gp_ref[...])
        refs[ng][...] = hn
        for g_ref, o_ref in zip(refs[:ng], refs[ng + 1:]):
            o_ref[...] = _rms(hn, g_ref[...]).astype(BF)

    return _call(
        body, name=name, grid=(t // tr,), in_specs=_row_specs(tr, d, 2) + _vec_specs(d, 1 + ng),
        out_specs=_row_specs(tr, d, 1 + ng),
        out_shape=[jax.ShapeDtypeStruct((t, d), F32)] + [jax.ShapeDtypeStruct((t, d), BF)] * ng,
        operands=(h, m, g_post.reshape(1, d), *[g.reshape(1, d) for g in gains]), sem=("parallel",), carries=carries)


def _resid_loss(name, h, m, g_post, target):
    t, d = h.shape
    tr = _tile(t, 256)

    def body(h_ref, m_ref, gp_ref, t_ref, dy_ref, loss_ref):
        diff = h_ref[...] + _rms(m_ref[...], gp_ref[...]) - t_ref[...]
        dy_ref[...] = diff * (1.0 / d)

        @pl.when(pl.program_id(0) == 0)
        def _():
            loss_ref[...] = jnp.zeros_like(loss_ref)

        per_row = jnp.sum(diff * diff, axis=-1, keepdims=True) * (1.0 / d)
        loss_ref[...] += 0.5 * jnp.sum(per_row, axis=0, keepdims=True)

    return pl.pallas_call(
        body, name=name, grid=(t // tr,),
        in_specs=_row_specs(tr, d, 2) + _vec_specs(d, 1) + _row_specs(tr, d, 1),
        out_specs=[pl.BlockSpec((tr, d), lambda i: (i, 0)), pl.BlockSpec((1, 1), lambda i: (0, 0))],
        out_shape=[jax.ShapeDtypeStruct((t, d), F32), jax.ShapeDtypeStruct((1, 1), F32)],
        compiler_params=_params(("arbitrary",)),
    )(h, m, g_post.reshape(1, d), target)


def _norm_bwd(name, x, g, dy, res=None, out_dtype=F32, carries=()):
    t, d = x.shape
    tr = _tile(t, 256)
    has_res = res is not None
    gains, dys = (g, dy) if isinstance(g, (list, tuple)) else ([g], [dy])
    n = len(gains)

    def body(x_ref, *refs):
        dy_refs, g_refs, rest = refs[:n], refs[n:2 * n], refs[2 * n:]
        dx_ref, dg_refs = rest[int(has_res)], rest[int(has_res) + 1:]
        xv = x_ref[...].astype(F32)
        dx = rest[0][...] if has_res else 0.0
        first = pl.program_id(0) == 0
        for dy_ref, g_ref, dg_ref in zip(dy_refs, g_refs, dg_refs):
            dx_one, dg = _rms_bwd_math(xv, g_ref[...], dy_ref[...].astype(F32))
            dx = dx + dx_one

            @pl.when(first)
            def _(dg_ref=dg_ref):
                dg_ref[...] = jnp.zeros_like(dg_ref)

            dg_ref[...] += dg
        dx_ref[...] = dx.astype(dx_ref.dtype)

    ops = [x, *dys, *[gain.reshape(1, d) for gain in gains]] + ([res] if has_res else [])
    return _call(
        body, name=name, grid=(t // tr,),
        in_specs=_row_specs(tr, d, 1 + n) + _vec_specs(d, n) + _row_specs(tr, d, int(has_res)),
        out_specs=[pl.BlockSpec((tr, d), lambda i: (i, 0))] + [pl.BlockSpec((1, d), lambda i: (0, 0))] * n,
        out_shape=[jax.ShapeDtypeStruct((t, d), out_dtype)] + [jax.ShapeDtypeStruct((1, d), F32)] * n,
        operands=ops, sem=("arbitrary",), carries=carries)


_GELU_C = math.sqrt(2.0 / math.pi)
_GELU_A = 0.044715


def _gelu(x):
    return 0.5 * x * (1.0 + jnp.tanh(_GELU_C * (x + _GELU_A * x * x * x)))


def _gelu_and_grad(x):
    th = jnp.tanh(_GELU_C * (x + _GELU_A * x * x * x))
    grad = 0.5 * (1.0 + th) + 0.5 * x * (1.0 - th * th) * (_GELU_C * (1.0 + 3.0 * _GELU_A * x * x))
    return 0.5 * x * (1.0 + th), grad


def _causal(n):
    return lax.broadcasted_iota(jnp.int32, (n, n), 1) <= lax.broadcasted_iota(jnp.int32, (n, n), 0)


def _sgu_fwd(name, p, v_gain, w_s, b_st, carries=()):
    t, da2 = p.shape
    da = da2 // 2
    ng = da // GROUP

    def body(p_ref, vg_ref, ws_ref, bst_ref, o_ref):
        keep = _causal(GROUP)
        for g in range(ng):
            lo = g * GROUP
            u = _gelu(p_ref[:, lo:lo + GROUP])
            vn = _rms(_gelu(p_ref[:, da + lo:da + lo + GROUP]), vg_ref[:, lo:lo + GROUP])
            w = jnp.where(keep, ws_ref[g], 0.0).astype(BF)
            mixed = jnp.dot(w, vn.astype(BF), preferred_element_type=F32) + bst_ref[:, g:g + 1]
            o_ref[:, lo:lo + GROUP] = (u * mixed).astype(BF)

    return _call(
        body, name=name, grid=(t // GROUP,),
        in_specs=[pl.BlockSpec((GROUP, da2), lambda i: (i, 0)), pl.BlockSpec((1, da), lambda i: (0, 0)),
                  pl.BlockSpec((ng, GROUP, GROUP), lambda i: (0, 0, 0)), pl.BlockSpec((GROUP, ng), lambda i: (0, 0))],
        out_specs=[pl.BlockSpec((GROUP, da), lambda i: (i, 0))], out_shape=[jax.ShapeDtypeStruct((t, da), BF)],
        operands=(p, v_gain, w_s, b_st), sem=("parallel",), carries=carries)


def _sgu_bwd(name, p, v_gain, w_s, w_st, b_st, dout, carries=()):
    t, da2 = p.shape
    da = da2 // 2
    ng = da // GROUP

    def body(p_ref, vg_ref, ws_ref, wst_ref, bst_ref, do_ref, dp_ref, dvg_ref, dws_ref, dbst_ref):
        @pl.when(pl.program_id(0) == 0)
        def _():
            dvg_ref[...] = jnp.zeros_like(dvg_ref)
            dws_ref[...] = jnp.zeros_like(dws_ref)
            dbst_ref[...] = jnp.zeros_like(dbst_ref)

        keep = _causal(GROUP)
        keep_t = lax.broadcasted_iota(jnp.int32, (GROUP, GROUP), 0) <= lax.broadcasted_iota(jnp.int32, (GROUP, GROUP), 1)
        for g in range(ng):
            lo = g * GROUP
            u, du = _gelu_and_grad(p_ref[:, lo:lo + GROUP])
            v, dv_act = _gelu_and_grad(p_ref[:, da + lo:da + lo + GROUP])
            gain = vg_ref[:, lo:lo + GROUP]
            r = lax.rsqrt(jnp.mean(v * v, axis=-1, keepdims=True) + NORM_EPS)
            vh = v * r
            vnb = (vh * gain).astype(BF)
            w = jnp.where(keep, ws_ref[g], 0.0).astype(BF)
            wt = jnp.where(keep_t, wst_ref[g], 0.0).astype(BF)
            mixed = jnp.dot(w, vnb, preferred_element_type=F32) + bst_ref[:, g:g + 1]
            dout_g = do_ref[:, lo:lo + GROUP].astype(F32)
            dmixed = dout_g * u
            dmb = dmixed.astype(BF)
            dbst_ref[:, g:g + 1] += jnp.sum(dmixed, axis=1, keepdims=True)
            dws_ref[g] += jnp.where(keep, lax.dot_general(dmb, vnb, _NT, preferred_element_type=F32), 0.0)
            dvn = jnp.dot(wt, dmb, preferred_element_type=F32)
            dvg_ref[:, lo:lo + GROUP] += jnp.sum(dvn * vh, axis=0, keepdims=True)
            dvh = dvn * gain
            dv = r * dvh - vh * (jnp.sum(vh * dvh, axis=-1, keepdims=True) * (r / GROUP))
            dp_ref[:, lo:lo + GROUP] = (dout_g * mixed * du).astype(BF)
            dp_ref[:, da + lo:da + lo + GROUP] = (dv * dv_act).astype(BF)

    full = lambda *shape: pl.BlockSpec(shape, lambda i: (0,) * len(shape))
    outs = _call(
        body, name=name, grid=(t // GROUP,),
        in_specs=[pl.BlockSpec((GROUP, da2), lambda i: (i, 0)), full(1, da), full(ng, GROUP, GROUP), full(ng, GROUP, GROUP),
                  full(GROUP, ng), pl.BlockSpec((GROUP, da), lambda i: (i, 0))],
        out_specs=[pl.BlockSpec((GROUP, da2), lambda i: (i, 0)), full(1, da), full(ng, GROUP, GROUP), full(GROUP, ng)],
        out_shape=[jax.ShapeDtypeStruct((t, da2), BF), jax.ShapeDtypeStruct((1, da), F32),
                   jax.ShapeDtypeStruct((ng, GROUP, GROUP), F32), jax.ShapeDtypeStruct((GROUP, ng), F32)],
        operands=(p, v_gain, w_s, w_st, b_st, dout), sem=("arbitrary",), carries=carries)
    return outs[:4], outs[4:]


_CONV_ROWS = 256
_CONV_COLS = 128


def _shift_down(x, prev, k):
    top = pltpu.roll(jnp.concatenate([prev, x[:8]], axis=0), k, 0)[8:16]
    return jnp.concatenate([top, pltpu.roll(x, k, 0)[8:]], axis=0)


def _conv_taps(a_ref, half, r0, first):
    rows = _rows(a_ref)
    x = a_ref[half, pl.ds(r0, rows), :]
    if first:
        prev = jnp.zeros((8, x.shape[1]), F32)
        return x, _shift_down(x, prev, 1), _shift_down(x, prev, 2)
    return x, a_ref[half, pl.ds(r0 - 1, rows), :], a_ref[half, pl.ds(r0 - 2, rows), :]


def _rows(a_ref):
    return min(_CONV_ROWS, a_ref.shape[1])


def _conv_fwd(name, a3, cw, cb, carries=()):
    _, t, f = a3.shape
    tc, rows = _CONV_COLS, min(_CONV_ROWS, t)

    def body(a_ref, cw_ref, cb_ref, y_ref):
        def chunk(r0, first):
            c = []
            for half in range(2):
                x, x1, x2 = _conv_taps(a_ref, half, r0, first)
                w = cw_ref[half]
                c.append(cb_ref[half] + (w[0:1] * x2 + w[1:2] * x1 + w[2:3] * x))
            y_ref[pl.ds(r0, rows), :] = (c[0] * jax.nn.sigmoid(c[0]) * c[1]).astype(BF)

        chunk(0, True)

        @pl.loop(1, t // rows)
        def _(r):
            chunk(pl.multiple_of(r * rows, rows), False)

    col = lambda *lead: pl.BlockSpec((*lead, tc), lambda j: (0,) * len(lead) + (j,))
    y, *dsts = _call(
        body, name=name, grid=(f // tc,), in_specs=[col(2, t), col(2, CONV_TAPS), col(2, 1)],
        out_specs=[col(t)], out_shape=[jax.ShapeDtypeStruct((t, f), BF)], operands=(a3, cw, cb),
        sem=("parallel",), vmem=40 * _MIB, carries=carries)
    return y, dsts


def _conv_bwd(name, a3, cw, cb, dy, carries=()):
    _, t, f = a3.shape
    tc, rows = _CONV_COLS, min(_CONV_ROWS, t)
    n_steps = t // rows

    def body(a_ref, cw_ref, cb_ref, dy_ref, da_ref, dcw_ref, dcb_ref, dc_ref):
        dcw_ref[...] = jnp.zeros_like(dcw_ref)
        dcb_ref[...] = jnp.zeros_like(dcb_ref)

        def chunk(r0, first, nxt):
            taps, c = [], []
            for half in range(2):
                x, x1, x2 = _conv_taps(a_ref, half, r0, first)
                w = cw_ref[half]
                taps.append((x2, x1, x))
                c.append(cb_ref[half] + (w[0:1] * x2 + w[1:2] * x1 + w[2:3] * x))
            gate, val = c
            sg = jax.nn.sigmoid(gate)
            dyv = dy_ref[pl.ds(r0, rows), :].astype(F32)
            dcs = (dyv * val * (sg * (1.0 + gate * (1.0 - sg))), dyv * (gate * sg))
            new_nxt = []
            for half in range(2):
                dc, w = dcs[half], cw_ref[half]
                dcb_ref[half] += jnp.sum(dc, axis=0, keepdims=True)
                for tap in range(CONV_TAPS):
                    dcw_ref[half, tap:tap + 1, :] += jnp.sum(dc * taps[half][tap], axis=0, keepdims=True)
                dc_ref[half, 0:rows, :] = dc
                dc_ref[half, rows:rows + 8, :] = nxt[half]
                da = w[2:3] * dc + w[1:2] * dc_ref[half, 1:rows + 1, :] + w[0:1] * dc_ref[half, 2:rows + 2, :]
                da_ref[half, pl.ds(r0, rows), :] = da.astype(BF)
                new_nxt.append(dc[:8])
            return tuple(new_nxt)

        zeros = jnp.zeros((8, tc), F32)
        nxt = lax.fori_loop(0, n_steps - 1, lambda s, nxt: chunk(pl.multiple_of((n_steps - 1 - s) * rows, rows), False, nxt),
                            (zeros, zeros))
        chunk(0, True, nxt)

    col = lambda *lead: pl.BlockSpec((*lead, tc), lambda j: (0,) * len(lead) + (j,))
    outs = _call(
        body, name=name, grid=(f // tc,), in_specs=[col(2, t), col(2, CONV_TAPS), col(2, 1), col(t)],
        out_specs=[col(2, t), col(2, CONV_TAPS), col(2, 1)],
        out_shape=[jax.ShapeDtypeStruct((2, t, f), BF), jax.ShapeDtypeStruct((2, CONV_TAPS, f), F32),
                   jax.ShapeDtypeStruct((2, 1, f), F32)],
        operands=(a3, cw, cb, dy), scratch=[pltpu.VMEM((2, rows + 8, tc), F32)], sem=("parallel",), vmem=40 * _MIB,
        carries=carries)
    return outs[:3], outs[3:]


_ATT_BLOCK = 256


def _split_dot(x, tri):
    hi = x.astype(BF)
    lo = (x - hi.astype(F32)).astype(BF)
    return jnp.dot(hi, tri, preferred_element_type=F32) + jnp.dot(lo, tri, preferred_element_type=F32)


_ATT_HEADS_FWD = 4
_ATT_HEADS_BWD = 4


def _logits(qb, kb, diagonal):
    z = lax.dot_general(qb, kb, _NT, preferred_element_type=F32) * (1.0 / math.sqrt(GROUP))
    lb = jnp.minimum(z, 0.0) - jnp.log(1.0 + jnp.exp(-jnp.abs(z)))
    if not diagonal:
        return lb, lb - z, None
    mask = lax.broadcasted_iota(jnp.int32, z.shape, 1) < lax.broadcasted_iota(jnp.int32, z.shape, 0)
    return lb, jnp.where(mask, lb - z, 0.0), mask


def _head(ref, g, rows=slice(None)):
    return ref[rows, g * GROUP:(g + 1) * GROUP]


def _attn_fwd(name, q, k, v, carries=()):
    t, hd = q.shape
    blk = min(_ATT_BLOCK, t)
    heads = min(_ATT_HEADS_FWD, hd // GROUP)
    gs = range(heads)

    def body(q_ref, k_ref, v_ref, o_ref, rest_ref, first_ref):
        hg, i = pl.program_id(0), pl.program_id(1)
        ri = lax.broadcasted_iota(jnp.int32, (blk, blk), 0)
        ci = lax.broadcasted_iota(jnp.int32, (blk, blk), 1)
        tri = (ri >= ci).astype(BF)

        def tile(j, state, diagonal):
            keys = pl.ds(pl.multiple_of(j * blk, blk), blk)
            logit = [_logits(_head(q_ref, g), _head(k_ref, g, keys), diagonal) for g in gs]
            incl = [_split_dot(logit[g][1], tri) for g in gs]
            a = [jnp.exp(logit[g][0] + (incl[g] - logit[g][1] + state[g][0])) for g in gs]
            if diagonal:
                a = [jnp.where(logit[g][2], a[g], 0.0) for g in gs]
            out = [state[g][1] + jnp.dot(a[g].astype(BF), _head(v_ref, g, keys), preferred_element_type=F32) for g in gs]
            return tuple((state[g][0] + incl[g][:, 0:1], out[g]) for g in gs)

        state = tile(i, ((jnp.zeros((blk, 1), F32), jnp.zeros((blk, GROUP), F32)),) * heads, True)

        def more(carry):
            j, state = carry
            live = functools.reduce(jnp.maximum, [jnp.max(right) for right, _ in state])
            return jnp.logical_and(j >= 0, live > EXP_FLOOR)

        j, state = lax.while_loop(more, lambda c: (c[0] - 1, tile(c[0], c[1], False)), (i - 1, state))
        for g, (right, acc) in enumerate(state):
            o_ref[:, g * GROUP:(g + 1) * GROUP] = acc.astype(BF)
            rest_ref[:, g * GROUP:(g + 1) * GROUP] = jnp.broadcast_to(right, (blk, GROUP))
        first_ref[hg, i] = (j + 1).astype(F32)

    qspec = pl.BlockSpec((blk, heads * GROUP), lambda h, i: (i, h))
    kvspec = pl.BlockSpec((t, heads * GROUP), lambda h, i: (0, h), pipeline_mode=pl.Buffered(1))
    groups = hd // (heads * GROUP)
    outs = _call(
        body, name=name, grid=(groups, t // blk), in_specs=[qspec, kvspec, kvspec],
        out_specs=[qspec, qspec, pl.BlockSpec(memory_space=pltpu.SMEM)],
        out_shape=[jax.ShapeDtypeStruct((t, hd), BF), jax.ShapeDtypeStruct((t, hd), F32),
                   jax.ShapeDtypeStruct((groups, t // blk), F32)],
        operands=(q, k, v), sem=("arbitrary", "arbitrary"), vmem=40 * _MIB, carries=carries, middle_at=0.75)
    return outs[:3], outs[3:]


def _attn_bwd(name, q, k, v, rest, first, do, carries=()):
    t, hd = q.shape
    blk = min(_ATT_BLOCK, t)
    nq = t // blk
    scale = 1.0 / math.sqrt(GROUP)
    heads = min(_ATT_HEADS_BWD, hd // GROUP)
    per_first = min(_ATT_HEADS_FWD, hd // GROUP) // heads
    gs = range(heads)

    def body(first_ref, q_ref, k_ref, v_ref, rest_ref, do_ref, dq_ref, dk_ref, dv_ref, dk_acc, dv_acc):
        hg, i = pl.program_id(0), pl.program_id(1)

        @pl.when(i == 0)
        def _():
            dk_acc[...] = jnp.zeros_like(dk_acc)
            dv_acc[...] = jnp.zeros_like(dv_acc)

        ri = lax.broadcasted_iota(jnp.int32, (blk, blk), 0)
        ci = lax.broadcasted_iota(jnp.int32, (blk, blk), 1)
        tri = (ri <= ci).astype(BF)

        def tile(j, state, diagonal):
            keys = pl.ds(pl.multiple_of(j * blk, blk), blk)
            qs, dos = [_head(q_ref, g) for g in gs], [_head(do_ref, g) for g in gs]
            kb, vb = [_head(k_ref, g, keys) for g in gs], [_head(v_ref, g, keys) for g in gs]
            logit = [_logits(qs[g], kb[g], diagonal) for g in gs]
            pre = [_split_dot(logit[g][1], tri) for g in gs]
            a = [jnp.exp(logit[g][0] + (rest_ref[:, g * GROUP:g * GROUP + 1] - state[g][0] - pre[g])) for g in gs]
            if diagonal:
                a = [jnp.where(logit[g][2], a[g], 0.0) for g in gs]
            gw = [a[g] * lax.dot_general(dos[g], vb[g], _NT, preferred_element_type=F32) for g in gs]
            gpre = [_split_dot(gw[g], tri) for g in gs]
            dz = []
            for g in gs:
                beta = jnp.exp(logit[g][0])
                d = (gw[g] * (1.0 - beta) - (state[g][1] + gpre[g] - gw[g]) * beta) * scale
                dz.append((jnp.where(logit[g][2], d, 0.0) if diagonal else d).astype(BF))
            for g in gs:
                dk_acc[keys, g * GROUP:(g + 1) * GROUP] += lax.dot_general(dz[g], qs[g], _TN, preferred_element_type=F32)
                dv_acc[keys, g * GROUP:(g + 1) * GROUP] += lax.dot_general(a[g].astype(BF), dos[g], _TN,
                                                                           preferred_element_type=F32)
            return tuple((state[g][0] + pre[g][:, blk - 1:blk], state[g][1] + gpre[g][:, blk - 1:blk],
                          state[g][2] + jnp.dot(dz[g], kb[g], preferred_element_type=F32)) for g in gs)

        zero = jnp.zeros((blk, 1), F32)
        first_block = jnp.clip(first_ref[hg // per_first, i].astype(jnp.int32), 0, i)
        state = lax.fori_loop(first_block, i, lambda j, c: tile(j, c, False),
                              ((zero, zero, jnp.zeros((blk, GROUP), F32)),) * heads)
        for g, (_, _, dq) in enumerate(tile(i, state, True)):
            dq_ref[:, g * GROUP:(g + 1) * GROUP] = dq.astype(BF)

        @pl.when(i == nq - 1)
        def _():
            dk_ref[...] = dk_acc[...].astype(BF)
            dv_ref[...] = dv_acc[...].astype(BF)

    qspec = pl.BlockSpec((blk, heads * GROUP), lambda h, i: (i, h))
    kvspec = pl.BlockSpec((t, heads * GROUP), lambda h, i: (0, h), pipeline_mode=pl.Buffered(1))
    outs = _call(
        body, name=name, grid=(hd // (heads * GROUP), nq),
        in_specs=[pl.BlockSpec(memory_space=pltpu.SMEM), qspec, kvspec, kvspec, qspec, qspec],
        out_specs=[qspec, kvspec, kvspec], out_shape=[jax.ShapeDtypeStruct((t, hd), BF)] * 3,
        operands=(first, q, k, v, rest, do),
        scratch=[pltpu.VMEM((t, heads * GROUP), F32), pltpu.VMEM((t, heads * GROUP), F32)],
        sem=("arbitrary", "arbitrary"), vmem=48 * _MIB, carries=carries)
    return outs[:3], outs[3:]


def _adamw_math(w, g, m, v):
    m = ADAM_B1 * m + (1.0 - ADAM_B1) * g
    v = ADAM_B2 * v + (1.0 - ADAM_B2) * (g * g)
    m_hat = m / (1.0 - ADAM_B1 ** ADAM_STEP)
    v_hat = v / (1.0 - ADAM_B2 ** ADAM_STEP)
    return -ADAM_LR * (m_hat / (jnp.sqrt(v_hat) + ADAM_EPS) + ADAM_WD * w), m, v


def _sum_adamw(name, parts, w, m, v):
    layers, r, c = w.shape
    budget = 512 * 1024 // layers
    tr = r if r * c <= budget else _tile_rows(r, max(8, (budget // c) // 8 * 8))
    n = r // tr

    def body(*refs):
        p_refs, (w_ref, m_ref, v_ref, g_ref, d_ref, nm_ref, nv_ref) = refs[:layers], refs[layers:]
        for layer, p_ref in enumerate(p_refs):
            @pl.when(pl.program_id(0) == layer)
            def _(p_ref=p_ref):
                g = p_ref[0].astype(F32)
                for dev in range(1, N_DEV):
                    g = g + p_ref[dev].astype(F32)
                g_ref[...] = g
                d_ref[...], nm_ref[...], nv_ref[...] = _adamw_math(w_ref[...], g, m_ref[...], v_ref[...])

    def part_spec(layer):
        return pl.BlockSpec((N_DEV, tr, c), lambda l, i: (0, jnp.where(l < layer, 0, jnp.where(l == layer, i, n - 1)), 0))

    row = pl.BlockSpec((None, tr, c), lambda l, i: (l, i, 0))
    return pl.pallas_call(
        body, name=name, grid=(layers, n), in_specs=[part_spec(layer) for layer in range(layers)] + [row] * 3,
        out_specs=[row] * 4, out_shape=[jax.ShapeDtypeStruct((layers, r, c), F32)] * 4,
        compiler_params=_params(("arbitrary", "arbitrary"), 40 * _MIB),
    )(*parts, w, m, v)


def _tile_rows(r, pref):
    t = min(r, pref)
    while r % t or t % 8:
        t -= 1
    return t


def _exchange(name, carries):
    return _call(lambda: None, name=name, grid=(1,), in_specs=[], out_specs=[], out_shape=[], operands=(), carries=carries)


def _all_reduce_small(name, groups):
    c = groups[0][0].shape[1]
    parts, offsets, starts, r = [], [], [], 0
    for group in groups:
        starts.append(r)
        for p in group:
            parts.append(p)
            offsets.append(r)
            r += p.shape[0]
        r = -(-r // 8) * 8
    n = len(parts)

    def body(*refs):
        out_ref, slots, send_sems, recv_sems = refs[n:]
        x, y, c_, me = _place()
        slots[me] = jnp.zeros((r, c), F32)
        for p_ref, off in zip(refs[:n], offsets):
            slots[me, off:off + p_ref.shape[0], :] = p_ref[...]

        here, sibling = (x, y, c_), (x, y, 1 - c_)
        chips = [(1 - x, y), (x, 1 - y), (1 - x, 1 - y)]

        def copy(sem, block, to):
            slot = slots.at[4 * block[0] + 2 * block[1] + block[2]]
            return pltpu.make_async_remote_copy(
                src_ref=slot, dst_ref=slot, send_sem=send_sems.at[sem], recv_sem=recv_sems.at[sem], device_id=to,
                device_id_type=MESH)

        sent = [copy(0, here, sibling)] + [copy(1 + j, here, (*chip, c_)) for j, chip in enumerate(chips)]
        for cp in sent:
            cp.start()
        for j, chip in enumerate(chips):
            copy(1 + j, (*chip, c_), here).wait_recv()
            sent.append(copy(4 + j, (*chip, c_), sibling))
            sent[-1].start()
        copy(0, sibling, here).wait_recv()
        for j, chip in enumerate(chips):
            copy(4 + j, (*chip, 1 - c_), here).wait_recv()
        for cp in sent:
            cp.wait_send()
        total = slots[0]
        for dev in range(1, N_DEV):
            total = total + slots[dev]
        out_ref[...] = total

    vm = pl.BlockSpec(memory_space=pltpu.VMEM)
    summed = pl.pallas_call(
        body, name=name, in_specs=[vm] * n, out_specs=vm, out_shape=jax.ShapeDtypeStruct((r, c), F32),
        scratch_shapes=[pltpu.VMEM((N_DEV, r, c), F32), pltpu.SemaphoreType.DMA((7,)), pltpu.SemaphoreType.DMA((7,))],
        compiler_params=_params(None, (N_DEV + 6) * r * c * 4 + 8 * _MIB),
    )(*parts)
    return summed, starts


def _adamw(name, g, w, m, v):
    cols = w.shape[-1]
    flat = lambda a: a.reshape(-1, cols)

    def body(g_ref, w_ref, m_ref, v_ref, d_ref, nm_ref, nv_ref):
        d_ref[...], nm_ref[...], nv_ref[...] = _adamw_math(w_ref[...], g_ref[...], m_ref[...], v_ref[...])

    outs = pl.pallas_call(body, name=name, out_shape=[jax.ShapeDtypeStruct(flat(w).shape, F32)] * 3)(
        flat(g), flat(w), flat(m), flat(v))
    return [o.reshape(w.shape) for o in outs]


def kernel(x, pre_mix_g, post_mix_g, pre_ffn_g, post_ffn_g, a_w_in, a_v_norm_g, a_w_spatial, a_b_spatial, a_w_out, kv_norm_g, w_k, w_v, b_w_q, b_w_o, ffn_w_up, ffn_conv_w, ffn_conv_b, ffn_w_down, loss_target, m_pre_mix_g, m_post_mix_g, m_pre_ffn_g, m_post_ffn_g, m_a_w_in, m_a_v_norm_g, m_a_w_spatial, m_a_b_spatial, m_a_w_out, m_kv_norm_g, m_w_k, m_w_v, m_b_w_q, m_b_w_o, m_ffn_w_up, m_ffn_conv_w, m_ffn_conv_b, m_ffn_w_down, v_pre_mix_g, v_post_mix_g, v_pre_ffn_g, v_post_ffn_g, v_a_w_in, v_a_v_norm_g, v_a_w_spatial, v_a_b_spatial, v_a_w_out, v_kv_norm_g, v_w_k, v_w_v, v_b_w_q, v_b_w_o, v_ffn_w_up, v_ffn_conv_w, v_ffn_conv_b, v_ffn_w_down):
    t, d = x.shape[1], x.shape[2]
    f = ffn_w_down.shape[1] * N_DEV
    ng = d // GROUP
    me = 4 * lax.axis_index("x") + 2 * lax.axis_index("y") + lax.axis_index("c")
    x2, target = x.reshape(t, d), loss_target.reshape(t, d)

    g_in, g_cw, g_vg = _exchange("gather_first", [
        _gather(a_w_in[0].astype(BF)), _gather(ffn_conv_w.reshape(2 * CONV_TAPS, -1)), _gather(a_v_norm_g)])
    up0 = ffn_w_up[0].astype(BF)
    cw_full = jnp.transpose(g_cw.reshape(N_DEV, 2, CONV_TAPS, -1), (1, 2, 0, 3)).reshape(2, CONV_TAPS, 2, f)
    cw_l = [jnp.transpose(cw_full[l], (1, 0, 2)) for l in range(2)]
    cb_l = [ffn_conv_b[l].reshape(2, 1, f) for l in range(2)]
    vg_full = g_vg.reshape(1, d)
    w_s = a_w_spatial[0]
    w_st = jnp.swapaxes(w_s, 1, 2)
    b_st = a_b_spatial[0].T

    (hn0,) = _norms_fwd("pre_mix0", x2, [pre_mix_g[0]])
    p0, (g_out, g_up0) = _mm_nn_blk("sgu_in", hn0, g_in, F32, carries=[
        _gather(a_w_out[0].astype(BF)), _gather(up0, 0, d // 4)])
    w_out_f = g_out.reshape(d, d)
    sg, g_up0 = _sgu_fwd("sgu", p0, vg_full, w_s, b_st, carries=[_gather(up0, d // 4, d // 2, dst=g_up0)])
    mix0, (g_up0,) = _mm_nn("sgu_out", sg, w_out_f, F32, carries=[_gather(up0, d // 2, 3 * d // 4, dst=g_up0)])
    h1, fn0, g_up0 = _resid_norms("post_mix0", x2, mix0, post_mix_g[0], [pre_ffn_g[0]],
                                  carries=[_gather(up0, 3 * d // 4, d, dst=g_up0)])
    a3_0, (g_dn0,) = _mm_nn_blk("ffn0_up", fn0, g_up0, F32, halves=True, carries=[_gather(ffn_w_down[0].astype(BF))])
    y0, (g_q,) = _conv_fwd("ffn0_conv", a3_0, cw_l[0], cb_l[0], carries=[_gather(b_w_q[0].astype(BF))])
    f0, (g_k, g_v) = _mm_nn("ffn0_down", y0, g_dn0.reshape(f, d), F32,
                            carries=[_gather(w_k.astype(BF)), _gather(w_v.astype(BF))])
    w_q_f, w_k_f, w_v_f = g_q.reshape(d, d), g_k.reshape(d, d), g_v.reshape(d, d)
    h2, hn1, kvn = _resid_norms("post_ffn0", h1, f0, post_ffn_g[0], [pre_mix_g[1], kv_norm_g])
    up1 = ffn_w_up[1].astype(BF)
    e = d // 8
    q, (g_up1,) = _mm_nn("attn_q", hn1, w_q_f, BF, carries=[_gather(up1, 0, e)])
    kk, (g_up1,) = _mm_nn("attn_k", kvn, w_k_f, BF, carries=[_gather(up1, e, 2 * e, dst=g_up1)])
    vv, (g_up1,) = _mm_nn("attn_v", kvn, w_v_f, BF, carries=[_gather(up1, 2 * e, 3 * e, dst=g_up1)])
    (att, rest, first), (g_o, g_up1) = _attn_fwd(
        "attn", q, kk, vv, carries=[_gather(b_w_o[0].astype(BF)), _gather(up1, 3 * e, 6 * e, dst=g_up1)])
    w_o_f = g_o.reshape(d, d)
    mix1, (g_up1,) = _mm_nn("attn_o", att, w_o_f, F32, carries=[_gather(up1, 6 * e, 7 * e, dst=g_up1)])
    h3, fn1, g_up1 = _resid_norms("post_mix1", h2, mix1, post_mix_g[1], [pre_ffn_g[1]],
                                  carries=[_gather(up1, 7 * e, d, dst=g_up1)])
    a3_1, (g_dn1,) = _mm_nn_blk("ffn1_up", fn1, g_up1, F32, halves=True, carries=[_gather(ffn_w_down[1].astype(BF))])
    y1, _ = _conv_fwd("ffn1_conv", a3_1, cw_l[1], cb_l[1])
    f1 = _mm_nn("ffn1_down", y1, g_dn1.reshape(f, d), F32)
    g_up = (g_up0, g_up1)
    w_dn_f = (g_dn0.reshape(f, d), g_dn1.reshape(f, d))
    dh, loss_part = _resid_loss("loss", h3, f1, post_ffn_g[1], target)
    loss = lax.psum(loss_part[0, 0], ("x", "y", "c"))

    def blocks(dw):
        return dw.reshape(N_DEV, -1, d)

    def split(result, carries):
        return result if carries else (result, [])

    def ffn_bwd(l, dh_out, h_in, fn, a3, yv, fo, with_dw=(), with_dx=(), with_up=()):
        dfo, d_post = _norm_bwd(f"post_ffn{l}_bwd", fo, post_ffn_g[l], dh_out, out_dtype=BF)
        dw_dn, sent_dw = split(_mm_tn(f"ffn{l}_down_dw", yv, dfo, carries=with_dw), with_dw)
        dy, sent_dx = split(_mm_nt(f"ffn{l}_down_dx", dfo, w_dn_f[l], BF, carries=with_dx), with_dx)
        (da3, dcw, dcb), (p_dn,) = _conv_bwd(f"ffn{l}_conv_bwd", a3, cw_l[l], cb_l[l], dy, carries=[_scatter(blocks(dw_dn))])
        dw_up, sent_up = split(_mm_tn_blk(f"ffn{l}_up_dw", fn, da3, halves=True, carries=with_up), with_up)
        dfn, (p_up,) = _mm_nt_blk(f"ffn{l}_up_dx", da3, g_up[l], F32, halves=True, carries=[_scatter(dw_up, 0, d // 2)])
        dh_in, d_pre, p_up = _norm_bwd(f"pre_ffn{l}_bwd", h_in, pre_ffn_g[l], dfn, res=dh_out,
                                       carries=[_scatter(dw_up, d // 2, 9 * d // 16, dst=p_up)])
        return dh_in, d_post, d_pre, dcw, dcb, p_dn, dw_up, p_up, list(sent_dw) + list(sent_dx) + list(sent_up)

    dh3, d_post_ffn1, d_pre_ffn1, dcw1, dcb1, p_dn1, dw_up1, p_up1, _ = ffn_bwd(1, dh, h3, fn1, a3_1, y1, f1)
    dmix1, d_post_mix1 = _norm_bwd("post_mix1_bwd", mix1, post_mix_g[1], dh3, out_dtype=BF)
    dw_o = _mm_tn("attn_o_dw", att, dmix1)
    datt = _mm_nt("attn_o_dx", dmix1, w_o_f, BF)
    (dq, dk, dv), (p_up1,) = _attn_bwd(
        "attn_bwd", q, kk, vv, rest, first, datt, carries=[_scatter(dw_up1, 9 * d // 16, d, dst=p_up1)])
    dw_o, qr = blocks(dw_o), d // 32
    dw_q, (p_o,) = _mm_tn("attn_q_dw", hn1, dq, carries=[_scatter(dw_o, 0, qr)])
    dw_k, (p_o,) = _mm_tn("attn_k_dw", kvn, dk, carries=[_scatter(dw_o, qr, 2 * qr, dst=p_o)])
    dw_v, (p_o,) = _mm_tn("attn_v_dw", kvn, dv, carries=[_scatter(dw_o, 2 * qr, 3 * qr, dst=p_o)])
    dhn1, (p_o,) = _mm_nt("attn_q_dx", dq, w_q_f, F32, carries=[_scatter(dw_o, 3 * qr, 4 * qr, dst=p_o)])
    dkvn = _mm_nt("attn_v_dx", dv, w_v_f, F32, add=_mm_nt("attn_k_dx", dk, w_k_f, F32))
    dh2, d_pre_mix1, d_kv = _norm_bwd("pre_mix1_kv_bwd", h2, [pre_mix_g[1], kv_norm_g], [dhn1, dkvn], res=dh3)
    dh1, d_post_ffn0, d_pre_ffn0, dcw0, dcb0, p_dn0, dw_up0, p_up0, (p_q, p_k, p_v) = ffn_bwd(
        0, dh2, h1, fn0, a3_0, y0, f0, with_dw=[_scatter(blocks(dw_q))], with_dx=[_scatter(blocks(dw_k))],
        with_up=[_scatter(blocks(dw_v))])
    dmix0, d_post_mix0 = _norm_bwd("post_mix0_bwd", mix0, post_mix_g[0], dh1, out_dtype=BF)
    dw_out, (p_up0,) = _mm_tn("sgu_out_dw", sg, dmix0, carries=[_scatter(dw_up0, 9 * d // 16, 11 * d // 16, dst=p_up0)])
    dsg = _mm_nt("sgu_out_dx", dmix0, w_out_f, BF)
    (dp0, d_vg, d_ws, d_bst), (p_up0,) = _sgu_bwd(
        "sgu_bwd", p0, vg_full, w_s, w_st, b_st, dsg, carries=[_scatter(dw_up0, 11 * d // 16, 15 * d // 16, dst=p_up0)])
    dw_in, (p_out,) = _mm_tn_blk("sgu_in_dw", hn0, dp0, carries=[_scatter(blocks(dw_out))])
    dhn0, (p_in,) = _mm_nt_blk("sgu_in_dx", dp0, g_in, F32, carries=[_scatter(dw_in, 0, 5 * d // 8)])
    grad_x, d_pre_mix0, p_in = _norm_bwd("pre_mix0_bwd", x2, pre_mix_g[0], dhn0, res=dh1,
                                         carries=[_scatter(dw_in, 5 * d // 8, 7 * d // 8, dst=p_in)])
    p_up0, p_in = _exchange("scatter_last", [_scatter(dw_up0, 15 * d // 16, d, dst=p_up0),
                                             _scatter(dw_in, 7 * d // 8, d, dst=p_in)])

    def conv_w_grad(dcw):
        return jnp.transpose(dcw, (1, 0, 2)).reshape(CONV_TAPS, 2 * f)

    small = [
        ([d_pre_mix0, d_pre_mix1], (2, d)), ([d_post_mix0, d_post_mix1], (2, d)),
        ([d_pre_ffn0, d_pre_ffn1], (2, d)), ([d_post_ffn0, d_post_ffn1], (2, d)),
        ([d_kv], (d,)), ([d_vg], (1, d)), ([d_bst.T], (1, ng, GROUP)), ([d_ws], (1, ng, GROUP, GROUP)),
        ([dcb0, dcb1], (2, 2 * f)), ([conv_w_grad(dcw0), conv_w_grad(dcw1)], (2, CONV_TAPS, 2 * f)),
    ]
    width = V7X_LANES * math.gcd(d // V7X_LANES, 2 * f // V7X_LANES)
    summed, offsets = _all_reduce_small("reduce_small", [[a.reshape(-1, width) for a in group] for group, _ in small])
    full = [summed[off:off + math.prod(shape) // width].reshape(shape) for off, (_, shape) in zip(offsets, small)]
    g_pre_mix, g_post_mix, g_pre_ffn, g_post_ffn, g_kv, g_vgain, g_bs, g_ws, g_cb, g_cwf = full
    cw_w = 2 * f // N_DEV
    g_vgain = lax.dynamic_slice_in_dim(g_vgain, me * (d // N_DEV), d // N_DEV, axis=1)
    g_cwf = lax.dynamic_slice_in_dim(g_cwf, me * cw_w, cw_w, axis=2)

    parts = [p_in, p_out, p_k, p_v, p_q, p_o, p_up0, p_up1, p_dn0, p_dn1]

    def small_update(name, g, w, m, v):
        return [g] + _adamw(name, g, w, m, v)

    def stacked(name, part0, part1, w, m, v):
        shape = (2, *part0.shape[1:])
        return [o.reshape(w.shape) for o in _sum_adamw(name, [part0, part1], w.reshape(shape), m.reshape(shape), v.reshape(shape))]

    def single(name, part, w, m, v):
        shape = (1, *part.shape[1:])
        return [o.reshape(w.shape) for o in _sum_adamw(name, [part], w.reshape(shape), m.reshape(shape), v.reshape(shape))]

    results = {
        "pre_mix_g": small_update("adam_pre_mix", g_pre_mix, pre_mix_g, m_pre_mix_g, v_pre_mix_g),
        "post_mix_g": small_update("adam_post_mix", g_post_mix, post_mix_g, m_post_mix_g, v_post_mix_g),
        "pre_ffn_g": small_update("adam_pre_ffn", g_pre_ffn, pre_ffn_g, m_pre_ffn_g, v_pre_ffn_g),
        "post_ffn_g": small_update("adam_post_ffn", g_post_ffn, post_ffn_g, m_post_ffn_g, v_post_ffn_g),
        "a_w_in": single("adam_a_w_in", parts[0], a_w_in, m_a_w_in, v_a_w_in),
        "a_v_norm_g": small_update("adam_a_v_norm", g_vgain, a_v_norm_g, m_a_v_norm_g, v_a_v_norm_g),
        "a_w_spatial": small_update("adam_a_w_spatial", g_ws, a_w_spatial, m_a_w_spatial, v_a_w_spatial),
        "a_b_spatial": small_update("adam_a_b_spatial", g_bs, a_b_spatial, m_a_b_spatial, v_a_b_spatial),
        "a_w_out": single("adam_a_w_out", parts[1], a_w_out, m_a_w_out, v_a_w_out),
        "kv_norm_g": small_update("adam_kv_norm", g_kv, kv_norm_g, m_kv_norm_g, v_kv_norm_g),
        "w_k": single("adam_w_k", parts[2], w_k, m_w_k, v_w_k),
        "w_v": single("adam_w_v", parts[3], w_v, m_w_v, v_w_v),
        "b_w_q": single("adam_b_w_q", parts[4], b_w_q, m_b_w_q, v_b_w_q),
        "b_w_o": single("adam_b_w_o", parts[5], b_w_o, m_b_w_o, v_b_w_o),
        "ffn_w_up": stacked("adam_ffn_w_up", parts[6], parts[7], ffn_w_up, m_ffn_w_up, v_ffn_w_up),
        "ffn_conv_w": small_update("adam_ffn_conv_w", g_cwf, ffn_conv_w, m_ffn_conv_w, v_ffn_conv_w),
        "ffn_conv_b": small_update("adam_ffn_conv_b", g_cb, ffn_conv_b, m_ffn_conv_b, v_ffn_conv_b),
        "ffn_w_down": stacked("adam_ffn_w_down", parts[8], parts[9], ffn_w_down, m_ffn_w_down, v_ffn_w_down),
    }
    order = ["pre_mix_g", "post_mix_g", "pre_ffn_g", "post_ffn_g", "a_w_in", "a_v_norm_g", "a_w_spatial", "a_b_spatial",
             "a_w_out", "kv_norm_g", "w_k", "w_v", "b_w_q", "b_w_o", "ffn_w_up", "ffn_conv_w", "ffn_conv_b", "ffn_w_down"]
    outs = [loss, grad_x.reshape(x.shape)]
    for idx in range(4):
        outs += [results[n][idx] for n in order]
    return tuple(outs)
```

```python
import functools
import math
from typing import NamedTuple, Optional

import jax
import jax.numpy as jnp
from jax import lax
from jax.experimental import pallas as pl
from jax.experimental.pallas import tpu as pltpu

F32 = jnp.float32
BF = jnp.bfloat16
MESH = pl.DeviceIdType.MESH

N_DEV = 8
NORM_EPS = 1e-6
GROUP = 128
CONV_TAPS = 3
ADAM_LR, ADAM_B1, ADAM_B2, ADAM_EPS, ADAM_WD, ADAM_STEP = 0.001, 0.9, 0.999, 1e-08, 0.01, 10
EXP_FLOOR = -104.0

V7X_LANES = 128
V7X_VMEM_BYTES = 64 * 1024 * 1024
_MIB = 1024 * 1024

_NN = (((1,), (0,)), ((), ()))
_NT = (((1,), (1,)), ((), ()))
_TN = (((0,), (0,)), ((), ()))


def _tile(n, pref):
    if n <= pref:
        return n
    t = (pref // V7X_LANES) * V7X_LANES
    while t > V7X_LANES and n % t:
        t -= V7X_LANES
    assert n % t == 0, (n, pref)
    return t


def _nbytes(shape, dtype):
    return math.prod(shape) * jnp.dtype(dtype).itemsize


def _params(sem=None, vmem=None):
    kw = {}
    if sem is not None:
        kw["dimension_semantics"] = sem
    if vmem is not None:
        kw["vmem_limit_bytes"] = int(min(max(vmem, 16 * _MIB), V7X_VMEM_BYTES - 8 * _MIB))
    return pltpu.CompilerParams(**kw)


def _place():
    x, y, c = lax.axis_index("x"), lax.axis_index("y"), lax.axis_index("c")
    return x, y, c, 4 * x + 2 * y + c


def _flip(x, y, c, k):
    return (1 - x if k & 4 else x, 1 - y if k & 2 else y, 1 - c if k & 1 else c)


class _Carry(NamedTuple):
    gather: bool
    src: jax.Array
    dst: Optional[jax.Array]
    lo: int
    hi: int


def _gather(src, lo=0, hi=None, dst=None):
    return _Carry(True, src, dst, lo, src.shape[0] if hi is None else hi)


def _scatter(src, lo=0, hi=None, dst=None):
    return _Carry(False, src, dst, lo, src.shape[1] if hi is None else hi)


def _carry_phases(carries, srcs, dsts, send_sems, recv_sems, local_sems):
    x, y, c, me = _place()
    here, sibling = (x, y, c), (x, y, 1 - c)
    chips = [(1 - x, y), (x, 1 - y), (1 - x, 1 - y)]

    def rows(u):
        return pl.ds(carries[u].lo, carries[u].hi - carries[u].lo)

    def block_copy(u, sem, block, to, from_src=False):
        slot = dsts[u].at[4 * block[0] + 2 * block[1] + block[2], rows(u)]
        return pltpu.make_async_remote_copy(
            src_ref=srcs[u].at[rows(u)] if from_src else slot, dst_ref=slot, send_sem=send_sems.at[u, sem],
            recv_sem=recv_sems.at[u, sem], device_id=to, device_id_type=MESH)

    def partial_copy(u, k):
        peer = _flip(x, y, c, k)
        return pltpu.make_async_remote_copy(
            src_ref=srcs[u].at[4 * peer[0] + 2 * peer[1] + peer[2], rows(u)], dst_ref=dsts[u].at[me, rows(u)],
            send_sem=send_sems.at[u, k - 1], recv_sem=recv_sems.at[u, k - 1], device_id=peer, device_id_type=MESH)

    def local_copy(u):
        src = srcs[u].at[rows(u)] if carries[u].gather else srcs[u].at[me, rows(u)]
        return pltpu.make_async_copy(src, dsts[u].at[me, rows(u)], local_sems.at[u])

    def first():
        for u, cr in enumerate(carries):
            local_copy(u).start()
            if cr.gather:
                block_copy(u, 0, here, sibling, from_src=True).start()
                for j, chip in enumerate(chips):
                    block_copy(u, 1 + j, here, (*chip, c), from_src=True).start()
            else:
                for k in range(1, N_DEV):
                    partial_copy(u, k).start()

    def middle():
        for u, cr in enumerate(carries):
            if cr.gather:
                for j, chip in enumerate(chips):
                    block_copy(u, 1 + j, (*chip, c), here).wait_recv()
                    block_copy(u, 4 + j, (*chip, c), sibling).start()

    def last():
        for u, cr in enumerate(carries):
            if cr.gather:
                block_copy(u, 0, sibling, here).wait_recv()
                for j, chip in enumerate(chips):
                    block_copy(u, 4 + j, (*chip, 1 - c), here).wait_recv()
                block_copy(u, 0, here, sibling, from_src=True).wait_send()
                for j, chip in enumerate(chips):
                    block_copy(u, 1 + j, here, (*chip, c), from_src=True).wait_send()
                    block_copy(u, 4 + j, (*chip, c), sibling).wait_send()
            else:
                for k in range(1, N_DEV):
                    partial_copy(u, k).wait()
            local_copy(u).wait()

    return first, middle, last


def _call(body, *, name, grid, in_specs, out_specs, out_shape, operands, scratch=(), sem=None, vmem=None,
          carries=(), middle_at=0.6):
    if not carries:
        return pl.pallas_call(
            body, name=name, grid=grid, in_specs=in_specs, out_specs=out_specs, out_shape=out_shape,
            scratch_shapes=list(scratch), compiler_params=_params(sem, vmem))(*operands)
    n_in, n_out, n_scr, nc = len(in_specs), len(out_specs), len(scratch), len(carries)
    given = [u for u, cr in enumerate(carries) if cr.dst is not None]
    steps = math.prod(grid)
    middle_step = min(steps - 1, int(steps * middle_at))

    def wrapped(*refs):
        ins, srcs = refs[:n_in], refs[n_in:n_in + nc]
        at = n_in + nc + len(given)
        outs, dsts = refs[at:at + n_out], refs[at + n_out:at + n_out + nc]
        at += n_out + nc
        scr, (send_sems, recv_sems, local_sems) = refs[at:at + n_scr], refs[at + n_scr:]
        first, middle, last = _carry_phases(carries, srcs, dsts, send_sems, recv_sems, local_sems)
        step = 0
        for axis, size in enumerate(grid):
            step = step * size + pl.program_id(axis)
        pl.when(step == 0)(first)
        body(*ins, *outs, *scr)
        pl.when(step == middle_step)(middle)
        pl.when(step == steps - 1)(last)

    any_spec = pl.BlockSpec(memory_space=pl.ANY)
    dst_shapes = [jax.ShapeDtypeStruct((N_DEV, *cr.src.shape) if cr.gather else cr.src.shape, cr.src.dtype) for cr in carries]
    return pl.pallas_call(
        wrapped, name=name, grid=grid, in_specs=list(in_specs) + [any_spec] * (nc + len(given)),
        out_specs=list(out_specs) + [any_spec] * nc, out_shape=list(out_shape) + dst_shapes,
        input_output_aliases={n_in + nc + g: n_out + u for g, u in enumerate(given)},
        scratch_shapes=list(scratch) + [pltpu.SemaphoreType.DMA((nc, 7)), pltpu.SemaphoreType.DMA((nc, 7)),
                                        pltpu.SemaphoreType.DMA((nc,))],
        compiler_params=_params(("arbitrary",) * len(grid), vmem),
    )(*operands, *[cr.src for cr in carries], *[carries[u].dst for u in given])


def _mm(name, a, b, *, dims, grid, a_blk, a_map, b_blk, b_map, o_blk, o_map, out_shape, out_dtype,
        add=None, add_blk=None, add_map=None, carries=(), b_slabs=1):
    nk = grid[2]
    assert add is None or nk == 1
    acc_shape = tuple(d for d in o_blk if d is not None)
    in_place = out_dtype == F32

    def body(*refs):
        if add is None:
            a_ref, b_ref, o_ref = refs[:3]
            c_ref, scr = None, refs[3:]
        else:
            a_ref, b_ref, c_ref, o_ref = refs[:4]
            scr = refs[4:]
        if b_slabs == 1:
            part = lax.dot_general(a_ref[...], b_ref[...], dims, preferred_element_type=F32)
        else:
            cw = b_ref.shape[2]
            part = sum(lax.dot_general(a_ref[:, s * cw:(s + 1) * cw], b_ref[s], dims, preferred_element_type=F32)
                       for s in range(b_slabs))
        if c_ref is not None:
            part = part + c_ref[...].astype(F32)
        if nk == 1:
            o_ref[...] = part.astype(o_ref.dtype)
            return
        acc = o_ref if in_place else scr[0]
        k = pl.program_id(2)

        @pl.when(k == 0)
        def _():
            acc[...] = part

        @pl.when(k > 0 if in_place else jnp.logical_and(k > 0, k < nk - 1))
        def _():
            acc[...] += part

        if not in_place:
            @pl.when(k == nk - 1)
            def _():
                o_ref[...] = (acc[...] + part).astype(o_ref.dtype)

    in_specs = [pl.BlockSpec(a_blk, a_map), pl.BlockSpec(b_blk, b_map)]
    operands = [a, b]
    scratch = [pltpu.VMEM(acc_shape, F32)] if nk > 1 and not in_place else []
    vmem = 2 * (_nbytes(acc_shape, out_dtype) + _nbytes([d for d in a_blk if d], a.dtype)
                + _nbytes([d for d in b_blk if d], b.dtype)) + (2 + len(scratch)) * _nbytes(acc_shape, F32)
    if add is not None:
        in_specs.append(pl.BlockSpec(add_blk, add_map))
        operands.append(add)
        vmem += 2 * _nbytes(acc_shape, add.dtype)
    out, *dsts = _call(
        body, name=name, grid=grid, in_specs=in_specs, out_specs=[pl.BlockSpec(o_blk, o_map)],
        out_shape=[jax.ShapeDtypeStruct(out_shape, out_dtype)], operands=operands, scratch=scratch,
        sem=("parallel", "parallel", "arbitrary"), vmem=vmem + 8 * _MIB, carries=carries)
    return (out, dsts) if carries else out


def _mm_nn(name, x, w, out_dtype, carries=()):
    t, kd = x.shape
    n = w.shape[1]
    tm, tn, tk = _tile(t, 1024), _tile(n, 1024), _tile(kd, 1536 if kd > 2048 else 2048)
    return _mm(name, x, w, dims=_NN, grid=(t // tm, n // tn, kd // tk),
               a_blk=(tm, tk), a_map=lambda i, j, k: (i, k), b_blk=(tk, tn), b_map=lambda i, j, k: (k, j),
               o_blk=(tm, tn), o_map=lambda i, j, k: (i, j), out_shape=(t, n), out_dtype=out_dtype, carries=carries)


def _mm_nn_blk(name, x, g, out_dtype, halves=False, carries=()):
    t, kd = x.shape
    cw = g.shape[2]
    tm = _tile(t, 1024)
    if halves:
        o_blk, o_map, out_shape = (None, tm, cw), (lambda i, j, k: (j // 4, i, j % 4)), (2, t, 4 * cw)
    else:
        o_blk, o_map, out_shape = (tm, cw), (lambda i, j, k: (i, j)), (t, N_DEV * cw)
    return _mm(name, x, g, dims=_NN, grid=(t // tm, N_DEV, 1),
               a_blk=(tm, kd), a_map=lambda i, j, k: (i, 0), b_blk=(None, kd, cw), b_map=lambda i, j, k: (j, 0, 0),
               o_blk=o_blk, o_map=o_map, out_shape=out_shape, out_dtype=out_dtype, carries=carries)


def _mm_nt(name, dy, w, out_dtype, add=None, carries=()):
    t, n = dy.shape
    kd = w.shape[0]
    tm, tn = _tile(t, 1024), _tile(kd, 512)
    kw = {}
    if add is not None:
        kw = dict(add=add, add_blk=(tm, tn), add_map=lambda i, j, k: (i, j))
    return _mm(name, dy, w, dims=_NT, grid=(t // tm, kd // tn, 1),
               a_blk=(tm, n), a_map=lambda i, j, k: (i, 0), b_blk=(tn, n), b_map=lambda i, j, k: (j, 0),
               o_blk=(tm, tn), o_map=lambda i, j, k: (i, j), out_shape=(t, kd), out_dtype=out_dtype, carries=carries, **kw)


def _mm_nt_blk(name, dy, g, out_dtype, halves=False, carries=()):
    kd, cw = g.shape[1], g.shape[2]
    t = dy.shape[1] if halves else dy.shape[0]
    tm = _tile(t, 512)
    slabs = 2 if cw > 512 else 4
    nk = N_DEV // slabs
    if halves:
        a_blk, a_map = (None, tm, slabs * cw), (lambda i, j, k: (k // (nk // 2), i, k % (nk // 2)))
    else:
        a_blk, a_map = (tm, slabs * cw), (lambda i, j, k: (i, k))
    return _mm(name, dy, g, dims=_NT, grid=(t // tm, 1, nk), b_slabs=slabs,
               a_blk=a_blk, a_map=a_map, b_blk=(slabs, kd, cw), b_map=lambda i, j, k: (k, 0, 0),
               o_blk=(tm, kd), o_map=lambda i, j, k: (i, 0), out_shape=(t, kd), out_dtype=out_dtype, carries=carries)


def _mm_tn(name, x, dy, carries=()):
    t, kd = x.shape
    n = dy.shape[1]
    tmx, tk = _tile(kd, 1024), _tile(t, 2048)
    return _mm(name, x, dy, dims=_TN, grid=(kd // tmx, 1, t // tk),
               a_blk=(tk, tmx), a_map=lambda i, j, k: (k, i), b_blk=(tk, n), b_map=lambda i, j, k: (k, 0),
               o_blk=(tmx, n), o_map=lambda i, j, k: (i, 0), out_shape=(kd, n), out_dtype=BF, carries=carries)


def _mm_tn_blk(name, x, dy, halves=False, carries=()):
    t, kd = x.shape
    cw = dy.shape[2] // 4 if halves else dy.shape[1] // N_DEV
    tmx, tk = _tile(kd, 2048 if cw <= 512 else 1024), _tile(t, 2048)
    if halves:
        b_blk, b_map = (None, tk, cw), (lambda i, j, k: (j // 4, k, j % 4))
    else:
        b_blk, b_map = (tk, cw), (lambda i, j, k: (k, j))
    return _mm(name, x, dy, dims=_TN, grid=(kd // tmx, N_DEV, t // tk),
               a_blk=(tk, tmx), a_map=lambda i, j, k: (k, i), b_blk=b_blk, b_map=b_map,
               o_blk=(None, tmx, cw), o_map=lambda i, j, k: (j, i, 0), out_shape=(N_DEV, kd, cw), out_dtype=BF,
               carries=carries)


def _rms(x, g):
    r = lax.rsqrt(jnp.mean(x * x, axis=-1, keepdims=True) + NORM_EPS)
    return x * r * g


def _rms_bwd_math(x, g, dy):
    d = x.shape[-1]
    r = lax.rsqrt(jnp.mean(x * x, axis=-1, keepdims=True) + NORM_EPS)
    xh = x * r
    u = dy * g
    dx = r * u - xh * (jnp.sum(xh * u, axis=-1, keepdims=True) * (r / d))
    return dx, jnp.sum(dy * xh, axis=0, keepdims=True)


def _row_specs(tr, d, n):
    return [pl.BlockSpec((tr, d), lambda i: (i, 0)) for _ in range(n)]


def _vec_specs(d, n):
    return [pl.BlockSpec((1, d), lambda i: (0, 0)) for _ in range(n)]


def _norms_fwd(name, h, gains):
    t, d = h.shape
    tr, ng = _tile(t, 256), len(gains)

    def body(h_ref, *refs):
        x = h_ref[...]
        for g_ref, o_ref in zip(refs[:ng], refs[ng:]):
            o_ref[...] = _rms(x, g_ref[...]).astype(BF)

    return pl.pallas_call(
        body, name=name, grid=(t // tr,), in_specs=_row_specs(tr, d, 1) + _vec_specs(d, ng),
        out_specs=_row_specs(tr, d, ng), out_shape=[jax.ShapeDtypeStruct((t, d), BF)] * ng,
        compiler_params=_params(("parallel",)),
    )(h, *[g.reshape(1, d) for g in gains])


def _resid_norms(name, h, m, g_post, gains, carries=()):
    t, d = h.shape
    tr, ng = _tile(t, 256), len(gains)

    def body(h_ref, m_ref, gp_ref, *refs):
        hn = h_ref[...] + _rms(m_ref[...], gp_ref[...])
        refs[ng][...] = hn
        for g_ref, o_ref in zip(refs[:ng], refs[ng + 1:]):
            o_ref[...] = _rms(hn, g_ref[...]).astype(BF)

    return _call(
        body, name=name, grid=(t // tr,), in_specs=_row_specs(tr, d, 2) + _vec_specs(d, 1 + ng),
        out_specs=_row_specs(tr, d, 1 + ng),
        out_shape=[jax.ShapeDtypeStruct((t, d), F32)] + [jax.ShapeDtypeStruct((t, d), BF)] * ng,
        operands=(h, m, g_post.reshape(1, d), *[g.reshape(1, d) for g in gains]), sem=("parallel",), carries=carries)


def _resid_loss(name, h, m, g_post, target):
    t, d = h.shape
    tr = _tile(t, 256)

    def body(h_ref, m_ref, gp_ref, t_ref, dy_ref, loss_ref):
        diff = h_ref[...] + _rms(m_ref[...], gp_ref[...]) - t_ref[...]
        dy_ref[...] = diff * (1.0 / d)

        @pl.when(pl.program_id(0) == 0)
        def _():
            loss_ref[...] = jnp.zeros_like(loss_ref)

        per_row = jnp.sum(diff * diff, axis=-1, keepdims=True) * (1.0 / d)
        loss_ref[...] += 0.5 * jnp.sum(per_row, axis=0, keepdims=True)

    return pl.pallas_call(
        body, name=name, grid=(t // tr,),
        in_specs=_row_specs(tr, d, 2) + _vec_specs(d, 1) + _row_specs(tr, d, 1),
        out_specs=[pl.BlockSpec((tr, d), lambda i: (i, 0)), pl.BlockSpec((1, 1), lambda i: (0, 0))],
        out_shape=[jax.ShapeDtypeStruct((t, d), F32), jax.ShapeDtypeStruct((1, 1), F32)],
        compiler_params=_params(("arbitrary",)),
    )(h, m, g_post.reshape(1, d), target)


def _norm_bwd(name, x, g, dy, res=None, out_dtype=F32, carries=()):
    t, d = x.shape
    tr = _tile(t, 256)
    has_res = res is not None
    gains, dys = (g, dy) if isinstance(g, (list, tuple)) else ([g], [dy])
    n = len(gains)

    def body(x_ref, *refs):
        dy_refs, g_refs, rest = refs[:n], refs[n:2 * n], refs[2 * n:]
        dx_ref, dg_refs = rest[int(has_res)], rest[int(has_res) + 1:]
        xv = x_ref[...].astype(F32)
        dx = rest[0][...] if has_res else 0.0
        first = pl.program_id(0) == 0
        for dy_ref, g_ref, dg_ref in zip(dy_refs, g_refs, dg_refs):
            dx_one, dg = _rms_bwd_math(xv, g_ref[...], dy_ref[...].astype(F32))
            dx = dx + dx_one

            @pl.when(first)
            def _(dg_ref=dg_ref):
                dg_ref[...] = jnp.zeros_like(dg_ref)

            dg_ref[...] += dg
        dx_ref[...] = dx.astype(dx_ref.dtype)

    ops = [x, *dys, *[gain.reshape(1, d) for gain in gains]] + ([res] if has_res else [])
    return _call(
        body, name=name, grid=(t // tr,),
        in_specs=_row_specs(tr, d, 1 + n) + _vec_specs(d, n) + _row_specs(tr, d, int(has_res)),
        out_specs=[pl.BlockSpec((tr, d), lambda i: (i, 0))] + [pl.BlockSpec((1, d), lambda i: (0, 0))] * n,
        out_shape=[jax.ShapeDtypeStruct((t, d), out_dtype)] + [jax.ShapeDtypeStruct((1, d), F32)] * n,
        operands=ops, sem=("arbitrary",), carries=carries)


_GELU_C = math.sqrt(2.0 / math.pi)
_GELU_A = 0.044715


def _gelu(x):
    return 0.5 * x * (1.0 + jnp.tanh(_GELU_C * (x + _GELU_A * x * x * x)))


def _gelu_and_grad(x):
    th = jnp.tanh(_GELU_C * (x + _GELU_A * x * x * x))
    grad = 0.5 * (1.0 + th) + 0.5 * x * (1.0 - th * th) * (_GELU_C * (1.0 + 3.0 * _GELU_A * x * x))
    return 0.5 * x * (1.0 + th), grad


def _causal(n):
    return lax.broadcasted_iota(jnp.int32, (n, n), 1) <= lax.broadcasted_iota(jnp.int32, (n, n), 0)


def _sgu_fwd(name, p, v_gain, w_s, b_st, carries=()):
    t, da2 = p.shape
    da = da2 // 2
    ng = da // GROUP

    def body(p_ref, vg_ref, ws_ref, bst_ref, o_ref):
        keep = _causal(GROUP)
        for g in range(ng):
            lo = g * GROUP
            u = _gelu(p_ref[:, lo:lo + GROUP])
            vn = _rms(_gelu(p_ref[:, da + lo:da + lo + GROUP]), vg_ref[:, lo:lo + GROUP])
            w = jnp.where(keep, ws_ref[g], 0.0).astype(BF)
            mixed = jnp.dot(w, vn.astype(BF), preferred_element_type=F32) + bst_ref[:, g:g + 1]
            o_ref[:, lo:lo + GROUP] = (u * mixed).astype(BF)

    return _call(
        body, name=name, grid=(t // GROUP,),
        in_specs=[pl.BlockSpec((GROUP, da2), lambda i: (i, 0)), pl.BlockSpec((1, da), lambda i: (0, 0)),
                  pl.BlockSpec((ng, GROUP, GROUP), lambda i: (0, 0, 0)), pl.BlockSpec((GROUP, ng), lambda i: (0, 0))],
        out_specs=[pl.BlockSpec((GROUP, da), lambda i: (i, 0))], out_shape=[jax.ShapeDtypeStruct((t, da), BF)],
        operands=(p, v_gain, w_s, b_st), sem=("parallel",), carries=carries)


def _sgu_bwd(name, p, v_gain, w_s, w_st, b_st, dout, carries=()):
    t, da2 = p.shape
    da = da2 // 2
    ng = da // GROUP

    def body(p_ref, vg_ref, ws_ref, wst_ref, bst_ref, do_ref, dp_ref, dvg_ref, dws_ref, dbst_ref):
        @pl.when(pl.program_id(0) == 0)
        def _():
            dvg_ref[...] = jnp.zeros_like(dvg_ref)
            dws_ref[...] = jnp.zeros_like(dws_ref)
            dbst_ref[...] = jnp.zeros_like(dbst_ref)

        keep = _causal(GROUP)
        keep_t = lax.broadcasted_iota(jnp.int32, (GROUP, GROUP), 0) <= lax.broadcasted_iota(jnp.int32, (GROUP, GROUP), 1)
        for g in range(ng):
            lo = g * GROUP
            u, du = _gelu_and_grad(p_ref[:, lo:lo + GROUP])
            v, dv_act = _gelu_and_grad(p_ref[:, da + lo:da + lo + GROUP])
            gain = vg_ref[:, lo:lo + GROUP]
            r = lax.rsqrt(jnp.mean(v * v, axis=-1, keepdims=True) + NORM_EPS)
            vh = v * r
            vnb = (vh * gain).astype(BF)
            w = jnp.where(keep, ws_ref[g], 0.0).astype(BF)
            wt = jnp.where(keep_t, wst_ref[g], 0.0).astype(BF)
            mixed = jnp.dot(w, vnb, preferred_element_type=F32) + bst_ref[:, g:g + 1]
            dout_g = do_ref[:, lo:lo + GROUP].astype(F32)
            dmixed = dout_g * u
            dmb = dmixed.astype(BF)
            dbst_ref[:, g:g + 1] += jnp.sum(dmixed, axis=1, keepdims=True)
            dws_ref[g] += jnp.where(keep, lax.dot_general(dmb, vnb, _NT, preferred_element_type=F32), 0.0)
            dvn = jnp.dot(wt, dmb, preferred_element_type=F32)
            dvg_ref[:, lo:lo + GROUP] += jnp.sum(dvn * vh, axis=0, keepdims=True)
            dvh = dvn * gain
            dv = r * dvh - vh * (jnp.sum(vh * dvh, axis=-1, keepdims=True) * (r / GROUP))
            dp_ref[:, lo:lo + GROUP] = (dout_g * mixed * du).astype(BF)
            dp_ref[:, da + lo:da + lo + GROUP] = (dv * dv_act).astype(BF)

    full = lambda *shape: pl.BlockSpec(shape, lambda i: (0,) * len(shape))
    outs = _call(
        body, name=name, grid=(t // GROUP,),
        in_specs=[pl.BlockSpec((GROUP, da2), lambda i: (i, 0)), full(1, da), full(ng, GROUP, GROUP), full(ng, GROUP, GROUP),
                  full(GROUP, ng), pl.BlockSpec((GROUP, da), lambda i: (i, 0))],
        out_specs=[pl.BlockSpec((GROUP, da2), lambda i: (i, 0)), full(1, da), full(ng, GROUP, GROUP), full(GROUP, ng)],
        out_shape=[jax.ShapeDtypeStruct((t, da2), BF), jax.ShapeDtypeStruct((1, da), F32),
                   jax.ShapeDtypeStruct((ng, GROUP, GROUP), F32), jax.ShapeDtypeStruct((GROUP, ng), F32)],
        operands=(p, v_gain, w_s, w_st, b_st, dout), sem=("arbitrary",), carries=carries)
    return outs[:4], outs[4:]


_CONV_ROWS = 128
_CONV_COLS = 256


def _shift_down(x, prev, k):
    top = pltpu.roll(jnp.concatenate([prev, x[:8]], axis=0), k, 0)[8:16]
    return jnp.concatenate([top, pltpu.roll(x, k, 0)[8:]], axis=0)


def _conv_taps(a_ref, half, r0, first):
    x = a_ref[half, pl.ds(r0, _rows(a_ref)), :]
    prev = jnp.zeros((8, x.shape[1]), F32) if first else a_ref[half, pl.ds(r0 - 8, 8), :]
    return x, _shift_down(x, prev, 1), _shift_down(x, prev, 2)


def _rows(a_ref):
    return min(_CONV_ROWS, a_ref.shape[1])


def _conv_cols(f):
    return _CONV_COLS if f % _CONV_COLS == 0 else V7X_LANES


def _conv_fwd(name, a3, cw, cb, carries=()):
    _, t, f = a3.shape
    rows, tc = min(_CONV_ROWS, t), _conv_cols(f)

    def body(a_ref, cw_ref, cb_ref, y_ref):
        def chunk(r0, first):
            c = []
            for half in range(2):
                x, x1, x2 = _conv_taps(a_ref, half, r0, first)
                w = cw_ref[half]
                c.append(cb_ref[half] + (w[0:1] * x2 + w[1:2] * x1 + w[2:3] * x))
            y_ref[pl.ds(r0, rows), :] = (c[0] * jax.nn.sigmoid(c[0]) * c[1]).astype(BF)

        chunk(0, True)

        @pl.loop(1, t // rows)
        def _(r):
            chunk(pl.multiple_of(r * rows, rows), False)

    col = lambda *lead: pl.BlockSpec((*lead, tc), lambda j: (0,) * len(lead) + (j,))
    y, *dsts = _call(
        body, name=name, grid=(f // tc,), in_specs=[col(2, t), col(2, CONV_TAPS), col(2, 1)],
        out_specs=[col(t)], out_shape=[jax.ShapeDtypeStruct((t, f), BF)], operands=(a3, cw, cb),
        sem=("parallel",), vmem=40 * _MIB, carries=carries)
    return y, dsts


def _conv_bwd(name, a3, cw, cb, dy, carries=()):
    _, t, f = a3.shape
    rows, tc = min(_CONV_ROWS, t), _conv_cols(f)
    n_steps = t // rows

    def body(a_ref, cw_ref, cb_ref, dy_ref, da_ref, dcw_ref, dcb_ref, dc_ref):
        dcw_ref[...] = jnp.zeros_like(dcw_ref)
        dcb_ref[...] = jnp.zeros_like(dcb_ref)

        def chunk(r0, first, nxt):
            taps, c = [], []
            for half in range(2):
                x, x1, x2 = _conv_taps(a_ref, half, r0, first)
                w = cw_ref[half]
                taps.append((x2, x1, x))
                c.append(cb_ref[half] + (w[0:1] * x2 + w[1:2] * x1 + w[2:3] * x))
            gate, val = c
            sg = jax.nn.sigmoid(gate)
            dyv = dy_ref[pl.ds(r0, rows), :].astype(F32)
            dcs = (dyv * val * (sg * (1.0 + gate * (1.0 - sg))), dyv * (gate * sg))
            new_nxt = []
            for half in range(2):
                dc, w = dcs[half], cw_ref[half]
                dcb_ref[half] += jnp.sum(dc, axis=0, keepdims=True)
                for tap in range(CONV_TAPS):
                    dcw_ref[half, tap:tap + 1, :] += jnp.sum(dc * taps[half][tap], axis=0, keepdims=True)
                dc_ref[half, 0:rows, :] = dc
                dc_ref[half, rows:rows + 8, :] = nxt[half]
                da = w[2:3] * dc + w[1:2] * dc_ref[half, 1:rows + 1, :] + w[0:1] * dc_ref[half, 2:rows + 2, :]
                da_ref[half, pl.ds(r0, rows), :] = da.astype(BF)
                new_nxt.append(dc[:8])
            return tuple(new_nxt)

        zeros = jnp.zeros((8, tc), F32)
        nxt = lax.fori_loop(0, n_steps - 1, lambda s, nxt: chunk(pl.multiple_of((n_steps - 1 - s) * rows, rows), False, nxt),
                            (zeros, zeros))
        chunk(0, True, nxt)

    col = lambda *lead: pl.BlockSpec((*lead, tc), lambda j: (0,) * len(lead) + (j,))
    outs = _call(
        body, name=name, grid=(f // tc,), in_specs=[col(2, t), col(2, CONV_TAPS), col(2, 1), col(t)],
        out_specs=[col(2, t), col(2, CONV_TAPS), col(2, 1)],
        out_shape=[jax.ShapeDtypeStruct((2, t, f), BF), jax.ShapeDtypeStruct((2, CONV_TAPS, f), F32),
                   jax.ShapeDtypeStruct((2, 1, f), F32)],
        operands=(a3, cw, cb, dy), scratch=[pltpu.VMEM((2, rows + 8, tc), F32)], sem=("parallel",), vmem=48 * _MIB,
        carries=carries)
    return outs[:3], outs[3:]


_ATT_BLOCK = 256


def _split_dot(x, tri):
    hi = x.astype(BF)
    lo = (x - hi.astype(F32)).astype(BF)
    return jnp.dot(hi, tri, preferred_element_type=F32) + jnp.dot(lo, tri, preferred_element_type=F32)


_ATT_HEADS_FWD = 4
_ATT_HEADS_BWD = 4


def _logits(qb, kb, diagonal):
    z = lax.dot_general(qb, kb, _NT, preferred_element_type=F32) * (1.0 / math.sqrt(GROUP))
    lb = jnp.minimum(z, 0.0) - jnp.log(1.0 + jnp.exp(-jnp.abs(z)))
    if not diagonal:
        return lb, lb - z, None
    mask = lax.broadcasted_iota(jnp.int32, z.shape, 1) < lax.broadcasted_iota(jnp.int32, z.shape, 0)
    return lb, jnp.where(mask, lb - z, 0.0), mask


def _head(ref, g, rows=slice(None)):
    return ref[rows, g * GROUP:(g + 1) * GROUP]


def _attn_fwd(name, q, k, v, carries=()):
    t, hd = q.shape
    blk = min(_ATT_BLOCK, t)
    heads = min(_ATT_HEADS_FWD, hd // GROUP)
    gs = range(heads)

    def body(q_ref, k_ref, v_ref, o_ref, rest_ref, first_ref):
        hg, i = pl.program_id(0), pl.program_id(1)
        ri = lax.broadcasted_iota(jnp.int32, (blk, blk), 0)
        ci = lax.broadcasted_iota(jnp.int32, (blk, blk), 1)
        tri = (ri >= ci).astype(BF)

        def tile(j, state, diagonal):
            keys = pl.ds(pl.multiple_of(j * blk, blk), blk)
            logit = [_logits(_head(q_ref, g), _head(k_ref, g, keys), diagonal) for g in gs]
            incl = [_split_dot(logit[g][1], tri) for g in gs]
            a = [jnp.exp(logit[g][0] + (incl[g] - logit[g][1] + state[g][0])) for g in gs]
            if diagonal:
                a = [jnp.where(logit[g][2], a[g], 0.0) for g in gs]
            out = [state[g][1] + jnp.dot(a[g].astype(BF), _head(v_ref, g, keys), preferred_element_type=F32) for g in gs]
            return tuple((state[g][0] + incl[g][:, 0:1], out[g]) for g in gs)

        state = tile(i, ((jnp.zeros((blk, 1), F32), jnp.zeros((blk, GROUP), F32)),) * heads, True)

        def more(carry):
            j, state = carry
            live = functools.reduce(jnp.maximum, [jnp.max(right) for right, _ in state])
            return jnp.logical_and(j >= 0, live > EXP_FLOOR)

        j, state = lax.while_loop(more, lambda c: (c[0] - 1, tile(c[0], c[1], False)), (i - 1, state))
        for g, (right, acc) in enumerate(state):
            o_ref[:, g * GROUP:(g + 1) * GROUP] = acc.astype(BF)
            rest_ref[:, g * GROUP:(g + 1) * GROUP] = jnp.broadcast_to(right, (blk, GROUP))
        first_ref[hg, i] = (j + 1).astype(F32)

    qspec = pl.BlockSpec((blk, heads * GROUP), lambda h, i: (i, h))
    kvspec = pl.BlockSpec((t, heads * GROUP), lambda h, i: (0, h), pipeline_mode=pl.Buffered(1))
    groups = hd // (heads * GROUP)
    outs = _call(
        body, name=name, grid=(groups, t // blk), in_specs=[qspec, kvspec, kvspec],
        out_specs=[qspec, qspec, pl.BlockSpec(memory_space=pltpu.SMEM)],
        out_shape=[jax.ShapeDtypeStruct((t, hd), BF), jax.ShapeDtypeStruct((t, hd), F32),
                   jax.ShapeDtypeStruct((groups, t // blk), F32)],
        operands=(q, k, v), sem=("arbitrary", "arbitrary"), vmem=40 * _MIB, carries=carries, middle_at=0.75)
    return outs[:3], outs[3:]


def _attn_bwd(name, q, k, v, rest, first, do, carries=()):
    t, hd = q.shape
    blk = min(_ATT_BLOCK, t)
    nq = t // blk
    scale = 1.0 / math.sqrt(GROUP)
    heads = min(_ATT_HEADS_BWD, hd // GROUP)
    per_first = min(_ATT_HEADS_FWD, hd // GROUP) // heads
    gs = range(heads)

    def body(first_ref, q_ref, k_ref, v_ref, rest_ref, do_ref, dq_ref, dk_ref, dv_ref, dk_acc, dv_acc):
        hg, i = pl.program_id(0), pl.program_id(1)

        @pl.when(i == 0)
        def _():
            dk_acc[...] = jnp.zeros_like(dk_acc)
            dv_acc[...] = jnp.zeros_like(dv_acc)

        ri = lax.broadcasted_iota(jnp.int32, (blk, blk), 0)
        ci = lax.broadcasted_iota(jnp.int32, (blk, blk), 1)
        tri = (ri <= ci).astype(BF)

        def tile(j, state, diagonal):
            keys = pl.ds(pl.multiple_of(j * blk, blk), blk)
            qs, dos = [_head(q_ref, g) for g in gs], [_head(do_ref, g) for g in gs]
            kb, vb = [_head(k_ref, g, keys) for g in gs], [_head(v_ref, g, keys) for g in gs]
            logit = [_logits(qs[g], kb[g], diagonal) for g in gs]
            pre = [_split_dot(logit[g][1], tri) for g in gs]
            a = [jnp.exp(logit[g][0] + (rest_ref[:, g * GROUP:g * GROUP + 1] - state[g][0] - pre[g])) for g in gs]
            if diagonal:
                a = [jnp.where(logit[g][2], a[g], 0.0) for g in gs]
            gw = [a[g] * lax.dot_general(dos[g], vb[g], _NT, preferred_element_type=F32) for g in gs]
            gpre = [_split_dot(gw[g], tri) for g in gs]
            dz = []
            for g in gs:
                beta = jnp.exp(logit[g][0])
                d = (gw[g] * (1.0 - beta) - (state[g][1] + gpre[g] - gw[g]) * beta) * scale
                dz.append((jnp.where(logit[g][2], d, 0.0) if diagonal else d).astype(BF))
            for g in gs:
                dk_acc[keys, g * GROUP:(g + 1) * GROUP] += lax.dot_general(dz[g], qs[g], _TN, preferred_element_type=F32)
                dv_acc[keys, g * GROUP:(g + 1) * GROUP] += lax.dot_general(a[g].astype(BF), dos[g], _TN,
                                                                           preferred_element_type=F32)
            return tuple((state[g][0] + pre[g][:, blk - 1:blk], state[g][1] + gpre[g][:, blk - 1:blk],
                          state[g][2] + jnp.dot(dz[g], kb[g], preferred_element_type=F32)) for g in gs)

        zero = jnp.zeros((blk, 1), F32)
        first_block = jnp.clip(first_ref[hg // per_first, i].astype(jnp.int32), 0, i)
        state = lax.fori_loop(first_block, i, lambda j, c: tile(j, c, False),
                              ((zero, zero, jnp.zeros((blk, GROUP), F32)),) * heads)
        for g, (_, _, dq) in enumerate(tile(i, state, True)):
            dq_ref[:, g * GROUP:(g + 1) * GROUP] = dq.astype(BF)

        @pl.when(i == nq - 1)
        def _():
            dk_ref[...] = dk_acc[...].astype(BF)
            dv_ref[...] = dv_acc[...].astype(BF)

    qspec = pl.BlockSpec((blk, heads * GROUP), lambda h, i: (i, h))
    kvspec = pl.BlockSpec((t, heads * GROUP), lambda h, i: (0, h), pipeline_mode=pl.Buffered(1))
    outs = _call(
        body, name=name, grid=(hd // (heads * GROUP), nq),
        in_specs=[pl.BlockSpec(memory_space=pltpu.SMEM), qspec, kvspec, kvspec, qspec, qspec],
        out_specs=[qspec, kvspec, kvspec], out_shape=[jax.ShapeDtypeStruct((t, hd), BF)] * 3,
        operands=(first, q, k, v, rest, do),
        scratch=[pltpu.VMEM((t, heads * GROUP), F32), pltpu.VMEM((t, heads * GROUP), F32)],
        sem=("arbitrary", "arbitrary"), vmem=48 * _MIB, carries=carries)
    return outs[:3], outs[3:]


def _adamw_math(w, g, m, v):
    m = ADAM_B1 * m + (1.0 - ADAM_B1) * g
    v = ADAM_B2 * v + (1.0 - ADAM_B2) * (g * g)
    m_hat = m / (1.0 - ADAM_B1 ** ADAM_STEP)
    v_hat = v / (1.0 - ADAM_B2 ** ADAM_STEP)
    return -ADAM_LR * (m_hat / (jnp.sqrt(v_hat) + ADAM_EPS) + ADAM_WD * w), m, v


def _sum_adamw(name, parts, w, m, v):
    layers, r, c = w.shape
    budget = 512 * 1024 // layers
    tr = r if r * c <= budget else _tile_rows(r, max(8, (budget // c) // 8 * 8))
    n = r // tr

    def body(*refs):
        p_refs, (w_ref, m_ref, v_ref, g_ref, d_ref, nm_ref, nv_ref) = refs[:layers], refs[layers:]
        for layer, p_ref in enumerate(p_refs):
            @pl.when(pl.program_id(0) == layer)
            def _(p_ref=p_ref):
                g = p_ref[0].astype(F32)
                for dev in range(1, N_DEV):
                    g = g + p_ref[dev].astype(F32)
                g_ref[...] = g
                d_ref[...], nm_ref[...], nv_ref[...] = _adamw_math(w_ref[...], g, m_ref[...], v_ref[...])

    def part_spec(layer):
        return pl.BlockSpec((N_DEV, tr, c), lambda l, i: (0, jnp.where(l < layer, 0, jnp.where(l == layer, i, n - 1)), 0))

    row = pl.BlockSpec((None, tr, c), lambda l, i: (l, i, 0))
    return pl.pallas_call(
        body, name=name, grid=(layers, n), in_specs=[part_spec(layer) for layer in range(layers)] + [row] * 3,
        out_specs=[row] * 4, out_shape=[jax.ShapeDtypeStruct((layers, r, c), F32)] * 4,
        compiler_params=_params(("arbitrary", "arbitrary"), 40 * _MIB),
    )(*parts, w, m, v)


def _tile_rows(r, pref):
    t = min(r, pref)
    while r % t or t % 8:
        t -= 1
    return t


def _exchange(name, carries):
    return _call(lambda: None, name=name, grid=(1,), in_specs=[], out_specs=[], out_shape=[], operands=(), carries=carries)


def _all_reduce_small(name, groups):
    c = groups[0][0].shape[1]
    parts, offsets, starts, r = [], [], [], 0
    for group in groups:
        starts.append(r)
        for p in group:
            parts.append(p)
            offsets.append(r)
            r += p.shape[0]
        r = -(-r // 8) * 8
    n = len(parts)

    def body(*refs):
        out_ref, slots, send_sems, recv_sems = refs[n:]
        x, y, c_, me = _place()
        slots[me] = jnp.zeros((r, c), F32)
        for p_ref, off in zip(refs[:n], offsets):
            slots[me, off:off + p_ref.shape[0], :] = p_ref[...]

        here, sibling = (x, y, c_), (x, y, 1 - c_)
        chips = [(1 - x, y), (x, 1 - y), (1 - x, 1 - y)]

        def copy(sem, block, to):
            slot = slots.at[4 * block[0] + 2 * block[1] + block[2]]
            return pltpu.make_async_remote_copy(
                src_ref=slot, dst_ref=slot, send_sem=send_sems.at[sem], recv_sem=recv_sems.at[sem], device_id=to,
                device_id_type=MESH)

        sent = [copy(0, here, sibling)] + [copy(1 + j, here, (*chip, c_)) for j, chip in enumerate(chips)]
        for cp in sent:
            cp.start()
        for j, chip in enumerate(chips):
            copy(1 + j, (*chip, c_), here).wait_recv()
            sent.append(copy(4 + j, (*chip, c_), sibling))
            sent[-1].start()
        copy(0, sibling, here).wait_recv()
        for j, chip in enumerate(chips):
            copy(4 + j, (*chip, 1 - c_), here).wait_recv()
        for cp in sent:
            cp.wait_send()
        total = slots[0]
        for dev in range(1, N_DEV):
            total = total + slots[dev]
        out_ref[...] = total

    vm = pl.BlockSpec(memory_space=pltpu.VMEM)
    summed = pl.pallas_call(
        body, name=name, in_specs=[vm] * n, out_specs=vm, out_shape=jax.ShapeDtypeStruct((r, c), F32),
        scratch_shapes=[pltpu.VMEM((N_DEV, r, c), F32), pltpu.SemaphoreType.DMA((7,)), pltpu.SemaphoreType.DMA((7,))],
        compiler_params=_params(None, (N_DEV + 6) * r * c * 4 + 8 * _MIB),
    )(*parts)
    return summed, starts


def _adamw(name, g, w, m, v):
    cols = w.shape[-1]
    flat = lambda a: a.reshape(-1, cols)

    def body(g_ref, w_ref, m_ref, v_ref, d_ref, nm_ref, nv_ref):
        d_ref[...], nm_ref[...], nv_ref[...] = _adamw_math(w_ref[...], g_ref[...], m_ref[...], v_ref[...])

    outs = pl.pallas_call(body, name=name, out_shape=[jax.ShapeDtypeStruct(flat(w).shape, F32)] * 3)(
        flat(g), flat(w), flat(m), flat(v))
    return [o.reshape(w.shape) for o in outs]


def kernel(x, pre_mix_g, post_mix_g, pre_ffn_g, post_ffn_g, a_w_in, a_v_norm_g, a_w_spatial, a_b_spatial, a_w_out, kv_norm_g, w_k, w_v, b_w_q, b_w_o, ffn_w_up, ffn_conv_w, ffn_conv_b, ffn_w_down, loss_target, m_pre_mix_g, m_post_mix_g, m_pre_ffn_g, m_post_ffn_g, m_a_w_in, m_a_v_norm_g, m_a_w_spatial, m_a_b_spatial, m_a_w_out, m_kv_norm_g, m_w_k, m_w_v, m_b_w_q, m_b_w_o, m_ffn_w_up, m_ffn_conv_w, m_ffn_conv_b, m_ffn_w_down, v_pre_mix_g, v_post_mix_g, v_pre_ffn_g, v_post_ffn_g, v_a_w_in, v_a_v_norm_g, v_a_w_spatial, v_a_b_spatial, v_a_w_out, v_kv_norm_g, v_w_k, v_w_v, v_b_w_q, v_b_w_o, v_ffn_w_up, v_ffn_conv_w, v_ffn_conv_b, v_ffn_w_down):
    t, d = x.shape[1], x.shape[2]
    f = ffn_w_down.shape[1] * N_DEV
    ng = d // GROUP
    me = 4 * lax.axis_index("x") + 2 * lax.axis_index("y") + lax.axis_index("c")
    x2, target = x.reshape(t, d), loss_target.reshape(t, d)

    g_in, g_cw, g_vg = _exchange("gather_first", [
        _gather(a_w_in[0].astype(BF)), _gather(ffn_conv_w.reshape(2 * CONV_TAPS, -1)), _gather(a_v_norm_g)])
    up0 = ffn_w_up[0].astype(BF)
    cw_full = jnp.transpose(g_cw.reshape(N_DEV, 2, CONV_TAPS, -1), (1, 2, 0, 3)).reshape(2, CONV_TAPS, 2, f)
    cw_l = [jnp.transpose(cw_full[l], (1, 0, 2)) for l in range(2)]
    cb_l = [ffn_conv_b[l].reshape(2, 1, f) for l in range(2)]
    vg_full = g_vg.reshape(1, d)
    w_s = a_w_spatial[0]
    w_st = jnp.swapaxes(w_s, 1, 2)
    b_st = a_b_spatial[0].T

    (hn0,) = _norms_fwd("pre_mix0", x2, [pre_mix_g[0]])
    p0, (g_out, g_up0) = _mm_nn_blk("sgu_in", hn0, g_in, F32, carries=[
        _gather(a_w_out[0].astype(BF)), _gather(up0, 0, d // 4)])
    w_out_f = g_out.reshape(d, d)
    sg, g_up0 = _sgu_fwd("sgu", p0, vg_full, w_s, b_st, carries=[_gather(up0, d // 4, d // 2, dst=g_up0)])
    mix0, (g_up0,) = _mm_nn("sgu_out", sg, w_out_f, F32, carries=[_gather(up0, d // 2, 3 * d // 4, dst=g_up0)])
    h1, fn0, g_up0 = _resid_norms("post_mix0", x2, mix0, post_mix_g[0], [pre_ffn_g[0]],
                                  carries=[_gather(up0, 3 * d // 4, d, dst=g_up0)])
    a3_0, (g_dn0,) = _mm_nn_blk("ffn0_up", fn0, g_up0, F32, halves=True, carries=[_gather(ffn_w_down[0].astype(BF))])
    y0, (g_q,) = _conv_fwd("ffn0_conv", a3_0, cw_l[0], cb_l[0], carries=[_gather(b_w_q[0].astype(BF))])
    f0, (g_k, g_v) = _mm_nn("ffn0_down", y0, g_dn0.reshape(f, d), F32,
                            carries=[_gather(w_k.astype(BF)), _gather(w_v.astype(BF))])
    w_q_f, w_k_f, w_v_f = g_q.reshape(d, d), g_k.reshape(d, d), g_v.reshape(d, d)
    h2, hn1, kvn = _resid_norms("post_ffn0", h1, f0, post_ffn_g[0], [pre_mix_g[1], kv_norm_g])
    up1 = ffn_w_up[1].astype(BF)
    e = d // 8
    q, (g_up1,) = _mm_nn("attn_q", hn1, w_q_f, BF, carries=[_gather(up1, 0, e)])
    kk, (g_up1,) = _mm_nn("attn_k", kvn, w_k_f, BF, carries=[_gather(up1, e, 2 * e, dst=g_up1)])
    vv, (g_up1,) = _mm_nn("attn_v", kvn, w_v_f, BF, carries=[_gather(up1, 2 * e, 3 * e, dst=g_up1)])
    (att, rest, first), (g_o, g_up1) = _attn_fwd(
        "attn", q, kk, vv, carries=[_gather(b_w_o[0].astype(BF)), _gather(up1, 3 * e, 6 * e, dst=g_up1)])
    w_o_f = g_o.reshape(d, d)
    mix1, (g_up1,) = _mm_nn("attn_o", att, w_o_f, F32, carries=[_gather(up1, 6 * e, 7 * e, dst=g_up1)])
    h3, fn1, g_up1 = _resid_norms("post_mix1", h2, mix1, post_mix_g[1], [pre_ffn_g[1]],
                                  carries=[_gather(up1, 7 * e, d, dst=g_up1)])
    a3_1, (g_dn1,) = _mm_nn_blk("ffn1_up", fn1, g_up1, F32, halves=True, carries=[_gather(ffn_w_down[1].astype(BF))])
    y1, _ = _conv_fwd("ffn1_conv", a3_1, cw_l[1], cb_l[1])
    f1 = _mm_nn("ffn1_down", y1, g_dn1.reshape(f, d), F32)
    g_up = (g_up0, g_up1)
    w_dn_f = (g_dn0.reshape(f, d), g_dn1.reshape(f, d))
    dh, loss_part = _resid_loss("loss", h3, f1, post_ffn_g[1], target)
    loss = lax.psum(loss_part[0, 0], ("x", "y", "c"))

    def blocks(dw):
        return dw.reshape(N_DEV, -1, d)

    def split(result, carries):
        return result if carries else (result, [])

    def ffn_bwd(l, dh_out, h_in, fn, a3, yv, fo, with_dw=(), with_dx=(), with_up=()):
        dfo, d_post = _norm_bwd(f"post_ffn{l}_bwd", fo, post_ffn_g[l], dh_out, out_dtype=BF)
        dw_dn, sent_dw = split(_mm_tn(f"ffn{l}_down_dw", yv, dfo, carries=with_dw), with_dw)
        dy, sent_dx = split(_mm_nt(f"ffn{l}_down_dx", dfo, w_dn_f[l], BF, carries=with_dx), with_dx)
        (da3, dcw, dcb), (p_dn,) = _conv_bwd(f"ffn{l}_conv_bwd", a3, cw_l[l], cb_l[l], dy, carries=[_scatter(blocks(dw_dn))])
        dw_up, sent_up = split(_mm_tn_blk(f"ffn{l}_up_dw", fn, da3, halves=True, carries=with_up), with_up)
        dfn, (p_up,) = _mm_nt_blk(f"ffn{l}_up_dx", da3, g_up[l], F32, halves=True, carries=[_scatter(dw_up, 0, d // 2)])
        dh_in, d_pre, p_up = _norm_bwd(f"pre_ffn{l}_bwd", h_in, pre_ffn_g[l], dfn, res=dh_out,
                                       carries=[_scatter(dw_up, d // 2, 9 * d // 16, dst=p_up)])
        return dh_in, d_post, d_pre, dcw, dcb, p_dn, dw_up, p_up, list(sent_dw) + list(sent_dx) + list(sent_up)

    dh3, d_post_ffn1, d_pre_ffn1, dcw1, dcb1, p_dn1, dw_up1, p_up1, _ = ffn_bwd(1, dh, h3, fn1, a3_1, y1, f1)
    dmix1, d_post_mix1 = _norm_bwd("post_mix1_bwd", mix1, post_mix_g[1], dh3, out_dtype=BF)
    dw_o = _mm_tn("attn_o_dw", att, dmix1)
    datt = _mm_nt("attn_o_dx", dmix1, w_o_f, BF)
    (dq, dk, dv), (p_up1,) = _attn_bwd(
        "attn_bwd", q, kk, vv, rest, first, datt, carries=[_scatter(dw_up1, 9 * d // 16, d, dst=p_up1)])
    dw_o, qr = blocks(dw_o), d // 32
    dw_q, (p_o,) = _mm_tn("attn_q_dw", hn1, dq, carries=[_scatter(dw_o, 0, qr)])
    dw_k, (p_o,) = _mm_tn("attn_k_dw", kvn, dk, carries=[_scatter(dw_o, qr, 2 * qr, dst=p_o)])
    dw_v, (p_o,) = _mm_tn("attn_v_dw", kvn, dv, carries=[_scatter(dw_o, 2 * qr, 3 * qr, dst=p_o)])
    dhn1, (p_o,) = _mm_nt("attn_q_dx", dq, w_q_f, F32, carries=[_scatter(dw_o, 3 * qr, 4 * qr, dst=p_o)])
    dkvn = _mm_nt("attn_v_dx", dv, w_v_f, F32, add=_mm_nt("attn_k_dx", dk, w_k_f, F32))
    dh2, d_pre_mix1, d_kv = _norm_bwd("pre_mix1_kv_bwd", h2, [pre_mix_g[1], kv_norm_g], [dhn1, dkvn], res=dh3)
    dh1, d_post_ffn0, d_pre_ffn0, dcw0, dcb0, p_dn0, dw_up0, p_up0, (p_q, p_k, p_v) = ffn_bwd(
        0, dh2, h1, fn0, a3_0, y0, f0, with_dw=[_scatter(blocks(dw_q))], with_dx=[_scatter(blocks(dw_k))],
        with_up=[_scatter(blocks(dw_v))])
    dmix0, d_post_mix0 = _norm_bwd("post_mix0_bwd", mix0, post_mix_g[0], dh1, out_dtype=BF)
    dw_out, (p_up0,) = _mm_tn("sgu_out_dw", sg, dmix0, carries=[_scatter(dw_up0, 9 * d // 16, 11 * d // 16, dst=p_up0)])
    dsg = _mm_nt("sgu_out_dx", dmix0, w_out_f, BF)
    (dp0, d_vg, d_ws, d_bst), (p_up0,) = _sgu_bwd(
        "sgu_bwd", p0, vg_full, w_s, w_st, b_st, dsg, carries=[_scatter(dw_up0, 11 * d // 16, 15 * d // 16, dst=p_up0)])
    dw_in, (p_out,) = _mm_tn_blk("sgu_in_dw", hn0, dp0, carries=[_scatter(blocks(dw_out))])
    dhn0, (p_in,) = _mm_nt_blk("sgu_in_dx", dp0, g_in, F32, carries=[_scatter(dw_in, 0, 5 * d // 8)])
    grad_x, d_pre_mix0, p_in = _norm_bwd("pre_mix0_bwd", x2, pre_mix_g[0], dhn0, res=dh1,
                                         carries=[_scatter(dw_in, 5 * d // 8, 7 * d // 8, dst=p_in)])
    p_up0, p_in = _exchange("scatter_last", [_scatter(dw_up0, 15 * d // 16, d, dst=p_up0),
                                             _scatter(dw_in, 7 * d // 8, d, dst=p_in)])

    def conv_w_grad(dcw):
        return jnp.transpose(dcw, (1, 0, 2)).reshape(CONV_TAPS, 2 * f)

    small = [
        ([d_pre_mix0, d_pre_mix1], (2, d)), ([d_post_mix0, d_post_mix1], (2, d)),
        ([d_pre_ffn0, d_pre_ffn1], (2, d)), ([d_post_ffn0, d_post_ffn1], (2, d)),
        ([d_kv], (d,)), ([d_vg], (1, d)), ([d_bst.T], (1, ng, GROUP)), ([d_ws], (1, ng, GROUP, GROUP)),
        ([dcb0, dcb1], (2, 2 * f)), ([conv_w_grad(dcw0), conv_w_grad(dcw1)], (2, CONV_TAPS, 2 * f)),
    ]
    width = V7X_LANES * math.gcd(d // V7X_LANES, 2 * f // V7X_LANES)
    summed, offsets = _all_reduce_small("reduce_small", [[a.reshape(-1, width) for a in group] for group, _ in small])
    full = [summed[off:off + math.prod(shape) // width].reshape(shape) for off, (_, shape) in zip(offsets, small)]
    g_pre_mix, g_post_mix, g_pre_ffn, g_post_ffn, g_kv, g_vgain, g_bs, g_ws, g_cb, g_cwf = full
    cw_w = 2 * f // N_DEV
    g_vgain = lax.dynamic_slice_in_dim(g_vgain, me * (d // N_DEV), d // N_DEV, axis=1)
    g_cwf = lax.dynamic_slice_in_dim(g_cwf, me * cw_w, cw_w, axis=2)

    parts = [p_in, p_out, p_k, p_v, p_q, p_o, p_up0, p_up1, p_dn0, p_dn1]

    def small_update(name, g, w, m, v):
        return [g] + _adamw(name, g, w, m, v)

    def stacked(name, part0, part1, w, m, v):
        shape = (2, *part0.shape[1:])
        return [o.reshape(w.shape) for o in _sum_adamw(name, [part0, part1], w.reshape(shape), m.reshape(shape), v.reshape(shape))]

    def single(name, part, w, m, v):
        shape = (1, *part.shape[1:])
        return [o.reshape(w.shape) for o in _sum_adamw(name, [part], w.reshape(shape), m.reshape(shape), v.reshape(shape))]

    results = {
        "pre_mix_g": small_update("adam_pre_mix", g_pre_mix, pre_mix_g, m_pre_mix_g, v_pre_mix_g),
        "post_mix_g": small_update("adam_post_mix", g_post_mix, post_mix_g, m_post_mix_g, v_post_mix_g),
        "pre_ffn_g": small_update("adam_pre_ffn", g_pre_ffn, pre_ffn_g, m_pre_ffn_g, v_pre_ffn_g),
        "post_ffn_g": small_update("adam_post_ffn", g_post_ffn, post_ffn_g, m_post_ffn_g, v_post_ffn_g),
        "a_w_in": single("adam_a_w_in", parts[0], a_w_in, m_a_w_in, v_a_w_in),
        "a_v_norm_g": small_update("adam_a_v_norm", g_vgain, a_v_norm_g, m_a_v_norm_g, v_a_v_norm_g),
        "a_w_spatial": small_update("adam_a_w_spatial", g_ws, a_w_spatial, m_a_w_spatial, v_a_w_spatial),
        "a_b_spatial": small_update("adam_a_b_spatial", g_bs, a_b_spatial, m_a_b_spatial, v_a_b_spatial),
        "a_w_out": single("adam_a_w_out", parts[1], a_w_out, m_a_w_out, v_a_w_out),
        "kv_norm_g": small_update("adam_kv_norm", g_kv, kv_norm_g, m_kv_norm_g, v_kv_norm_g),
        "w_k": single("adam_w_k", parts[2], w_k, m_w_k, v_w_k),
        "w_v": single("adam_w_v", parts[3], w_v, m_w_v, v_w_v),
        "b_w_q": single("adam_b_w_q", parts[4], b_w_q, m_b_w_q, v_b_w_q),
        "b_w_o": single("adam_b_w_o", parts[5], b_w_o, m_b_w_o, v_b_w_o),
        "ffn_w_up": stacked("adam_ffn_w_up", parts[6], parts[7], ffn_w_up, m_ffn_w_up, v_ffn_w_up),
        "ffn_conv_w": small_update("adam_ffn_conv_w", g_cwf, ffn_conv_w, m_ffn_conv_w, v_ffn_conv_w),
        "ffn_conv_b": small_update("adam_ffn_conv_b", g_cb, ffn_conv_b, m_ffn_conv_b, v_ffn_conv_b),
        "ffn_w_down": stacked("adam_ffn_w_down", parts[8], parts[9], ffn_w_down, m_ffn_w_down, v_ffn_w_down),
    }
    order = ["pre_mix_g", "post_mix_g", "pre_ffn_g", "post_ffn_g", "a_w_in", "a_v_norm_g", "a_w_spatial", "a_b_spatial",
             "a_w_out", "kv_norm_g", "w_k", "w_v", "b_w_q", "b_w_o", "ffn_w_up", "ffn_conv_w", "ffn_conv_b", "ffn_w_down"]
    outs = [loss, grad_x.reshape(x.shape)]
    for idx in range(4):
        outs += [results[n][idx] for n in order]
    return tuple(outs)
```

```python
import functools
import math
from typing import NamedTuple, Optional

import jax
import jax.numpy as jnp
from jax import lax
from jax.experimental import pallas as pl
from jax.experimental.pallas import tpu as pltpu

F32 = jnp.float32
BF = jnp.bfloat16
MESH = pl.DeviceIdType.MESH

N_DEV = 8
NORM_EPS = 1e-6
GROUP = 128
CONV_TAPS = 3
ADAM_LR, ADAM_B1, ADAM_B2, ADAM_EPS, ADAM_WD, ADAM_STEP = 0.001, 0.9, 0.999, 1e-08, 0.01, 10
EXP_FLOOR = -104.0

V7X_LANES = 128
V7X_VMEM_BYTES = 64 * 1024 * 1024
_MIB = 1024 * 1024

_NN = (((1,), (0,)), ((), ()))
_NT = (((1,), (1,)), ((), ()))
_TN = (((0,), (0,)), ((), ()))


def _tile(n, pref):
    if n <= pref:
        return n
    t = (pref // V7X_LANES) * V7X_LANES
    while t > V7X_LANES and n % t:
        t -= V7X_LANES
    assert n % t == 0, (n, pref)
    return t


def _nbytes(shape, dtype):
    return math.prod(shape) * jnp.dtype(dtype).itemsize


def _params(sem=None, vmem=None):
    kw = {}
    if sem is not None:
        kw["dimension_semantics"] = sem
    if vmem is not None:
        kw["vmem_limit_bytes"] = int(min(max(vmem, 16 * _MIB), V7X_VMEM_BYTES - 8 * _MIB))
    return pltpu.CompilerParams(**kw)


def _place():
    x, y, c = lax.axis_index("x"), lax.axis_index("y"), lax.axis_index("c")
    return x, y, c, 4 * x + 2 * y + c


def _flip(x, y, c, k):
    return (1 - x if k & 4 else x, 1 - y if k & 2 else y, 1 - c if k & 1 else c)


class _Carry(NamedTuple):
    gather: bool
    src: jax.Array
    dst: Optional[jax.Array]
    lo: int
    hi: int


def _gather(src, lo=0, hi=None, dst=None):
    return _Carry(True, src, dst, lo, src.shape[0] if hi is None else hi)


def _scatter(src, lo=0, hi=None, dst=None):
    return _Carry(False, src, dst, lo, src.shape[1] if hi is None else hi)


def _carry_phases(carries, srcs, dsts, send_sems, recv_sems, local_sems):
    x, y, c, me = _place()
    here, sibling = (x, y, c), (x, y, 1 - c)
    chips = [(1 - x, y), (x, 1 - y), (1 - x, 1 - y)]

    def rows(u):
        return pl.ds(carries[u].lo, carries[u].hi - carries[u].lo)

    def block_copy(u, sem, block, to, from_src=False):
        slot = dsts[u].at[4 * block[0] + 2 * block[1] + block[2], rows(u)]
        return pltpu.make_async_remote_copy(
            src_ref=srcs[u].at[rows(u)] if from_src else slot, dst_ref=slot, send_sem=send_sems.at[u, sem],
            recv_sem=recv_sems.at[u, sem], device_id=to, device_id_type=MESH)

    def partial_copy(u, k):
        peer = _flip(x, y, c, k)
        return pltpu.make_async_remote_copy(
            src_ref=srcs[u].at[4 * peer[0] + 2 * peer[1] + peer[2], rows(u)], dst_ref=dsts[u].at[me, rows(u)],
            send_sem=send_sems.at[u, k - 1], recv_sem=recv_sems.at[u, k - 1], device_id=peer, device_id_type=MESH)

    def local_copy(u):
        src = srcs[u].at[rows(u)] if carries[u].gather else srcs[u].at[me, rows(u)]
        return pltpu.make_async_copy(src, dsts[u].at[me, rows(u)], local_sems.at[u])

    def first():
        for u, cr in enumerate(carries):
            local_copy(u).start()
            if cr.gather:
                block_copy(u, 0, here, sibling, from_src=True).start()
                for j, chip in enumerate(chips):
                    block_copy(u, 1 + j, here, (*chip, c), from_src=True).start()
            else:
                for k in range(1, N_DEV):
                    partial_copy(u, k).start()

    def middle():
        for u, cr in enumerate(carries):
            if cr.gather:
                for j, chip in enumerate(chips):
                    block_copy(u, 1 + j, (*chip, c), here).wait_recv()
                    block_copy(u, 4 + j, (*chip, c), sibling).start()

    def last():
        for u, cr in enumerate(carries):
            if cr.gather:
                block_copy(u, 0, sibling, here).wait_recv()
                for j, chip in enumerate(chips):
                    block_copy(u, 4 + j, (*chip, 1 - c), here).wait_recv()
                block_copy(u, 0, here, sibling, from_src=True).wait_send()
                for j, chip in enumerate(chips):
                    block_copy(u, 1 + j, here, (*chip, c), from_src=True).wait_send()
                    block_copy(u, 4 + j, (*chip, c), sibling).wait_send()
            else:
                for k in range(1, N_DEV):
                    partial_copy(u, k).wait()
            local_copy(u).wait()

    return first, middle, last


def _call(body, *, name, grid, in_specs, out_specs, out_shape, operands, scratch=(), sem=None, vmem=None,
          carries=(), middle_at=0.6):
    if not carries:
        return pl.pallas_call(
            body, name=name, grid=grid, in_specs=in_specs, out_specs=out_specs, out_shape=out_shape,
            scratch_shapes=list(scratch), compiler_params=_params(sem, vmem))(*operands)
    n_in, n_out, n_scr, nc = len(in_specs), len(out_specs), len(scratch), len(carries)
    given = [u for u, cr in enumerate(carries) if cr.dst is not None]
    steps = math.prod(grid)
    middle_step = min(steps - 1, int(steps * middle_at))

    def wrapped(*refs):
        ins, srcs = refs[:n_in], refs[n_in:n_in + nc]
        at = n_in + nc + len(given)
        outs, dsts = refs[at:at + n_out], refs[at + n_out:at + n_out + nc]
        at += n_out + nc
        scr, (send_sems, recv_sems, local_sems) = refs[at:at + n_scr], refs[at + n_scr:]
        first, middle, last = _carry_phases(carries, srcs, dsts, send_sems, recv_sems, local_sems)
        step = 0
        for axis, size in enumerate(grid):
            step = step * size + pl.program_id(axis)
        pl.when(step == 0)(first)
        body(*ins, *outs, *scr)
        pl.when(step == middle_step)(middle)
        pl.when(step == steps - 1)(last)

    any_spec = pl.BlockSpec(memory_space=pl.ANY)
    dst_shapes = [jax.ShapeDtypeStruct((N_DEV, *cr.src.shape) if cr.gather else cr.src.shape, cr.src.dtype) for cr in carries]
    return pl.pallas_call(
        wrapped, name=name, grid=grid, in_specs=list(in_specs) + [any_spec] * (nc + len(given)),
        out_specs=list(out_specs) + [any_spec] * nc, out_shape=list(out_shape) + dst_shapes,
        input_output_aliases={n_in + nc + g: n_out + u for g, u in enumerate(given)},
        scratch_shapes=list(scratch) + [pltpu.SemaphoreType.DMA((nc, 7)), pltpu.SemaphoreType.DMA((nc, 7)),
                                        pltpu.SemaphoreType.DMA((nc,))],
        compiler_params=_params(("arbitrary",) * len(grid), vmem),
    )(*operands, *[cr.src for cr in carries], *[carries[u].dst for u in given])


def _mm(name, a, b, *, dims, grid, a_blk, a_map, b_blk, b_map, o_blk, o_map, out_shape, out_dtype,
        add=None, add_blk=None, add_map=None, carries=(), b_slabs=1):
    nk = grid[2]
    assert add is None or nk == 1
    acc_shape = tuple(d for d in o_blk if d is not None)
    in_place = out_dtype == F32

    def body(*refs):
        if add is None:
            a_ref, b_ref, o_ref = refs[:3]
            c_ref, scr = None, refs[3:]
        else:
            a_ref, b_ref, c_ref, o_ref = refs[:4]
            scr = refs[4:]
        if b_slabs == 1:
            part = lax.dot_general(a_ref[...], b_ref[...], dims, preferred_element_type=F32)
        else:
            cw = b_ref.shape[2]
            part = sum(lax.dot_general(a_ref[:, s * cw:(s + 1) * cw], b_ref[s], dims, preferred_element_type=F32)
                       for s in range(b_slabs))
        if c_ref is not None:
            part = part + c_ref[...].astype(F32)
        if nk == 1:
            o_ref[...] = part.astype(o_ref.dtype)
            return
        acc = o_ref if in_place else scr[0]
        k = pl.program_id(2)

        @pl.when(k == 0)
        def _():
            acc[...] = part

        @pl.when(k > 0 if in_place else jnp.logical_and(k > 0, k < nk - 1))
        def _():
            acc[...] += part

        if not in_place:
            @pl.when(k == nk - 1)
            def _():
                o_ref[...] = (acc[...] + part).astype(o_ref.dtype)

    in_specs = [pl.BlockSpec(a_blk, a_map), pl.BlockSpec(b_blk, b_map)]
    operands = [a, b]
    scratch = [pltpu.VMEM(acc_shape, F32)] if nk > 1 and not in_place else []
    vmem = 2 * (_nbytes(acc_shape, out_dtype) + _nbytes([d for d in a_blk if d], a.dtype)
                + _nbytes([d for d in b_blk if d], b.dtype)) + (2 + len(scratch)) * _nbytes(acc_shape, F32)
    if add is not None:
        in_specs.append(pl.BlockSpec(add_blk, add_map))
        operands.append(add)
        vmem += 2 * _nbytes(acc_shape, add.dtype)
    out, *dsts = _call(
        body, name=name, grid=grid, in_specs=in_specs, out_specs=[pl.BlockSpec(o_blk, o_map)],
        out_shape=[jax.ShapeDtypeStruct(out_shape, out_dtype)], operands=operands, scratch=scratch,
        sem=("parallel", "parallel", "arbitrary"), vmem=vmem + 8 * _MIB, carries=carries)
    return (out, dsts) if carries else out


def _mm_nn(name, x, w, out_dtype, carries=()):
    t, kd = x.shape
    n = w.shape[1]
    tm, tn, tk = _tile(t, 1024), _tile(n, 1024), _tile(kd, 1536 if kd > 2048 else 2048)
    return _mm(name, x, w, dims=_NN, grid=(t // tm, n // tn, kd // tk),
               a_blk=(tm, tk), a_map=lambda i, j, k: (i, k), b_blk=(tk, tn), b_map=lambda i, j, k: (k, j),
               o_blk=(tm, tn), o_map=lambda i, j, k: (i, j), out_shape=(t, n), out_dtype=out_dtype, carries=carries)


def _mm_nn_blk(name, x, g, out_dtype, halves=False, carries=()):
    t, kd = x.shape
    cw = g.shape[2]
    tm = _tile(t, 1024)
    if halves:
        o_blk, o_map, out_shape = (None, tm, cw), (lambda i, j, k: (j // 4, i, j % 4)), (2, t, 4 * cw)
    else:
        o_blk, o_map, out_shape = (tm, cw), (lambda i, j, k: (i, j)), (t, N_DEV * cw)
    return _mm(name, x, g, dims=_NN, grid=(t // tm, N_DEV, 1),
               a_blk=(tm, kd), a_map=lambda i, j, k: (i, 0), b_blk=(None, kd, cw), b_map=lambda i, j, k: (j, 0, 0),
               o_blk=o_blk, o_map=o_map, out_shape=out_shape, out_dtype=out_dtype, carries=carries)


def _mm_nt(name, dy, w, out_dtype, add=None, carries=()):
    t, n = dy.shape
    kd = w.shape[0]
    tm, tn = _tile(t, 1024), _tile(kd, 512)
    kw = {}
    if add is not None:
        kw = dict(add=add, add_blk=(tm, tn), add_map=lambda i, j, k: (i, j))
    return _mm(name, dy, w, dims=_NT, grid=(t // tm, kd // tn, 1),
               a_blk=(tm, n), a_map=lambda i, j, k: (i, 0), b_blk=(tn, n), b_map=lambda i, j, k: (j, 0),
               o_blk=(tm, tn), o_map=lambda i, j, k: (i, j), out_shape=(t, kd), out_dtype=out_dtype, carries=carries, **kw)


def _mm_nt_blk(name, dy, g, out_dtype, halves=False, carries=()):
    kd, cw = g.shape[1], g.shape[2]
    t = dy.shape[1] if halves else dy.shape[0]
    tm = _tile(t, 512)
    slabs = 2 if cw > 512 else 4
    nk = N_DEV // slabs
    if halves:
        a_blk, a_map = (None, tm, slabs * cw), (lambda i, j, k: (k // (nk // 2), i, k % (nk // 2)))
    else:
        a_blk, a_map = (tm, slabs * cw), (lambda i, j, k: (i, k))
    return _mm(name, dy, g, dims=_NT, grid=(t // tm, 1, nk), b_slabs=slabs,
               a_blk=a_blk, a_map=a_map, b_blk=(slabs, kd, cw), b_map=lambda i, j, k: (k, 0, 0),
               o_blk=(tm, kd), o_map=lambda i, j, k: (i, 0), out_shape=(t, kd), out_dtype=out_dtype, carries=carries)


def _mm_tn(name, x, dy, carries=()):
    t, kd = x.shape
    n = dy.shape[1]
    tmx, tk = _tile(kd, 1024), _tile(t, 2048)
    return _mm(name, x, dy, dims=_TN, grid=(kd // tmx, 1, t // tk),
               a_blk=(tk, tmx), a_map=lambda i, j, k: (k, i), b_blk=(tk, n), b_map=lambda i, j, k: (k, 0),
               o_blk=(tmx, n), o_map=lambda i, j, k: (i, 0), out_shape=(kd, n), out_dtype=BF, carries=carries)


def _mm_tn_blk(name, x, dy, halves=False, carries=()):
    t, kd = x.shape
    cw = dy.shape[2] // 4 if halves else dy.shape[1] // N_DEV
    tmx, tk = _tile(kd, 2048 if cw <= 512 else 1024), _tile(t, 2048)
    if halves:
        b_blk, b_map = (None, tk, cw), (lambda i, j, k: (j // 4, k, j % 4))
    else:
        b_blk, b_map = (tk, cw), (lambda i, j, k: (k, j))
    return _mm(name, x, dy, dims=_TN, grid=(kd // tmx, N_DEV, t // tk),
               a_blk=(tk, tmx), a_map=lambda i, j, k: (k, i), b_blk=b_blk, b_map=b_map,
               o_blk=(None, tmx, cw), o_map=lambda i, j, k: (j, i, 0), out_shape=(N_DEV, kd, cw), out_dtype=BF,
               carries=carries)


def _rms(x, g):
    r = lax.rsqrt(jnp.mean(x * x, axis=-1, keepdims=True) + NORM_EPS)
    return x * r * g


def _rms_bwd_math(x, g, dy):
    d = x.shape[-1]
    r = lax.rsqrt(jnp.mean(x * x, axis=-1, keepdims=True) + NORM_EPS)
    xh = x * r
    u = dy * g
    dx = r * u - xh * (jnp.sum(xh * u, axis=-1, keepdims=True) * (r / d))
    return dx, jnp.sum(dy * xh, axis=0, keepdims=True)


def _row_specs(tr, d, n):
    return [pl.BlockSpec((tr, d), lambda i: (i, 0)) for _ in range(n)]


def _vec_specs(d, n):
    return [pl.BlockSpec((1, d), lambda i: (0, 0)) for _ in range(n)]


def _norms_fwd(name, h, gains):
    t, d = h.shape
    tr, ng = _tile(t, 256), len(gains)

    def body(h_ref, *refs):
        x = h_ref[...]
        for g_ref, o_ref in zip(refs[:ng], refs[ng:]):
            o_ref[...] = _rms(x, g_ref[...]).astype(BF)

    return pl.pallas_call(
        body, name=name, grid=(t // tr,), in_specs=_row_specs(tr, d, 1) + _vec_specs(d, ng),
        out_specs=_row_specs(tr, d, ng), out_shape=[jax.ShapeDtypeStruct((t, d), BF)] * ng,
        compiler_params=_params(("parallel",)),
    )(h, *[g.reshape(1, d) for g in gains])


def _resid_norms(name, h, m, g_post, gains, carries=()):
    t, d = h.shape
    tr, ng = _tile(t, 256), len(gains)

    def body(h_ref, m_ref, gp_ref, *refs):
        hn = h_ref[...] + _rms(m_ref[...], gp_ref[...])
        refs[ng][...] = hn
        for g_ref, o_ref in zip(refs[:ng], refs[ng + 1:]):
            o_ref[...] = _rms(hn, g_ref[...]).astype(BF)

    return _call(
        body, name=name, grid=(t // tr,), in_specs=_row_specs(tr, d, 2) + _vec_specs(d, 1 + ng),
        out_specs=_row_specs(tr, d, 1 + ng),
        out_shape=[jax.ShapeDtypeStruct((t, d), F32)] + [jax.ShapeDtypeStruct((t, d), BF)] * ng,
        operands=(h, m, g_post.reshape(1, d), *[g.reshape(1, d) for g in gains]), sem=("parallel",), carries=carries)


def _resid_loss(name, h, m, g_post, target):
    t, d = h.shape
    tr = _tile(t, 256)

    def body(h_ref, m_ref, gp_ref, t_ref, dy_ref, loss_ref):
        diff = h_ref[...] + _rms(m_ref[...], gp_ref[...]) - t_ref[...]
        dy_ref[...] = diff * (1.0 / d)

        @pl.when(pl.program_id(0) == 0)
        def _():
            loss_ref[...] = jnp.zeros_like(loss_ref)

        per_row = jnp.sum(diff * diff, axis=-1, keepdims=True) * (1.0 / d)
        loss_ref[...] += 0.5 * jnp.sum(per_row, axis=0, keepdims=True)

    return pl.pallas_call(
        body, name=name, grid=(t // tr,),
        in_specs=_row_specs(tr, d, 2) + _vec_specs(d, 1) + _row_specs(tr, d, 1),
        out_specs=[pl.BlockSpec((tr, d), lambda i: (i, 0)), pl.BlockSpec((1, 1), lambda i: (0, 0))],
        out_shape=[jax.ShapeDtypeStruct((t, d), F32), jax.ShapeDtypeStruct((1, 1), F32)],
        compiler_params=_params(("arbitrary",)),
    )(h, m, g_post.reshape(1, d), target)


def _norm_bwd(name, x, g, dy, res=None, out_dtype=F32, carries=()):
    t, d = x.shape
    tr = _tile(t, 256)
    has_res = res is not None
    gains, dys = (g, dy) if isinstance(g, (list, tuple)) else ([g], [dy])
    n = len(gains)

    def body(x_ref, *refs):
        dy_refs, g_refs, rest = refs[:n], refs[n:2 * n], refs[2 * n:]
        dx_ref, dg_refs = rest[int(has_res)], rest[int(has_res) + 1:]
        xv = x_ref[...].astype(F32)
        dx = rest[0][...] if has_res else 0.0
        first = pl.program_id(0) == 0
        for dy_ref, g_ref, dg_ref in zip(dy_refs, g_refs, dg_refs):
            dx_one, dg = _rms_bwd_math(xv, g_ref[...], dy_ref[...].astype(F32))
            dx = dx + dx_one

            @pl.when(first)
            def _(dg_ref=dg_ref):
                dg_ref[...] = jnp.zeros_like(dg_ref)

            dg_ref[...] += dg
        dx_ref[...] = dx.astype(dx_ref.dtype)

    ops = [x, *dys, *[gain.reshape(1, d) for gain in gains]] + ([res] if has_res else [])
    return _call(
        body, name=name, grid=(t // tr,),
        in_specs=_row_specs(tr, d, 1 + n) + _vec_specs(d, n) + _row_specs(tr, d, int(has_res)),
        out_specs=[pl.BlockSpec((tr, d), lambda i: (i, 0))] + [pl.BlockSpec((1, d), lambda i: (0, 0))] * n,
        out_shape=[jax.ShapeDtypeStruct((t, d), out_dtype)] + [jax.ShapeDtypeStruct((1, d), F32)] * n,
        operands=ops, sem=("arbitrary",), carries=carries)


_GELU_C = math.sqrt(2.0 / math.pi)
_GELU_A = 0.044715


def _gelu(x):
    return 0.5 * x * (1.0 + jnp.tanh(_GELU_C * (x + _GELU_A * x * x * x)))


def _gelu_and_grad(x):
    th = jnp.tanh(_GELU_C * (x + _GELU_A * x * x * x))
    grad = 0.5 * (1.0 + th) + 0.5 * x * (1.0 - th * th) * (_GELU_C * (1.0 + 3.0 * _GELU_A * x * x))
    return 0.5 * x * (1.0 + th), grad


def _causal(n):
    return lax.broadcasted_iota(jnp.int32, (n, n), 1) <= lax.broadcasted_iota(jnp.int32, (n, n), 0)


def _sgu_fwd(name, p, v_gain, w_s, b_st, carries=()):
    t, da2 = p.shape
    da = da2 // 2
    ng = da // GROUP

    def body(p_ref, vg_ref, ws_ref, bst_ref, o_ref):
        keep = _causal(GROUP)
        for g in range(ng):
            lo = g * GROUP
            u = _gelu(p_ref[:, lo:lo + GROUP])
            vn = _rms(_gelu(p_ref[:, da + lo:da + lo + GROUP]), vg_ref[:, lo:lo + GROUP])
            w = jnp.where(keep, ws_ref[g], 0.0).astype(BF)
            mixed = jnp.dot(w, vn.astype(BF), preferred_element_type=F32) + bst_ref[:, g:g + 1]
            o_ref[:, lo:lo + GROUP] = (u * mixed).astype(BF)

    return _call(
        body, name=name, grid=(t // GROUP,),
        in_specs=[pl.BlockSpec((GROUP, da2), lambda i: (i, 0)), pl.BlockSpec((1, da), lambda i: (0, 0)),
                  pl.BlockSpec((ng, GROUP, GROUP), lambda i: (0, 0, 0)), pl.BlockSpec((GROUP, ng), lambda i: (0, 0))],
        out_specs=[pl.BlockSpec((GROUP, da), lambda i: (i, 0))], out_shape=[jax.ShapeDtypeStruct((t, da), BF)],
        operands=(p, v_gain, w_s, b_st), sem=("parallel",), carries=carries)


def _sgu_bwd(name, p, v_gain, w_s, w_st, b_st, dout, carries=()):
    t, da2 = p.shape
    da = da2 // 2
    ng = da // GROUP

    def body(p_ref, vg_ref, ws_ref, wst_ref, bst_ref, do_ref, dp_ref, dvg_ref, dws_ref, dbst_ref):
        @pl.when(pl.program_id(0) == 0)
        def _():
            dvg_ref[...] = jnp.zeros_like(dvg_ref)
            dws_ref[...] = jnp.zeros_like(dws_ref)
            dbst_ref[...] = jnp.zeros_like(dbst_ref)

        keep = _causal(GROUP)
        keep_t = lax.broadcasted_iota(jnp.int32, (GROUP, GROUP), 0) <= lax.broadcasted_iota(jnp.int32, (GROUP, GROUP), 1)
        for g in range(ng):
            lo = g * GROUP
            u, du = _gelu_and_grad(p_ref[:, lo:lo + GROUP])
            v, dv_act = _gelu_and_grad(p_ref[:, da + lo:da + lo + GROUP])
            gain = vg_ref[:, lo:lo + GROUP]
            r = lax.rsqrt(jnp.mean(v * v, axis=-1, keepdims=True) + NORM_EPS)
            vh = v * r
            vnb = (vh * gain).astype(BF)
            w = jnp.where(keep, ws_ref[g], 0.0).astype(BF)
            wt = jnp.where(keep_t, wst_ref[g], 0.0).astype(BF)
            mixed = jnp.dot(w, vnb, preferred_element_type=F32) + bst_ref[:, g:g + 1]
            dout_g = do_ref[:, lo:lo + GROUP].astype(F32)
            dmixed = dout_g * u
            dmb = dmixed.astype(BF)
            dbst_ref[:, g:g + 1] += jnp.sum(dmixed, axis=1, keepdims=True)
            dws_ref[g] += jnp.where(keep, lax.dot_general(dmb, vnb, _NT, preferred_element_type=F32), 0.0)
            dvn = jnp.dot(wt, dmb, preferred_element_type=F32)
            dvg_ref[:, lo:lo + GROUP] += jnp.sum(dvn * vh, axis=0, keepdims=True)
            dvh = dvn * gain
            dv = r * dvh - vh * (jnp.sum(vh * dvh, axis=-1, keepdims=True) * (r / GROUP))
            dp_ref[:, lo:lo + GROUP] = (dout_g * mixed * du).astype(BF)
            dp_ref[:, da + lo:da + lo + GROUP] = (dv * dv_act).astype(BF)

    full = lambda *shape: pl.BlockSpec(shape, lambda i: (0,) * len(shape))
    outs = _call(
        body, name=name, grid=(t // GROUP,),
        in_specs=[pl.BlockSpec((GROUP, da2), lambda i: (i, 0)), full(1, da), full(ng, GROUP, GROUP), full(ng, GROUP, GROUP),
                  full(GROUP, ng), pl.BlockSpec((GROUP, da), lambda i: (i, 0))],
        out_specs=[pl.BlockSpec((GROUP, da2), lambda i: (i, 0)), full(1, da), full(ng, GROUP, GROUP), full(GROUP, ng)],
        out_shape=[jax.ShapeDtypeStruct((t, da2), BF), jax.ShapeDtypeStruct((1, da), F32),
                   jax.ShapeDtypeStruct((ng, GROUP, GROUP), F32), jax.ShapeDtypeStruct((GROUP, ng), F32)],
        operands=(p, v_gain, w_s, w_st, b_st, dout), sem=("arbitrary",), carries=carries)
    return outs[:4], outs[4:]


_CONV_ROWS = 256
_CONV_COLS = 128


def _shift_down(x, prev, k):
    top = pltpu.roll(jnp.concatenate([prev, x[:8]], axis=0), k, 0)[8:16]
    return jnp.concatenate([top, pltpu.roll(x, k, 0)[8:]], axis=0)


def _conv_taps(a_ref, half, r0, first):
    rows = _rows(a_ref)
    x = a_ref[half, pl.ds(r0, rows), :]
    if first:
        prev = jnp.zeros((8, x.shape[1]), F32)
        return x, _shift_down(x, prev, 1), _shift_down(x, prev, 2)
    return x, a_ref[half, pl.ds(r0 - 1, rows), :], a_ref[half, pl.ds(r0 - 2, rows), :]


def _rows(a_ref):
    return min(_CONV_ROWS, a_ref.shape[1])


def _conv_fwd(name, a3, cw, cb, carries=()):
    _, t, f = a3.shape
    tc, rows = _CONV_COLS, min(_CONV_ROWS, t)

    def body(a_ref, cw_ref, cb_ref, y_ref):
        def chunk(r0, first):
            c = []
            for half in range(2):
                x, x1, x2 = _conv_taps(a_ref, half, r0, first)
                w = cw_ref[half]
                c.append(cb_ref[half] + (w[0:1] * x2 + w[1:2] * x1 + w[2:3] * x))
            y_ref[pl.ds(r0, rows), :] = (c[0] * jax.nn.sigmoid(c[0]) * c[1]).astype(BF)

        chunk(0, True)

        @pl.loop(1, t // rows)
        def _(r):
            chunk(pl.multiple_of(r * rows, rows), False)

    col = lambda *lead: pl.BlockSpec((*lead, tc), lambda j: (0,) * len(lead) + (j,))
    y, *dsts = _call(
        body, name=name, grid=(f // tc,), in_specs=[col(2, t), col(2, CONV_TAPS), col(2, 1)],
        out_specs=[col(t)], out_shape=[jax.ShapeDtypeStruct((t, f), BF)], operands=(a3, cw, cb),
        sem=("parallel",), vmem=40 * _MIB, carries=carries)
    return y, dsts


def _conv_bwd(name, a3, cw, cb, dy, carries=()):
    _, t, f = a3.shape
    tc, rows = _CONV_COLS, min(_CONV_ROWS, t)
    n_steps = t // rows

    def body(a_ref, cw_ref, cb_ref, dy_ref, da_ref, dcw_ref, dcb_ref, dc_ref):
        dcw_ref[...] = jnp.zeros_like(dcw_ref)
        dcb_ref[...] = jnp.zeros_like(dcb_ref)

        def chunk(r0, first, nxt):
            taps, c = [], []
            for half in range(2):
                x, x1, x2 = _conv_taps(a_ref, half, r0, first)
                w = cw_ref[half]
                taps.append((x2, x1, x))
                c.append(cb_ref[half] + (w[0:1] * x2 + w[1:2] * x1 + w[2:3] * x))
            gate, val = c
            sg = jax.nn.sigmoid(gate)
            dyv = dy_ref[pl.ds(r0, rows), :].astype(F32)
            dcs = (dyv * val * (sg * (1.0 + gate * (1.0 - sg))), dyv * (gate * sg))
            new_nxt = []
            for half in range(2):
                dc, w = dcs[half], cw_ref[half]
                dcb_ref[half] += jnp.sum(dc, axis=0, keepdims=True)
                for tap in range(CONV_TAPS):
                    dcw_ref[half, tap:tap + 1, :] += jnp.sum(dc * taps[half][tap], axis=0, keepdims=True)
                dc_ref[half, 0:rows, :] = dc
                dc_ref[half, rows:rows + 8, :] = nxt[half]
                da = w[2:3] * dc + w[1:2] * dc_ref[half, 1:rows + 1, :] + w[0:1] * dc_ref[half, 2:rows + 2, :]
                da_ref[half, pl.ds(r0, rows), :] = da.astype(BF)
                new_nxt.append(dc[:8])
            return tuple(new_nxt)

        zeros = jnp.zeros((8, tc), F32)
        nxt = lax.fori_loop(0, n_steps - 1, lambda s, nxt: chunk(pl.multiple_of((n_steps - 1 - s) * rows, rows), False, nxt),
                            (zeros, zeros))
        chunk(0, True, nxt)

    col = lambda *lead: pl.BlockSpec((*lead, tc), lambda j: (0,) * len(lead) + (j,))
    outs = _call(
        body, name=name, grid=(f // tc,), in_specs=[col(2, t), col(2, CONV_TAPS), col(2, 1), col(t)],
        out_specs=[col(2, t), col(2, CONV_TAPS), col(2, 1)],
        out_shape=[jax.ShapeDtypeStruct((2, t, f), BF), jax.ShapeDtypeStruct((2, CONV_TAPS, f), F32),
                   jax.ShapeDtypeStruct((2, 1, f), F32)],
        operands=(a3, cw, cb, dy), scratch=[pltpu.VMEM((2, rows + 8, tc), F32)], sem=("parallel",), vmem=48 * _MIB,
        carries=carries)
    return outs[:3], outs[3:]


_ATT_BLOCK = 256


def _split_dot(x, tri):
    hi = x.astype(BF)
    lo = (x - hi.astype(F32)).astype(BF)
    return jnp.dot(hi, tri, preferred_element_type=F32) + jnp.dot(lo, tri, preferred_element_type=F32)


_ATT_HEADS_FWD = 4
_ATT_HEADS_BWD = 4


def _logits(qb, kb, diagonal):
    z = lax.dot_general(qb, kb, _NT, preferred_element_type=F32) * (1.0 / math.sqrt(GROUP))
    lb = jnp.minimum(z, 0.0) - jnp.log(1.0 + jnp.exp(-jnp.abs(z)))
    if not diagonal:
        return lb, lb - z, None
    mask = lax.broadcasted_iota(jnp.int32, z.shape, 1) < lax.broadcasted_iota(jnp.int32, z.shape, 0)
    return lb, jnp.where(mask, lb - z, 0.0), mask


def _head(ref, g, rows=slice(None)):
    return ref[rows, g * GROUP:(g + 1) * GROUP]


def _attn_fwd(name, q, k, v, carries=()):
    t, hd = q.shape
    blk = min(_ATT_BLOCK, t)
    heads = min(_ATT_HEADS_FWD, hd // GROUP)
    gs = range(heads)

    def body(q_ref, k_ref, v_ref, o_ref, rest_ref, first_ref):
        hg, i = pl.program_id(0), pl.program_id(1)
        ri = lax.broadcasted_iota(jnp.int32, (blk, blk), 0)
        ci = lax.broadcasted_iota(jnp.int32, (blk, blk), 1)
        tri = (ri >= ci).astype(BF)

        def tile(j, state, diagonal):
            keys = pl.ds(pl.multiple_of(j * blk, blk), blk)
            logit = [_logits(_head(q_ref, g), _head(k_ref, g, keys), diagonal) for g in gs]
            incl = [_split_dot(logit[g][1], tri) for g in gs]
            a = [jnp.exp(logit[g][0] + (incl[g] - logit[g][1] + state[g][0])) for g in gs]
            if diagonal:
                a = [jnp.where(logit[g][2], a[g], 0.0) for g in gs]
            out = [state[g][1] + jnp.dot(a[g].astype(BF), _head(v_ref, g, keys), preferred_element_type=F32) for g in gs]
            return tuple((state[g][0] + incl[g][:, 0:1], out[g]) for g in gs)

        state = tile(i, ((jnp.zeros((blk, 1), F32), jnp.zeros((blk, GROUP), F32)),) * heads, True)

        def more(carry):
            j, state = carry
            live = functools.reduce(jnp.maximum, [jnp.max(right) for right, _ in state])
            return jnp.logical_and(j >= 0, live > EXP_FLOOR)

        j, state = lax.while_loop(more, lambda c: (c[0] - 1, tile(c[0], c[1], False)), (i - 1, state))
        for g, (right, acc) in enumerate(state):
            o_ref[:, g * GROUP:(g + 1) * GROUP] = acc.astype(BF)
            rest_ref[:, g * GROUP:(g + 1) * GROUP] = jnp.broadcast_to(right, (blk, GROUP))
        first_ref[hg, i] = (j + 1).astype(F32)

    qspec = pl.BlockSpec((blk, heads * GROUP), lambda h, i: (i, h))
    kvspec = pl.BlockSpec((t, heads * GROUP), lambda h, i: (0, h), pipeline_mode=pl.Buffered(1))
    groups = hd // (heads * GROUP)
    outs = _call(
        body, name=name, grid=(groups, t // blk), in_specs=[qspec, kvspec, kvspec],
        out_specs=[qspec, qspec, pl.BlockSpec(memory_space=pltpu.SMEM)],
        out_shape=[jax.ShapeDtypeStruct((t, hd), BF), jax.ShapeDtypeStruct((t, hd), F32),
                   jax.ShapeDtypeStruct((groups, t // blk), F32)],
        operands=(q, k, v), sem=("arbitrary", "arbitrary"), vmem=40 * _MIB, carries=carries, middle_at=0.75)
    return outs[:3], outs[3:]


def _attn_bwd(name, q, k, v, rest, first, do, carries=()):
    t, hd = q.shape
    blk = min(_ATT_BLOCK, t)
    nq = t // blk
    scale = 1.0 / math.sqrt(GROUP)
    heads = min(_ATT_HEADS_BWD, hd // GROUP)
    per_first = min(_ATT_HEADS_FWD, hd // GROUP) // heads
    gs = range(heads)

    def body(first_ref, q_ref, k_ref, v_ref, rest_ref, do_ref, dq_ref, dk_ref, dv_ref, dk_acc, dv_acc):
        hg, i = pl.program_id(0), pl.program_id(1)

        @pl.when(i == 0)
        def _():
            dk_acc[...] = jnp.zeros_like(dk_acc)
            dv_acc[...] = jnp.zeros_like(dv_acc)

        ri = lax.broadcasted_iota(jnp.int32, (blk, blk), 0)
        ci = lax.broadcasted_iota(jnp.int32, (blk, blk), 1)
        tri = (ri <= ci).astype(BF)

        def tile(j, state, diagonal):
            keys = pl.ds(pl.multiple_of(j * blk, blk), blk)
            qs, dos = [_head(q_ref, g) for g in gs], [_head(do_ref, g) for g in gs]
            kb, vb = [_head(k_ref, g, keys) for g in gs], [_head(v_ref, g, keys) for g in gs]
            logit = [_logits(qs[g], kb[g], diagonal) for g in gs]
            pre = [_split_dot(logit[g][1], tri) for g in gs]
            a = [jnp.exp(logit[g][0] + (rest_ref[:, g * GROUP:g * GROUP + 1] - state[g][0] - pre[g])) for g in gs]
            if diagonal:
                a = [jnp.where(logit[g][2], a[g], 0.0) for g in gs]
            gw = [a[g] * lax.dot_general(dos[g], vb[g], _NT, preferred_element_type=F32) for g in gs]
            gpre = [_split_dot(gw[g], tri) for g in gs]
            dz = []
            for g in gs:
                beta = jnp.exp(logit[g][0])
                d = (gw[g] * (1.0 - beta) - (state[g][1] + gpre[g] - gw[g]) * beta) * scale
                dz.append((jnp.where(logit[g][2], d, 0.0) if diagonal else d).astype(BF))
            for g in gs:
                dk_acc[keys, g * GROUP:(g + 1) * GROUP] += lax.dot_general(dz[g], qs[g], _TN, preferred_element_type=F32)
                dv_acc[keys, g * GROUP:(g + 1) * GROUP] += lax.dot_general(a[g].astype(BF), dos[g], _TN,
                                                                           preferred_element_type=F32)
            return tuple((state[g][0] + pre[g][:, blk - 1:blk], state[g][1] + gpre[g][:, blk - 1:blk],
                          state[g][2] + jnp.dot(dz[g], kb[g], preferred_element_type=F32)) for g in gs)

        zero = jnp.zeros((blk, 1), F32)
        first_block = jnp.clip(first_ref[hg // per_first, i].astype(jnp.int32), 0, i)
        state = lax.fori_loop(first_block, i, lambda j, c: tile(j, c, False),
                              ((zero, zero, jnp.zeros((blk, GROUP), F32)),) * heads)
        for g, (_, _, dq) in enumerate(tile(i, state, True)):
            dq_ref[:, g * GROUP:(g + 1) * GROUP] = dq.astype(BF)

        @pl.when(i == nq - 1)
        def _():
            dk_ref[...] = dk_acc[...].astype(BF)
            dv_ref[...] = dv_acc[...].astype(BF)

    qspec = pl.BlockSpec((blk, heads * GROUP), lambda h, i: (i, h))
    kvspec = pl.BlockSpec((t, heads * GROUP), lambda h, i: (0, h), pipeline_mode=pl.Buffered(1))
    outs = _call(
        body, name=name, grid=(hd // (heads * GROUP), nq),
        in_specs=[pl.BlockSpec(memory_space=pltpu.SMEM), qspec, kvspec, kvspec, qspec, qspec],
        out_specs=[qspec, kvspec, kvspec], out_shape=[jax.ShapeDtypeStruct((t, hd), BF)] * 3,
        operands=(first, q, k, v, rest, do),
        scratch=[pltpu.VMEM((t, heads * GROUP), F32), pltpu.VMEM((t, heads * GROUP), F32)],
        sem=("arbitrary", "arbitrary"), vmem=48 * _MIB, carries=carries)
    return outs[:3], outs[3:]


def _adamw_math(w, g, m, v):
    m = ADAM_B1 * m + (1.0 - ADAM_B1) * g
    v = ADAM_B2 * v + (1.0 - ADAM_B2) * (g * g)
    m_hat = m / (1.0 - ADAM_B1 ** ADAM_STEP)
    v_hat = v / (1.0 - ADAM_B2 ** ADAM_STEP)
    return -ADAM_LR * (m_hat / (jnp.sqrt(v_hat) + ADAM_EPS) + ADAM_WD * w), m, v


def _sum_adamw(name, parts, w, m, v):
    layers, r, c = w.shape
    budget = 512 * 1024 // layers
    tr = r if r * c <= budget else _tile_rows(r, max(8, (budget // c) // 8 * 8))
    n = r // tr

    def body(*refs):
        p_refs, (w_ref, m_ref, v_ref, g_ref, d_ref, nm_ref, nv_ref) = refs[:layers], refs[layers:]
        for layer, p_ref in enumerate(p_refs):
            @pl.when(pl.program_id(0) == layer)
            def _(p_ref=p_ref):
                g = p_ref[0].astype(F32)
                for dev in range(1, N_DEV):
                    g = g + p_ref[dev].astype(F32)
                g_ref[...] = g
                d_ref[...], nm_ref[...], nv_ref[...] = _adamw_math(w_ref[...], g, m_ref[...], v_ref[...])

    def part_spec(layer):
        return pl.BlockSpec((N_DEV, tr, c), lambda l, i: (0, jnp.where(l < layer, 0, jnp.where(l == layer, i, n - 1)), 0))

    row = pl.BlockSpec((None, tr, c), lambda l, i: (l, i, 0))
    return pl.pallas_call(
        body, name=name, grid=(layers, n), in_specs=[part_spec(layer) for layer in range(layers)] + [row] * 3,
        out_specs=[row] * 4, out_shape=[jax.ShapeDtypeStruct((layers, r, c), F32)] * 4,
        compiler_params=_params(("arbitrary", "arbitrary"), 40 * _MIB),
    )(*parts, w, m, v)


def _tile_rows(r, pref):
    t = min(r, pref)
    while r % t or t % 8:
        t -= 1
    return t


def _exchange(name, carries):
    return _call(lambda: None, name=name, grid=(1,), in_specs=[], out_specs=[], out_shape=[], operands=(), carries=carries)


def _all_reduce_small(name, groups):
    c = groups[0][0].shape[1]
    parts, offsets, starts, r = [], [], [], 0
    for group in groups:
        starts.append(r)
        for p in group:
            parts.append(p)
            offsets.append(r)
            r += p.shape[0]
        r = -(-r // 8) * 8
    n = len(parts)

    def body(*refs):
        out_ref, slots, send_sems, recv_sems = refs[n:]
        x, y, c_, me = _place()
        slots[me] = jnp.zeros((r, c), F32)
        for p_ref, off in zip(refs[:n], offsets):
            slots[me, off:off + p_ref.shape[0], :] = p_ref[...]

        here, sibling = (x, y, c_), (x, y, 1 - c_)
        chips = [(1 - x, y), (x, 1 - y), (1 - x, 1 - y)]

        def copy(sem, block, to):
            slot = slots.at[4 * block[0] + 2 * block[1] + block[2]]
            return pltpu.make_async_remote_copy(
                src_ref=slot, dst_ref=slot, send_sem=send_sems.at[sem], recv_sem=recv_sems.at[sem], device_id=to,
                device_id_type=MESH)

        sent = [copy(0, here, sibling)] + [copy(1 + j, here, (*chip, c_)) for j, chip in enumerate(chips)]
        for cp in sent:
            cp.start()
        for j, chip in enumerate(chips):
            copy(1 + j, (*chip, c_), here).wait_recv()
            sent.append(copy(4 + j, (*chip, c_), sibling))
            sent[-1].start()
        copy(0, sibling, here).wait_recv()
        for j, chip in enumerate(chips):
            copy(4 + j, (*chip, 1 - c_), here).wait_recv()
        for cp in sent:
            cp.wait_send()
        total = slots[0]
        for dev in range(1, N_DEV):
            total = total + slots[dev]
        out_ref[...] = total

    vm = pl.BlockSpec(memory_space=pltpu.VMEM)
    summed = pl.pallas_call(
        body, name=name, in_specs=[vm] * n, out_specs=vm, out_shape=jax.ShapeDtypeStruct((r, c), F32),
        scratch_shapes=[pltpu.VMEM((N_DEV, r, c), F32), pltpu.SemaphoreType.DMA((7,)), pltpu.SemaphoreType.DMA((7,))],
        compiler_params=_params(None, (N_DEV + 6) * r * c * 4 + 8 * _MIB),
    )(*parts)
    return summed, starts


def _adamw(name, g, w, m, v):
    cols = w.shape[-1]
    flat = lambda a: a.reshape(-1, cols)

    def body(g_ref, w_ref, m_ref, v_ref, d_ref, nm_ref, nv_ref):
        d_ref[...], nm_ref[...], nv_ref[...] = _adamw_math(w_ref[...], g_ref[...], m_ref[...], v_ref[...])

    outs = pl.pallas_call(body, name=name, out_shape=[jax.ShapeDtypeStruct(flat(w).shape, F32)] * 3)(
        flat(g), flat(w), flat(m), flat(v))
    return [o.reshape(w.shape) for o in outs]


def kernel(x, pre_mix_g, post_mix_g, pre_ffn_g, post_ffn_g, a_w_in, a_v_norm_g, a_w_spatial, a_b_spatial, a_w_out, kv_norm_g, w_k, w_v, b_w_q, b_w_o, ffn_w_up, ffn_conv_w, ffn_conv_b, ffn_w_down, loss_target, m_pre_mix_g, m_post_mix_g, m_pre_ffn_g, m_post_ffn_g, m_a_w_in, m_a_v_norm_g, m_a_w_spatial, m_a_b_spatial, m_a_w_out, m_kv_norm_g, m_w_k, m_w_v, m_b_w_q, m_b_w_o, m_ffn_w_up, m_ffn_conv_w, m_ffn_conv_b, m_ffn_w_down, v_pre_mix_g, v_post_mix_g, v_pre_ffn_g, v_post_ffn_g, v_a_w_in, v_a_v_norm_g, v_a_w_spatial, v_a_b_spatial, v_a_w_out, v_kv_norm_g, v_w_k, v_w_v, v_b_w_q, v_b_w_o, v_ffn_w_up, v_ffn_conv_w, v_ffn_conv_b, v_ffn_w_down):
    t, d = x.shape[1], x.shape[2]
    f = ffn_w_down.shape[1] * N_DEV
    ng = d // GROUP
    me = 4 * lax.axis_index("x") + 2 * lax.axis_index("y") + lax.axis_index("c")
    x2, target = x.reshape(t, d), loss_target.reshape(t, d)

    g_in, g_cw, g_vg = _exchange("gather_first", [
        _gather(a_w_in[0].astype(BF)), _gather(ffn_conv_w.reshape(2 * CONV_TAPS, -1)), _gather(a_v_norm_g)])
    up0 = ffn_w_up[0].astype(BF)
    cw_full = jnp.transpose(g_cw.reshape(N_DEV, 2, CONV_TAPS, -1), (1, 2, 0, 3)).reshape(2, CONV_TAPS, 2, f)
    cw_l = [jnp.transpose(cw_full[l], (1, 0, 2)) for l in range(2)]
    cb_l = [ffn_conv_b[l].reshape(2, 1, f) for l in range(2)]
    vg_full = g_vg.reshape(1, d)
    w_s = a_w_spatial[0]
    w_st = jnp.swapaxes(w_s, 1, 2)
    b_st = a_b_spatial[0].T

    (hn0,) = _norms_fwd("pre_mix0", x2, [pre_mix_g[0]])
    p0, (g_out, g_up0) = _mm_nn_blk("sgu_in", hn0, g_in, F32, carries=[
        _gather(a_w_out[0].astype(BF)), _gather(up0, 0, d // 4)])
    w_out_f = g_out.reshape(d, d)
    sg, g_up0 = _sgu_fwd("sgu", p0, vg_full, w_s, b_st, carries=[_gather(up0, d // 4, d // 2, dst=g_up0)])
    mix0, (g_up0,) = _mm_nn("sgu_out", sg, w_out_f, F32, carries=[_gather(up0, d // 2, 3 * d // 4, dst=g_up0)])
    h1, fn0, g_up0 = _resid_norms("post_mix0", x2, mix0, post_mix_g[0], [pre_ffn_g[0]],
                                  carries=[_gather(up0, 3 * d // 4, d, dst=g_up0)])
    a3_0, (g_dn0,) = _mm_nn_blk("ffn0_up", fn0, g_up0, F32, halves=True, carries=[_gather(ffn_w_down[0].astype(BF))])
    y0, (g_q, g_k) = _conv_fwd("ffn0_conv", a3_0, cw_l[0], cb_l[0],
                               carries=[_gather(b_w_q[0].astype(BF)), _gather(w_k.astype(BF))])
    f0, (g_v,) = _mm_nn("ffn0_down", y0, g_dn0.reshape(f, d), F32, carries=[_gather(w_v.astype(BF))])
    w_q_f, w_k_f, w_v_f = g_q.reshape(d, d), g_k.reshape(d, d), g_v.reshape(d, d)
    h2, hn1, kvn = _resid_norms("post_ffn0", h1, f0, post_ffn_g[0], [pre_mix_g[1], kv_norm_g])
    up1 = ffn_w_up[1].astype(BF)
    e = d // 8
    q, (g_up1,) = _mm_nn("attn_q", hn1, w_q_f, BF, carries=[_gather(up1, 0, e)])
    kk, (g_up1,) = _mm_nn("attn_k", kvn, w_k_f, BF, carries=[_gather(up1, e, 2 * e, dst=g_up1)])
    vv, (g_up1,) = _mm_nn("attn_v", kvn, w_v_f, BF, carries=[_gather(up1, 2 * e, 3 * e, dst=g_up1)])
    (att, rest, first), (g_o, g_up1) = _attn_fwd(
        "attn", q, kk, vv, carries=[_gather(b_w_o[0].astype(BF)), _gather(up1, 3 * e, 6 * e, dst=g_up1)])
    w_o_f = g_o.reshape(d, d)
    mix1, (g_up1,) = _mm_nn("attn_o", att, w_o_f, F32, carries=[_gather(up1, 6 * e, 7 * e, dst=g_up1)])
    h3, fn1, g_up1 = _resid_norms("post_mix1", h2, mix1, post_mix_g[1], [pre_ffn_g[1]],
                                  carries=[_gather(up1, 7 * e, d, dst=g_up1)])
    a3_1, (g_dn1,) = _mm_nn_blk("ffn1_up", fn1, g_up1, F32, halves=True, carries=[_gather(ffn_w_down[1].astype(BF))])
    y1, _ = _conv_fwd("ffn1_conv", a3_1, cw_l[1], cb_l[1])
    f1 = _mm_nn("ffn1_down", y1, g_dn1.reshape(f, d), F32)
    g_up = (g_up0, g_up1)
    w_dn_f = (g_dn0.reshape(f, d), g_dn1.reshape(f, d))
    dh, loss_part = _resid_loss("loss", h3, f1, post_ffn_g[1], target)
    loss = lax.psum(loss_part[0, 0], ("x", "y", "c"))

    def blocks(dw):
        return dw.reshape(N_DEV, -1, d)

    def split(result, carries):
        return result if carries else (result, [])

    def ffn_bwd(l, dh_out, h_in, fn, a3, yv, fo, with_dw=(), with_dx=(), with_up=()):
        dfo, d_post = _norm_bwd(f"post_ffn{l}_bwd", fo, post_ffn_g[l], dh_out, out_dtype=BF)
        dw_dn, sent_dw = split(_mm_tn(f"ffn{l}_down_dw", yv, dfo, carries=with_dw), with_dw)
        dy, sent_dx = split(_mm_nt(f"ffn{l}_down_dx", dfo, w_dn_f[l], BF, carries=with_dx), with_dx)
        dw_dn = blocks(dw_dn)
        cut = dw_dn.shape[1] * 5 // 8 // 16 * 16
        (da3, dcw, dcb), (p_dn,) = _conv_bwd(f"ffn{l}_conv_bwd", a3, cw_l[l], cb_l[l], dy, carries=[_scatter(dw_dn, 0, cut)])
        dw_up, (p_dn, *sent_up) = _mm_tn_blk(f"ffn{l}_up_dw", fn, da3, halves=True,
                                             carries=[_scatter(dw_dn, cut, dst=p_dn), *with_up])
        dfn, (p_up,) = _mm_nt_blk(f"ffn{l}_up_dx", da3, g_up[l], F32, halves=True, carries=[_scatter(dw_up, 0, d // 2)])
        dh_in, d_pre, p_up = _norm_bwd(f"pre_ffn{l}_bwd", h_in, pre_ffn_g[l], dfn, res=dh_out,
                                       carries=[_scatter(dw_up, d // 2, 9 * d // 16, dst=p_up)])
        return dh_in, d_post, d_pre, dcw, dcb, p_dn, dw_up, p_up, list(sent_dw) + list(sent_dx) + list(sent_up)

    dh3, d_post_ffn1, d_pre_ffn1, dcw1, dcb1, p_dn1, dw_up1, p_up1, _ = ffn_bwd(1, dh, h3, fn1, a3_1, y1, f1)
    dmix1, d_post_mix1 = _norm_bwd("post_mix1_bwd", mix1, post_mix_g[1], dh3, out_dtype=BF)
    dw_o = _mm_tn("attn_o_dw", att, dmix1)
    datt = _mm_nt("attn_o_dx", dmix1, w_o_f, BF)
    (dq, dk, dv), (p_up1,) = _attn_bwd(
        "attn_bwd", q, kk, vv, rest, first, datt, carries=[_scatter(dw_up1, 9 * d // 16, d, dst=p_up1)])
    dw_o, qr = blocks(dw_o), d // 32
    dw_q, (p_o,) = _mm_tn("attn_q_dw", hn1, dq, carries=[_scatter(dw_o, 0, qr)])
    dw_k, (p_o,) = _mm_tn("attn_k_dw", kvn, dk, carries=[_scatter(dw_o, qr, 2 * qr, dst=p_o)])
    dw_v, (p_o,) = _mm_tn("attn_v_dw", kvn, dv, carries=[_scatter(dw_o, 2 * qr, 3 * qr, dst=p_o)])
    dhn1, (p_o,) = _mm_nt("attn_q_dx", dq, w_q_f, F32, carries=[_scatter(dw_o, 3 * qr, 4 * qr, dst=p_o)])
    dkvn = _mm_nt("attn_v_dx", dv, w_v_f, F32, add=_mm_nt("attn_k_dx", dk, w_k_f, F32))
    dh2, d_pre_mix1, d_kv = _norm_bwd("pre_mix1_kv_bwd", h2, [pre_mix_g[1], kv_norm_g], [dhn1, dkvn], res=dh3)
    dh1, d_post_ffn0, d_pre_ffn0, dcw0, dcb0, p_dn0, dw_up0, p_up0, (p_q, p_k, p_v) = ffn_bwd(
        0, dh2, h1, fn0, a3_0, y0, f0, with_dw=[_scatter(blocks(dw_q))], with_dx=[_scatter(blocks(dw_k))],
        with_up=[_scatter(blocks(dw_v))])
    dmix0, d_post_mix0 = _norm_bwd("post_mix0_bwd", mix0, post_mix_g[0], dh1, out_dtype=BF)
    dw_out, (p_up0,) = _mm_tn("sgu_out_dw", sg, dmix0, carries=[_scatter(dw_up0, 9 * d // 16, 11 * d // 16, dst=p_up0)])
    dsg = _mm_nt("sgu_out_dx", dmix0, w_out_f, BF)
    (dp0, d_vg, d_ws, d_bst), (p_up0,) = _sgu_bwd(
        "sgu_bwd", p0, vg_full, w_s, w_st, b_st, dsg, carries=[_scatter(dw_up0, 11 * d // 16, 15 * d // 16, dst=p_up0)])
    dw_in, (p_out,) = _mm_tn_blk("sgu_in_dw", hn0, dp0, carries=[_scatter(blocks(dw_out))])
    dhn0, (p_in,) = _mm_nt_blk("sgu_in_dx", dp0, g_in, F32, carries=[_scatter(dw_in, 0, 5 * d // 8)])
    grad_x, d_pre_mix0, p_in = _norm_bwd("pre_mix0_bwd", x2, pre_mix_g[0], dhn0, res=dh1,
                                         carries=[_scatter(dw_in, 5 * d // 8, 7 * d // 8, dst=p_in)])
    p_up0, p_in = _exchange("scatter_last", [_scatter(dw_up0, 15 * d // 16, d, dst=p_up0),
                                             _scatter(dw_in, 7 * d // 8, d, dst=p_in)])

    def conv_w_grad(dcw):
        return jnp.transpose(dcw, (1, 0, 2)).reshape(CONV_TAPS, 2 * f)

    small = [
        ([d_pre_mix0, d_pre_mix1], (2, d)), ([d_post_mix0, d_post_mix1], (2, d)),
        ([d_pre_ffn0, d_pre_ffn1], (2, d)), ([d_post_ffn0, d_post_ffn1], (2, d)),
        ([d_kv], (d,)), ([d_vg], (1, d)), ([d_bst.T], (1, ng, GROUP)), ([d_ws], (1, ng, GROUP, GROUP)),
        ([dcb0, dcb1], (2, 2 * f)), ([conv_w_grad(dcw0), conv_w_grad(dcw1)], (2, CONV_TAPS, 2 * f)),
    ]
    width = V7X_LANES * math.gcd(d // V7X_LANES, 2 * f // V7X_LANES)
    summed, offsets = _all_reduce_small("reduce_small", [[a.reshape(-1, width) for a in group] for group, _ in small])
    full = [summed[off:off + math.prod(shape) // width].reshape(shape) for off, (_, shape) in zip(offsets, small)]
    g_pre_mix, g_post_mix, g_pre_ffn, g_post_ffn, g_kv, g_vgain, g_bs, g_ws, g_cb, g_cwf = full
    cw_w = 2 * f // N_DEV
    g_vgain = lax.dynamic_slice_in_dim(g_vgain, me * (d // N_DEV), d // N_DEV, axis=1)
    g_cwf = lax.dynamic_slice_in_dim(g_cwf, me * cw_w, cw_w, axis=2)

    parts = [p_in, p_out, p_k, p_v, p_q, p_o, p_up0, p_up1, p_dn0, p_dn1]

    def small_update(name, g, w, m, v):
        return [g] + _adamw(name, g, w, m, v)

    def stacked(name, part0, part1, w, m, v):
        shape = (2, *part0.shape[1:])
        return [o.reshape(w.shape) for o in _sum_adamw(name, [part0, part1], w.reshape(shape), m.reshape(shape), v.reshape(shape))]

    def single(name, part, w, m, v):
        shape = (1, *part.shape[1:])
        return [o.reshape(w.shape) for o in _sum_adamw(name, [part], w.reshape(shape), m.reshape(shape), v.reshape(shape))]

    results = {
        "pre_mix_g": small_update("adam_pre_mix", g_pre_mix, pre_mix_g, m_pre_mix_g, v_pre_mix_g),
        "post_mix_g": small_update("adam_post_mix", g_post_mix, post_mix_g, m_post_mix_g, v_post_mix_g),
        "pre_ffn_g": small_update("adam_pre_ffn", g_pre_ffn, pre_ffn_g, m_pre_ffn_g, v_pre_ffn_g),
        "post_ffn_g": small_update("adam_post_ffn", g_post_ffn, post_ffn_g, m_post_ffn_g, v_post_ffn_g),
        "a_w_in": single("adam_a_w_in", parts[0], a_w_in, m_a_w_in, v_a_w_in),
        "a_v_norm_g": small_update("adam_a_v_norm", g_vgain, a_v_norm_g, m_a_v_norm_g, v_a_v_norm_g),
        "a_w_spatial": small_update("adam_a_w_spatial", g_ws, a_w_spatial, m_a_w_spatial, v_a_w_spatial),
        "a_b_spatial": small_update("adam_a_b_spatial", g_bs, a_b_spatial, m_a_b_spatial, v_a_b_spatial),
        "a_w_out": single("adam_a_w_out", parts[1], a_w_out, m_a_w_out, v_a_w_out),
        "kv_norm_g": small_update("adam_kv_norm", g_kv, kv_norm_g, m_kv_norm_g, v_kv_norm_g),
        "w_k": single("adam_w_k", parts[2], w_k, m_w_k, v_w_k),
        "w_v": single("adam_w_v", parts[3], w_v, m_w_v, v_w_v),
        "b_w_q": single("adam_b_w_q", parts[4], b_w_q, m_b_w_q, v_b_w_q),
        "b_w_o": single("adam_b_w_o", parts[5], b_w_o, m_b_w_o, v_b_w_o),
        "ffn_w_up": stacked("adam_ffn_w_up", parts[6], parts[7], ffn_w_up, m_ffn_w_up, v_ffn_w_up),
        "ffn_conv_w": small_update("adam_ffn_conv_w", g_cwf, ffn_conv_w, m_ffn_conv_w, v_ffn_conv_w),
        "ffn_conv_b": small_update("adam_ffn_conv_b", g_cb, ffn_conv_b, m_ffn_conv_b, v_ffn_conv_b),
        "ffn_w_down": stacked("adam_ffn_w_down", parts[8], parts[9], ffn_w_down, m_ffn_w_down, v_ffn_w_down),
    }
    order = ["pre_mix_g", "post_mix_g", "pre_ffn_g", "post_ffn_g", "a_w_in", "a_v_norm_g", "a_w_spatial", "a_b_spatial",
             "a_w_out", "kv_norm_g", "w_k", "w_v", "b_w_q", "b_w_o", "ffn_w_up", "ffn_conv_w", "ffn_conv_b", "ffn_w_down"]
    outs = [loss, grad_x.reshape(x.shape)]
    for idx in range(4):
        outs += [results[n][idx] for n in order]
    return tuple(outs)
```

```python
import functools
import math
from typing import NamedTuple, Optional

import jax
import jax.numpy as jnp
from jax import lax
from jax.experimental import pallas as pl
from jax.experimental.pallas import tpu as pltpu

F32 = jnp.float32
BF = jnp.bfloat16
MESH = pl.DeviceIdType.MESH

N_DEV = 8
NORM_EPS = 1e-6
GROUP = 128
CONV_TAPS = 3
ADAM_LR, ADAM_B1, ADAM_B2, ADAM_EPS, ADAM_WD, ADAM_STEP = 0.001, 0.9, 0.999, 1e-08, 0.01, 10
EXP_FLOOR = -104.0

V7X_LANES = 128
V7X_VMEM_BYTES = 64 * 1024 * 1024
_MIB = 1024 * 1024

_NN = (((1,), (0,)), ((), ()))
_NT = (((1,), (1,)), ((), ()))
_TN = (((0,), (0,)), ((), ()))


def _tile(n, pref):
    if n <= pref:
        return n
    t = (pref // V7X_LANES) * V7X_LANES
    while t > V7X_LANES and n % t:
        t -= V7X_LANES
    assert n % t == 0, (n, pref)
    return t


def _nbytes(shape, dtype):
    return math.prod(shape) * jnp.dtype(dtype).itemsize


def _params(sem=None, vmem=None):
    kw = {}
    if sem is not None:
        kw["dimension_semantics"] = sem
    if vmem is not None:
        kw["vmem_limit_bytes"] = int(min(max(vmem, 16 * _MIB), V7X_VMEM_BYTES - 8 * _MIB))
    return pltpu.CompilerParams(**kw)


def _place():
    x, y, c = lax.axis_index("x"), lax.axis_index("y"), lax.axis_index("c")
    return x, y, c, 4 * x + 2 * y + c


def _flip(x, y, c, k):
    return (1 - x if k & 4 else x, 1 - y if k & 2 else y, 1 - c if k & 1 else c)


class _Carry(NamedTuple):
    gather: bool
    src: jax.Array
    dst: Optional[jax.Array]
    lo: int
    hi: int


def _gather(src, lo=0, hi=None, dst=None):
    return _Carry(True, src, dst, lo, src.shape[0] if hi is None else hi)


def _scatter(src, lo=0, hi=None, dst=None):
    return _Carry(False, src, dst, lo, src.shape[1] if hi is None else hi)


def _carry_phases(carries, srcs, dsts, send_sems, recv_sems, local_sems):
    x, y, c, me = _place()
    here, sibling = (x, y, c), (x, y, 1 - c)
    chips = [(1 - x, y), (x, 1 - y), (1 - x, 1 - y)]

    def rows(u):
        return pl.ds(carries[u].lo, carries[u].hi - carries[u].lo)

    def block_copy(u, sem, block, to, from_src=False):
        slot = dsts[u].at[4 * block[0] + 2 * block[1] + block[2], rows(u)]
        return pltpu.make_async_remote_copy(
            src_ref=srcs[u].at[rows(u)] if from_src else slot, dst_ref=slot, send_sem=send_sems.at[u, sem],
            recv_sem=recv_sems.at[u, sem], device_id=to, device_id_type=MESH)

    def partial_copy(u, k):
        peer = _flip(x, y, c, k)
        return pltpu.make_async_remote_copy(
            src_ref=srcs[u].at[4 * peer[0] + 2 * peer[1] + peer[2], rows(u)], dst_ref=dsts[u].at[me, rows(u)],
            send_sem=send_sems.at[u, k - 1], recv_sem=recv_sems.at[u, k - 1], device_id=peer, device_id_type=MESH)

    def local_copy(u):
        src = srcs[u].at[rows(u)] if carries[u].gather else srcs[u].at[me, rows(u)]
        return pltpu.make_async_copy(src, dsts[u].at[me, rows(u)], local_sems.at[u])

    def first():
        for u, cr in enumerate(carries):
            local_copy(u).start()
            if cr.gather:
                block_copy(u, 0, here, sibling, from_src=True).start()
                for j, chip in enumerate(chips):
                    block_copy(u, 1 + j, here, (*chip, c), from_src=True).start()
            else:
                for k in range(1, N_DEV):
                    partial_copy(u, k).start()

    def middle():
        for u, cr in enumerate(carries):
            if cr.gather:
                for j, chip in enumerate(chips):
                    block_copy(u, 1 + j, (*chip, c), here).wait_recv()
                    block_copy(u, 4 + j, (*chip, c), sibling).start()

    def last():
        for u, cr in enumerate(carries):
            if cr.gather:
                block_copy(u, 0, sibling, here).wait_recv()
                for j, chip in enumerate(chips):
                    block_copy(u, 4 + j, (*chip, 1 - c), here).wait_recv()
                block_copy(u, 0, here, sibling, from_src=True).wait_send()
                for j, chip in enumerate(chips):
                    block_copy(u, 1 + j, here, (*chip, c), from_src=True).wait_send()
                    block_copy(u, 4 + j, (*chip, c), sibling).wait_send()
            else:
                for k in range(1, N_DEV):
                    partial_copy(u, k).wait()
            local_copy(u).wait()

    return first, middle, last


def _call(body, *, name, grid, in_specs, out_specs, out_shape, operands, scratch=(), sem=None, vmem=None,
          carries=(), middle_at=0.6):
    if not carries:
        return pl.pallas_call(
            body, name=name, grid=grid, in_specs=in_specs, out_specs=out_specs, out_shape=out_shape,
            scratch_shapes=list(scratch), compiler_params=_params(sem, vmem))(*operands)
    n_in, n_out, n_scr, nc = len(in_specs), len(out_specs), len(scratch), len(carries)
    given = [u for u, cr in enumerate(carries) if cr.dst is not None]
    steps = math.prod(grid)
    middle_step = min(steps - 1, int(steps * middle_at))

    def wrapped(*refs):
        ins, srcs = refs[:n_in], refs[n_in:n_in + nc]
        at = n_in + nc + len(given)
        outs, dsts = refs[at:at + n_out], refs[at + n_out:at + n_out + nc]
        at += n_out + nc
        scr, (send_sems, recv_sems, local_sems) = refs[at:at + n_scr], refs[at + n_scr:]
        first, middle, last = _carry_phases(carries, srcs, dsts, send_sems, recv_sems, local_sems)
        step = 0
        for axis, size in enumerate(grid):
            step = step * size + pl.program_id(axis)
        pl.when(step == 0)(first)
        body(*ins, *outs, *scr)
        pl.when(step == middle_step)(middle)
        pl.when(step == steps - 1)(last)

    any_spec = pl.BlockSpec(memory_space=pl.ANY)
    dst_shapes = [jax.ShapeDtypeStruct((N_DEV, *cr.src.shape) if cr.gather else cr.src.shape, cr.src.dtype) for cr in carries]
    return pl.pallas_call(
        wrapped, name=name, grid=grid, in_specs=list(in_specs) + [any_spec] * (nc + len(given)),
        out_specs=list(out_specs) + [any_spec] * nc, out_shape=list(out_shape) + dst_shapes,
        input_output_aliases={n_in + nc + g: n_out + u for g, u in enumerate(given)},
        scratch_shapes=list(scratch) + [pltpu.SemaphoreType.DMA((nc, 7)), pltpu.SemaphoreType.DMA((nc, 7)),
                                        pltpu.SemaphoreType.DMA((nc,))],
        compiler_params=_params(("arbitrary",) * len(grid), vmem),
    )(*operands, *[cr.src for cr in carries], *[carries[u].dst for u in given])


def _mm(name, a, b, *, dims, grid, a_blk, a_map, b_blk, b_map, o_blk, o_map, out_shape, out_dtype,
        add=None, add_blk=None, add_map=None, carries=(), b_slabs=1):
    nk = grid[2]
    assert add is None or nk == 1
    acc_shape = tuple(d for d in o_blk if d is not None)
    in_place = out_dtype == F32

    def body(*refs):
        if add is None:
            a_ref, b_ref, o_ref = refs[:3]
            c_ref, scr = None, refs[3:]
        else:
            a_ref, b_ref, c_ref, o_ref = refs[:4]
            scr = refs[4:]
        if b_slabs == 1:
            part = lax.dot_general(a_ref[...], b_ref[...], dims, preferred_element_type=F32)
        else:
            cw = b_ref.shape[2]
            part = sum(lax.dot_general(a_ref[:, s * cw:(s + 1) * cw], b_ref[s], dims, preferred_element_type=F32)
                       for s in range(b_slabs))
        if c_ref is not None:
            part = part + c_ref[...].astype(F32)
        if nk == 1:
            o_ref[...] = part.astype(o_ref.dtype)
            return
        acc = o_ref if in_place else scr[0]
        k = pl.program_id(2)

        @pl.when(k == 0)
        def _():
            acc[...] = part

        @pl.when(k > 0 if in_place else jnp.logical_and(k > 0, k < nk - 1))
        def _():
            acc[...] += part

        if not in_place:
            @pl.when(k == nk - 1)
            def _():
                o_ref[...] = (acc[...] + part).astype(o_ref.dtype)

    in_specs = [pl.BlockSpec(a_blk, a_map), pl.BlockSpec(b_blk, b_map)]
    operands = [a, b]
    scratch = [pltpu.VMEM(acc_shape, F32)] if nk > 1 and not in_place else []
    vmem = 2 * (_nbytes(acc_shape, out_dtype) + _nbytes([d for d in a_blk if d], a.dtype)
                + _nbytes([d for d in b_blk if d], b.dtype)) + (2 + len(scratch)) * _nbytes(acc_shape, F32)
    if add is not None:
        in_specs.append(pl.BlockSpec(add_blk, add_map))
        operands.append(add)
        vmem += 2 * _nbytes(acc_shape, add.dtype)
    out, *dsts = _call(
        body, name=name, grid=grid, in_specs=in_specs, out_specs=[pl.BlockSpec(o_blk, o_map)],
        out_shape=[jax.ShapeDtypeStruct(out_shape, out_dtype)], operands=operands, scratch=scratch,
        sem=("parallel", "parallel", "arbitrary"), vmem=vmem + 8 * _MIB, carries=carries)
    return (out, dsts) if carries else out


def _mm_nn(name, x, w, out_dtype, carries=()):
    t, kd = x.shape
    n = w.shape[1]
    tm, tn, tk = _tile(t, 1024), _tile(n, 1024), _tile(kd, 1536 if kd > 2048 else 2048)
    return _mm(name, x, w, dims=_NN, grid=(t // tm, n // tn, kd // tk),
               a_blk=(tm, tk), a_map=lambda i, j, k: (i, k), b_blk=(tk, tn), b_map=lambda i, j, k: (k, j),
               o_blk=(tm, tn), o_map=lambda i, j, k: (i, j), out_shape=(t, n), out_dtype=out_dtype, carries=carries)


def _mm_nn_blk(name, x, g, out_dtype, halves=False, carries=()):
    t, kd = x.shape
    cw = g.shape[2]
    tm = _tile(t, 1024)
    if halves:
        o_blk, o_map, out_shape = (None, tm, cw), (lambda i, j, k: (j // 4, i, j % 4)), (2, t, 4 * cw)
    else:
        o_blk, o_map, out_shape = (tm, cw), (lambda i, j, k: (i, j)), (t, N_DEV * cw)
    return _mm(name, x, g, dims=_NN, grid=(t // tm, N_DEV, 1),
               a_blk=(tm, kd), a_map=lambda i, j, k: (i, 0), b_blk=(None, kd, cw), b_map=lambda i, j, k: (j, 0, 0),
               o_blk=o_blk, o_map=o_map, out_shape=out_shape, out_dtype=out_dtype, carries=carries)


def _mm_nt(name, dy, w, out_dtype, add=None, carries=()):
    t, n = dy.shape
    kd = w.shape[0]
    tm, tn = _tile(t, 1024), _tile(kd, 512)
    kw = {}
    if add is not None:
        kw = dict(add=add, add_blk=(tm, tn), add_map=lambda i, j, k: (i, j))
    return _mm(name, dy, w, dims=_NT, grid=(t // tm, kd // tn, 1),
               a_blk=(tm, n), a_map=lambda i, j, k: (i, 0), b_blk=(tn, n), b_map=lambda i, j, k: (j, 0),
               o_blk=(tm, tn), o_map=lambda i, j, k: (i, j), out_shape=(t, kd), out_dtype=out_dtype, carries=carries, **kw)


def _mm_nt_blk(name, dy, g, out_dtype, halves=False, carries=()):
    kd, cw = g.shape[1], g.shape[2]
    t = dy.shape[1] if halves else dy.shape[0]
    tm = _tile(t, 512)
    slabs = 2 if cw > 512 else 4
    nk = N_DEV // slabs
    if halves:
        a_blk, a_map = (None, tm, slabs * cw), (lambda i, j, k: (k // (nk // 2), i, k % (nk // 2)))
    else:
        a_blk, a_map = (tm, slabs * cw), (lambda i, j, k: (i, k))
    return _mm(name, dy, g, dims=_NT, grid=(t // tm, 1, nk), b_slabs=slabs,
               a_blk=a_blk, a_map=a_map, b_blk=(slabs, kd, cw), b_map=lambda i, j, k: (k, 0, 0),
               o_blk=(tm, kd), o_map=lambda i, j, k: (i, 0), out_shape=(t, kd), out_dtype=out_dtype, carries=carries)


def _mm_tn(name, x, dy, carries=()):
    t, kd = x.shape
    n = dy.shape[1]
    tmx, tk = _tile(kd, 1024), _tile(t, 2048)
    return _mm(name, x, dy, dims=_TN, grid=(kd // tmx, 1, t // tk),
               a_blk=(tk, tmx), a_map=lambda i, j, k: (k, i), b_blk=(tk, n), b_map=lambda i, j, k: (k, 0),
               o_blk=(tmx, n), o_map=lambda i, j, k: (i, 0), out_shape=(kd, n), out_dtype=BF, carries=carries)


def _mm_tn_blk(name, x, dy, halves=False, carries=()):
    t, kd = x.shape
    cw = dy.shape[2] // 4 if halves else dy.shape[1] // N_DEV
    tmx, tk = _tile(kd, 2048 if cw <= 512 else 1024), _tile(t, 2048)
    if halves:
        b_blk, b_map = (None, tk, cw), (lambda i, j, k: (j // 4, k, j % 4))
    else:
        b_blk, b_map = (tk, cw), (lambda i, j, k: (k, j))
    return _mm(name, x, dy, dims=_TN, grid=(kd // tmx, N_DEV, t // tk),
               a_blk=(tk, tmx), a_map=lambda i, j, k: (k, i), b_blk=b_blk, b_map=b_map,
               o_blk=(None, tmx, cw), o_map=lambda i, j, k: (j, i, 0), out_shape=(N_DEV, kd, cw), out_dtype=BF,
               carries=carries)


def _rms(x, g):
    r = lax.rsqrt(jnp.mean(x * x, axis=-1, keepdims=True) + NORM_EPS)
    return x * r * g


def _rms_bwd_math(x, g, dy):
    d = x.shape[-1]
    r = lax.rsqrt(jnp.mean(x * x, axis=-1, keepdims=True) + NORM_EPS)
    xh = x * r
    u = dy * g
    dx = r * u - xh * (jnp.sum(xh * u, axis=-1, keepdims=True) * (r / d))
    return dx, jnp.sum(dy * xh, axis=0, keepdims=True)


def _row_specs(tr, d, n):
    return [pl.BlockSpec((tr, d), lambda i: (i, 0)) for _ in range(n)]


def _vec_specs(d, n):
    return [pl.BlockSpec((1, d), lambda i: (0, 0)) for _ in range(n)]


def _norms_fwd(name, h, gains, carries=()):
    t, d = h.shape
    tr, ng = _tile(t, 256), len(gains)

    def body(h_ref, *refs):
        x = h_ref[...]
        for g_ref, o_ref in zip(refs[:ng], refs[ng:]):
            o_ref[...] = _rms(x, g_ref[...]).astype(BF)

    return _call(
        body, name=name, grid=(t // tr,), in_specs=_row_specs(tr, d, 1) + _vec_specs(d, ng),
        out_specs=_row_specs(tr, d, ng), out_shape=[jax.ShapeDtypeStruct((t, d), BF)] * ng,
        operands=(h, *[g.reshape(1, d) for g in gains]), sem=("parallel",), carries=carries)


def _resid_norms(name, h, m, g_post, gains, carries=()):
    t, d = h.shape
    tr, ng = _tile(t, 256), len(gains)

    def body(h_ref, m_ref, gp_ref, *refs):
        hn = h_ref[...] + _rms(m_ref[...], gp_ref[...])
        refs[ng][...] = hn
        for g_ref, o_ref in zip(refs[:ng], refs[ng + 1:]):
            o_ref[...] = _rms(hn, g_ref[...]).astype(BF)

    return _call(
        body, name=name, grid=(t // tr,), in_specs=_row_specs(tr, d, 2) + _vec_specs(d, 1 + ng),
        out_specs=_row_specs(tr, d, 1 + ng),
        out_shape=[jax.ShapeDtypeStruct((t, d), F32)] + [jax.ShapeDtypeStruct((t, d), BF)] * ng,
        operands=(h, m, g_post.reshape(1, d), *[g.reshape(1, d) for g in gains]), sem=("parallel",), carries=carries)


def _resid_loss(name, h, m, g_post, target):
    t, d = h.shape
    tr = _tile(t, 256)

    def body(h_ref, m_ref, gp_ref, t_ref, dy_ref, loss_ref):
        diff = h_ref[...] + _rms(m_ref[...], gp_ref[...]) - t_ref[...]
        dy_ref[...] = diff * (1.0 / d)

        @pl.when(pl.program_id(0) == 0)
        def _():
            loss_ref[...] = jnp.zeros_like(loss_ref)

        per_row = jnp.sum(diff * diff, axis=-1, keepdims=True) * (1.0 / d)
        loss_ref[...] += 0.5 * jnp.sum(per_row, axis=0, keepdims=True)

    return pl.pallas_call(
        body, name=name, grid=(t // tr,),
        in_specs=_row_specs(tr, d, 2) + _vec_specs(d, 1) + _row_specs(tr, d, 1),
        out_specs=[pl.BlockSpec((tr, d), lambda i: (i, 0)), pl.BlockSpec((1, 1), lambda i: (0, 0))],
        out_shape=[jax.ShapeDtypeStruct((t, d), F32), jax.ShapeDtypeStruct((1, 1), F32)],
        compiler_params=_params(("arbitrary",)),
    )(h, m, g_post.reshape(1, d), target)


def _norm_bwd(name, x, g, dy, res=None, out_dtype=F32, carries=()):
    t, d = x.shape
    tr = _tile(t, 256)
    has_res = res is not None
    gains, dys = (g, dy) if isinstance(g, (list, tuple)) else ([g], [dy])
    n = len(gains)

    def body(x_ref, *refs):
        dy_refs, g_refs, rest = refs[:n], refs[n:2 * n], refs[2 * n:]
        dx_ref, dg_refs = rest[int(has_res)], rest[int(has_res) + 1:]
        xv = x_ref[...].astype(F32)
        dx = rest[0][...] if has_res else 0.0
        first = pl.program_id(0) == 0
        for dy_ref, g_ref, dg_ref in zip(dy_refs, g_refs, dg_refs):
            dx_one, dg = _rms_bwd_math(xv, g_ref[...], dy_ref[...].astype(F32))
            dx = dx + dx_one

            @pl.when(first)
            def _(dg_ref=dg_ref):
                dg_ref[...] = jnp.zeros_like(dg_ref)

            dg_ref[...] += dg
        dx_ref[...] = dx.astype(dx_ref.dtype)

    ops = [x, *dys, *[gain.reshape(1, d) for gain in gains]] + ([res] if has_res else [])
    return _call(
        body, name=name, grid=(t // tr,),
        in_specs=_row_specs(tr, d, 1 + n) + _vec_specs(d, n) + _row_specs(tr, d, int(has_res)),
        out_specs=[pl.BlockSpec((tr, d), lambda i: (i, 0))] + [pl.BlockSpec((1, d), lambda i: (0, 0))] * n,
        out_shape=[jax.ShapeDtypeStruct((t, d), out_dtype)] + [jax.ShapeDtypeStruct((1, d), F32)] * n,
        operands=ops, sem=("arbitrary",), carries=carries)


_GELU_C = math.sqrt(2.0 / math.pi)
_GELU_A = 0.044715


def _gelu(x):
    return 0.5 * x * (1.0 + jnp.tanh(_GELU_C * (x + _GELU_A * x * x * x)))


def _gelu_and_grad(x):
    th = jnp.tanh(_GELU_C * (x + _GELU_A * x * x * x))
    grad = 0.5 * (1.0 + th) + 0.5 * x * (1.0 - th * th) * (_GELU_C * (1.0 + 3.0 * _GELU_A * x * x))
    return 0.5 * x * (1.0 + th), grad


def _causal(n):
    return lax.broadcasted_iota(jnp.int32, (n, n), 1) <= lax.broadcasted_iota(jnp.int32, (n, n), 0)


def _sgu_fwd(name, p, v_gain, w_s, b_st, carries=()):
    t, da2 = p.shape
    da = da2 // 2
    ng = da // GROUP

    def body(p_ref, vg_ref, ws_ref, bst_ref, o_ref):
        keep = _causal(GROUP)
        for g in range(ng):
            lo = g * GROUP
            u = _gelu(p_ref[:, lo:lo + GROUP])
            vn = _rms(_gelu(p_ref[:, da + lo:da + lo + GROUP]), vg_ref[:, lo:lo + GROUP])
            w = jnp.where(keep, ws_ref[g], 0.0).astype(BF)
            mixed = jnp.dot(w, vn.astype(BF), preferred_element_type=F32) + bst_ref[:, g:g + 1]
            o_ref[:, lo:lo + GROUP] = (u * mixed).astype(BF)

    return _call(
        body, name=name, grid=(t // GROUP,),
        in_specs=[pl.BlockSpec((GROUP, da2), lambda i: (i, 0)), pl.BlockSpec((1, da), lambda i: (0, 0)),
                  pl.BlockSpec((ng, GROUP, GROUP), lambda i: (0, 0, 0)), pl.BlockSpec((GROUP, ng), lambda i: (0, 0))],
        out_specs=[pl.BlockSpec((GROUP, da), lambda i: (i, 0))], out_shape=[jax.ShapeDtypeStruct((t, da), BF)],
        operands=(p, v_gain, w_s, b_st), sem=("parallel",), carries=carries)


def _sgu_bwd(name, p, v_gain, w_s, w_st, b_st, dout, carries=()):
    t, da2 = p.shape
    da = da2 // 2
    ng = da // GROUP

    def body(p_ref, vg_ref, ws_ref, wst_ref, bst_ref, do_ref, dp_ref, dvg_ref, dws_ref, dbst_ref):
        @pl.when(pl.program_id(0) == 0)
        def _():
            dvg_ref[...] = jnp.zeros_like(dvg_ref)
            dws_ref[...] = jnp.zeros_like(dws_ref)
            dbst_ref[...] = jnp.zeros_like(dbst_ref)

        keep = _causal(GROUP)
        keep_t = lax.broadcasted_iota(jnp.int32, (GROUP, GROUP), 0) <= lax.broadcasted_iota(jnp.int32, (GROUP, GROUP), 1)
        for g in range(ng):
            lo = g * GROUP
            u, du = _gelu_and_grad(p_ref[:, lo:lo + GROUP])
            v, dv_act = _gelu_and_grad(p_ref[:, da + lo:da + lo + GROUP])
            gain = vg_ref[:, lo:lo + GROUP]
            r = lax.rsqrt(jnp.mean(v * v, axis=-1, keepdims=True) + NORM_EPS)
            vh = v * r
            vnb = (vh * gain).astype(BF)
            w = jnp.where(keep, ws_ref[g], 0.0).astype(BF)
            wt = jnp.where(keep_t, wst_ref[g], 0.0).astype(BF)
            mixed = jnp.dot(w, vnb, preferred_element_type=F32) + bst_ref[:, g:g + 1]
            dout_g = do_ref[:, lo:lo + GROUP].astype(F32)
            dmixed = dout_g * u
            dmb = dmixed.astype(BF)
            dbst_ref[:, g:g + 1] += jnp.sum(dmixed, axis=1, keepdims=True)
            dws_ref[g] += jnp.where(keep, lax.dot_general(dmb, vnb, _NT, preferred_element_type=F32), 0.0)
            dvn = jnp.dot(wt, dmb, preferred_element_type=F32)
            dvg_ref[:, lo:lo + GROUP] += jnp.sum(dvn * vh, axis=0, keepdims=True)
            dvh = dvn * gain
            dv = r * dvh - vh * (jnp.sum(vh * dvh, axis=-1, keepdims=True) * (r / GROUP))
            dp_ref[:, lo:lo + GROUP] = (dout_g * mixed * du).astype(BF)
            dp_ref[:, da + lo:da + lo + GROUP] = (dv * dv_act).astype(BF)

    full = lambda *shape: pl.BlockSpec(shape, lambda i: (0,) * len(shape))
    outs = _call(
        body, name=name, grid=(t // GROUP,),
        in_specs=[pl.BlockSpec((GROUP, da2), lambda i: (i, 0)), full(1, da), full(ng, GROUP, GROUP), full(ng, GROUP, GROUP),
                  full(GROUP, ng), pl.BlockSpec((GROUP, da), lambda i: (i, 0))],
        out_specs=[pl.BlockSpec((GROUP, da2), lambda i: (i, 0)), full(1, da), full(ng, GROUP, GROUP), full(GROUP, ng)],
        out_shape=[jax.ShapeDtypeStruct((t, da2), BF), jax.ShapeDtypeStruct((1, da), F32),
                   jax.ShapeDtypeStruct((ng, GROUP, GROUP), F32), jax.ShapeDtypeStruct((GROUP, ng), F32)],
        operands=(p, v_gain, w_s, w_st, b_st, dout), sem=("arbitrary",), carries=carries)
    return outs[:4], outs[4:]


_CONV_ROWS = 256
_CONV_COLS = 128


def _shift_down(x, prev, k):
    top = pltpu.roll(jnp.concatenate([prev, x[:8]], axis=0), k, 0)[8:16]
    return jnp.concatenate([top, pltpu.roll(x, k, 0)[8:]], axis=0)


def _conv_taps(a_ref, half, r0, first):
    rows = _rows(a_ref)
    x = a_ref[half, pl.ds(r0, rows), :]
    if first:
        prev = jnp.zeros((8, x.shape[1]), F32)
        return x, _shift_down(x, prev, 1), _shift_down(x, prev, 2)
    return x, a_ref[half, pl.ds(r0 - 1, rows), :], a_ref[half, pl.ds(r0 - 2, rows), :]


def _rows(a_ref):
    return min(_CONV_ROWS, a_ref.shape[1])


def _conv_fwd(name, a3, cw, cb, carries=()):
    _, t, f = a3.shape
    tc, rows = _CONV_COLS, min(_CONV_ROWS, t)

    def body(a_ref, cw_ref, cb_ref, y_ref):
        def chunk(r0, first):
            c = []
            for half in range(2):
                x, x1, x2 = _conv_taps(a_ref, half, r0, first)
                w = cw_ref[half]
                c.append(cb_ref[half] + (w[0:1] * x2 + w[1:2] * x1 + w[2:3] * x))
            y_ref[pl.ds(r0, rows), :] = (c[0] * jax.nn.sigmoid(c[0]) * c[1]).astype(BF)

        chunk(0, True)

        @pl.loop(1, t // rows)
        def _(r):
            chunk(pl.multiple_of(r * rows, rows), False)

    col = lambda *lead: pl.BlockSpec((*lead, tc), lambda j: (0,) * len(lead) + (j,))
    y, *dsts = _call(
        body, name=name, grid=(f // tc,), in_specs=[col(2, t), col(2, CONV_TAPS), col(2, 1)],
        out_specs=[col(t)], out_shape=[jax.ShapeDtypeStruct((t, f), BF)], operands=(a3, cw, cb),
        sem=("parallel",), vmem=40 * _MIB, carries=carries)
    return y, dsts


def _conv_bwd(name, a3, cw, cb, dy, carries=()):
    _, t, f = a3.shape
    tc, rows = _CONV_COLS, min(_CONV_ROWS, t)
    n_steps = t // rows

    def body(a_ref, cw_ref, cb_ref, dy_ref, da_ref, dcw_ref, dcb_ref, dc_ref):
        dcw_ref[...] = jnp.zeros_like(dcw_ref)
        dcb_ref[...] = jnp.zeros_like(dcb_ref)

        def chunk(r0, first, nxt):
            taps, c = [], []
            for half in range(2):
                x, x1, x2 = _conv_taps(a_ref, half, r0, first)
                w = cw_ref[half]
                taps.append((x2, x1, x))
                c.append(cb_ref[half] + (w[0:1] * x2 + w[1:2] * x1 + w[2:3] * x))
            gate, val = c
            sg = jax.nn.sigmoid(gate)
            dyv = dy_ref[pl.ds(r0, rows), :].astype(F32)
            dcs = (dyv * val * (sg * (1.0 + gate * (1.0 - sg))), dyv * (gate * sg))
            new_nxt = []
            for half in range(2):
                dc, w = dcs[half], cw_ref[half]
                dcb_ref[half] += jnp.sum(dc, axis=0, keepdims=True)
                for tap in range(CONV_TAPS):
                    dcw_ref[half, tap:tap + 1, :] += jnp.sum(dc * taps[half][tap], axis=0, keepdims=True)
                dc_ref[half, 0:rows, :] = dc
                dc_ref[half, rows:rows + 8, :] = nxt[half]
                da = w[2:3] * dc + w[1:2] * dc_ref[half, 1:rows + 1, :] + w[0:1] * dc_ref[half, 2:rows + 2, :]
                da_ref[half, pl.ds(r0, rows), :] = da.astype(BF)
                new_nxt.append(dc[:8])
            return tuple(new_nxt)

        zeros = jnp.zeros((8, tc), F32)
        nxt = lax.fori_loop(0, n_steps - 1, lambda s, nxt: chunk(pl.multiple_of((n_steps - 1 - s) * rows, rows), False, nxt),
                            (zeros, zeros))
        chunk(0, True, nxt)

    col = lambda *lead: pl.BlockSpec((*lead, tc), lambda j: (0,) * len(lead) + (j,))
    outs = _call(
        body, name=name, grid=(f // tc,), in_specs=[col(2, t), col(2, CONV_TAPS), col(2, 1), col(t)],
        out_specs=[col(2, t), col(2, CONV_TAPS), col(2, 1)],
        out_shape=[jax.ShapeDtypeStruct((2, t, f), BF), jax.ShapeDtypeStruct((2, CONV_TAPS, f), F32),
                   jax.ShapeDtypeStruct((2, 1, f), F32)],
        operands=(a3, cw, cb, dy), scratch=[pltpu.VMEM((2, rows + 8, tc), F32)], sem=("parallel",), vmem=48 * _MIB,
        carries=carries)
    return outs[:3], outs[3:]


_ATT_BLOCK = 256


def _split_dot(x, tri):
    hi = x.astype(BF)
    lo = (x - hi.astype(F32)).astype(BF)
    return jnp.dot(hi, tri, preferred_element_type=F32) + jnp.dot(lo, tri, preferred_element_type=F32)


_ATT_HEADS_FWD = 4
_ATT_HEADS_BWD = 4


def _logits(qb, kb, diagonal):
    z = lax.dot_general(qb, kb, _NT, preferred_element_type=F32) * (1.0 / math.sqrt(GROUP))
    lb = jnp.minimum(z, 0.0) - jnp.log(1.0 + jnp.exp(-jnp.abs(z)))
    if not diagonal:
        return lb, lb - z, None
    mask = lax.broadcasted_iota(jnp.int32, z.shape, 1) < lax.broadcasted_iota(jnp.int32, z.shape, 0)
    return lb, jnp.where(mask, lb - z, 0.0), mask


def _head(ref, g, rows=slice(None)):
    return ref[rows, g * GROUP:(g + 1) * GROUP]


def _attn_fwd(name, q, k, v, carries=()):
    t, hd = q.shape
    blk = min(_ATT_BLOCK, t)
    heads = min(_ATT_HEADS_FWD, hd // GROUP)
    gs = range(heads)

    def body(q_ref, k_ref, v_ref, o_ref, rest_ref, first_ref):
        hg, i = pl.program_id(0), pl.program_id(1)
        ri = lax.broadcasted_iota(jnp.int32, (blk, blk), 0)
        ci = lax.broadcasted_iota(jnp.int32, (blk, blk), 1)
        tri = (ri >= ci).astype(BF)

        def tile(j, state, diagonal):
            keys = pl.ds(pl.multiple_of(j * blk, blk), blk)
            logit = [_logits(_head(q_ref, g), _head(k_ref, g, keys), diagonal) for g in gs]
            incl = [_split_dot(logit[g][1], tri) for g in gs]
            a = [jnp.exp(logit[g][0] + (incl[g] - logit[g][1] + state[g][0])) for g in gs]
            if diagonal:
                a = [jnp.where(logit[g][2], a[g], 0.0) for g in gs]
            out = [state[g][1] + jnp.dot(a[g].astype(BF), _head(v_ref, g, keys), preferred_element_type=F32) for g in gs]
            return tuple((state[g][0] + incl[g][:, 0:1], out[g]) for g in gs)

        state = tile(i, ((jnp.zeros((blk, 1), F32), jnp.zeros((blk, GROUP), F32)),) * heads, True)

        def more(carry):
            j, state = carry
            live = functools.reduce(jnp.maximum, [jnp.max(right) for right, _ in state])
            return jnp.logical_and(j >= 0, live > EXP_FLOOR)

        j, state = lax.while_loop(more, lambda c: (c[0] - 1, tile(c[0], c[1], False)), (i - 1, state))
        for g, (right, acc) in enumerate(state):
            o_ref[:, g * GROUP:(g + 1) * GROUP] = acc.astype(BF)
            rest_ref[:, g * GROUP:(g + 1) * GROUP] = jnp.broadcast_to(right, (blk, GROUP))
        first_ref[hg, i] = (j + 1).astype(F32)

    qspec = pl.BlockSpec((blk, heads * GROUP), lambda h, i: (i, h))
    kvspec = pl.BlockSpec((t, heads * GROUP), lambda h, i: (0, h), pipeline_mode=pl.Buffered(1))
    groups = hd // (heads * GROUP)
    outs = _call(
        body, name=name, grid=(groups, t // blk), in_specs=[qspec, kvspec, kvspec],
        out_specs=[qspec, qspec, pl.BlockSpec(memory_space=pltpu.SMEM)],
        out_shape=[jax.ShapeDtypeStruct((t, hd), BF), jax.ShapeDtypeStruct((t, hd), F32),
                   jax.ShapeDtypeStruct((groups, t // blk), F32)],
        operands=(q, k, v), sem=("arbitrary", "arbitrary"), vmem=40 * _MIB, carries=carries, middle_at=0.75)
    return outs[:3], outs[3:]


def _attn_bwd(name, q, k, v, rest, first, do, carries=()):
    t, hd = q.shape
    blk = min(_ATT_BLOCK, t)
    nq = t // blk
    scale = 1.0 / math.sqrt(GROUP)
    heads = min(_ATT_HEADS_BWD, hd // GROUP)
    per_first = min(_ATT_HEADS_FWD, hd // GROUP) // heads
    gs = range(heads)

    def body(first_ref, q_ref, k_ref, v_ref, rest_ref, do_ref, dq_ref, dk_ref, dv_ref, dk_acc, dv_acc):
        hg, i = pl.program_id(0), pl.program_id(1)

        @pl.when(i == 0)
        def _():
            dk_acc[...] = jnp.zeros_like(dk_acc)
            dv_acc[...] = jnp.zeros_like(dv_acc)

        ri = lax.broadcasted_iota(jnp.int32, (blk, blk), 0)
        ci = lax.broadcasted_iota(jnp.int32, (blk, blk), 1)
        tri = (ri <= ci).astype(BF)

        def tile(j, state, diagonal):
            keys = pl.ds(pl.multiple_of(j * blk, blk), blk)
            qs, dos = [_head(q_ref, g) for g in gs], [_head(do_ref, g) for g in gs]
            kb, vb = [_head(k_ref, g, keys) for g in gs], [_head(v_ref, g, keys) for g in gs]
            logit = [_logits(qs[g], kb[g], diagonal) for g in gs]
            pre = [_split_dot(logit[g][1], tri) for g in gs]
            a = [jnp.exp(logit[g][0] + (rest_ref[:, g * GROUP:g * GROUP + 1] - state[g][0] - pre[g])) for g in gs]
            if diagonal:
                a = [jnp.where(logit[g][2], a[g], 0.0) for g in gs]
            gw = [a[g] * lax.dot_general(dos[g], vb[g], _NT, preferred_element_type=F32) for g in gs]
            gpre = [_split_dot(gw[g], tri) for g in gs]
            dz = []
            for g in gs:
                beta = jnp.exp(logit[g][0])
                d = (gw[g] * (1.0 - beta) - (state[g][1] + gpre[g] - gw[g]) * beta) * scale
                dz.append((jnp.where(logit[g][2], d, 0.0) if diagonal else d).astype(BF))
            for g in gs:
                dk_acc[keys, g * GROUP:(g + 1) * GROUP] += lax.dot_general(dz[g], qs[g], _TN, preferred_element_type=F32)
                dv_acc[keys, g * GROUP:(g + 1) * GROUP] += lax.dot_general(a[g].astype(BF), dos[g], _TN,
                                                                           preferred_element_type=F32)
            return tuple((state[g][0] + pre[g][:, blk - 1:blk], state[g][1] + gpre[g][:, blk - 1:blk],
                          state[g][2] + jnp.dot(dz[g], kb[g], preferred_element_type=F32)) for g in gs)

        zero = jnp.zeros((blk, 1), F32)
        first_block = jnp.clip(first_ref[hg // per_first, i].astype(jnp.int32), 0, i)
        state = lax.fori_loop(first_block, i, lambda j, c: tile(j, c, False),
                              ((zero, zero, jnp.zeros((blk, GROUP), F32)),) * heads)
        for g, (_, _, dq) in enumerate(tile(i, state, True)):
            dq_ref[:, g * GROUP:(g + 1) * GROUP] = dq.astype(BF)

        @pl.when(i == nq - 1)
        def _():
            dk_ref[...] = dk_acc[...].astype(BF)
            dv_ref[...] = dv_acc[...].astype(BF)

    qspec = pl.BlockSpec((blk, heads * GROUP), lambda h, i: (i, h))
    kvspec = pl.BlockSpec((t, heads * GROUP), lambda h, i: (0, h), pipeline_mode=pl.Buffered(1))
    outs = _call(
        body, name=name, grid=(hd // (heads * GROUP), nq),
        in_specs=[pl.BlockSpec(memory_space=pltpu.SMEM), qspec, kvspec, kvspec, qspec, qspec],
        out_specs=[qspec, kvspec, kvspec], out_shape=[jax.ShapeDtypeStruct((t, hd), BF)] * 3,
        operands=(first, q, k, v, rest, do),
        scratch=[pltpu.VMEM((t, heads * GROUP), F32), pltpu.VMEM((t, heads * GROUP), F32)],
        sem=("arbitrary", "arbitrary"), vmem=48 * _MIB, carries=carries)
    return outs[:3], outs[3:]


def _adamw_math(w, g, m, v):
    m = ADAM_B1 * m + (1.0 - ADAM_B1) * g
    v = ADAM_B2 * v + (1.0 - ADAM_B2) * (g * g)
    m_hat = m / (1.0 - ADAM_B1 ** ADAM_STEP)
    v_hat = v / (1.0 - ADAM_B2 ** ADAM_STEP)
    return -ADAM_LR * (m_hat / (jnp.sqrt(v_hat) + ADAM_EPS) + ADAM_WD * w), m, v


def _sum_adamw(name, parts, w, m, v):
    layers, r, c = w.shape
    budget = 512 * 1024 // layers
    tr = r if r * c <= budget else _tile_rows(r, max(8, (budget // c) // 8 * 8))
    n = r // tr

    def body(*refs):
        p_refs, (w_ref, m_ref, v_ref, g_ref, d_ref, nm_ref, nv_ref) = refs[:layers], refs[layers:]
        for layer, p_ref in enumerate(p_refs):
            @pl.when(pl.program_id(0) == layer)
            def _(p_ref=p_ref):
                g = p_ref[0].astype(F32)
                for dev in range(1, N_DEV):
                    g = g + p_ref[dev].astype(F32)
                g_ref[...] = g
                d_ref[...], nm_ref[...], nv_ref[...] = _adamw_math(w_ref[...], g, m_ref[...], v_ref[...])

    def part_spec(layer):
        return pl.BlockSpec((N_DEV, tr, c), lambda l, i: (0, jnp.where(l < layer, 0, jnp.where(l == layer, i, n - 1)), 0))

    row = pl.BlockSpec((None, tr, c), lambda l, i: (l, i, 0))
    return pl.pallas_call(
        body, name=name, grid=(layers, n), in_specs=[part_spec(layer) for layer in range(layers)] + [row] * 3,
        out_specs=[row] * 4, out_shape=[jax.ShapeDtypeStruct((layers, r, c), F32)] * 4,
        compiler_params=_params(("arbitrary", "arbitrary"), 40 * _MIB),
    )(*parts, w, m, v)


def _tile_rows(r, pref):
    t = min(r, pref)
    while r % t or t % 8:
        t -= 1
    return t


def _exchange(name, carries):
    return _call(lambda: None, name=name, grid=(1,), in_specs=[], out_specs=[], out_shape=[], operands=(), carries=carries)


def _all_reduce_small(name, groups):
    c = groups[0][0].shape[1]
    parts, offsets, starts, r = [], [], [], 0
    for group in groups:
        starts.append(r)
        for p in group:
            parts.append(p)
            offsets.append(r)
            r += p.shape[0]
        r = -(-r // 8) * 8
    n = len(parts)

    def body(*refs):
        out_ref, slots, send_sems, recv_sems = refs[n:]
        x, y, c_, me = _place()
        slots[me] = jnp.zeros((r, c), F32)
        for p_ref, off in zip(refs[:n], offsets):
            slots[me, off:off + p_ref.shape[0], :] = p_ref[...]

        here, sibling = (x, y, c_), (x, y, 1 - c_)
        chips = [(1 - x, y), (x, 1 - y), (1 - x, 1 - y)]

        def copy(sem, block, to):
            slot = slots.at[4 * block[0] + 2 * block[1] + block[2]]
            return pltpu.make_async_remote_copy(
                src_ref=slot, dst_ref=slot, send_sem=send_sems.at[sem], recv_sem=recv_sems.at[sem], device_id=to,
                device_id_type=MESH)

        sent = [copy(0, here, sibling)] + [copy(1 + j, here, (*chip, c_)) for j, chip in enumerate(chips)]
        for cp in sent:
            cp.start()
        for j, chip in enumerate(chips):
            copy(1 + j, (*chip, c_), here).wait_recv()
            sent.append(copy(4 + j, (*chip, c_), sibling))
            sent[-1].start()
        copy(0, sibling, here).wait_recv()
        for j, chip in enumerate(chips):
            copy(4 + j, (*chip, 1 - c_), here).wait_recv()
        for cp in sent:
            cp.wait_send()
        total = slots[0]
        for dev in range(1, N_DEV):
            total = total + slots[dev]
        out_ref[...] = total

    vm = pl.BlockSpec(memory_space=pltpu.VMEM)
    summed = pl.pallas_call(
        body, name=name, in_specs=[vm] * n, out_specs=vm, out_shape=jax.ShapeDtypeStruct((r, c), F32),
        scratch_shapes=[pltpu.VMEM((N_DEV, r, c), F32), pltpu.SemaphoreType.DMA((7,)), pltpu.SemaphoreType.DMA((7,))],
        compiler_params=_params(None, (N_DEV + 6) * r * c * 4 + 8 * _MIB),
    )(*parts)
    return summed, starts


def _adamw(name, g, w, m, v):
    cols = w.shape[-1]
    flat = lambda a: a.reshape(-1, cols)

    def body(g_ref, w_ref, m_ref, v_ref, d_ref, nm_ref, nv_ref):
        d_ref[...], nm_ref[...], nv_ref[...] = _adamw_math(w_ref[...], g_ref[...], m_ref[...], v_ref[...])

    outs = pl.pallas_call(body, name=name, out_shape=[jax.ShapeDtypeStruct(flat(w).shape, F32)] * 3)(
        flat(g), flat(w), flat(m), flat(v))
    return [o.reshape(w.shape) for o in outs]


def kernel(x, pre_mix_g, post_mix_g, pre_ffn_g, post_ffn_g, a_w_in, a_v_norm_g, a_w_spatial, a_b_spatial, a_w_out, kv_norm_g, w_k, w_v, b_w_q, b_w_o, ffn_w_up, ffn_conv_w, ffn_conv_b, ffn_w_down, loss_target, m_pre_mix_g, m_post_mix_g, m_pre_ffn_g, m_post_ffn_g, m_a_w_in, m_a_v_norm_g, m_a_w_spatial, m_a_b_spatial, m_a_w_out, m_kv_norm_g, m_w_k, m_w_v, m_b_w_q, m_b_w_o, m_ffn_w_up, m_ffn_conv_w, m_ffn_conv_b, m_ffn_w_down, v_pre_mix_g, v_post_mix_g, v_pre_ffn_g, v_post_ffn_g, v_a_w_in, v_a_v_norm_g, v_a_w_spatial, v_a_b_spatial, v_a_w_out, v_kv_norm_g, v_w_k, v_w_v, v_b_w_q, v_b_w_o, v_ffn_w_up, v_ffn_conv_w, v_ffn_conv_b, v_ffn_w_down):
    t, d = x.shape[1], x.shape[2]
    f = ffn_w_down.shape[1] * N_DEV
    ng = d // GROUP
    me = 4 * lax.axis_index("x") + 2 * lax.axis_index("y") + lax.axis_index("c")
    x2, target = x.reshape(t, d), loss_target.reshape(t, d)

    hn0, g_in, g_cw, g_vg = _norms_fwd("pre_mix0", x2, [pre_mix_g[0]], carries=[
        _gather(a_w_in[0].astype(BF)), _gather(ffn_conv_w.reshape(2 * CONV_TAPS, -1)), _gather(a_v_norm_g)])
    up0 = ffn_w_up[0].astype(BF)
    cw_full = jnp.transpose(g_cw.reshape(N_DEV, 2, CONV_TAPS, -1), (1, 2, 0, 3)).reshape(2, CONV_TAPS, 2, f)
    cw_l = [jnp.transpose(cw_full[l], (1, 0, 2)) for l in range(2)]
    cb_l = [ffn_conv_b[l].reshape(2, 1, f) for l in range(2)]
    vg_full = g_vg.reshape(1, d)
    w_s = a_w_spatial[0]
    w_st = jnp.swapaxes(w_s, 1, 2)
    b_st = a_b_spatial[0].T

    p0, (g_out, g_up0) = _mm_nn_blk("sgu_in", hn0, g_in, F32, carries=[
        _gather(a_w_out[0].astype(BF)), _gather(up0, 0, d // 4)])
    w_out_f = g_out.reshape(d, d)
    sg, g_up0 = _sgu_fwd("sgu", p0, vg_full, w_s, b_st, carries=[_gather(up0, d // 4, d // 2, dst=g_up0)])
    mix0, (g_up0,) = _mm_nn("sgu_out", sg, w_out_f, F32, carries=[_gather(up0, d // 2, 3 * d // 4, dst=g_up0)])
    h1, fn0, g_up0 = _resid_norms("post_mix0", x2, mix0, post_mix_g[0], [pre_ffn_g[0]],
                                  carries=[_gather(up0, 3 * d // 4, d, dst=g_up0)])
    a3_0, (g_dn0,) = _mm_nn_blk("ffn0_up", fn0, g_up0, F32, halves=True, carries=[_gather(ffn_w_down[0].astype(BF))])
    y0, (g_q,) = _conv_fwd("ffn0_conv", a3_0, cw_l[0], cb_l[0], carries=[_gather(b_w_q[0].astype(BF))])
    wv = w_v.astype(BF)
    f0, (g_k, g_v) = _mm_nn("ffn0_down", y0, g_dn0.reshape(f, d), F32,
                            carries=[_gather(w_k.astype(BF)), _gather(wv, 0, d // 16)])
    h2, hn1, kvn, g_v = _resid_norms("post_ffn0", h1, f0, post_ffn_g[0], [pre_mix_g[1], kv_norm_g],
                                     carries=[_gather(wv, d // 16, d // 8, dst=g_v)])
    w_q_f, w_k_f, w_v_f = g_q.reshape(d, d), g_k.reshape(d, d), g_v.reshape(d, d)
    up1 = ffn_w_up[1].astype(BF)
    e = d // 8
    q, (g_up1,) = _mm_nn("attn_q", hn1, w_q_f, BF, carries=[_gather(up1, 0, e)])
    kk, (g_up1,) = _mm_nn("attn_k", kvn, w_k_f, BF, carries=[_gather(up1, e, 2 * e, dst=g_up1)])
    vv, (g_up1,) = _mm_nn("attn_v", kvn, w_v_f, BF, carries=[_gather(up1, 2 * e, 3 * e, dst=g_up1)])
    (att, rest, first), (g_o, g_up1) = _attn_fwd(
        "attn", q, kk, vv, carries=[_gather(b_w_o[0].astype(BF)), _gather(up1, 3 * e, 6 * e, dst=g_up1)])
    w_o_f = g_o.reshape(d, d)
    mix1, (g_up1,) = _mm_nn("attn_o", att, w_o_f, F32, carries=[_gather(up1, 6 * e, 7 * e, dst=g_up1)])
    h3, fn1, g_up1 = _resid_norms("post_mix1", h2, mix1, post_mix_g[1], [pre_ffn_g[1]],
                                  carries=[_gather(up1, 7 * e, d, dst=g_up1)])
    a3_1, (g_dn1,) = _mm_nn_blk("ffn1_up", fn1, g_up1, F32, halves=True, carries=[_gather(ffn_w_down[1].astype(BF))])
    y1, _ = _conv_fwd("ffn1_conv", a3_1, cw_l[1], cb_l[1])
    f1 = _mm_nn("ffn1_down", y1, g_dn1.reshape(f, d), F32)
    g_up = (g_up0, g_up1)
    w_dn_f = (g_dn0.reshape(f, d), g_dn1.reshape(f, d))
    dh, loss_part = _resid_loss("loss", h3, f1, post_ffn_g[1], target)
    loss = lax.psum(loss_part[0, 0], ("x", "y", "c"))

    def blocks(dw):
        return dw.reshape(N_DEV, -1, d)

    def split(result, carries):
        return result if carries else (result, [])

    def ffn_bwd(l, dh_out, h_in, fn, a3, yv, fo, with_dw=(), with_dx=(), with_up=()):
        dfo, d_post = _norm_bwd(f"post_ffn{l}_bwd", fo, post_ffn_g[l], dh_out, out_dtype=BF)
        dw_dn, sent_dw = split(_mm_tn(f"ffn{l}_down_dw", yv, dfo, carries=with_dw), with_dw)
        dy, sent_dx = split(_mm_nt(f"ffn{l}_down_dx", dfo, w_dn_f[l], BF, carries=with_dx), with_dx)
        dw_dn = blocks(dw_dn)
        cut = dw_dn.shape[1] * 5 // 8 // 16 * 16
        (da3, dcw, dcb), (p_dn,) = _conv_bwd(f"ffn{l}_conv_bwd", a3, cw_l[l], cb_l[l], dy, carries=[_scatter(dw_dn, 0, cut)])
        dw_up, (p_dn, *sent_up) = _mm_tn_blk(f"ffn{l}_up_dw", fn, da3, halves=True,
                                             carries=[_scatter(dw_dn, cut, dst=p_dn), *with_up])
        dfn, (p_up,) = _mm_nt_blk(f"ffn{l}_up_dx", da3, g_up[l], F32, halves=True, carries=[_scatter(dw_up, 0, d // 2)])
        dh_in, d_pre, p_up = _norm_bwd(f"pre_ffn{l}_bwd", h_in, pre_ffn_g[l], dfn, res=dh_out,
                                       carries=[_scatter(dw_up, d // 2, 9 * d // 16, dst=p_up)])
        return dh_in, d_post, d_pre, dcw, dcb, p_dn, dw_up, p_up, list(sent_dw) + list(sent_dx) + list(sent_up)

    dh3, d_post_ffn1, d_pre_ffn1, dcw1, dcb1, p_dn1, dw_up1, p_up1, _ = ffn_bwd(1, dh, h3, fn1, a3_1, y1, f1)
    dmix1, d_post_mix1 = _norm_bwd("post_mix1_bwd", mix1, post_mix_g[1], dh3, out_dtype=BF)
    dw_o = _mm_tn("attn_o_dw", att, dmix1)
    datt = _mm_nt("attn_o_dx", dmix1, w_o_f, BF)
    (dq, dk, dv), (p_up1,) = _attn_bwd(
        "attn_bwd", q, kk, vv, rest, first, datt, carries=[_scatter(dw_up1, 9 * d // 16, d, dst=p_up1)])
    dw_o, qr = blocks(dw_o), d // 32
    dw_q, (p_o,) = _mm_tn("attn_q_dw", hn1, dq, carries=[_scatter(dw_o, 0, qr)])
    dw_k, (p_o,) = _mm_tn("attn_k_dw", kvn, dk, carries=[_scatter(dw_o, qr, 2 * qr, dst=p_o)])
    dw_v, (p_o,) = _mm_tn("attn_v_dw", kvn, dv, carries=[_scatter(dw_o, 2 * qr, 3 * qr, dst=p_o)])
    dhn1, (p_o,) = _mm_nt("attn_q_dx", dq, w_q_f, F32, carries=[_scatter(dw_o, 3 * qr, 4 * qr, dst=p_o)])
    dkvn = _mm_nt("attn_v_dx", dv, w_v_f, F32, add=_mm_nt("attn_k_dx", dk, w_k_f, F32))
    dh2, d_pre_mix1, d_kv = _norm_bwd("pre_mix1_kv_bwd", h2, [pre_mix_g[1], kv_norm_g], [dhn1, dkvn], res=dh3)
    dh1, d_post_ffn0, d_pre_ffn0, dcw0, dcb0, p_dn0, dw_up0, p_up0, (p_q, p_k, p_v) = ffn_bwd(
        0, dh2, h1, fn0, a3_0, y0, f0, with_dw=[_scatter(blocks(dw_q))], with_dx=[_scatter(blocks(dw_k))],
        with_up=[_scatter(blocks(dw_v))])
    dmix0, d_post_mix0 = _norm_bwd("post_mix0_bwd", mix0, post_mix_g[0], dh1, out_dtype=BF)
    dw_out, (p_up0,) = _mm_tn("sgu_out_dw", sg, dmix0, carries=[_scatter(dw_up0, 9 * d // 16, 11 * d // 16, dst=p_up0)])
    dsg = _mm_nt("sgu_out_dx", dmix0, w_out_f, BF)
    (dp0, d_vg, d_ws, d_bst), (p_up0,) = _sgu_bwd(
        "sgu_bwd", p0, vg_full, w_s, w_st, b_st, dsg, carries=[_scatter(dw_up0, 11 * d // 16, 15 * d // 16, dst=p_up0)])
    dw_in, (p_out,) = _mm_tn_blk("sgu_in_dw", hn0, dp0, carries=[_scatter(blocks(dw_out))])
    dhn0, (p_in,) = _mm_nt_blk("sgu_in_dx", dp0, g_in, F32, carries=[_scatter(dw_in, 0, 5 * d // 8)])
    grad_x, d_pre_mix0, p_in = _norm_bwd("pre_mix0_bwd", x2, pre_mix_g[0], dhn0, res=dh1,
                                         carries=[_scatter(dw_in, 5 * d // 8, 7 * d // 8, dst=p_in)])
    p_up0, p_in = _exchange("scatter_last", [_scatter(dw_up0, 15 * d // 16, d, dst=p_up0),
                                             _scatter(dw_in, 7 * d // 8, d, dst=p_in)])

    def conv_w_grad(dcw):
        return jnp.transpose(dcw, (1, 0, 2)).reshape(CONV_TAPS, 2 * f)

    small = [
        ([d_pre_mix0, d_pre_mix1], (2, d)), ([d_post_mix0, d_post_mix1], (2, d)),
        ([d_pre_ffn0, d_pre_ffn1], (2, d)), ([d_post_ffn0, d_post_ffn1], (2, d)),
        ([d_kv], (d,)), ([d_vg], (1, d)), ([d_bst.T], (1, ng, GROUP)), ([d_ws], (1, ng, GROUP, GROUP)),
        ([dcb0, dcb1], (2, 2 * f)), ([conv_w_grad(dcw0), conv_w_grad(dcw1)], (2, CONV_TAPS, 2 * f)),
    ]
    width = V7X_LANES * math.gcd(d // V7X_LANES, 2 * f // V7X_LANES)
    summed, offsets = _all_reduce_small("reduce_small", [[a.reshape(-1, width) for a in group] for group, _ in small])
    full = [summed[off:off + math.prod(shape) // width].reshape(shape) for off, (_, shape) in zip(offsets, small)]
    g_pre_mix, g_post_mix, g_pre_ffn, g_post_ffn, g_kv, g_vgain, g_bs, g_ws, g_cb, g_cwf = full
    cw_w = 2 * f // N_DEV
    g_vgain = lax.dynamic_slice_in_dim(g_vgain, me * (d // N_DEV), d // N_DEV, axis=1)
    g_cwf = lax.dynamic_slice_in_dim(g_cwf, me * cw_w, cw_w, axis=2)

    parts = [p_in, p_out, p_k, p_v, p_q, p_o, p_up0, p_up1, p_dn0, p_dn1]

    def small_update(name, g, w, m, v):
        return [g] + _adamw(name, g, w, m, v)

    def stacked(name, part0, part1, w, m, v):
        shape = (2, *part0.shape[1:])
        return [o.reshape(w.shape) for o in _sum_adamw(name, [part0, part1], w.reshape(shape), m.reshape(shape), v.reshape(shape))]

    def single(name, part, w, m, v):
        shape = (1, *part.shape[1:])
        return [o.reshape(w.shape) for o in _sum_adamw(name, [part], w.reshape(shape), m.reshape(shape), v.reshape(shape))]

    results = {
        "pre_mix_g": small_update("adam_pre_mix", g_pre_mix, pre_mix_g, m_pre_mix_g, v_pre_mix_g),
        "post_mix_g": small_update("adam_post_mix", g_post_mix, post_mix_g, m_post_mix_g, v_post_mix_g),
        "pre_ffn_g": small_update("adam_pre_ffn", g_pre_ffn, pre_ffn_g, m_pre_ffn_g, v_pre_ffn_g),
        "post_ffn_g": small_update("adam_post_ffn", g_post_ffn, post_ffn_g, m_post_ffn_g, v_post_ffn_g),
        "a_w_in": single("adam_a_w_in", parts[0], a_w_in, m_a_w_in, v_a_w_in),
        "a_v_norm_g": small_update("adam_a_v_norm", g_vgain, a_v_norm_g, m_a_v_norm_g, v_a_v_norm_g),
        "a_w_spatial": small_update("adam_a_w_spatial", g_ws, a_w_spatial, m_a_w_spatial, v_a_w_spatial),
        "a_b_spatial": small_update("adam_a_b_spatial", g_bs, a_b_spatial, m_a_b_spatial, v_a_b_spatial),
        "a_w_out": single("adam_a_w_out", parts[1], a_w_out, m_a_w_out, v_a_w_out),
        "kv_norm_g": small_update("adam_kv_norm", g_kv, kv_norm_g, m_kv_norm_g, v_kv_norm_g),
        "w_k": single("adam_w_k", parts[2], w_k, m_w_k, v_w_k),
        "w_v": single("adam_w_v", parts[3], w_v, m_w_v, v_w_v),
        "b_w_q": single("adam_b_w_q", parts[4], b_w_q, m_b_w_q, v_b_w_q),
        "b_w_o": single("adam_b_w_o", parts[5], b_w_o, m_b_w_o, v_b_w_o),
        "ffn_w_up": stacked("adam_ffn_w_up", parts[6], parts[7], ffn_w_up, m_ffn_w_up, v_ffn_w_up),
        "ffn_conv_w": small_update("adam_ffn_conv_w", g_cwf, ffn_conv_w, m_ffn_conv_w, v_ffn_conv_w),
        "ffn_conv_b": small_update("adam_ffn_conv_b", g_cb, ffn_conv_b, m_ffn_conv_b, v_ffn_conv_b),
        "ffn_w_down": stacked("adam_ffn_w_down", parts[8], parts[9], ffn_w_down, m_ffn_w_down, v_ffn_w_down),
    }
    order = ["pre_mix_g", "post_mix_g", "pre_ffn_g", "post_ffn_g", "a_w_in", "a_v_norm_g", "a_w_spatial", "a_b_spatial",
             "a_w_out", "kv_norm_g", "w_k", "w_v", "b_w_q", "b_w_o", "ffn_w_up", "ffn_conv_w", "ffn_conv_b", "ffn_w_down"]
    outs = [loss, grad_x.reshape(x.shape)]
    for idx in range(4):
        outs += [results[n][idx] for n in order]
    return tuple(outs)
```

```python
import functools
import math
from typing import NamedTuple, Optional

import jax
import jax.numpy as jnp
from jax import lax
from jax.experimental import pallas as pl
from jax.experimental.pallas import tpu as pltpu

F32 = jnp.float32
BF = jnp.bfloat16
MESH = pl.DeviceIdType.MESH

N_DEV = 8
NORM_EPS = 1e-6
GROUP = 128
CONV_TAPS = 3
ADAM_LR, ADAM_B1, ADAM_B2, ADAM_EPS, ADAM_WD, ADAM_STEP = 0.001, 0.9, 0.999, 1e-08, 0.01, 10
EXP_FLOOR = -104.0

V7X_LANES = 128
V7X_VMEM_BYTES = 64 * 1024 * 1024
_MIB = 1024 * 1024

_NN = (((1,), (0,)), ((), ()))
_NT = (((1,), (1,)), ((), ()))
_TN = (((0,), (0,)), ((), ()))


def _tile(n, pref):
    if n <= pref:
        return n
    t = (pref // V7X_LANES) * V7X_LANES
    while t > V7X_LANES and n % t:
        t -= V7X_LANES
    assert n % t == 0, (n, pref)
    return t


def _nbytes(shape, dtype):
    return math.prod(shape) * jnp.dtype(dtype).itemsize


def _params(sem=None, vmem=None):
    kw = {}
    if sem is not None:
        kw["dimension_semantics"] = sem
    if vmem is not None:
        kw["vmem_limit_bytes"] = int(min(max(vmem, 16 * _MIB), V7X_VMEM_BYTES - 8 * _MIB))
    return pltpu.CompilerParams(**kw)


def _place():
    x, y, c = lax.axis_index("x"), lax.axis_index("y"), lax.axis_index("c")
    return x, y, c, 4 * x + 2 * y + c


def _flip(x, y, c, k):
    return (1 - x if k & 4 else x, 1 - y if k & 2 else y, 1 - c if k & 1 else c)


class _Carry(NamedTuple):
    gather: bool
    src: jax.Array
    dst: Optional[jax.Array]
    lo: int
    hi: int


def _gather(src, lo=0, hi=None, dst=None):
    return _Carry(True, src, dst, lo, src.shape[0] if hi is None else hi)


def _scatter(src, lo=0, hi=None, dst=None):
    return _Carry(False, src, dst, lo, src.shape[1] if hi is None else hi)


def _carry_phases(carries, srcs, dsts, send_sems, recv_sems, local_sems):
    x, y, c, me = _place()
    here, sibling = (x, y, c), (x, y, 1 - c)
    chips = [(1 - x, y), (x, 1 - y), (1 - x, 1 - y)]

    def rows(u):
        return pl.ds(carries[u].lo, carries[u].hi - carries[u].lo)

    def block_copy(u, sem, block, to, from_src=False):
        slot = dsts[u].at[4 * block[0] + 2 * block[1] + block[2], rows(u)]
        return pltpu.make_async_remote_copy(
            src_ref=srcs[u].at[rows(u)] if from_src else slot, dst_ref=slot, send_sem=send_sems.at[u, sem],
            recv_sem=recv_sems.at[u, sem], device_id=to, device_id_type=MESH)

    def partial_copy(u, k):
        peer = _flip(x, y, c, k)
        return pltpu.make_async_remote_copy(
            src_ref=srcs[u].at[4 * peer[0] + 2 * peer[1] + peer[2], rows(u)], dst_ref=dsts[u].at[me, rows(u)],
            send_sem=send_sems.at[u, k - 1], recv_sem=recv_sems.at[u, k - 1], device_id=peer, device_id_type=MESH)

    def local_copy(u):
        src = srcs[u].at[rows(u)] if carries[u].gather else srcs[u].at[me, rows(u)]
        return pltpu.make_async_copy(src, dsts[u].at[me, rows(u)], local_sems.at[u])

    def first():
        for u, cr in enumerate(carries):
            local_copy(u).start()
            if cr.gather:
                block_copy(u, 0, here, sibling, from_src=True).start()
                for j, chip in enumerate(chips):
                    block_copy(u, 1 + j, here, (*chip, c), from_src=True).start()
            else:
                for k in range(1, N_DEV):
                    partial_copy(u, k).start()

    def middle():
        for u, cr in enumerate(carries):
            if cr.gather:
                for j, chip in enumerate(chips):
                    block_copy(u, 1 + j, (*chip, c), here).wait_recv()
                    block_copy(u, 4 + j, (*chip, c), sibling).start()

    def last():
        for u, cr in enumerate(carries):
            if cr.gather:
                block_copy(u, 0, sibling, here).wait_recv()
                for j, chip in enumerate(chips):
                    block_copy(u, 4 + j, (*chip, 1 - c), here).wait_recv()
                block_copy(u, 0, here, sibling, from_src=True).wait_send()
                for j, chip in enumerate(chips):
                    block_copy(u, 1 + j, here, (*chip, c), from_src=True).wait_send()
                    block_copy(u, 4 + j, (*chip, c), sibling).wait_send()
            else:
                for k in range(1, N_DEV):
                    partial_copy(u, k).wait()
            local_copy(u).wait()

    return first, middle, last


def _call(body, *, name, grid, in_specs, out_specs, out_shape, operands, scratch=(), sem=None, vmem=None,
          carries=(), middle_at=0.6):
    if not carries:
        return pl.pallas_call(
            body, name=name, grid=grid, in_specs=in_specs, out_specs=out_specs, out_shape=out_shape,
            scratch_shapes=list(scratch), compiler_params=_params(sem, vmem))(*operands)
    n_in, n_out, n_scr, nc = len(in_specs), len(out_specs), len(scratch), len(carries)
    given = [u for u, cr in enumerate(carries) if cr.dst is not None]
    steps = math.prod(grid)
    middle_step = min(steps - 1, int(steps * middle_at))

    def wrapped(*refs):
        ins, srcs = refs[:n_in], refs[n_in:n_in + nc]
        at = n_in + nc + len(given)
        outs, dsts = refs[at:at + n_out], refs[at + n_out:at + n_out + nc]
        at += n_out + nc
        scr, (send_sems, recv_sems, local_sems) = refs[at:at + n_scr], refs[at + n_scr:]
        first, middle, last = _carry_phases(carries, srcs, dsts, send_sems, recv_sems, local_sems)
        step = 0
        for axis, size in enumerate(grid):
            step = step * size + pl.program_id(axis)
        pl.when(step == 0)(first)
        body(*ins, *outs, *scr)
        pl.when(step == middle_step)(middle)
        pl.when(step == steps - 1)(last)

    any_spec = pl.BlockSpec(memory_space=pl.ANY)
    dst_shapes = [jax.ShapeDtypeStruct((N_DEV, *cr.src.shape) if cr.gather else cr.src.shape, cr.src.dtype) for cr in carries]
    return pl.pallas_call(
        wrapped, name=name, grid=grid, in_specs=list(in_specs) + [any_spec] * (nc + len(given)),
        out_specs=list(out_specs) + [any_spec] * nc, out_shape=list(out_shape) + dst_shapes,
        input_output_aliases={n_in + nc + g: n_out + u for g, u in enumerate(given)},
        scratch_shapes=list(scratch) + [pltpu.SemaphoreType.DMA((nc, 7)), pltpu.SemaphoreType.DMA((nc, 7)),
                                        pltpu.SemaphoreType.DMA((nc,))],
        compiler_params=_params(("arbitrary",) * len(grid), vmem),
    )(*operands, *[cr.src for cr in carries], *[carries[u].dst for u in given])


def _mm(name, a, b, *, dims, grid, a_blk, a_map, b_blk, b_map, o_blk, o_map, out_shape, out_dtype,
        add=None, add_blk=None, add_map=None, carries=(), b_slabs=1):
    nk = grid[2]
    assert add is None or nk == 1
    acc_shape = tuple(d for d in o_blk if d is not None)
    in_place = out_dtype == F32

    def body(*refs):
        if add is None:
            a_ref, b_ref, o_ref = refs[:3]
            c_ref, scr = None, refs[3:]
        else:
            a_ref, b_ref, c_ref, o_ref = refs[:4]
            scr = refs[4:]
        if b_slabs == 1:
            part = lax.dot_general(a_ref[...], b_ref[...], dims, preferred_element_type=F32)
        else:
            cw = b_ref.shape[2]
            part = sum(lax.dot_general(a_ref[:, s * cw:(s + 1) * cw], b_ref[s], dims, preferred_element_type=F32)
                       for s in range(b_slabs))
        if c_ref is not None:
            part = part + c_ref[...].astype(F32)
        if nk == 1:
            o_ref[...] = part.astype(o_ref.dtype)
            return
        acc = o_ref if in_place else scr[0]
        k = pl.program_id(2)

        @pl.when(k == 0)
        def _():
            acc[...] = part

        @pl.when(k > 0 if in_place else jnp.logical_and(k > 0, k < nk - 1))
        def _():
            acc[...] += part

        if not in_place:
            @pl.when(k == nk - 1)
            def _():
                o_ref[...] = (acc[...] + part).astype(o_ref.dtype)

    in_specs = [pl.BlockSpec(a_blk, a_map), pl.BlockSpec(b_blk, b_map)]
    operands = [a, b]
    scratch = [pltpu.VMEM(acc_shape, F32)] if nk > 1 and not in_place else []
    vmem = 2 * (_nbytes(acc_shape, out_dtype) + _nbytes([d for d in a_blk if d], a.dtype)
                + _nbytes([d for d in b_blk if d], b.dtype)) + (2 + len(scratch)) * _nbytes(acc_shape, F32)
    if add is not None:
        in_specs.append(pl.BlockSpec(add_blk, add_map))
        operands.append(add)
        vmem += 2 * _nbytes(acc_shape, add.dtype)
    out, *dsts = _call(
        body, name=name, grid=grid, in_specs=in_specs, out_specs=[pl.BlockSpec(o_blk, o_map)],
        out_shape=[jax.ShapeDtypeStruct(out_shape, out_dtype)], operands=operands, scratch=scratch,
        sem=("parallel", "parallel", "arbitrary"), vmem=vmem + 8 * _MIB, carries=carries)
    return (out, dsts) if carries else out


def _mm_nn(name, x, w, out_dtype, carries=()):
    t, kd = x.shape
    n = w.shape[1]
    tm, tn, tk = _tile(t, 1024), _tile(n, 1024), _tile(kd, 1536 if kd > 2048 else 2048)
    return _mm(name, x, w, dims=_NN, grid=(t // tm, n // tn, kd // tk),
               a_blk=(tm, tk), a_map=lambda i, j, k: (i, k), b_blk=(tk, tn), b_map=lambda i, j, k: (k, j),
               o_blk=(tm, tn), o_map=lambda i, j, k: (i, j), out_shape=(t, n), out_dtype=out_dtype, carries=carries)


def _mm_nn_blk(name, x, g, out_dtype, halves=False, carries=()):
    t, kd = x.shape
    cw = g.shape[2]
    tm = _tile(t, 1024)
    if halves:
        o_blk, o_map, out_shape = (None, tm, cw), (lambda i, j, k: (j // 4, i, j % 4)), (2, t, 4 * cw)
    else:
        o_blk, o_map, out_shape = (tm, cw), (lambda i, j, k: (i, j)), (t, N_DEV * cw)
    return _mm(name, x, g, dims=_NN, grid=(t // tm, N_DEV, 1),
               a_blk=(tm, kd), a_map=lambda i, j, k: (i, 0), b_blk=(None, kd, cw), b_map=lambda i, j, k: (j, 0, 0),
               o_blk=o_blk, o_map=o_map, out_shape=out_shape, out_dtype=out_dtype, carries=carries)


def _mm_nt(name, dy, w, out_dtype, add=None, carries=()):
    t, n = dy.shape
    kd = w.shape[0]
    tm, tn = _tile(t, 1024), _tile(kd, 512)
    kw = {}
    if add is not None:
        kw = dict(add=add, add_blk=(tm, tn), add_map=lambda i, j, k: (i, j))
    return _mm(name, dy, w, dims=_NT, grid=(t // tm, kd // tn, 1),
               a_blk=(tm, n), a_map=lambda i, j, k: (i, 0), b_blk=(tn, n), b_map=lambda i, j, k: (j, 0),
               o_blk=(tm, tn), o_map=lambda i, j, k: (i, j), out_shape=(t, kd), out_dtype=out_dtype, carries=carries, **kw)


def _mm_nt_blk(name, dy, g, out_dtype, halves=False, carries=()):
    kd, cw = g.shape[1], g.shape[2]
    t = dy.shape[1] if halves else dy.shape[0]
    tm = _tile(t, 512)
    slabs = 2 if cw > 512 else 4
    nk = N_DEV // slabs
    if halves:
        a_blk, a_map = (None, tm, slabs * cw), (lambda i, j, k: (k // (nk // 2), i, k % (nk // 2)))
    else:
        a_blk, a_map = (tm, slabs * cw), (lambda i, j, k: (i, k))
    return _mm(name, dy, g, dims=_NT, grid=(t // tm, 1, nk), b_slabs=slabs,
               a_blk=a_blk, a_map=a_map, b_blk=(slabs, kd, cw), b_map=lambda i, j, k: (k, 0, 0),
               o_blk=(tm, kd), o_map=lambda i, j, k: (i, 0), out_shape=(t, kd), out_dtype=out_dtype, carries=carries)


def _mm_tn(name, x, dy, carries=()):
    t, kd = x.shape
    n = dy.shape[1]
    tmx, tk = _tile(kd, 1024), _tile(t, 2048)
    return _mm(name, x, dy, dims=_TN, grid=(kd // tmx, 1, t // tk),
               a_blk=(tk, tmx), a_map=lambda i, j, k: (k, i), b_blk=(tk, n), b_map=lambda i, j, k: (k, 0),
               o_blk=(tmx, n), o_map=lambda i, j, k: (i, 0), out_shape=(kd, n), out_dtype=BF, carries=carries)


def _mm_tn_blk(name, x, dy, halves=False, carries=()):
    t, kd = x.shape
    cw = dy.shape[2] // 4 if halves else dy.shape[1] // N_DEV
    tmx, tk = _tile(kd, 2048 if cw <= 512 else 1024), _tile(t, 2048)
    if halves:
        b_blk, b_map = (None, tk, cw), (lambda i, j, k: (j // 4, k, j % 4))
    else:
        b_blk, b_map = (tk, cw), (lambda i, j, k: (k, j))
    return _mm(name, x, dy, dims=_TN, grid=(kd // tmx, N_DEV, t // tk),
               a_blk=(tk, tmx), a_map=lambda i, j, k: (k, i), b_blk=b_blk, b_map=b_map,
               o_blk=(None, tmx, cw), o_map=lambda i, j, k: (j, i, 0), out_shape=(N_DEV, kd, cw), out_dtype=BF,
               carries=carries)


def _rms(x, g):
    r = lax.rsqrt(jnp.mean(x * x, axis=-1, keepdims=True) + NORM_EPS)
    return x * r * g


def _rms_bwd_math(x, g, dy):
    d = x.shape[-1]
    r = lax.rsqrt(jnp.mean(x * x, axis=-1, keepdims=True) + NORM_EPS)
    xh = x * r
    u = dy * g
    dx = r * u - xh * (jnp.sum(xh * u, axis=-1, keepdims=True) * (r / d))
    return dx, jnp.sum(dy * xh, axis=0, keepdims=True)


def _row_specs(tr, d, n):
    return [pl.BlockSpec((tr, d), lambda i: (i, 0)) for _ in range(n)]


def _vec_specs(d, n):
    return [pl.BlockSpec((1, d), lambda i: (0, 0)) for _ in range(n)]


def _norms_fwd(name, h, gains, carries=()):
    t, d = h.shape
    tr, ng = _tile(t, 256), len(gains)

    def body(h_ref, *refs):
        x = h_ref[...]
        for g_ref, o_ref in zip(refs[:ng], refs[ng:]):
            o_ref[...] = _rms(x, g_ref[...]).astype(BF)

    return _call(
        body, name=name, grid=(t // tr,), in_specs=_row_specs(tr, d, 1) + _vec_specs(d, ng),
        out_specs=_row_specs(tr, d, ng), out_shape=[jax.ShapeDtypeStruct((t, d), BF)] * ng,
        operands=(h, *[g.reshape(1, d) for g in gains]), sem=("parallel",), carries=carries)


def _resid_norms(name, h, m, g_post, gains, carries=()):
    t, d = h.shape
    tr, ng = _tile(t, 256), len(gains)

    def body(h_ref, m_ref, gp_ref, *refs):
        hn = h_ref[...] + _rms(m_ref[...], gp_ref[...])
        refs[ng][...] = hn
        for g_ref, o_ref in zip(refs[:ng], refs[ng + 1:]):
            o_ref[...] = _rms(hn, g_ref[...]).astype(BF)

    return _call(
        body, name=name, grid=(t // tr,), in_specs=_row_specs(tr, d, 2) + _vec_specs(d, 1 + ng),
        out_specs=_row_specs(tr, d, 1 + ng),
        out_shape=[jax.ShapeDtypeStruct((t, d), F32)] + [jax.ShapeDtypeStruct((t, d), BF)] * ng,
        operands=(h, m, g_post.reshape(1, d), *[g.reshape(1, d) for g in gains]), sem=("parallel",), carries=carries)


def _resid_loss(name, h, m, g_post, target):
    t, d = h.shape
    tr = _tile(t, 256)

    def body(h_ref, m_ref, gp_ref, t_ref, dy_ref, loss_ref, dm_ref, dg_ref):
        mv = m_ref[...]
        diff = h_ref[...] + _rms(mv, gp_ref[...]) - t_ref[...]
        dy = diff * (1.0 / d)
        dy_ref[...] = dy
        dm, dg = _rms_bwd_math(mv, gp_ref[...], dy)
        dm_ref[...] = dm.astype(BF)

        @pl.when(pl.program_id(0) == 0)
        def _():
            loss_ref[...] = jnp.zeros_like(loss_ref)
            dg_ref[...] = jnp.zeros_like(dg_ref)

        per_row = jnp.sum(diff * diff, axis=-1, keepdims=True) * (1.0 / d)
        loss_ref[...] += 0.5 * jnp.sum(per_row, axis=0, keepdims=True)
        dg_ref[...] += dg

    row = pl.BlockSpec((tr, d), lambda i: (i, 0))
    return pl.pallas_call(
        body, name=name, grid=(t // tr,),
        in_specs=_row_specs(tr, d, 2) + _vec_specs(d, 1) + _row_specs(tr, d, 1),
        out_specs=[row, pl.BlockSpec((1, 1), lambda i: (0, 0)), row, pl.BlockSpec((1, d), lambda i: (0, 0))],
        out_shape=[jax.ShapeDtypeStruct((t, d), F32), jax.ShapeDtypeStruct((1, 1), F32),
                   jax.ShapeDtypeStruct((t, d), BF), jax.ShapeDtypeStruct((1, d), F32)],
        compiler_params=_params(("arbitrary",)),
    )(h, m, g_post.reshape(1, d), target)


def _norm_bwd(name, x, g, dy, res=None, out_dtype=F32, then=None, carries=()):
    t, d = x.shape
    tr = _tile(t, 256)
    has_res, has_then = res is not None, then is not None
    gains, dys = (g, dy) if isinstance(g, (list, tuple)) else ([g], [dy])
    n = len(gains)

    def accumulate(dg_ref, dg):
        @pl.when(pl.program_id(0) == 0)
        def _():
            dg_ref[...] = jnp.zeros_like(dg_ref)

        dg_ref[...] += dg

    def body(x_ref, *refs):
        dy_refs, g_refs, rest = refs[:n], refs[n:2 * n], list(refs[2 * n:])
        dx = rest.pop(0)[...] if has_res else 0.0
        then_refs = (rest.pop(0), rest.pop(0)) if has_then else None
        dx_ref, dg_refs = rest[0], rest[1:1 + n]
        xv = x_ref[...].astype(F32)
        for dy_ref, g_ref, dg_ref in zip(dy_refs, g_refs, dg_refs):
            dx_one, dg = _rms_bwd_math(xv, g_ref[...], dy_ref[...].astype(F32))
            dx = dx + dx_one
            accumulate(dg_ref, dg)
        dx_ref[...] = dx.astype(dx_ref.dtype)
        if has_then:
            d2, dg2 = _rms_bwd_math(then_refs[0][...], then_refs[1][...], dx)
            rest[1 + n][...] = d2.astype(BF)
            accumulate(rest[2 + n], dg2)

    ops = [x, *dys, *[gain.reshape(1, d) for gain in gains]] + ([res] if has_res else [])
    in_specs = _row_specs(tr, d, 1 + n) + _vec_specs(d, n) + _row_specs(tr, d, int(has_res))
    out_specs = [pl.BlockSpec((tr, d), lambda i: (i, 0))] + [pl.BlockSpec((1, d), lambda i: (0, 0))] * n
    out_shape = [jax.ShapeDtypeStruct((t, d), out_dtype)] + [jax.ShapeDtypeStruct((1, d), F32)] * n
    if has_then:
        ops += [then[0], then[1].reshape(1, d)]
        in_specs += _row_specs(tr, d, 1) + _vec_specs(d, 1)
        out_specs += [pl.BlockSpec((tr, d), lambda i: (i, 0)), pl.BlockSpec((1, d), lambda i: (0, 0))]
        out_shape += [jax.ShapeDtypeStruct((t, d), BF), jax.ShapeDtypeStruct((1, d), F32)]
    return _call(body, name=name, grid=(t // tr,), in_specs=in_specs, out_specs=out_specs, out_shape=out_shape,
                 operands=ops, sem=("arbitrary",), carries=carries)


_GELU_C = math.sqrt(2.0 / math.pi)
_GELU_A = 0.044715


def _gelu(x):
    return 0.5 * x * (1.0 + jnp.tanh(_GELU_C * (x + _GELU_A * x * x * x)))


def _gelu_and_grad(x):
    th = jnp.tanh(_GELU_C * (x + _GELU_A * x * x * x))
    grad = 0.5 * (1.0 + th) + 0.5 * x * (1.0 - th * th) * (_GELU_C * (1.0 + 3.0 * _GELU_A * x * x))
    return 0.5 * x * (1.0 + th), grad


def _causal(n):
    return lax.broadcasted_iota(jnp.int32, (n, n), 1) <= lax.broadcasted_iota(jnp.int32, (n, n), 0)


def _sgu_fwd(name, p, v_gain, w_s, b_st, carries=()):
    t, da2 = p.shape
    da = da2 // 2
    ng = da // GROUP

    def body(p_ref, vg_ref, ws_ref, bst_ref, o_ref):
        keep = _causal(GROUP)
        for g in range(ng):
            lo = g * GROUP
            u = _gelu(p_ref[:, lo:lo + GROUP])
            vn = _rms(_gelu(p_ref[:, da + lo:da + lo + GROUP]), vg_ref[:, lo:lo + GROUP])
            w = jnp.where(keep, ws_ref[g], 0.0).astype(BF)
            mixed = jnp.dot(w, vn.astype(BF), preferred_element_type=F32) + bst_ref[:, g:g + 1]
            o_ref[:, lo:lo + GROUP] = (u * mixed).astype(BF)

    return _call(
        body, name=name, grid=(t // GROUP,),
        in_specs=[pl.BlockSpec((GROUP, da2), lambda i: (i, 0)), pl.BlockSpec((1, da), lambda i: (0, 0)),
                  pl.BlockSpec((ng, GROUP, GROUP), lambda i: (0, 0, 0)), pl.BlockSpec((GROUP, ng), lambda i: (0, 0))],
        out_specs=[pl.BlockSpec((GROUP, da), lambda i: (i, 0))], out_shape=[jax.ShapeDtypeStruct((t, da), BF)],
        operands=(p, v_gain, w_s, b_st), sem=("parallel",), carries=carries)


def _sgu_bwd(name, p, v_gain, w_s, w_st, b_st, dout, carries=()):
    t, da2 = p.shape
    da = da2 // 2
    ng = da // GROUP

    def body(p_ref, vg_ref, ws_ref, wst_ref, bst_ref, do_ref, dp_ref, dvg_ref, dws_ref, dbst_ref):
        @pl.when(pl.program_id(0) == 0)
        def _():
            dvg_ref[...] = jnp.zeros_like(dvg_ref)
            dws_ref[...] = jnp.zeros_like(dws_ref)
            dbst_ref[...] = jnp.zeros_like(dbst_ref)

        keep = _causal(GROUP)
        keep_t = lax.broadcasted_iota(jnp.int32, (GROUP, GROUP), 0) <= lax.broadcasted_iota(jnp.int32, (GROUP, GROUP), 1)
        for g in range(ng):
            lo = g * GROUP
            u, du = _gelu_and_grad(p_ref[:, lo:lo + GROUP])
            v, dv_act = _gelu_and_grad(p_ref[:, da + lo:da + lo + GROUP])
            gain = vg_ref[:, lo:lo + GROUP]
            r = lax.rsqrt(jnp.mean(v * v, axis=-1, keepdims=True) + NORM_EPS)
            vh = v * r
            vnb = (vh * gain).astype(BF)
            w = jnp.where(keep, ws_ref[g], 0.0).astype(BF)
            wt = jnp.where(keep_t, wst_ref[g], 0.0).astype(BF)
            mixed = jnp.dot(w, vnb, preferred_element_type=F32) + bst_ref[:, g:g + 1]
            dout_g = do_ref[:, lo:lo + GROUP].astype(F32)
            dmixed = dout_g * u
            dmb = dmixed.astype(BF)
            dbst_ref[:, g:g + 1] += jnp.sum(dmixed, axis=1, keepdims=True)
            dws_ref[g] += jnp.where(keep, lax.dot_general(dmb, vnb, _NT, preferred_element_type=F32), 0.0)
            dvn = jnp.dot(wt, dmb, preferred_element_type=F32)
            dvg_ref[:, lo:lo + GROUP] += jnp.sum(dvn * vh, axis=0, keepdims=True)
            dvh = dvn * gain
            dv = r * dvh - vh * (jnp.sum(vh * dvh, axis=-1, keepdims=True) * (r / GROUP))
            dp_ref[:, lo:lo + GROUP] = (dout_g * mixed * du).astype(BF)
            dp_ref[:, da + lo:da + lo + GROUP] = (dv * dv_act).astype(BF)

    full = lambda *shape: pl.BlockSpec(shape, lambda i: (0,) * len(shape))
    outs = _call(
        body, name=name, grid=(t // GROUP,),
        in_specs=[pl.BlockSpec((GROUP, da2), lambda i: (i, 0)), full(1, da), full(ng, GROUP, GROUP), full(ng, GROUP, GROUP),
                  full(GROUP, ng), pl.BlockSpec((GROUP, da), lambda i: (i, 0))],
        out_specs=[pl.BlockSpec((GROUP, da2), lambda i: (i, 0)), full(1, da), full(ng, GROUP, GROUP), full(GROUP, ng)],
        out_shape=[jax.ShapeDtypeStruct((t, da2), BF), jax.ShapeDtypeStruct((1, da), F32),
                   jax.ShapeDtypeStruct((ng, GROUP, GROUP), F32), jax.ShapeDtypeStruct((GROUP, ng), F32)],
        operands=(p, v_gain, w_s, w_st, b_st, dout), sem=("arbitrary",), carries=carries)
    return outs[:4], outs[4:]


_CONV_ROWS = 256
_CONV_COLS = 128


def _shift_down(x, prev, k):
    top = pltpu.roll(jnp.concatenate([prev, x[:8]], axis=0), k, 0)[8:16]
    return jnp.concatenate([top, pltpu.roll(x, k, 0)[8:]], axis=0)


def _conv_taps(a_ref, half, r0, first):
    rows = _rows(a_ref)
    x = a_ref[half, pl.ds(r0, rows), :]
    if first:
        prev = jnp.zeros((8, x.shape[1]), F32)
        return x, _shift_down(x, prev, 1), _shift_down(x, prev, 2)
    return x, a_ref[half, pl.ds(r0 - 1, rows), :], a_ref[half, pl.ds(r0 - 2, rows), :]


def _rows(a_ref):
    return min(_CONV_ROWS, a_ref.shape[1])


def _conv_fwd(name, a3, cw, cb, carries=()):
    _, t, f = a3.shape
    tc, rows = _CONV_COLS, min(_CONV_ROWS, t)

    def body(a_ref, cw_ref, cb_ref, y_ref):
        def chunk(r0, first):
            c = []
            for half in range(2):
                x, x1, x2 = _conv_taps(a_ref, half, r0, first)
                w = cw_ref[half]
                c.append(cb_ref[half] + (w[0:1] * x2 + w[1:2] * x1 + w[2:3] * x))
            y_ref[pl.ds(r0, rows), :] = (c[0] * jax.nn.sigmoid(c[0]) * c[1]).astype(BF)

        chunk(0, True)

        @pl.loop(1, t // rows)
        def _(r):
            chunk(pl.multiple_of(r * rows, rows), False)

    col = lambda *lead: pl.BlockSpec((*lead, tc), lambda j: (0,) * len(lead) + (j,))
    y, *dsts = _call(
        body, name=name, grid=(f // tc,), in_specs=[col(2, t), col(2, CONV_TAPS), col(2, 1)],
        out_specs=[col(t)], out_shape=[jax.ShapeDtypeStruct((t, f), BF)], operands=(a3, cw, cb),
        sem=("parallel",), vmem=40 * _MIB, carries=carries)
    return y, dsts


def _conv_bwd(name, a3, cw, cb, dy, carries=()):
    _, t, f = a3.shape
    tc, rows = _CONV_COLS, min(_CONV_ROWS, t)
    n_steps = t // rows

    def body(a_ref, cw_ref, cb_ref, dy_ref, da_ref, dcw_ref, dcb_ref, dc_ref):
        dcw_ref[...] = jnp.zeros_like(dcw_ref)
        dcb_ref[...] = jnp.zeros_like(dcb_ref)

        def chunk(r0, first, nxt):
            taps, c = [], []
            for half in range(2):
                x, x1, x2 = _conv_taps(a_ref, half, r0, first)
                w = cw_ref[half]
                taps.append((x2, x1, x))
                c.append(cb_ref[half] + (w[0:1] * x2 + w[1:2] * x1 + w[2:3] * x))
            gate, val = c
            sg = jax.nn.sigmoid(gate)
            dyv = dy_ref[pl.ds(r0, rows), :].astype(F32)
            dcs = (dyv * val * (sg * (1.0 + gate * (1.0 - sg))), dyv * (gate * sg))
            new_nxt = []
            for half in range(2):
                dc, w = dcs[half], cw_ref[half]
                dcb_ref[half] += jnp.sum(dc, axis=0, keepdims=True)
                for tap in range(CONV_TAPS):
                    dcw_ref[half, tap:tap + 1, :] += jnp.sum(dc * taps[half][tap], axis=0, keepdims=True)
                dc_ref[half, 0:rows, :] = dc
                dc_ref[half, rows:rows + 8, :] = nxt[half]
                da = w[2:3] * dc + w[1:2] * dc_ref[half, 1:rows + 1, :] + w[0:1] * dc_ref[half, 2:rows + 2, :]
                da_ref[half, pl.ds(r0, rows), :] = da.astype(BF)
                new_nxt.append(dc[:8])
            return tuple(new_nxt)

        zeros = jnp.zeros((8, tc), F32)
        nxt = lax.fori_loop(0, n_steps - 1, lambda s, nxt: chunk(pl.multiple_of((n_steps - 1 - s) * rows, rows), False, nxt),
                            (zeros, zeros))
        chunk(0, True, nxt)

    col = lambda *lead: pl.BlockSpec((*lead, tc), lambda j: (0,) * len(lead) + (j,))
    outs = _call(
        body, name=name, grid=(f // tc,), in_specs=[col(2, t), col(2, CONV_TAPS), col(2, 1), col(t)],
        out_specs=[col(2, t), col(2, CONV_TAPS), col(2, 1)],
        out_shape=[jax.ShapeDtypeStruct((2, t, f), BF), jax.ShapeDtypeStruct((2, CONV_TAPS, f), F32),
                   jax.ShapeDtypeStruct((2, 1, f), F32)],
        operands=(a3, cw, cb, dy), scratch=[pltpu.VMEM((2, rows + 8, tc), F32)], sem=("parallel",), vmem=48 * _MIB,
        carries=carries)
    return outs[:3], outs[3:]


_ATT_BLOCK = 256


def _split_dot(x, tri):
    hi = x.astype(BF)
    lo = (x - hi.astype(F32)).astype(BF)
    return jnp.dot(hi, tri, preferred_element_type=F32) + jnp.dot(lo, tri, preferred_element_type=F32)


_ATT_HEADS_FWD = 4
_ATT_HEADS_BWD = 4


def _logits(qb, kb, diagonal):
    z = lax.dot_general(qb, kb, _NT, preferred_element_type=F32) * (1.0 / math.sqrt(GROUP))
    lb = jnp.minimum(z, 0.0) - jnp.log(1.0 + jnp.exp(-jnp.abs(z)))
    if not diagonal:
        return lb, lb - z, None
    mask = lax.broadcasted_iota(jnp.int32, z.shape, 1) < lax.broadcasted_iota(jnp.int32, z.shape, 0)
    return lb, jnp.where(mask, lb - z, 0.0), mask


def _head(ref, g, rows=slice(None)):
    return ref[rows, g * GROUP:(g + 1) * GROUP]


def _attn_fwd(name, q, k, v, carries=()):
    t, hd = q.shape
    blk = min(_ATT_BLOCK, t)
    heads = min(_ATT_HEADS_FWD, hd // GROUP)
    gs = range(heads)

    def body(q_ref, k_ref, v_ref, o_ref, rest_ref, first_ref):
        hg, i = pl.program_id(0), pl.program_id(1)
        ri = lax.broadcasted_iota(jnp.int32, (blk, blk), 0)
        ci = lax.broadcasted_iota(jnp.int32, (blk, blk), 1)
        tri = (ri >= ci).astype(BF)

        def tile(j, state, diagonal):
            keys = pl.ds(pl.multiple_of(j * blk, blk), blk)
            logit = [_logits(_head(q_ref, g), _head(k_ref, g, keys), diagonal) for g in gs]
            incl = [_split_dot(logit[g][1], tri) for g in gs]
            a = [jnp.exp(logit[g][0] + (incl[g] - logit[g][1] + state[g][0])) for g in gs]
            if diagonal:
                a = [jnp.where(logit[g][2], a[g], 0.0) for g in gs]
            out = [state[g][1] + jnp.dot(a[g].astype(BF), _head(v_ref, g, keys), preferred_element_type=F32) for g in gs]
            return tuple((state[g][0] + incl[g][:, 0:1], out[g]) for g in gs)

        state = tile(i, ((jnp.zeros((blk, 1), F32), jnp.zeros((blk, GROUP), F32)),) * heads, True)

        def more(carry):
            j, state = carry
            live = functools.reduce(jnp.maximum, [jnp.max(right) for right, _ in state])
            return jnp.logical_and(j >= 0, live > EXP_FLOOR)

        j, state = lax.while_loop(more, lambda c: (c[0] - 1, tile(c[0], c[1], False)), (i - 1, state))
        for g, (right, acc) in enumerate(state):
            o_ref[:, g * GROUP:(g + 1) * GROUP] = acc.astype(BF)
            rest_ref[:, g * GROUP:(g + 1) * GROUP] = jnp.broadcast_to(right, (blk, GROUP))
        first_ref[hg, i] = (j + 1).astype(F32)

    qspec = pl.BlockSpec((blk, heads * GROUP), lambda h, i: (i, h))
    kvspec = pl.BlockSpec((t, heads * GROUP), lambda h, i: (0, h), pipeline_mode=pl.Buffered(1))
    groups = hd // (heads * GROUP)
    outs = _call(
        body, name=name, grid=(groups, t // blk), in_specs=[qspec, kvspec, kvspec],
        out_specs=[qspec, qspec, pl.BlockSpec(memory_space=pltpu.SMEM)],
        out_shape=[jax.ShapeDtypeStruct((t, hd), BF), jax.ShapeDtypeStruct((t, hd), F32),
                   jax.ShapeDtypeStruct((groups, t // blk), F32)],
        operands=(q, k, v), sem=("arbitrary", "arbitrary"), vmem=40 * _MIB, carries=carries, middle_at=0.75)
    return outs[:3], outs[3:]


def _attn_bwd(name, q, k, v, rest, first, do, carries=()):
    t, hd = q.shape
    blk = min(_ATT_BLOCK, t)
    nq = t // blk
    scale = 1.0 / math.sqrt(GROUP)
    heads = min(_ATT_HEADS_BWD, hd // GROUP)
    per_first = min(_ATT_HEADS_FWD, hd // GROUP) // heads
    gs = range(heads)

    def body(first_ref, q_ref, k_ref, v_ref, rest_ref, do_ref, dq_ref, dk_ref, dv_ref, dk_acc, dv_acc):
        hg, i = pl.program_id(0), pl.program_id(1)

        @pl.when(i == 0)
        def _():
            dk_acc[...] = jnp.zeros_like(dk_acc)
            dv_acc[...] = jnp.zeros_like(dv_acc)

        ri = lax.broadcasted_iota(jnp.int32, (blk, blk), 0)
        ci = lax.broadcasted_iota(jnp.int32, (blk, blk), 1)
        tri = (ri <= ci).astype(BF)

        def tile(j, state, diagonal):
            keys = pl.ds(pl.multiple_of(j * blk, blk), blk)
            qs, dos = [_head(q_ref, g) for g in gs], [_head(do_ref, g) for g in gs]
            kb, vb = [_head(k_ref, g, keys) for g in gs], [_head(v_ref, g, keys) for g in gs]
            logit = [_logits(qs[g], kb[g], diagonal) for g in gs]
            pre = [_split_dot(logit[g][1], tri) for g in gs]
            a = [jnp.exp(logit[g][0] + (rest_ref[:, g * GROUP:g * GROUP + 1] - state[g][0] - pre[g])) for g in gs]
            if diagonal:
                a = [jnp.where(logit[g][2], a[g], 0.0) for g in gs]
            gw = [a[g] * lax.dot_general(dos[g], vb[g], _NT, preferred_element_type=F32) for g in gs]
            gpre = [_split_dot(gw[g], tri) for g in gs]
            dz = []
            for g in gs:
                beta = jnp.exp(logit[g][0])
                d = (gw[g] * (1.0 - beta) - (state[g][1] + gpre[g] - gw[g]) * beta) * scale
                dz.append((jnp.where(logit[g][2], d, 0.0) if diagonal else d).astype(BF))
            for g in gs:
                dk_acc[keys, g * GROUP:(g + 1) * GROUP] += lax.dot_general(dz[g], qs[g], _TN, preferred_element_type=F32)
                dv_acc[keys, g * GROUP:(g + 1) * GROUP] += lax.dot_general(a[g].astype(BF), dos[g], _TN,
                                                                           preferred_element_type=F32)
            return tuple((state[g][0] + pre[g][:, blk - 1:blk], state[g][1] + gpre[g][:, blk - 1:blk],
                          state[g][2] + jnp.dot(dz[g], kb[g], preferred_element_type=F32)) for g in gs)

        zero = jnp.zeros((blk, 1), F32)
        first_block = jnp.clip(first_ref[hg // per_first, i].astype(jnp.int32), 0, i)
        state = lax.fori_loop(first_block, i, lambda j, c: tile(j, c, False),
                              ((zero, zero, jnp.zeros((blk, GROUP), F32)),) * heads)
        for g, (_, _, dq) in enumerate(tile(i, state, True)):
            dq_ref[:, g * GROUP:(g + 1) * GROUP] = dq.astype(BF)

        @pl.when(i == nq - 1)
        def _():
            dk_ref[...] = dk_acc[...].astype(BF)
            dv_ref[...] = dv_acc[...].astype(BF)

    qspec = pl.BlockSpec((blk, heads * GROUP), lambda h, i: (i, h))
    kvspec = pl.BlockSpec((t, heads * GROUP), lambda h, i: (0, h), pipeline_mode=pl.Buffered(1))
    outs = _call(
        body, name=name, grid=(hd // (heads * GROUP), nq),
        in_specs=[pl.BlockSpec(memory_space=pltpu.SMEM), qspec, kvspec, kvspec, qspec, qspec],
        out_specs=[qspec, kvspec, kvspec], out_shape=[jax.ShapeDtypeStruct((t, hd), BF)] * 3,
        operands=(first, q, k, v, rest, do),
        scratch=[pltpu.VMEM((t, heads * GROUP), F32), pltpu.VMEM((t, heads * GROUP), F32)],
        sem=("arbitrary", "arbitrary"), vmem=48 * _MIB, carries=carries)
    return outs[:3], outs[3:]


def _adamw_math(w, g, m, v):
    m = ADAM_B1 * m + (1.0 - ADAM_B1) * g
    v = ADAM_B2 * v + (1.0 - ADAM_B2) * (g * g)
    m_hat = m / (1.0 - ADAM_B1 ** ADAM_STEP)
    v_hat = v / (1.0 - ADAM_B2 ** ADAM_STEP)
    return -ADAM_LR * (m_hat / (jnp.sqrt(v_hat) + ADAM_EPS) + ADAM_WD * w), m, v


def _sum_adamw(name, parts, w, m, v):
    layers, r, c = w.shape
    budget = 512 * 1024 // layers
    tr = r if r * c <= budget else _tile_rows(r, max(8, (budget // c) // 8 * 8))
    n = r // tr

    def body(*refs):
        p_refs, (w_ref, m_ref, v_ref, g_ref, d_ref, nm_ref, nv_ref) = refs[:layers], refs[layers:]
        for layer, p_ref in enumerate(p_refs):
            @pl.when(pl.program_id(0) == layer)
            def _(p_ref=p_ref):
                g = p_ref[0].astype(F32)
                for dev in range(1, N_DEV):
                    g = g + p_ref[dev].astype(F32)
                g_ref[...] = g
                d_ref[...], nm_ref[...], nv_ref[...] = _adamw_math(w_ref[...], g, m_ref[...], v_ref[...])

    def part_spec(layer):
        return pl.BlockSpec((N_DEV, tr, c), lambda l, i: (0, jnp.where(l < layer, 0, jnp.where(l == layer, i, n - 1)), 0))

    row = pl.BlockSpec((None, tr, c), lambda l, i: (l, i, 0))
    return pl.pallas_call(
        body, name=name, grid=(layers, n), in_specs=[part_spec(layer) for layer in range(layers)] + [row] * 3,
        out_specs=[row] * 4, out_shape=[jax.ShapeDtypeStruct((layers, r, c), F32)] * 4,
        compiler_params=_params(("arbitrary", "arbitrary"), 40 * _MIB),
    )(*parts, w, m, v)


def _tile_rows(r, pref):
    t = min(r, pref)
    while r % t or t % 8:
        t -= 1
    return t


def _exchange(name, carries):
    return _call(lambda: None, name=name, grid=(1,), in_specs=[], out_specs=[], out_shape=[], operands=(), carries=carries)


def _all_reduce_small(name, groups):
    c = groups[0][0].shape[1]
    parts, offsets, starts, r = [], [], [], 0
    for group in groups:
        starts.append(r)
        for p in group:
            parts.append(p)
            offsets.append(r)
            r += p.shape[0]
        r = -(-r // 8) * 8
    n = len(parts)

    def body(*refs):
        out_ref, slots, send_sems, recv_sems = refs[n:]
        x, y, c_, me = _place()
        slots[me] = jnp.zeros((r, c), F32)
        for p_ref, off in zip(refs[:n], offsets):
            slots[me, off:off + p_ref.shape[0], :] = p_ref[...]

        here, sibling = (x, y, c_), (x, y, 1 - c_)
        chips = [(1 - x, y), (x, 1 - y), (1 - x, 1 - y)]

        def copy(sem, block, to):
            slot = slots.at[4 * block[0] + 2 * block[1] + block[2]]
            return pltpu.make_async_remote_copy(
                src_ref=slot, dst_ref=slot, send_sem=send_sems.at[sem], recv_sem=recv_sems.at[sem], device_id=to,
                device_id_type=MESH)

        sent = [copy(0, here, sibling)] + [copy(1 + j, here, (*chip, c_)) for j, chip in enumerate(chips)]
        for cp in sent:
            cp.start()
        for j, chip in enumerate(chips):
            copy(1 + j, (*chip, c_), here).wait_recv()
            sent.append(copy(4 + j, (*chip, c_), sibling))
            sent[-1].start()
        copy(0, sibling, here).wait_recv()
        for j, chip in enumerate(chips):
            copy(4 + j, (*chip, 1 - c_), here).wait_recv()
        for cp in sent:
            cp.wait_send()
        total = slots[0]
        for dev in range(1, N_DEV):
            total = total + slots[dev]
        out_ref[...] = total

    vm = pl.BlockSpec(memory_space=pltpu.VMEM)
    summed = pl.pallas_call(
        body, name=name, in_specs=[vm] * n, out_specs=vm, out_shape=jax.ShapeDtypeStruct((r, c), F32),
        scratch_shapes=[pltpu.VMEM((N_DEV, r, c), F32), pltpu.SemaphoreType.DMA((7,)), pltpu.SemaphoreType.DMA((7,))],
        compiler_params=_params(None, (N_DEV + 6) * r * c * 4 + 8 * _MIB),
    )(*parts)
    return summed, starts


def _adamw(name, g, w, m, v):
    cols = w.shape[-1]
    flat = lambda a: a.reshape(-1, cols)

    def body(g_ref, w_ref, m_ref, v_ref, d_ref, nm_ref, nv_ref):
        d_ref[...], nm_ref[...], nv_ref[...] = _adamw_math(w_ref[...], g_ref[...], m_ref[...], v_ref[...])

    outs = pl.pallas_call(body, name=name, out_shape=[jax.ShapeDtypeStruct(flat(w).shape, F32)] * 3)(
        flat(g), flat(w), flat(m), flat(v))
    return [o.reshape(w.shape) for o in outs]


def kernel(x, pre_mix_g, post_mix_g, pre_ffn_g, post_ffn_g, a_w_in, a_v_norm_g, a_w_spatial, a_b_spatial, a_w_out, kv_norm_g, w_k, w_v, b_w_q, b_w_o, ffn_w_up, ffn_conv_w, ffn_conv_b, ffn_w_down, loss_target, m_pre_mix_g, m_post_mix_g, m_pre_ffn_g, m_post_ffn_g, m_a_w_in, m_a_v_norm_g, m_a_w_spatial, m_a_b_spatial, m_a_w_out, m_kv_norm_g, m_w_k, m_w_v, m_b_w_q, m_b_w_o, m_ffn_w_up, m_ffn_conv_w, m_ffn_conv_b, m_ffn_w_down, v_pre_mix_g, v_post_mix_g, v_pre_ffn_g, v_post_ffn_g, v_a_w_in, v_a_v_norm_g, v_a_w_spatial, v_a_b_spatial, v_a_w_out, v_kv_norm_g, v_w_k, v_w_v, v_b_w_q, v_b_w_o, v_ffn_w_up, v_ffn_conv_w, v_ffn_conv_b, v_ffn_w_down):
    t, d = x.shape[1], x.shape[2]
    f = ffn_w_down.shape[1] * N_DEV
    ng = d // GROUP
    me = 4 * lax.axis_index("x") + 2 * lax.axis_index("y") + lax.axis_index("c")
    x2, target = x.reshape(t, d), loss_target.reshape(t, d)

    hn0, g_in, g_cw, g_vg = _norms_fwd("pre_mix0", x2, [pre_mix_g[0]], carries=[
        _gather(a_w_in[0].astype(BF)), _gather(ffn_conv_w.reshape(2 * CONV_TAPS, -1)), _gather(a_v_norm_g)])
    up0 = ffn_w_up[0].astype(BF)
    cw_full = jnp.transpose(g_cw.reshape(N_DEV, 2, CONV_TAPS, -1), (1, 2, 0, 3)).reshape(2, CONV_TAPS, 2, f)
    cw_l = [jnp.transpose(cw_full[l], (1, 0, 2)) for l in range(2)]
    cb_l = [ffn_conv_b[l].reshape(2, 1, f) for l in range(2)]
    vg_full = g_vg.reshape(1, d)
    w_s = a_w_spatial[0]
    w_st = jnp.swapaxes(w_s, 1, 2)
    b_st = a_b_spatial[0].T

    p0, (g_out, g_up0) = _mm_nn_blk("sgu_in", hn0, g_in, F32, carries=[
        _gather(a_w_out[0].astype(BF)), _gather(up0, 0, d // 4)])
    w_out_f = g_out.reshape(d, d)
    sg, g_up0 = _sgu_fwd("sgu", p0, vg_full, w_s, b_st, carries=[_gather(up0, d // 4, d // 2, dst=g_up0)])
    mix0, (g_up0,) = _mm_nn("sgu_out", sg, w_out_f, F32, carries=[_gather(up0, d // 2, 3 * d // 4, dst=g_up0)])
    h1, fn0, g_up0 = _resid_norms("post_mix0", x2, mix0, post_mix_g[0], [pre_ffn_g[0]],
                                  carries=[_gather(up0, 3 * d // 4, d, dst=g_up0)])
    a3_0, (g_dn0,) = _mm_nn_blk("ffn0_up", fn0, g_up0, F32, halves=True, carries=[_gather(ffn_w_down[0].astype(BF))])
    y0, (g_q,) = _conv_fwd("ffn0_conv", a3_0, cw_l[0], cb_l[0], carries=[_gather(b_w_q[0].astype(BF))])
    wv = w_v.astype(BF)
    f0, (g_k, g_v) = _mm_nn("ffn0_down", y0, g_dn0.reshape(f, d), F32,
                            carries=[_gather(w_k.astype(BF)), _gather(wv, 0, d // 16)])
    h2, hn1, kvn, g_v = _resid_norms("post_ffn0", h1, f0, post_ffn_g[0], [pre_mix_g[1], kv_norm_g],
                                     carries=[_gather(wv, d // 16, d // 8, dst=g_v)])
    w_q_f, w_k_f, w_v_f = g_q.reshape(d, d), g_k.reshape(d, d), g_v.reshape(d, d)
    up1 = ffn_w_up[1].astype(BF)
    e = d // 8
    q, (g_up1,) = _mm_nn("attn_q", hn1, w_q_f, BF, carries=[_gather(up1, 0, e)])
    kk, (g_up1,) = _mm_nn("attn_k", kvn, w_k_f, BF, carries=[_gather(up1, e, 2 * e, dst=g_up1)])
    vv, (g_up1,) = _mm_nn("attn_v", kvn, w_v_f, BF, carries=[_gather(up1, 2 * e, 3 * e, dst=g_up1)])
    (att, rest, first), (g_o, g_up1) = _attn_fwd(
        "attn", q, kk, vv, carries=[_gather(b_w_o[0].astype(BF)), _gather(up1, 3 * e, 6 * e, dst=g_up1)])
    w_o_f = g_o.reshape(d, d)
    mix1, (g_up1,) = _mm_nn("attn_o", att, w_o_f, F32, carries=[_gather(up1, 6 * e, 7 * e, dst=g_up1)])
    h3, fn1, g_up1 = _resid_norms("post_mix1", h2, mix1, post_mix_g[1], [pre_ffn_g[1]],
                                  carries=[_gather(up1, 7 * e, d, dst=g_up1)])
    a3_1, (g_dn1,) = _mm_nn_blk("ffn1_up", fn1, g_up1, F32, halves=True, carries=[_gather(ffn_w_down[1].astype(BF))])
    y1, _ = _conv_fwd("ffn1_conv", a3_1, cw_l[1], cb_l[1])
    f1 = _mm_nn("ffn1_down", y1, g_dn1.reshape(f, d), F32)
    g_up = (g_up0, g_up1)
    w_dn_f = (g_dn0.reshape(f, d), g_dn1.reshape(f, d))
    dh, loss_part, df1, d_post_ffn1 = _resid_loss("loss", h3, f1, post_ffn_g[1], target)
    loss = lax.psum(loss_part[0, 0], ("x", "y", "c"))

    def blocks(dw):
        return dw.reshape(N_DEV, -1, d)

    def split(result, carries):
        return result if carries else (result, [])

    def ffn_bwd(l, dh_out, dfo, h_in, fn, a3, yv, mix, with_dw=(), with_dx=(), with_up=()):
        dw_dn, sent_dw = split(_mm_tn(f"ffn{l}_down_dw", yv, dfo, carries=with_dw), with_dw)
        dy, sent_dx = split(_mm_nt(f"ffn{l}_down_dx", dfo, w_dn_f[l], BF, carries=with_dx), with_dx)
        dw_dn = blocks(dw_dn)
        cut = dw_dn.shape[1] * 5 // 8 // 16 * 16
        (da3, dcw, dcb), (p_dn,) = _conv_bwd(f"ffn{l}_conv_bwd", a3, cw_l[l], cb_l[l], dy, carries=[_scatter(dw_dn, 0, cut)])
        dw_up, (p_dn, *sent_up) = _mm_tn_blk(f"ffn{l}_up_dw", fn, da3, halves=True,
                                             carries=[_scatter(dw_dn, cut, dst=p_dn), *with_up])
        dfn, (p_up,) = _mm_nt_blk(f"ffn{l}_up_dx", da3, g_up[l], F32, halves=True, carries=[_scatter(dw_up, 0, d // 2)])
        dh_in, d_pre, dmix, d_post_mix, p_up = _norm_bwd(
            f"pre_ffn{l}_bwd", h_in, pre_ffn_g[l], dfn, res=dh_out, then=(mix, post_mix_g[l]),
            carries=[_scatter(dw_up, d // 2, 9 * d // 16, dst=p_up)])
        return dh_in, dmix, d_pre, d_post_mix, dcw, dcb, p_dn, dw_up, p_up, list(sent_dw) + list(sent_dx) + list(sent_up)

    dh3, dmix1, d_pre_ffn1, d_post_mix1, dcw1, dcb1, p_dn1, dw_up1, p_up1, _ = ffn_bwd(
        1, dh, df1, h3, fn1, a3_1, y1, mix1)
    dw_o = _mm_tn("attn_o_dw", att, dmix1)
    datt = _mm_nt("attn_o_dx", dmix1, w_o_f, BF)
    (dq, dk, dv), (p_up1,) = _attn_bwd(
        "attn_bwd", q, kk, vv, rest, first, datt, carries=[_scatter(dw_up1, 9 * d // 16, d, dst=p_up1)])
    dw_o, qr = blocks(dw_o), d // 32
    dw_q, (p_o,) = _mm_tn("attn_q_dw", hn1, dq, carries=[_scatter(dw_o, 0, qr)])
    dw_k, (p_o,) = _mm_tn("attn_k_dw", kvn, dk, carries=[_scatter(dw_o, qr, 2 * qr, dst=p_o)])
    dw_v, (p_o,) = _mm_tn("attn_v_dw", kvn, dv, carries=[_scatter(dw_o, 2 * qr, 3 * qr, dst=p_o)])
    dhn1, (p_o,) = _mm_nt("attn_q_dx", dq, w_q_f, F32, carries=[_scatter(dw_o, 3 * qr, 4 * qr, dst=p_o)])
    dkvn = _mm_nt("attn_v_dx", dv, w_v_f, F32, add=_mm_nt("attn_k_dx", dk, w_k_f, F32))
    dh2, d_pre_mix1, d_kv, df0, d_post_ffn0 = _norm_bwd(
        "pre_mix1_kv_bwd", h2, [pre_mix_g[1], kv_norm_g], [dhn1, dkvn], res=dh3, then=(f0, post_ffn_g[0]))
    dh1, dmix0, d_pre_ffn0, d_post_mix0, dcw0, dcb0, p_dn0, dw_up0, p_up0, (p_q, p_k, p_v) = ffn_bwd(
        0, dh2, df0, h1, fn0, a3_0, y0, mix0, with_dw=[_scatter(blocks(dw_q))], with_dx=[_scatter(blocks(dw_k))],
        with_up=[_scatter(blocks(dw_v))])
    dw_out, (p_up0,) = _mm_tn("sgu_out_dw", sg, dmix0, carries=[_scatter(dw_up0, 9 * d // 16, 11 * d // 16, dst=p_up0)])
    dsg = _mm_nt("sgu_out_dx", dmix0, w_out_f, BF)
    (dp0, d_vg, d_ws, d_bst), (p_up0,) = _sgu_bwd(
        "sgu_bwd", p0, vg_full, w_s, w_st, b_st, dsg, carries=[_scatter(dw_up0, 11 * d // 16, 15 * d // 16, dst=p_up0)])
    dw_in, (p_out,) = _mm_tn_blk("sgu_in_dw", hn0, dp0, carries=[_scatter(blocks(dw_out))])
    dhn0, (p_in,) = _mm_nt_blk("sgu_in_dx", dp0, g_in, F32, carries=[_scatter(dw_in, 0, 5 * d // 8)])
    grad_x, d_pre_mix0, p_in = _norm_bwd("pre_mix0_bwd", x2, pre_mix_g[0], dhn0, res=dh1,
                                         carries=[_scatter(dw_in, 5 * d // 8, 7 * d // 8, dst=p_in)])
    p_up0, p_in = _exchange("scatter_last", [_scatter(dw_up0, 15 * d // 16, d, dst=p_up0),
                                             _scatter(dw_in, 7 * d // 8, d, dst=p_in)])

    def conv_w_grad(dcw):
        return jnp.transpose(dcw, (1, 0, 2)).reshape(CONV_TAPS, 2 * f)

    small = [
        ([d_pre_mix0, d_pre_mix1], (2, d)), ([d_post_mix0, d_post_mix1], (2, d)),
        ([d_pre_ffn0, d_pre_ffn1], (2, d)), ([d_post_ffn0, d_post_ffn1], (2, d)),
        ([d_kv], (d,)), ([d_vg], (1, d)), ([d_bst.T], (1, ng, GROUP)), ([d_ws], (1, ng, GROUP, GROUP)),
        ([dcb0, dcb1], (2, 2 * f)), ([conv_w_grad(dcw0), conv_w_grad(dcw1)], (2, CONV_TAPS, 2 * f)),
    ]
    width = V7X_LANES * math.gcd(d // V7X_LANES, 2 * f // V7X_LANES)
    summed, offsets = _all_reduce_small("reduce_small", [[a.reshape(-1, width) for a in group] for group, _ in small])
    full = [summed[off:off + math.prod(shape) // width].reshape(shape) for off, (_, shape) in zip(offsets, small)]
    g_pre_mix, g_post_mix, g_pre_ffn, g_post_ffn, g_kv, g_vgain, g_bs, g_ws, g_cb, g_cwf = full
    cw_w = 2 * f // N_DEV
    g_vgain = lax.dynamic_slice_in_dim(g_vgain, me * (d // N_DEV), d // N_DEV, axis=1)
    g_cwf = lax.dynamic_slice_in_dim(g_cwf, me * cw_w, cw_w, axis=2)

    parts = [p_in, p_out, p_k, p_v, p_q, p_o, p_up0, p_up1, p_dn0, p_dn1]

    def small_update(name, g, w, m, v):
        return [g] + _adamw(name, g, w, m, v)

    def stacked(name, part0, part1, w, m, v):
        shape = (2, *part0.shape[1:])
        return [o.reshape(w.shape) for o in _sum_adamw(name, [part0, part1], w.reshape(shape), m.reshape(shape), v.reshape(shape))]

    def single(name, part, w, m, v):
        shape = (1, *part.shape[1:])
        return [o.reshape(w.shape) for o in _sum_adamw(name, [part], w.reshape(shape), m.reshape(shape), v.reshape(shape))]

    results = {
        "pre_mix_g": small_update("adam_pre_mix", g_pre_mix, pre_mix_g, m_pre_mix_g, v_pre_mix_g),
        "post_mix_g": small_update("adam_post_mix", g_post_mix, post_mix_g, m_post_mix_g, v_post_mix_g),
        "pre_ffn_g": small_update("adam_pre_ffn", g_pre_ffn, pre_ffn_g, m_pre_ffn_g, v_pre_ffn_g),
        "post_ffn_g": small_update("adam_post_ffn", g_post_ffn, post_ffn_g, m_post_ffn_g, v_post_ffn_g),
        "a_w_in": single("adam_a_w_in", parts[0], a_w_in, m_a_w_in, v_a_w_in),
        "a_v_norm_g": small_update("adam_a_v_norm", g_vgain, a_v_norm_g, m_a_v_norm_g, v_a_v_norm_g),
        "a_w_spatial": small_update("adam_a_w_spatial", g_ws, a_w_spatial, m_a_w_spatial, v_a_w_spatial),
        "a_b_spatial": small_update("adam_a_b_spatial", g_bs, a_b_spatial, m_a_b_spatial, v_a_b_spatial),
        "a_w_out": single("adam_a_w_out", parts[1], a_w_out, m_a_w_out, v_a_w_out),
        "kv_norm_g": small_update("adam_kv_norm", g_kv, kv_norm_g, m_kv_norm_g, v_kv_norm_g),
        "w_k": single("adam_w_k", parts[2], w_k, m_w_k, v_w_k),
        "w_v": single("adam_w_v", parts[3], w_v, m_w_v, v_w_v),
        "b_w_q": single("adam_b_w_q", parts[4], b_w_q, m_b_w_q, v_b_w_q),
        "b_w_o": single("adam_b_w_o", parts[5], b_w_o, m_b_w_o, v_b_w_o),
        "ffn_w_up": stacked("adam_ffn_w_up", parts[6], parts[7], ffn_w_up, m_ffn_w_up, v_ffn_w_up),
        "ffn_conv_w": small_update("adam_ffn_conv_w", g_cwf, ffn_conv_w, m_ffn_conv_w, v_ffn_conv_w),
        "ffn_conv_b": small_update("adam_ffn_conv_b", g_cb, ffn_conv_b, m_ffn_conv_b, v_ffn_conv_b),
        "ffn_w_down": stacked("adam_ffn_w_down", parts[8], parts[9], ffn_w_down, m_ffn_w_down, v_ffn_w_down),
    }
    order = ["pre_mix_g", "post_mix_g", "pre_ffn_g", "post_ffn_g", "a_w_in", "a_v_norm_g", "a_w_spatial", "a_b_spatial",
             "a_w_out", "kv_norm_g", "w_k", "w_v", "b_w_q", "b_w_o", "ffn_w_up", "ffn_conv_w", "ffn_conv_b", "ffn_w_down"]
    outs = [loss, grad_x.reshape(x.shape)]
    for idx in range(4):
        outs += [results[n][idx] for n in order]
    return tuple(outs)
```

```python
import functools
import math
from typing import NamedTuple, Optional

import jax
import jax.numpy as jnp
from jax import lax
from jax.experimental import pallas as pl
from jax.experimental.pallas import tpu as pltpu

F32 = jnp.float32
BF = jnp.bfloat16
MESH = pl.DeviceIdType.MESH

N_DEV = 8
NORM_EPS = 1e-6
GROUP = 128
CONV_TAPS = 3
ADAM_LR, ADAM_B1, ADAM_B2, ADAM_EPS, ADAM_WD, ADAM_STEP = 0.001, 0.9, 0.999, 1e-08, 0.01, 10
EXP_FLOOR = -104.0

V7X_LANES = 128
V7X_VMEM_BYTES = 64 * 1024 * 1024
_MIB = 1024 * 1024

_NN = (((1,), (0,)), ((), ()))
_NT = (((1,), (1,)), ((), ()))
_TN = (((0,), (0,)), ((), ()))


def _tile(n, pref):
    if n <= pref:
        return n
    t = (pref // V7X_LANES) * V7X_LANES
    while t > V7X_LANES and n % t:
        t -= V7X_LANES
    assert n % t == 0, (n, pref)
    return t


def _nbytes(shape, dtype):
    return math.prod(shape) * jnp.dtype(dtype).itemsize


def _params(sem=None, vmem=None):
    kw = {}
    if sem is not None:
        kw["dimension_semantics"] = sem
    if vmem is not None:
        kw["vmem_limit_bytes"] = int(min(max(vmem, 16 * _MIB), V7X_VMEM_BYTES - 8 * _MIB))
    return pltpu.CompilerParams(**kw)


def _place():
    x, y, c = lax.axis_index("x"), lax.axis_index("y"), lax.axis_index("c")
    return x, y, c, 4 * x + 2 * y + c


def _flip(x, y, c, k):
    return (1 - x if k & 4 else x, 1 - y if k & 2 else y, 1 - c if k & 1 else c)


class _Carry(NamedTuple):
    gather: bool
    src: jax.Array
    dst: Optional[jax.Array]
    lo: int
    hi: int


def _gather(src, lo=0, hi=None, dst=None):
    return _Carry(True, src, dst, lo, src.shape[0] if hi is None else hi)


def _scatter(src, lo=0, hi=None, dst=None):
    return _Carry(False, src, dst, lo, src.shape[1] if hi is None else hi)


def _carry_phases(carries, srcs, dsts, send_sems, recv_sems, local_sems):
    x, y, c, me = _place()
    here, sibling = (x, y, c), (x, y, 1 - c)
    chips = [(1 - x, y), (x, 1 - y), (1 - x, 1 - y)]

    def rows(u):
        return pl.ds(carries[u].lo, carries[u].hi - carries[u].lo)

    def block_copy(u, sem, block, to, from_src=False):
        slot = dsts[u].at[4 * block[0] + 2 * block[1] + block[2], rows(u)]
        return pltpu.make_async_remote_copy(
            src_ref=srcs[u].at[rows(u)] if from_src else slot, dst_ref=slot, send_sem=send_sems.at[u, sem],
            recv_sem=recv_sems.at[u, sem], device_id=to, device_id_type=MESH)

    def partial_copy(u, k):
        peer = _flip(x, y, c, k)
        return pltpu.make_async_remote_copy(
            src_ref=srcs[u].at[4 * peer[0] + 2 * peer[1] + peer[2], rows(u)], dst_ref=dsts[u].at[me, rows(u)],
            send_sem=send_sems.at[u, k - 1], recv_sem=recv_sems.at[u, k - 1], device_id=peer, device_id_type=MESH)

    def local_copy(u):
        src = srcs[u].at[rows(u)] if carries[u].gather else srcs[u].at[me, rows(u)]
        return pltpu.make_async_copy(src, dsts[u].at[me, rows(u)], local_sems.at[u])

    def first():
        for u, cr in enumerate(carries):
            local_copy(u).start()
            if cr.gather:
                block_copy(u, 0, here, sibling, from_src=True).start()
                for j, chip in enumerate(chips):
                    block_copy(u, 1 + j, here, (*chip, c), from_src=True).start()
            else:
                for k in range(1, N_DEV):
                    partial_copy(u, k).start()

    def middle():
        for u, cr in enumerate(carries):
            if cr.gather:
                for j, chip in enumerate(chips):
                    block_copy(u, 1 + j, (*chip, c), here).wait_recv()
                    block_copy(u, 4 + j, (*chip, c), sibling).start()

    def last():
        for u, cr in enumerate(carries):
            if cr.gather:
                block_copy(u, 0, sibling, here).wait_recv()
                for j, chip in enumerate(chips):
                    block_copy(u, 4 + j, (*chip, 1 - c), here).wait_recv()
                block_copy(u, 0, here, sibling, from_src=True).wait_send()
                for j, chip in enumerate(chips):
                    block_copy(u, 1 + j, here, (*chip, c), from_src=True).wait_send()
                    block_copy(u, 4 + j, (*chip, c), sibling).wait_send()
            else:
                for k in range(1, N_DEV):
                    partial_copy(u, k).wait()
            local_copy(u).wait()

    return first, middle, last


def _call(body, *, name, grid, in_specs, out_specs, out_shape, operands, scratch=(), sem=None, vmem=None,
          carries=(), middle_at=0.6):
    if not carries:
        return pl.pallas_call(
            body, name=name, grid=grid, in_specs=in_specs, out_specs=out_specs, out_shape=out_shape,
            scratch_shapes=list(scratch), compiler_params=_params(sem, vmem))(*operands)
    n_in, n_out, n_scr, nc = len(in_specs), len(out_specs), len(scratch), len(carries)
    given = [u for u, cr in enumerate(carries) if cr.dst is not None]
    steps = math.prod(grid)
    middle_step = min(steps - 1, int(steps * middle_at))

    def wrapped(*refs):
        ins, srcs = refs[:n_in], refs[n_in:n_in + nc]
        at = n_in + nc + len(given)
        outs, dsts = refs[at:at + n_out], refs[at + n_out:at + n_out + nc]
        at += n_out + nc
        scr, (send_sems, recv_sems, local_sems) = refs[at:at + n_scr], refs[at + n_scr:]
        first, middle, last = _carry_phases(carries, srcs, dsts, send_sems, recv_sems, local_sems)
        step = 0
        for axis, size in enumerate(grid):
            step = step * size + pl.program_id(axis)
        pl.when(step == 0)(first)
        body(*ins, *outs, *scr)
        pl.when(step == middle_step)(middle)
        pl.when(step == steps - 1)(last)

    any_spec = pl.BlockSpec(memory_space=pl.ANY)
    dst_shapes = [jax.ShapeDtypeStruct((N_DEV, *cr.src.shape) if cr.gather else cr.src.shape, cr.src.dtype) for cr in carries]
    return pl.pallas_call(
        wrapped, name=name, grid=grid, in_specs=list(in_specs) + [any_spec] * (nc + len(given)),
        out_specs=list(out_specs) + [any_spec] * nc, out_shape=list(out_shape) + dst_shapes,
        input_output_aliases={n_in + nc + g: n_out + u for g, u in enumerate(given)},
        scratch_shapes=list(scratch) + [pltpu.SemaphoreType.DMA((nc, 7)), pltpu.SemaphoreType.DMA((nc, 7)),
                                        pltpu.SemaphoreType.DMA((nc,))],
        compiler_params=_params(("arbitrary",) * len(grid), vmem),
    )(*operands, *[cr.src for cr in carries], *[carries[u].dst for u in given])


def _mm(name, a, b, *, dims, grid, a_blk, a_map, b_blk, b_map, o_blk, o_map, out_shape, out_dtype,
        add=None, add_blk=None, add_map=None, carries=(), b_slabs=1):
    nk = grid[2]
    assert add is None or nk == 1
    acc_shape = tuple(d for d in o_blk if d is not None)
    in_place = out_dtype == F32

    def body(*refs):
        if add is None:
            a_ref, b_ref, o_ref = refs[:3]
            c_ref, scr = None, refs[3:]
        else:
            a_ref, b_ref, c_ref, o_ref = refs[:4]
            scr = refs[4:]
        if b_slabs == 1:
            part = lax.dot_general(a_ref[...], b_ref[...], dims, preferred_element_type=F32)
        else:
            cw = b_ref.shape[2]
            part = sum(lax.dot_general(a_ref[:, s * cw:(s + 1) * cw], b_ref[s], dims, preferred_element_type=F32)
                       for s in range(b_slabs))
        if c_ref is not None:
            part = part + c_ref[...].astype(F32)
        if nk == 1:
            o_ref[...] = part.astype(o_ref.dtype)
            return
        acc = o_ref if in_place else scr[0]
        k = pl.program_id(2)

        @pl.when(k == 0)
        def _():
            acc[...] = part

        @pl.when(k > 0 if in_place else jnp.logical_and(k > 0, k < nk - 1))
        def _():
            acc[...] += part

        if not in_place:
            @pl.when(k == nk - 1)
            def _():
                o_ref[...] = (acc[...] + part).astype(o_ref.dtype)

    in_specs = [pl.BlockSpec(a_blk, a_map), pl.BlockSpec(b_blk, b_map)]
    operands = [a, b]
    scratch = [pltpu.VMEM(acc_shape, F32)] if nk > 1 and not in_place else []
    vmem = 2 * (_nbytes(acc_shape, out_dtype) + _nbytes([d for d in a_blk if d], a.dtype)
                + _nbytes([d for d in b_blk if d], b.dtype)) + (2 + len(scratch)) * _nbytes(acc_shape, F32)
    if add is not None:
        in_specs.append(pl.BlockSpec(add_blk, add_map))
        operands.append(add)
        vmem += 2 * _nbytes(acc_shape, add.dtype)
    out, *dsts = _call(
        body, name=name, grid=grid, in_specs=in_specs, out_specs=[pl.BlockSpec(o_blk, o_map)],
        out_shape=[jax.ShapeDtypeStruct(out_shape, out_dtype)], operands=operands, scratch=scratch,
        sem=("parallel", "parallel", "arbitrary"), vmem=vmem + 8 * _MIB, carries=carries)
    return (out, dsts) if carries else out


def _mm_nn(name, x, w, out_dtype, carries=()):
    t, kd = x.shape
    n = w.shape[1]
    tm, tn, tk = _tile(t, 1024), _tile(n, 2048 if kd > 2048 else 1024), _tile(kd, 1536 if kd > 2048 else 2048)
    return _mm(name, x, w, dims=_NN, grid=(t // tm, n // tn, kd // tk),
               a_blk=(tm, tk), a_map=lambda i, j, k: (i, k), b_blk=(tk, tn), b_map=lambda i, j, k: (k, j),
               o_blk=(tm, tn), o_map=lambda i, j, k: (i, j), out_shape=(t, n), out_dtype=out_dtype, carries=carries)


def _mm_nn_blk(name, x, g, out_dtype, halves=False, carries=()):
    t, kd = x.shape
    cw = g.shape[2]
    tm = _tile(t, 1024)
    if halves:
        o_blk, o_map, out_shape = (None, tm, cw), (lambda i, j, k: (j // 4, i, j % 4)), (2, t, 4 * cw)
    else:
        o_blk, o_map, out_shape = (tm, cw), (lambda i, j, k: (i, j)), (t, N_DEV * cw)
    return _mm(name, x, g, dims=_NN, grid=(t // tm, N_DEV, 1),
               a_blk=(tm, kd), a_map=lambda i, j, k: (i, 0), b_blk=(None, kd, cw), b_map=lambda i, j, k: (j, 0, 0),
               o_blk=o_blk, o_map=o_map, out_shape=out_shape, out_dtype=out_dtype, carries=carries)


def _mm_nt(name, dy, w, out_dtype, add=None, carries=()):
    t, n = dy.shape
    kd = w.shape[0]
    tm, tn = _tile(t, 1024), _tile(kd, 1408)
    kw = {}
    if add is not None:
        kw = dict(add=add, add_blk=(tm, tn), add_map=lambda i, j, k: (i, j))
    return _mm(name, dy, w, dims=_NT, grid=(t // tm, kd // tn, 1),
               a_blk=(tm, n), a_map=lambda i, j, k: (i, 0), b_blk=(tn, n), b_map=lambda i, j, k: (j, 0),
               o_blk=(tm, tn), o_map=lambda i, j, k: (i, j), out_shape=(t, kd), out_dtype=out_dtype, carries=carries, **kw)


def _mm_nt_blk(name, dy, g, out_dtype, halves=False, carries=()):
    kd, cw = g.shape[1], g.shape[2]
    t = dy.shape[1] if halves else dy.shape[0]
    tm = _tile(t, 512)
    slabs = 2 if cw > 512 else 4
    nk = N_DEV // slabs
    if halves:
        a_blk, a_map = (None, tm, slabs * cw), (lambda i, j, k: (k // (nk // 2), i, k % (nk // 2)))
    else:
        a_blk, a_map = (tm, slabs * cw), (lambda i, j, k: (i, k))
    return _mm(name, dy, g, dims=_NT, grid=(t // tm, 1, nk), b_slabs=slabs,
               a_blk=a_blk, a_map=a_map, b_blk=(slabs, kd, cw), b_map=lambda i, j, k: (k, 0, 0),
               o_blk=(tm, kd), o_map=lambda i, j, k: (i, 0), out_shape=(t, kd), out_dtype=out_dtype, carries=carries)


def _mm_tn(name, x, dy, carries=()):
    t, kd = x.shape
    n = dy.shape[1]
    tmx, tk = _tile(kd, 1024), _tile(t, 2048)
    return _mm(name, x, dy, dims=_TN, grid=(kd // tmx, 1, t // tk),
               a_blk=(tk, tmx), a_map=lambda i, j, k: (k, i), b_blk=(tk, n), b_map=lambda i, j, k: (k, 0),
               o_blk=(tmx, n), o_map=lambda i, j, k: (i, 0), out_shape=(kd, n), out_dtype=BF, carries=carries)


def _mm_tn_blk(name, x, dy, halves=False, carries=()):
    t, kd = x.shape
    cw = dy.shape[2] // 4 if halves else dy.shape[1] // N_DEV
    tmx, tk = _tile(kd, 2048 if cw <= 512 else 1024), _tile(t, 2048)
    if halves:
        b_blk, b_map = (None, tk, cw), (lambda i, j, k: (j // 4, k, j % 4))
    else:
        b_blk, b_map = (tk, cw), (lambda i, j, k: (k, j))
    return _mm(name, x, dy, dims=_TN, grid=(kd // tmx, N_DEV, t // tk),
               a_blk=(tk, tmx), a_map=lambda i, j, k: (k, i), b_blk=b_blk, b_map=b_map,
               o_blk=(None, tmx, cw), o_map=lambda i, j, k: (j, i, 0), out_shape=(N_DEV, kd, cw), out_dtype=BF,
               carries=carries)


def _rms(x, g):
    r = lax.rsqrt(jnp.mean(x * x, axis=-1, keepdims=True) + NORM_EPS)
    return x * r * g


def _rms_bwd_math(x, g, dy):
    d = x.shape[-1]
    r = lax.rsqrt(jnp.mean(x * x, axis=-1, keepdims=True) + NORM_EPS)
    xh = x * r
    u = dy * g
    dx = r * u - xh * (jnp.sum(xh * u, axis=-1, keepdims=True) * (r / d))
    return dx, jnp.sum(dy * xh, axis=0, keepdims=True)


def _row_specs(tr, d, n):
    return [pl.BlockSpec((tr, d), lambda i: (i, 0)) for _ in range(n)]


def _vec_specs(d, n):
    return [pl.BlockSpec((1, d), lambda i: (0, 0)) for _ in range(n)]


def _norms_fwd(name, h, gains, carries=()):
    t, d = h.shape
    tr, ng = _tile(t, 256), len(gains)

    def body(h_ref, *refs):
        x = h_ref[...]
        for g_ref, o_ref in zip(refs[:ng], refs[ng:]):
            o_ref[...] = _rms(x, g_ref[...]).astype(BF)

    return _call(
        body, name=name, grid=(t // tr,), in_specs=_row_specs(tr, d, 1) + _vec_specs(d, ng),
        out_specs=_row_specs(tr, d, ng), out_shape=[jax.ShapeDtypeStruct((t, d), BF)] * ng,
        operands=(h, *[g.reshape(1, d) for g in gains]), sem=("parallel",), carries=carries)


def _resid_norms(name, h, m, g_post, gains, carries=()):
    t, d = h.shape
    tr, ng = _tile(t, 256), len(gains)

    def body(h_ref, m_ref, gp_ref, *refs):
        hn = h_ref[...] + _rms(m_ref[...], gp_ref[...])
        refs[ng][...] = hn
        for g_ref, o_ref in zip(refs[:ng], refs[ng + 1:]):
            o_ref[...] = _rms(hn, g_ref[...]).astype(BF)

    return _call(
        body, name=name, grid=(t // tr,), in_specs=_row_specs(tr, d, 2) + _vec_specs(d, 1 + ng),
        out_specs=_row_specs(tr, d, 1 + ng),
        out_shape=[jax.ShapeDtypeStruct((t, d), F32)] + [jax.ShapeDtypeStruct((t, d), BF)] * ng,
        operands=(h, m, g_post.reshape(1, d), *[g.reshape(1, d) for g in gains]), sem=("parallel",), carries=carries)


def _resid_loss(name, h, m, g_post, target):
    t, d = h.shape
    tr = _tile(t, 256)

    def body(h_ref, m_ref, gp_ref, t_ref, dy_ref, loss_ref, dm_ref, dg_ref):
        mv = m_ref[...]
        diff = h_ref[...] + _rms(mv, gp_ref[...]) - t_ref[...]
        dy = diff * (1.0 / d)
        dy_ref[...] = dy
        dm, dg = _rms_bwd_math(mv, gp_ref[...], dy)
        dm_ref[...] = dm.astype(BF)

        @pl.when(pl.program_id(0) == 0)
        def _():
            loss_ref[...] = jnp.zeros_like(loss_ref)
            dg_ref[...] = jnp.zeros_like(dg_ref)

        per_row = jnp.sum(diff * diff, axis=-1, keepdims=True) * (1.0 / d)
        loss_ref[...] += 0.5 * jnp.sum(per_row, axis=0, keepdims=True)
        dg_ref[...] += dg

    row = pl.BlockSpec((tr, d), lambda i: (i, 0))
    return pl.pallas_call(
        body, name=name, grid=(t // tr,),
        in_specs=_row_specs(tr, d, 2) + _vec_specs(d, 1) + _row_specs(tr, d, 1),
        out_specs=[row, pl.BlockSpec((1, 1), lambda i: (0, 0)), row, pl.BlockSpec((1, d), lambda i: (0, 0))],
        out_shape=[jax.ShapeDtypeStruct((t, d), F32), jax.ShapeDtypeStruct((1, 1), F32),
                   jax.ShapeDtypeStruct((t, d), BF), jax.ShapeDtypeStruct((1, d), F32)],
        compiler_params=_params(("arbitrary",)),
    )(h, m, g_post.reshape(1, d), target)


def _norm_bwd(name, x, g, dy, res=None, out_dtype=F32, then=None, carries=()):
    t, d = x.shape
    tr = _tile(t, 256)
    has_res, has_then = res is not None, then is not None
    gains, dys = (g, dy) if isinstance(g, (list, tuple)) else ([g], [dy])
    n = len(gains)

    def accumulate(dg_ref, dg):
        @pl.when(pl.program_id(0) == 0)
        def _():
            dg_ref[...] = jnp.zeros_like(dg_ref)

        dg_ref[...] += dg

    def body(x_ref, *refs):
        dy_refs, g_refs, rest = refs[:n], refs[n:2 * n], list(refs[2 * n:])
        dx = rest.pop(0)[...] if has_res else 0.0
        then_refs = (rest.pop(0), rest.pop(0)) if has_then else None
        dx_ref, dg_refs = rest[0], rest[1:1 + n]
        xv = x_ref[...].astype(F32)
        for dy_ref, g_ref, dg_ref in zip(dy_refs, g_refs, dg_refs):
            dx_one, dg = _rms_bwd_math(xv, g_ref[...], dy_ref[...].astype(F32))
            dx = dx + dx_one
            accumulate(dg_ref, dg)
        dx_ref[...] = dx.astype(dx_ref.dtype)
        if has_then:
            d2, dg2 = _rms_bwd_math(then_refs[0][...], then_refs[1][...], dx)
            rest[1 + n][...] = d2.astype(BF)
            accumulate(rest[2 + n], dg2)

    ops = [x, *dys, *[gain.reshape(1, d) for gain in gains]] + ([res] if has_res else [])
    in_specs = _row_specs(tr, d, 1 + n) + _vec_specs(d, n) + _row_specs(tr, d, int(has_res))
    out_specs = [pl.BlockSpec((tr, d), lambda i: (i, 0))] + [pl.BlockSpec((1, d), lambda i: (0, 0))] * n
    out_shape = [jax.ShapeDtypeStruct((t, d), out_dtype)] + [jax.ShapeDtypeStruct((1, d), F32)] * n
    if has_then:
        ops += [then[0], then[1].reshape(1, d)]
        in_specs += _row_specs(tr, d, 1) + _vec_specs(d, 1)
        out_specs += [pl.BlockSpec((tr, d), lambda i: (i, 0)), pl.BlockSpec((1, d), lambda i: (0, 0))]
        out_shape += [jax.ShapeDtypeStruct((t, d), BF), jax.ShapeDtypeStruct((1, d), F32)]
    return _call(body, name=name, grid=(t // tr,), in_specs=in_specs, out_specs=out_specs, out_shape=out_shape,
                 operands=ops, sem=("arbitrary",), carries=carries)


_GELU_C = math.sqrt(2.0 / math.pi)
_GELU_A = 0.044715


def _gelu(x):
    return 0.5 * x * (1.0 + jnp.tanh(_GELU_C * (x + _GELU_A * x * x * x)))


def _gelu_and_grad(x):
    th = jnp.tanh(_GELU_C * (x + _GELU_A * x * x * x))
    grad = 0.5 * (1.0 + th) + 0.5 * x * (1.0 - th * th) * (_GELU_C * (1.0 + 3.0 * _GELU_A * x * x))
    return 0.5 * x * (1.0 + th), grad


def _causal(n):
    return lax.broadcasted_iota(jnp.int32, (n, n), 1) <= lax.broadcasted_iota(jnp.int32, (n, n), 0)


def _sgu_fwd(name, p, v_gain, w_s, b_st, carries=()):
    t, da2 = p.shape
    da = da2 // 2
    ng = da // GROUP

    def body(p_ref, vg_ref, ws_ref, bst_ref, o_ref):
        keep = _causal(GROUP)
        for g in range(ng):
            lo = g * GROUP
            u = _gelu(p_ref[:, lo:lo + GROUP])
            vn = _rms(_gelu(p_ref[:, da + lo:da + lo + GROUP]), vg_ref[:, lo:lo + GROUP])
            w = jnp.where(keep, ws_ref[g], 0.0).astype(BF)
            mixed = jnp.dot(w, vn.astype(BF), preferred_element_type=F32) + bst_ref[:, g:g + 1]
            o_ref[:, lo:lo + GROUP] = (u * mixed).astype(BF)

    return _call(
        body, name=name, grid=(t // GROUP,),
        in_specs=[pl.BlockSpec((GROUP, da2), lambda i: (i, 0)), pl.BlockSpec((1, da), lambda i: (0, 0)),
                  pl.BlockSpec((ng, GROUP, GROUP), lambda i: (0, 0, 0)), pl.BlockSpec((GROUP, ng), lambda i: (0, 0))],
        out_specs=[pl.BlockSpec((GROUP, da), lambda i: (i, 0))], out_shape=[jax.ShapeDtypeStruct((t, da), BF)],
        operands=(p, v_gain, w_s, b_st), sem=("parallel",), carries=carries)


def _sgu_bwd(name, p, v_gain, w_s, w_st, b_st, dout, carries=()):
    t, da2 = p.shape
    da = da2 // 2
    ng = da // GROUP

    def body(p_ref, vg_ref, ws_ref, wst_ref, bst_ref, do_ref, dp_ref, dvg_ref, dws_ref, dbst_ref):
        @pl.when(pl.program_id(0) == 0)
        def _():
            dvg_ref[...] = jnp.zeros_like(dvg_ref)
            dws_ref[...] = jnp.zeros_like(dws_ref)
            dbst_ref[...] = jnp.zeros_like(dbst_ref)

        keep = _causal(GROUP)
        keep_t = lax.broadcasted_iota(jnp.int32, (GROUP, GROUP), 0) <= lax.broadcasted_iota(jnp.int32, (GROUP, GROUP), 1)
        for g in range(ng):
            lo = g * GROUP
            u, du = _gelu_and_grad(p_ref[:, lo:lo + GROUP])
            v, dv_act = _gelu_and_grad(p_ref[:, da + lo:da + lo + GROUP])
            gain = vg_ref[:, lo:lo + GROUP]
            r = lax.rsqrt(jnp.mean(v * v, axis=-1, keepdims=True) + NORM_EPS)
            vh = v * r
            vnb = (vh * gain).astype(BF)
            w = jnp.where(keep, ws_ref[g], 0.0).astype(BF)
            wt = jnp.where(keep_t, wst_ref[g], 0.0).astype(BF)
            mixed = jnp.dot(w, vnb, preferred_element_type=F32) + bst_ref[:, g:g + 1]
            dout_g = do_ref[:, lo:lo + GROUP].astype(F32)
            dmixed = dout_g * u
            dmb = dmixed.astype(BF)
            dbst_ref[:, g:g + 1] += jnp.sum(dmixed, axis=1, keepdims=True)
            dws_ref[g] += jnp.where(keep, lax.dot_general(dmb, vnb, _NT, preferred_element_type=F32), 0.0)
            dvn = jnp.dot(wt, dmb, preferred_element_type=F32)
            dvg_ref[:, lo:lo + GROUP] += jnp.sum(dvn * vh, axis=0, keepdims=True)
            dvh = dvn * gain
            dv = r * dvh - vh * (jnp.sum(vh * dvh, axis=-1, keepdims=True) * (r / GROUP))
            dp_ref[:, lo:lo + GROUP] = (dout_g * mixed * du).astype(BF)
            dp_ref[:, da + lo:da + lo + GROUP] = (dv * dv_act).astype(BF)

    full = lambda *shape: pl.BlockSpec(shape, lambda i: (0,) * len(shape))
    outs = _call(
        body, name=name, grid=(t // GROUP,),
        in_specs=[pl.BlockSpec((GROUP, da2), lambda i: (i, 0)), full(1, da), full(ng, GROUP, GROUP), full(ng, GROUP, GROUP),
                  full(GROUP, ng), pl.BlockSpec((GROUP, da), lambda i: (i, 0))],
        out_specs=[pl.BlockSpec((GROUP, da2), lambda i: (i, 0)), full(1, da), full(ng, GROUP, GROUP), full(GROUP, ng)],
        out_shape=[jax.ShapeDtypeStruct((t, da2), BF), jax.ShapeDtypeStruct((1, da), F32),
                   jax.ShapeDtypeStruct((ng, GROUP, GROUP), F32), jax.ShapeDtypeStruct((GROUP, ng), F32)],
        operands=(p, v_gain, w_s, w_st, b_st, dout), sem=("arbitrary",), carries=carries)
    return outs[:4], outs[4:]


_CONV_ROWS = 256
_CONV_COLS = 128


def _shift_down(x, prev, k):
    top = pltpu.roll(jnp.concatenate([prev, x[:8]], axis=0), k, 0)[8:16]
    return jnp.concatenate([top, pltpu.roll(x, k, 0)[8:]], axis=0)


def _conv_taps(a_ref, half, r0, first):
    rows = _rows(a_ref)
    x = a_ref[half, pl.ds(r0, rows), :]
    if first:
        prev = jnp.zeros((8, x.shape[1]), F32)
        return x, _shift_down(x, prev, 1), _shift_down(x, prev, 2)
    return x, a_ref[half, pl.ds(r0 - 1, rows), :], a_ref[half, pl.ds(r0 - 2, rows), :]


def _rows(a_ref):
    return min(_CONV_ROWS, a_ref.shape[1])


def _conv_fwd(name, a3, cw, cb, carries=()):
    _, t, f = a3.shape
    tc, rows = _CONV_COLS, min(_CONV_ROWS, t)

    def body(a_ref, cw_ref, cb_ref, y_ref):
        def chunk(r0, first):
            c = []
            for half in range(2):
                x, x1, x2 = _conv_taps(a_ref, half, r0, first)
                w = cw_ref[half]
                c.append(cb_ref[half] + (w[0:1] * x2 + w[1:2] * x1 + w[2:3] * x))
            y_ref[pl.ds(r0, rows), :] = (c[0] * jax.nn.sigmoid(c[0]) * c[1]).astype(BF)

        chunk(0, True)

        @pl.loop(1, t // rows)
        def _(r):
            chunk(pl.multiple_of(r * rows, rows), False)

    col = lambda *lead: pl.BlockSpec((*lead, tc), lambda j: (0,) * len(lead) + (j,))
    y, *dsts = _call(
        body, name=name, grid=(f // tc,), in_specs=[col(2, t), col(2, CONV_TAPS), col(2, 1)],
        out_specs=[col(t)], out_shape=[jax.ShapeDtypeStruct((t, f), BF)], operands=(a3, cw, cb),
        sem=("parallel",), vmem=40 * _MIB, carries=carries)
    return y, dsts


def _conv_bwd(name, a3, cw, cb, dy, carries=()):
    _, t, f = a3.shape
    tc, rows = _CONV_COLS, min(_CONV_ROWS, t)
    n_steps = t // rows

    def body(a_ref, cw_ref, cb_ref, dy_ref, da_ref, dcw_ref, dcb_ref, dc_ref):
        dcw_ref[...] = jnp.zeros_like(dcw_ref)
        dcb_ref[...] = jnp.zeros_like(dcb_ref)

        def chunk(r0, first, nxt):
            taps, c = [], []
            for half in range(2):
                x, x1, x2 = _conv_taps(a_ref, half, r0, first)
                w = cw_ref[half]
                taps.append((x2, x1, x))
                c.append(cb_ref[half] + (w[0:1] * x2 + w[1:2] * x1 + w[2:3] * x))
            gate, val = c
            sg = jax.nn.sigmoid(gate)
            dyv = dy_ref[pl.ds(r0, rows), :].astype(F32)
            dcs = (dyv * val * (sg * (1.0 + gate * (1.0 - sg))), dyv * (gate * sg))
            new_nxt = []
            for half in range(2):
                dc, w = dcs[half], cw_ref[half]
                dcb_ref[half] += jnp.sum(dc, axis=0, keepdims=True)
                for tap in range(CONV_TAPS):
                    dcw_ref[half, tap:tap + 1, :] += jnp.sum(dc * taps[half][tap], axis=0, keepdims=True)
                dc_ref[half, 0:rows, :] = dc
                dc_ref[half, rows:rows + 8, :] = nxt[half]
                da = w[2:3] * dc + w[1:2] * dc_ref[half, 1:rows + 1, :] + w[0:1] * dc_ref[half, 2:rows + 2, :]
                da_ref[half, pl.ds(r0, rows), :] = da.astype(BF)
                new_nxt.append(dc[:8])
            return tuple(new_nxt)

        zeros = jnp.zeros((8, tc), F32)
        nxt = lax.fori_loop(0, n_steps - 1, lambda s, nxt: chunk(pl.multiple_of((n_steps - 1 - s) * rows, rows), False, nxt),
                            (zeros, zeros))
        chunk(0, True, nxt)

    col = lambda *lead: pl.BlockSpec((*lead, tc), lambda j: (0,) * len(lead) + (j,))
    outs = _call(
        body, name=name, grid=(f // tc,), in_specs=[col(2, t), col(2, CONV_TAPS), col(2, 1), col(t)],
        out_specs=[col(2, t), col(2, CONV_TAPS), col(2, 1)],
        out_shape=[jax.ShapeDtypeStruct((2, t, f), BF), jax.ShapeDtypeStruct((2, CONV_TAPS, f), F32),
                   jax.ShapeDtypeStruct((2, 1, f), F32)],
        operands=(a3, cw, cb, dy), scratch=[pltpu.VMEM((2, rows + 8, tc), F32)], sem=("parallel",), vmem=48 * _MIB,
        carries=carries)
    return outs[:3], outs[3:]


_ATT_BLOCK = 256


def _split_dot(x, tri):
    hi = x.astype(BF)
    lo = (x - hi.astype(F32)).astype(BF)
    return jnp.dot(hi, tri, preferred_element_type=F32) + jnp.dot(lo, tri, preferred_element_type=F32)


_ATT_HEADS_FWD = 8
_ATT_HEADS_BWD = 4


def _logits(qb, kb, diagonal):
    z = lax.dot_general(qb, kb, _NT, preferred_element_type=F32) * (1.0 / math.sqrt(GROUP))
    lb = jnp.minimum(z, 0.0) - jnp.log(1.0 + jnp.exp(-jnp.abs(z)))
    if not diagonal:
        return lb, lb - z, None
    mask = lax.broadcasted_iota(jnp.int32, z.shape, 1) < lax.broadcasted_iota(jnp.int32, z.shape, 0)
    return lb, jnp.where(mask, lb - z, 0.0), mask


def _head(ref, g, rows=slice(None)):
    return ref[rows, g * GROUP:(g + 1) * GROUP]


def _attn_fwd(name, q, k, v, carries=()):
    t, hd = q.shape
    blk = min(_ATT_BLOCK, t)
    heads = min(_ATT_HEADS_FWD, hd // GROUP)
    gs = range(heads)

    def body(q_ref, k_ref, v_ref, o_ref, rest_ref, first_ref):
        hg, i = pl.program_id(0), pl.program_id(1)
        ri = lax.broadcasted_iota(jnp.int32, (blk, blk), 0)
        ci = lax.broadcasted_iota(jnp.int32, (blk, blk), 1)
        tri = (ri >= ci).astype(BF)

        def tile(j, state, diagonal):
            keys = pl.ds(pl.multiple_of(j * blk, blk), blk)
            logit = [_logits(_head(q_ref, g), _head(k_ref, g, keys), diagonal) for g in gs]
            incl = [_split_dot(logit[g][1], tri) for g in gs]
            a = [jnp.exp(logit[g][0] + (incl[g] - logit[g][1] + state[g][0])) for g in gs]
            if diagonal:
                a = [jnp.where(logit[g][2], a[g], 0.0) for g in gs]
            out = [state[g][1] + jnp.dot(a[g].astype(BF), _head(v_ref, g, keys), preferred_element_type=F32) for g in gs]
            return tuple((state[g][0] + incl[g][:, 0:1], out[g]) for g in gs)

        state = tile(i, ((jnp.zeros((blk, 1), F32), jnp.zeros((blk, GROUP), F32)),) * heads, True)

        def more(carry):
            j, state = carry
            live = functools.reduce(jnp.maximum, [jnp.max(right) for right, _ in state])
            return jnp.logical_and(j >= 0, live > EXP_FLOOR)

        j, state = lax.while_loop(more, lambda c: (c[0] - 1, tile(c[0], c[1], False)), (i - 1, state))
        for g, (right, acc) in enumerate(state):
            o_ref[:, g * GROUP:(g + 1) * GROUP] = acc.astype(BF)
            rest_ref[:, g * GROUP:(g + 1) * GROUP] = jnp.broadcast_to(right, (blk, GROUP))
        first_ref[hg, i] = (j + 1).astype(F32)

    qspec = pl.BlockSpec((blk, heads * GROUP), lambda h, i: (i, h))
    kvspec = pl.BlockSpec((t, heads * GROUP), lambda h, i: (0, h), pipeline_mode=pl.Buffered(1))
    groups = hd // (heads * GROUP)
    outs = _call(
        body, name=name, grid=(groups, t // blk), in_specs=[qspec, kvspec, kvspec],
        out_specs=[qspec, qspec, pl.BlockSpec(memory_space=pltpu.SMEM)],
        out_shape=[jax.ShapeDtypeStruct((t, hd), BF), jax.ShapeDtypeStruct((t, hd), F32),
                   jax.ShapeDtypeStruct((groups, t // blk), F32)],
        operands=(q, k, v), sem=("arbitrary", "arbitrary"), vmem=40 * _MIB, carries=carries, middle_at=0.75)
    return outs[:3], outs[3:]


def _attn_bwd(name, q, k, v, rest, first, do, carries=()):
    t, hd = q.shape
    blk = min(_ATT_BLOCK, t)
    nq = t // blk
    scale = 1.0 / math.sqrt(GROUP)
    heads = min(_ATT_HEADS_BWD, hd // GROUP)
    per_first = min(_ATT_HEADS_FWD, hd // GROUP) // heads
    gs = range(heads)

    def body(first_ref, q_ref, k_ref, v_ref, rest_ref, do_ref, dq_ref, dk_ref, dv_ref, dk_acc, dv_acc):
        hg, i = pl.program_id(0), pl.program_id(1)

        @pl.when(i == 0)
        def _():
            dk_acc[...] = jnp.zeros_like(dk_acc)
            dv_acc[...] = jnp.zeros_like(dv_acc)

        ri = lax.broadcasted_iota(jnp.int32, (blk, blk), 0)
        ci = lax.broadcasted_iota(jnp.int32, (blk, blk), 1)
        tri = (ri <= ci).astype(BF)

        def tile(j, state, diagonal):
            keys = pl.ds(pl.multiple_of(j * blk, blk), blk)
            qs, dos = [_head(q_ref, g) for g in gs], [_head(do_ref, g) for g in gs]
            kb, vb = [_head(k_ref, g, keys) for g in gs], [_head(v_ref, g, keys) for g in gs]
            logit = [_logits(qs[g], kb[g], diagonal) for g in gs]
            pre = [_split_dot(logit[g][1], tri) for g in gs]
            a = [jnp.exp(logit[g][0] + (rest_ref[:, g * GROUP:g * GROUP + 1] - state[g][0] - pre[g])) for g in gs]
            if diagonal:
                a = [jnp.where(logit[g][2], a[g], 0.0) for g in gs]
            gw = [a[g] * lax.dot_general(dos[g], vb[g], _NT, preferred_element_type=F32) for g in gs]
            gpre = [_split_dot(gw[g], tri) for g in gs]
            dz = []
            for g in gs:
                beta = jnp.exp(logit[g][0])
                d = (gw[g] * (1.0 - beta) - (state[g][1] + gpre[g] - gw[g]) * beta) * scale
                dz.append((jnp.where(logit[g][2], d, 0.0) if diagonal else d).astype(BF))
            for g in gs:
                dk_acc[keys, g * GROUP:(g + 1) * GROUP] += lax.dot_general(dz[g], qs[g], _TN, preferred_element_type=F32)
                dv_acc[keys, g * GROUP:(g + 1) * GROUP] += lax.dot_general(a[g].astype(BF), dos[g], _TN,
                                                                           preferred_element_type=F32)
            return tuple((state[g][0] + pre[g][:, blk - 1:blk], state[g][1] + gpre[g][:, blk - 1:blk],
                          state[g][2] + jnp.dot(dz[g], kb[g], preferred_element_type=F32)) for g in gs)

        zero = jnp.zeros((blk, 1), F32)
        first_block = jnp.clip(first_ref[hg // per_first, i].astype(jnp.int32), 0, i)
        state = lax.fori_loop(first_block, i, lambda j, c: tile(j, c, False),
                              ((zero, zero, jnp.zeros((blk, GROUP), F32)),) * heads)
        for g, (_, _, dq) in enumerate(tile(i, state, True)):
            dq_ref[:, g * GROUP:(g + 1) * GROUP] = dq.astype(BF)

        @pl.when(i == nq - 1)
        def _():
            dk_ref[...] = dk_acc[...].astype(BF)
            dv_ref[...] = dv_acc[...].astype(BF)

    qspec = pl.BlockSpec((blk, heads * GROUP), lambda h, i: (i, h))
    kvspec = pl.BlockSpec((t, heads * GROUP), lambda h, i: (0, h), pipeline_mode=pl.Buffered(1))
    outs = _call(
        body, name=name, grid=(hd // (heads * GROUP), nq),
        in_specs=[pl.BlockSpec(memory_space=pltpu.SMEM), qspec, kvspec, kvspec, qspec, qspec],
        out_specs=[qspec, kvspec, kvspec], out_shape=[jax.ShapeDtypeStruct((t, hd), BF)] * 3,
        operands=(first, q, k, v, rest, do),
        scratch=[pltpu.VMEM((t, heads * GROUP), F32), pltpu.VMEM((t, heads * GROUP), F32)],
        sem=("arbitrary", "arbitrary"), vmem=48 * _MIB, carries=carries)
    return outs[:3], outs[3:]


def _adamw_math(w, g, m, v):
    m = ADAM_B1 * m + (1.0 - ADAM_B1) * g
    v = ADAM_B2 * v + (1.0 - ADAM_B2) * (g * g)
    m_hat = m / (1.0 - ADAM_B1 ** ADAM_STEP)
    v_hat = v / (1.0 - ADAM_B2 ** ADAM_STEP)
    return -ADAM_LR * (m_hat / (jnp.sqrt(v_hat) + ADAM_EPS) + ADAM_WD * w), m, v


def _sum_adamw(name, parts, w, m, v):
    layers, r, c = w.shape
    budget = 512 * 1024 // layers
    tr = r if r * c <= budget else _tile_rows(r, max(8, (budget // c) // 8 * 8))
    n = r // tr

    def body(*refs):
        p_refs, (w_ref, m_ref, v_ref, g_ref, d_ref, nm_ref, nv_ref) = refs[:layers], refs[layers:]
        for layer, p_ref in enumerate(p_refs):
            @pl.when(pl.program_id(0) == layer)
            def _(p_ref=p_ref):
                g = p_ref[0].astype(F32)
                for dev in range(1, N_DEV):
                    g = g + p_ref[dev].astype(F32)
                g_ref[...] = g
                d_ref[...], nm_ref[...], nv_ref[...] = _adamw_math(w_ref[...], g, m_ref[...], v_ref[...])

    def part_spec(layer):
        return pl.BlockSpec((N_DEV, tr, c), lambda l, i: (0, jnp.where(l < layer, 0, jnp.where(l == layer, i, n - 1)), 0))

    row = pl.BlockSpec((None, tr, c), lambda l, i: (l, i, 0))
    return pl.pallas_call(
        body, name=name, grid=(layers, n), in_specs=[part_spec(layer) for layer in range(layers)] + [row] * 3,
        out_specs=[row] * 4, out_shape=[jax.ShapeDtypeStruct((layers, r, c), F32)] * 4,
        compiler_params=_params(("arbitrary", "arbitrary"), 40 * _MIB),
    )(*parts, w, m, v)


def _tile_rows(r, pref):
    t = min(r, pref)
    while r % t or t % 8:
        t -= 1
    return t


def _exchange(name, carries):
    return _call(lambda: None, name=name, grid=(1,), in_specs=[], out_specs=[], out_shape=[], operands=(), carries=carries)


def _all_reduce_small(name, groups):
    c = groups[0][0].shape[1]
    parts, offsets, starts, r = [], [], [], 0
    for group in groups:
        starts.append(r)
        for p in group:
            parts.append(p)
            offsets.append(r)
            r += p.shape[0]
        r = -(-r // 8) * 8
    n = len(parts)

    def body(*refs):
        out_ref, slots, send_sems, recv_sems = refs[n:]
        x, y, c_, me = _place()
        slots[me] = jnp.zeros((r, c), F32)
        for p_ref, off in zip(refs[:n], offsets):
            slots[me, off:off + p_ref.shape[0], :] = p_ref[...]

        here, sibling = (x, y, c_), (x, y, 1 - c_)
        chips = [(1 - x, y), (x, 1 - y), (1 - x, 1 - y)]

        def copy(sem, block, to):
            slot = slots.at[4 * block[0] + 2 * block[1] + block[2]]
            return pltpu.make_async_remote_copy(
                src_ref=slot, dst_ref=slot, send_sem=send_sems.at[sem], recv_sem=recv_sems.at[sem], device_id=to,
                device_id_type=MESH)

        sent = [copy(0, here, sibling)] + [copy(1 + j, here, (*chip, c_)) for j, chip in enumerate(chips)]
        for cp in sent:
            cp.start()
        for j, chip in enumerate(chips):
            copy(1 + j, (*chip, c_), here).wait_recv()
            sent.append(copy(4 + j, (*chip, c_), sibling))
            sent[-1].start()
        copy(0, sibling, here).wait_recv()
        for j, chip in enumerate(chips):
            copy(4 + j, (*chip, 1 - c_), here).wait_recv()
        for cp in sent:
            cp.wait_send()
        total = slots[0]
        for dev in range(1, N_DEV):
            total = total + slots[dev]
        out_ref[...] = total

    vm = pl.BlockSpec(memory_space=pltpu.VMEM)
    summed = pl.pallas_call(
        body, name=name, in_specs=[vm] * n, out_specs=vm, out_shape=jax.ShapeDtypeStruct((r, c), F32),
        scratch_shapes=[pltpu.VMEM((N_DEV, r, c), F32), pltpu.SemaphoreType.DMA((7,)), pltpu.SemaphoreType.DMA((7,))],
        compiler_params=_params(None, (N_DEV + 6) * r * c * 4 + 8 * _MIB),
    )(*parts)
    return summed, starts


def _adamw(name, g, w, m, v):
    cols = w.shape[-1]
    flat = lambda a: a.reshape(-1, cols)

    def body(g_ref, w_ref, m_ref, v_ref, d_ref, nm_ref, nv_ref):
        d_ref[...], nm_ref[...], nv_ref[...] = _adamw_math(w_ref[...], g_ref[...], m_ref[...], v_ref[...])

    outs = pl.pallas_call(body, name=name, out_shape=[jax.ShapeDtypeStruct(flat(w).shape, F32)] * 3)(
        flat(g), flat(w), flat(m), flat(v))
    return [o.reshape(w.shape) for o in outs]


def kernel(x, pre_mix_g, post_mix_g, pre_ffn_g, post_ffn_g, a_w_in, a_v_norm_g, a_w_spatial, a_b_spatial, a_w_out, kv_norm_g, w_k, w_v, b_w_q, b_w_o, ffn_w_up, ffn_conv_w, ffn_conv_b, ffn_w_down, loss_target, m_pre_mix_g, m_post_mix_g, m_pre_ffn_g, m_post_ffn_g, m_a_w_in, m_a_v_norm_g, m_a_w_spatial, m_a_b_spatial, m_a_w_out, m_kv_norm_g, m_w_k, m_w_v, m_b_w_q, m_b_w_o, m_ffn_w_up, m_ffn_conv_w, m_ffn_conv_b, m_ffn_w_down, v_pre_mix_g, v_post_mix_g, v_pre_ffn_g, v_post_ffn_g, v_a_w_in, v_a_v_norm_g, v_a_w_spatial, v_a_b_spatial, v_a_w_out, v_kv_norm_g, v_w_k, v_w_v, v_b_w_q, v_b_w_o, v_ffn_w_up, v_ffn_conv_w, v_ffn_conv_b, v_ffn_w_down):
    t, d = x.shape[1], x.shape[2]
    f = ffn_w_down.shape[1] * N_DEV
    ng = d // GROUP
    me = 4 * lax.axis_index("x") + 2 * lax.axis_index("y") + lax.axis_index("c")
    x2, target = x.reshape(t, d), loss_target.reshape(t, d)

    hn0, g_in, g_cw, g_vg = _norms_fwd("pre_mix0", x2, [pre_mix_g[0]], carries=[
        _gather(a_w_in[0].astype(BF)), _gather(ffn_conv_w.reshape(2 * CONV_TAPS, -1)), _gather(a_v_norm_g)])
    up0 = ffn_w_up[0].astype(BF)
    cw_full = jnp.transpose(g_cw.reshape(N_DEV, 2, CONV_TAPS, -1), (1, 2, 0, 3)).reshape(2, CONV_TAPS, 2, f)
    cw_l = [jnp.transpose(cw_full[l], (1, 0, 2)) for l in range(2)]
    cb_l = [ffn_conv_b[l].reshape(2, 1, f) for l in range(2)]
    vg_full = g_vg.reshape(1, d)
    w_s = a_w_spatial[0]
    w_st = jnp.swapaxes(w_s, 1, 2)
    b_st = a_b_spatial[0].T

    p0, (g_out, g_up0) = _mm_nn_blk("sgu_in", hn0, g_in, F32, carries=[
        _gather(a_w_out[0].astype(BF)), _gather(up0, 0, d // 4)])
    w_out_f = g_out.reshape(d, d)
    sg, g_up0 = _sgu_fwd("sgu", p0, vg_full, w_s, b_st, carries=[_gather(up0, d // 4, d // 2, dst=g_up0)])
    mix0, (g_up0,) = _mm_nn("sgu_out", sg, w_out_f, F32, carries=[_gather(up0, d // 2, 3 * d // 4, dst=g_up0)])
    h1, fn0, g_up0 = _resid_norms("post_mix0", x2, mix0, post_mix_g[0], [pre_ffn_g[0]],
                                  carries=[_gather(up0, 3 * d // 4, d, dst=g_up0)])
    a3_0, (g_dn0,) = _mm_nn_blk("ffn0_up", fn0, g_up0, F32, halves=True, carries=[_gather(ffn_w_down[0].astype(BF))])
    y0, (g_q,) = _conv_fwd("ffn0_conv", a3_0, cw_l[0], cb_l[0], carries=[_gather(b_w_q[0].astype(BF))])
    wv = w_v.astype(BF)
    f0, (g_k, g_v) = _mm_nn("ffn0_down", y0, g_dn0.reshape(f, d), F32,
                            carries=[_gather(w_k.astype(BF)), _gather(wv, 0, d // 16)])
    h2, hn1, kvn, g_v = _resid_norms("post_ffn0", h1, f0, post_ffn_g[0], [pre_mix_g[1], kv_norm_g],
                                     carries=[_gather(wv, d // 16, d // 8, dst=g_v)])
    w_q_f, w_k_f, w_v_f = g_q.reshape(d, d), g_k.reshape(d, d), g_v.reshape(d, d)
    up1 = ffn_w_up[1].astype(BF)
    e = d // 8
    q, (g_up1,) = _mm_nn("attn_q", hn1, w_q_f, BF, carries=[_gather(up1, 0, e)])
    kk, (g_up1,) = _mm_nn("attn_k", kvn, w_k_f, BF, carries=[_gather(up1, e, 2 * e, dst=g_up1)])
    vv, (g_up1,) = _mm_nn("attn_v", kvn, w_v_f, BF, carries=[_gather(up1, 2 * e, 3 * e, dst=g_up1)])
    (att, rest, first), (g_o, g_up1) = _attn_fwd(
        "attn", q, kk, vv, carries=[_gather(b_w_o[0].astype(BF)), _gather(up1, 3 * e, 6 * e, dst=g_up1)])
    w_o_f = g_o.reshape(d, d)
    mix1, (g_up1,) = _mm_nn("attn_o", att, w_o_f, F32, carries=[_gather(up1, 6 * e, 7 * e, dst=g_up1)])
    h3, fn1, g_up1 = _resid_norms("post_mix1", h2, mix1, post_mix_g[1], [pre_ffn_g[1]],
                                  carries=[_gather(up1, 7 * e, d, dst=g_up1)])
    a3_1, (g_dn1,) = _mm_nn_blk("ffn1_up", fn1, g_up1, F32, halves=True, carries=[_gather(ffn_w_down[1].astype(BF))])
    y1, _ = _conv_fwd("ffn1_conv", a3_1, cw_l[1], cb_l[1])
    f1 = _mm_nn("ffn1_down", y1, g_dn1.reshape(f, d), F32)
    g_up = (g_up0, g_up1)
    w_dn_f = (g_dn0.reshape(f, d), g_dn1.reshape(f, d))
    dh, loss_part, df1, d_post_ffn1 = _resid_loss("loss", h3, f1, post_ffn_g[1], target)
    loss = lax.psum(loss_part[0, 0], ("x", "y", "c"))

    def blocks(dw):
        return dw.reshape(N_DEV, -1, d)

    def split(result, carries):
        return result if carries else (result, [])

    def ffn_bwd(l, dh_out, dfo, h_in, fn, a3, yv, mix, with_dw=(), with_dx=(), with_up=()):
        dw_dn, sent_dw = split(_mm_tn(f"ffn{l}_down_dw", yv, dfo, carries=with_dw), with_dw)
        dy, sent_dx = split(_mm_nt(f"ffn{l}_down_dx", dfo, w_dn_f[l], BF, carries=with_dx), with_dx)
        dw_dn = blocks(dw_dn)
        cut = dw_dn.shape[1] * 5 // 8 // 16 * 16
        (da3, dcw, dcb), (p_dn,) = _conv_bwd(f"ffn{l}_conv_bwd", a3, cw_l[l], cb_l[l], dy, carries=[_scatter(dw_dn, 0, cut)])
        dw_up, (p_dn, *sent_up) = _mm_tn_blk(f"ffn{l}_up_dw", fn, da3, halves=True,
                                             carries=[_scatter(dw_dn, cut, dst=p_dn), *with_up])
        dfn, (p_up,) = _mm_nt_blk(f"ffn{l}_up_dx", da3, g_up[l], F32, halves=True, carries=[_scatter(dw_up, 0, d // 2)])
        dh_in, d_pre, dmix, d_post_mix, p_up = _norm_bwd(
            f"pre_ffn{l}_bwd", h_in, pre_ffn_g[l], dfn, res=dh_out, then=(mix, post_mix_g[l]),
            carries=[_scatter(dw_up, d // 2, 9 * d // 16, dst=p_up)])
        return dh_in, dmix, d_pre, d_post_mix, dcw, dcb, p_dn, dw_up, p_up, list(sent_dw) + list(sent_dx) + list(sent_up)

    dh3, dmix1, d_pre_ffn1, d_post_mix1, dcw1, dcb1, p_dn1, dw_up1, p_up1, _ = ffn_bwd(
        1, dh, df1, h3, fn1, a3_1, y1, mix1)
    dw_o = _mm_tn("attn_o_dw", att, dmix1)
    datt = _mm_nt("attn_o_dx", dmix1, w_o_f, BF)
    (dq, dk, dv), (p_up1,) = _attn_bwd(
        "attn_bwd", q, kk, vv, rest, first, datt, carries=[_scatter(dw_up1, 9 * d // 16, d, dst=p_up1)])
    dw_o, qr = blocks(dw_o), d // 32
    dw_q, (p_o,) = _mm_tn("attn_q_dw", hn1, dq, carries=[_scatter(dw_o, 0, qr)])
    dw_k, (p_o,) = _mm_tn("attn_k_dw", kvn, dk, carries=[_scatter(dw_o, qr, 2 * qr, dst=p_o)])
    dw_v, (p_o,) = _mm_tn("attn_v_dw", kvn, dv, carries=[_scatter(dw_o, 2 * qr, 3 * qr, dst=p_o)])
    dhn1, (p_o,) = _mm_nt("attn_q_dx", dq, w_q_f, F32, carries=[_scatter(dw_o, 3 * qr, 4 * qr, dst=p_o)])
    dkvn = _mm_nt("attn_v_dx", dv, w_v_f, F32, add=_mm_nt("attn_k_dx", dk, w_k_f, F32))
    dh2, d_pre_mix1, d_kv, df0, d_post_ffn0 = _norm_bwd(
        "pre_mix1_kv_bwd", h2, [pre_mix_g[1], kv_norm_g], [dhn1, dkvn], res=dh3, then=(f0, post_ffn_g[0]))
    dh1, dmix0, d_pre_ffn0, d_post_mix0, dcw0, dcb0, p_dn0, dw_up0, p_up0, (p_q, p_k, p_v) = ffn_bwd(
        0, dh2, df0, h1, fn0, a3_0, y0, mix0, with_dw=[_scatter(blocks(dw_q))], with_dx=[_scatter(blocks(dw_k))],
        with_up=[_scatter(blocks(dw_v))])
    dw_out, (p_up0,) = _mm_tn("sgu_out_dw", sg, dmix0, carries=[_scatter(dw_up0, 9 * d // 16, 11 * d // 16, dst=p_up0)])
    dsg = _mm_nt("sgu_out_dx", dmix0, w_out_f, BF)
    (dp0, d_vg, d_ws, d_bst), (p_up0,) = _sgu_bwd(
        "sgu_bwd", p0, vg_full, w_s, w_st, b_st, dsg, carries=[_scatter(dw_up0, 11 * d // 16, 15 * d // 16, dst=p_up0)])
    dw_in, (p_out,) = _mm_tn_blk("sgu_in_dw", hn0, dp0, carries=[_scatter(blocks(dw_out))])
    dhn0, (p_in,) = _mm_nt_blk("sgu_in_dx", dp0, g_in, F32, carries=[_scatter(dw_in, 0, 5 * d // 8)])
    grad_x, d_pre_mix0, p_in = _norm_bwd("pre_mix0_bwd", x2, pre_mix_g[0], dhn0, res=dh1,
                                         carries=[_scatter(dw_in, 5 * d // 8, 7 * d // 8, dst=p_in)])
    p_up0, p_in = _exchange("scatter_last", [_scatter(dw_up0, 15 * d // 16, d, dst=p_up0),
                                             _scatter(dw_in, 7 * d // 8, d, dst=p_in)])

    def conv_w_grad(dcw):
        return jnp.transpose(dcw, (1, 0, 2)).reshape(CONV_TAPS, 2 * f)

    small = [
        ([d_pre_mix0, d_pre_mix1], (2, d)), ([d_post_mix0, d_post_mix1], (2, d)),
        ([d_pre_ffn0, d_pre_ffn1], (2, d)), ([d_post_ffn0, d_post_ffn1], (2, d)),
        ([d_kv], (d,)), ([d_vg], (1, d)), ([d_bst.T], (1, ng, GROUP)), ([d_ws], (1, ng, GROUP, GROUP)),
        ([dcb0, dcb1], (2, 2 * f)), ([conv_w_grad(dcw0), conv_w_grad(dcw1)], (2, CONV_TAPS, 2 * f)),
    ]
    width = V7X_LANES * math.gcd(d // V7X_LANES, 2 * f // V7X_LANES)
    summed, offsets = _all_reduce_small("reduce_small", [[a.reshape(-1, width) for a in group] for group, _ in small])
    full = [summed[off:off + math.prod(shape) // width].reshape(shape) for off, (_, shape) in zip(offsets, small)]
    g_pre_mix, g_post_mix, g_pre_ffn, g_post_ffn, g_kv, g_vgain, g_bs, g_ws, g_cb, g_cwf = full
    cw_w = 2 * f // N_DEV
    g_vgain = lax.dynamic_slice_in_dim(g_vgain, me * (d // N_DEV), d // N_DEV, axis=1)
    g_cwf = lax.dynamic_slice_in_dim(g_cwf, me * cw_w, cw_w, axis=2)

    parts = [p_in, p_out, p_k, p_v, p_q, p_o, p_up0, p_up1, p_dn0, p_dn1]

    def small_update(name, g, w, m, v):
        return [g] + _adamw(name, g, w, m, v)

    def stacked(name, part0, part1, w, m, v):
        shape = (2, *part0.shape[1:])
        return [o.reshape(w.shape) for o in _sum_adamw(name, [part0, part1], w.reshape(shape), m.reshape(shape), v.reshape(shape))]

    def single(name, part, w, m, v):
        shape = (1, *part.shape[1:])
        return [o.reshape(w.shape) for o in _sum_adamw(name, [part], w.reshape(shape), m.reshape(shape), v.reshape(shape))]

    results = {
        "pre_mix_g": small_update("adam_pre_mix", g_pre_mix, pre_mix_g, m_pre_mix_g, v_pre_mix_g),
        "post_mix_g": small_update("adam_post_mix", g_post_mix, post_mix_g, m_post_mix_g, v_post_mix_g),
        "pre_ffn_g": small_update("adam_pre_ffn", g_pre_ffn, pre_ffn_g, m_pre_ffn_g, v_pre_ffn_g),
        "post_ffn_g": small_update("adam_post_ffn", g_post_ffn, post_ffn_g, m_post_ffn_g, v_post_ffn_g),
        "a_w_in": single("adam_a_w_in", parts[0], a_w_in, m_a_w_in, v_a_w_in),
        "a_v_norm_g": small_update("adam_a_v_norm", g_vgain, a_v_norm_g, m_a_v_norm_g, v_a_v_norm_g),
        "a_w_spatial": small_update("adam_a_w_spatial", g_ws, a_w_spatial, m_a_w_spatial, v_a_w_spatial),
        "a_b_spatial": small_update("adam_a_b_spatial", g_bs, a_b_spatial, m_a_b_spatial, v_a_b_spatial),
        "a_w_out": single("adam_a_w_out", parts[1], a_w_out, m_a_w_out, v_a_w_out),
        "kv_norm_g": small_update("adam_kv_norm", g_kv, kv_norm_g, m_kv_norm_g, v_kv_norm_g),
        "w_k": single("adam_w_k", parts[2], w_k, m_w_k, v_w_k),
        "w_v": single("adam_w_v", parts[3], w_v, m_w_v, v_w_v),
        "b_w_q": single("adam_b_w_q", parts[4], b_w_q, m_b_w_q, v_b_w_q),
        "b_w_o": single("adam_b_w_o", parts[5], b_w_o, m_b_w_o, v_b_w_o),
        "ffn_w_up": stacked("adam_ffn_w_up", parts[6], parts[7], ffn_w_up, m_ffn_w_up, v_ffn_w_up),
        "ffn_conv_w": small_update("adam_ffn_conv_w", g_cwf, ffn_conv_w, m_ffn_conv_w, v_ffn_conv_w),
        "ffn_conv_b": small_update("adam_ffn_conv_b", g_cb, ffn_conv_b, m_ffn_conv_b, v_ffn_conv_b),
        "ffn_w_down": stacked("adam_ffn_w_down", parts[8], parts[9], ffn_w_down, m_ffn_w_down, v_ffn_w_down),
    }
    order = ["pre_mix_g", "post_mix_g", "pre_ffn_g", "post_ffn_g", "a_w_in", "a_v_norm_g", "a_w_spatial", "a_b_spatial",
             "a_w_out", "kv_norm_g", "w_k", "w_v", "b_w_q", "b_w_o", "ffn_w_up", "ffn_conv_w", "ffn_conv_b", "ffn_w_down"]
    outs = [loss, grad_x.reshape(x.shape)]
    for idx in range(4):
        outs += [results[n][idx] for n in order]
    return tuple(outs)
```

```python
import functools
import math
from typing import NamedTuple, Optional

import jax
import jax.numpy as jnp
from jax import lax
from jax.experimental import pallas as pl
from jax.experimental.pallas import tpu as pltpu

F32 = jnp.float32
BF = jnp.bfloat16
MESH = pl.DeviceIdType.MESH

N_DEV = 8
NORM_EPS = 1e-6
GROUP = 128
CONV_TAPS = 3
ADAM_LR, ADAM_B1, ADAM_B2, ADAM_EPS, ADAM_WD, ADAM_STEP = 0.001, 0.9, 0.999, 1e-08, 0.01, 10
EXP_FLOOR = -104.0

V7X_LANES = 128
V7X_VMEM_BYTES = 64 * 1024 * 1024
_MIB = 1024 * 1024

_NN = (((1,), (0,)), ((), ()))
_NT = (((1,), (1,)), ((), ()))
_TN = (((0,), (0,)), ((), ()))


def _tile(n, pref):
    if n <= pref:
        return n
    t = (pref // V7X_LANES) * V7X_LANES
    while t > V7X_LANES and n % t:
        t -= V7X_LANES
    assert n % t == 0, (n, pref)
    return t


def _nbytes(shape, dtype):
    return math.prod(shape) * jnp.dtype(dtype).itemsize


def _params(sem=None, vmem=None):
    kw = {}
    if sem is not None:
        kw["dimension_semantics"] = sem
    if vmem is not None:
        kw["vmem_limit_bytes"] = int(min(max(vmem, 16 * _MIB), V7X_VMEM_BYTES - 8 * _MIB))
    return pltpu.CompilerParams(**kw)


def _place():
    x, y, c = lax.axis_index("x"), lax.axis_index("y"), lax.axis_index("c")
    return x, y, c, 4 * x + 2 * y + c


def _flip(x, y, c, k):
    return (1 - x if k & 4 else x, 1 - y if k & 2 else y, 1 - c if k & 1 else c)


class _Carry(NamedTuple):
    gather: bool
    src: jax.Array
    dst: Optional[jax.Array]
    lo: int
    hi: int


def _gather(src, lo=0, hi=None, dst=None):
    return _Carry(True, src, dst, lo, src.shape[0] if hi is None else hi)


def _scatter(src, lo=0, hi=None, dst=None):
    return _Carry(False, src, dst, lo, src.shape[1] if hi is None else hi)


def _carry_phases(carries, srcs, dsts, send_sems, recv_sems, local_sems):
    x, y, c, me = _place()
    here, sibling = (x, y, c), (x, y, 1 - c)
    chips = [(1 - x, y), (x, 1 - y), (1 - x, 1 - y)]

    def rows(u):
        return pl.ds(carries[u].lo, carries[u].hi - carries[u].lo)

    def block_copy(u, sem, block, to, from_src=False):
        slot = dsts[u].at[4 * block[0] + 2 * block[1] + block[2], rows(u)]
        return pltpu.make_async_remote_copy(
            src_ref=srcs[u].at[rows(u)] if from_src else slot, dst_ref=slot, send_sem=send_sems.at[u, sem],
            recv_sem=recv_sems.at[u, sem], device_id=to, device_id_type=MESH)

    def partial_copy(u, k):
        peer = _flip(x, y, c, k)
        return pltpu.make_async_remote_copy(
            src_ref=srcs[u].at[4 * peer[0] + 2 * peer[1] + peer[2], rows(u)], dst_ref=dsts[u].at[me, rows(u)],
            send_sem=send_sems.at[u, k - 1], recv_sem=recv_sems.at[u, k - 1], device_id=peer, device_id_type=MESH)

    def local_copy(u):
        src = srcs[u].at[rows(u)] if carries[u].gather else srcs[u].at[me, rows(u)]
        return pltpu.make_async_copy(src, dsts[u].at[me, rows(u)], local_sems.at[u])

    def first():
        for u, cr in enumerate(carries):
            local_copy(u).start()
            if cr.gather:
                block_copy(u, 0, here, sibling, from_src=True).start()
                for j, chip in enumerate(chips):
                    block_copy(u, 1 + j, here, (*chip, c), from_src=True).start()
            else:
                for k in range(1, N_DEV):
                    partial_copy(u, k).start()

    def middle():
        for u, cr in enumerate(carries):
            if cr.gather:
                for j, chip in enumerate(chips):
                    block_copy(u, 1 + j, (*chip, c), here).wait_recv()
                    block_copy(u, 4 + j, (*chip, c), sibling).start()

    def last():
        for u, cr in enumerate(carries):
            if cr.gather:
                block_copy(u, 0, sibling, here).wait_recv()
                for j, chip in enumerate(chips):
                    block_copy(u, 4 + j, (*chip, 1 - c), here).wait_recv()
                block_copy(u, 0, here, sibling, from_src=True).wait_send()
                for j, chip in enumerate(chips):
                    block_copy(u, 1 + j, here, (*chip, c), from_src=True).wait_send()
                    block_copy(u, 4 + j, (*chip, c), sibling).wait_send()
            else:
                for k in range(1, N_DEV):
                    partial_copy(u, k).wait()
            local_copy(u).wait()

    return first, middle, last


def _call(body, *, name, grid, in_specs, out_specs, out_shape, operands, scratch=(), sem=None, vmem=None,
          carries=(), middle_at=0.6):
    if not carries:
        return pl.pallas_call(
            body, name=name, grid=grid, in_specs=in_specs, out_specs=out_specs, out_shape=out_shape,
            scratch_shapes=list(scratch), compiler_params=_params(sem, vmem))(*operands)
    n_in, n_out, n_scr, nc = len(in_specs), len(out_specs), len(scratch), len(carries)
    given = [u for u, cr in enumerate(carries) if cr.dst is not None]
    steps = math.prod(grid)
    middle_step = min(steps - 1, int(steps * middle_at))

    def wrapped(*refs):
        ins, srcs = refs[:n_in], refs[n_in:n_in + nc]
        at = n_in + nc + len(given)
        outs, dsts = refs[at:at + n_out], refs[at + n_out:at + n_out + nc]
        at += n_out + nc
        scr, (send_sems, recv_sems, local_sems) = refs[at:at + n_scr], refs[at + n_scr:]
        first, middle, last = _carry_phases(carries, srcs, dsts, send_sems, recv_sems, local_sems)
        step = 0
        for axis, size in enumerate(grid):
            step = step * size + pl.program_id(axis)
        pl.when(step == 0)(first)
        body(*ins, *outs, *scr)
        pl.when(step == middle_step)(middle)
        pl.when(step == steps - 1)(last)

    any_spec = pl.BlockSpec(memory_space=pl.ANY)
    dst_shapes = [jax.ShapeDtypeStruct((N_DEV, *cr.src.shape) if cr.gather else cr.src.shape, cr.src.dtype) for cr in carries]
    return pl.pallas_call(
        wrapped, name=name, grid=grid, in_specs=list(in_specs) + [any_spec] * (nc + len(given)),
        out_specs=list(out_specs) + [any_spec] * nc, out_shape=list(out_shape) + dst_shapes,
        input_output_aliases={n_in + nc + g: n_out + u for g, u in enumerate(given)},
        scratch_shapes=list(scratch) + [pltpu.SemaphoreType.DMA((nc, 7)), pltpu.SemaphoreType.DMA((nc, 7)),
                                        pltpu.SemaphoreType.DMA((nc,))],
        compiler_params=_params(("arbitrary",) * len(grid), vmem),
    )(*operands, *[cr.src for cr in carries], *[carries[u].dst for u in given])


def _mm(name, a, b, *, dims, grid, a_blk, a_map, b_blk, b_map, o_blk, o_map, out_shape, out_dtype,
        add=None, add_blk=None, add_map=None, carries=(), b_slabs=1):
    nk = grid[2]
    assert add is None or nk == 1
    acc_shape = tuple(d for d in o_blk if d is not None)
    in_place = out_dtype == F32

    def body(*refs):
        if add is None:
            a_ref, b_ref, o_ref = refs[:3]
            c_ref, scr = None, refs[3:]
        else:
            a_ref, b_ref, c_ref, o_ref = refs[:4]
            scr = refs[4:]
        if b_slabs == 1:
            part = lax.dot_general(a_ref[...], b_ref[...], dims, preferred_element_type=F32)
        else:
            cw = b_ref.shape[2]
            part = sum(lax.dot_general(a_ref[:, s * cw:(s + 1) * cw], b_ref[s], dims, preferred_element_type=F32)
                       for s in range(b_slabs))
        if c_ref is not None:
            part = part + c_ref[...].astype(F32)
        if nk == 1:
            o_ref[...] = part.astype(o_ref.dtype)
            return
        acc = o_ref if in_place else scr[0]
        k = pl.program_id(2)

        @pl.when(k == 0)
        def _():
            acc[...] = part

        @pl.when(k > 0 if in_place else jnp.logical_and(k > 0, k < nk - 1))
        def _():
            acc[...] += part

        if not in_place:
            @pl.when(k == nk - 1)
            def _():
                o_ref[...] = (acc[...] + part).astype(o_ref.dtype)

    in_specs = [pl.BlockSpec(a_blk, a_map), pl.BlockSpec(b_blk, b_map)]
    operands = [a, b]
    scratch = [pltpu.VMEM(acc_shape, F32)] if nk > 1 and not in_place else []
    vmem = 2 * (_nbytes(acc_shape, out_dtype) + _nbytes([d for d in a_blk if d], a.dtype)
                + _nbytes([d for d in b_blk if d], b.dtype)) + (2 + len(scratch)) * _nbytes(acc_shape, F32)
    if add is not None:
        in_specs.append(pl.BlockSpec(add_blk, add_map))
        operands.append(add)
        vmem += 2 * _nbytes(acc_shape, add.dtype)
    out, *dsts = _call(
        body, name=name, grid=grid, in_specs=in_specs, out_specs=[pl.BlockSpec(o_blk, o_map)],
        out_shape=[jax.ShapeDtypeStruct(out_shape, out_dtype)], operands=operands, scratch=scratch,
        sem=("parallel", "parallel", "arbitrary"), vmem=vmem + 8 * _MIB, carries=carries)
    return (out, dsts) if carries else out


def _mm_nn(name, x, w, out_dtype, carries=()):
    t, kd = x.shape
    n = w.shape[1]
    tm, tn, tk = _tile(t, 1024), _tile(n, 2048 if kd > 2048 else 1024), _tile(kd, 1536 if kd > 2048 else 2048)
    return _mm(name, x, w, dims=_NN, grid=(t // tm, n // tn, kd // tk),
               a_blk=(tm, tk), a_map=lambda i, j, k: (i, k), b_blk=(tk, tn), b_map=lambda i, j, k: (k, j),
               o_blk=(tm, tn), o_map=lambda i, j, k: (i, j), out_shape=(t, n), out_dtype=out_dtype, carries=carries)


def _mm_nn_blk(name, x, g, out_dtype, halves=False, carries=()):
    t, kd = x.shape
    cw = g.shape[2]
    tm = _tile(t, 1024)
    if halves:
        o_blk, o_map, out_shape = (None, tm, cw), (lambda i, j, k: (j // 4, i, j % 4)), (2, t, 4 * cw)
    else:
        o_blk, o_map, out_shape = (tm, cw), (lambda i, j, k: (i, j)), (t, N_DEV * cw)
    return _mm(name, x, g, dims=_NN, grid=(t // tm, N_DEV, 1),
               a_blk=(tm, kd), a_map=lambda i, j, k: (i, 0), b_blk=(None, kd, cw), b_map=lambda i, j, k: (j, 0, 0),
               o_blk=o_blk, o_map=o_map, out_shape=out_shape, out_dtype=out_dtype, carries=carries)


def _mm_nt(name, dy, w, out_dtype, add=None, carries=()):
    t, n = dy.shape
    kd = w.shape[0]
    tm, tn = _tile(t, 1024), _tile(kd, 1408)
    kw = {}
    if add is not None:
        kw = dict(add=add, add_blk=(tm, tn), add_map=lambda i, j, k: (i, j))
    return _mm(name, dy, w, dims=_NT, grid=(t // tm, kd // tn, 1),
               a_blk=(tm, n), a_map=lambda i, j, k: (i, 0), b_blk=(tn, n), b_map=lambda i, j, k: (j, 0),
               o_blk=(tm, tn), o_map=lambda i, j, k: (i, j), out_shape=(t, kd), out_dtype=out_dtype, carries=carries, **kw)


def _mm_nt_blk(name, dy, g, out_dtype, halves=False, carries=()):
    kd, cw = g.shape[1], g.shape[2]
    t = dy.shape[1] if halves else dy.shape[0]
    tm = _tile(t, 512)
    slabs = 2 if cw > 512 else 4
    nk = N_DEV // slabs
    if halves:
        a_blk, a_map = (None, tm, slabs * cw), (lambda i, j, k: (k // (nk // 2), i, k % (nk // 2)))
    else:
        a_blk, a_map = (tm, slabs * cw), (lambda i, j, k: (i, k))
    return _mm(name, dy, g, dims=_NT, grid=(t // tm, 1, nk), b_slabs=slabs,
               a_blk=a_blk, a_map=a_map, b_blk=(slabs, kd, cw), b_map=lambda i, j, k: (k, 0, 0),
               o_blk=(tm, kd), o_map=lambda i, j, k: (i, 0), out_shape=(t, kd), out_dtype=out_dtype, carries=carries)


def _mm_tn(name, x, dy, carries=()):
    t, kd = x.shape
    n = dy.shape[1]
    tmx, tk = _tile(kd, 1024), _tile(t, 2048)
    return _mm(name, x, dy, dims=_TN, grid=(kd // tmx, 1, t // tk),
               a_blk=(tk, tmx), a_map=lambda i, j, k: (k, i), b_blk=(tk, n), b_map=lambda i, j, k: (k, 0),
               o_blk=(tmx, n), o_map=lambda i, j, k: (i, 0), out_shape=(kd, n), out_dtype=BF, carries=carries)


def _mm_tn_blk(name, x, dy, halves=False, carries=()):
    t, kd = x.shape
    cw = dy.shape[2] // 4 if halves else dy.shape[1] // N_DEV
    tmx, tk = _tile(kd, 2048 if cw <= 512 else 1024), _tile(t, 2048)
    if halves:
        b_blk, b_map = (None, tk, cw), (lambda i, j, k: (j // 4, k, j % 4))
    else:
        b_blk, b_map = (tk, cw), (lambda i, j, k: (k, j))
    return _mm(name, x, dy, dims=_TN, grid=(kd // tmx, N_DEV, t // tk),
               a_blk=(tk, tmx), a_map=lambda i, j, k: (k, i), b_blk=b_blk, b_map=b_map,
               o_blk=(None, tmx, cw), o_map=lambda i, j, k: (j, i, 0), out_shape=(N_DEV, kd, cw), out_dtype=BF,
               carries=carries)


def _rms(x, g):
    r = lax.rsqrt(jnp.mean(x * x, axis=-1, keepdims=True) + NORM_EPS)
    return x * r * g


def _rms_bwd_math(x, g, dy):
    d = x.shape[-1]
    r = lax.rsqrt(jnp.mean(x * x, axis=-1, keepdims=True) + NORM_EPS)
    xh = x * r
    u = dy * g
    dx = r * u - xh * (jnp.sum(xh * u, axis=-1, keepdims=True) * (r / d))
    return dx, jnp.sum(dy * xh, axis=0, keepdims=True)


def _row_specs(tr, d, n):
    return [pl.BlockSpec((tr, d), lambda i: (i, 0)) for _ in range(n)]


def _vec_specs(d, n):
    return [pl.BlockSpec((1, d), lambda i: (0, 0)) for _ in range(n)]


def _norms_fwd(name, h, gains, carries=()):
    t, d = h.shape
    tr, ng = _tile(t, 256), len(gains)

    def body(h_ref, *refs):
        x = h_ref[...]
        for g_ref, o_ref in zip(refs[:ng], refs[ng:]):
            o_ref[...] = _rms(x, g_ref[...]).astype(BF)

    return _call(
        body, name=name, grid=(t // tr,), in_specs=_row_specs(tr, d, 1) + _vec_specs(d, ng),
        out_specs=_row_specs(tr, d, ng), out_shape=[jax.ShapeDtypeStruct((t, d), BF)] * ng,
        operands=(h, *[g.reshape(1, d) for g in gains]), sem=("parallel",), carries=carries)


def _resid_norms(name, h, m, g_post, gains, carries=()):
    t, d = h.shape
    tr, ng = _tile(t, 256), len(gains)

    def body(h_ref, m_ref, gp_ref, *refs):
        hn = h_ref[...] + _rms(m_ref[...], gp_ref[...])
        refs[ng][...] = hn
        for g_ref, o_ref in zip(refs[:ng], refs[ng + 1:]):
            o_ref[...] = _rms(hn, g_ref[...]).astype(BF)

    return _call(
        body, name=name, grid=(t // tr,), in_specs=_row_specs(tr, d, 2) + _vec_specs(d, 1 + ng),
        out_specs=_row_specs(tr, d, 1 + ng),
        out_shape=[jax.ShapeDtypeStruct((t, d), F32)] + [jax.ShapeDtypeStruct((t, d), BF)] * ng,
        operands=(h, m, g_post.reshape(1, d), *[g.reshape(1, d) for g in gains]), sem=("parallel",), carries=carries)


def _resid_loss(name, h, m, g_post, target):
    t, d = h.shape
    tr = _tile(t, 256)

    def body(h_ref, m_ref, gp_ref, t_ref, dy_ref, loss_ref, dm_ref, dg_ref):
        mv = m_ref[...]
        diff = h_ref[...] + _rms(mv, gp_ref[...]) - t_ref[...]
        dy = diff * (1.0 / d)
        dy_ref[...] = dy
        dm, dg = _rms_bwd_math(mv, gp_ref[...], dy)
        dm_ref[...] = dm.astype(BF)

        @pl.when(pl.program_id(0) == 0)
        def _():
            loss_ref[...] = jnp.zeros_like(loss_ref)
            dg_ref[...] = jnp.zeros_like(dg_ref)

        per_row = jnp.sum(diff * diff, axis=-1, keepdims=True) * (1.0 / d)
        loss_ref[...] += 0.5 * jnp.sum(per_row, axis=0, keepdims=True)
        dg_ref[...] += dg

    row = pl.BlockSpec((tr, d), lambda i: (i, 0))
    return pl.pallas_call(
        body, name=name, grid=(t // tr,),
        in_specs=_row_specs(tr, d, 2) + _vec_specs(d, 1) + _row_specs(tr, d, 1),
        out_specs=[row, pl.BlockSpec((1, 1), lambda i: (0, 0)), row, pl.BlockSpec((1, d), lambda i: (0, 0))],
        out_shape=[jax.ShapeDtypeStruct((t, d), F32), jax.ShapeDtypeStruct((1, 1), F32),
                   jax.ShapeDtypeStruct((t, d), BF), jax.ShapeDtypeStruct((1, d), F32)],
        compiler_params=_params(("arbitrary",)),
    )(h, m, g_post.reshape(1, d), target)


def _norm_bwd(name, x, g, dy, res=None, out_dtype=F32, then=None, carries=()):
    t, d = x.shape
    tr = _tile(t, 256)
    has_res, has_then = res is not None, then is not None
    gains, dys = (g, dy) if isinstance(g, (list, tuple)) else ([g], [dy])
    n = len(gains)

    def accumulate(dg_ref, dg):
        @pl.when(pl.program_id(0) == 0)
        def _():
            dg_ref[...] = jnp.zeros_like(dg_ref)

        dg_ref[...] += dg

    def body(x_ref, *refs):
        dy_refs, g_refs, rest = refs[:n], refs[n:2 * n], list(refs[2 * n:])
        dx = rest.pop(0)[...] if has_res else 0.0
        then_refs = (rest.pop(0), rest.pop(0)) if has_then else None
        dx_ref, dg_refs = rest[0], rest[1:1 + n]
        xv = x_ref[...].astype(F32)
        for dy_ref, g_ref, dg_ref in zip(dy_refs, g_refs, dg_refs):
            dx_one, dg = _rms_bwd_math(xv, g_ref[...], dy_ref[...].astype(F32))
            dx = dx + dx_one
            accumulate(dg_ref, dg)
        dx_ref[...] = dx.astype(dx_ref.dtype)
        if has_then:
            d2, dg2 = _rms_bwd_math(then_refs[0][...], then_refs[1][...], dx)
            rest[1 + n][...] = d2.astype(BF)
            accumulate(rest[2 + n], dg2)

    ops = [x, *dys, *[gain.reshape(1, d) for gain in gains]] + ([res] if has_res else [])
    in_specs = _row_specs(tr, d, 1 + n) + _vec_specs(d, n) + _row_specs(tr, d, int(has_res))
    out_specs = [pl.BlockSpec((tr, d), lambda i: (i, 0))] + [pl.BlockSpec((1, d), lambda i: (0, 0))] * n
    out_shape = [jax.ShapeDtypeStruct((t, d), out_dtype)] + [jax.ShapeDtypeStruct((1, d), F32)] * n
    if has_then:
        ops += [then[0], then[1].reshape(1, d)]
        in_specs += _row_specs(tr, d, 1) + _vec_specs(d, 1)
        out_specs += [pl.BlockSpec((tr, d), lambda i: (i, 0)), pl.BlockSpec((1, d), lambda i: (0, 0))]
        out_shape += [jax.ShapeDtypeStruct((t, d), BF), jax.ShapeDtypeStruct((1, d), F32)]
    return _call(body, name=name, grid=(t // tr,), in_specs=in_specs, out_specs=out_specs, out_shape=out_shape,
                 operands=ops, sem=("arbitrary",), carries=carries)


_GELU_C = math.sqrt(2.0 / math.pi)
_GELU_A = 0.044715


def _gelu(x):
    return 0.5 * x * (1.0 + jnp.tanh(_GELU_C * (x + _GELU_A * x * x * x)))


def _gelu_and_grad(x):
    th = jnp.tanh(_GELU_C * (x + _GELU_A * x * x * x))
    grad = 0.5 * (1.0 + th) + 0.5 * x * (1.0 - th * th) * (_GELU_C * (1.0 + 3.0 * _GELU_A * x * x))
    return 0.5 * x * (1.0 + th), grad


def _causal(n):
    return lax.broadcasted_iota(jnp.int32, (n, n), 1) <= lax.broadcasted_iota(jnp.int32, (n, n), 0)


def _sgu_fwd(name, p, v_gain, w_s, b_st, carries=()):
    t, da2 = p.shape
    da = da2 // 2
    ng = da // GROUP

    def body(p_ref, vg_ref, ws_ref, bst_ref, o_ref):
        keep = _causal(GROUP)
        for g in range(ng):
            lo = g * GROUP
            u = _gelu(p_ref[:, lo:lo + GROUP])
            vn = _rms(_gelu(p_ref[:, da + lo:da + lo + GROUP]), vg_ref[:, lo:lo + GROUP])
            w = jnp.where(keep, ws_ref[g], 0.0).astype(BF)
            mixed = jnp.dot(w, vn.astype(BF), preferred_element_type=F32) + bst_ref[:, g:g + 1]
            o_ref[:, lo:lo + GROUP] = (u * mixed).astype(BF)

    return _call(
        body, name=name, grid=(t // GROUP,),
        in_specs=[pl.BlockSpec((GROUP, da2), lambda i: (i, 0)), pl.BlockSpec((1, da), lambda i: (0, 0)),
                  pl.BlockSpec((ng, GROUP, GROUP), lambda i: (0, 0, 0)), pl.BlockSpec((GROUP, ng), lambda i: (0, 0))],
        out_specs=[pl.BlockSpec((GROUP, da), lambda i: (i, 0))], out_shape=[jax.ShapeDtypeStruct((t, da), BF)],
        operands=(p, v_gain, w_s, b_st), sem=("parallel",), carries=carries)


def _sgu_bwd(name, p, v_gain, w_s, w_st, b_st, dout, carries=()):
    t, da2 = p.shape
    da = da2 // 2
    ng = da // GROUP

    def body(p_ref, vg_ref, ws_ref, wst_ref, bst_ref, do_ref, dp_ref, dvg_ref, dws_ref, dbst_ref):
        @pl.when(pl.program_id(0) == 0)
        def _():
            dvg_ref[...] = jnp.zeros_like(dvg_ref)
            dws_ref[...] = jnp.zeros_like(dws_ref)
            dbst_ref[...] = jnp.zeros_like(dbst_ref)

        keep = _causal(GROUP)
        keep_t = lax.broadcasted_iota(jnp.int32, (GROUP, GROUP), 0) <= lax.broadcasted_iota(jnp.int32, (GROUP, GROUP), 1)
        for g in range(ng):
            lo = g * GROUP
            u, du = _gelu_and_grad(p_ref[:, lo:lo + GROUP])
            v, dv_act = _gelu_and_grad(p_ref[:, da + lo:da + lo + GROUP])
            gain = vg_ref[:, lo:lo + GROUP]
            r = lax.rsqrt(jnp.mean(v * v, axis=-1, keepdims=True) + NORM_EPS)
            vh = v * r
            vnb = (vh * gain).astype(BF)
            w = jnp.where(keep, ws_ref[g], 0.0).astype(BF)
            wt = jnp.where(keep_t, wst_ref[g], 0.0).astype(BF)
            mixed = jnp.dot(w, vnb, preferred_element_type=F32) + bst_ref[:, g:g + 1]
            dout_g = do_ref[:, lo:lo + GROUP].astype(F32)
            dmixed = dout_g * u
            dmb = dmixed.astype(BF)
            dbst_ref[:, g:g + 1] += jnp.sum(dmixed, axis=1, keepdims=True)
            dws_ref[g] += jnp.where(keep, lax.dot_general(dmb, vnb, _NT, preferred_element_type=F32), 0.0)
            dvn = jnp.dot(wt, dmb, preferred_element_type=F32)
            dvg_ref[:, lo:lo + GROUP] += jnp.sum(dvn * vh, axis=0, keepdims=True)
            dvh = dvn * gain
            dv = r * dvh - vh * (jnp.sum(vh * dvh, axis=-1, keepdims=True) * (r / GROUP))
            dp_ref[:, lo:lo + GROUP] = (dout_g * mixed * du).astype(BF)
            dp_ref[:, da + lo:da + lo + GROUP] = (dv * dv_act).astype(BF)

    full = lambda *shape: pl.BlockSpec(shape, lambda i: (0,) * len(shape))
    outs = _call(
        body, name=name, grid=(t // GROUP,),
        in_specs=[pl.BlockSpec((GROUP, da2), lambda i: (i, 0)), full(1, da), full(ng, GROUP, GROUP), full(ng, GROUP, GROUP),
                  full(GROUP, ng), pl.BlockSpec((GROUP, da), lambda i: (i, 0))],
        out_specs=[pl.BlockSpec((GROUP, da2), lambda i: (i, 0)), full(1, da), full(ng, GROUP, GROUP), full(GROUP, ng)],
        out_shape=[jax.ShapeDtypeStruct((t, da2), BF), jax.ShapeDtypeStruct((1, da), F32),
                   jax.ShapeDtypeStruct((ng, GROUP, GROUP), F32), jax.ShapeDtypeStruct((GROUP, ng), F32)],
        operands=(p, v_gain, w_s, w_st, b_st, dout), sem=("arbitrary",), carries=carries)
    return outs[:4], outs[4:]


_CONV_ROWS = 256
_CONV_COLS = 128


def _shift_down(x, prev, k):
    top = pltpu.roll(jnp.concatenate([prev, x[:8]], axis=0), k, 0)[8:16]
    return jnp.concatenate([top, pltpu.roll(x, k, 0)[8:]], axis=0)


def _conv_taps(a_ref, half, r0, first):
    rows = _rows(a_ref)
    x = a_ref[half, pl.ds(r0, rows), :]
    if first:
        prev = jnp.zeros((8, x.shape[1]), F32)
        return x, _shift_down(x, prev, 1), _shift_down(x, prev, 2)
    return x, a_ref[half, pl.ds(r0 - 1, rows), :], a_ref[half, pl.ds(r0 - 2, rows), :]


def _rows(a_ref):
    return min(_CONV_ROWS, a_ref.shape[1])


def _conv_fwd(name, a3, cw, cb, carries=()):
    _, t, f = a3.shape
    tc, rows = _CONV_COLS, min(_CONV_ROWS, t)

    def body(a_ref, cw_ref, cb_ref, y_ref):
        def chunk(r0, first):
            c = []
            for half in range(2):
                x, x1, x2 = _conv_taps(a_ref, half, r0, first)
                w = cw_ref[half]
                c.append(cb_ref[half] + (w[0:1] * x2 + w[1:2] * x1 + w[2:3] * x))
            y_ref[pl.ds(r0, rows), :] = (c[0] * jax.nn.sigmoid(c[0]) * c[1]).astype(BF)

        chunk(0, True)

        @pl.loop(1, t // rows)
        def _(r):
            chunk(pl.multiple_of(r * rows, rows), False)

    col = lambda *lead: pl.BlockSpec((*lead, tc), lambda j: (0,) * len(lead) + (j,))
    y, *dsts = _call(
        body, name=name, grid=(f // tc,), in_specs=[col(2, t), col(2, CONV_TAPS), col(2, 1)],
        out_specs=[col(t)], out_shape=[jax.ShapeDtypeStruct((t, f), BF)], operands=(a3, cw, cb),
        sem=("parallel",), vmem=40 * _MIB, carries=carries)
    return y, dsts


def _conv_bwd(name, a3, cw, cb, dy, carries=()):
    _, t, f = a3.shape
    tc, rows = _CONV_COLS, min(_CONV_ROWS, t)
    n_steps = t // rows

    def body(a_ref, cw_ref, cb_ref, dy_ref, da_ref, dcw_ref, dcb_ref, dc_ref):
        dcw_ref[...] = jnp.zeros_like(dcw_ref)
        dcb_ref[...] = jnp.zeros_like(dcb_ref)

        def chunk(r0, first, nxt):
            taps, c = [], []
            for half in range(2):
                x, x1, x2 = _conv_taps(a_ref, half, r0, first)
                w = cw_ref[half]
                taps.append((x2, x1, x))
                c.append(cb_ref[half] + (w[0:1] * x2 + w[1:2] * x1 + w[2:3] * x))
            gate, val = c
            sg = jax.nn.sigmoid(gate)
            dyv = dy_ref[pl.ds(r0, rows), :].astype(F32)
            dcs = (dyv * val * (sg * (1.0 + gate * (1.0 - sg))), dyv * (gate * sg))
            new_nxt = []
            for half in range(2):
                dc, w = dcs[half], cw_ref[half]
                dcb_ref[half] += jnp.sum(dc, axis=0, keepdims=True)
                for tap in range(CONV_TAPS):
                    dcw_ref[half, tap:tap + 1, :] += jnp.sum(dc * taps[half][tap], axis=0, keepdims=True)
                dc_ref[half, 0:rows, :] = dc
                dc_ref[half, rows:rows + 8, :] = nxt[half]
                da = w[2:3] * dc + w[1:2] * dc_ref[half, 1:rows + 1, :] + w[0:1] * dc_ref[half, 2:rows + 2, :]
                da_ref[half, pl.ds(r0, rows), :] = da.astype(BF)
                new_nxt.append(dc[:8])
            return tuple(new_nxt)

        zeros = jnp.zeros((8, tc), F32)
        nxt = lax.fori_loop(0, n_steps - 1, lambda s, nxt: chunk(pl.multiple_of((n_steps - 1 - s) * rows, rows), False, nxt),
                            (zeros, zeros))
        chunk(0, True, nxt)

    col = lambda *lead: pl.BlockSpec((*lead, tc), lambda j: (0,) * len(lead) + (j,))
    outs = _call(
        body, name=name, grid=(f // tc,), in_specs=[col(2, t), col(2, CONV_TAPS), col(2, 1), col(t)],
        out_specs=[col(2, t), col(2, CONV_TAPS), col(2, 1)],
        out_shape=[jax.ShapeDtypeStruct((2, t, f), BF), jax.ShapeDtypeStruct((2, CONV_TAPS, f), F32),
                   jax.ShapeDtypeStruct((2, 1, f), F32)],
        operands=(a3, cw, cb, dy), scratch=[pltpu.VMEM((2, rows + 8, tc), F32)], sem=("parallel",), vmem=48 * _MIB,
        carries=carries)
    return outs[:3], outs[3:]


_ATT_BLOCK = 256


def _split_dot(x, tri):
    hi = x.astype(BF)
    lo = (x - hi.astype(F32)).astype(BF)
    return jnp.dot(hi, tri, preferred_element_type=F32) + jnp.dot(lo, tri, preferred_element_type=F32)


_ATT_HEADS_FWD = 8
_ATT_HEADS_BWD = 4


def _logits(qb, kb, diagonal):
    z = lax.dot_general(qb, kb, _NT, preferred_element_type=F32) * (1.0 / math.sqrt(GROUP))
    lb = jnp.minimum(z, 0.0) - jnp.log(1.0 + jnp.exp(-jnp.abs(z)))
    if not diagonal:
        return lb, lb - z, None
    mask = lax.broadcasted_iota(jnp.int32, z.shape, 1) < lax.broadcasted_iota(jnp.int32, z.shape, 0)
    return lb, jnp.where(mask, lb - z, 0.0), mask


def _head(ref, g, rows=slice(None)):
    return ref[rows, g * GROUP:(g + 1) * GROUP]


def _attn_fwd(name, q, k, v, carries=()):
    t, hd = q.shape
    blk = min(_ATT_BLOCK, t)
    heads = min(_ATT_HEADS_FWD, hd // GROUP)
    gs = range(heads)

    def body(q_ref, k_ref, v_ref, o_ref, rest_ref, first_ref):
        hg, i = pl.program_id(0), pl.program_id(1)
        ri = lax.broadcasted_iota(jnp.int32, (blk, blk), 0)
        ci = lax.broadcasted_iota(jnp.int32, (blk, blk), 1)
        tri = (ri >= ci).astype(BF)

        def tile(j, state, diagonal):
            keys = pl.ds(pl.multiple_of(j * blk, blk), blk)
            logit = [_logits(_head(q_ref, g), _head(k_ref, g, keys), diagonal) for g in gs]
            incl = [_split_dot(logit[g][1], tri) for g in gs]
            a = [jnp.exp(logit[g][0] + (incl[g] - logit[g][1] + state[g][0])) for g in gs]
            if diagonal:
                a = [jnp.where(logit[g][2], a[g], 0.0) for g in gs]
            out = [state[g][1] + jnp.dot(a[g].astype(BF), _head(v_ref, g, keys), preferred_element_type=F32) for g in gs]
            return tuple((state[g][0] + incl[g][:, 0:1], out[g]) for g in gs)

        state = tile(i, ((jnp.zeros((blk, 1), F32), jnp.zeros((blk, GROUP), F32)),) * heads, True)

        def more(carry):
            j, state = carry
            live = functools.reduce(jnp.maximum, [jnp.max(right) for right, _ in state])
            return jnp.logical_and(j >= 0, live > EXP_FLOOR)

        j, state = lax.while_loop(more, lambda c: (c[0] - 1, tile(c[0], c[1], False)), (i - 1, state))
        for g, (right, acc) in enumerate(state):
            o_ref[:, g * GROUP:(g + 1) * GROUP] = acc.astype(BF)
            rest_ref[:, g * GROUP:(g + 1) * GROUP] = jnp.broadcast_to(right, (blk, GROUP))
        first_ref[hg, i] = (j + 1).astype(F32)

    qspec = pl.BlockSpec((blk, heads * GROUP), lambda h, i: (i, h))
    kvspec = pl.BlockSpec((t, heads * GROUP), lambda h, i: (0, h), pipeline_mode=pl.Buffered(1))
    groups = hd // (heads * GROUP)
    outs = _call(
        body, name=name, grid=(groups, t // blk), in_specs=[qspec, kvspec, kvspec],
        out_specs=[qspec, qspec, pl.BlockSpec(memory_space=pltpu.SMEM)],
        out_shape=[jax.ShapeDtypeStruct((t, hd), BF), jax.ShapeDtypeStruct((t, hd), F32),
                   jax.ShapeDtypeStruct((groups, t // blk), F32)],
        operands=(q, k, v), sem=("arbitrary", "arbitrary"), vmem=40 * _MIB, carries=carries, middle_at=0.75)
    return outs[:3], outs[3:]


def _attn_bwd(name, q, k, v, rest, first, do, carries=()):
    t, hd = q.shape
    blk = min(_ATT_BLOCK, t)
    nq = t // blk
    scale = 1.0 / math.sqrt(GROUP)
    heads = min(_ATT_HEADS_BWD, hd // GROUP)
    per_first = min(_ATT_HEADS_FWD, hd // GROUP) // heads
    gs = range(heads)

    def body(first_ref, q_ref, k_ref, v_ref, rest_ref, do_ref, dq_ref, dk_ref, dv_ref, dk_acc, dv_acc):
        hg, i = pl.program_id(0), pl.program_id(1)

        @pl.when(i == 0)
        def _():
            dk_acc[...] = jnp.zeros_like(dk_acc)
            dv_acc[...] = jnp.zeros_like(dv_acc)

        ri = lax.broadcasted_iota(jnp.int32, (blk, blk), 0)
        ci = lax.broadcasted_iota(jnp.int32, (blk, blk), 1)
        tri = (ri <= ci).astype(BF)

        def tile(j, state, diagonal):
            keys = pl.ds(pl.multiple_of(j * blk, blk), blk)
            qs, dos = [_head(q_ref, g) for g in gs], [_head(do_ref, g) for g in gs]
            kb, vb = [_head(k_ref, g, keys) for g in gs], [_head(v_ref, g, keys) for g in gs]
            logit = [_logits(qs[g], kb[g], diagonal) for g in gs]
            pre = [_split_dot(logit[g][1], tri) for g in gs]
            a = [jnp.exp(logit[g][0] + (rest_ref[:, g * GROUP:g * GROUP + 1] - state[g][0] - pre[g])) for g in gs]
            if diagonal:
                a = [jnp.where(logit[g][2], a[g], 0.0) for g in gs]
            gw = [a[g] * lax.dot_general(dos[g], vb[g], _NT, preferred_element_type=F32) for g in gs]
            gpre = [_split_dot(gw[g], tri) for g in gs]
            dz = []
            for g in gs:
                beta = jnp.exp(logit[g][0])
                d = (gw[g] * (1.0 - beta) - (state[g][1] + gpre[g] - gw[g]) * beta) * scale
                dz.append((jnp.where(logit[g][2], d, 0.0) if diagonal else d).astype(BF))
            for g in gs:
                dk_acc[keys, g * GROUP:(g + 1) * GROUP] += lax.dot_general(dz[g], qs[g], _TN, preferred_element_type=F32)
                dv_acc[keys, g * GROUP:(g + 1) * GROUP] += lax.dot_general(a[g].astype(BF), dos[g], _TN,
                                                                           preferred_element_type=F32)
            return tuple((state[g][0] + pre[g][:, blk - 1:blk], state[g][1] + gpre[g][:, blk - 1:blk],
                          state[g][2] + jnp.dot(dz[g], kb[g], preferred_element_type=F32)) for g in gs)

        zero = jnp.zeros((blk, 1), F32)
        first_block = jnp.clip(first_ref[hg // per_first, i].astype(jnp.int32), 0, i)
        state = lax.fori_loop(first_block, i, lambda j, c: tile(j, c, False),
                              ((zero, zero, jnp.zeros((blk, GROUP), F32)),) * heads)
        for g, (_, _, dq) in enumerate(tile(i, state, True)):
            dq_ref[:, g * GROUP:(g + 1) * GROUP] = dq.astype(BF)

        @pl.when(i == nq - 1)
        def _():
            dk_ref[...] = dk_acc[...].astype(BF)
            dv_ref[...] = dv_acc[...].astype(BF)

    qspec = pl.BlockSpec((blk, heads * GROUP), lambda h, i: (i, h))
    kvspec = pl.BlockSpec((t, heads * GROUP), lambda h, i: (0, h), pipeline_mode=pl.Buffered(1))
    outs = _call(
        body, name=name, grid=(hd // (heads * GROUP), nq),
        in_specs=[pl.BlockSpec(memory_space=pltpu.SMEM), qspec, kvspec, kvspec, qspec, qspec],
        out_specs=[qspec, kvspec, kvspec], out_shape=[jax.ShapeDtypeStruct((t, hd), BF)] * 3,
        operands=(first, q, k, v, rest, do),
        scratch=[pltpu.VMEM((t, heads * GROUP), F32), pltpu.VMEM((t, heads * GROUP), F32)],
        sem=("arbitrary", "arbitrary"), vmem=48 * _MIB, carries=carries)
    return outs[:3], outs[3:]


def _adamw_math(w, g, m, v):
    m = ADAM_B1 * m + (1.0 - ADAM_B1) * g
    v = ADAM_B2 * v + (1.0 - ADAM_B2) * (g * g)
    m_hat = m / (1.0 - ADAM_B1 ** ADAM_STEP)
    v_hat = v / (1.0 - ADAM_B2 ** ADAM_STEP)
    return -ADAM_LR * (m_hat / (jnp.sqrt(v_hat) + ADAM_EPS) + ADAM_WD * w), m, v


def _sum_adamw(name, parts, w, m, v, carries=()):
    layers, r, c = w.shape
    budget = 512 * 1024 // layers
    tr = r if r * c <= budget else _tile_rows(r, max(8, (budget // c) // 8 * 8))
    n = r // tr

    def body(*refs):
        p_refs, (w_ref, m_ref, v_ref, g_ref, d_ref, nm_ref, nv_ref) = refs[:layers], refs[layers:]
        for layer, p_ref in enumerate(p_refs):
            @pl.when(pl.program_id(0) == layer)
            def _(p_ref=p_ref):
                g = p_ref[0].astype(F32)
                for dev in range(1, N_DEV):
                    g = g + p_ref[dev].astype(F32)
                g_ref[...] = g
                d_ref[...], nm_ref[...], nv_ref[...] = _adamw_math(w_ref[...], g, m_ref[...], v_ref[...])

    def part_spec(layer):
        return pl.BlockSpec((N_DEV, tr, c), lambda l, i: (0, jnp.where(l < layer, 0, jnp.where(l == layer, i, n - 1)), 0))

    row = pl.BlockSpec((None, tr, c), lambda l, i: (l, i, 0))
    return _call(
        body, name=name, grid=(layers, n), in_specs=[part_spec(layer) for layer in range(layers)] + [row] * 3,
        out_specs=[row] * 4, out_shape=[jax.ShapeDtypeStruct((layers, r, c), F32)] * 4,
        operands=(*parts, w, m, v), sem=("arbitrary", "arbitrary"), vmem=40 * _MIB, carries=carries)


def _tile_rows(r, pref):
    t = min(r, pref)
    while r % t or t % 8:
        t -= 1
    return t


def _all_reduce_small(name, groups, exchange):
    c = groups[0][0].shape[1]
    parts, offsets, starts, r = [], [], [], 0
    for group in groups:
        starts.append(r)
        for p in group:
            parts.append(p)
            offsets.append(r)
            r += p.shape[0]
        r = -(-r // 8) * 8

    def pack(*refs):
        out_ref = refs[-1]
        out_ref[...] = jnp.zeros((r, c), F32)
        for p_ref, off in zip(refs[:-1], offsets):
            out_ref[off:off + p_ref.shape[0], :] = p_ref[...]

    packed = pl.pallas_call(pack, name=name + "_pack", out_shape=jax.ShapeDtypeStruct((r, c), F32))(*parts)
    (gathered,) = exchange(_gather(packed))
    tr = _tile_rows(r, 64)

    def add(g_ref, out_ref):
        total = g_ref[0]
        for dev in range(1, N_DEV):
            total = total + g_ref[dev]
        out_ref[...] = total

    summed = pl.pallas_call(
        add, name=name + "_sum", grid=(r // tr,), in_specs=[pl.BlockSpec((N_DEV, tr, c), lambda i: (0, i, 0))],
        out_specs=pl.BlockSpec((tr, c), lambda i: (i, 0)), out_shape=jax.ShapeDtypeStruct((r, c), F32),
        compiler_params=_params(("parallel",)))(gathered)
    return summed, starts


def _adamw(name, g, w, m, v):
    cols = w.shape[-1]
    flat = lambda a: a.reshape(-1, cols)

    def body(g_ref, w_ref, m_ref, v_ref, d_ref, nm_ref, nv_ref):
        d_ref[...], nm_ref[...], nv_ref[...] = _adamw_math(w_ref[...], g_ref[...], m_ref[...], v_ref[...])

    outs = pl.pallas_call(body, name=name, out_shape=[jax.ShapeDtypeStruct(flat(w).shape, F32)] * 3)(
        flat(g), flat(w), flat(m), flat(v))
    return [o.reshape(w.shape) for o in outs]


def kernel(x, pre_mix_g, post_mix_g, pre_ffn_g, post_ffn_g, a_w_in, a_v_norm_g, a_w_spatial, a_b_spatial, a_w_out, kv_norm_g, w_k, w_v, b_w_q, b_w_o, ffn_w_up, ffn_conv_w, ffn_conv_b, ffn_w_down, loss_target, m_pre_mix_g, m_post_mix_g, m_pre_ffn_g, m_post_ffn_g, m_a_w_in, m_a_v_norm_g, m_a_w_spatial, m_a_b_spatial, m_a_w_out, m_kv_norm_g, m_w_k, m_w_v, m_b_w_q, m_b_w_o, m_ffn_w_up, m_ffn_conv_w, m_ffn_conv_b, m_ffn_w_down, v_pre_mix_g, v_post_mix_g, v_pre_ffn_g, v_post_ffn_g, v_a_w_in, v_a_v_norm_g, v_a_w_spatial, v_a_b_spatial, v_a_w_out, v_kv_norm_g, v_w_k, v_w_v, v_b_w_q, v_b_w_o, v_ffn_w_up, v_ffn_conv_w, v_ffn_conv_b, v_ffn_w_down):
    t, d = x.shape[1], x.shape[2]
    f = ffn_w_down.shape[1] * N_DEV
    ng = d // GROUP
    me = 4 * lax.axis_index("x") + 2 * lax.axis_index("y") + lax.axis_index("c")
    x2, target = x.reshape(t, d), loss_target.reshape(t, d)

    hn0, g_in, g_cw, g_vg = _norms_fwd("pre_mix0", x2, [pre_mix_g[0]], carries=[
        _gather(a_w_in[0].astype(BF)), _gather(ffn_conv_w.reshape(2 * CONV_TAPS, -1)), _gather(a_v_norm_g)])
    up0 = ffn_w_up[0].astype(BF)
    cw_full = jnp.transpose(g_cw.reshape(N_DEV, 2, CONV_TAPS, -1), (1, 2, 0, 3)).reshape(2, CONV_TAPS, 2, f)
    cw_l = [jnp.transpose(cw_full[l], (1, 0, 2)) for l in range(2)]
    cb_l = [ffn_conv_b[l].reshape(2, 1, f) for l in range(2)]
    vg_full = g_vg.reshape(1, d)
    w_s = a_w_spatial[0]
    w_st = jnp.swapaxes(w_s, 1, 2)
    b_st = a_b_spatial[0].T

    p0, (g_out, g_up0) = _mm_nn_blk("sgu_in", hn0, g_in, F32, carries=[
        _gather(a_w_out[0].astype(BF)), _gather(up0, 0, d // 4)])
    w_out_f = g_out.reshape(d, d)
    sg, g_up0 = _sgu_fwd("sgu", p0, vg_full, w_s, b_st, carries=[_gather(up0, d // 4, d // 2, dst=g_up0)])
    mix0, (g_up0,) = _mm_nn("sgu_out", sg, w_out_f, F32, carries=[_gather(up0, d // 2, 3 * d // 4, dst=g_up0)])
    h1, fn0, g_up0 = _resid_norms("post_mix0", x2, mix0, post_mix_g[0], [pre_ffn_g[0]],
                                  carries=[_gather(up0, 3 * d // 4, d, dst=g_up0)])
    a3_0, (g_dn0,) = _mm_nn_blk("ffn0_up", fn0, g_up0, F32, halves=True, carries=[_gather(ffn_w_down[0].astype(BF))])
    y0, (g_q,) = _conv_fwd("ffn0_conv", a3_0, cw_l[0], cb_l[0], carries=[_gather(b_w_q[0].astype(BF))])
    wv = w_v.astype(BF)
    f0, (g_k, g_v) = _mm_nn("ffn0_down", y0, g_dn0.reshape(f, d), F32,
                            carries=[_gather(w_k.astype(BF)), _gather(wv, 0, d // 16)])
    h2, hn1, kvn, g_v = _resid_norms("post_ffn0", h1, f0, post_ffn_g[0], [pre_mix_g[1], kv_norm_g],
                                     carries=[_gather(wv, d // 16, d // 8, dst=g_v)])
    w_q_f, w_k_f, w_v_f = g_q.reshape(d, d), g_k.reshape(d, d), g_v.reshape(d, d)
    up1 = ffn_w_up[1].astype(BF)
    e = d // 8
    q, (g_up1,) = _mm_nn("attn_q", hn1, w_q_f, BF, carries=[_gather(up1, 0, e)])
    kk, (g_up1,) = _mm_nn("attn_k", kvn, w_k_f, BF, carries=[_gather(up1, e, 2 * e, dst=g_up1)])
    vv, (g_up1,) = _mm_nn("attn_v", kvn, w_v_f, BF, carries=[_gather(up1, 2 * e, 3 * e, dst=g_up1)])
    (att, rest, first), (g_o, g_up1) = _attn_fwd(
        "attn", q, kk, vv, carries=[_gather(b_w_o[0].astype(BF)), _gather(up1, 3 * e, 6 * e, dst=g_up1)])
    w_o_f = g_o.reshape(d, d)
    mix1, (g_up1,) = _mm_nn("attn_o", att, w_o_f, F32, carries=[_gather(up1, 6 * e, 7 * e, dst=g_up1)])
    h3, fn1, g_up1 = _resid_norms("post_mix1", h2, mix1, post_mix_g[1], [pre_ffn_g[1]],
                                  carries=[_gather(up1, 7 * e, d, dst=g_up1)])
    a3_1, (g_dn1,) = _mm_nn_blk("ffn1_up", fn1, g_up1, F32, halves=True, carries=[_gather(ffn_w_down[1].astype(BF))])
    y1, _ = _conv_fwd("ffn1_conv", a3_1, cw_l[1], cb_l[1])
    f1 = _mm_nn("ffn1_down", y1, g_dn1.reshape(f, d), F32)
    g_up = (g_up0, g_up1)
    w_dn_f = (g_dn0.reshape(f, d), g_dn1.reshape(f, d))
    dh, loss_part, df1, d_post_ffn1 = _resid_loss("loss", h3, f1, post_ffn_g[1], target)
    loss = lax.psum(loss_part[0, 0], ("x", "y", "c"))

    def blocks(dw):
        return dw.reshape(N_DEV, -1, d)

    def split(result, carries):
        return result if carries else (result, [])

    def ffn_bwd(l, dh_out, dfo, h_in, fn, a3, yv, mix, with_dw=(), with_dx=(), with_up=()):
        dw_dn, sent_dw = split(_mm_tn(f"ffn{l}_down_dw", yv, dfo, carries=with_dw), with_dw)
        dy, sent_dx = split(_mm_nt(f"ffn{l}_down_dx", dfo, w_dn_f[l], BF, carries=with_dx), with_dx)
        dw_dn = blocks(dw_dn)
        cut = dw_dn.shape[1] * 5 // 8 // 16 * 16
        (da3, dcw, dcb), (p_dn,) = _conv_bwd(f"ffn{l}_conv_bwd", a3, cw_l[l], cb_l[l], dy, carries=[_scatter(dw_dn, 0, cut)])
        dw_up, (p_dn, *sent_up) = _mm_tn_blk(f"ffn{l}_up_dw", fn, da3, halves=True,
                                             carries=[_scatter(dw_dn, cut, dst=p_dn), *with_up])
        dfn, (p_up,) = _mm_nt_blk(f"ffn{l}_up_dx", da3, g_up[l], F32, halves=True, carries=[_scatter(dw_up, 0, d // 2)])
        dh_in, d_pre, dmix, d_post_mix, p_up = _norm_bwd(
            f"pre_ffn{l}_bwd", h_in, pre_ffn_g[l], dfn, res=dh_out, then=(mix, post_mix_g[l]),
            carries=[_scatter(dw_up, d // 2, 9 * d // 16, dst=p_up)])
        return dh_in, dmix, d_pre, d_post_mix, dcw, dcb, p_dn, dw_up, p_up, list(sent_dw) + list(sent_dx) + list(sent_up)

    dh3, dmix1, d_pre_ffn1, d_post_mix1, dcw1, dcb1, p_dn1, dw_up1, p_up1, _ = ffn_bwd(
        1, dh, df1, h3, fn1, a3_1, y1, mix1)
    dw_o = _mm_tn("attn_o_dw", att, dmix1)
    datt = _mm_nt("attn_o_dx", dmix1, w_o_f, BF)
    (dq, dk, dv), (p_up1,) = _attn_bwd(
        "attn_bwd", q, kk, vv, rest, first, datt, carries=[_scatter(dw_up1, 9 * d // 16, d, dst=p_up1)])
    dw_o, qr = blocks(dw_o), d // 32
    dw_q, (p_o,) = _mm_tn("attn_q_dw", hn1, dq, carries=[_scatter(dw_o, 0, qr)])
    dw_k, (p_o,) = _mm_tn("attn_k_dw", kvn, dk, carries=[_scatter(dw_o, qr, 2 * qr, dst=p_o)])
    dw_v, (p_o,) = _mm_tn("attn_v_dw", kvn, dv, carries=[_scatter(dw_o, 2 * qr, 3 * qr, dst=p_o)])
    dhn1, (p_o,) = _mm_nt("attn_q_dx", dq, w_q_f, F32, carries=[_scatter(dw_o, 3 * qr, 4 * qr, dst=p_o)])
    dkvn = _mm_nt("attn_v_dx", dv, w_v_f, F32, add=_mm_nt("attn_k_dx", dk, w_k_f, F32))
    dh2, d_pre_mix1, d_kv, df0, d_post_ffn0 = _norm_bwd(
        "pre_mix1_kv_bwd", h2, [pre_mix_g[1], kv_norm_g], [dhn1, dkvn], res=dh3, then=(f0, post_ffn_g[0]))
    dh1, dmix0, d_pre_ffn0, d_post_mix0, dcw0, dcb0, p_dn0, dw_up0, p_up0, (p_q, p_k, p_v) = ffn_bwd(
        0, dh2, df0, h1, fn0, a3_0, y0, mix0, with_dw=[_scatter(blocks(dw_q))], with_dx=[_scatter(blocks(dw_k))],
        with_up=[_scatter(blocks(dw_v))])
    dw_out, (p_up0,) = _mm_tn("sgu_out_dw", sg, dmix0, carries=[_scatter(dw_up0, 9 * d // 16, 11 * d // 16, dst=p_up0)])
    dsg = _mm_nt("sgu_out_dx", dmix0, w_out_f, BF)
    (dp0, d_vg, d_ws, d_bst), (p_up0,) = _sgu_bwd(
        "sgu_bwd", p0, vg_full, w_s, w_st, b_st, dsg, carries=[_scatter(dw_up0, 11 * d // 16, 15 * d // 16, dst=p_up0)])
    dw_in, (p_out,) = _mm_tn_blk("sgu_in_dw", hn0, dp0, carries=[_scatter(blocks(dw_out))])
    dhn0, (p_in,) = _mm_nt_blk("sgu_in_dx", dp0, g_in, F32, carries=[_scatter(dw_in, 0, 5 * d // 8)])
    grad_x, d_pre_mix0, p_in = _norm_bwd("pre_mix0_bwd", x2, pre_mix_g[0], dhn0, res=dh1,
                                         carries=[_scatter(dw_in, 5 * d // 8, 7 * d // 8, dst=p_in)])

    def stacked(name, part0, part1, w, m, v, carries):
        shape = (2, *part0.shape[1:])
        outs = _sum_adamw(name, [part0, part1], w.reshape(shape), m.reshape(shape), v.reshape(shape), carries=carries)
        return [o.reshape(w.shape) for o in outs[:4]], outs[4:]

    update_down, (p_up0, p_in) = stacked(
        "adam_ffn_w_down", p_dn0, p_dn1, ffn_w_down, m_ffn_w_down, v_ffn_w_down,
        [_scatter(dw_up0, 15 * d // 16, d, dst=p_up0), _scatter(dw_in, 7 * d // 8, d, dst=p_in)])
    update_up = []

    def gather_behind_adam_up(carry):
        update, gathered = stacked("adam_ffn_w_up", p_up0, p_up1, ffn_w_up, m_ffn_w_up, v_ffn_w_up, [carry])
        update_up.extend(update)
        return gathered

    def conv_w_grad(dcw):
        return jnp.transpose(dcw, (1, 0, 2)).reshape(CONV_TAPS, 2 * f)

    small = [
        ([d_pre_mix0, d_pre_mix1], (2, d)), ([d_post_mix0, d_post_mix1], (2, d)),
        ([d_pre_ffn0, d_pre_ffn1], (2, d)), ([d_post_ffn0, d_post_ffn1], (2, d)),
        ([d_kv], (d,)), ([d_vg], (1, d)), ([d_bst.T], (1, ng, GROUP)), ([d_ws], (1, ng, GROUP, GROUP)),
        ([dcb0, dcb1], (2, 2 * f)), ([conv_w_grad(dcw0), conv_w_grad(dcw1)], (2, CONV_TAPS, 2 * f)),
    ]
    width = V7X_LANES * math.gcd(d // V7X_LANES, 2 * f // V7X_LANES)
    summed, offsets = _all_reduce_small("reduce_small", [[a.reshape(-1, width) for a in group] for group, _ in small],
                                        gather_behind_adam_up)
    full = [summed[off:off + math.prod(shape) // width].reshape(shape) for off, (_, shape) in zip(offsets, small)]
    g_pre_mix, g_post_mix, g_pre_ffn, g_post_ffn, g_kv, g_vgain, g_bs, g_ws, g_cb, g_cwf = full
    cw_w = 2 * f // N_DEV
    g_vgain = lax.dynamic_slice_in_dim(g_vgain, me * (d // N_DEV), d // N_DEV, axis=1)
    g_cwf = lax.dynamic_slice_in_dim(g_cwf, me * cw_w, cw_w, axis=2)

    parts = [p_in, p_out, p_k, p_v, p_q, p_o]

    def small_update(name, g, w, m, v):
        return [g] + _adamw(name, g, w, m, v)

    def single(name, part, w, m, v):
        shape = (1, *part.shape[1:])
        return [o.reshape(w.shape) for o in _sum_adamw(name, [part], w.reshape(shape), m.reshape(shape), v.reshape(shape))]

    results = {
        "pre_mix_g": small_update("adam_pre_mix", g_pre_mix, pre_mix_g, m_pre_mix_g, v_pre_mix_g),
        "post_mix_g": small_update("adam_post_mix", g_post_mix, post_mix_g, m_post_mix_g, v_post_mix_g),
        "pre_ffn_g": small_update("adam_pre_ffn", g_pre_ffn, pre_ffn_g, m_pre_ffn_g, v_pre_ffn_g),
        "post_ffn_g": small_update("adam_post_ffn", g_post_ffn, post_ffn_g, m_post_ffn_g, v_post_ffn_g),
        "a_w_in": single("adam_a_w_in", parts[0], a_w_in, m_a_w_in, v_a_w_in),
        "a_v_norm_g": small_update("adam_a_v_norm", g_vgain, a_v_norm_g, m_a_v_norm_g, v_a_v_norm_g),
        "a_w_spatial": small_update("adam_a_w_spatial", g_ws, a_w_spatial, m_a_w_spatial, v_a_w_spatial),
        "a_b_spatial": small_update("adam_a_b_spatial", g_bs, a_b_spatial, m_a_b_spatial, v_a_b_spatial),
        "a_w_out": single("adam_a_w_out", parts[1], a_w_out, m_a_w_out, v_a_w_out),
        "kv_norm_g": small_update("adam_kv_norm", g_kv, kv_norm_g, m_kv_norm_g, v_kv_norm_g),
        "w_k": single("adam_w_k", parts[2], w_k, m_w_k, v_w_k),
        "w_v": single("adam_w_v", parts[3], w_v, m_w_v, v_w_v),
        "b_w_q": single("adam_b_w_q", parts[4], b_w_q, m_b_w_q, v_b_w_q),
        "b_w_o": single("adam_b_w_o", parts[5], b_w_o, m_b_w_o, v_b_w_o),
        "ffn_w_up": update_up,
        "ffn_conv_w": small_update("adam_ffn_conv_w", g_cwf, ffn_conv_w, m_ffn_conv_w, v_ffn_conv_w),
        "ffn_conv_b": small_update("adam_ffn_conv_b", g_cb, ffn_conv_b, m_ffn_conv_b, v_ffn_conv_b),
        "ffn_w_down": update_down,
    }
    order = ["pre_mix_g", "post_mix_g", "pre_ffn_g", "post_ffn_g", "a_w_in", "a_v_norm_g", "a_w_spatial", "a_b_spatial",
             "a_w_out", "kv_norm_g", "w_k", "w_v", "b_w_q", "b_w_o", "ffn_w_up", "ffn_conv_w", "ffn_conv_b", "ffn_w_down"]
    outs = [loss, grad_x.reshape(x.shape)]
    for idx in range(4):
        outs += [results[n][idx] for n in order]
    return tuple(outs)
```

```python
import functools
import math
from typing import NamedTuple, Optional

import jax
import jax.numpy as jnp
from jax import lax
from jax.experimental import pallas as pl
from jax.experimental.pallas import tpu as pltpu

F32 = jnp.float32
BF = jnp.bfloat16
MESH = pl.DeviceIdType.MESH

N_DEV = 8
NORM_EPS = 1e-6
GROUP = 128
CONV_TAPS = 3
ADAM_LR, ADAM_B1, ADAM_B2, ADAM_EPS, ADAM_WD, ADAM_STEP = 0.001, 0.9, 0.999, 1e-08, 0.01, 10
EXP_FLOOR = -104.0

V7X_LANES = 128
V7X_VMEM_BYTES = 64 * 1024 * 1024
_MIB = 1024 * 1024

_NN = (((1,), (0,)), ((), ()))
_NT = (((1,), (1,)), ((), ()))
_TN = (((0,), (0,)), ((), ()))


def _tile(n, pref):
    if n <= pref:
        return n
    t = (pref // V7X_LANES) * V7X_LANES
    while t > V7X_LANES and n % t:
        t -= V7X_LANES
    assert n % t == 0, (n, pref)
    return t


def _nbytes(shape, dtype):
    return math.prod(shape) * jnp.dtype(dtype).itemsize


def _params(sem=None, vmem=None):
    kw = {}
    if sem is not None:
        kw["dimension_semantics"] = sem
    if vmem is not None:
        kw["vmem_limit_bytes"] = int(min(max(vmem, 16 * _MIB), V7X_VMEM_BYTES - 8 * _MIB))
    return pltpu.CompilerParams(**kw)


def _place():
    x, y, c = lax.axis_index("x"), lax.axis_index("y"), lax.axis_index("c")
    return x, y, c, 4 * x + 2 * y + c


def _flip(x, y, c, k):
    return (1 - x if k & 4 else x, 1 - y if k & 2 else y, 1 - c if k & 1 else c)


class _Carry(NamedTuple):
    gather: bool
    src: jax.Array
    dst: Optional[jax.Array]
    lo: int
    hi: int


def _gather(src, lo=0, hi=None, dst=None):
    return _Carry(True, src, dst, lo, src.shape[0] if hi is None else hi)


def _scatter(src, lo=0, hi=None, dst=None):
    return _Carry(False, src, dst, lo, src.shape[1] if hi is None else hi)


def _carry_phases(carries, srcs, dsts, send_sems, recv_sems, local_sems):
    x, y, c, me = _place()
    here, sibling = (x, y, c), (x, y, 1 - c)
    chips = [(1 - x, y), (x, 1 - y), (1 - x, 1 - y)]

    def rows(u):
        return pl.ds(carries[u].lo, carries[u].hi - carries[u].lo)

    def block_copy(u, sem, block, to, from_src=False):
        slot = dsts[u].at[4 * block[0] + 2 * block[1] + block[2], rows(u)]
        return pltpu.make_async_remote_copy(
            src_ref=srcs[u].at[rows(u)] if from_src else slot, dst_ref=slot, send_sem=send_sems.at[u, sem],
            recv_sem=recv_sems.at[u, sem], device_id=to, device_id_type=MESH)

    def partial_copy(u, k):
        peer = _flip(x, y, c, k)
        return pltpu.make_async_remote_copy(
            src_ref=srcs[u].at[4 * peer[0] + 2 * peer[1] + peer[2], rows(u)], dst_ref=dsts[u].at[me, rows(u)],
            send_sem=send_sems.at[u, k - 1], recv_sem=recv_sems.at[u, k - 1], device_id=peer, device_id_type=MESH)

    def local_copy(u):
        src = srcs[u].at[rows(u)] if carries[u].gather else srcs[u].at[me, rows(u)]
        return pltpu.make_async_copy(src, dsts[u].at[me, rows(u)], local_sems.at[u])

    def first():
        for u, cr in enumerate(carries):
            local_copy(u).start()
            if cr.gather:
                block_copy(u, 0, here, sibling, from_src=True).start()
                for j, chip in enumerate(chips):
                    block_copy(u, 1 + j, here, (*chip, c), from_src=True).start()
            else:
                for k in range(1, N_DEV):
                    partial_copy(u, k).start()

    def middle():
        for u, cr in enumerate(carries):
            if cr.gather:
                for j, chip in enumerate(chips):
                    block_copy(u, 1 + j, (*chip, c), here).wait_recv()
                    block_copy(u, 4 + j, (*chip, c), sibling).start()

    def last():
        for u, cr in enumerate(carries):
            if cr.gather:
                block_copy(u, 0, sibling, here).wait_recv()
                for j, chip in enumerate(chips):
                    block_copy(u, 4 + j, (*chip, 1 - c), here).wait_recv()
                block_copy(u, 0, here, sibling, from_src=True).wait_send()
                for j, chip in enumerate(chips):
                    block_copy(u, 1 + j, here, (*chip, c), from_src=True).wait_send()
                    block_copy(u, 4 + j, (*chip, c), sibling).wait_send()
            else:
                for k in range(1, N_DEV):
                    partial_copy(u, k).wait()
            local_copy(u).wait()

    return first, middle, last


def _call(body, *, name, grid, in_specs, out_specs, out_shape, operands, scratch=(), sem=None, vmem=None,
          carries=(), middle_at=0.6):
    if not carries:
        return pl.pallas_call(
            body, name=name, grid=grid, in_specs=in_specs, out_specs=out_specs, out_shape=out_shape,
            scratch_shapes=list(scratch), compiler_params=_params(sem, vmem))(*operands)
    n_in, n_out, n_scr, nc = len(in_specs), len(out_specs), len(scratch), len(carries)
    given = [u for u, cr in enumerate(carries) if cr.dst is not None]
    steps = math.prod(grid)
    middle_step = min(steps - 1, int(steps * middle_at))

    def wrapped(*refs):
        ins, srcs = refs[:n_in], refs[n_in:n_in + nc]
        at = n_in + nc + len(given)
        outs, dsts = refs[at:at + n_out], refs[at + n_out:at + n_out + nc]
        at += n_out + nc
        scr, (send_sems, recv_sems, local_sems) = refs[at:at + n_scr], refs[at + n_scr:]
        first, middle, last = _carry_phases(carries, srcs, dsts, send_sems, recv_sems, local_sems)
        step = 0
        for axis, size in enumerate(grid):
            step = step * size + pl.program_id(axis)
        pl.when(step == 0)(first)
        body(*ins, *outs, *scr)
        pl.when(step == middle_step)(middle)
        pl.when(step == steps - 1)(last)

    any_spec = pl.BlockSpec(memory_space=pl.ANY)
    dst_shapes = [jax.ShapeDtypeStruct((N_DEV, *cr.src.shape) if cr.gather else cr.src.shape, cr.src.dtype) for cr in carries]
    return pl.pallas_call(
        wrapped, name=name, grid=grid, in_specs=list(in_specs) + [any_spec] * (nc + len(given)),
        out_specs=list(out_specs) + [any_spec] * nc, out_shape=list(out_shape) + dst_shapes,
        input_output_aliases={n_in + nc + g: n_out + u for g, u in enumerate(given)},
        scratch_shapes=list(scratch) + [pltpu.SemaphoreType.DMA((nc, 7)), pltpu.SemaphoreType.DMA((nc, 7)),
                                        pltpu.SemaphoreType.DMA((nc,))],
        compiler_params=_params(("arbitrary",) * len(grid), vmem),
    )(*operands, *[cr.src for cr in carries], *[carries[u].dst for u in given])


def _mm(name, a, b, *, dims, grid, a_blk, a_map, b_blk, b_map, o_blk, o_map, out_shape, out_dtype,
        add=None, add_blk=None, add_map=None, carries=(), b_slabs=1):
    nk = grid[2]
    assert add is None or nk == 1
    acc_shape = tuple(d for d in o_blk if d is not None)
    in_place = out_dtype == F32

    def body(*refs):
        if add is None:
            a_ref, b_ref, o_ref = refs[:3]
            c_ref, scr = None, refs[3:]
        else:
            a_ref, b_ref, c_ref, o_ref = refs[:4]
            scr = refs[4:]
        if b_slabs == 1:
            part = lax.dot_general(a_ref[...], b_ref[...], dims, preferred_element_type=F32)
        else:
            cw = b_ref.shape[2]
            part = sum(lax.dot_general(a_ref[:, s * cw:(s + 1) * cw], b_ref[s], dims, preferred_element_type=F32)
                       for s in range(b_slabs))
        if c_ref is not None:
            part = part + c_ref[...].astype(F32)
        if nk == 1:
            o_ref[...] = part.astype(o_ref.dtype)
            return
        acc = o_ref if in_place else scr[0]
        k = pl.program_id(2)

        @pl.when(k == 0)
        def _():
            acc[...] = part

        @pl.when(k > 0 if in_place else jnp.logical_and(k > 0, k < nk - 1))
        def _():
            acc[...] += part

        if not in_place:
            @pl.when(k == nk - 1)
            def _():
                o_ref[...] = (acc[...] + part).astype(o_ref.dtype)

    in_specs = [pl.BlockSpec(a_blk, a_map), pl.BlockSpec(b_blk, b_map)]
    operands = [a, b]
    scratch = [pltpu.VMEM(acc_shape, F32)] if nk > 1 and not in_place else []
    vmem = 2 * (_nbytes(acc_shape, out_dtype) + _nbytes([d for d in a_blk if d], a.dtype)
                + _nbytes([d for d in b_blk if d], b.dtype)) + (2 + len(scratch)) * _nbytes(acc_shape, F32)
    if add is not None:
        in_specs.append(pl.BlockSpec(add_blk, add_map))
        operands.append(add)
        vmem += 2 * _nbytes(acc_shape, add.dtype)
    out, *dsts = _call(
        body, name=name, grid=grid, in_specs=in_specs, out_specs=[pl.BlockSpec(o_blk, o_map)],
        out_shape=[jax.ShapeDtypeStruct(out_shape, out_dtype)], operands=operands, scratch=scratch,
        sem=("parallel", "parallel", "arbitrary"), vmem=vmem + 8 * _MIB, carries=carries)
    return (out, dsts) if carries else out


def _mm_nn(name, x, w, out_dtype, carries=()):
    t, kd = x.shape
    n = w.shape[1]
    tm, tn, tk = _tile(t, 1024), _tile(n, 2048 if kd > 2048 else 1024), _tile(kd, 1536 if kd > 2048 else 2048)
    return _mm(name, x, w, dims=_NN, grid=(t // tm, n // tn, kd // tk),
               a_blk=(tm, tk), a_map=lambda i, j, k: (i, k), b_blk=(tk, tn), b_map=lambda i, j, k: (k, j),
               o_blk=(tm, tn), o_map=lambda i, j, k: (i, j), out_shape=(t, n), out_dtype=out_dtype, carries=carries)


def _mm_nn_blk(name, x, g, out_dtype, halves=False, carries=()):
    t, kd = x.shape
    cw = g.shape[2]
    tm = _tile(t, 1024)
    if halves:
        o_blk, o_map, out_shape = (None, tm, cw), (lambda i, j, k: (j // 4, i, j % 4)), (2, t, 4 * cw)
    else:
        o_blk, o_map, out_shape = (tm, cw), (lambda i, j, k: (i, j)), (t, N_DEV * cw)
    return _mm(name, x, g, dims=_NN, grid=(t // tm, N_DEV, 1),
               a_blk=(tm, kd), a_map=lambda i, j, k: (i, 0), b_blk=(None, kd, cw), b_map=lambda i, j, k: (j, 0, 0),
               o_blk=o_blk, o_map=o_map, out_shape=out_shape, out_dtype=out_dtype, carries=carries)


def _mm_nt(name, dy, w, out_dtype, add=None, carries=()):
    t, n = dy.shape
    kd = w.shape[0]
    tm, tn = _tile(t, 1024), _tile(kd, 1408)
    kw = {}
    if add is not None:
        kw = dict(add=add, add_blk=(tm, tn), add_map=lambda i, j, k: (i, j))
    return _mm(name, dy, w, dims=_NT, grid=(t // tm, kd // tn, 1),
               a_blk=(tm, n), a_map=lambda i, j, k: (i, 0), b_blk=(tn, n), b_map=lambda i, j, k: (j, 0),
               o_blk=(tm, tn), o_map=lambda i, j, k: (i, j), out_shape=(t, kd), out_dtype=out_dtype, carries=carries, **kw)


def _mm_nt_blk(name, dy, g, out_dtype, halves=False, carries=()):
    kd, cw = g.shape[1], g.shape[2]
    t = dy.shape[1] if halves else dy.shape[0]
    tm = _tile(t, 512)
    slabs = 2 if cw > 512 else 4
    nk = N_DEV // slabs
    if halves:
        a_blk, a_map = (None, tm, slabs * cw), (lambda i, j, k: (k // (nk // 2), i, k % (nk // 2)))
    else:
        a_blk, a_map = (tm, slabs * cw), (lambda i, j, k: (i, k))
    return _mm(name, dy, g, dims=_NT, grid=(t // tm, 1, nk), b_slabs=slabs,
               a_blk=a_blk, a_map=a_map, b_blk=(slabs, kd, cw), b_map=lambda i, j, k: (k, 0, 0),
               o_blk=(tm, kd), o_map=lambda i, j, k: (i, 0), out_shape=(t, kd), out_dtype=out_dtype, carries=carries)


def _mm_tn(name, x, dy, carries=()):
    t, kd = x.shape
    n = dy.shape[1]
    tmx, tk = _tile(kd, 1024), _tile(t, 2048)
    return _mm(name, x, dy, dims=_TN, grid=(kd // tmx, 1, t // tk),
               a_blk=(tk, tmx), a_map=lambda i, j, k: (k, i), b_blk=(tk, n), b_map=lambda i, j, k: (k, 0),
               o_blk=(tmx, n), o_map=lambda i, j, k: (i, 0), out_shape=(kd, n), out_dtype=BF, carries=carries)


def _mm_tn_blk(name, x, dy, halves=False, carries=()):
    t, kd = x.shape
    cw = dy.shape[2] // 4 if halves else dy.shape[1] // N_DEV
    tmx, tk = _tile(kd, 2048 if cw <= 512 else 1024), _tile(t, 2048)
    if halves:
        b_blk, b_map = (None, tk, cw), (lambda i, j, k: (j // 4, k, j % 4))
    else:
        b_blk, b_map = (tk, cw), (lambda i, j, k: (k, j))
    return _mm(name, x, dy, dims=_TN, grid=(kd // tmx, N_DEV, t // tk),
               a_blk=(tk, tmx), a_map=lambda i, j, k: (k, i), b_blk=b_blk, b_map=b_map,
               o_blk=(None, tmx, cw), o_map=lambda i, j, k: (j, i, 0), out_shape=(N_DEV, kd, cw), out_dtype=BF,
               carries=carries)


def _rms(x, g):
    r = lax.rsqrt(jnp.mean(x * x, axis=-1, keepdims=True) + NORM_EPS)
    return x * r * g


def _rms_bwd_math(x, g, dy):
    d = x.shape[-1]
    r = lax.rsqrt(jnp.mean(x * x, axis=-1, keepdims=True) + NORM_EPS)
    xh = x * r
    u = dy * g
    dx = r * u - xh * (jnp.sum(xh * u, axis=-1, keepdims=True) * (r / d))
    return dx, jnp.sum(dy * xh, axis=0, keepdims=True)


def _row_specs(tr, d, n):
    return [pl.BlockSpec((tr, d), lambda i: (i, 0)) for _ in range(n)]


def _vec_specs(d, n):
    return [pl.BlockSpec((1, d), lambda i: (0, 0)) for _ in range(n)]


def _norms_fwd(name, h, gains, carries=()):
    t, d = h.shape
    tr, ng = _tile(t, 256), len(gains)

    def body(h_ref, *refs):
        x = h_ref[...]
        for g_ref, o_ref in zip(refs[:ng], refs[ng:]):
            o_ref[...] = _rms(x, g_ref[...]).astype(BF)

    return _call(
        body, name=name, grid=(t // tr,), in_specs=_row_specs(tr, d, 1) + _vec_specs(d, ng),
        out_specs=_row_specs(tr, d, ng), out_shape=[jax.ShapeDtypeStruct((t, d), BF)] * ng,
        operands=(h, *[g.reshape(1, d) for g in gains]), sem=("parallel",), carries=carries)


def _resid_norms(name, h, m, g_post, gains, carries=()):
    t, d = h.shape
    tr, ng = _tile(t, 256), len(gains)

    def body(h_ref, m_ref, gp_ref, *refs):
        hn = h_ref[...] + _rms(m_ref[...], gp_ref[...])
        refs[ng][...] = hn
        for g_ref, o_ref in zip(refs[:ng], refs[ng + 1:]):
            o_ref[...] = _rms(hn, g_ref[...]).astype(BF)

    return _call(
        body, name=name, grid=(t // tr,), in_specs=_row_specs(tr, d, 2) + _vec_specs(d, 1 + ng),
        out_specs=_row_specs(tr, d, 1 + ng),
        out_shape=[jax.ShapeDtypeStruct((t, d), F32)] + [jax.ShapeDtypeStruct((t, d), BF)] * ng,
        operands=(h, m, g_post.reshape(1, d), *[g.reshape(1, d) for g in gains]), sem=("parallel",), carries=carries)


def _resid_loss(name, h, m, g_post, target):
    t, d = h.shape
    tr = _tile(t, 256)

    def body(h_ref, m_ref, gp_ref, t_ref, dy_ref, loss_ref, dm_ref, dg_ref):
        mv = m_ref[...]
        diff = h_ref[...] + _rms(mv, gp_ref[...]) - t_ref[...]
        dy = diff * (1.0 / d)
        dy_ref[...] = dy
        dm, dg = _rms_bwd_math(mv, gp_ref[...], dy)
        dm_ref[...] = dm.astype(BF)

        @pl.when(pl.program_id(0) == 0)
        def _():
            loss_ref[...] = jnp.zeros_like(loss_ref)
            dg_ref[...] = jnp.zeros_like(dg_ref)

        per_row = jnp.sum(diff * diff, axis=-1, keepdims=True) * (1.0 / d)
        loss_ref[...] += 0.5 * jnp.sum(per_row, axis=0, keepdims=True)
        dg_ref[...] += dg

    row = pl.BlockSpec((tr, d), lambda i: (i, 0))
    return pl.pallas_call(
        body, name=name, grid=(t // tr,),
        in_specs=_row_specs(tr, d, 2) + _vec_specs(d, 1) + _row_specs(tr, d, 1),
        out_specs=[row, pl.BlockSpec((1, 1), lambda i: (0, 0)), row, pl.BlockSpec((1, d), lambda i: (0, 0))],
        out_shape=[jax.ShapeDtypeStruct((t, d), F32), jax.ShapeDtypeStruct((1, 1), F32),
                   jax.ShapeDtypeStruct((t, d), BF), jax.ShapeDtypeStruct((1, d), F32)],
        compiler_params=_params(("arbitrary",)),
    )(h, m, g_post.reshape(1, d), target)


def _norm_bwd(name, x, g, dy, res=None, out_dtype=F32, then=None, carries=()):
    t, d = x.shape
    tr = _tile(t, 256)
    has_res, has_then = res is not None, then is not None
    gains, dys = (g, dy) if isinstance(g, (list, tuple)) else ([g], [dy])
    n = len(gains)

    def accumulate(dg_ref, dg):
        @pl.when(pl.program_id(0) == 0)
        def _():
            dg_ref[...] = jnp.zeros_like(dg_ref)

        dg_ref[...] += dg

    def body(x_ref, *refs):
        dy_refs, g_refs, rest = refs[:n], refs[n:2 * n], list(refs[2 * n:])
        dx = rest.pop(0)[...] if has_res else 0.0
        then_refs = (rest.pop(0), rest.pop(0)) if has_then else None
        dx_ref, dg_refs = rest[0], rest[1:1 + n]
        xv = x_ref[...].astype(F32)
        for dy_ref, g_ref, dg_ref in zip(dy_refs, g_refs, dg_refs):
            dx_one, dg = _rms_bwd_math(xv, g_ref[...], dy_ref[...].astype(F32))
            dx = dx + dx_one
            accumulate(dg_ref, dg)
        dx_ref[...] = dx.astype(dx_ref.dtype)
        if has_then:
            d2, dg2 = _rms_bwd_math(then_refs[0][...], then_refs[1][...], dx)
            rest[1 + n][...] = d2.astype(BF)
            accumulate(rest[2 + n], dg2)

    ops = [x, *dys, *[gain.reshape(1, d) for gain in gains]] + ([res] if has_res else [])
    in_specs = _row_specs(tr, d, 1 + n) + _vec_specs(d, n) + _row_specs(tr, d, int(has_res))
    out_specs = [pl.BlockSpec((tr, d), lambda i: (i, 0))] + [pl.BlockSpec((1, d), lambda i: (0, 0))] * n
    out_shape = [jax.ShapeDtypeStruct((t, d), out_dtype)] + [jax.ShapeDtypeStruct((1, d), F32)] * n
    if has_then:
        ops += [then[0], then[1].reshape(1, d)]
        in_specs += _row_specs(tr, d, 1) + _vec_specs(d, 1)
        out_specs += [pl.BlockSpec((tr, d), lambda i: (i, 0)), pl.BlockSpec((1, d), lambda i: (0, 0))]
        out_shape += [jax.ShapeDtypeStruct((t, d), BF), jax.ShapeDtypeStruct((1, d), F32)]
    return _call(body, name=name, grid=(t // tr,), in_specs=in_specs, out_specs=out_specs, out_shape=out_shape,
                 operands=ops, sem=("arbitrary",), carries=carries)


_GELU_C = math.sqrt(2.0 / math.pi)
_GELU_A = 0.044715


def _gelu(x):
    return 0.5 * x * (1.0 + jnp.tanh(_GELU_C * (x + _GELU_A * x * x * x)))


def _gelu_and_grad(x):
    th = jnp.tanh(_GELU_C * (x + _GELU_A * x * x * x))
    grad = 0.5 * (1.0 + th) + 0.5 * x * (1.0 - th * th) * (_GELU_C * (1.0 + 3.0 * _GELU_A * x * x))
    return 0.5 * x * (1.0 + th), grad


def _causal(n):
    return lax.broadcasted_iota(jnp.int32, (n, n), 1) <= lax.broadcasted_iota(jnp.int32, (n, n), 0)


def _sgu_fwd(name, p, v_gain, w_s, b_st, carries=()):
    t, da2 = p.shape
    da = da2 // 2
    ng = da // GROUP

    def body(p_ref, vg_ref, ws_ref, bst_ref, o_ref):
        keep = _causal(GROUP)
        for g in range(ng):
            lo = g * GROUP
            u = _gelu(p_ref[:, lo:lo + GROUP])
            vn = _rms(_gelu(p_ref[:, da + lo:da + lo + GROUP]), vg_ref[:, lo:lo + GROUP])
            w = jnp.where(keep, ws_ref[g], 0.0).astype(BF)
            mixed = jnp.dot(w, vn.astype(BF), preferred_element_type=F32) + bst_ref[:, g:g + 1]
            o_ref[:, lo:lo + GROUP] = (u * mixed).astype(BF)

    return _call(
        body, name=name, grid=(t // GROUP,),
        in_specs=[pl.BlockSpec((GROUP, da2), lambda i: (i, 0)), pl.BlockSpec((1, da), lambda i: (0, 0)),
                  pl.BlockSpec((ng, GROUP, GROUP), lambda i: (0, 0, 0)), pl.BlockSpec((GROUP, ng), lambda i: (0, 0))],
        out_specs=[pl.BlockSpec((GROUP, da), lambda i: (i, 0))], out_shape=[jax.ShapeDtypeStruct((t, da), BF)],
        operands=(p, v_gain, w_s, b_st), sem=("parallel",), carries=carries)


def _sgu_bwd(name, p, v_gain, w_s, w_st, b_st, dout, carries=()):
    t, da2 = p.shape
    da = da2 // 2
    ng = da // GROUP

    def body(p_ref, vg_ref, ws_ref, wst_ref, bst_ref, do_ref, dp_ref, dvg_ref, dws_ref, dbst_ref):
        @pl.when(pl.program_id(0) == 0)
        def _():
            dvg_ref[...] = jnp.zeros_like(dvg_ref)
            dws_ref[...] = jnp.zeros_like(dws_ref)
            dbst_ref[...] = jnp.zeros_like(dbst_ref)

        keep = _causal(GROUP)
        keep_t = lax.broadcasted_iota(jnp.int32, (GROUP, GROUP), 0) <= lax.broadcasted_iota(jnp.int32, (GROUP, GROUP), 1)
        for g in range(ng):
            lo = g * GROUP
            u, du = _gelu_and_grad(p_ref[:, lo:lo + GROUP])
            v, dv_act = _gelu_and_grad(p_ref[:, da + lo:da + lo + GROUP])
            gain = vg_ref[:, lo:lo + GROUP]
            r = lax.rsqrt(jnp.mean(v * v, axis=-1, keepdims=True) + NORM_EPS)
            vh = v * r
            vnb = (vh * gain).astype(BF)
            w = jnp.where(keep, ws_ref[g], 0.0).astype(BF)
            wt = jnp.where(keep_t, wst_ref[g], 0.0).astype(BF)
            mixed = jnp.dot(w, vnb, preferred_element_type=F32) + bst_ref[:, g:g + 1]
            dout_g = do_ref[:, lo:lo + GROUP].astype(F32)
            dmixed = dout_g * u
            dmb = dmixed.astype(BF)
            dbst_ref[:, g:g + 1] += jnp.sum(dmixed, axis=1, keepdims=True)
            dws_ref[g] += jnp.where(keep, lax.dot_general(dmb, vnb, _NT, preferred_element_type=F32), 0.0)
            dvn = jnp.dot(wt, dmb, preferred_element_type=F32)
            dvg_ref[:, lo:lo + GROUP] += jnp.sum(dvn * vh, axis=0, keepdims=True)
            dvh = dvn * gain
            dv = r * dvh - vh * (jnp.sum(vh * dvh, axis=-1, keepdims=True) * (r / GROUP))
            dp_ref[:, lo:lo + GROUP] = (dout_g * mixed * du).astype(BF)
            dp_ref[:, da + lo:da + lo + GROUP] = (dv * dv_act).astype(BF)

    full = lambda *shape: pl.BlockSpec(shape, lambda i: (0,) * len(shape))
    outs = _call(
        body, name=name, grid=(t // GROUP,),
        in_specs=[pl.BlockSpec((GROUP, da2), lambda i: (i, 0)), full(1, da), full(ng, GROUP, GROUP), full(ng, GROUP, GROUP),
                  full(GROUP, ng), pl.BlockSpec((GROUP, da), lambda i: (i, 0))],
        out_specs=[pl.BlockSpec((GROUP, da2), lambda i: (i, 0)), full(1, da), full(ng, GROUP, GROUP), full(GROUP, ng)],
        out_shape=[jax.ShapeDtypeStruct((t, da2), BF), jax.ShapeDtypeStruct((1, da), F32),
                   jax.ShapeDtypeStruct((ng, GROUP, GROUP), F32), jax.ShapeDtypeStruct((GROUP, ng), F32)],
        operands=(p, v_gain, w_s, w_st, b_st, dout), sem=("arbitrary",), carries=carries)
    return outs[:4], outs[4:]


_CONV_ROWS = 256
_CONV_COLS = 128


def _shift_down(x, prev, k):
    top = pltpu.roll(jnp.concatenate([prev, x[:8]], axis=0), k, 0)[8:16]
    return jnp.concatenate([top, pltpu.roll(x, k, 0)[8:]], axis=0)


def _conv_taps(a_ref, half, r0, first):
    rows = _rows(a_ref)
    x = a_ref[half, pl.ds(r0, rows), :]
    if first:
        prev = jnp.zeros((8, x.shape[1]), F32)
        return x, _shift_down(x, prev, 1), _shift_down(x, prev, 2)
    return x, a_ref[half, pl.ds(r0 - 1, rows), :], a_ref[half, pl.ds(r0 - 2, rows), :]


def _rows(a_ref):
    return min(_CONV_ROWS, a_ref.shape[1])


def _conv_fwd(name, a3, cw, cb, carries=()):
    _, t, f = a3.shape
    tc, rows = _CONV_COLS, min(_CONV_ROWS, t)

    def body(a_ref, cw_ref, cb_ref, y_ref):
        def chunk(r0, first):
            c = []
            for half in range(2):
                x, x1, x2 = _conv_taps(a_ref, half, r0, first)
                w = cw_ref[half]
                c.append(cb_ref[half] + (w[0:1] * x2 + w[1:2] * x1 + w[2:3] * x))
            y_ref[pl.ds(r0, rows), :] = (c[0] * jax.nn.sigmoid(c[0]) * c[1]).astype(BF)

        chunk(0, True)

        @pl.loop(1, t // rows)
        def _(r):
            chunk(pl.multiple_of(r * rows, rows), False)

    col = lambda *lead: pl.BlockSpec((*lead, tc), lambda j: (0,) * len(lead) + (j,))
    y, *dsts = _call(
        body, name=name, grid=(f // tc,), in_specs=[col(2, t), col(2, CONV_TAPS), col(2, 1)],
        out_specs=[col(t)], out_shape=[jax.ShapeDtypeStruct((t, f), BF)], operands=(a3, cw, cb),
        sem=("parallel",), vmem=40 * _MIB, carries=carries)
    return y, dsts


def _conv_bwd(name, a3, cw, cb, dy, carries=()):
    _, t, f = a3.shape
    tc, rows = _CONV_COLS, min(_CONV_ROWS, t)
    n_steps = t // rows

    def body(a_ref, cw_ref, cb_ref, dy_ref, da_ref, dcw_ref, dcb_ref, dc_ref):
        dcw_ref[...] = jnp.zeros_like(dcw_ref)
        dcb_ref[...] = jnp.zeros_like(dcb_ref)

        def chunk(r0, first, nxt):
            taps, c = [], []
            for half in range(2):
                x, x1, x2 = _conv_taps(a_ref, half, r0, first)
                w = cw_ref[half]
                taps.append((x2, x1, x))
                c.append(cb_ref[half] + (w[0:1] * x2 + w[1:2] * x1 + w[2:3] * x))
            gate, val = c
            sg = jax.nn.sigmoid(gate)
            dyv = dy_ref[pl.ds(r0, rows), :].astype(F32)
            dcs = (dyv * val * (sg * (1.0 + gate * (1.0 - sg))), dyv * (gate * sg))
            new_nxt = []
            for half in range(2):
                dc, w = dcs[half], cw_ref[half]
                dcb_ref[half] += jnp.sum(dc, axis=0, keepdims=True)
                for tap in range(CONV_TAPS):
                    dcw_ref[half, tap:tap + 1, :] += jnp.sum(dc * taps[half][tap], axis=0, keepdims=True)
                dc_ref[half, 0:rows, :] = dc
                dc_ref[half, rows:rows + 8, :] = nxt[half]
                da = w[2:3] * dc + w[1:2] * dc_ref[half, 1:rows + 1, :] + w[0:1] * dc_ref[half, 2:rows + 2, :]
                da_ref[half, pl.ds(r0, rows), :] = da.astype(BF)
                new_nxt.append(dc[:8])
            return tuple(new_nxt)

        zeros = jnp.zeros((8, tc), F32)
        nxt = lax.fori_loop(0, n_steps - 1, lambda s, nxt: chunk(pl.multiple_of((n_steps - 1 - s) * rows, rows), False, nxt),
                            (zeros, zeros))
        chunk(0, True, nxt)

    col = lambda *lead: pl.BlockSpec((*lead, tc), lambda j: (0,) * len(lead) + (j,))
    outs = _call(
        body, name=name, grid=(f // tc,), in_specs=[col(2, t), col(2, CONV_TAPS), col(2, 1), col(t)],
        out_specs=[col(2, t), col(2, CONV_TAPS), col(2, 1)],
        out_shape=[jax.ShapeDtypeStruct((2, t, f), BF), jax.ShapeDtypeStruct((2, CONV_TAPS, f), F32),
                   jax.ShapeDtypeStruct((2, 1, f), F32)],
        operands=(a3, cw, cb, dy), scratch=[pltpu.VMEM((2, rows + 8, tc), F32)], sem=("parallel",), vmem=48 * _MIB,
        carries=carries)
    return outs[:3], outs[3:]


_ATT_BLOCK = 256


def _split_dot(x, tri):
    hi = x.astype(BF)
    lo = (x - hi.astype(F32)).astype(BF)
    return jnp.dot(hi, tri, preferred_element_type=F32) + jnp.dot(lo, tri, preferred_element_type=F32)


_ATT_HEADS_FWD = 8
_ATT_HEADS_BWD = 4


def _logits(qb, kb, diagonal):
    z = lax.dot_general(qb, kb, _NT, preferred_element_type=F32) * (1.0 / math.sqrt(GROUP))
    lb = jnp.minimum(z, 0.0) - jnp.log(1.0 + jnp.exp(-jnp.abs(z)))
    if not diagonal:
        return lb, lb - z, None
    mask = lax.broadcasted_iota(jnp.int32, z.shape, 1) < lax.broadcasted_iota(jnp.int32, z.shape, 0)
    return lb, jnp.where(mask, lb - z, 0.0), mask


def _head(ref, g, rows=slice(None)):
    return ref[rows, g * GROUP:(g + 1) * GROUP]


def _attn_fwd(name, q, k, v, carries=()):
    t, hd = q.shape
    blk = min(_ATT_BLOCK, t)
    heads = min(_ATT_HEADS_FWD, hd // GROUP)
    gs = range(heads)

    def body(q_ref, k_ref, v_ref, o_ref, rest_ref, first_ref):
        hg, i = pl.program_id(0), pl.program_id(1)
        ri = lax.broadcasted_iota(jnp.int32, (blk, blk), 0)
        ci = lax.broadcasted_iota(jnp.int32, (blk, blk), 1)
        tri = (ri >= ci).astype(BF)

        def tile(j, state, diagonal):
            keys = pl.ds(pl.multiple_of(j * blk, blk), blk)
            logit = [_logits(_head(q_ref, g), _head(k_ref, g, keys), diagonal) for g in gs]
            incl = [_split_dot(logit[g][1], tri) for g in gs]
            a = [jnp.exp(logit[g][0] + (incl[g] - logit[g][1] + state[g][0])) for g in gs]
            if diagonal:
                a = [jnp.where(logit[g][2], a[g], 0.0) for g in gs]
            out = [state[g][1] + jnp.dot(a[g].astype(BF), _head(v_ref, g, keys), preferred_element_type=F32) for g in gs]
            return tuple((state[g][0] + incl[g][:, 0:1], out[g]) for g in gs)

        state = tile(i, ((jnp.zeros((blk, 1), F32), jnp.zeros((blk, GROUP), F32)),) * heads, True)

        def more(carry):
            j, state = carry
            live = functools.reduce(jnp.maximum, [jnp.max(right) for right, _ in state])
            return jnp.logical_and(j >= 0, live > EXP_FLOOR)

        j, state = lax.while_loop(more, lambda c: (c[0] - 1, tile(c[0], c[1], False)), (i - 1, state))
        for g, (right, acc) in enumerate(state):
            o_ref[:, g * GROUP:(g + 1) * GROUP] = acc.astype(BF)
            rest_ref[:, g * GROUP:(g + 1) * GROUP] = jnp.broadcast_to(right, (blk, GROUP))
        first_ref[hg, i] = (j + 1).astype(F32)

    qspec = pl.BlockSpec((blk, heads * GROUP), lambda h, i: (i, h))
    kvspec = pl.BlockSpec((t, heads * GROUP), lambda h, i: (0, h), pipeline_mode=pl.Buffered(1))
    groups = hd // (heads * GROUP)
    outs = _call(
        body, name=name, grid=(groups, t // blk), in_specs=[qspec, kvspec, kvspec],
        out_specs=[qspec, qspec, pl.BlockSpec(memory_space=pltpu.SMEM)],
        out_shape=[jax.ShapeDtypeStruct((t, hd), BF), jax.ShapeDtypeStruct((t, hd), F32),
                   jax.ShapeDtypeStruct((groups, t // blk), F32)],
        operands=(q, k, v), sem=("arbitrary", "arbitrary"), vmem=40 * _MIB, carries=carries, middle_at=0.75)
    return outs[:3], outs[3:]


def _attn_bwd(name, q, k, v, rest, first, do, carries=()):
    t, hd = q.shape
    blk = min(_ATT_BLOCK, t)
    nq = t // blk
    scale = 1.0 / math.sqrt(GROUP)
    heads = min(_ATT_HEADS_BWD, hd // GROUP)
    per_first = min(_ATT_HEADS_FWD, hd // GROUP) // heads
    gs = range(heads)

    def body(first_ref, q_ref, k_ref, v_ref, rest_ref, do_ref, dq_ref, dk_ref, dv_ref, dk_acc, dv_acc):
        hg, i = pl.program_id(0), pl.program_id(1)

        @pl.when(i == 0)
        def _():
            dk_acc[...] = jnp.zeros_like(dk_acc)
            dv_acc[...] = jnp.zeros_like(dv_acc)

        ri = lax.broadcasted_iota(jnp.int32, (blk, blk), 0)
        ci = lax.broadcasted_iota(jnp.int32, (blk, blk), 1)
        tri = (ri <= ci).astype(BF)

        def tile(j, state, diagonal):
            keys = pl.ds(pl.multiple_of(j * blk, blk), blk)
            qs, dos = [_head(q_ref, g) for g in gs], [_head(do_ref, g) for g in gs]
            kb, vb = [_head(k_ref, g, keys) for g in gs], [_head(v_ref, g, keys) for g in gs]
            logit = [_logits(qs[g], kb[g], diagonal) for g in gs]
            pre = [_split_dot(logit[g][1], tri) for g in gs]
            a = [jnp.exp(logit[g][0] + (rest_ref[:, g * GROUP:g * GROUP + 1] - state[g][0] - pre[g])) for g in gs]
            if diagonal:
                a = [jnp.where(logit[g][2], a[g], 0.0) for g in gs]
            gw = [a[g] * lax.dot_general(dos[g], vb[g], _NT, preferred_element_type=F32) for g in gs]
            gpre = [_split_dot(gw[g], tri) for g in gs]
            dz = []
            for g in gs:
                beta = jnp.exp(logit[g][0])
                d = (gw[g] * (1.0 - beta) - (state[g][1] + gpre[g] - gw[g]) * beta) * scale
                dz.append((jnp.where(logit[g][2], d, 0.0) if diagonal else d).astype(BF))
            for g in gs:
                dk_acc[keys, g * GROUP:(g + 1) * GROUP] += lax.dot_general(dz[g], qs[g], _TN, preferred_element_type=F32)
                dv_acc[keys, g * GROUP:(g + 1) * GROUP] += lax.dot_general(a[g].astype(BF), dos[g], _TN,
                                                                           preferred_element_type=F32)
            return tuple((state[g][0] + pre[g][:, blk - 1:blk], state[g][1] + gpre[g][:, blk - 1:blk],
                          state[g][2] + jnp.dot(dz[g], kb[g], preferred_element_type=F32)) for g in gs)

        zero = jnp.zeros((blk, 1), F32)
        first_block = jnp.clip(first_ref[hg // per_first, i].astype(jnp.int32), 0, i)
        state = lax.fori_loop(first_block, i, lambda j, c: tile(j, c, False),
                              ((zero, zero, jnp.zeros((blk, GROUP), F32)),) * heads)
        for g, (_, _, dq) in enumerate(tile(i, state, True)):
            dq_ref[:, g * GROUP:(g + 1) * GROUP] = dq.astype(BF)

        @pl.when(i == nq - 1)
        def _():
            dk_ref[...] = dk_acc[...].astype(BF)
            dv_ref[...] = dv_acc[...].astype(BF)

    qspec = pl.BlockSpec((blk, heads * GROUP), lambda h, i: (i, h))
    kvspec = pl.BlockSpec((t, heads * GROUP), lambda h, i: (0, h), pipeline_mode=pl.Buffered(1))
    outs = _call(
        body, name=name, grid=(hd // (heads * GROUP), nq),
        in_specs=[pl.BlockSpec(memory_space=pltpu.SMEM), qspec, kvspec, kvspec, qspec, qspec],
        out_specs=[qspec, kvspec, kvspec], out_shape=[jax.ShapeDtypeStruct((t, hd), BF)] * 3,
        operands=(first, q, k, v, rest, do),
        scratch=[pltpu.VMEM((t, heads * GROUP), F32), pltpu.VMEM((t, heads * GROUP), F32)],
        sem=("arbitrary", "arbitrary"), vmem=48 * _MIB, carries=carries)
    return outs[:3], outs[3:]


def _adamw_math(w, g, m, v):
    m = ADAM_B1 * m + (1.0 - ADAM_B1) * g
    v = ADAM_B2 * v + (1.0 - ADAM_B2) * (g * g)
    m_hat = m / (1.0 - ADAM_B1 ** ADAM_STEP)
    v_hat = v / (1.0 - ADAM_B2 ** ADAM_STEP)
    return -ADAM_LR * (m_hat / (jnp.sqrt(v_hat) + ADAM_EPS) + ADAM_WD * w), m, v


def _sum_adamw(name, parts, w, m, v):
    layers, r, c = w.shape
    budget = 512 * 1024 // layers
    tr = r if r * c <= budget else _tile_rows(r, max(8, (budget // c) // 8 * 8))
    n = r // tr

    def body(*refs):
        p_refs, (w_ref, m_ref, v_ref, g_ref, d_ref, nm_ref, nv_ref) = refs[:layers], refs[layers:]
        for layer, p_ref in enumerate(p_refs):
            @pl.when(pl.program_id(0) == layer)
            def _(p_ref=p_ref):
                g = p_ref[0].astype(F32)
                for dev in range(1, N_DEV):
                    g = g + p_ref[dev].astype(F32)
                g_ref[...] = g
                d_ref[...], nm_ref[...], nv_ref[...] = _adamw_math(w_ref[...], g, m_ref[...], v_ref[...])

    def part_spec(layer):
        return pl.BlockSpec((N_DEV, tr, c), lambda l, i: (0, jnp.where(l < layer, 0, jnp.where(l == layer, i, n - 1)), 0))

    row = pl.BlockSpec((None, tr, c), lambda l, i: (l, i, 0))
    return pl.pallas_call(
        body, name=name, grid=(layers, n), in_specs=[part_spec(layer) for layer in range(layers)] + [row] * 3,
        out_specs=[row] * 4, out_shape=[jax.ShapeDtypeStruct((layers, r, c), F32)] * 4,
        compiler_params=_params(("arbitrary", "arbitrary"), 40 * _MIB),
    )(*parts, w, m, v)


def _tile_rows(r, pref):
    t = min(r, pref)
    while r % t or t % 8:
        t -= 1
    return t


def _exchange(name, carries):
    return _call(lambda: None, name=name, grid=(1,), in_specs=[], out_specs=[], out_shape=[], operands=(), carries=carries)


def _all_reduce_small(name, groups):
    c = groups[0][0].shape[1]
    parts, offsets, starts, r = [], [], [], 0
    for group in groups:
        starts.append(r)
        for p in group:
            parts.append(p)
            offsets.append(r)
            r += p.shape[0]
        r = -(-r // 8) * 8
    n = len(parts)

    def body(*refs):
        out_ref, slots, send_sems, recv_sems = refs[n:]
        x, y, c_, me = _place()
        slots[me] = jnp.zeros((r, c), F32)
        for p_ref, off in zip(refs[:n], offsets):
            slots[me, off:off + p_ref.shape[0], :] = p_ref[...]

        here, sibling = (x, y, c_), (x, y, 1 - c_)
        chips = [(1 - x, y), (x, 1 - y), (1 - x, 1 - y)]

        def copy(sem, block, to):
            slot = slots.at[4 * block[0] + 2 * block[1] + block[2]]
            return pltpu.make_async_remote_copy(
                src_ref=slot, dst_ref=slot, send_sem=send_sems.at[sem], recv_sem=recv_sems.at[sem], device_id=to,
                device_id_type=MESH)

        sent = [copy(0, here, sibling)] + [copy(1 + j, here, (*chip, c_)) for j, chip in enumerate(chips)]
        for cp in sent:
            cp.start()
        for j, chip in enumerate(chips):
            copy(1 + j, (*chip, c_), here).wait_recv()
            sent.append(copy(4 + j, (*chip, c_), sibling))
            sent[-1].start()
        copy(0, sibling, here).wait_recv()
        for j, chip in enumerate(chips):
            copy(4 + j, (*chip, 1 - c_), here).wait_recv()
        for cp in sent:
            cp.wait_send()
        total = slots[0]
        for dev in range(1, N_DEV):
            total = total + slots[dev]
        out_ref[...] = total

    vm = pl.BlockSpec(memory_space=pltpu.VMEM)
    summed = pl.pallas_call(
        body, name=name, in_specs=[vm] * n, out_specs=vm, out_shape=jax.ShapeDtypeStruct((r, c), F32),
        scratch_shapes=[pltpu.VMEM((N_DEV, r, c), F32), pltpu.SemaphoreType.DMA((7,)), pltpu.SemaphoreType.DMA((7,))],
        compiler_params=_params(None, (N_DEV + 6) * r * c * 4 + 8 * _MIB),
    )(*parts)
    return summed, starts


def _adamw(name, g, w, m, v):
    cols = w.shape[-1]
    flat = lambda a: a.reshape(-1, cols)

    def body(g_ref, w_ref, m_ref, v_ref, d_ref, nm_ref, nv_ref):
        d_ref[...], nm_ref[...], nv_ref[...] = _adamw_math(w_ref[...], g_ref[...], m_ref[...], v_ref[...])

    outs = pl.pallas_call(body, name=name, out_shape=[jax.ShapeDtypeStruct(flat(w).shape, F32)] * 3)(
        flat(g), flat(w), flat(m), flat(v))
    return [o.reshape(w.shape) for o in outs]


def kernel(x, pre_mix_g, post_mix_g, pre_ffn_g, post_ffn_g, a_w_in, a_v_norm_g, a_w_spatial, a_b_spatial, a_w_out, kv_norm_g, w_k, w_v, b_w_q, b_w_o, ffn_w_up, ffn_conv_w, ffn_conv_b, ffn_w_down, loss_target, m_pre_mix_g, m_post_mix_g, m_pre_ffn_g, m_post_ffn_g, m_a_w_in, m_a_v_norm_g, m_a_w_spatial, m_a_b_spatial, m_a_w_out, m_kv_norm_g, m_w_k, m_w_v, m_b_w_q, m_b_w_o, m_ffn_w_up, m_ffn_conv_w, m_ffn_conv_b, m_ffn_w_down, v_pre_mix_g, v_post_mix_g, v_pre_ffn_g, v_post_ffn_g, v_a_w_in, v_a_v_norm_g, v_a_w_spatial, v_a_b_spatial, v_a_w_out, v_kv_norm_g, v_w_k, v_w_v, v_b_w_q, v_b_w_o, v_ffn_w_up, v_ffn_conv_w, v_ffn_conv_b, v_ffn_w_down):
    t, d = x.shape[1], x.shape[2]
    f = ffn_w_down.shape[1] * N_DEV
    ng = d // GROUP
    me = 4 * lax.axis_index("x") + 2 * lax.axis_index("y") + lax.axis_index("c")
    x2, target = x.reshape(t, d), loss_target.reshape(t, d)

    hn0, g_in, g_cw, g_vg = _norms_fwd("pre_mix0", x2, [pre_mix_g[0]], carries=[
        _gather(a_w_in[0].astype(BF)), _gather(ffn_conv_w.reshape(2 * CONV_TAPS, -1)), _gather(a_v_norm_g)])
    up0 = ffn_w_up[0].astype(BF)
    cw_full = jnp.transpose(g_cw.reshape(N_DEV, 2, CONV_TAPS, -1), (1, 2, 0, 3)).reshape(2, CONV_TAPS, 2, f)
    cw_l = [jnp.transpose(cw_full[l], (1, 0, 2)) for l in range(2)]
    cb_l = [ffn_conv_b[l].reshape(2, 1, f) for l in range(2)]
    vg_full = g_vg.reshape(1, d)
    w_s = a_w_spatial[0]
    w_st = jnp.swapaxes(w_s, 1, 2)
    b_st = a_b_spatial[0].T

    p0, (g_out, g_up0) = _mm_nn_blk("sgu_in", hn0, g_in, F32, carries=[
        _gather(a_w_out[0].astype(BF)), _gather(up0, 0, d // 4)])
    w_out_f = g_out.reshape(d, d)
    sg, g_up0 = _sgu_fwd("sgu", p0, vg_full, w_s, b_st, carries=[_gather(up0, d // 4, d // 2, dst=g_up0)])
    mix0, (g_up0,) = _mm_nn("sgu_out", sg, w_out_f, F32, carries=[_gather(up0, d // 2, 3 * d // 4, dst=g_up0)])
    h1, fn0, g_up0 = _resid_norms("post_mix0", x2, mix0, post_mix_g[0], [pre_ffn_g[0]],
                                  carries=[_gather(up0, 3 * d // 4, d, dst=g_up0)])
    a3_0, (g_dn0,) = _mm_nn_blk("ffn0_up", fn0, g_up0, F32, halves=True, carries=[_gather(ffn_w_down[0].astype(BF))])
    y0, (g_q,) = _conv_fwd("ffn0_conv", a3_0, cw_l[0], cb_l[0], carries=[_gather(b_w_q[0].astype(BF))])
    wv = w_v.astype(BF)
    f0, (g_k, g_v) = _mm_nn("ffn0_down", y0, g_dn0.reshape(f, d), F32,
                            carries=[_gather(w_k.astype(BF)), _gather(wv, 0, d // 16)])
    h2, hn1, kvn, g_v = _resid_norms("post_ffn0", h1, f0, post_ffn_g[0], [pre_mix_g[1], kv_norm_g],
                                     carries=[_gather(wv, d // 16, d // 8, dst=g_v)])
    w_q_f, w_k_f, w_v_f = g_q.reshape(d, d), g_k.reshape(d, d), g_v.reshape(d, d)
    up1 = ffn_w_up[1].astype(BF)
    e = d // 8
    q, (g_up1,) = _mm_nn("attn_q", hn1, w_q_f, BF, carries=[_gather(up1, 0, e)])
    kk, (g_up1,) = _mm_nn("attn_k", kvn, w_k_f, BF, carries=[_gather(up1, e, 2 * e, dst=g_up1)])
    vv, (g_up1,) = _mm_nn("attn_v", kvn, w_v_f, BF, carries=[_gather(up1, 2 * e, 3 * e, dst=g_up1)])
    (att, rest, first), (g_o, g_up1) = _attn_fwd(
        "attn", q, kk, vv, carries=[_gather(b_w_o[0].astype(BF)), _gather(up1, 3 * e, 6 * e, dst=g_up1)])
    w_o_f = g_o.reshape(d, d)
    mix1, (g_up1,) = _mm_nn("attn_o", att, w_o_f, F32, carries=[_gather(up1, 6 * e, 7 * e, dst=g_up1)])
    h3, fn1, g_up1 = _resid_norms("post_mix1", h2, mix1, post_mix_g[1], [pre_ffn_g[1]],
                                  carries=[_gather(up1, 7 * e, d, dst=g_up1)])
    a3_1, (g_dn1,) = _mm_nn_blk("ffn1_up", fn1, g_up1, F32, halves=True, carries=[_gather(ffn_w_down[1].astype(BF))])
    y1, _ = _conv_fwd("ffn1_conv", a3_1, cw_l[1], cb_l[1])
    f1 = _mm_nn("ffn1_down", y1, g_dn1.reshape(f, d), F32)
    g_up = (g_up0, g_up1)
    w_dn_f = (g_dn0.reshape(f, d), g_dn1.reshape(f, d))
    dh, loss_part, df1, d_post_ffn1 = _resid_loss("loss", h3, f1, post_ffn_g[1], target)
    loss = lax.psum(loss_part[0, 0], ("x", "y", "c"))

    def blocks(dw):
        return dw.reshape(N_DEV, -1, d)

    def split(result, carries):
        return result if carries else (result, [])

    def ffn_bwd(l, dh_out, dfo, h_in, fn, a3, yv, mix, with_dw=(), with_dx=(), with_up=()):
        dw_dn, sent_dw = split(_mm_tn(f"ffn{l}_down_dw", yv, dfo, carries=with_dw), with_dw)
        dy, sent_dx = split(_mm_nt(f"ffn{l}_down_dx", dfo, w_dn_f[l], BF, carries=with_dx), with_dx)
        dw_dn = blocks(dw_dn)
        cut = dw_dn.shape[1] * 5 // 8 // 16 * 16
        (da3, dcw, dcb), (p_dn,) = _conv_bwd(f"ffn{l}_conv_bwd", a3, cw_l[l], cb_l[l], dy, carries=[_scatter(dw_dn, 0, cut)])
        dw_up, (p_dn, *sent_up) = _mm_tn_blk(f"ffn{l}_up_dw", fn, da3, halves=True,
                                             carries=[_scatter(dw_dn, cut, dst=p_dn), *with_up])
        dfn, (p_up,) = _mm_nt_blk(f"ffn{l}_up_dx", da3, g_up[l], F32, halves=True, carries=[_scatter(dw_up, 0, d // 2)])
        dh_in, d_pre, dmix, d_post_mix, p_up = _norm_bwd(
            f"pre_ffn{l}_bwd", h_in, pre_ffn_g[l], dfn, res=dh_out, then=(mix, post_mix_g[l]),
            carries=[_scatter(dw_up, d // 2, 10 * d // 16, dst=p_up)])
        return dh_in, dmix, d_pre, d_post_mix, dcw, dcb, p_dn, dw_up, p_up, list(sent_dw) + list(sent_dx) + list(sent_up)

    dh3, dmix1, d_pre_ffn1, d_post_mix1, dcw1, dcb1, p_dn1, dw_up1, p_up1, _ = ffn_bwd(
        1, dh, df1, h3, fn1, a3_1, y1, mix1)
    dw_o = _mm_tn("attn_o_dw", att, dmix1)
    datt = _mm_nt("attn_o_dx", dmix1, w_o_f, BF)
    (dq, dk, dv), (p_up1,) = _attn_bwd(
        "attn_bwd", q, kk, vv, rest, first, datt, carries=[_scatter(dw_up1, 10 * d // 16, d, dst=p_up1)])
    dw_o, qr = blocks(dw_o), d // 32
    dw_q, (p_o,) = _mm_tn("attn_q_dw", hn1, dq, carries=[_scatter(dw_o, 0, qr)])
    dw_k, (p_o,) = _mm_tn("attn_k_dw", kvn, dk, carries=[_scatter(dw_o, qr, 2 * qr, dst=p_o)])
    dw_v, (p_o,) = _mm_tn("attn_v_dw", kvn, dv, carries=[_scatter(dw_o, 2 * qr, 3 * qr, dst=p_o)])
    dhn1, (p_o,) = _mm_nt("attn_q_dx", dq, w_q_f, F32, carries=[_scatter(dw_o, 3 * qr, 4 * qr, dst=p_o)])
    dkvn = _mm_nt("attn_v_dx", dv, w_v_f, F32, add=_mm_nt("attn_k_dx", dk, w_k_f, F32))
    dh2, d_pre_mix1, d_kv, df0, d_post_ffn0 = _norm_bwd(
        "pre_mix1_kv_bwd", h2, [pre_mix_g[1], kv_norm_g], [dhn1, dkvn], res=dh3, then=(f0, post_ffn_g[0]))
    dh1, dmix0, d_pre_ffn0, d_post_mix0, dcw0, dcb0, p_dn0, dw_up0, p_up0, (p_q, p_k, p_v) = ffn_bwd(
        0, dh2, df0, h1, fn0, a3_0, y0, mix0, with_dw=[_scatter(blocks(dw_q))], with_dx=[_scatter(blocks(dw_k))],
        with_up=[_scatter(blocks(dw_v))])
    dw_out, (p_up0,) = _mm_tn("sgu_out_dw", sg, dmix0, carries=[_scatter(dw_up0, 10 * d // 16, 11 * d // 16, dst=p_up0)])
    dsg, (p_up0,) = _mm_nt("sgu_out_dx", dmix0, w_out_f, BF, carries=[_scatter(dw_up0, 11 * d // 16, 12 * d // 16, dst=p_up0)])
    (dp0, d_vg, d_ws, d_bst), (p_up0,) = _sgu_bwd(
        "sgu_bwd", p0, vg_full, w_s, w_st, b_st, dsg, carries=[_scatter(dw_up0, 12 * d // 16, 15 * d // 16, dst=p_up0)])
    dw_in, (p_out,) = _mm_tn_blk("sgu_in_dw", hn0, dp0, carries=[_scatter(blocks(dw_out))])
    dhn0, (p_in,) = _mm_nt_blk("sgu_in_dx", dp0, g_in, F32, carries=[_scatter(dw_in, 0, 5 * d // 8)])
    grad_x, d_pre_mix0, p_in = _norm_bwd("pre_mix0_bwd", x2, pre_mix_g[0], dhn0, res=dh1,
                                         carries=[_scatter(dw_in, 5 * d // 8, 7 * d // 8, dst=p_in)])
    p_up0, p_in = _exchange("scatter_last", [_scatter(dw_up0, 15 * d // 16, d, dst=p_up0),
                                             _scatter(dw_in, 7 * d // 8, d, dst=p_in)])

    def conv_w_grad(dcw):
        return jnp.transpose(dcw, (1, 0, 2)).reshape(CONV_TAPS, 2 * f)

    small = [
        ([d_pre_mix0, d_pre_mix1], (2, d)), ([d_post_mix0, d_post_mix1], (2, d)),
        ([d_pre_ffn0, d_pre_ffn1], (2, d)), ([d_post_ffn0, d_post_ffn1], (2, d)),
        ([d_kv], (d,)), ([d_vg], (1, d)), ([d_bst.T], (1, ng, GROUP)), ([d_ws], (1, ng, GROUP, GROUP)),
        ([dcb0, dcb1], (2, 2 * f)), ([conv_w_grad(dcw0), conv_w_grad(dcw1)], (2, CONV_TAPS, 2 * f)),
    ]
    width = V7X_LANES * math.gcd(d // V7X_LANES, 2 * f // V7X_LANES)
    summed, offsets = _all_reduce_small("reduce_small", [[a.reshape(-1, width) for a in group] for group, _ in small])
    full = [summed[off:off + math.prod(shape) // width].reshape(shape) for off, (_, shape) in zip(offsets, small)]
    g_pre_mix, g_post_mix, g_pre_ffn, g_post_ffn, g_kv, g_vgain, g_bs, g_ws, g_cb, g_cwf = full
    cw_w = 2 * f // N_DEV
    g_vgain = lax.dynamic_slice_in_dim(g_vgain, me * (d // N_DEV), d // N_DEV, axis=1)
    g_cwf = lax.dynamic_slice_in_dim(g_cwf, me * cw_w, cw_w, axis=2)

    parts = [p_in, p_out, p_k, p_v, p_q, p_o, p_up0, p_up1, p_dn0, p_dn1]

    def small_update(name, g, w, m, v):
        return [g] + _adamw(name, g, w, m, v)

    def stacked(name, part0, part1, w, m, v):
        shape = (2, *part0.shape[1:])
        return [o.reshape(w.shape) for o in _sum_adamw(name, [part0, part1], w.reshape(shape), m.reshape(shape), v.reshape(shape))]

    def single(name, part, w, m, v):
        shape = (1, *part.shape[1:])
        return [o.reshape(w.shape) for o in _sum_adamw(name, [part], w.reshape(shape), m.reshape(shape), v.reshape(shape))]

    results = {
        "pre_mix_g": small_update("adam_pre_mix", g_pre_mix, pre_mix_g, m_pre_mix_g, v_pre_mix_g),
        "post_mix_g": small_update("adam_post_mix", g_post_mix, post_mix_g, m_post_mix_g, v_post_mix_g),
        "pre_ffn_g": small_update("adam_pre_ffn", g_pre_ffn, pre_ffn_g, m_pre_ffn_g, v_pre_ffn_g),
        "post_ffn_g": small_update("adam_post_ffn", g_post_ffn, post_ffn_g, m_post_ffn_g, v_post_ffn_g),
        "a_w_in": single("adam_a_w_in", parts[0], a_w_in, m_a_w_in, v_a_w_in),
        "a_v_norm_g": small_update("adam_a_v_norm", g_vgain, a_v_norm_g, m_a_v_norm_g, v_a_v_norm_g),
        "a_w_spatial": small_update("adam_a_w_spatial", g_ws, a_w_spatial, m_a_w_spatial, v_a_w_spatial),
        "a_b_spatial": small_update("adam_a_b_spatial", g_bs, a_b_spatial, m_a_b_spatial, v_a_b_spatial),
        "a_w_out": single("adam_a_w_out", parts[1], a_w_out, m_a_w_out, v_a_w_out),
        "kv_norm_g": small_update("adam_kv_norm", g_kv, kv_norm_g, m_kv_norm_g, v_kv_norm_g),
        "w_k": single("adam_w_k", parts[2], w_k, m_w_k, v_w_k),
        "w_v": single("adam_w_v", parts[3], w_v, m_w_v, v_w_v),
        "b_w_q": single("adam_b_w_q", parts[4], b_w_q, m_b_w_q, v_b_w_q),
        "b_w_o": single("adam_b_w_o", parts[5], b_w_o, m_b_w_o, v_b_w_o),
        "ffn_w_up": stacked("adam_ffn_w_up", parts[6], parts[7], ffn_w_up, m_ffn_w_up, v_ffn_w_up),
        "ffn_conv_w": small_update("adam_ffn_conv_w", g_cwf, ffn_conv_w, m_ffn_conv_w, v_ffn_conv_w),
        "ffn_conv_b": small_update("adam_ffn_conv_b", g_cb, ffn_conv_b, m_ffn_conv_b, v_ffn_conv_b),
        "ffn_w_down": stacked("adam_ffn_w_down", parts[8], parts[9], ffn_w_down, m_ffn_w_down, v_ffn_w_down),
    }
    order = ["pre_mix_g", "post_mix_g", "pre_ffn_g", "post_ffn_g", "a_w_in", "a_v_norm_g", "a_w_spatial", "a_b_spatial",
             "a_w_out", "kv_norm_g", "w_k", "w_v", "b_w_q", "b_w_o", "ffn_w_up", "ffn_conv_w", "ffn_conv_b", "ffn_w_down"]
    outs = [loss, grad_x.reshape(x.shape)]
    for idx in range(4):
        outs += [results[n][idx] for n in order]
    return tuple(outs)
```

```python
import functools
import math
from typing import NamedTuple, Optional

import jax
import jax.numpy as jnp
from jax import lax
from jax.experimental import pallas as pl
from jax.experimental.pallas import tpu as pltpu

F32 = jnp.float32
BF = jnp.bfloat16
MESH = pl.DeviceIdType.MESH

N_DEV = 8
NORM_EPS = 1e-6
GROUP = 128
CONV_TAPS = 3
ADAM_LR, ADAM_B1, ADAM_B2, ADAM_EPS, ADAM_WD, ADAM_STEP = 0.001, 0.9, 0.999, 1e-08, 0.01, 10
EXP_FLOOR = -104.0

V7X_LANES = 128
V7X_VMEM_BYTES = 64 * 1024 * 1024
_MIB = 1024 * 1024

_NN = (((1,), (0,)), ((), ()))
_NT = (((1,), (1,)), ((), ()))
_TN = (((0,), (0,)), ((), ()))


def _tile(n, pref):
    if n <= pref:
        return n
    t = (pref // V7X_LANES) * V7X_LANES
    while t > V7X_LANES and n % t:
        t -= V7X_LANES
    assert n % t == 0, (n, pref)
    return t


def _nbytes(shape, dtype):
    return math.prod(shape) * jnp.dtype(dtype).itemsize


def _params(sem=None, vmem=None):
    kw = {}
    if sem is not None:
        kw["dimension_semantics"] = sem
    if vmem is not None:
        kw["vmem_limit_bytes"] = int(min(max(vmem, 16 * _MIB), V7X_VMEM_BYTES - 8 * _MIB))
    return pltpu.CompilerParams(**kw)


def _place():
    x, y, c = lax.axis_index("x"), lax.axis_index("y"), lax.axis_index("c")
    return x, y, c, 4 * x + 2 * y + c


def _flip(x, y, c, k):
    return (1 - x if k & 4 else x, 1 - y if k & 2 else y, 1 - c if k & 1 else c)


class _Carry(NamedTuple):
    gather: bool
    src: jax.Array
    dst: Optional[jax.Array]
    lo: int
    hi: int


def _gather(src, lo=0, hi=None, dst=None):
    return _Carry(True, src, dst, lo, src.shape[0] if hi is None else hi)


def _scatter(src, lo=0, hi=None, dst=None):
    return _Carry(False, src, dst, lo, src.shape[1] if hi is None else hi)


def _carry_phases(carries, srcs, dsts, send_sems, recv_sems, local_sems):
    x, y, c, me = _place()
    here, sibling = (x, y, c), (x, y, 1 - c)
    chips = [(1 - x, y), (x, 1 - y), (1 - x, 1 - y)]

    def rows(u):
        return pl.ds(carries[u].lo, carries[u].hi - carries[u].lo)

    def block_copy(u, sem, block, to, from_src=False):
        slot = dsts[u].at[4 * block[0] + 2 * block[1] + block[2], rows(u)]
        return pltpu.make_async_remote_copy(
            src_ref=srcs[u].at[rows(u)] if from_src else slot, dst_ref=slot, send_sem=send_sems.at[u, sem],
            recv_sem=recv_sems.at[u, sem], device_id=to, device_id_type=MESH)

    def partial_copy(u, k):
        peer = _flip(x, y, c, k)
        return pltpu.make_async_remote_copy(
            src_ref=srcs[u].at[4 * peer[0] + 2 * peer[1] + peer[2], rows(u)], dst_ref=dsts[u].at[me, rows(u)],
            send_sem=send_sems.at[u, k - 1], recv_sem=recv_sems.at[u, k - 1], device_id=peer, device_id_type=MESH)

    def local_copy(u):
        src = srcs[u].at[rows(u)] if carries[u].gather else srcs[u].at[me, rows(u)]
        return pltpu.make_async_copy(src, dsts[u].at[me, rows(u)], local_sems.at[u])

    def first():
        for u, cr in enumerate(carries):
            local_copy(u).start()
            if cr.gather:
                block_copy(u, 0, here, sibling, from_src=True).start()
                for j, chip in enumerate(chips):
                    block_copy(u, 1 + j, here, (*chip, c), from_src=True).start()
            else:
                for k in range(1, N_DEV):
                    partial_copy(u, k).start()

    def middle():
        for u, cr in enumerate(carries):
            if cr.gather:
                for j, chip in enumerate(chips):
                    block_copy(u, 1 + j, (*chip, c), here).wait_recv()
                    block_copy(u, 4 + j, (*chip, c), sibling).start()

    def last():
        for u, cr in enumerate(carries):
            if cr.gather:
                block_copy(u, 0, sibling, here).wait_recv()
                for j, chip in enumerate(chips):
                    block_copy(u, 4 + j, (*chip, 1 - c), here).wait_recv()
                block_copy(u, 0, here, sibling, from_src=True).wait_send()
                for j, chip in enumerate(chips):
                    block_copy(u, 1 + j, here, (*chip, c), from_src=True).wait_send()
                    block_copy(u, 4 + j, (*chip, c), sibling).wait_send()
            else:
                for k in range(1, N_DEV):
                    partial_copy(u, k).wait()
            local_copy(u).wait()

    return first, middle, last


def _call(body, *, name, grid, in_specs, out_specs, out_shape, operands, scratch=(), sem=None, vmem=None,
          carries=(), middle_at=0.6):
    if not carries:
        return pl.pallas_call(
            body, name=name, grid=grid, in_specs=in_specs, out_specs=out_specs, out_shape=out_shape,
            scratch_shapes=list(scratch), compiler_params=_params(sem, vmem))(*operands)
    n_in, n_out, n_scr, nc = len(in_specs), len(out_specs), len(scratch), len(carries)
    given = [u for u, cr in enumerate(carries) if cr.dst is not None]
    steps = math.prod(grid)
    middle_step = min(steps - 1, int(steps * middle_at))

    def wrapped(*refs):
        ins, srcs = refs[:n_in], refs[n_in:n_in + nc]
        at = n_in + nc + len(given)
        outs, dsts = refs[at:at + n_out], refs[at + n_out:at + n_out + nc]
        at += n_out + nc
        scr, (send_sems, recv_sems, local_sems) = refs[at:at + n_scr], refs[at + n_scr:]
        first, middle, last = _carry_phases(carries, srcs, dsts, send_sems, recv_sems, local_sems)
        step = 0
        for axis, size in enumerate(grid):
            step = step * size + pl.program_id(axis)
        pl.when(step == 0)(first)
        body(*ins, *outs, *scr)
        pl.when(step == middle_step)(middle)
        pl.when(step == steps - 1)(last)

    any_spec = pl.BlockSpec(memory_space=pl.ANY)
    dst_shapes = [jax.ShapeDtypeStruct((N_DEV, *cr.src.shape) if cr.gather else cr.src.shape, cr.src.dtype) for cr in carries]
    return pl.pallas_call(
        wrapped, name=name, grid=grid, in_specs=list(in_specs) + [any_spec] * (nc + len(given)),
        out_specs=list(out_specs) + [any_spec] * nc, out_shape=list(out_shape) + dst_shapes,
        input_output_aliases={n_in + nc + g: n_out + u for g, u in enumerate(given)},
        scratch_shapes=list(scratch) + [pltpu.SemaphoreType.DMA((nc, 7)), pltpu.SemaphoreType.DMA((nc, 7)),
                                        pltpu.SemaphoreType.DMA((nc,))],
        compiler_params=_params(("arbitrary",) * len(grid), vmem),
    )(*operands, *[cr.src for cr in carries], *[carries[u].dst for u in given])


def _mm(name, a, b, *, dims, grid, a_blk, a_map, b_blk, b_map, o_blk, o_map, out_shape, out_dtype,
        add=None, add_blk=None, add_map=None, carries=(), b_slabs=1):
    nk = grid[2]
    assert add is None or nk == 1
    acc_shape = tuple(d for d in o_blk if d is not None)
    in_place = out_dtype == F32

    def body(*refs):
        if add is None:
            a_ref, b_ref, o_ref = refs[:3]
            c_ref, scr = None, refs[3:]
        else:
            a_ref, b_ref, c_ref, o_ref = refs[:4]
            scr = refs[4:]
        if b_slabs == 1:
            part = lax.dot_general(a_ref[...], b_ref[...], dims, preferred_element_type=F32)
        else:
            cw = b_ref.shape[2]
            part = sum(lax.dot_general(a_ref[:, s * cw:(s + 1) * cw], b_ref[s], dims, preferred_element_type=F32)
                       for s in range(b_slabs))
        if c_ref is not None:
            part = part + c_ref[...].astype(F32)
        if nk == 1:
            o_ref[...] = part.astype(o_ref.dtype)
            return
        acc = o_ref if in_place else scr[0]
        k = pl.program_id(2)

        @pl.when(k == 0)
        def _():
            acc[...] = part

        @pl.when(k > 0 if in_place else jnp.logical_and(k > 0, k < nk - 1))
        def _():
            acc[...] += part

        if not in_place:
            @pl.when(k == nk - 1)
            def _():
                o_ref[...] = (acc[...] + part).astype(o_ref.dtype)

    in_specs = [pl.BlockSpec(a_blk, a_map), pl.BlockSpec(b_blk, b_map)]
    operands = [a, b]
    scratch = [pltpu.VMEM(acc_shape, F32)] if nk > 1 and not in_place else []
    vmem = 2 * (_nbytes(acc_shape, out_dtype) + _nbytes([d for d in a_blk if d], a.dtype)
                + _nbytes([d for d in b_blk if d], b.dtype)) + (2 + len(scratch)) * _nbytes(acc_shape, F32)
    if add is not None:
        in_specs.append(pl.BlockSpec(add_blk, add_map))
        operands.append(add)
        vmem += 2 * _nbytes(acc_shape, add.dtype)
    out, *dsts = _call(
        body, name=name, grid=grid, in_specs=in_specs, out_specs=[pl.BlockSpec(o_blk, o_map)],
        out_shape=[jax.ShapeDtypeStruct(out_shape, out_dtype)], operands=operands, scratch=scratch,
        sem=("parallel", "parallel", "arbitrary"), vmem=vmem + 8 * _MIB, carries=carries)
    return (out, dsts) if carries else out


def _mm_nn(name, x, w, out_dtype, carries=()):
    t, kd = x.shape
    n = w.shape[1]
    tm, tn, tk = _tile(t, 1024), _tile(n, 2048 if kd > 2048 else 1024), _tile(kd, 1536 if kd > 2048 else 2048)
    return _mm(name, x, w, dims=_NN, grid=(t // tm, n // tn, kd // tk),
               a_blk=(tm, tk), a_map=lambda i, j, k: (i, k), b_blk=(tk, tn), b_map=lambda i, j, k: (k, j),
               o_blk=(tm, tn), o_map=lambda i, j, k: (i, j), out_shape=(t, n), out_dtype=out_dtype, carries=carries)


def _mm_nn_blk(name, x, g, out_dtype, halves=False, carries=()):
    t, kd = x.shape
    cw = g.shape[2]
    tm = _tile(t, 1024)
    if halves:
        o_blk, o_map, out_shape = (None, tm, cw), (lambda i, j, k: (j // 4, i, j % 4)), (2, t, 4 * cw)
    else:
        o_blk, o_map, out_shape = (tm, cw), (lambda i, j, k: (i, j)), (t, N_DEV * cw)
    return _mm(name, x, g, dims=_NN, grid=(t // tm, N_DEV, 1),
               a_blk=(tm, kd), a_map=lambda i, j, k: (i, 0), b_blk=(None, kd, cw), b_map=lambda i, j, k: (j, 0, 0),
               o_blk=o_blk, o_map=o_map, out_shape=out_shape, out_dtype=out_dtype, carries=carries)


def _mm_nt(name, dy, w, out_dtype, add=None, carries=()):
    t, n = dy.shape
    kd = w.shape[0]
    tm, tn = _tile(t, 1024), _tile(kd, 1408)
    kw = {}
    if add is not None:
        kw = dict(add=add, add_blk=(tm, tn), add_map=lambda i, j, k: (i, j))
    return _mm(name, dy, w, dims=_NT, grid=(t // tm, kd // tn, 1),
               a_blk=(tm, n), a_map=lambda i, j, k: (i, 0), b_blk=(tn, n), b_map=lambda i, j, k: (j, 0),
               o_blk=(tm, tn), o_map=lambda i, j, k: (i, j), out_shape=(t, kd), out_dtype=out_dtype, carries=carries, **kw)


def _mm_nt_blk(name, dy, g, out_dtype, halves=False, carries=()):
    kd, cw = g.shape[1], g.shape[2]
    t = dy.shape[1] if halves else dy.shape[0]
    tm = _tile(t, 512)
    slabs = 2 if cw > 512 else 4
    nk = N_DEV // slabs
    if halves:
        a_blk, a_map = (None, tm, slabs * cw), (lambda i, j, k: (k // (nk // 2), i, k % (nk // 2)))
    else:
        a_blk, a_map = (tm, slabs * cw), (lambda i, j, k: (i, k))
    return _mm(name, dy, g, dims=_NT, grid=(t // tm, 1, nk), b_slabs=slabs,
               a_blk=a_blk, a_map=a_map, b_blk=(slabs, kd, cw), b_map=lambda i, j, k: (k, 0, 0),
               o_blk=(tm, kd), o_map=lambda i, j, k: (i, 0), out_shape=(t, kd), out_dtype=out_dtype, carries=carries)


def _mm_tn(name, x, dy, carries=()):
    t, kd = x.shape
    n = dy.shape[1]
    tmx, tk = _tile(kd, 1024), _tile(t, 2048)
    return _mm(name, x, dy, dims=_TN, grid=(kd // tmx, 1, t // tk),
               a_blk=(tk, tmx), a_map=lambda i, j, k: (k, i), b_blk=(tk, n), b_map=lambda i, j, k: (k, 0),
               o_blk=(tmx, n), o_map=lambda i, j, k: (i, 0), out_shape=(kd, n), out_dtype=BF, carries=carries)


def _mm_tn_blk(name, x, dy, halves=False, carries=()):
    t, kd = x.shape
    cw = dy.shape[2] // 4 if halves else dy.shape[1] // N_DEV
    tmx, tk = _tile(kd, 2048 if cw <= 512 else 1024), _tile(t, 2048)
    if halves:
        b_blk, b_map = (None, tk, cw), (lambda i, j, k: (j // 4, k, j % 4))
    else:
        b_blk, b_map = (tk, cw), (lambda i, j, k: (k, j))
    return _mm(name, x, dy, dims=_TN, grid=(kd // tmx, N_DEV, t // tk),
               a_blk=(tk, tmx), a_map=lambda i, j, k: (k, i), b_blk=b_blk, b_map=b_map,
               o_blk=(None, tmx, cw), o_map=lambda i, j, k: (j, i, 0), out_shape=(N_DEV, kd, cw), out_dtype=BF,
               carries=carries)


def _rms(x, g):
    r = lax.rsqrt(jnp.mean(x * x, axis=-1, keepdims=True) + NORM_EPS)
    return x * r * g


def _rms_bwd_math(x, g, dy):
    d = x.shape[-1]
    r = lax.rsqrt(jnp.mean(x * x, axis=-1, keepdims=True) + NORM_EPS)
    xh = x * r
    u = dy * g
    dx = r * u - xh * (jnp.sum(xh * u, axis=-1, keepdims=True) * (r / d))
    return dx, jnp.sum(dy * xh, axis=0, keepdims=True)


def _row_specs(tr, d, n):
    return [pl.BlockSpec((tr, d), lambda i: (i, 0)) for _ in range(n)]


def _vec_specs(d, n):
    return [pl.BlockSpec((1, d), lambda i: (0, 0)) for _ in range(n)]


def _norms_fwd(name, h, gains, carries=()):
    t, d = h.shape
    tr, ng = _tile(t, 256), len(gains)

    def body(h_ref, *refs):
        x = h_ref[...]
        for g_ref, o_ref in zip(refs[:ng], refs[ng:]):
            o_ref[...] = _rms(x, g_ref[...]).astype(BF)

    return _call(
        body, name=name, grid=(t // tr,), in_specs=_row_specs(tr, d, 1) + _vec_specs(d, ng),
        out_specs=_row_specs(tr, d, ng), out_shape=[jax.ShapeDtypeStruct((t, d), BF)] * ng,
        operands=(h, *[g.reshape(1, d) for g in gains]), sem=("parallel",), carries=carries)


def _resid_norms(name, h, m, g_post, gains, carries=()):
    t, d = h.shape
    tr, ng = _tile(t, 256), len(gains)

    def body(h_ref, m_ref, gp_ref, *refs):
        hn = h_ref[...] + _rms(m_ref[...], gp_ref[...])
        refs[ng][...] = hn
        for g_ref, o_ref in zip(refs[:ng], refs[ng + 1:]):
            o_ref[...] = _rms(hn, g_ref[...]).astype(BF)

    return _call(
        body, name=name, grid=(t // tr,), in_specs=_row_specs(tr, d, 2) + _vec_specs(d, 1 + ng),
        out_specs=_row_specs(tr, d, 1 + ng),
        out_shape=[jax.ShapeDtypeStruct((t, d), F32)] + [jax.ShapeDtypeStruct((t, d), BF)] * ng,
        operands=(h, m, g_post.reshape(1, d), *[g.reshape(1, d) for g in gains]), sem=("parallel",), carries=carries)


def _resid_loss(name, h, m, g_post, target):
    t, d = h.shape
    tr = _tile(t, 256)

    def body(h_ref, m_ref, gp_ref, t_ref, dy_ref, loss_ref, dm_ref, dg_ref):
        mv = m_ref[...]
        diff = h_ref[...] + _rms(mv, gp_ref[...]) - t_ref[...]
        dy = diff * (1.0 / d)
        dy_ref[...] = dy
        dm, dg = _rms_bwd_math(mv, gp_ref[...], dy)
        dm_ref[...] = dm.astype(BF)

        @pl.when(pl.program_id(0) == 0)
        def _():
            loss_ref[...] = jnp.zeros_like(loss_ref)
            dg_ref[...] = jnp.zeros_like(dg_ref)

        per_row = jnp.sum(diff * diff, axis=-1, keepdims=True) * (1.0 / d)
        loss_ref[...] += 0.5 * jnp.sum(per_row, axis=0, keepdims=True)
        dg_ref[...] += dg

    row = pl.BlockSpec((tr, d), lambda i: (i, 0))
    return pl.pallas_call(
        body, name=name, grid=(t // tr,),
        in_specs=_row_specs(tr, d, 2) + _vec_specs(d, 1) + _row_specs(tr, d, 1),
        out_specs=[row, pl.BlockSpec((1, 1), lambda i: (0, 0)), row, pl.BlockSpec((1, d), lambda i: (0, 0))],
        out_shape=[jax.ShapeDtypeStruct((t, d), F32), jax.ShapeDtypeStruct((1, 1), F32),
                   jax.ShapeDtypeStruct((t, d), BF), jax.ShapeDtypeStruct((1, d), F32)],
        compiler_params=_params(("arbitrary",)),
    )(h, m, g_post.reshape(1, d), target)


def _norm_bwd(name, x, g, dy, res=None, out_dtype=F32, then=None, carries=()):
    t, d = x.shape
    tr = _tile(t, 256)
    has_res, has_then = res is not None, then is not None
    gains, dys = (g, dy) if isinstance(g, (list, tuple)) else ([g], [dy])
    n = len(gains)

    def accumulate(dg_ref, dg):
        @pl.when(pl.program_id(0) == 0)
        def _():
            dg_ref[...] = jnp.zeros_like(dg_ref)

        dg_ref[...] += dg

    def body(x_ref, *refs):
        dy_refs, g_refs, rest = refs[:n], refs[n:2 * n], list(refs[2 * n:])
        dx = rest.pop(0)[...] if has_res else 0.0
        then_refs = (rest.pop(0), rest.pop(0)) if has_then else None
        dx_ref, dg_refs = rest[0], rest[1:1 + n]
        xv = x_ref[...].astype(F32)
        for dy_ref, g_ref, dg_ref in zip(dy_refs, g_refs, dg_refs):
            dx_one, dg = _rms_bwd_math(xv, g_ref[...], dy_ref[...].astype(F32))
            dx = dx + dx_one
            accumulate(dg_ref, dg)
        dx_ref[...] = dx.astype(dx_ref.dtype)
        if has_then:
            d2, dg2 = _rms_bwd_math(then_refs[0][...], then_refs[1][...], dx)
            rest[1 + n][...] = d2.astype(BF)
            accumulate(rest[2 + n], dg2)

    ops = [x, *dys, *[gain.reshape(1, d) for gain in gains]] + ([res] if has_res else [])
    in_specs = _row_specs(tr, d, 1 + n) + _vec_specs(d, n) + _row_specs(tr, d, int(has_res))
    out_specs = [pl.BlockSpec((tr, d), lambda i: (i, 0))] + [pl.BlockSpec((1, d), lambda i: (0, 0))] * n
    out_shape = [jax.ShapeDtypeStruct((t, d), out_dtype)] + [jax.ShapeDtypeStruct((1, d), F32)] * n
    if has_then:
        ops += [then[0], then[1].reshape(1, d)]
        in_specs += _row_specs(tr, d, 1) + _vec_specs(d, 1)
        out_specs += [pl.BlockSpec((tr, d), lambda i: (i, 0)), pl.BlockSpec((1, d), lambda i: (0, 0))]
        out_shape += [jax.ShapeDtypeStruct((t, d), BF), jax.ShapeDtypeStruct((1, d), F32)]
    return _call(body, name=name, grid=(t // tr,), in_specs=in_specs, out_specs=out_specs, out_shape=out_shape,
                 operands=ops, sem=("arbitrary",), carries=carries)


_GELU_C = math.sqrt(2.0 / math.pi)
_GELU_A = 0.044715


def _gelu(x):
    return 0.5 * x * (1.0 + jnp.tanh(_GELU_C * (x + _GELU_A * x * x * x)))


def _gelu_and_grad(x):
    th = jnp.tanh(_GELU_C * (x + _GELU_A * x * x * x))
    grad = 0.5 * (1.0 + th) + 0.5 * x * (1.0 - th * th) * (_GELU_C * (1.0 + 3.0 * _GELU_A * x * x))
    return 0.5 * x * (1.0 + th), grad


def _causal(n):
    return lax.broadcasted_iota(jnp.int32, (n, n), 1) <= lax.broadcasted_iota(jnp.int32, (n, n), 0)


def _sgu_fwd(name, p, v_gain, w_s, b_st, carries=()):
    t, da2 = p.shape
    da = da2 // 2
    ng = da // GROUP

    def body(p_ref, vg_ref, ws_ref, bst_ref, o_ref):
        keep = _causal(GROUP)
        for g in range(ng):
            lo = g * GROUP
            u = _gelu(p_ref[:, lo:lo + GROUP])
            vn = _rms(_gelu(p_ref[:, da + lo:da + lo + GROUP]), vg_ref[:, lo:lo + GROUP])
            w = jnp.where(keep, ws_ref[g], 0.0).astype(BF)
            mixed = jnp.dot(w, vn.astype(BF), preferred_element_type=F32) + bst_ref[:, g:g + 1]
            o_ref[:, lo:lo + GROUP] = (u * mixed).astype(BF)

    return _call(
        body, name=name, grid=(t // GROUP,),
        in_specs=[pl.BlockSpec((GROUP, da2), lambda i: (i, 0)), pl.BlockSpec((1, da), lambda i: (0, 0)),
                  pl.BlockSpec((ng, GROUP, GROUP), lambda i: (0, 0, 0)), pl.BlockSpec((GROUP, ng), lambda i: (0, 0))],
        out_specs=[pl.BlockSpec((GROUP, da), lambda i: (i, 0))], out_shape=[jax.ShapeDtypeStruct((t, da), BF)],
        operands=(p, v_gain, w_s, b_st), sem=("parallel",), carries=carries)


def _sgu_bwd(name, p, v_gain, w_s, w_st, b_st, dout, carries=()):
    t, da2 = p.shape
    da = da2 // 2
    ng = da // GROUP

    def body(p_ref, vg_ref, ws_ref, wst_ref, bst_ref, do_ref, dp_ref, dvg_ref, dws_ref, dbst_ref):
        @pl.when(pl.program_id(0) == 0)
        def _():
            dvg_ref[...] = jnp.zeros_like(dvg_ref)
            dws_ref[...] = jnp.zeros_like(dws_ref)
            dbst_ref[...] = jnp.zeros_like(dbst_ref)

        keep = _causal(GROUP)
        keep_t = lax.broadcasted_iota(jnp.int32, (GROUP, GROUP), 0) <= lax.broadcasted_iota(jnp.int32, (GROUP, GROUP), 1)
        for g in range(ng):
            lo = g * GROUP
            u, du = _gelu_and_grad(p_ref[:, lo:lo + GROUP])
            v, dv_act = _gelu_and_grad(p_ref[:, da + lo:da + lo + GROUP])
            gain = vg_ref[:, lo:lo + GROUP]
            r = lax.rsqrt(jnp.mean(v * v, axis=-1, keepdims=True) + NORM_EPS)
            vh = v * r
            vnb = (vh * gain).astype(BF)
            w = jnp.where(keep, ws_ref[g], 0.0).astype(BF)
            wt = jnp.where(keep_t, wst_ref[g], 0.0).astype(BF)
            mixed = jnp.dot(w, vnb, preferred_element_type=F32) + bst_ref[:, g:g + 1]
            dout_g = do_ref[:, lo:lo + GROUP].astype(F32)
            dmixed = dout_g * u
            dmb = dmixed.astype(BF)
            dbst_ref[:, g:g + 1] += jnp.sum(dmixed, axis=1, keepdims=True)
            dws_ref[g] += jnp.where(keep, lax.dot_general(dmb, vnb, _NT, preferred_element_type=F32), 0.0)
            dvn = jnp.dot(wt, dmb, preferred_element_type=F32)
            dvg_ref[:, lo:lo + GROUP] += jnp.sum(dvn * vh, axis=0, keepdims=True)
            dvh = dvn * gain
            dv = r * dvh - vh * (jnp.sum(vh * dvh, axis=-1, keepdims=True) * (r / GROUP))
            dp_ref[:, lo:lo + GROUP] = (dout_g * mixed * du).astype(BF)
            dp_ref[:, da + lo:da + lo + GROUP] = (dv * dv_act).astype(BF)

    full = lambda *shape: pl.BlockSpec(shape, lambda i: (0,) * len(shape))
    outs = _call(
        body, name=name, grid=(t // GROUP,),
        in_specs=[pl.BlockSpec((GROUP, da2), lambda i: (i, 0)), full(1, da), full(ng, GROUP, GROUP), full(ng, GROUP, GROUP),
                  full(GROUP, ng), pl.BlockSpec((GROUP, da), lambda i: (i, 0))],
        out_specs=[pl.BlockSpec((GROUP, da2), lambda i: (i, 0)), full(1, da), full(ng, GROUP, GROUP), full(GROUP, ng)],
        out_shape=[jax.ShapeDtypeStruct((t, da2), BF), jax.ShapeDtypeStruct((1, da), F32),
                   jax.ShapeDtypeStruct((ng, GROUP, GROUP), F32), jax.ShapeDtypeStruct((GROUP, ng), F32)],
        operands=(p, v_gain, w_s, w_st, b_st, dout), sem=("arbitrary",), carries=carries)
    return outs[:4], outs[4:]


_CONV_ROWS = 256
_CONV_COLS = 128


def _shift_down(x, prev, k):
    top = pltpu.roll(jnp.concatenate([prev, x[:8]], axis=0), k, 0)[8:16]
    return jnp.concatenate([top, pltpu.roll(x, k, 0)[8:]], axis=0)


def _conv_taps(a_ref, half, r0, first):
    rows = _rows(a_ref)
    x = a_ref[half, pl.ds(r0, rows), :]
    if first:
        prev = jnp.zeros((8, x.shape[1]), F32)
        return x, _shift_down(x, prev, 1), _shift_down(x, prev, 2)
    return x, a_ref[half, pl.ds(r0 - 1, rows), :], a_ref[half, pl.ds(r0 - 2, rows), :]


def _rows(a_ref):
    return min(_CONV_ROWS, a_ref.shape[1])


def _conv_fwd(name, a3, cw, cb, carries=()):
    _, t, f = a3.shape
    tc, rows = _CONV_COLS, min(_CONV_ROWS, t)

    def body(a_ref, cw_ref, cb_ref, y_ref):
        def chunk(r0, first):
            c = []
            for half in range(2):
                x, x1, x2 = _conv_taps(a_ref, half, r0, first)
                w = cw_ref[half]
                c.append(cb_ref[half] + (w[0:1] * x2 + w[1:2] * x1 + w[2:3] * x))
            y_ref[pl.ds(r0, rows), :] = (c[0] * jax.nn.sigmoid(c[0]) * c[1]).astype(BF)

        chunk(0, True)

        @pl.loop(1, t // rows)
        def _(r):
            chunk(pl.multiple_of(r * rows, rows), False)

    col = lambda *lead: pl.BlockSpec((*lead, tc), lambda j: (0,) * len(lead) + (j,))
    y, *dsts = _call(
        body, name=name, grid=(f // tc,), in_specs=[col(2, t), col(2, CONV_TAPS), col(2, 1)],
        out_specs=[col(t)], out_shape=[jax.ShapeDtypeStruct((t, f), BF)], operands=(a3, cw, cb),
        sem=("parallel",), vmem=40 * _MIB, carries=carries)
    return y, dsts


def _conv_bwd(name, a3, cw, cb, dy, carries=()):
    _, t, f = a3.shape
    tc, rows = _CONV_COLS, min(_CONV_ROWS, t)
    n_steps = t // rows

    def body(a_ref, cw_ref, cb_ref, dy_ref, da_ref, dcw_ref, dcb_ref, dc_ref):
        dcw_ref[...] = jnp.zeros_like(dcw_ref)
        dcb_ref[...] = jnp.zeros_like(dcb_ref)

        def chunk(r0, first, nxt):
            taps, c = [], []
            for half in range(2):
                x, x1, x2 = _conv_taps(a_ref, half, r0, first)
                w = cw_ref[half]
                taps.append((x2, x1, x))
                c.append(cb_ref[half] + (w[0:1] * x2 + w[1:2] * x1 + w[2:3] * x))
            gate, val = c
            sg = jax.nn.sigmoid(gate)
            dyv = dy_ref[pl.ds(r0, rows), :].astype(F32)
            dcs = (dyv * val * (sg * (1.0 + gate * (1.0 - sg))), dyv * (gate * sg))
            new_nxt = []
            for half in range(2):
                dc, w = dcs[half], cw_ref[half]
                dcb_ref[half] += jnp.sum(dc, axis=0, keepdims=True)
                for tap in range(CONV_TAPS):
                    dcw_ref[half, tap:tap + 1, :] += jnp.sum(dc * taps[half][tap], axis=0, keepdims=True)
                dc_ref[half, 0:rows, :] = dc
                dc_ref[half, rows:rows + 8, :] = nxt[half]
                da = w[2:3] * dc + w[1:2] * dc_ref[half, 1:rows + 1, :] + w[0:1] * dc_ref[half, 2:rows + 2, :]
                da_ref[half, pl.ds(r0, rows), :] = da.astype(BF)
                new_nxt.append(dc[:8])
            return tuple(new_nxt)

        zeros = jnp.zeros((8, tc), F32)
        nxt = lax.fori_loop(0, n_steps - 1, lambda s, nxt: chunk(pl.multiple_of((n_steps - 1 - s) * rows, rows), False, nxt),
                            (zeros, zeros))
        chunk(0, True, nxt)

    col = lambda *lead: pl.BlockSpec((*lead, tc), lambda j: (0,) * len(lead) + (j,))
    outs = _call(
        body, name=name, grid=(f // tc,), in_specs=[col(2, t), col(2, CONV_TAPS), col(2, 1), col(t)],
        out_specs=[col(2, t), col(2, CONV_TAPS), col(2, 1)],
        out_shape=[jax.ShapeDtypeStruct((2, t, f), BF), jax.ShapeDtypeStruct((2, CONV_TAPS, f), F32),
                   jax.ShapeDtypeStruct((2, 1, f), F32)],
        operands=(a3, cw, cb, dy), scratch=[pltpu.VMEM((2, rows + 8, tc), F32)], sem=("parallel",), vmem=48 * _MIB,
        carries=carries)
    return outs[:3], outs[3:]


_ATT_BLOCK = 256


def _split_dot(x, tri):
    hi = x.astype(BF)
    lo = (x - hi.astype(F32)).astype(BF)
    return jnp.dot(hi, tri, preferred_element_type=F32) + jnp.dot(lo, tri, preferred_element_type=F32)


_ATT_HEADS_FWD = 8
_ATT_HEADS_BWD = 4


def _logits(qb, kb, diagonal):
    z = lax.dot_general(qb, kb, _NT, preferred_element_type=F32) * (1.0 / math.sqrt(GROUP))
    lb = jnp.minimum(z, 0.0) - jnp.log(1.0 + jnp.exp(-jnp.abs(z)))
    if not diagonal:
        return lb, lb - z, None
    mask = lax.broadcasted_iota(jnp.int32, z.shape, 1) < lax.broadcasted_iota(jnp.int32, z.shape, 0)
    return lb, jnp.where(mask, lb - z, 0.0), mask


def _head(ref, g, rows=slice(None)):
    return ref[rows, g * GROUP:(g + 1) * GROUP]


def _attn_fwd(name, q, k, v, carries=()):
    t, hd = q.shape
    blk = min(_ATT_BLOCK, t)
    heads = min(_ATT_HEADS_FWD, hd // GROUP)
    gs = range(heads)

    def body(q_ref, k_ref, v_ref, o_ref, rest_ref, first_ref):
        hg, i = pl.program_id(0), pl.program_id(1)
        ri = lax.broadcasted_iota(jnp.int32, (blk, blk), 0)
        ci = lax.broadcasted_iota(jnp.int32, (blk, blk), 1)
        tri = (ri >= ci).astype(BF)

        def tile(j, state, diagonal):
            keys = pl.ds(pl.multiple_of(j * blk, blk), blk)
            logit = [_logits(_head(q_ref, g), _head(k_ref, g, keys), diagonal) for g in gs]
            incl = [_split_dot(logit[g][1], tri) for g in gs]
            a = [jnp.exp(logit[g][0] + (incl[g] - logit[g][1] + state[g][0])) for g in gs]
            if diagonal:
                a = [jnp.where(logit[g][2], a[g], 0.0) for g in gs]
            out = [state[g][1] + jnp.dot(a[g].astype(BF), _head(v_ref, g, keys), preferred_element_type=F32) for g in gs]
            return tuple((state[g][0] + incl[g][:, 0:1], out[g]) for g in gs)

        state = tile(i, ((jnp.zeros((blk, 1), F32), jnp.zeros((blk, GROUP), F32)),) * heads, True)

        def more(carry):
            j, state = carry
            live = functools.reduce(jnp.maximum, [jnp.max(right) for right, _ in state])
            return jnp.logical_and(j >= 0, live > EXP_FLOOR)

        j, state = lax.while_loop(more, lambda c: (c[0] - 1, tile(c[0], c[1], False)), (i - 1, state))
        for g, (right, acc) in enumerate(state):
            o_ref[:, g * GROUP:(g + 1) * GROUP] = acc.astype(BF)
            rest_ref[:, g * GROUP:(g + 1) * GROUP] = jnp.broadcast_to(right, (blk, GROUP))
        first_ref[hg, i] = (j + 1).astype(F32)

    qspec = pl.BlockSpec((blk, heads * GROUP), lambda h, i: (i, h))
    kvspec = pl.BlockSpec((t, heads * GROUP), lambda h, i: (0, h), pipeline_mode=pl.Buffered(1))
    groups = hd // (heads * GROUP)
    outs = _call(
        body, name=name, grid=(groups, t // blk), in_specs=[qspec, kvspec, kvspec],
        out_specs=[qspec, qspec, pl.BlockSpec(memory_space=pltpu.SMEM)],
        out_shape=[jax.ShapeDtypeStruct((t, hd), BF), jax.ShapeDtypeStruct((t, hd), F32),
                   jax.ShapeDtypeStruct((groups, t // blk), F32)],
        operands=(q, k, v), sem=("arbitrary", "arbitrary"), vmem=40 * _MIB, carries=carries, middle_at=0.75)
    return outs[:3], outs[3:]


def _attn_bwd(name, q, k, v, rest, first, do, carries=()):
    t, hd = q.shape
    blk = min(_ATT_BLOCK, t)
    nq = t // blk
    scale = 1.0 / math.sqrt(GROUP)
    heads = min(_ATT_HEADS_BWD, hd // GROUP)
    per_first = min(_ATT_HEADS_FWD, hd // GROUP) // heads
    gs = range(heads)

    def body(first_ref, q_ref, k_ref, v_ref, rest_ref, do_ref, dq_ref, dk_ref, dv_ref, dk_acc, dv_acc):
        hg, i = pl.program_id(0), pl.program_id(1)

        @pl.when(i == 0)
        def _():
            dk_acc[...] = jnp.zeros_like(dk_acc)
            dv_acc[...] = jnp.zeros_like(dv_acc)

        ri = lax.broadcasted_iota(jnp.int32, (blk, blk), 0)
        ci = lax.broadcasted_iota(jnp.int32, (blk, blk), 1)
        tri = (ri <= ci).astype(BF)

        def tile(j, state, diagonal):
            keys = pl.ds(pl.multiple_of(j * blk, blk), blk)
            qs, dos = [_head(q_ref, g) for g in gs], [_head(do_ref, g) for g in gs]
            kb, vb = [_head(k_ref, g, keys) for g in gs], [_head(v_ref, g, keys) for g in gs]
            logit = [_logits(qs[g], kb[g], diagonal) for g in gs]
            pre = [_split_dot(logit[g][1], tri) for g in gs]
            a = [jnp.exp(logit[g][0] + (rest_ref[:, g * GROUP:g * GROUP + 1] - state[g][0] - pre[g])) for g in gs]
            if diagonal:
                a = [jnp.where(logit[g][2], a[g], 0.0) for g in gs]
            gw = [a[g] * lax.dot_general(dos[g], vb[g], _NT, preferred_element_type=F32) for g in gs]
            gpre = [_split_dot(gw[g], tri) for g in gs]
            dz = []
            for g in gs:
                beta = jnp.exp(logit[g][0])
                d = (gw[g] * (1.0 - beta) - (state[g][1] + gpre[g] - gw[g]) * beta) * scale
                dz.append((jnp.where(logit[g][2], d, 0.0) if diagonal else d).astype(BF))
            for g in gs:
                dk_acc[keys, g * GROUP:(g + 1) * GROUP] += lax.dot_general(dz[g], qs[g], _TN, preferred_element_type=F32)
                dv_acc[keys, g * GROUP:(g + 1) * GROUP] += lax.dot_general(a[g].astype(BF), dos[g], _TN,
                                                                           preferred_element_type=F32)
            return tuple((state[g][0] + pre[g][:, blk - 1:blk], state[g][1] + gpre[g][:, blk - 1:blk],
                          state[g][2] + jnp.dot(dz[g], kb[g], preferred_element_type=F32)) for g in gs)

        zero = jnp.zeros((blk, 1), F32)
        first_block = jnp.clip(first_ref[hg // per_first, i].astype(jnp.int32), 0, i)
        state = lax.fori_loop(first_block, i, lambda j, c: tile(j, c, False),
                              ((zero, zero, jnp.zeros((blk, GROUP), F32)),) * heads)
        for g, (_, _, dq) in enumerate(tile(i, state, True)):
            dq_ref[:, g * GROUP:(g + 1) * GROUP] = dq.astype(BF)

        @pl.when(i == nq - 1)
        def _():
            dk_ref[...] = dk_acc[...].astype(BF)
            dv_ref[...] = dv_acc[...].astype(BF)

    qspec = pl.BlockSpec((blk, heads * GROUP), lambda h, i: (i, h))
    kvspec = pl.BlockSpec((t, heads * GROUP), lambda h, i: (0, h), pipeline_mode=pl.Buffered(1))
    outs = _call(
        body, name=name, grid=(hd // (heads * GROUP), nq),
        in_specs=[pl.BlockSpec(memory_space=pltpu.SMEM), qspec, kvspec, kvspec, qspec, qspec],
        out_specs=[qspec, kvspec, kvspec], out_shape=[jax.ShapeDtypeStruct((t, hd), BF)] * 3,
        operands=(first, q, k, v, rest, do),
        scratch=[pltpu.VMEM((t, heads * GROUP), F32), pltpu.VMEM((t, heads * GROUP), F32)],
        sem=("arbitrary", "arbitrary"), vmem=48 * _MIB, carries=carries)
    return outs[:3], outs[3:]


def _adamw_math(w, g, m, v):
    m = ADAM_B1 * m + (1.0 - ADAM_B1) * g
    v = ADAM_B2 * v + (1.0 - ADAM_B2) * (g * g)
    m_hat = m / (1.0 - ADAM_B1 ** ADAM_STEP)
    v_hat = v / (1.0 - ADAM_B2 ** ADAM_STEP)
    return -ADAM_LR * (m_hat / (jnp.sqrt(v_hat) + ADAM_EPS) + ADAM_WD * w), m, v


def _sum_adamw(name, parts, w, m, v):
    layers, r, c = w.shape
    budget = 512 * 1024 // layers
    tr = r if r * c <= budget else _tile_rows(r, max(8, (budget // c) // 8 * 8))
    n = r // tr

    def body(*refs):
        p_refs, (w_ref, m_ref, v_ref, g_ref, d_ref, nm_ref, nv_ref) = refs[:layers], refs[layers:]
        for layer, p_ref in enumerate(p_refs):
            @pl.when(pl.program_id(0) == layer)
            def _(p_ref=p_ref):
                g = p_ref[0].astype(F32)
                for dev in range(1, N_DEV):
                    g = g + p_ref[dev].astype(F32)
                g_ref[...] = g
                d_ref[...], nm_ref[...], nv_ref[...] = _adamw_math(w_ref[...], g, m_ref[...], v_ref[...])

    def part_spec(layer):
        return pl.BlockSpec((N_DEV, tr, c), lambda l, i: (0, jnp.where(l < layer, 0, jnp.where(l == layer, i, n - 1)), 0))

    row = pl.BlockSpec((None, tr, c), lambda l, i: (l, i, 0))
    return pl.pallas_call(
        body, name=name, grid=(layers, n), in_specs=[part_spec(layer) for layer in range(layers)] + [row] * 3,
        out_specs=[row] * 4, out_shape=[jax.ShapeDtypeStruct((layers, r, c), F32)] * 4,
        compiler_params=_params(("arbitrary", "arbitrary"), 40 * _MIB),
    )(*parts, w, m, v)


def _tile_rows(r, pref):
    t = min(r, pref)
    while r % t or t % 8:
        t -= 1
    return t


def _exchange(name, carries):
    return _call(lambda: None, name=name, grid=(1,), in_specs=[], out_specs=[], out_shape=[], operands=(), carries=carries)


def _all_reduce_small(name, groups):
    c = groups[0][0].shape[1]
    parts, offsets, starts, r = [], [], [], 0
    for group in groups:
        starts.append(r)
        for p in group:
            parts.append(p)
            offsets.append(r)
            r += p.shape[0]
        r = -(-r // 8) * 8
    n = len(parts)

    def body(*refs):
        out_ref, slots, send_sems, recv_sems = refs[n:]
        x, y, c_, me = _place()
        slots[me] = jnp.zeros((r, c), F32)
        for p_ref, off in zip(refs[:n], offsets):
            slots[me, off:off + p_ref.shape[0], :] = p_ref[...]

        here, sibling = (x, y, c_), (x, y, 1 - c_)
        chips = [(1 - x, y), (x, 1 - y), (1 - x, 1 - y)]

        def copy(sem, block, to):
            slot = slots.at[4 * block[0] + 2 * block[1] + block[2]]
            return pltpu.make_async_remote_copy(
                src_ref=slot, dst_ref=slot, send_sem=send_sems.at[sem], recv_sem=recv_sems.at[sem], device_id=to,
                device_id_type=MESH)

        sent = [copy(0, here, sibling)] + [copy(1 + j, here, (*chip, c_)) for j, chip in enumerate(chips)]
        for cp in sent:
            cp.start()
        for j, chip in enumerate(chips):
            copy(1 + j, (*chip, c_), here).wait_recv()
            sent.append(copy(4 + j, (*chip, c_), sibling))
            sent[-1].start()
        copy(0, sibling, here).wait_recv()
        for j, chip in enumerate(chips):
            copy(4 + j, (*chip, 1 - c_), here).wait_recv()
        for cp in sent:
            cp.wait_send()
        total = slots[0]
        for dev in range(1, N_DEV):
            total = total + slots[dev]
        out_ref[...] = total

    vm = pl.BlockSpec(memory_space=pltpu.VMEM)
    summed = pl.pallas_call(
        body, name=name, in_specs=[vm] * n, out_specs=vm, out_shape=jax.ShapeDtypeStruct((r, c), F32),
        scratch_shapes=[pltpu.VMEM((N_DEV, r, c), F32), pltpu.SemaphoreType.DMA((7,)), pltpu.SemaphoreType.DMA((7,))],
        compiler_params=_params(None, (N_DEV + 6) * r * c * 4 + 8 * _MIB),
    )(*parts)
    return summed, starts


def _adamw(name, g, w, m, v):
    cols = w.shape[-1]
    flat = lambda a: a.reshape(-1, cols)

    def body(g_ref, w_ref, m_ref, v_ref, d_ref, nm_ref, nv_ref):
        d_ref[...], nm_ref[...], nv_ref[...] = _adamw_math(w_ref[...], g_ref[...], m_ref[...], v_ref[...])

    outs = pl.pallas_call(body, name=name, out_shape=[jax.ShapeDtypeStruct(flat(w).shape, F32)] * 3)(
        flat(g), flat(w), flat(m), flat(v))
    return [o.reshape(w.shape) for o in outs]


def kernel(x, pre_mix_g, post_mix_g, pre_ffn_g, post_ffn_g, a_w_in, a_v_norm_g, a_w_spatial, a_b_spatial, a_w_out, kv_norm_g, w_k, w_v, b_w_q, b_w_o, ffn_w_up, ffn_conv_w, ffn_conv_b, ffn_w_down, loss_target, m_pre_mix_g, m_post_mix_g, m_pre_ffn_g, m_post_ffn_g, m_a_w_in, m_a_v_norm_g, m_a_w_spatial, m_a_b_spatial, m_a_w_out, m_kv_norm_g, m_w_k, m_w_v, m_b_w_q, m_b_w_o, m_ffn_w_up, m_ffn_conv_w, m_ffn_conv_b, m_ffn_w_down, v_pre_mix_g, v_post_mix_g, v_pre_ffn_g, v_post_ffn_g, v_a_w_in, v_a_v_norm_g, v_a_w_spatial, v_a_b_spatial, v_a_w_out, v_kv_norm_g, v_w_k, v_w_v, v_b_w_q, v_b_w_o, v_ffn_w_up, v_ffn_conv_w, v_ffn_conv_b, v_ffn_w_down):
    t, d = x.shape[1], x.shape[2]
    f = ffn_w_down.shape[1] * N_DEV
    ng = d // GROUP
    me = 4 * lax.axis_index("x") + 2 * lax.axis_index("y") + lax.axis_index("c")
    x2, target = x.reshape(t, d), loss_target.reshape(t, d)

    hn0, g_in, g_cw, g_vg = _norms_fwd("pre_mix0", x2, [pre_mix_g[0]], carries=[
        _gather(a_w_in[0].astype(BF)), _gather(ffn_conv_w.reshape(2 * CONV_TAPS, -1)), _gather(a_v_norm_g)])
    up0 = ffn_w_up[0].astype(BF)
    cw_full = jnp.transpose(g_cw.reshape(N_DEV, 2, CONV_TAPS, -1), (1, 2, 0, 3)).reshape(2, CONV_TAPS, 2, f)
    cw_l = [jnp.transpose(cw_full[l], (1, 0, 2)) for l in range(2)]
    cb_l = [ffn_conv_b[l].reshape(2, 1, f) for l in range(2)]
    vg_full = g_vg.reshape(1, d)
    w_s = a_w_spatial[0]
    w_st = jnp.swapaxes(w_s, 1, 2)
    b_st = a_b_spatial[0].T

    p0, (g_out, g_up0) = _mm_nn_blk("sgu_in", hn0, g_in, F32, carries=[
        _gather(a_w_out[0].astype(BF)), _gather(up0, 0, d // 4)])
    w_out_f = g_out.reshape(d, d)
    sg, g_up0 = _sgu_fwd("sgu", p0, vg_full, w_s, b_st, carries=[_gather(up0, d // 4, d // 2, dst=g_up0)])
    mix0, (g_up0,) = _mm_nn("sgu_out", sg, w_out_f, F32, carries=[_gather(up0, d // 2, 3 * d // 4, dst=g_up0)])
    h1, fn0, g_up0 = _resid_norms("post_mix0", x2, mix0, post_mix_g[0], [pre_ffn_g[0]],
                                  carries=[_gather(up0, 3 * d // 4, d, dst=g_up0)])
    a3_0, (g_dn0,) = _mm_nn_blk("ffn0_up", fn0, g_up0, F32, halves=True, carries=[_gather(ffn_w_down[0].astype(BF))])
    y0, (g_q,) = _conv_fwd("ffn0_conv", a3_0, cw_l[0], cb_l[0], carries=[_gather(b_w_q[0].astype(BF))])
    wv = w_v.astype(BF)
    f0, (g_k, g_v) = _mm_nn("ffn0_down", y0, g_dn0.reshape(f, d), F32,
                            carries=[_gather(w_k.astype(BF)), _gather(wv, 0, d // 16)])
    h2, hn1, kvn, g_v = _resid_norms("post_ffn0", h1, f0, post_ffn_g[0], [pre_mix_g[1], kv_norm_g],
                                     carries=[_gather(wv, d // 16, d // 8, dst=g_v)])
    w_q_f, w_k_f, w_v_f = g_q.reshape(d, d), g_k.reshape(d, d), g_v.reshape(d, d)
    up1 = ffn_w_up[1].astype(BF)
    e = d // 8
    q, (g_up1,) = _mm_nn("attn_q", hn1, w_q_f, BF, carries=[_gather(up1, 0, e)])
    kk, (g_up1,) = _mm_nn("attn_k", kvn, w_k_f, BF, carries=[_gather(up1, e, 2 * e, dst=g_up1)])
    vv, (g_up1,) = _mm_nn("attn_v", kvn, w_v_f, BF, carries=[_gather(up1, 2 * e, 3 * e, dst=g_up1)])
    (att, rest, first), (g_o, g_up1) = _attn_fwd(
        "attn", q, kk, vv, carries=[_gather(b_w_o[0].astype(BF)), _gather(up1, 3 * e, 6 * e, dst=g_up1)])
    w_o_f = g_o.reshape(d, d)
    mix1, (g_up1,) = _mm_nn("attn_o", att, w_o_f, F32, carries=[_gather(up1, 6 * e, 7 * e, dst=g_up1)])
    h3, fn1, g_up1 = _resid_norms("post_mix1", h2, mix1, post_mix_g[1], [pre_ffn_g[1]],
                                  carries=[_gather(up1, 7 * e, d, dst=g_up1)])
    a3_1, (g_dn1,) = _mm_nn_blk("ffn1_up", fn1, g_up1, F32, halves=True, carries=[_gather(ffn_w_down[1].astype(BF))])
    y1, _ = _conv_fwd("ffn1_conv", a3_1, cw_l[1], cb_l[1])
    f1 = _mm_nn("ffn1_down", y1, g_dn1.reshape(f, d), F32)
    g_up = (g_up0, g_up1)
    w_dn_f = (g_dn0.reshape(f, d), g_dn1.reshape(f, d))
    dh, loss_part, df1, d_post_ffn1 = _resid_loss("loss", h3, f1, post_ffn_g[1], target)
    loss = lax.psum(loss_part[0, 0], ("x", "y", "c"))

    def blocks(dw):
        return dw.reshape(N_DEV, -1, d)

    def split(result, carries):
        return result if carries else (result, [])

    def ffn_bwd(l, dh_out, dfo, h_in, fn, a3, yv, mix, with_dw=(), with_dx=(), with_up=()):
        dw_dn, sent_dw = split(_mm_tn(f"ffn{l}_down_dw", yv, dfo, carries=with_dw), with_dw)
        dy, sent_dx = split(_mm_nt(f"ffn{l}_down_dx", dfo, w_dn_f[l], BF, carries=with_dx), with_dx)
        dw_dn = blocks(dw_dn)
        cut = dw_dn.shape[1] * 5 // 8 // 16 * 16
        (da3, dcw, dcb), (p_dn,) = _conv_bwd(f"ffn{l}_conv_bwd", a3, cw_l[l], cb_l[l], dy, carries=[_scatter(dw_dn, 0, cut)])
        dw_up, (p_dn, *sent_up) = _mm_tn_blk(f"ffn{l}_up_dw", fn, da3, halves=True,
                                             carries=[_scatter(dw_dn, cut, dst=p_dn), *with_up])
        dfn, (p_up,) = _mm_nt_blk(f"ffn{l}_up_dx", da3, g_up[l], F32, halves=True, carries=[_scatter(dw_up, 0, d // 2)])
        dh_in, d_pre, dmix, d_post_mix, p_up = _norm_bwd(
            f"pre_ffn{l}_bwd", h_in, pre_ffn_g[l], dfn, res=dh_out, then=(mix, post_mix_g[l]),
            carries=[_scatter(dw_up, d // 2, 10 * d // 16, dst=p_up)])
        return dh_in, dmix, d_pre, d_post_mix, dcw, dcb, p_dn, dw_up, p_up, list(sent_dw) + list(sent_dx) + list(sent_up)

    dh3, dmix1, d_pre_ffn1, d_post_mix1, dcw1, dcb1, p_dn1, dw_up1, p_up1, _ = ffn_bwd(
        1, dh, df1, h3, fn1, a3_1, y1, mix1)
    dw_o = _mm_tn("attn_o_dw", att, dmix1)
    datt = _mm_nt("attn_o_dx", dmix1, w_o_f, BF)
    (dq, dk, dv), (p_up1,) = _attn_bwd(
        "attn_bwd", q, kk, vv, rest, first, datt, carries=[_scatter(dw_up1, 10 * d // 16, d, dst=p_up1)])
    dw_o, qr = blocks(dw_o), d // 32
    dw_q, (p_o,) = _mm_tn("attn_q_dw", hn1, dq, carries=[_scatter(dw_o, 0, qr)])
    dw_k, (p_o,) = _mm_tn("attn_k_dw", kvn, dk, carries=[_scatter(dw_o, qr, 2 * qr, dst=p_o)])
    dw_v, (p_o,) = _mm_tn("attn_v_dw", kvn, dv, carries=[_scatter(dw_o, 2 * qr, 3 * qr, dst=p_o)])
    dhn1, (p_o,) = _mm_nt("attn_q_dx", dq, w_q_f, F32, carries=[_scatter(dw_o, 3 * qr, 4 * qr, dst=p_o)])
    dkvn = _mm_nt("attn_v_dx", dv, w_v_f, F32, add=_mm_nt("attn_k_dx", dk, w_k_f, F32))
    dh2, d_pre_mix1, d_kv, df0, d_post_ffn0 = _norm_bwd(
        "pre_mix1_kv_bwd", h2, [pre_mix_g[1], kv_norm_g], [dhn1, dkvn], res=dh3, then=(f0, post_ffn_g[0]))
    dh1, dmix0, d_pre_ffn0, d_post_mix0, dcw0, dcb0, p_dn0, dw_up0, p_up0, (p_q, p_k, p_v) = ffn_bwd(
        0, dh2, df0, h1, fn0, a3_0, y0, mix0, with_dw=[_scatter(blocks(dw_q))], with_dx=[_scatter(blocks(dw_k))],
        with_up=[_scatter(blocks(dw_v))])
    dw_out, (p_up0,) = _mm_tn("sgu_out_dw", sg, dmix0, carries=[_scatter(dw_up0, 10 * d // 16, 11 * d // 16, dst=p_up0)])
    dsg, (p_up0,) = _mm_nt("sgu_out_dx", dmix0, w_out_f, BF, carries=[_scatter(dw_up0, 11 * d // 16, 12 * d // 16, dst=p_up0)])
    (dp0, d_vg, d_ws, d_bst), (p_up0,) = _sgu_bwd(
        "sgu_bwd", p0, vg_full, w_s, w_st, b_st, dsg, carries=[_scatter(dw_up0, 12 * d // 16, d, dst=p_up0)])
    dw_in, (p_out,) = _mm_tn_blk("sgu_in_dw", hn0, dp0, carries=[_scatter(blocks(dw_out))])
    dhn0, (p_in,) = _mm_nt_blk("sgu_in_dx", dp0, g_in, F32, carries=[_scatter(dw_in, 0, 5 * d // 8)])
    grad_x, d_pre_mix0, p_in = _norm_bwd("pre_mix0_bwd", x2, pre_mix_g[0], dhn0, res=dh1,
                                         carries=[_scatter(dw_in, 5 * d // 8, 7 * d // 8, dst=p_in)])
    (p_in,) = _exchange("scatter_last", [_scatter(dw_in, 7 * d // 8, d, dst=p_in)])

    def conv_w_grad(dcw):
        return jnp.transpose(dcw, (1, 0, 2)).reshape(CONV_TAPS, 2 * f)

    small = [
        ([d_pre_mix0, d_pre_mix1], (2, d)), ([d_post_mix0, d_post_mix1], (2, d)),
        ([d_pre_ffn0, d_pre_ffn1], (2, d)), ([d_post_ffn0, d_post_ffn1], (2, d)),
        ([d_kv], (d,)), ([d_vg], (1, d)), ([d_bst.T], (1, ng, GROUP)), ([d_ws], (1, ng, GROUP, GROUP)),
        ([dcb0, dcb1], (2, 2 * f)), ([conv_w_grad(dcw0), conv_w_grad(dcw1)], (2, CONV_TAPS, 2 * f)),
    ]
    width = V7X_LANES * math.gcd(d // V7X_LANES, 2 * f // V7X_LANES)
    summed, offsets = _all_reduce_small("reduce_small", [[a.reshape(-1, width) for a in group] for group, _ in small])
    full = [summed[off:off + math.prod(shape) // width].reshape(shape) for off, (_, shape) in zip(offsets, small)]
    g_pre_mix, g_post_mix, g_pre_ffn, g_post_ffn, g_kv, g_vgain, g_bs, g_ws, g_cb, g_cwf = full
    cw_w = 2 * f // N_DEV
    g_vgain = lax.dynamic_slice_in_dim(g_vgain, me * (d // N_DEV), d // N_DEV, axis=1)
    g_cwf = lax.dynamic_slice_in_dim(g_cwf, me * cw_w, cw_w, axis=2)

    parts = [p_in, p_out, p_k, p_v, p_q, p_o, p_up0, p_up1, p_dn0, p_dn1]

    def small_update(name, g, w, m, v):
        return [g] + _adamw(name, g, w, m, v)

    def stacked(name, part0, part1, w, m, v):
        shape = (2, *part0.shape[1:])
        return [o.reshape(w.shape) for o in _sum_adamw(name, [part0, part1], w.reshape(shape), m.reshape(shape), v.reshape(shape))]

    def single(name, part, w, m, v):
        shape = (1, *part.shape[1:])
        return [o.reshape(w.shape) for o in _sum_adamw(name, [part], w.reshape(shape), m.reshape(shape), v.reshape(shape))]

    results = {
        "pre_mix_g": small_update("adam_pre_mix", g_pre_mix, pre_mix_g, m_pre_mix_g, v_pre_mix_g),
        "post_mix_g": small_update("adam_post_mix", g_post_mix, post_mix_g, m_post_mix_g, v_post_mix_g),
        "pre_ffn_g": small_update("adam_pre_ffn", g_pre_ffn, pre_ffn_g, m_pre_ffn_g, v_pre_ffn_g),
        "post_ffn_g": small_update("adam_post_ffn", g_post_ffn, post_ffn_g, m_post_ffn_g, v_post_ffn_g),
        "a_w_in": single("adam_a_w_in", parts[0], a_w_in, m_a_w_in, v_a_w_in),
        "a_v_norm_g": small_update("adam_a_v_norm", g_vgain, a_v_norm_g, m_a_v_norm_g, v_a_v_norm_g),
        "a_w_spatial": small_update("adam_a_w_spatial", g_ws, a_w_spatial, m_a_w_spatial, v_a_w_spatial),
        "a_b_spatial": small_update("adam_a_b_spatial", g_bs, a_b_spatial, m_a_b_spatial, v_a_b_spatial),
        "a_w_out": single("adam_a_w_out", parts[1], a_w_out, m_a_w_out, v_a_w_out),
        "kv_norm_g": small_update("adam_kv_norm", g_kv, kv_norm_g, m_kv_norm_g, v_kv_norm_g),
        "w_k": single("adam_w_k", parts[2], w_k, m_w_k, v_w_k),
        "w_v": single("adam_w_v", parts[3], w_v, m_w_v, v_w_v),
        "b_w_q": single("adam_b_w_q", parts[4], b_w_q, m_b_w_q, v_b_w_q),
        "b_w_o": single("adam_b_w_o", parts[5], b_w_o, m_b_w_o, v_b_w_o),
        "ffn_w_up": stacked("adam_ffn_w_up", parts[6], parts[7], ffn_w_up, m_ffn_w_up, v_ffn_w_up),
        "ffn_conv_w": small_update("adam_ffn_conv_w", g_cwf, ffn_conv_w, m_ffn_conv_w, v_ffn_conv_w),
        "ffn_conv_b": small_update("adam_ffn_conv_b", g_cb, ffn_conv_b, m_ffn_conv_b, v_ffn_conv_b),
        "ffn_w_down": stacked("adam_ffn_w_down", parts[8], parts[9], ffn_w_down, m_ffn_w_down, v_ffn_w_down),
    }
    order = ["pre_mix_g", "post_mix_g", "pre_ffn_g", "post_ffn_g", "a_w_in", "a_v_norm_g", "a_w_spatial", "a_b_spatial",
             "a_w_out", "kv_norm_g", "w_k", "w_v", "b_w_q", "b_w_o", "ffn_w_up", "ffn_conv_w", "ffn_conv_b", "ffn_w_down"]
    outs = [loss, grad_x.reshape(x.shape)]
    for idx in range(4):
        outs += [results[n][idx] for n in order]
    return tuple(outs)
```

```python
import functools
import math
from typing import NamedTuple, Optional

import jax
import jax.numpy as jnp
from jax import lax
from jax.experimental import pallas as pl
from jax.experimental.pallas import tpu as pltpu

F32 = jnp.float32
BF = jnp.bfloat16
MESH = pl.DeviceIdType.MESH

N_DEV = 8
NORM_EPS = 1e-6
GROUP = 128
CONV_TAPS = 3
ADAM_LR, ADAM_B1, ADAM_B2, ADAM_EPS, ADAM_WD, ADAM_STEP = 0.001, 0.9, 0.999, 1e-08, 0.01, 10
EXP_FLOOR = -104.0

V7X_LANES = 128
V7X_VMEM_BYTES = 64 * 1024 * 1024
_MIB = 1024 * 1024

_NN = (((1,), (0,)), ((), ()))
_NT = (((1,), (1,)), ((), ()))
_TN = (((0,), (0,)), ((), ()))


def _tile(n, pref):
    if n <= pref:
        return n
    t = (pref // V7X_LANES) * V7X_LANES
    while t > V7X_LANES and n % t:
        t -= V7X_LANES
    assert n % t == 0, (n, pref)
    return t


def _nbytes(shape, dtype):
    return math.prod(shape) * jnp.dtype(dtype).itemsize


def _params(sem=None, vmem=None):
    kw = {}
    if sem is not None:
        kw["dimension_semantics"] = sem
    if vmem is not None:
        kw["vmem_limit_bytes"] = int(min(max(vmem, 16 * _MIB), V7X_VMEM_BYTES - 8 * _MIB))
    return pltpu.CompilerParams(**kw)


def _place():
    x, y, c = lax.axis_index("x"), lax.axis_index("y"), lax.axis_index("c")
    return x, y, c, 4 * x + 2 * y + c


def _flip(x, y, c, k):
    return (1 - x if k & 4 else x, 1 - y if k & 2 else y, 1 - c if k & 1 else c)


class _Carry(NamedTuple):
    gather: bool
    src: jax.Array
    dst: Optional[jax.Array]
    lo: int
    hi: int


def _gather(src, lo=0, hi=None, dst=None):
    return _Carry(True, src, dst, lo, src.shape[0] if hi is None else hi)


def _scatter(src, lo=0, hi=None, dst=None):
    return _Carry(False, src, dst, lo, src.shape[1] if hi is None else hi)


def _carry_phases(carries, srcs, dsts, send_sems, recv_sems, local_sems):
    x, y, c, me = _place()
    here, sibling = (x, y, c), (x, y, 1 - c)
    chips = [(1 - x, y), (x, 1 - y), (1 - x, 1 - y)]

    def rows(u):
        return pl.ds(carries[u].lo, carries[u].hi - carries[u].lo)

    def block_copy(u, sem, block, to, from_src=False):
        slot = dsts[u].at[4 * block[0] + 2 * block[1] + block[2], rows(u)]
        return pltpu.make_async_remote_copy(
            src_ref=srcs[u].at[rows(u)] if from_src else slot, dst_ref=slot, send_sem=send_sems.at[u, sem],
            recv_sem=recv_sems.at[u, sem], device_id=to, device_id_type=MESH)

    def partial_copy(u, k):
        peer = _flip(x, y, c, k)
        return pltpu.make_async_remote_copy(
            src_ref=srcs[u].at[4 * peer[0] + 2 * peer[1] + peer[2], rows(u)], dst_ref=dsts[u].at[me, rows(u)],
            send_sem=send_sems.at[u, k - 1], recv_sem=recv_sems.at[u, k - 1], device_id=peer, device_id_type=MESH)

    def local_copy(u):
        src = srcs[u].at[rows(u)] if carries[u].gather else srcs[u].at[me, rows(u)]
        return pltpu.make_async_copy(src, dsts[u].at[me, rows(u)], local_sems.at[u])

    def first():
        for u, cr in enumerate(carries):
            local_copy(u).start()
            if cr.gather:
                block_copy(u, 0, here, sibling, from_src=True).start()
                for j, chip in enumerate(chips):
                    block_copy(u, 1 + j, here, (*chip, c), from_src=True).start()
            else:
                for k in range(1, N_DEV):
                    partial_copy(u, k).start()

    def middle():
        for u, cr in enumerate(carries):
            if cr.gather:
                for j, chip in enumerate(chips):
                    block_copy(u, 1 + j, (*chip, c), here).wait_recv()
                    block_copy(u, 4 + j, (*chip, c), sibling).start()

    def last():
        for u, cr in enumerate(carries):
            if cr.gather:
                block_copy(u, 0, sibling, here).wait_recv()
                for j, chip in enumerate(chips):
                    block_copy(u, 4 + j, (*chip, 1 - c), here).wait_recv()
                block_copy(u, 0, here, sibling, from_src=True).wait_send()
                for j, chip in enumerate(chips):
                    block_copy(u, 1 + j, here, (*chip, c), from_src=True).wait_send()
                    block_copy(u, 4 + j, (*chip, c), sibling).wait_send()
            else:
                for k in range(1, N_DEV):
                    partial_copy(u, k).wait()
            local_copy(u).wait()

    return first, middle, last


def _call(body, *, name, grid, in_specs, out_specs, out_shape, operands, scratch=(), sem=None, vmem=None,
          carries=(), middle_at=0.6):
    if not carries:
        return pl.pallas_call(
            body, name=name, grid=grid, in_specs=in_specs, out_specs=out_specs, out_shape=out_shape,
            scratch_shapes=list(scratch), compiler_params=_params(sem, vmem))(*operands)
    n_in, n_out, n_scr, nc = len(in_specs), len(out_specs), len(scratch), len(carries)
    given = [u for u, cr in enumerate(carries) if cr.dst is not None]
    steps = math.prod(grid)
    middle_step = min(steps - 1, int(steps * middle_at))

    def wrapped(*refs):
        ins, srcs = refs[:n_in], refs[n_in:n_in + nc]
        at = n_in + nc + len(given)
        outs, dsts = refs[at:at + n_out], refs[at + n_out:at + n_out + nc]
        at += n_out + nc
        scr, (send_sems, recv_sems, local_sems) = refs[at:at + n_scr], refs[at + n_scr:]
        first, middle, last = _carry_phases(carries, srcs, dsts, send_sems, recv_sems, local_sems)
        step = 0
        for axis, size in enumerate(grid):
            step = step * size + pl.program_id(axis)
        pl.when(step == 0)(first)
        body(*ins, *outs, *scr)
        pl.when(step == middle_step)(middle)
        pl.when(step == steps - 1)(last)

    any_spec = pl.BlockSpec(memory_space=pl.ANY)
    dst_shapes = [jax.ShapeDtypeStruct((N_DEV, *cr.src.shape) if cr.gather else cr.src.shape, cr.src.dtype) for cr in carries]
    return pl.pallas_call(
        wrapped, name=name, grid=grid, in_specs=list(in_specs) + [any_spec] * (nc + len(given)),
        out_specs=list(out_specs) + [any_spec] * nc, out_shape=list(out_shape) + dst_shapes,
        input_output_aliases={n_in + nc + g: n_out + u for g, u in enumerate(given)},
        scratch_shapes=list(scratch) + [pltpu.SemaphoreType.DMA((nc, 7)), pltpu.SemaphoreType.DMA((nc, 7)),
                                        pltpu.SemaphoreType.DMA((nc,))],
        compiler_params=_params(("arbitrary",) * len(grid), vmem),
    )(*operands, *[cr.src for cr in carries], *[carries[u].dst for u in given])


def _mm(name, a, b, *, dims, grid, a_blk, a_map, b_blk, b_map, o_blk, o_map, out_shape, out_dtype,
        add=None, add_blk=None, add_map=None, carries=(), b_slabs=1):
    nk = grid[2]
    assert add is None or nk == 1
    acc_shape = tuple(d for d in o_blk if d is not None)
    in_place = out_dtype == F32

    def body(*refs):
        if add is None:
            a_ref, b_ref, o_ref = refs[:3]
            c_ref, scr = None, refs[3:]
        else:
            a_ref, b_ref, c_ref, o_ref = refs[:4]
            scr = refs[4:]
        if b_slabs == 1:
            part = lax.dot_general(a_ref[...], b_ref[...], dims, preferred_element_type=F32)
        else:
            cw = b_ref.shape[2]
            part = sum(lax.dot_general(a_ref[:, s * cw:(s + 1) * cw], b_ref[s], dims, preferred_element_type=F32)
                       for s in range(b_slabs))
        if c_ref is not None:
            part = part + c_ref[...].astype(F32)
        if nk == 1:
            o_ref[...] = part.astype(o_ref.dtype)
            return
        acc = o_ref if in_place else scr[0]
        k = pl.program_id(2)

        @pl.when(k == 0)
        def _():
            acc[...] = part

        @pl.when(k > 0 if in_place else jnp.logical_and(k > 0, k < nk - 1))
        def _():
            acc[...] += part

        if not in_place:
            @pl.when(k == nk - 1)
            def _():
                o_ref[...] = (acc[...] + part).astype(o_ref.dtype)

    in_specs = [pl.BlockSpec(a_blk, a_map), pl.BlockSpec(b_blk, b_map)]
    operands = [a, b]
    scratch = [pltpu.VMEM(acc_shape, F32)] if nk > 1 and not in_place else []
    vmem = 2 * (_nbytes(acc_shape, out_dtype) + _nbytes([d for d in a_blk if d], a.dtype)
                + _nbytes([d for d in b_blk if d], b.dtype)) + (2 + len(scratch)) * _nbytes(acc_shape, F32)
    if add is not None:
        in_specs.append(pl.BlockSpec(add_blk, add_map))
        operands.append(add)
        vmem += 2 * _nbytes(acc_shape, add.dtype)
    out, *dsts = _call(
        body, name=name, grid=grid, in_specs=in_specs, out_specs=[pl.BlockSpec(o_blk, o_map)],
        out_shape=[jax.ShapeDtypeStruct(out_shape, out_dtype)], operands=operands, scratch=scratch,
        sem=("parallel", "parallel", "arbitrary"), vmem=vmem + 8 * _MIB, carries=carries)
    return (out, dsts) if carries else out


def _mm_nn(name, x, w, out_dtype, carries=()):
    t, kd = x.shape
    n = w.shape[1]
    tm, tn, tk = _tile(t, 1024), _tile(n, 2048 if kd > 2048 else 1024), _tile(kd, 1536 if kd > 2048 else 2048)
    return _mm(name, x, w, dims=_NN, grid=(t // tm, n // tn, kd // tk),
               a_blk=(tm, tk), a_map=lambda i, j, k: (i, k), b_blk=(tk, tn), b_map=lambda i, j, k: (k, j),
               o_blk=(tm, tn), o_map=lambda i, j, k: (i, j), out_shape=(t, n), out_dtype=out_dtype, carries=carries)


def _mm_nn_blk(name, x, g, out_dtype, halves=False, carries=()):
    t, kd = x.shape
    cw = g.shape[2]
    tm = _tile(t, 1024)
    if halves:
        o_blk, o_map, out_shape = (None, tm, cw), (lambda i, j, k: (j // 4, i, j % 4)), (2, t, 4 * cw)
    else:
        o_blk, o_map, out_shape = (tm, cw), (lambda i, j, k: (i, j)), (t, N_DEV * cw)
    return _mm(name, x, g, dims=_NN, grid=(t // tm, N_DEV, 1),
               a_blk=(tm, kd), a_map=lambda i, j, k: (i, 0), b_blk=(None, kd, cw), b_map=lambda i, j, k: (j, 0, 0),
               o_blk=o_blk, o_map=o_map, out_shape=out_shape, out_dtype=out_dtype, carries=carries)


def _mm_nt(name, dy, w, out_dtype, add=None, carries=()):
    t, n = dy.shape
    kd = w.shape[0]
    tm, tn = _tile(t, 1024), _tile(kd, 1408)
    kw = {}
    if add is not None:
        kw = dict(add=add, add_blk=(tm, tn), add_map=lambda i, j, k: (i, j))
    return _mm(name, dy, w, dims=_NT, grid=(t // tm, kd // tn, 1),
               a_blk=(tm, n), a_map=lambda i, j, k: (i, 0), b_blk=(tn, n), b_map=lambda i, j, k: (j, 0),
               o_blk=(tm, tn), o_map=lambda i, j, k: (i, j), out_shape=(t, kd), out_dtype=out_dtype, carries=carries, **kw)


def _mm_nt_blk(name, dy, g, out_dtype, halves=False, carries=()):
    kd, cw = g.shape[1], g.shape[2]
    t = dy.shape[1] if halves else dy.shape[0]
    tm = _tile(t, 512)
    slabs = 2 if cw > 512 else 4
    nk = N_DEV // slabs
    if halves:
        a_blk, a_map = (None, tm, slabs * cw), (lambda i, j, k: (k // (nk // 2), i, k % (nk // 2)))
    else:
        a_blk, a_map = (tm, slabs * cw), (lambda i, j, k: (i, k))
    return _mm(name, dy, g, dims=_NT, grid=(t // tm, 1, nk), b_slabs=slabs,
               a_blk=a_blk, a_map=a_map, b_blk=(slabs, kd, cw), b_map=lambda i, j, k: (k, 0, 0),
               o_blk=(tm, kd), o_map=lambda i, j, k: (i, 0), out_shape=(t, kd), out_dtype=out_dtype, carries=carries)


def _mm_tn(name, x, dy, carries=()):
    t, kd = x.shape
    n = dy.shape[1]
    tmx, tk = _tile(kd, 1024), _tile(t, 2048)
    return _mm(name, x, dy, dims=_TN, grid=(kd // tmx, 1, t // tk),
               a_blk=(tk, tmx), a_map=lambda i, j, k: (k, i), b_blk=(tk, n), b_map=lambda i, j, k: (k, 0),
               o_blk=(tmx, n), o_map=lambda i, j, k: (i, 0), out_shape=(kd, n), out_dtype=BF, carries=carries)


def _mm_tn_blk(name, x, dy, halves=False, carries=()):
    t, kd = x.shape
    cw = dy.shape[2] // 4 if halves else dy.shape[1] // N_DEV
    tmx, tk = _tile(kd, 2048 if cw <= 512 else 1024), _tile(t, 2048)
    if halves:
        b_blk, b_map = (None, tk, cw), (lambda i, j, k: (j // 4, k, j % 4))
    else:
        b_blk, b_map = (tk, cw), (lambda i, j, k: (k, j))
    return _mm(name, x, dy, dims=_TN, grid=(kd // tmx, N_DEV, t // tk),
               a_blk=(tk, tmx), a_map=lambda i, j, k: (k, i), b_blk=b_blk, b_map=b_map,
               o_blk=(None, tmx, cw), o_map=lambda i, j, k: (j, i, 0), out_shape=(N_DEV, kd, cw), out_dtype=BF,
               carries=carries)


def _rms(x, g):
    r = lax.rsqrt(jnp.mean(x * x, axis=-1, keepdims=True) + NORM_EPS)
    return x * r * g


def _rms_bwd_math(x, g, dy):
    d = x.shape[-1]
    r = lax.rsqrt(jnp.mean(x * x, axis=-1, keepdims=True) + NORM_EPS)
    xh = x * r
    u = dy * g
    dx = r * u - xh * (jnp.sum(xh * u, axis=-1, keepdims=True) * (r / d))
    return dx, jnp.sum(dy * xh, axis=0, keepdims=True)


def _row_specs(tr, d, n):
    return [pl.BlockSpec((tr, d), lambda i: (i, 0)) for _ in range(n)]


def _vec_specs(d, n):
    return [pl.BlockSpec((1, d), lambda i: (0, 0)) for _ in range(n)]


def _norms_fwd(name, h, gains, carries=()):
    t, d = h.shape
    tr, ng = _tile(t, 256), len(gains)

    def body(h_ref, *refs):
        x = h_ref[...]
        for g_ref, o_ref in zip(refs[:ng], refs[ng:]):
            o_ref[...] = _rms(x, g_ref[...]).astype(BF)

    return _call(
        body, name=name, grid=(t // tr,), in_specs=_row_specs(tr, d, 1) + _vec_specs(d, ng),
        out_specs=_row_specs(tr, d, ng), out_shape=[jax.ShapeDtypeStruct((t, d), BF)] * ng,
        operands=(h, *[g.reshape(1, d) for g in gains]), sem=("parallel",), carries=carries)


def _resid_norms(name, h, m, g_post, gains, carries=()):
    t, d = h.shape
    tr, ng = _tile(t, 256), len(gains)

    def body(h_ref, m_ref, gp_ref, *refs):
        hn = h_ref[...] + _rms(m_ref[...], gp_ref[...])
        refs[ng][...] = hn
        for g_ref, o_ref in zip(refs[:ng], refs[ng + 1:]):
            o_ref[...] = _rms(hn, g_ref[...]).astype(BF)

    return _call(
        body, name=name, grid=(t // tr,), in_specs=_row_specs(tr, d, 2) + _vec_specs(d, 1 + ng),
        out_specs=_row_specs(tr, d, 1 + ng),
        out_shape=[jax.ShapeDtypeStruct((t, d), F32)] + [jax.ShapeDtypeStruct((t, d), BF)] * ng,
        operands=(h, m, g_post.reshape(1, d), *[g.reshape(1, d) for g in gains]), sem=("parallel",), carries=carries)


def _resid_loss(name, h, m, g_post, target):
    t, d = h.shape
    tr = _tile(t, 256)

    def body(h_ref, m_ref, gp_ref, t_ref, dy_ref, loss_ref, dm_ref, dg_ref):
        mv = m_ref[...]
        diff = h_ref[...] + _rms(mv, gp_ref[...]) - t_ref[...]
        dy = diff * (1.0 / d)
        dy_ref[...] = dy
        dm, dg = _rms_bwd_math(mv, gp_ref[...], dy)
        dm_ref[...] = dm.astype(BF)

        @pl.when(pl.program_id(0) == 0)
        def _():
            loss_ref[...] = jnp.zeros_like(loss_ref)
            dg_ref[...] = jnp.zeros_like(dg_ref)

        per_row = jnp.sum(diff * diff, axis=-1, keepdims=True) * (1.0 / d)
        loss_ref[...] += 0.5 * jnp.sum(per_row, axis=0, keepdims=True)
        dg_ref[...] += dg

    row = pl.BlockSpec((tr, d), lambda i: (i, 0))
    return pl.pallas_call(
        body, name=name, grid=(t // tr,),
        in_specs=_row_specs(tr, d, 2) + _vec_specs(d, 1) + _row_specs(tr, d, 1),
        out_specs=[row, pl.BlockSpec((1, 1), lambda i: (0, 0)), row, pl.BlockSpec((1, d), lambda i: (0, 0))],
        out_shape=[jax.ShapeDtypeStruct((t, d), F32), jax.ShapeDtypeStruct((1, 1), F32),
                   jax.ShapeDtypeStruct((t, d), BF), jax.ShapeDtypeStruct((1, d), F32)],
        compiler_params=_params(("arbitrary",)),
    )(h, m, g_post.reshape(1, d), target)


def _norm_bwd(name, x, g, dy, res=None, out_dtype=F32, then=None, carries=()):
    t, d = x.shape
    tr = _tile(t, 256)
    has_res, has_then = res is not None, then is not None
    gains, dys = (g, dy) if isinstance(g, (list, tuple)) else ([g], [dy])
    n = len(gains)

    def accumulate(dg_ref, dg):
        @pl.when(pl.program_id(0) == 0)
        def _():
            dg_ref[...] = jnp.zeros_like(dg_ref)

        dg_ref[...] += dg

    def body(x_ref, *refs):
        dy_refs, g_refs, rest = refs[:n], refs[n:2 * n], list(refs[2 * n:])
        dx = rest.pop(0)[...] if has_res else 0.0
        then_refs = (rest.pop(0), rest.pop(0)) if has_then else None
        dx_ref, dg_refs = rest[0], rest[1:1 + n]
        xv = x_ref[...].astype(F32)
        for dy_ref, g_ref, dg_ref in zip(dy_refs, g_refs, dg_refs):
            dx_one, dg = _rms_bwd_math(xv, g_ref[...], dy_ref[...].astype(F32))
            dx = dx + dx_one
            accumulate(dg_ref, dg)
        dx_ref[...] = dx.astype(dx_ref.dtype)
        if has_then:
            d2, dg2 = _rms_bwd_math(then_refs[0][...], then_refs[1][...], dx)
            rest[1 + n][...] = d2.astype(BF)
            accumulate(rest[2 + n], dg2)

    ops = [x, *dys, *[gain.reshape(1, d) for gain in gains]] + ([res] if has_res else [])
    in_specs = _row_specs(tr, d, 1 + n) + _vec_specs(d, n) + _row_specs(tr, d, int(has_res))
    out_specs = [pl.BlockSpec((tr, d), lambda i: (i, 0))] + [pl.BlockSpec((1, d), lambda i: (0, 0))] * n
    out_shape = [jax.ShapeDtypeStruct((t, d), out_dtype)] + [jax.ShapeDtypeStruct((1, d), F32)] * n
    if has_then:
        ops += [then[0], then[1].reshape(1, d)]
        in_specs += _row_specs(tr, d, 1) + _vec_specs(d, 1)
        out_specs += [pl.BlockSpec((tr, d), lambda i: (i, 0)), pl.BlockSpec((1, d), lambda i: (0, 0))]
        out_shape += [jax.ShapeDtypeStruct((t, d), BF), jax.ShapeDtypeStruct((1, d), F32)]
    return _call(body, name=name, grid=(t // tr,), in_specs=in_specs, out_specs=out_specs, out_shape=out_shape,
                 operands=ops, sem=("arbitrary",), carries=carries)


_GELU_C = math.sqrt(2.0 / math.pi)
_GELU_A = 0.044715


def _gelu(x):
    return 0.5 * x * (1.0 + jnp.tanh(_GELU_C * (x + _GELU_A * x * x * x)))


def _gelu_and_grad(x):
    th = jnp.tanh(_GELU_C * (x + _GELU_A * x * x * x))
    grad = 0.5 * (1.0 + th) + 0.5 * x * (1.0 - th * th) * (_GELU_C * (1.0 + 3.0 * _GELU_A * x * x))
    return 0.5 * x * (1.0 + th), grad


def _causal(n):
    return lax.broadcasted_iota(jnp.int32, (n, n), 1) <= lax.broadcasted_iota(jnp.int32, (n, n), 0)


def _sgu_fwd(name, p, v_gain, w_s, b_st, carries=()):
    t, da2 = p.shape
    da = da2 // 2
    ng = da // GROUP

    def body(p_ref, vg_ref, ws_ref, bst_ref, o_ref):
        keep = _causal(GROUP)
        for g in range(ng):
            lo = g * GROUP
            u = _gelu(p_ref[:, lo:lo + GROUP])
            vn = _rms(_gelu(p_ref[:, da + lo:da + lo + GROUP]), vg_ref[:, lo:lo + GROUP])
            w = jnp.where(keep, ws_ref[g], 0.0).astype(BF)
            mixed = jnp.dot(w, vn.astype(BF), preferred_element_type=F32) + bst_ref[:, g:g + 1]
            o_ref[:, lo:lo + GROUP] = (u * mixed).astype(BF)

    return _call(
        body, name=name, grid=(t // GROUP,),
        in_specs=[pl.BlockSpec((GROUP, da2), lambda i: (i, 0)), pl.BlockSpec((1, da), lambda i: (0, 0)),
                  pl.BlockSpec((ng, GROUP, GROUP), lambda i: (0, 0, 0)), pl.BlockSpec((GROUP, ng), lambda i: (0, 0))],
        out_specs=[pl.BlockSpec((GROUP, da), lambda i: (i, 0))], out_shape=[jax.ShapeDtypeStruct((t, da), BF)],
        operands=(p, v_gain, w_s, b_st), sem=("parallel",), carries=carries)


def _sgu_bwd(name, p, v_gain, w_s, w_st, b_st, dout, carries=()):
    t, da2 = p.shape
    da = da2 // 2
    ng = da // GROUP

    def body(p_ref, vg_ref, ws_ref, wst_ref, bst_ref, do_ref, dp_ref, dvg_ref, dws_ref, dbst_ref):
        @pl.when(pl.program_id(0) == 0)
        def _():
            dvg_ref[...] = jnp.zeros_like(dvg_ref)
            dws_ref[...] = jnp.zeros_like(dws_ref)
            dbst_ref[...] = jnp.zeros_like(dbst_ref)

        keep = _causal(GROUP)
        keep_t = lax.broadcasted_iota(jnp.int32, (GROUP, GROUP), 0) <= lax.broadcasted_iota(jnp.int32, (GROUP, GROUP), 1)
        for g in range(ng):
            lo = g * GROUP
            u, du = _gelu_and_grad(p_ref[:, lo:lo + GROUP])
            v, dv_act = _gelu_and_grad(p_ref[:, da + lo:da + lo + GROUP])
            gain = vg_ref[:, lo:lo + GROUP]
            r = lax.rsqrt(jnp.mean(v * v, axis=-1, keepdims=True) + NORM_EPS)
            vh = v * r
            vnb = (vh * gain).astype(BF)
            w = jnp.where(keep, ws_ref[g], 0.0).astype(BF)
            wt = jnp.where(keep_t, wst_ref[g], 0.0).astype(BF)
            mixed = jnp.dot(w, vnb, preferred_element_type=F32) + bst_ref[:, g:g + 1]
            dout_g = do_ref[:, lo:lo + GROUP].astype(F32)
            dmixed = dout_g * u
            dmb = dmixed.astype(BF)
            dbst_ref[:, g:g + 1] += jnp.sum(dmixed, axis=1, keepdims=True)
            dws_ref[g] += jnp.where(keep, lax.dot_general(dmb, vnb, _NT, preferred_element_type=F32), 0.0)
            dvn = jnp.dot(wt, dmb, preferred_element_type=F32)
            dvg_ref[:, lo:lo + GROUP] += jnp.sum(dvn * vh, axis=0, keepdims=True)
            dvh = dvn * gain
            dv = r * dvh - vh * (jnp.sum(vh * dvh, axis=-1, keepdims=True) * (r / GROUP))
            dp_ref[:, lo:lo + GROUP] = (dout_g * mixed * du).astype(BF)
            dp_ref[:, da + lo:da + lo + GROUP] = (dv * dv_act).astype(BF)

    full = lambda *shape: pl.BlockSpec(shape, lambda i: (0,) * len(shape))
    outs = _call(
        body, name=name, grid=(t // GROUP,),
        in_specs=[pl.BlockSpec((GROUP, da2), lambda i: (i, 0)), full(1, da), full(ng, GROUP, GROUP), full(ng, GROUP, GROUP),
                  full(GROUP, ng), pl.BlockSpec((GROUP, da), lambda i: (i, 0))],
        out_specs=[pl.BlockSpec((GROUP, da2), lambda i: (i, 0)), full(1, da), full(ng, GROUP, GROUP), full(GROUP, ng)],
        out_shape=[jax.ShapeDtypeStruct((t, da2), BF), jax.ShapeDtypeStruct((1, da), F32),
                   jax.ShapeDtypeStruct((ng, GROUP, GROUP), F32), jax.ShapeDtypeStruct((GROUP, ng), F32)],
        operands=(p, v_gain, w_s, w_st, b_st, dout), sem=("arbitrary",), carries=carries)
    return outs[:4], outs[4:]


_CONV_ROWS = 256
_CONV_COLS = 128


def _shift_down(x, prev, k):
    top = pltpu.roll(jnp.concatenate([prev, x[:8]], axis=0), k, 0)[8:16]
    return jnp.concatenate([top, pltpu.roll(x, k, 0)[8:]], axis=0)


def _conv_taps(a_ref, half, r0, first):
    rows = _rows(a_ref)
    x = a_ref[half, pl.ds(r0, rows), :]
    if first:
        prev = jnp.zeros((8, x.shape[1]), F32)
        return x, _shift_down(x, prev, 1), _shift_down(x, prev, 2)
    return x, a_ref[half, pl.ds(r0 - 1, rows), :], a_ref[half, pl.ds(r0 - 2, rows), :]


def _rows(a_ref):
    return min(_CONV_ROWS, a_ref.shape[1])


def _conv_fwd(name, a3, cw, cb, carries=()):
    _, t, f = a3.shape
    tc, rows = _CONV_COLS, min(_CONV_ROWS, t)

    def body(a_ref, cw_ref, cb_ref, y_ref):
        def chunk(r0, first):
            c = []
            for half in range(2):
                x, x1, x2 = _conv_taps(a_ref, half, r0, first)
                w = cw_ref[half]
                c.append(cb_ref[half] + (w[0:1] * x2 + w[1:2] * x1 + w[2:3] * x))
            y_ref[pl.ds(r0, rows), :] = (c[0] * jax.nn.sigmoid(c[0]) * c[1]).astype(BF)

        chunk(0, True)

        @pl.loop(1, t // rows)
        def _(r):
            chunk(pl.multiple_of(r * rows, rows), False)

    col = lambda *lead: pl.BlockSpec((*lead, tc), lambda j: (0,) * len(lead) + (j,))
    y, *dsts = _call(
        body, name=name, grid=(f // tc,), in_specs=[col(2, t), col(2, CONV_TAPS), col(2, 1)],
        out_specs=[col(t)], out_shape=[jax.ShapeDtypeStruct((t, f), BF)], operands=(a3, cw, cb),
        sem=("parallel",), vmem=40 * _MIB, carries=carries)
    return y, dsts


def _conv_bwd(name, a3, cw, cb, dy, carries=()):
    _, t, f = a3.shape
    tc, rows = _CONV_COLS, min(_CONV_ROWS, t)
    n_steps = t // rows

    def body(a_ref, cw_ref, cb_ref, dy_ref, da_ref, dcw_ref, dcb_ref, dc_ref):
        dcw_ref[...] = jnp.zeros_like(dcw_ref)
        dcb_ref[...] = jnp.zeros_like(dcb_ref)

        def chunk(r0, first, nxt):
            taps, c = [], []
            for half in range(2):
                x, x1, x2 = _conv_taps(a_ref, half, r0, first)
                w = cw_ref[half]
                taps.append((x2, x1, x))
                c.append(cb_ref[half] + (w[0:1] * x2 + w[1:2] * x1 + w[2:3] * x))
            gate, val = c
            sg = jax.nn.sigmoid(gate)
            dyv = dy_ref[pl.ds(r0, rows), :].astype(F32)
            dcs = (dyv * val * (sg * (1.0 + gate * (1.0 - sg))), dyv * (gate * sg))
            new_nxt = []
            for half in range(2):
                dc, w = dcs[half], cw_ref[half]
                dcb_ref[half] += jnp.sum(dc, axis=0, keepdims=True)
                for tap in range(CONV_TAPS):
                    dcw_ref[half, tap:tap + 1, :] += jnp.sum(dc * taps[half][tap], axis=0, keepdims=True)
                dc_ref[half, 0:rows, :] = dc
                dc_ref[half, rows:rows + 8, :] = nxt[half]
                da = w[2:3] * dc + w[1:2] * dc_ref[half, 1:rows + 1, :] + w[0:1] * dc_ref[half, 2:rows + 2, :]
                da_ref[half, pl.ds(r0, rows), :] = da.astype(BF)
                new_nxt.append(dc[:8])
            return tuple(new_nxt)

        zeros = jnp.zeros((8, tc), F32)
        nxt = lax.fori_loop(0, n_steps - 1, lambda s, nxt: chunk(pl.multiple_of((n_steps - 1 - s) * rows, rows), False, nxt),
                            (zeros, zeros))
        chunk(0, True, nxt)

    col = lambda *lead: pl.BlockSpec((*lead, tc), lambda j: (0,) * len(lead) + (j,))
    outs = _call(
        body, name=name, grid=(f // tc,), in_specs=[col(2, t), col(2, CONV_TAPS), col(2, 1), col(t)],
        out_specs=[col(2, t), col(2, CONV_TAPS), col(2, 1)],
        out_shape=[jax.ShapeDtypeStruct((2, t, f), BF), jax.ShapeDtypeStruct((2, CONV_TAPS, f), F32),
                   jax.ShapeDtypeStruct((2, 1, f), F32)],
        operands=(a3, cw, cb, dy), scratch=[pltpu.VMEM((2, rows + 8, tc), F32)], sem=("parallel",), vmem=48 * _MIB,
        carries=carries)
    return outs[:3], outs[3:]


_ATT_BLOCK = 256


def _split_dot(x, tri):
    hi = x.astype(BF)
    lo = (x - hi.astype(F32)).astype(BF)
    return jnp.dot(hi, tri, preferred_element_type=F32) + jnp.dot(lo, tri, preferred_element_type=F32)


_ATT_HEADS_FWD = 8
_ATT_HEADS_BWD = 4
_ATT_STRIPS_BWD = 2
_ATT_UNITS_TOGETHER = 8


def _logits(qb, kb, diagonal, row0=0):
    z = lax.dot_general(qb, kb, _NT, preferred_element_type=F32) * (1.0 / math.sqrt(GROUP))
    lb = jnp.minimum(z, 0.0) - jnp.log(1.0 + jnp.exp(-jnp.abs(z)))
    if not diagonal:
        return lb, lb - z, None
    mask = lax.broadcasted_iota(jnp.int32, z.shape, 1) < row0 + lax.broadcasted_iota(jnp.int32, z.shape, 0)
    return lb, jnp.where(mask, lb - z, 0.0), mask


def _head(ref, g, rows=slice(None)):
    return ref[rows, g * GROUP:(g + 1) * GROUP]


def _attn_fwd(name, q, k, v, carries=()):
    t, hd = q.shape
    blk = min(_ATT_BLOCK, t)
    heads = min(_ATT_HEADS_FWD, hd // GROUP)
    gs = range(heads)

    def body(q_ref, k_ref, v_ref, o_ref, rest_ref, first_ref):
        hg, i = pl.program_id(0), pl.program_id(1)
        ri = lax.broadcasted_iota(jnp.int32, (blk, blk), 0)
        ci = lax.broadcasted_iota(jnp.int32, (blk, blk), 1)
        tri = (ri >= ci).astype(BF)

        def tile(j, state, diagonal):
            keys = pl.ds(pl.multiple_of(j * blk, blk), blk)
            logit = [_logits(_head(q_ref, g), _head(k_ref, g, keys), diagonal) for g in gs]
            incl = [_split_dot(logit[g][1], tri) for g in gs]
            a = [jnp.exp(logit[g][0] + (incl[g] - logit[g][1] + state[g][0])) for g in gs]
            if diagonal:
                a = [jnp.where(logit[g][2], a[g], 0.0) for g in gs]
            out = [state[g][1] + jnp.dot(a[g].astype(BF), _head(v_ref, g, keys), preferred_element_type=F32) for g in gs]
            return tuple((state[g][0] + incl[g][:, 0:1], out[g]) for g in gs)

        state = tile(i, ((jnp.zeros((blk, 1), F32), jnp.zeros((blk, GROUP), F32)),) * heads, True)

        def more(carry):
            j, state = carry
            live = functools.reduce(jnp.maximum, [jnp.max(right) for right, _ in state])
            return jnp.logical_and(j >= 0, live > EXP_FLOOR)

        j, state = lax.while_loop(more, lambda c: (c[0] - 1, tile(c[0], c[1], False)), (i - 1, state))
        for g, (right, acc) in enumerate(state):
            o_ref[:, g * GROUP:(g + 1) * GROUP] = acc.astype(BF)
            rest_ref[:, g * GROUP:(g + 1) * GROUP] = jnp.broadcast_to(right, (blk, GROUP))
        first_ref[hg, i] = (j + 1).astype(F32)

    qspec = pl.BlockSpec((blk, heads * GROUP), lambda h, i: (i, h))
    kvspec = pl.BlockSpec((t, heads * GROUP), lambda h, i: (0, h), pipeline_mode=pl.Buffered(1))
    groups = hd // (heads * GROUP)
    outs = _call(
        body, name=name, grid=(groups, t // blk), in_specs=[qspec, kvspec, kvspec],
        out_specs=[qspec, qspec, pl.BlockSpec(memory_space=pltpu.SMEM)],
        out_shape=[jax.ShapeDtypeStruct((t, hd), BF), jax.ShapeDtypeStruct((t, hd), F32),
                   jax.ShapeDtypeStruct((groups, t // blk), F32)],
        operands=(q, k, v), sem=("arbitrary", "arbitrary"), vmem=40 * _MIB, carries=carries, middle_at=0.75)
    return outs[:3], outs[3:]


def _attn_bwd(name, q, k, v, rest, first, do, carries=()):
    t, hd = q.shape
    blk = min(_ATT_BLOCK, t)
    nq = t // blk
    scale = 1.0 / math.sqrt(GROUP)
    heads = min(_ATT_HEADS_BWD, hd // GROUP)
    per_first = min(_ATT_HEADS_FWD, hd // GROUP) // heads
    gs = range(heads)
    rows = blk // _ATT_STRIPS_BWD
    units = [(g, slice(s * rows, (s + 1) * rows), s * rows) for g in gs for s in range(_ATT_STRIPS_BWD)]

    def body(first_ref, q_ref, k_ref, v_ref, rest_ref, do_ref, dq_ref, dk_ref, dv_ref, dk_acc, dv_acc):
        hg, i = pl.program_id(0), pl.program_id(1)

        @pl.when(i == 0)
        def _():
            dk_acc[...] = jnp.zeros_like(dk_acc)
            dv_acc[...] = jnp.zeros_like(dv_acc)

        ri = lax.broadcasted_iota(jnp.int32, (blk, blk), 0)
        ci = lax.broadcasted_iota(jnp.int32, (blk, blk), 1)
        tri = (ri <= ci).astype(BF)

        def tile(j, state, diagonal):
            keys = pl.ds(pl.multiple_of(j * blk, blk), blk)
            new = []
            for lo in range(0, len(units), _ATT_UNITS_TOGETHER):
                new += stages(keys, state, diagonal, range(lo, min(lo + _ATT_UNITS_TOGETHER, len(units))))
            return tuple(new)

        def stages(keys, state, diagonal, us):
            qs = {u: _head(q_ref, units[u][0], units[u][1]) for u in us}
            dos = {u: _head(do_ref, units[u][0], units[u][1]) for u in us}
            mine = sorted({units[u][0] for u in us})
            kb, vb = {g: _head(k_ref, g, keys) for g in mine}, {g: _head(v_ref, g, keys) for g in mine}
            logit = {u: _logits(qs[u], kb[units[u][0]], diagonal, units[u][2]) for u in us}
            pre = {u: _split_dot(logit[u][1], tri) for u in us}
            a = {u: jnp.exp(logit[u][0] + (rest_ref[units[u][1], units[u][0] * GROUP:units[u][0] * GROUP + 1]
                                           - state[u][0] - pre[u])) for u in us}
            if diagonal:
                a = {u: jnp.where(logit[u][2], a[u], 0.0) for u in us}
            gw = {u: a[u] * lax.dot_general(dos[u], vb[units[u][0]], _NT, preferred_element_type=F32) for u in us}
            gpre = {u: _split_dot(gw[u], tri) for u in us}
            dz = {}
            for u in us:
                beta = jnp.exp(logit[u][0])
                d = (gw[u] * (1.0 - beta) - (state[u][1] + gpre[u] - gw[u]) * beta) * scale
                dz[u] = (jnp.where(logit[u][2], d, 0.0) if diagonal else d).astype(BF)
            for g in mine:
                its = [u for u in us if units[u][0] == g]
                dk_acc[keys, g * GROUP:(g + 1) * GROUP] += sum(
                    lax.dot_general(dz[u], qs[u], _TN, preferred_element_type=F32) for u in its)
                dv_acc[keys, g * GROUP:(g + 1) * GROUP] += sum(
                    lax.dot_general(a[u].astype(BF), dos[u], _TN, preferred_element_type=F32) for u in its)
            return [(state[u][0] + pre[u][:, blk - 1:blk], state[u][1] + gpre[u][:, blk - 1:blk],
                     state[u][2] + jnp.dot(dz[u], kb[units[u][0]], preferred_element_type=F32)) for u in us]

        zero = jnp.zeros((rows, 1), F32)
        first_block = jnp.clip(first_ref[hg // per_first, i].astype(jnp.int32), 0, i)
        state = lax.fori_loop(first_block, i, lambda j, c: tile(j, c, False),
                              ((zero, zero, jnp.zeros((rows, GROUP), F32)),) * len(units))
        for (g, strip, _), (_, _, dq) in zip(units, tile(i, state, True)):
            dq_ref[strip, g * GROUP:(g + 1) * GROUP] = dq.astype(BF)

        @pl.when(i == nq - 1)
        def _():
            dk_ref[...] = dk_acc[...].astype(BF)
            dv_ref[...] = dv_acc[...].astype(BF)

    qspec = pl.BlockSpec((blk, heads * GROUP), lambda h, i: (i, h))
    kvspec = pl.BlockSpec((t, heads * GROUP), lambda h, i: (0, h), pipeline_mode=pl.Buffered(1))
    outs = _call(
        body, name=name, grid=(hd // (heads * GROUP), nq),
        in_specs=[pl.BlockSpec(memory_space=pltpu.SMEM), qspec, kvspec, kvspec, qspec, qspec],
        out_specs=[qspec, kvspec, kvspec], out_shape=[jax.ShapeDtypeStruct((t, hd), BF)] * 3,
        operands=(first, q, k, v, rest, do),
        scratch=[pltpu.VMEM((t, heads * GROUP), F32), pltpu.VMEM((t, heads * GROUP), F32)],
        sem=("arbitrary", "arbitrary"), vmem=48 * _MIB, carries=carries)
    return outs[:3], outs[3:]


def _adamw_math(w, g, m, v):
    m = ADAM_B1 * m + (1.0 - ADAM_B1) * g
    v = ADAM_B2 * v + (1.0 - ADAM_B2) * (g * g)
    m_hat = m / (1.0 - ADAM_B1 ** ADAM_STEP)
    v_hat = v / (1.0 - ADAM_B2 ** ADAM_STEP)
    return -ADAM_LR * (m_hat / (jnp.sqrt(v_hat) + ADAM_EPS) + ADAM_WD * w), m, v


def _sum_adamw(name, parts, w, m, v):
    layers, r, c = w.shape
    budget = 512 * 1024 // layers
    tr = r if r * c <= budget else _tile_rows(r, max(8, (budget // c) // 8 * 8))
    n = r // tr

    def body(*refs):
        p_refs, (w_ref, m_ref, v_ref, g_ref, d_ref, nm_ref, nv_ref) = refs[:layers], refs[layers:]
        for layer, p_ref in enumerate(p_refs):
            @pl.when(pl.program_id(0) == layer)
            def _(p_ref=p_ref):
                g = p_ref[0].astype(F32)
                for dev in range(1, N_DEV):
                    g = g + p_ref[dev].astype(F32)
                g_ref[...] = g
                d_ref[...], nm_ref[...], nv_ref[...] = _adamw_math(w_ref[...], g, m_ref[...], v_ref[...])

    def part_spec(layer):
        return pl.BlockSpec((N_DEV, tr, c), lambda l, i: (0, jnp.where(l < layer, 0, jnp.where(l == layer, i, n - 1)), 0))

    row = pl.BlockSpec((None, tr, c), lambda l, i: (l, i, 0))
    return pl.pallas_call(
        body, name=name, grid=(layers, n), in_specs=[part_spec(layer) for layer in range(layers)] + [row] * 3,
        out_specs=[row] * 4, out_shape=[jax.ShapeDtypeStruct((layers, r, c), F32)] * 4,
        compiler_params=_params(("arbitrary", "arbitrary"), 40 * _MIB),
    )(*parts, w, m, v)


def _tile_rows(r, pref):
    t = min(r, pref)
    while r % t or t % 8:
        t -= 1
    return t


def _exchange(name, carries):
    return _call(lambda: None, name=name, grid=(1,), in_specs=[], out_specs=[], out_shape=[], operands=(), carries=carries)


def _all_reduce_small(name, groups):
    c = groups[0][0].shape[1]
    parts, offsets, starts, r = [], [], [], 0
    for group in groups:
        starts.append(r)
        for p in group:
            parts.append(p)
            offsets.append(r)
            r += p.shape[0]
        r = -(-r // 8) * 8
    n = len(parts)

    def body(*refs):
        out_ref, slots, send_sems, recv_sems = refs[n:]
        x, y, c_, me = _place()
        slots[me] = jnp.zeros((r, c), F32)
        for p_ref, off in zip(refs[:n], offsets):
            slots[me, off:off + p_ref.shape[0], :] = p_ref[...]

        here, sibling = (x, y, c_), (x, y, 1 - c_)
        chips = [(1 - x, y), (x, 1 - y), (1 - x, 1 - y)]

        def copy(sem, block, to):
            slot = slots.at[4 * block[0] + 2 * block[1] + block[2]]
            return pltpu.make_async_remote_copy(
                src_ref=slot, dst_ref=slot, send_sem=send_sems.at[sem], recv_sem=recv_sems.at[sem], device_id=to,
                device_id_type=MESH)

        sent = [copy(0, here, sibling)] + [copy(1 + j, here, (*chip, c_)) for j, chip in enumerate(chips)]
        for cp in sent:
            cp.start()
        for j, chip in enumerate(chips):
            copy(1 + j, (*chip, c_), here).wait_recv()
            sent.append(copy(4 + j, (*chip, c_), sibling))
            sent[-1].start()
        copy(0, sibling, here).wait_recv()
        for j, chip in enumerate(chips):
            copy(4 + j, (*chip, 1 - c_), here).wait_recv()
        for cp in sent:
            cp.wait_send()
        total = slots[0]
        for dev in range(1, N_DEV):
            total = total + slots[dev]
        out_ref[...] = total

    vm = pl.BlockSpec(memory_space=pltpu.VMEM)
    summed = pl.pallas_call(
        body, name=name, in_specs=[vm] * n, out_specs=vm, out_shape=jax.ShapeDtypeStruct((r, c), F32),
        scratch_shapes=[pltpu.VMEM((N_DEV, r, c), F32), pltpu.SemaphoreType.DMA((7,)), pltpu.SemaphoreType.DMA((7,))],
        compiler_params=_params(None, (N_DEV + 6) * r * c * 4 + 8 * _MIB),
    )(*parts)
    return summed, starts


def _adamw(name, g, w, m, v):
    cols = w.shape[-1]
    flat = lambda a: a.reshape(-1, cols)

    def body(g_ref, w_ref, m_ref, v_ref, d_ref, nm_ref, nv_ref):
        d_ref[...], nm_ref[...], nv_ref[...] = _adamw_math(w_ref[...], g_ref[...], m_ref[...], v_ref[...])

    outs = pl.pallas_call(body, name=name, out_shape=[jax.ShapeDtypeStruct(flat(w).shape, F32)] * 3)(
        flat(g), flat(w), flat(m), flat(v))
    return [o.reshape(w.shape) for o in outs]


def kernel(x, pre_mix_g, post_mix_g, pre_ffn_g, post_ffn_g, a_w_in, a_v_norm_g, a_w_spatial, a_b_spatial, a_w_out, kv_norm_g, w_k, w_v, b_w_q, b_w_o, ffn_w_up, ffn_conv_w, ffn_conv_b, ffn_w_down, loss_target, m_pre_mix_g, m_post_mix_g, m_pre_ffn_g, m_post_ffn_g, m_a_w_in, m_a_v_norm_g, m_a_w_spatial, m_a_b_spatial, m_a_w_out, m_kv_norm_g, m_w_k, m_w_v, m_b_w_q, m_b_w_o, m_ffn_w_up, m_ffn_conv_w, m_ffn_conv_b, m_ffn_w_down, v_pre_mix_g, v_post_mix_g, v_pre_ffn_g, v_post_ffn_g, v_a_w_in, v_a_v_norm_g, v_a_w_spatial, v_a_b_spatial, v_a_w_out, v_kv_norm_g, v_w_k, v_w_v, v_b_w_q, v_b_w_o, v_ffn_w_up, v_ffn_conv_w, v_ffn_conv_b, v_ffn_w_down):
    t, d = x.shape[1], x.shape[2]
    f = ffn_w_down.shape[1] * N_DEV
    ng = d // GROUP
    me = 4 * lax.axis_index("x") + 2 * lax.axis_index("y") + lax.axis_index("c")
    x2, target = x.reshape(t, d), loss_target.reshape(t, d)

    hn0, g_in, g_cw, g_vg = _norms_fwd("pre_mix0", x2, [pre_mix_g[0]], carries=[
        _gather(a_w_in[0].astype(BF)), _gather(ffn_conv_w.reshape(2 * CONV_TAPS, -1)), _gather(a_v_norm_g)])
    up0 = ffn_w_up[0].astype(BF)
    cw_full = jnp.transpose(g_cw.reshape(N_DEV, 2, CONV_TAPS, -1), (1, 2, 0, 3)).reshape(2, CONV_TAPS, 2, f)
    cw_l = [jnp.transpose(cw_full[l], (1, 0, 2)) for l in range(2)]
    cb_l = [ffn_conv_b[l].reshape(2, 1, f) for l in range(2)]
    vg_full = g_vg.reshape(1, d)
    w_s = a_w_spatial[0]
    w_st = jnp.swapaxes(w_s, 1, 2)
    b_st = a_b_spatial[0].T

    p0, (g_out, g_up0) = _mm_nn_blk("sgu_in", hn0, g_in, F32, carries=[
        _gather(a_w_out[0].astype(BF)), _gather(up0, 0, d // 4)])
    w_out_f = g_out.reshape(d, d)
    sg, g_up0 = _sgu_fwd("sgu", p0, vg_full, w_s, b_st, carries=[_gather(up0, d // 4, d // 2, dst=g_up0)])
    mix0, (g_up0,) = _mm_nn("sgu_out", sg, w_out_f, F32, carries=[_gather(up0, d // 2, 3 * d // 4, dst=g_up0)])
    h1, fn0, g_up0 = _resid_norms("post_mix0", x2, mix0, post_mix_g[0], [pre_ffn_g[0]],
                                  carries=[_gather(up0, 3 * d // 4, d, dst=g_up0)])
    a3_0, (g_dn0,) = _mm_nn_blk("ffn0_up", fn0, g_up0, F32, halves=True, carries=[_gather(ffn_w_down[0].astype(BF))])
    y0, (g_q,) = _conv_fwd("ffn0_conv", a3_0, cw_l[0], cb_l[0], carries=[_gather(b_w_q[0].astype(BF))])
    wv = w_v.astype(BF)
    f0, (g_k, g_v) = _mm_nn("ffn0_down", y0, g_dn0.reshape(f, d), F32,
                            carries=[_gather(w_k.astype(BF)), _gather(wv, 0, d // 16)])
    h2, hn1, kvn, g_v = _resid_norms("post_ffn0", h1, f0, post_ffn_g[0], [pre_mix_g[1], kv_norm_g],
                                     carries=[_gather(wv, d // 16, d // 8, dst=g_v)])
    w_q_f, w_k_f, w_v_f = g_q.reshape(d, d), g_k.reshape(d, d), g_v.reshape(d, d)
    up1 = ffn_w_up[1].astype(BF)
    e = d // 8
    q, (g_up1,) = _mm_nn("attn_q", hn1, w_q_f, BF, carries=[_gather(up1, 0, e)])
    kk, (g_up1,) = _mm_nn("attn_k", kvn, w_k_f, BF, carries=[_gather(up1, e, 2 * e, dst=g_up1)])
    vv, (g_up1,) = _mm_nn("attn_v", kvn, w_v_f, BF, carries=[_gather(up1, 2 * e, 3 * e, dst=g_up1)])
    (att, rest, first), (g_o, g_up1) = _attn_fwd(
        "attn", q, kk, vv, carries=[_gather(b_w_o[0].astype(BF)), _gather(up1, 3 * e, 6 * e, dst=g_up1)])
    w_o_f = g_o.reshape(d, d)
    mix1, (g_up1,) = _mm_nn("attn_o", att, w_o_f, F32, carries=[_gather(up1, 6 * e, 7 * e, dst=g_up1)])
    h3, fn1, g_up1 = _resid_norms("post_mix1", h2, mix1, post_mix_g[1], [pre_ffn_g[1]],
                                  carries=[_gather(up1, 7 * e, d, dst=g_up1)])
    a3_1, (g_dn1,) = _mm_nn_blk("ffn1_up", fn1, g_up1, F32, halves=True, carries=[_gather(ffn_w_down[1].astype(BF))])
    y1, _ = _conv_fwd("ffn1_conv", a3_1, cw_l[1], cb_l[1])
    f1 = _mm_nn("ffn1_down", y1, g_dn1.reshape(f, d), F32)
    g_up = (g_up0, g_up1)
    w_dn_f = (g_dn0.reshape(f, d), g_dn1.reshape(f, d))
    dh, loss_part, df1, d_post_ffn1 = _resid_loss("loss", h3, f1, post_ffn_g[1], target)
    loss = lax.psum(loss_part[0, 0], ("x", "y", "c"))

    def blocks(dw):
        return dw.reshape(N_DEV, -1, d)

    def split(result, carries):
        return result if carries else (result, [])

    def ffn_bwd(l, dh_out, dfo, h_in, fn, a3, yv, mix, with_dw=(), with_dx=(), with_up=()):
        dw_dn, sent_dw = split(_mm_tn(f"ffn{l}_down_dw", yv, dfo, carries=with_dw), with_dw)
        dy, sent_dx = split(_mm_nt(f"ffn{l}_down_dx", dfo, w_dn_f[l], BF, carries=with_dx), with_dx)
        dw_dn = blocks(dw_dn)
        cut = dw_dn.shape[1] * 5 // 8 // 16 * 16
        (da3, dcw, dcb), (p_dn,) = _conv_bwd(f"ffn{l}_conv_bwd", a3, cw_l[l], cb_l[l], dy, carries=[_scatter(dw_dn, 0, cut)])
        dw_up, (p_dn, *sent_up) = _mm_tn_blk(f"ffn{l}_up_dw", fn, da3, halves=True,
                                             carries=[_scatter(dw_dn, cut, dst=p_dn), *with_up])
        dfn, (p_up,) = _mm_nt_blk(f"ffn{l}_up_dx", da3, g_up[l], F32, halves=True, carries=[_scatter(dw_up, 0, d // 2)])
        dh_in, d_pre, dmix, d_post_mix, p_up = _norm_bwd(
            f"pre_ffn{l}_bwd", h_in, pre_ffn_g[l], dfn, res=dh_out, then=(mix, post_mix_g[l]),
            carries=[_scatter(dw_up, d // 2, 10 * d // 16, dst=p_up)])
        return dh_in, dmix, d_pre, d_post_mix, dcw, dcb, p_dn, dw_up, p_up, list(sent_dw) + list(sent_dx) + list(sent_up)

    dh3, dmix1, d_pre_ffn1, d_post_mix1, dcw1, dcb1, p_dn1, dw_up1, p_up1, _ = ffn_bwd(
        1, dh, df1, h3, fn1, a3_1, y1, mix1)
    dw_o = _mm_tn("attn_o_dw", att, dmix1)
    datt = _mm_nt("attn_o_dx", dmix1, w_o_f, BF)
    (dq, dk, dv), (p_up1,) = _attn_bwd(
        "attn_bwd", q, kk, vv, rest, first, datt, carries=[_scatter(dw_up1, 10 * d // 16, d, dst=p_up1)])
    dw_o, qr = blocks(dw_o), d // 32
    dw_q, (p_o,) = _mm_tn("attn_q_dw", hn1, dq, carries=[_scatter(dw_o, 0, qr)])
    dw_k, (p_o,) = _mm_tn("attn_k_dw", kvn, dk, carries=[_scatter(dw_o, qr, 2 * qr, dst=p_o)])
    dw_v, (p_o,) = _mm_tn("attn_v_dw", kvn, dv, carries=[_scatter(dw_o, 2 * qr, 3 * qr, dst=p_o)])
    dhn1, (p_o,) = _mm_nt("attn_q_dx", dq, w_q_f, F32, carries=[_scatter(dw_o, 3 * qr, 4 * qr, dst=p_o)])
    dkvn = _mm_nt("attn_v_dx", dv, w_v_f, F32, add=_mm_nt("attn_k_dx", dk, w_k_f, F32))
    dh2, d_pre_mix1, d_kv, df0, d_post_ffn0 = _norm_bwd(
        "pre_mix1_kv_bwd", h2, [pre_mix_g[1], kv_norm_g], [dhn1, dkvn], res=dh3, then=(f0, post_ffn_g[0]))
    dh1, dmix0, d_pre_ffn0, d_post_mix0, dcw0, dcb0, p_dn0, dw_up0, p_up0, (p_q, p_k, p_v) = ffn_bwd(
        0, dh2, df0, h1, fn0, a3_0, y0, mix0, with_dw=[_scatter(blocks(dw_q))], with_dx=[_scatter(blocks(dw_k))],
        with_up=[_scatter(blocks(dw_v))])
    dw_out, (p_up0,) = _mm_tn("sgu_out_dw", sg, dmix0, carries=[_scatter(dw_up0, 10 * d // 16, 11 * d // 16, dst=p_up0)])
    dsg, (p_up0,) = _mm_nt("sgu_out_dx", dmix0, w_out_f, BF, carries=[_scatter(dw_up0, 11 * d // 16, 12 * d // 16, dst=p_up0)])
    (dp0, d_vg, d_ws, d_bst), (p_up0,) = _sgu_bwd(
        "sgu_bwd", p0, vg_full, w_s, w_st, b_st, dsg, carries=[_scatter(dw_up0, 12 * d // 16, d, dst=p_up0)])
    dw_in, (p_out,) = _mm_tn_blk("sgu_in_dw", hn0, dp0, carries=[_scatter(blocks(dw_out))])
    dhn0, (p_in,) = _mm_nt_blk("sgu_in_dx", dp0, g_in, F32, carries=[_scatter(dw_in, 0, 5 * d // 8)])
    grad_x, d_pre_mix0, p_in = _norm_bwd("pre_mix0_bwd", x2, pre_mix_g[0], dhn0, res=dh1,
                                         carries=[_scatter(dw_in, 5 * d // 8, 7 * d // 8, dst=p_in)])
    (p_in,) = _exchange("scatter_last", [_scatter(dw_in, 7 * d // 8, d, dst=p_in)])

    def conv_w_grad(dcw):
        return jnp.transpose(dcw, (1, 0, 2)).reshape(CONV_TAPS, 2 * f)

    small = [
        ([d_pre_mix0, d_pre_mix1], (2, d)), ([d_post_mix0, d_post_mix1], (2, d)),
        ([d_pre_ffn0, d_pre_ffn1], (2, d)), ([d_post_ffn0, d_post_ffn1], (2, d)),
        ([d_kv], (d,)), ([d_vg], (1, d)), ([d_bst.T], (1, ng, GROUP)), ([d_ws], (1, ng, GROUP, GROUP)),
        ([dcb0, dcb1], (2, 2 * f)), ([conv_w_grad(dcw0), conv_w_grad(dcw1)], (2, CONV_TAPS, 2 * f)),
    ]
    width = V7X_LANES * math.gcd(d // V7X_LANES, 2 * f // V7X_LANES)
    summed, offsets = _all_reduce_small("reduce_small", [[a.reshape(-1, width) for a in group] for group, _ in small])
    full = [summed[off:off + math.prod(shape) // width].reshape(shape) for off, (_, shape) in zip(offsets, small)]
    g_pre_mix, g_post_mix, g_pre_ffn, g_post_ffn, g_kv, g_vgain, g_bs, g_ws, g_cb, g_cwf = full
    cw_w = 2 * f // N_DEV
    g_vgain = lax.dynamic_slice_in_dim(g_vgain, me * (d // N_DEV), d // N_DEV, axis=1)
    g_cwf = lax.dynamic_slice_in_dim(g_cwf, me * cw_w, cw_w, axis=2)

    parts = [p_in, p_out, p_k, p_v, p_q, p_o, p_up0, p_up1, p_dn0, p_dn1]

    def small_update(name, g, w, m, v):
        return [g] + _adamw(name, g, w, m, v)

    def stacked(name, part0, part1, w, m, v):
        shape = (2, *part0.shape[1:])
        return [o.reshape(w.shape) for o in _sum_adamw(name, [part0, part1], w.reshape(shape), m.reshape(shape), v.reshape(shape))]

    def single(name, part, w, m, v):
        shape = (1, *part.shape[1:])
        return [o.reshape(w.shape) for o in _sum_adamw(name, [part], w.reshape(shape), m.reshape(shape), v.reshape(shape))]

    results = {
        "pre_mix_g": small_update("adam_pre_mix", g_pre_mix, pre_mix_g, m_pre_mix_g, v_pre_mix_g),
        "post_mix_g": small_update("adam_post_mix", g_post_mix, post_mix_g, m_post_mix_g, v_post_mix_g),
        "pre_ffn_g": small_update("adam_pre_ffn", g_pre_ffn, pre_ffn_g, m_pre_ffn_g, v_pre_ffn_g),
        "post_ffn_g": small_update("adam_post_ffn", g_post_ffn, post_ffn_g, m_post_ffn_g, v_post_ffn_g),
        "a_w_in": single("adam_a_w_in", parts[0], a_w_in, m_a_w_in, v_a_w_in),
        "a_v_norm_g": small_update("adam_a_v_norm", g_vgain, a_v_norm_g, m_a_v_norm_g, v_a_v_norm_g),
        "a_w_spatial": small_update("adam_a_w_spatial", g_ws, a_w_spatial, m_a_w_spatial, v_a_w_spatial),
        "a_b_spatial": small_update("adam_a_b_spatial", g_bs, a_b_spatial, m_a_b_spatial, v_a_b_spatial),
        "a_w_out": single("adam_a_w_out", parts[1], a_w_out, m_a_w_out, v_a_w_out),
        "kv_norm_g": small_update("adam_kv_norm", g_kv, kv_norm_g, m_kv_norm_g, v_kv_norm_g),
        "w_k": single("adam_w_k", parts[2], w_k, m_w_k, v_w_k),
        "w_v": single("adam_w_v", parts[3], w_v, m_w_v, v_w_v),
        "b_w_q": single("adam_b_w_q", parts[4], b_w_q, m_b_w_q, v_b_w_q),
        "b_w_o": single("adam_b_w_o", parts[5], b_w_o, m_b_w_o, v_b_w_o),
        "ffn_w_up": stacked("adam_ffn_w_up", parts[6], parts[7], ffn_w_up, m_ffn_w_up, v_ffn_w_up),
        "ffn_conv_w": small_update("adam_ffn_conv_w", g_cwf, ffn_conv_w, m_ffn_conv_w, v_ffn_conv_w),
        "ffn_conv_b": small_update("adam_ffn_conv_b", g_cb, ffn_conv_b, m_ffn_conv_b, v_ffn_conv_b),
        "ffn_w_down": stacked("adam_ffn_w_down", parts[8], parts[9], ffn_w_down, m_ffn_w_down, v_ffn_w_down),
    }
    order = ["pre_mix_g", "post_mix_g", "pre_ffn_g", "post_ffn_g", "a_w_in", "a_v_norm_g", "a_w_spatial", "a_b_spatial",
             "a_w_out", "kv_norm_g", "w_k", "w_v", "b_w_q", "b_w_o", "ffn_w_up", "ffn_conv_w", "ffn_conv_b", "ffn_w_down"]
    outs = [loss, grad_x.reshape(x.shape)]
    for idx in range(4):
        outs += [results[n][idx] for n in order]
    return tuple(outs)
```
